```python
import math
import jax, jax.numpy as jnp
from jax import lax
import numpy as np

D_MODEL = 1024
BATCH = 4
SEQ = 8192
DEPTH = 2

N_MIXERS = 2
N_HEADS = 16
HEAD_DIM = D_MODEL // N_HEADS
MOBA_BLOCK = 256
MOBA_TOPK = 3
MOBA_QCHUNK = 16
FOX_QBLOCK = 128
REL_BUCKETS = 32
REL_MAX_DIST = 128
N_EXPERTS = 256
TOP_K = 8
N_GROUPS = 8
TOPK_GROUPS = 4
D_EXPERT = 256
D_SHARED = 256
ROUTED_SCALE = 2.5
EXPERT_BLOCK = 128
LN_EPS = 1e-5
DN_ALPHA = (2 * DEPTH) ** 0.25
DN_BETA = (8 * DEPTH) ** -0.25
N_MOBA = (DEPTH + 1) // 2
N_FOX = DEPTH // 2
NEG = -1e30

kernel_name = "hybrid_moba_fox_moe_deepnorm"


def layer_norm(x, g, b):
    xf = x.astype(jnp.float32)
    mu = xf.mean(-1, keepdims=True)
    var = jnp.square(xf - mu).mean(-1, keepdims=True)
    return ((xf - mu) * lax.rsqrt(var + LN_EPS) * g.astype(jnp.float32) + b.astype(jnp.float32)).astype(x.dtype)


def rel_bucket(dist):
    n = jnp.maximum(dist, 0)
    max_exact = REL_BUCKETS // 2
    nf = jnp.maximum(n, 1).astype(jnp.float32)
    large = max_exact + (jnp.log(nf / max_exact) / math.log(REL_MAX_DIST / max_exact)
                         * (REL_BUCKETS - max_exact)).astype(jnp.int32)
    large = jnp.minimum(large, REL_BUCKETS - 1)
    return jnp.where(n < max_exact, n, large)


def split_heads(z):
    B, S, _ = z.shape
    return z.reshape(B, S, N_HEADS, HEAD_DIM).transpose(0, 2, 1, 3)


def merge_heads(o):
    B, H, S, dh = o.shape
    return o.transpose(0, 2, 1, 3).reshape(B, S, H * dh)


def moba_attention(q, k, v, rel_bias):
    B, H, S, dh = q.shape
    nb = -(-S // MOBA_BLOCK)
    pad = nb * MOBA_BLOCK - S
    kp = jnp.pad(k, ((0, 0), (0, 0), (0, pad), (0, 0))).reshape(B, H, nb, MOBA_BLOCK, dh)
    vp = jnp.pad(v, ((0, 0), (0, 0), (0, pad), (0, 0))).reshape(B, H, nb, MOBA_BLOCK, dh)
    k_mean = kp.astype(jnp.float32).mean(axis=3)
    n_sel = min(MOBA_TOPK, nb)
    scale = dh ** -0.5
    bias_t = rel_bias.T.astype(jnp.float32)
    bi = jnp.arange(B)[:, None, None, None]
    hi = jnp.arange(H)[None, :, None, None]
    offs = jnp.arange(MOBA_BLOCK)
    blk_ids = jnp.arange(nb)

    def chunk(c):
        t0 = c * MOBA_QCHUNK
        qc = lax.dynamic_slice_in_dim(q, t0, MOBA_QCHUNK, axis=2)
        pos = t0 + jnp.arange(MOBA_QCHUNK)
        own = t0 // MOBA_BLOCK
        bscore = jnp.einsum('bhqd,bhnd->bhqn', qc.astype(jnp.float32), k_mean)
        bscore = jnp.where(blk_ids < own, bscore, NEG)
        _, sel = lax.top_k(bscore, n_sel)
        sel_valid = sel < own
        ks = kp[bi, hi, sel]
        vs = vp[bi, hi, sel]
        s_sel = jnp.einsum('bhqd,bhqnkd->bhqnk', qc, ks, preferred_element_type=jnp.float32) * scale
        key_pos = sel[..., None] * MOBA_BLOCK + offs
        b_sel = bias_t[hi[..., None], rel_bucket(pos[:, None, None] - key_pos)]
        s_sel = jnp.where(sel_valid[..., None], s_sel + b_sel, NEG)
        s_sel = s_sel.reshape(B, H, MOBA_QCHUNK, n_sel * MOBA_BLOCK)
        k_own = lax.dynamic_index_in_dim(kp, own, axis=2, keepdims=False)
        v_own = lax.dynamic_index_in_dim(vp, own, axis=2, keepdims=False)
        own_pos = own * MOBA_BLOCK + offs
        s_own = (jnp.einsum('bhqd,bhkd->bhqk', qc, k_own, preferred_element_type=jnp.float32) * scale
                 + bias_t[:, rel_bucket(pos[:, None] - own_pos[None, :])])
        s_own = jnp.where(own_pos[None, :] <= pos[:, None], s_own, NEG)
        p = jax.nn.softmax(jnp.concatenate([s_sel, s_own], axis=-1), axis=-1)
        p_sel = p[..., :n_sel * MOBA_BLOCK].reshape(B, H, MOBA_QCHUNK, n_sel, MOBA_BLOCK)
        p_own = p[..., n_sel * MOBA_BLOCK:]
        o = (jnp.einsum('bhqnk,bhqnkd->bhqd', p_sel, vs.astype(jnp.float32))
             + jnp.einsum('bhqk,bhkd->bhqd', p_own, v_own.astype(jnp.float32)))
        return o.astype(q.dtype)

    out = lax.map(chunk, jnp.arange(S // MOBA_QCHUNK))
    return out.transpose(1, 2, 0, 3, 4).reshape(B, H, S, dh)


def forgetting_attention(q, k, v, log_f):
    B, H, S, dh = q.shape
    c = lax.cumsum(log_f, axis=2)
    scale = dh ** -0.5
    key_pos = jnp.arange(S)

    def block(i):
        t0 = i * FOX_QBLOCK
        qb = lax.dynamic_slice_in_dim(q, t0, FOX_QBLOCK, axis=2)
        cb = lax.dynamic_slice_in_dim(c, t0, FOX_QBLOCK, axis=2)
        pos = t0 + jnp.arange(FOX_QBLOCK)
        s = (jnp.einsum('bhqd,bhkd->bhqk', qb, k, preferred_element_type=jnp.float32) * scale
             + cb[..., None] - c[:, :, None, :])
        s = jnp.where(key_pos[None, :] <= pos[:, None], s, NEG)
        p = jax.nn.softmax(s, axis=-1)
        return jnp.einsum('bhqk,bhkd->bhqd', p, v.astype(jnp.float32)).astype(q.dtype)

    out = lax.map(block, jnp.arange(S // FOX_QBLOCK))
    return out.transpose(1, 2, 0, 3, 4).reshape(B, H, S, dh)


def moba_mixer(x, w_in, w_out, rel_bias):
    q, k, v = jnp.split(x @ w_in, 3, axis=-1)
    o = moba_attention(split_heads(q), split_heads(k), split_heads(v), rel_bias)
    return merge_heads(o) @ w_out


def fox_mixer(x, w_in, b_f, w_out):
    proj = x @ w_in
    D = D_MODEL
    q, k, v = proj[..., :D], proj[..., D:2 * D], proj[..., 2 * D:3 * D]
    log_f = jax.nn.log_sigmoid((proj[..., 3 * D:] + b_f).astype(jnp.float32)).transpose(0, 2, 1)
    o = forgetting_attention(split_heads(q), split_heads(k), split_heads(v), log_f)
    return merge_heads(o) @ w_out


def swiglu(h, wg, wu, wd):
    return (jax.nn.silu(h @ wg) * (h @ wu)) @ wd


def route(h, w_router, router_bias):
    T = h.shape[0]
    s = jax.nn.sigmoid((h @ w_router).astype(jnp.float32))
    sb = s + router_bias.astype(jnp.float32)
    grp = sb.reshape(T, N_GROUPS, N_EXPERTS // N_GROUPS)
    gscore = lax.top_k(grp, 2)[0].sum(-1)
    _, gidx = lax.top_k(gscore, TOPK_GROUPS)
    gmask = jax.nn.one_hot(gidx, N_GROUPS, dtype=jnp.float32).sum(-2) > 0
    emask = jnp.repeat(gmask, N_EXPERTS // N_GROUPS, axis=-1)
    _, idx = lax.top_k(jnp.where(emask, sb, NEG), TOP_K)
    w = jnp.take_along_axis(s, idx, axis=-1)
    gates = w / w.sum(-1, keepdims=True) * ROUTED_SCALE
    return idx, gates


def moe_routed(h, idx, gates, w_gate, w_up, w_down):
    T, D = h.shape
    E = w_gate.shape[0]
    flat_e = idx.reshape(-1)
    flat_t = jnp.repeat(jnp.arange(T, dtype=jnp.int32), TOP_K)
    flat_g = gates.reshape(-1)
    order = jnp.argsort(flat_e)
    se = flat_e[order]
    counts = jnp.bincount(flat_e, length=E)
    padded = (counts + EXPERT_BLOCK - 1) // EXPERT_BLOCK * EXPERT_BLOCK
    pad_end = jnp.cumsum(padded)
    pad_start = pad_end - padded
    start = jnp.cumsum(counts) - counts
    dest = pad_start[se] + jnp.arange(T * TOP_K) - start[se]
    n_blocks = (T * TOP_K + E * (EXPERT_BLOCK - 1)) // EXPERT_BLOCK
    n_pad = n_blocks * EXPERT_BLOCK
    tok = jnp.full((n_pad,), T, jnp.int32).at[dest].set(flat_t[order])
    gbuf = jnp.zeros((n_pad,), jnp.float32).at[dest].set(flat_g[order])
    blk_e = jnp.minimum(jnp.searchsorted(pad_end, jnp.arange(n_blocks) * EXPERT_BLOCK, side='right'), E - 1)
    h_pad = jnp.concatenate([h, jnp.zeros((1, D), h.dtype)], axis=0)

    def run(args):
        tb, gb, e = args
        yb = swiglu(h_pad[tb], w_gate[e], w_up[e], w_down[e])
        return yb * gb[:, None].astype(yb.dtype)

    y = lax.map(run, (tok.reshape(n_blocks, EXPERT_BLOCK), gbuf.reshape(n_blocks, EXPERT_BLOCK), blk_e))
    return jax.ops.segment_sum(y.reshape(n_pad, D), tok, num_segments=T + 1)[:T]


def moe_layer(x, w_router, router_bias, w_gate, w_up, w_down, ws_gate, ws_up, ws_down):
    B, S, D = x.shape
    h = x.reshape(B * S, D)
    idx, gates = route(h, w_router, router_bias)
    y = moe_routed(h, idx, gates, w_gate, w_up, w_down) + swiglu(h, ws_gate, ws_up, ws_down)
    return y.reshape(B, S, D)


def setup_inputs(seed: int = 0) -> dict:
    key = jax.random.key(seed)
    ks = jax.random.split(key, 20)
    D, H, E = D_MODEL, N_HEADS, N_EXPERTS
    nrm = jax.random.normal
    s_in = D ** -0.5
    qkv_scale = jnp.concatenate([jnp.full((2 * D,), s_in), jnp.full((D,), s_in * DN_BETA)])
    moba_w_in = nrm(ks[1], (N_MOBA, D, 3 * D), jnp.float32) * qkv_scale
    fox_scale = jnp.concatenate([qkv_scale, jnp.full((H,), s_in)])
    fox_w_in = nrm(ks[2], (N_FOX, D, 3 * D + H), jnp.float32) * fox_scale
    return {
        "x": nrm(ks[0], (BATCH, SEQ, D), jnp.float32),
        "rel_bias": nrm(ks[3], (REL_BUCKETS, H), jnp.float32) * 0.5,
        "moba_w_in": moba_w_in,
        "moba_w_out": nrm(ks[4], (N_MOBA, D, D), jnp.float32) * s_in * DN_BETA,
        "fox_w_in": fox_w_in,
        "fox_b_f": jax.random.uniform(ks[5], (N_FOX, H), jnp.float32, 1.0, 6.0),
        "fox_w_out": nrm(ks[6], (N_FOX, D, D), jnp.float32) * s_in * DN_BETA,
        "ln1_g": 1.0 + 0.05 * nrm(ks[7], (DEPTH, D), jnp.float32),
        "ln1_b": 0.02 * nrm(ks[8], (DEPTH, D), jnp.float32),
        "ln2_g": 1.0 + 0.05 * nrm(ks[9], (DEPTH, D), jnp.float32),
        "ln2_b": 0.02 * nrm(ks[10], (DEPTH, D), jnp.float32),
        "w_router": nrm(ks[11], (DEPTH, D, E), jnp.float32) * s_in,
        "router_bias": 0.01 * nrm(ks[12], (DEPTH, E), jnp.float32),
        "w_gate": nrm(ks[13], (DEPTH, E, D, D_EXPERT), jnp.float32) * s_in,
        "w_up": nrm(ks[14], (DEPTH, E, D, D_EXPERT), jnp.float32) * s_in * DN_BETA,
        "w_down": nrm(ks[15], (DEPTH, E, D_EXPERT, D), jnp.float32) * D_EXPERT ** -0.5 * DN_BETA,
        "ws_gate": nrm(ks[16], (DEPTH, D, D_SHARED), jnp.float32) * s_in,
        "ws_up": nrm(ks[17], (DEPTH, D, D_SHARED), jnp.float32) * s_in * DN_BETA,
        "ws_down": nrm(ks[18], (DEPTH, D_SHARED, D), jnp.float32) * D_SHARED ** -0.5 * DN_BETA,
    }


def reference(x, rel_bias, moba_w_in, moba_w_out, fox_w_in, fox_b_f, fox_w_out,
              ln1_g, ln1_b, ln2_g, ln2_b, w_router, router_bias,
              w_gate, w_up, w_down, ws_gate, ws_up, ws_down):
    for i in range(DEPTH):
        j = i // N_MIXERS
        if i % N_MIXERS == 0:
            y = moba_mixer(x, moba_w_in[j], moba_w_out[j], rel_bias)
        else:
            y = fox_mixer(x, fox_w_in[j], fox_b_f[j], fox_w_out[j])
        x = layer_norm(DN_ALPHA * x + y, ln1_g[i], ln1_b[i])
        y = moe_layer(x, w_router[i], router_bias[i], w_gate[i], w_up[i], w_down[i],
                      ws_gate[i], ws_up[i], ws_down[i])
        x = layer_norm(DN_ALPHA * x + y, ln2_g[i], ln2_b[i])
    return x
```

```python
import functools
import math

import jax
import jax.numpy as jnp
import numpy as np
from jax import lax
from jax.experimental import pallas as pl
from jax.experimental.pallas import tpu as pltpu

F32 = jnp.float32
BF16 = jnp.bfloat16
HIGHEST = lax.Precision.HIGHEST

HEAD_DIM = 64
MOBA_BLOCK = 256
MOBA_TOPK = 3
REL_MAX_DIST = 128
TOP_K = 8
N_GROUPS = 8
TOPK_GROUPS = 4
ROUTED_SCALE = 2.5
LN_EPS = 1e-5
NEG = -1e30

LANES = 128
HEADS_PER_TILE = LANES // HEAD_DIM
EXPERT_ROWS = 256
VMEM_LIMIT = 48 * 1024 * 1024


def _cparams(*sem):
    return pltpu.CompilerParams(dimension_semantics=sem, vmem_limit_bytes=VMEM_LIMIT)


def _nt_dot(a, b, **kw):
    return lax.dot_general(a, b, (((1,), (1,)), ((), ())), preferred_element_type=F32, **kw)


def _proj_kernel(x_ref, w_ref, o_ref):
    o_ref[...] = jnp.dot(x_ref[...].astype(BF16), w_ref[...],
                         preferred_element_type=F32).astype(o_ref.dtype)


def _project(x2d, w, tm, tn):
    T, K = x2d.shape
    N = w.shape[1]
    return pl.pallas_call(
        _proj_kernel,
        out_shape=jax.ShapeDtypeStruct((T, N), BF16),
        grid=(T // tm, N // tn),
        in_specs=[pl.BlockSpec((tm, K), lambda i, j: (i, 0)),
                  pl.BlockSpec((K, tn), lambda i, j: (0, j))],
        out_specs=pl.BlockSpec((tm, tn), lambda i, j: (i, j)),
        compiler_params=_cparams("parallel", "arbitrary"),
        name="qkv_proj",
    )(x2d, w)


def _layer_norm(r, g, b):
    mu = jnp.mean(r, axis=-1, keepdims=True)
    d = r - mu
    var = jnp.mean(d * d, axis=-1, keepdims=True)
    return d * lax.rsqrt(var + LN_EPS) * g + b


def _outproj_ln_kernel(a_ref, w_ref, x_ref, g_ref, b_ref, o_ref, *, alpha):
    y = jnp.dot(a_ref[...], w_ref[...], preferred_element_type=F32)
    o_ref[...] = _layer_norm(alpha * x_ref[...] + y, g_ref[...], b_ref[...])


def _outproj_ln(attn, w_out, x2d, g, b, alpha, tm):
    T, D = x2d.shape
    row = lambda i: (i, 0)
    fixed = lambda i: (0, 0)
    return pl.pallas_call(
        functools.partial(_outproj_ln_kernel, alpha=alpha),
        out_shape=jax.ShapeDtypeStruct((T, D), F32),
        grid=(T // tm,),
        in_specs=[pl.BlockSpec((tm, D), row), pl.BlockSpec((D, D), fixed),
                  pl.BlockSpec((tm, D), row), pl.BlockSpec((1, D), fixed),
                  pl.BlockSpec((1, D), fixed)],
        out_specs=pl.BlockSpec((tm, D), row),
        compiler_params=_cparams("parallel"),
        name="outproj_ln",
    )(attn, w_out, x2d, g.reshape(1, D), b.reshape(1, D))


def _softmax_step(carry, s, vt):
    m, l, acc = carry
    m_new = jnp.maximum(m, jnp.max(s, axis=1, keepdims=True))
    alpha = jnp.exp(m - m_new)
    p = jnp.exp(s - m_new)
    l = alpha * l + jnp.sum(p, axis=1, keepdims=True)
    acc = alpha * acc + jnp.dot(p.astype(BF16), vt, preferred_element_type=F32)
    return m_new, l, acc


def _softmax_init(tq):
    return (jnp.full((tq, 1), -jnp.inf, F32), jnp.zeros((tq, 1), F32),
            jnp.zeros((tq, LANES), F32))


def _head_queries(q_ref, tq):
    lane = lax.broadcasted_iota(jnp.int32, (tq, LANES), 1)
    q2 = q_ref[0] * (HEAD_DIM ** -0.5)
    zero = jnp.zeros_like(q2)
    return lane, [jnp.where((lane // HEAD_DIM) == a, q2, zero) for a in range(HEADS_PER_TILE)]


def _fox_kernel(q_ref, k_ref, v_ref, c_ref, o_ref, *, tq):
    i = pl.program_id(2)
    lane, queries = _head_queries(q_ref, tq)
    row = lax.broadcasted_iota(jnp.int32, (tq, tq), 0)
    col = lax.broadcasted_iota(jnp.int32, (tq, tq), 1)
    outs = []
    for a, qa in enumerate(queries):
        def tile(j, carry, diagonal, qa=qa, a=a):
            start = pl.multiple_of(j * tq, tq)
            kt = k_ref[0, pl.ds(start, tq), :]
            vt = v_ref[0, pl.ds(start, tq), :]
            s = _nt_dot(qa, kt) - c_ref[0, 0, a:a + 1, pl.ds(start, tq)]
            if diagonal:
                s = jnp.where(col <= row, s, NEG)
            return _softmax_step(carry, s, vt)

        carry = tile(i, _softmax_init(tq), True)
        m, l, acc = lax.fori_loop(0, i, functools.partial(tile, diagonal=False), carry)
        outs.append(acc / l)
    o_ref[0] = jnp.where(lane < HEAD_DIM, outs[0], outs[1]).astype(o_ref.dtype)


def _fox_attention(qkv, c_pairs, tq):
    B, S, D3 = qkv.shape
    D = D3 // 3
    n_tiles = D // LANES
    return pl.pallas_call(
        functools.partial(_fox_kernel, tq=tq),
        out_shape=jax.ShapeDtypeStruct((B, S, D), BF16),
        grid=(B, n_tiles, S // tq),
        in_specs=[pl.BlockSpec((1, tq, LANES), lambda b, h, i: (b, i, h)),
                  pl.BlockSpec((1, S, LANES), lambda b, h, i: (b, 0, n_tiles + h)),
                  pl.BlockSpec((1, S, LANES), lambda b, h, i: (b, 0, 2 * n_tiles + h)),
                  pl.BlockSpec((1, 1, HEADS_PER_TILE, S), lambda b, h, i: (b, h, 0, 0))],
        out_specs=pl.BlockSpec((1, tq, LANES), lambda b, h, i: (b, i, h)),
        compiler_params=_cparams("parallel", "parallel", "arbitrary"),
        name="fox_attention",
    )(qkv, qkv, qkv, c_pairs)


def _log_sigmoid(z):
    return jnp.minimum(z, 0.0) - jnp.log1p(jnp.exp(-jnp.abs(z)))


def _fox_gates_kernel(x_ref, w_ref, b_ref, o_ref, carry_ref, *, ts):
    @pl.when(pl.program_id(1) == 0)
    def _():
        carry_ref[...] = jnp.zeros_like(carry_ref)

    z = jnp.dot(x_ref[0], w_ref[...], preferred_element_type=F32, precision=HIGHEST)
    lf = _log_sigmoid(z + b_ref[...])
    r = lax.broadcasted_iota(jnp.int32, (ts, ts), 0)
    c = lax.broadcasted_iota(jnp.int32, (ts, ts), 1)
    tri = (c <= r).astype(F32)
    cum = jnp.dot(tri, lf, preferred_element_type=F32, precision=HIGHEST) + carry_ref[...]
    carry_ref[...] = cum[ts - 1:ts, :]
    o_ref[0] = cum.T[:o_ref.shape[1], :]


def _fox_gates(x3d, w_f, b_f, ts):
    B, S, D = x3d.shape
    H = w_f.shape[1]
    w_pad = jnp.zeros((D, LANES), F32).at[:, :H].set(w_f)
    b_pad = jnp.zeros((1, LANES), F32).at[0, :H].set(b_f)
    return pl.pallas_call(
        functools.partial(_fox_gates_kernel, ts=ts),
        out_shape=jax.ShapeDtypeStruct((B, H, S), F32),
        grid=(B, S // ts),
        in_specs=[pl.BlockSpec((1, ts, D), lambda b, s: (b, s, 0)),
                  pl.BlockSpec((D, LANES), lambda b, s: (0, 0)),
                  pl.BlockSpec((1, LANES), lambda b, s: (0, 0))],
        out_specs=pl.BlockSpec((1, H, ts), lambda b, s: (b, 0, s)),
        scratch_shapes=[pltpu.VMEM((1, LANES), F32)],
        compiler_params=_cparams("parallel", "arbitrary"),
        name="fox_gates",
    )(x3d, w_pad, b_pad)


def _moba_kernel(q_ref, k_ref, v_ref, bown_ref, badj_ref, o_ref, kmean_ref, *, blk, nb):
    i = pl.program_id(2)
    nbp = kmean_ref.shape[0]

    @pl.when(i == 0)
    def _():
        kmean_ref[...] = jnp.zeros_like(kmean_ref)
        for n in range(nb):
            kmean_ref[n:n + 1, :] = jnp.mean(
                k_ref[0, n * blk:(n + 1) * blk, :].astype(F32), axis=0, keepdims=True)

    lane, queries = _head_queries(q_ref, blk)
    row = lax.broadcasted_iota(jnp.int32, (blk, blk), 0)
    col = lax.broadcasted_iota(jnp.int32, (blk, blk), 1)
    bcol = lax.broadcasted_iota(jnp.int32, (blk, nbp), 1)
    own_start = pl.multiple_of(i * blk, blk)
    outs = []
    for a, qa in enumerate(queries):
        bscore = _nt_dot(qa.astype(F32), kmean_ref[...], precision=HIGHEST)
        cand = jnp.where(bcol < i, bscore, NEG)
        sel = jnp.zeros((blk, nbp), F32)
        for _ in range(MOBA_TOPK):
            mx = jnp.max(cand, axis=1, keepdims=True)
            first = jnp.min(jnp.where(cand == mx, bcol, nbp), axis=1, keepdims=True)
            hit = bcol == first
            sel = jnp.where(hit, 1.0, sel)
            cand = jnp.where(hit, -jnp.inf, cand)
        sel = jnp.where(bcol < i, sel, 0.0)

        s = _nt_dot(qa, k_ref[0, pl.ds(own_start, blk), :]) + bown_ref[a]
        s = jnp.where(col <= row, s, NEG)
        carry = _softmax_step(_softmax_init(blk), s, v_ref[0, pl.ds(own_start, blk), :])

        bias_far = badj_ref[a, blk - 1:blk, 0:1]

        def past(j, carry, qa=qa, a=a, sel=sel, bias_far=bias_far):
            start = pl.multiple_of(j * blk, blk)
            chosen = jnp.max(jnp.where(bcol == j, sel, 0.0), axis=1, keepdims=True) > 0.0
            bias = jnp.where(j == i - 1, badj_ref[a], bias_far)
            s = _nt_dot(qa, k_ref[0, pl.ds(start, blk), :]) + bias
            s = jnp.where(chosen, s, NEG)
            return _softmax_step(carry, s, v_ref[0, pl.ds(start, blk), :])

        m, l, acc = lax.fori_loop(0, i, past, carry)
        outs.append(acc / l)
    o_ref[0] = jnp.where(lane < HEAD_DIM, outs[0], outs[1]).astype(o_ref.dtype)


def _rel_bucket(dist, n_buckets):
    n = jnp.maximum(dist, 0)
    max_exact = n_buckets // 2
    nf = jnp.maximum(n, 1).astype(F32)
    large = max_exact + (jnp.log(nf / max_exact) / math.log(REL_MAX_DIST / max_exact)
                         * (n_buckets - max_exact)).astype(jnp.int32)
    large = jnp.minimum(large, n_buckets - 1)
    return jnp.where(n < max_exact, n, large)


def _moba_attention(qkv, rel_bias):
    B, S, D3 = qkv.shape
    D = D3 // 3
    n_tiles = D // LANES
    blk = MOBA_BLOCK
    nb = S // blk
    nbp = max(8, -(-nb // 8) * 8)
    r = jnp.arange(blk)
    delta = r[:, None] - r[None, :]
    bias_t = rel_bias.T.astype(F32)
    n_buckets = rel_bias.shape[0]
    b_own = bias_t[:, _rel_bucket(delta, n_buckets)]
    b_adj = bias_t[:, _rel_bucket(delta + blk, n_buckets)]
    hp = HEADS_PER_TILE
    return pl.pallas_call(
        functools.partial(_moba_kernel, blk=blk, nb=nb),
        out_shape=jax.ShapeDtypeStruct((B, S, D), BF16),
        grid=(B, n_tiles, nb),
        in_specs=[pl.BlockSpec((1, blk, LANES), lambda b, h, i: (b, i, h)),
                  pl.BlockSpec((1, S, LANES), lambda b, h, i: (b, 0, n_tiles + h)),
                  pl.BlockSpec((1, S, LANES), lambda b, h, i: (b, 0, 2 * n_tiles + h)),
                  pl.BlockSpec((hp, blk, blk), lambda b, h, i: (h, 0, 0)),
                  pl.BlockSpec((hp, blk, blk), lambda b, h, i: (h, 0, 0))],
        out_specs=pl.BlockSpec((1, blk, LANES), lambda b, h, i: (b, i, h)),
        scratch_shapes=[pltpu.VMEM((nbp, LANES), F32)],
        compiler_params=_cparams("parallel", "parallel", "arbitrary"),
        name="moba_attention",
    )(qkv, qkv, qkv, b_own, b_adj)


def _router_kernel(x_ref, wt_ref, bias_ref, idx_ref, gate_ref, rank_ref, cnt_ref, run_ref, *, tm):
    @pl.when(pl.program_id(0) == 0)
    def _():
        run_ref[...] = jnp.zeros_like(run_ref)

    E = wt_ref.shape[0]
    gsz = E // N_GROUPS
    logits = _nt_dot(wt_ref[...], x_ref[...], precision=HIGHEST)
    s = jax.nn.sigmoid(logits)
    sb = s + bias_ref[...]
    neg_inf = -jnp.inf

    giota = lax.broadcasted_iota(jnp.int32, (gsz, tm), 0)
    gscore = []
    for g in range(N_GROUPS):
        blk = sb[g * gsz:(g + 1) * gsz, :]
        m1 = jnp.max(blk, axis=0, keepdims=True)
        i1 = jnp.min(jnp.where(blk == m1, giota, gsz), axis=0, keepdims=True)
        m2 = jnp.max(jnp.where(giota == i1, neg_inf, blk), axis=0, keepdims=True)
        gscore.append(m1 + m2)

    gsel = [jnp.zeros((1, tm), jnp.bool_) for _ in range(N_GROUPS)]
    for _ in range(TOPK_GROUPS):
        mx = functools.reduce(jnp.maximum, gscore)
        found = jnp.zeros((1, tm), jnp.bool_)
        for g in range(N_GROUPS):
            hit = (gscore[g] == mx) & jnp.logical_not(found)
            gsel[g] = gsel[g] | hit
            found = found | hit
            gscore[g] = jnp.where(hit, neg_inf, gscore[g])
    emask = jnp.concatenate([jnp.broadcast_to(gsel[g], (gsz, tm)) for g in range(N_GROUPS)], axis=0)
    cand = jnp.where(emask, sb, NEG)

    eiota = lax.broadcasted_iota(jnp.int32, (E, tm), 0)
    hits, idxs, ws = [], [], []
    for _ in range(TOP_K):
        mx = jnp.max(cand, axis=0, keepdims=True)
        first = jnp.min(jnp.where(cand == mx, eiota, E), axis=0, keepdims=True)
        hit = eiota == first
        hits.append(hit)
        idxs.append(first)
        ws.append(jnp.sum(jnp.where(hit, s, 0.0), axis=0, keepdims=True))
        cand = jnp.where(hit, neg_inf, cand)
    wsum = functools.reduce(jnp.add, ws)

    chosen = functools.reduce(jnp.logical_or, hits)
    onehot = jnp.where(chosen, 1.0, 0.0)
    tr = lax.broadcasted_iota(jnp.int32, (tm, tm), 0)
    tc = lax.broadcasted_iota(jnp.int32, (tm, tm), 1)
    before = (tr < tc).astype(BF16)
    prior = jnp.dot(onehot.astype(BF16), before, preferred_element_type=F32) + run_ref[...]
    for k in range(TOP_K):
        idx_ref[k:k + 1, :] = idxs[k]
        gate_ref[k:k + 1, :] = ws[k] / wsum * ROUTED_SCALE
        rank_ref[k:k + 1, :] = jnp.sum(jnp.where(hits[k], prior, 0.0), axis=0,
                                       keepdims=True).astype(jnp.int32)
    run_ref[...] = run_ref[...] + jnp.sum(onehot, axis=1, keepdims=True)
    cnt_ref[...] = run_ref[...]


def _router(x2d, w_router, router_bias, tm):
    T, D = x2d.shape
    E = w_router.shape[1]
    tok = lambda i: (0, i)
    fixed = lambda i: (0, 0)
    return pl.pallas_call(
        functools.partial(_router_kernel, tm=tm),
        out_shape=(jax.ShapeDtypeStruct((TOP_K, T), jnp.int32),
                   jax.ShapeDtypeStruct((TOP_K, T), F32),
                   jax.ShapeDtypeStruct((TOP_K, T), jnp.int32),
                   jax.ShapeDtypeStruct((E, 1), F32)),
        grid=(T // tm,),
        in_specs=[pl.BlockSpec((tm, D), lambda i: (i, 0)),
                  pl.BlockSpec((E, D), fixed),
                  pl.BlockSpec((E, 1), fixed)],
        out_specs=(pl.BlockSpec((TOP_K, tm), tok), pl.BlockSpec((TOP_K, tm), tok),
                   pl.BlockSpec((TOP_K, tm), tok), pl.BlockSpec((E, 1), fixed)),
        scratch_shapes=[pltpu.VMEM((E, 1), F32)],
        compiler_params=_cparams("arbitrary"),
        name="router",
    )(x2d, w_router.T, router_bias.reshape(E, 1).astype(F32))


def _row_copy(src_ref, src_row, dst_ref, dst_row, sem):
    return pltpu.make_async_copy(src_ref.at[pl.ds(src_row, 1), :],
                                 dst_ref.at[pl.ds(dst_row, 1), :], sem)


def _dispatch_kernel(dest_ref, x_ref, zero_ref, xs_ref, sem, *, tt):
    del zero_ref

    def issue(r, c):
        for k in range(TOP_K):
            _row_copy(x_ref, r, xs_ref, dest_ref[r * TOP_K + k], sem).start()
        return c

    def drain(r, c):
        for k in range(TOP_K):
            _row_copy(x_ref, 0, xs_ref, 0, sem).wait()
        return c

    lax.fori_loop(0, tt, issue, 0)
    lax.fori_loop(0, tt, drain, 0)


def _dispatch(x2d, dest_flat, n_rows, tt):
    T, D = x2d.shape
    return pl.pallas_call(
        functools.partial(_dispatch_kernel, tt=tt),
        out_shape=jax.ShapeDtypeStruct((n_rows, D), F32),
        grid=(T // tt,),
        in_specs=[pl.BlockSpec((tt * TOP_K,), lambda i: (i,), memory_space=pltpu.SMEM),
                  pl.BlockSpec((tt, D), lambda i: (i, 0)),
                  pl.BlockSpec(memory_space=pl.ANY)],
        out_specs=pl.BlockSpec(memory_space=pl.ANY),
        scratch_shapes=[pltpu.SemaphoreType.DMA],
        input_output_aliases={2: 0},
        compiler_params=_cparams("arbitrary"),
        name="moe_dispatch",
    )(dest_flat, x2d, jnp.zeros((n_rows, D), F32))


def _silu(g):
    return g * jax.nn.sigmoid(g)


def _expert_kernel(blk_e_ref, n_used_ref, x_ref, wg_ref, wu_ref, wd_ref, o_ref):
    del blk_e_ref
    i = pl.program_id(0)

    @pl.when(i < n_used_ref[0])
    def _():
        x = x_ref[...].astype(BF16)
        g = jnp.dot(x, wg_ref[0].astype(BF16), preferred_element_type=F32)
        u = jnp.dot(x, wu_ref[0].astype(BF16), preferred_element_type=F32)
        a = (_silu(g) * u).astype(BF16)
        o_ref[...] = jnp.dot(a, wd_ref[0].astype(BF16), preferred_element_type=F32)

    @pl.when(i >= n_used_ref[0])
    def _():
        o_ref[...] = jnp.zeros_like(o_ref)


def _expert_mlp(xs, blk_e, n_used, w_gate, w_up, w_down):
    n_rows, D = xs.shape
    E, _, F = w_gate.shape
    tm = EXPERT_ROWS
    rows = lambda i, be, nu: (i, 0)
    return pl.pallas_call(
        _expert_kernel,
        out_shape=jax.ShapeDtypeStruct((n_rows, D), F32),
        grid_spec=pltpu.PrefetchScalarGridSpec(
            num_scalar_prefetch=2,
            grid=(n_rows // tm,),
            in_specs=[pl.BlockSpec((tm, D), rows),
                      pl.BlockSpec((1, D, F), lambda i, be, nu: (be[i], 0, 0)),
                      pl.BlockSpec((1, D, F), lambda i, be, nu: (be[i], 0, 0)),
                      pl.BlockSpec((1, F, D), lambda i, be, nu: (be[i], 0, 0))],
            out_specs=pl.BlockSpec((tm, D), rows)),
        compiler_params=_cparams("arbitrary"),
        name="moe_experts",
    )(blk_e, n_used, xs, w_gate, w_up, w_down)


def _combine_kernel(dest_ref, x_ref, gate_ref, wsg_ref, wsu_ref, wsd_ref, g_ref, b_ref, ys_ref,
                    o_ref, rows_ref, sem, *, tt, alpha):
    def issue(r, c):
        for k in range(TOP_K):
            _row_copy(ys_ref, dest_ref[r * TOP_K + k], rows_ref.at[k], r, sem).start()
        return c

    def drain(r, c):
        for k in range(TOP_K):
            _row_copy(ys_ref, 0, rows_ref.at[0], 0, sem).wait()
        return c

    lax.fori_loop(0, tt, issue, 0)
    x = x_ref[...]
    xb = x.astype(BF16)
    g = jnp.dot(xb, wsg_ref[...], preferred_element_type=F32)
    u = jnp.dot(xb, wsu_ref[...], preferred_element_type=F32)
    y = jnp.dot((_silu(g) * u).astype(BF16), wsd_ref[...], preferred_element_type=F32)
    lax.fori_loop(0, tt, drain, 0)
    gates = gate_ref[...]
    for k in range(TOP_K):
        y = y + rows_ref[k] * gates[:, k:k + 1]
    o_ref[...] = _layer_norm(alpha * x + y, g_ref[...], b_ref[...])


def _combine(x2d, ys, dest_flat, gates_tk, ws_gate, ws_up, ws_down, g, b, alpha, tt):
    T, D = x2d.shape
    F = ws_gate.shape[1]
    row = lambda i: (i, 0)
    fixed = lambda i: (0, 0)
    return pl.pallas_call(
        functools.partial(_combine_kernel, tt=tt, alpha=alpha),
        out_shape=jax.ShapeDtypeStruct((T, D), F32),
        grid=(T // tt,),
        in_specs=[pl.BlockSpec((tt * TOP_K,), lambda i: (i,), memory_space=pltpu.SMEM),
                  pl.BlockSpec((tt, D), row),
                  pl.BlockSpec((tt, TOP_K), row),
                  pl.BlockSpec((D, F), fixed), pl.BlockSpec((D, F), fixed),
                  pl.BlockSpec((F, D), fixed),
                  pl.BlockSpec((1, D), fixed), pl.BlockSpec((1, D), fixed),
                  pl.BlockSpec(memory_space=pl.ANY)],
        out_specs=pl.BlockSpec((tt, D), row),
        scratch_shapes=[pltpu.VMEM((TOP_K, tt, D), F32), pltpu.SemaphoreType.DMA],
        compiler_params=_cparams("arbitrary"),
        name="moe_combine",
    )(dest_flat, x2d, gates_tk, ws_gate.astype(BF16), ws_up.astype(BF16), ws_down.astype(BF16),
      g.reshape(1, D), b.reshape(1, D), ys)


def _pick(n, pref):
    t = min(n, pref)
    while n % t:
        t //= 2
    return t


def _moe_layer(x2d, w_router, router_bias, w_gate, w_up, w_down, ws_gate, ws_up, ws_down,
               g, b, alpha):
    T, D = x2d.shape
    E = w_router.shape[1]
    idx, gates, rank, counts = _router(x2d, w_router, router_bias, _pick(T, 256))
    counts = counts[:, 0].astype(jnp.int32)
    padded = (counts + EXPERT_ROWS - 1) // EXPERT_ROWS * EXPERT_ROWS
    seg_end = jnp.cumsum(padded)
    seg_start = seg_end - padded
    n_blocks = (T * TOP_K + E * (EXPERT_ROWS - 1)) // EXPERT_ROWS
    dest = (seg_start[idx] + rank).T.reshape(-1)
    blk_e = jnp.minimum(jnp.searchsorted(seg_end, jnp.arange(n_blocks) * EXPERT_ROWS, side='right'),
                        E - 1).astype(jnp.int32)
    n_used = (seg_end[-1:] // EXPERT_ROWS).astype(jnp.int32)
    tt = _pick(T, 128)
    xs = _dispatch(x2d, dest, n_blocks * EXPERT_ROWS, tt)
    ys = _expert_mlp(xs, blk_e, n_used, w_gate, w_up, w_down)
    return _combine(x2d, ys, dest, gates.T, ws_gate, ws_up, ws_down, g, b, alpha, tt)


def kernel(x, rel_bias, moba_w_in, moba_w_out, fox_w_in, fox_b_f, fox_w_out, ln1_g, ln1_b, ln2_g,
           ln2_b, w_router, router_bias, w_gate, w_up, w_down, ws_gate, ws_up, ws_down):
    B, S, D = x.shape
    T = B * S
    depth = ln1_g.shape[0]
    alpha = (2 * depth) ** 0.25
    tm = _pick(T, 512)
    x2d = x.reshape(T, D)
    for i in range(depth):
        j = i // 2
        if i % 2 == 0:
            qkv = _project(x2d, moba_w_in[j].astype(BF16), tm, _pick(3 * D, 1024))
            attn = _moba_attention(qkv.reshape(B, S, 3 * D), rel_bias)
            w_out = moba_w_out[j]
        else:
            w_in = fox_w_in[j]
            qkv = _project(x2d, w_in[:, :3 * D].astype(BF16), tm, _pick(3 * D, 1024))
            c = _fox_gates(x2d.reshape(B, S, D), w_in[:, 3 * D:], fox_b_f[j], _pick(S, 512))
            c = c.reshape(B, c.shape[1] // HEADS_PER_TILE, HEADS_PER_TILE, S)
            attn = _fox_attention(qkv.reshape(B, S, 3 * D), c, _pick(S, 256))
            w_out = fox_w_out[j]
        x2d = _outproj_ln(attn.reshape(T, D), w_out.astype(BF16), x2d, ln1_g[i], ln1_b[i], alpha, tm)
        x2d = _moe_layer(x2d, w_router[i], router_bias[i], w_gate[i], w_up[i], w_down[i],
                         ws_gate[i], ws_up[i], ws_down[i], ln2_g[i], ln2_b[i], alpha)
    return x2d.reshape(B, S, D)
```

```python
import functools
import math

import jax
import jax.numpy as jnp
from jax import lax
from jax.experimental import pallas as pl
from jax.experimental.pallas import tpu as pltpu

F32 = jnp.float32
BF16 = jnp.bfloat16
HIGHEST = lax.Precision.HIGHEST

HEAD_DIM = 64
MOBA_BLOCK = 256
MOBA_TOPK = 3
REL_MAX_DIST = 128
TOP_K = 8
N_GROUPS = 8
TOPK_GROUPS = 4
ROUTED_SCALE = 2.5
LN_EPS = 1e-5
NEG = -1e30

LANES = 128
HEADS_PER_TILE = LANES // HEAD_DIM
EXPERT_ROWS = 256
VMEM_LIMIT = 48 * 1024 * 1024


def _cparams(*sem):
    return pltpu.CompilerParams(dimension_semantics=sem, vmem_limit_bytes=VMEM_LIMIT)


def _nt_dot(a, b, **kw):
    return lax.dot_general(a, b, (((1,), (1,)), ((), ())), preferred_element_type=F32, **kw)


def _proj_kernel(x_ref, w_ref, o_ref):
    o_ref[...] = jnp.dot(x_ref[...].astype(BF16), w_ref[...],
                         preferred_element_type=F32).astype(o_ref.dtype)


def _project(x2d, w, tm, tn):
    T, K = x2d.shape
    N = w.shape[1]
    return pl.pallas_call(
        _proj_kernel,
        out_shape=jax.ShapeDtypeStruct((T, N), BF16),
        grid=(T // tm, N // tn),
        in_specs=[pl.BlockSpec((tm, K), lambda i, j: (i, 0)),
                  pl.BlockSpec((K, tn), lambda i, j: (0, j))],
        out_specs=pl.BlockSpec((tm, tn), lambda i, j: (i, j)),
        compiler_params=_cparams("parallel", "arbitrary"),
        name="k_proj",
    )(x2d, w)


def _proj_t_kernel(x_ref, wt_ref, o_ref):
    o_ref[...] = _nt_dot(wt_ref[...], x_ref[...].astype(BF16)).astype(o_ref.dtype)


def _project_t(x2d, wt, tm, tn):
    T, K = x2d.shape
    N = wt.shape[0]
    return pl.pallas_call(
        _proj_t_kernel,
        out_shape=jax.ShapeDtypeStruct((N, T), BF16),
        grid=(T // tm, N // tn),
        in_specs=[pl.BlockSpec((tm, K), lambda i, j: (i, 0)),
                  pl.BlockSpec((tn, K), lambda i, j: (j, 0))],
        out_specs=pl.BlockSpec((tn, tm), lambda i, j: (j, i)),
        compiler_params=_cparams("parallel", "arbitrary"),
        name="qv_proj_t",
    )(x2d, wt)


def _layer_norm(r, g, b):
    mu = jnp.mean(r, axis=-1, keepdims=True)
    d = r - mu
    var = jnp.mean(d * d, axis=-1, keepdims=True)
    return d * lax.rsqrt(var + LN_EPS) * g + b


def _outproj_ln_kernel(at_ref, w_ref, x_ref, g_ref, b_ref, o_ref, *, alpha):
    y = lax.dot_general(at_ref[...], w_ref[...], (((0,), (0,)), ((), ())),
                        preferred_element_type=F32)
    o_ref[...] = _layer_norm(alpha * x_ref[...] + y, g_ref[...], b_ref[...])


def _outproj_ln(attn_t, w_out, x2d, g, b, alpha, tm):
    T, D = x2d.shape
    row = lambda i: (i, 0)
    fixed = lambda i: (0, 0)
    return pl.pallas_call(
        functools.partial(_outproj_ln_kernel, alpha=alpha),
        out_shape=jax.ShapeDtypeStruct((T, D), F32),
        grid=(T // tm,),
        in_specs=[pl.BlockSpec((D, tm), lambda i: (0, i)), pl.BlockSpec((D, D), fixed),
                  pl.BlockSpec((tm, D), row), pl.BlockSpec((1, D), fixed),
                  pl.BlockSpec((1, D), fixed)],
        out_specs=pl.BlockSpec((tm, D), row),
        compiler_params=_cparams("parallel"),
        name="outproj_ln",
    )(attn_t, w_out, x2d, g.reshape(1, D), b.reshape(1, D))


def _spare(a, n=0):
    return ((a + 1) % HEADS_PER_TILE) * HEAD_DIM + n


def _attend(carries, chains):
    scores = []
    for blocks in chains:
        parts = []
        for kt, qt, _, extra, mask in blocks:
            s = jnp.dot(kt, qt, preferred_element_type=F32)
            if extra is not None:
                s = s + extra
            if mask is not None:
                s = jnp.where(mask, s, NEG)
            parts.append(s)
        scores.append(parts)
    probs = []
    for (m, _), parts in zip(carries, scores):
        tops = [jnp.max(s, axis=0, keepdims=True) for s in parts]
        m_new = functools.reduce(jnp.maximum, tops, m)
        probs.append((m_new, jnp.exp(m - m_new), [jnp.exp(s - m_new).astype(BF16) for s in parts]))
    out = []
    for (_, acc), (m_new, decay, ps), blocks in zip(carries, probs, chains):
        acc = decay * acc
        for (_, _, vt, _, _), p in zip(blocks, ps):
            acc = acc + jnp.dot(vt, p, preferred_element_type=F32)
        out.append((m_new, acc))
    return tuple(out)


def _scores_init(tq):
    return jnp.full((1, tq), -jnp.inf, F32), jnp.zeros((LANES, tq), F32)


def _finish(carries, o_ref):
    row = lax.broadcasted_iota(jnp.int32, carries[0][1].shape, 0)
    outs = []
    for a, (_, acc) in enumerate(carries):
        s = _spare(a)
        outs.append(acc / acc[s:s + 1, :])
    o_ref[...] = jnp.where(row < HEAD_DIM, outs[0], outs[1]).astype(o_ref.dtype)


def _fill_v_aug(vt_ref, vaug_ref, tk):
    n_tiles = vt_ref.shape[1] // tk
    row = lax.broadcasted_iota(jnp.int32, (LANES, tk), 0)
    for n in range(n_tiles):
        v = vt_ref[:, n * tk:(n + 1) * tk]
        for a in range(HEADS_PER_TILE):
            vaug_ref[a, n] = jnp.where(row // HEAD_DIM == a, v, jnp.ones_like(v))


FOX_SPLIT = 3


def _fox_kernel(qt_ref, k_ref, vt_ref, o_ref, vaug_ref, *, tq):
    i = pl.program_id(2)

    @pl.when(i == 0)
    def _():
        _fill_v_aug(vt_ref, vaug_ref, tq)

    row = lax.broadcasted_iota(jnp.int32, (LANES, tq), 0)
    qt = qt_ref[...]
    queries = []
    for a in range(HEADS_PER_TILE):
        offs = (row >= _spare(a)) & (row < _spare(a, FOX_SPLIT))
        queries.append(jnp.where(row // HEAD_DIM == a, qt,
                                 jnp.where(offs, -1.0, 0.0).astype(BF16)))
    key = lax.broadcasted_iota(jnp.int32, (tq, tq), 0)
    qry = lax.broadcasted_iota(jnp.int32, (tq, tq), 1)
    causal = key <= qry

    def block(a, j, mask=None):
        start = pl.multiple_of(j * tq, tq)
        return k_ref[a, 0, pl.ds(start, tq), :], queries[a], vaug_ref[a, j], None, mask

    heads = range(HEADS_PER_TILE)
    carries = lax.cond(
        i % 2 == 1,
        lambda c: _attend(c, [[block(a, i - 1), block(a, i, causal)] for a in heads]),
        lambda c: _attend(c, [[block(a, i, causal)] for a in heads]),
        tuple(_scores_init(tq) for _ in heads))
    carries = lax.fori_loop(
        0, i // 2,
        lambda j, c: _attend(c, [[block(a, 2 * j), block(a, 2 * j + 1)] for a in heads]),
        carries)
    _finish(carries, o_ref)


def _fox_attention(qv_t, k_aug, B, S, tq):
    D = k_aug.shape[-1]
    n_tiles = D // LANES
    nq = S // tq
    return pl.pallas_call(
        functools.partial(_fox_kernel, tq=tq),
        out_shape=jax.ShapeDtypeStruct((D, B * S), BF16),
        grid=(B, n_tiles, nq),
        in_specs=[pl.BlockSpec((LANES, tq), lambda b, h, i: (h, b * nq + i)),
                  pl.BlockSpec((HEADS_PER_TILE, 1, S, LANES), lambda b, h, i: (0, b, 0, h)),
                  pl.BlockSpec((LANES, S), lambda b, h, i: (n_tiles + h, b))],
        out_specs=pl.BlockSpec((LANES, tq), lambda b, h, i: (h, b * nq + i)),
        scratch_shapes=[pltpu.VMEM((HEADS_PER_TILE, nq, LANES, tq), BF16)],
        compiler_params=_cparams("parallel", "parallel", "arbitrary"),
        name="fox_attention",
    )(qv_t, k_aug, qv_t)


def _log_sigmoid(z):
    return jnp.minimum(z, 0.0) - jnp.log1p(jnp.exp(-jnp.abs(z)))


def _fox_gates_kernel(x_ref, w_ref, b_ref, o_ref, carry_ref, *, ts):
    @pl.when(pl.program_id(1) == 0)
    def _():
        carry_ref[...] = jnp.zeros_like(carry_ref)

    z = jnp.dot(x_ref[0], w_ref[...], preferred_element_type=F32, precision=HIGHEST)
    lf = _log_sigmoid(z + b_ref[...])
    r = lax.broadcasted_iota(jnp.int32, (ts, ts), 0)
    c = lax.broadcasted_iota(jnp.int32, (ts, ts), 1)
    tri = (c <= r).astype(F32)
    cum = jnp.dot(tri, lf, preferred_element_type=F32, precision=HIGHEST) + carry_ref[...]
    carry_ref[...] = cum[ts - 1:ts, :]
    o_ref[0] = cum


def _fox_gates(x3d, w_f, b_f, ts):
    B, S, D = x3d.shape
    H = w_f.shape[1]
    w_pad = jnp.zeros((D, LANES), F32).at[:, :H].set(w_f)
    b_pad = jnp.zeros((1, LANES), F32).at[0, :H].set(b_f)
    return pl.pallas_call(
        functools.partial(_fox_gates_kernel, ts=ts),
        out_shape=jax.ShapeDtypeStruct((B, S, LANES), F32),
        grid=(B, S // ts),
        in_specs=[pl.BlockSpec((1, ts, D), lambda b, s: (b, s, 0)),
                  pl.BlockSpec((D, LANES), lambda b, s: (0, 0)),
                  pl.BlockSpec((1, LANES), lambda b, s: (0, 0))],
        out_specs=pl.BlockSpec((1, ts, LANES), lambda b, s: (b, s, 0)),
        scratch_shapes=[pltpu.VMEM((1, LANES), F32)],
        compiler_params=_cparams("parallel", "arbitrary"),
        name="fox_gates",
    )(x3d, w_pad, b_pad)


def _fox_kprep_kernel(k_ref, c_ref, o_ref):
    tm, D = k_ref.shape
    lane = lax.broadcasted_iota(jnp.int32, (tm, LANES), 1)
    for t in range(D // LANES):
        k = k_ref[:, t * LANES:(t + 1) * LANES].astype(F32)
        for a in range(HEADS_PER_TILE):
            h = t * HEADS_PER_TILE + a
            rest = c_ref[:, h:h + 1]
            aug = jnp.zeros((tm, LANES), F32)
            for n in range(FOX_SPLIT):
                piece = rest.astype(BF16).astype(F32)
                rest = rest - piece
                aug = jnp.where(lane == _spare(a, n), piece, aug)
            o_ref[a, :, t * LANES:(t + 1) * LANES] = jnp.where(lane // HEAD_DIM == a, k,
                                                               aug).astype(BF16)


def _fox_kprep(k2d, c_tok, tm):
    T, D = k2d.shape
    return pl.pallas_call(
        _fox_kprep_kernel,
        out_shape=jax.ShapeDtypeStruct((HEADS_PER_TILE, T, D), BF16),
        grid=(T // tm,),
        in_specs=[pl.BlockSpec((tm, D), lambda i: (i, 0)),
                  pl.BlockSpec((tm, LANES), lambda i: (i, 0))],
        out_specs=pl.BlockSpec((HEADS_PER_TILE, tm, D), lambda i: (0, i, 0)),
        compiler_params=_cparams("parallel"),
        name="fox_kprep",
    )(k2d, c_tok)


MOBA_SLAB = 16


def _moba_kernel(bfar_ref, qt_ref, k_ref, vt_ref, kmean_ref, bown_ref, badj_ref, o_ref,
                 vaug_ref, sel_ref, *, blk):
    t = pl.program_id(1)
    i = pl.program_id(2)
    nb = kmean_ref.shape[1]

    @pl.when(i == 0)
    def _():
        _fill_v_aug(vt_ref, vaug_ref, blk)

    row = lax.broadcasted_iota(jnp.int32, (LANES, blk), 0)
    brow = lax.broadcasted_iota(jnp.int32, (nb, blk), 0)
    qt = qt_ref[...]
    zero = jnp.zeros_like(qt)
    base, slabs = [], []
    for a in range(HEADS_PER_TILE):
        qa = jnp.where(row // HEAD_DIM == a, qt, zero)
        base.append(qa)
        bscore = jnp.dot(kmean_ref[0], qa.astype(F32), preferred_element_type=F32,
                         precision=HIGHEST)
        cand = jnp.where(brow < i, bscore, NEG)
        sel = jnp.zeros((nb, blk), F32)
        for _ in range(MOBA_TOPK):
            mx = jnp.max(cand, axis=0, keepdims=True)
            first = jnp.min(jnp.where(cand == mx, brow, nb), axis=0, keepdims=True)
            hit = brow == first
            sel = jnp.where(hit, 1.0, sel)
            cand = jnp.where(hit, -jnp.inf, cand)
        sel_ref[a] = jnp.where(brow < i, sel, 0.0)
        slabs.append(_spare(a) // MOBA_SLAB * MOBA_SLAB)

    srow = lax.broadcasted_iota(jnp.int32, (MOBA_SLAB, blk), 0)

    def query(a, j, bias_hi, bias_lo):
        s0 = slabs[a]
        chosen = sel_ref[a, pl.ds(j, 1), :] > 0.0
        slab = jnp.where(srow == _spare(a) - s0, jnp.where(chosen, bias_hi, NEG),
                         jnp.where(srow == _spare(a, 1) - s0, bias_lo, 0.0)).astype(BF16)
        head = [base[a][:s0]] if s0 else []
        tail = [base[a][s0 + MOBA_SLAB:]] if s0 + MOBA_SLAB < LANES else []
        return jnp.concatenate(head + [slab] + tail, axis=0)

    key = lax.broadcasted_iota(jnp.int32, (blk, blk), 0)
    qry = lax.broadcasted_iota(jnp.int32, (blk, blk), 1)
    causal = key <= qry
    heads = range(HEADS_PER_TILE)

    def keys(a, j):
        start = pl.multiple_of(j * blk, blk)
        return k_ref[a, 0, pl.ds(start, blk), :]

    def own(a):
        return keys(a, i), base[a], vaug_ref[a, i], bown_ref[a], causal

    def prev(a):
        return keys(a, i - 1), query(a, i - 1, 0.0, 0.0), vaug_ref[a, i - 1], badj_ref[a], None

    def far(a, j):
        h = t * HEADS_PER_TILE + a
        return keys(a, j), query(a, j, bfar_ref[0, h], bfar_ref[1, h]), vaug_ref[a, j], None, None

    carries = lax.cond(
        i >= 1,
        lambda c: _attend(c, [[prev(a), own(a)] for a in heads]),
        lambda c: _attend(c, [[own(a)] for a in heads]),
        tuple(_scores_init(blk) for _ in heads))
    carries = lax.cond(
        (i >= 2) & (i % 2 == 0),
        lambda c: _attend(c, [[far(a, i - 2)] for a in heads]),
        lambda c: c, carries)
    carries = lax.fori_loop(
        0, (i - 1) // 2,
        lambda j, c: _attend(c, [[far(a, 2 * j), far(a, 2 * j + 1)] for a in heads]),
        carries)
    _finish(carries, o_ref)


def _rel_bucket(dist, n_buckets):
    n = jnp.maximum(dist, 0)
    max_exact = n_buckets // 2
    nf = jnp.maximum(n, 1).astype(F32)
    large = max_exact + (jnp.log(nf / max_exact) / math.log(REL_MAX_DIST / max_exact)
                         * (n_buckets - max_exact)).astype(jnp.int32)
    large = jnp.minimum(large, n_buckets - 1)
    return jnp.where(n < max_exact, n, large)


def _moba_attention(qv_t, k_aug, k_mean, rel_bias, B, S):
    D = k_aug.shape[-1]
    n_tiles = D // LANES
    blk = MOBA_BLOCK
    nb = S // blk
    hp = HEADS_PER_TILE
    r = jnp.arange(blk)
    delta = r[None, :] - r[:, None]
    bias_t = rel_bias.T.astype(F32)
    n_buckets = rel_bias.shape[0]
    b_own = bias_t[:, _rel_bucket(delta, n_buckets)]
    b_adj = bias_t[:, _rel_bucket(delta + blk, n_buckets)]
    b_far = bias_t[:, n_buckets - 1]
    far_hi = b_far.astype(BF16).astype(F32)
    far_lo = (b_far - far_hi).astype(BF16).astype(F32)
    b_far2 = jnp.stack([far_hi, far_lo])
    return pl.pallas_call(
        functools.partial(_moba_kernel, blk=blk),
        out_shape=jax.ShapeDtypeStruct((D, B * S), BF16),
        grid_spec=pltpu.PrefetchScalarGridSpec(
            num_scalar_prefetch=1,
            grid=(B, n_tiles, nb),
            in_specs=[pl.BlockSpec((LANES, blk), lambda b, h, i, f: (h, b * nb + i)),
                      pl.BlockSpec((hp, 1, S, LANES), lambda b, h, i, f: (0, b, 0, h)),
                      pl.BlockSpec((LANES, S), lambda b, h, i, f: (n_tiles + h, b)),
                      pl.BlockSpec((1, nb, LANES), lambda b, h, i, f: (b, 0, h)),
                      pl.BlockSpec((hp, blk, blk), lambda b, h, i, f: (h, 0, 0)),
                      pl.BlockSpec((hp, blk, blk), lambda b, h, i, f: (h, 0, 0))],
            out_specs=pl.BlockSpec((LANES, blk), lambda b, h, i, f: (h, b * nb + i)),
            scratch_shapes=[pltpu.VMEM((hp, nb, LANES, blk), BF16),
                            pltpu.VMEM((hp, nb, blk), F32)]),
        compiler_params=_cparams("parallel", "parallel", "arbitrary"),
        name="moba_attention",
    )(b_far2, qv_t, k_aug, qv_t, k_mean, b_own, b_adj)


def _moba_kprep_kernel(k_ref, o_ref, mean_ref):
    k = k_ref[0]
    col = lax.broadcasted_iota(jnp.int32, k.shape, 1)
    lane = col % LANES
    for a in range(HEADS_PER_TILE):
        ones = (lane == _spare(a)) | (lane == _spare(a, 1))
        o_ref[a, 0] = jnp.where((col // HEAD_DIM) % HEADS_PER_TILE == a, k,
                                jnp.where(ones, 1.0, 0.0).astype(BF16))
    mean_ref[0, 0] = jnp.mean(k.astype(F32), axis=0, keepdims=True)


def _moba_kprep(k3d):
    B, S, D = k3d.shape
    blk = MOBA_BLOCK
    nb = S // blk
    k_aug, k_mean = pl.pallas_call(
        _moba_kprep_kernel,
        out_shape=(jax.ShapeDtypeStruct((HEADS_PER_TILE, B, S, D), BF16),
                   jax.ShapeDtypeStruct((B, nb, 1, D), F32)),
        grid=(B, nb),
        in_specs=[pl.BlockSpec((1, blk, D), lambda b, n: (b, n, 0))],
        out_specs=(pl.BlockSpec((HEADS_PER_TILE, 1, blk, D), lambda b, n: (0, b, n, 0)),
                   pl.BlockSpec((1, 1, 1, D), lambda b, n: (b, n, 0, 0))),
        compiler_params=_cparams("parallel", "parallel"),
        name="moba_kprep",
    )(k3d)
    return k_aug, k_mean.reshape(B, nb, D)


def _router_kernel(x_ref, wt_ref, bias_ref, idx_ref, gate_ref, rank_ref, cnt_ref, run_ref, *, tm):
    @pl.when(pl.program_id(0) == 0)
    def _():
        run_ref[...] = jnp.zeros_like(run_ref)

    E = wt_ref.shape[0]
    gsz = E // N_GROUPS
    logits = _nt_dot(wt_ref[...], x_ref[...], precision=HIGHEST)
    s = jax.nn.sigmoid(logits)
    sb = s + bias_ref[...]
    neg_inf = -jnp.inf

    giota = lax.broadcasted_iota(jnp.int32, (gsz, tm), 0)
    gscore = []
    for g in range(N_GROUPS):
        blk = sb[g * gsz:(g + 1) * gsz, :]
        m1 = jnp.max(blk, axis=0, keepdims=True)
        i1 = jnp.min(jnp.where(blk == m1, giota, gsz), axis=0, keepdims=True)
        m2 = jnp.max(jnp.where(giota == i1, neg_inf, blk), axis=0, keepdims=True)
        gscore.append(m1 + m2)

    gsel = [jnp.zeros((1, tm), jnp.bool_) for _ in range(N_GROUPS)]
    for _ in range(TOPK_GROUPS):
        mx = functools.reduce(jnp.maximum, gscore)
        found = jnp.zeros((1, tm), jnp.bool_)
        for g in range(N_GROUPS):
            hit = (gscore[g] == mx) & jnp.logical_not(found)
            gsel[g] = gsel[g] | hit
            found = found | hit
            gscore[g] = jnp.where(hit, neg_inf, gscore[g])
    emask = jnp.concatenate([jnp.broadcast_to(gsel[g], (gsz, tm)) for g in range(N_GROUPS)], axis=0)
    cand = jnp.where(emask, sb, NEG)

    eiota = lax.broadcasted_iota(jnp.int32, (E, tm), 0)
    hits, idxs, ws = [], [], []
    for _ in range(TOP_K):
        mx = jnp.max(cand, axis=0, keepdims=True)
        first = jnp.min(jnp.where(cand == mx, eiota, E), axis=0, keepdims=True)
        hit = eiota == first
        hits.append(hit)
        idxs.append(first)
        ws.append(jnp.sum(jnp.where(hit, s, 0.0), axis=0, keepdims=True))
        cand = jnp.where(hit, neg_inf, cand)
    wsum = functools.reduce(jnp.add, ws)

    chosen = functools.reduce(jnp.logical_or, hits)
    onehot = jnp.where(chosen, 1.0, 0.0)
    tr = lax.broadcasted_iota(jnp.int32, (tm, tm), 0)
    tc = lax.broadcasted_iota(jnp.int32, (tm, tm), 1)
    before = (tr < tc).astype(BF16)
    prior = jnp.dot(onehot.astype(BF16), before, preferred_element_type=F32) + run_ref[...]
    for k in range(TOP_K):
        idx_ref[k:k + 1, :] = idxs[k]
        gate_ref[k:k + 1, :] = ws[k] / wsum * ROUTED_SCALE
        rank_ref[k:k + 1, :] = jnp.sum(jnp.where(hits[k], prior, 0.0), axis=0,
                                       keepdims=True).astype(jnp.int32)
    run_ref[...] = run_ref[...] + jnp.sum(onehot, axis=1, keepdims=True)
    cnt_ref[...] = run_ref[...]


def _router(x2d, w_router, router_bias, tm):
    T, D = x2d.shape
    E = w_router.shape[1]
    tok = lambda i: (0, i)
    fixed = lambda i: (0, 0)
    return pl.pallas_call(
        functools.partial(_router_kernel, tm=tm),
        out_shape=(jax.ShapeDtypeStruct((TOP_K, T), jnp.int32),
                   jax.ShapeDtypeStruct((TOP_K, T), F32),
                   jax.ShapeDtypeStruct((TOP_K, T), jnp.int32),
                   jax.ShapeDtypeStruct((E, 1), F32)),
        grid=(T // tm,),
        in_specs=[pl.BlockSpec((tm, D), lambda i: (i, 0)),
                  pl.BlockSpec((E, D), fixed),
                  pl.BlockSpec((E, 1), fixed)],
        out_specs=(pl.BlockSpec((TOP_K, tm), tok), pl.BlockSpec((TOP_K, tm), tok),
                   pl.BlockSpec((TOP_K, tm), tok), pl.BlockSpec((E, 1), fixed)),
        scratch_shapes=[pltpu.VMEM((E, 1), F32)],
        compiler_params=_cparams("arbitrary"),
        name="router",
    )(x2d, w_router.T, router_bias.reshape(E, 1).astype(F32))


def _row_copy(src_ref, src_row, dst_ref, dst_row, sem):
    return pltpu.make_async_copy(src_ref.at[pl.ds(src_row, 1), :],
                                 dst_ref.at[pl.ds(dst_row, 1), :], sem)


def _dispatch_kernel(dest_ref, x_ref, zero_ref, xs_ref, sem, *, tt):
    del zero_ref

    def issue(r, c):
        for k in range(TOP_K):
            _row_copy(x_ref, r, xs_ref, dest_ref[r * TOP_K + k], sem).start()
        return c

    def drain(r, c):
        for k in range(TOP_K):
            _row_copy(x_ref, 0, xs_ref, 0, sem).wait()
        return c

    lax.fori_loop(0, tt, issue, 0)
    lax.fori_loop(0, tt, drain, 0)


def _dispatch(x2d, dest_flat, n_rows, tt):
    T, D = x2d.shape
    return pl.pallas_call(
        functools.partial(_dispatch_kernel, tt=tt),
        out_shape=jax.ShapeDtypeStruct((n_rows, D), F32),
        grid=(T // tt,),
        in_specs=[pl.BlockSpec((tt * TOP_K,), lambda i: (i,), memory_space=pltpu.SMEM),
                  pl.BlockSpec((tt, D), lambda i: (i, 0)),
                  pl.BlockSpec(memory_space=pl.ANY)],
        out_specs=pl.BlockSpec(memory_space=pl.ANY),
        scratch_shapes=[pltpu.SemaphoreType.DMA],
        input_output_aliases={2: 0},
        compiler_params=_cparams("arbitrary"),
        name="moe_dispatch",
    )(dest_flat, x2d, jnp.zeros((n_rows, D), F32))


def _silu(g):
    return g * jax.nn.sigmoid(g)


def _expert_kernel(blk_e_ref, n_used_ref, x_ref, wg_ref, wu_ref, wd_ref, o_ref):
    del blk_e_ref
    i = pl.program_id(0)

    @pl.when(i < n_used_ref[0])
    def _():
        x = x_ref[...].astype(BF16)
        g = jnp.dot(x, wg_ref[0].astype(BF16), preferred_element_type=F32)
        u = jnp.dot(x, wu_ref[0].astype(BF16), preferred_element_type=F32)
        a = (_silu(g) * u).astype(BF16)
        o_ref[...] = jnp.dot(a, wd_ref[0].astype(BF16), preferred_element_type=F32)

    @pl.when(i >= n_used_ref[0])
    def _():
        o_ref[...] = jnp.zeros_like(o_ref)


def _expert_mlp(xs, blk_e, n_used, w_gate, w_up, w_down):
    n_rows, D = xs.shape
    E, _, F = w_gate.shape
    tm = EXPERT_ROWS
    rows = lambda i, be, nu: (i, 0)
    return pl.pallas_call(
        _expert_kernel,
        out_shape=jax.ShapeDtypeStruct((n_rows, D), F32),
        grid_spec=pltpu.PrefetchScalarGridSpec(
            num_scalar_prefetch=2,
            grid=(n_rows // tm,),
            in_specs=[pl.BlockSpec((tm, D), rows),
                      pl.BlockSpec((1, D, F), lambda i, be, nu: (be[i], 0, 0)),
                      pl.BlockSpec((1, D, F), lambda i, be, nu: (be[i], 0, 0)),
                      pl.BlockSpec((1, F, D), lambda i, be, nu: (be[i], 0, 0))],
            out_specs=pl.BlockSpec((tm, D), rows)),
        compiler_params=_cparams("arbitrary"),
        name="moe_experts",
    )(blk_e, n_used, xs, w_gate, w_up, w_down)


def _combine_kernel(dest_ref, x_ref, gate_ref, wsg_ref, wsu_ref, wsd_ref, g_ref, b_ref, ys_ref,
                    o_ref, rows_ref, sem, *, tt, alpha):
    def issue(r, c):
        for k in range(TOP_K):
            _row_copy(ys_ref, dest_ref[r * TOP_K + k], rows_ref.at[k], r, sem).start()
        return c

    def drain(r, c):
        for k in range(TOP_K):
            _row_copy(ys_ref, 0, rows_ref.at[0], 0, sem).wait()
        return c

    lax.fori_loop(0, tt, issue, 0)
    x = x_ref[...]
    xb = x.astype(BF16)
    g = jnp.dot(xb, wsg_ref[...], preferred_element_type=F32)
    u = jnp.dot(xb, wsu_ref[...], preferred_element_type=F32)
    y = jnp.dot((_silu(g) * u).astype(BF16), wsd_ref[...], preferred_element_type=F32)
    lax.fori_loop(0, tt, drain, 0)
    gates = gate_ref[...]
    for k in range(TOP_K):
        y = y + rows_ref[k] * gates[:, k:k + 1]
    o_ref[...] = _layer_norm(alpha * x + y, g_ref[...], b_ref[...])


def _combine(x2d, ys, dest_flat, gates_tk, ws_gate, ws_up, ws_down, g, b, alpha, tt):
    T, D = x2d.shape
    F = ws_gate.shape[1]
    row = lambda i: (i, 0)
    fixed = lambda i: (0, 0)
    return pl.pallas_call(
        functools.partial(_combine_kernel, tt=tt, alpha=alpha),
        out_shape=jax.ShapeDtypeStruct((T, D), F32),
        grid=(T // tt,),
        in_specs=[pl.BlockSpec((tt * TOP_K,), lambda i: (i,), memory_space=pltpu.SMEM),
                  pl.BlockSpec((tt, D), row),
                  pl.BlockSpec((tt, TOP_K), row),
                  pl.BlockSpec((D, F), fixed), pl.BlockSpec((D, F), fixed),
                  pl.BlockSpec((F, D), fixed),
                  pl.BlockSpec((1, D), fixed), pl.BlockSpec((1, D), fixed),
                  pl.BlockSpec(memory_space=pl.ANY)],
        out_specs=pl.BlockSpec((tt, D), row),
        scratch_shapes=[pltpu.VMEM((TOP_K, tt, D), F32), pltpu.SemaphoreType.DMA],
        compiler_params=_cparams("arbitrary"),
        name="moe_combine",
    )(dest_flat, x2d, gates_tk, ws_gate.astype(BF16), ws_up.astype(BF16), ws_down.astype(BF16),
      g.reshape(1, D), b.reshape(1, D), ys)


def _pick(n, pref):
    t = min(n, pref)
    while n % t:
        t //= 2
    return t


def _moe_layer(x2d, w_router, router_bias, w_gate, w_up, w_down, ws_gate, ws_up, ws_down,
               g, b, alpha):
    T, D = x2d.shape
    E = w_router.shape[1]
    idx, gates, rank, counts = _router(x2d, w_router, router_bias, _pick(T, 256))
    counts = counts[:, 0].astype(jnp.int32)
    padded = (counts + EXPERT_ROWS - 1) // EXPERT_ROWS * EXPERT_ROWS
    seg_end = jnp.cumsum(padded)
    seg_start = seg_end - padded
    n_blocks = (T * TOP_K + E * (EXPERT_ROWS - 1)) // EXPERT_ROWS
    dest = (seg_start[idx] + rank).T.reshape(-1)
    blk_e = jnp.minimum(jnp.searchsorted(seg_end, jnp.arange(n_blocks) * EXPERT_ROWS, side='right'),
                        E - 1).astype(jnp.int32)
    n_used = (seg_end[-1:] // EXPERT_ROWS).astype(jnp.int32)
    tt = _pick(T, 128)
    xs = _dispatch(x2d, dest, n_blocks * EXPERT_ROWS, tt)
    ys = _expert_mlp(xs, blk_e, n_used, w_gate, w_up, w_down)
    return _combine(x2d, ys, dest, gates.T, ws_gate, ws_up, ws_down, g, b, alpha, tt)


def _mixer_layer(x2d, i, B, S, p, alpha):
    T, D = x2d.shape
    tm = _pick(T, 512)
    tn = _pick(D, 1024)
    j = i // 2
    w_in = p["moba_w_in"][j] if i % 2 == 0 else p["fox_w_in"][j]
    w_qv_t = jnp.concatenate([w_in[:, :D].T * HEAD_DIM ** -0.5, w_in[:, 2 * D:3 * D].T]).astype(BF16)
    qv_t = _project_t(x2d, w_qv_t, tm, tn)
    k = _project(x2d, w_in[:, D:2 * D].astype(BF16), tm, tn)
    if i % 2 == 0:
        k_aug, k_mean = _moba_kprep(k.reshape(B, S, D))
        attn_t = _moba_attention(qv_t, k_aug, k_mean, p["rel_bias"], B, S)
        w_out = p["moba_w_out"][j]
    else:
        c_tok = _fox_gates(x2d.reshape(B, S, D), w_in[:, 3 * D:], p["fox_b_f"][j], _pick(S, 512))
        k_aug = _fox_kprep(k, c_tok.reshape(T, LANES), tm).reshape(HEADS_PER_TILE, B, S, D)
        attn_t = _fox_attention(qv_t, k_aug, B, S, _pick(S, 256))
        w_out = p["fox_w_out"][j]
    return _outproj_ln(attn_t, w_out.astype(BF16), x2d, p["ln1_g"][i], p["ln1_b"][i], alpha, tm)


def kernel(x, rel_bias, moba_w_in, moba_w_out, fox_w_in, fox_b_f, fox_w_out, ln1_g, ln1_b, ln2_g,
           ln2_b, w_router, router_bias, w_gate, w_up, w_down, ws_gate, ws_up, ws_down):
    B, S, D = x.shape
    depth = ln1_g.shape[0]
    alpha = (2 * depth) ** 0.25
    p = dict(rel_bias=rel_bias, moba_w_in=moba_w_in, moba_w_out=moba_w_out, fox_w_in=fox_w_in,
             fox_b_f=fox_b_f, fox_w_out=fox_w_out, ln1_g=ln1_g, ln1_b=ln1_b)
    x2d = x.reshape(B * S, D)
    for i in range(depth):
        x2d = _mixer_layer(x2d, i, B, S, p, alpha)
        x2d = _moe_layer(x2d, w_router[i], router_bias[i], w_gate[i], w_up[i], w_down[i],
                         ws_gate[i], ws_up[i], ws_down[i], ln2_g[i], ln2_b[i], alpha)
    return x2d.reshape(B, S, D)
```

```python
import functools
import math

import jax
import jax.numpy as jnp
from jax import lax
from jax.experimental import pallas as pl
from jax.experimental.pallas import tpu as pltpu

F32 = jnp.float32
BF16 = jnp.bfloat16
HIGHEST = lax.Precision.HIGHEST

HEAD_DIM = 64
MOBA_BLOCK = 256
MOBA_TOPK = 3
REL_MAX_DIST = 128
TOP_K = 8
N_GROUPS = 8
TOPK_GROUPS = 4
ROUTED_SCALE = 2.5
LN_EPS = 1e-5
NEG = -1e30

LANES = 128
HEADS_PER_TILE = LANES // HEAD_DIM
EXPERT_ROWS = 256
VMEM_LIMIT = 48 * 1024 * 1024


def _cparams(*sem):
    return pltpu.CompilerParams(dimension_semantics=sem, vmem_limit_bytes=VMEM_LIMIT)


def _nt_dot(a, b, **kw):
    return lax.dot_general(a, b, (((1,), (1,)), ((), ())), preferred_element_type=F32, **kw)


def _proj_kernel(x_ref, w_ref, o_ref):
    o_ref[...] = jnp.dot(x_ref[...].astype(BF16), w_ref[...],
                         preferred_element_type=F32).astype(o_ref.dtype)


def _project(x2d, w, tm, tn):
    T, K = x2d.shape
    N = w.shape[1]
    return pl.pallas_call(
        _proj_kernel,
        out_shape=jax.ShapeDtypeStruct((T, N), BF16),
        grid=(T // tm, N // tn),
        in_specs=[pl.BlockSpec((tm, K), lambda i, j: (i, 0)),
                  pl.BlockSpec((K, tn), lambda i, j: (0, j))],
        out_specs=pl.BlockSpec((tm, tn), lambda i, j: (i, j)),
        compiler_params=_cparams("parallel", "arbitrary"),
        name="k_proj",
    )(x2d, w)


def _proj_t_kernel(x_ref, wt_ref, o_ref):
    o_ref[...] = _nt_dot(wt_ref[...], x_ref[...].astype(BF16)).astype(o_ref.dtype)


def _project_t(x2d, wt, tm, tn):
    T, K = x2d.shape
    N = wt.shape[0]
    return pl.pallas_call(
        _proj_t_kernel,
        out_shape=jax.ShapeDtypeStruct((N, T), BF16),
        grid=(T // tm, N // tn),
        in_specs=[pl.BlockSpec((tm, K), lambda i, j: (i, 0)),
                  pl.BlockSpec((tn, K), lambda i, j: (j, 0))],
        out_specs=pl.BlockSpec((tn, tm), lambda i, j: (j, i)),
        compiler_params=_cparams("parallel", "arbitrary"),
        name="qv_proj_t",
    )(x2d, wt)


def _layer_norm(r, g, b):
    mu = jnp.mean(r, axis=-1, keepdims=True)
    d = r - mu
    var = jnp.mean(d * d, axis=-1, keepdims=True)
    return d * lax.rsqrt(var + LN_EPS) * g + b


def _outproj_ln_kernel(at_ref, w_ref, x_ref, g_ref, b_ref, o_ref, *, alpha):
    y = lax.dot_general(at_ref[...], w_ref[...], (((0,), (0,)), ((), ())),
                        preferred_element_type=F32)
    o_ref[...] = _layer_norm(alpha * x_ref[...] + y, g_ref[...], b_ref[...])


def _outproj_ln(attn_t, w_out, x2d, g, b, alpha, tm):
    T, D = x2d.shape
    row = lambda i: (i, 0)
    fixed = lambda i: (0, 0)
    return pl.pallas_call(
        functools.partial(_outproj_ln_kernel, alpha=alpha),
        out_shape=jax.ShapeDtypeStruct((T, D), F32),
        grid=(T // tm,),
        in_specs=[pl.BlockSpec((D, tm), lambda i: (0, i)), pl.BlockSpec((D, D), fixed),
                  pl.BlockSpec((tm, D), row), pl.BlockSpec((1, D), fixed),
                  pl.BlockSpec((1, D), fixed)],
        out_specs=pl.BlockSpec((tm, D), row),
        compiler_params=_cparams("parallel"),
        name="outproj_ln",
    )(attn_t, w_out, x2d, g.reshape(1, D), b.reshape(1, D))


def _spare(a, n=0):
    return ((a + 1) % HEADS_PER_TILE) * HEAD_DIM + n


def _attend(carries, chains):
    scores = []
    for blocks in chains:
        parts = []
        for kt, qt, _, extra, mask in blocks:
            s = jnp.dot(kt, qt, preferred_element_type=F32)
            if extra is not None:
                s = s + extra
            if mask is not None:
                s = jnp.where(mask, s, NEG)
            parts.append(s)
        scores.append(parts)
    probs = []
    for (m, _), parts in zip(carries, scores):
        tops = [jnp.max(s, axis=0, keepdims=True) for s in parts]
        m_new = functools.reduce(jnp.maximum, tops, m)
        probs.append((m_new, jnp.exp(m - m_new), [jnp.exp(s - m_new).astype(BF16) for s in parts]))
    out = []
    for (_, acc), (m_new, decay, ps), blocks in zip(carries, probs, chains):
        acc = decay * acc
        for (_, _, vt, _, _), p in zip(blocks, ps):
            acc = acc + jnp.dot(vt, p, preferred_element_type=F32)
        out.append((m_new, acc))
    return tuple(out)


def _scores_init(tq):
    return jnp.full((1, tq), -jnp.inf, F32), jnp.zeros((LANES, tq), F32)


def _finish(carries, o_ref):
    row = lax.broadcasted_iota(jnp.int32, carries[0][1].shape, 0)
    outs = []
    for a, (_, acc) in enumerate(carries):
        s = _spare(a)
        outs.append(acc / acc[s:s + 1, :])
    o_ref[...] = jnp.where(row < HEAD_DIM, outs[0], outs[1]).astype(o_ref.dtype)


def _fill_v_aug(vt_ref, vaug_ref, tk):
    n_tiles = vt_ref.shape[1] // tk
    row = lax.broadcasted_iota(jnp.int32, (LANES, tk), 0)
    for n in range(n_tiles):
        v = vt_ref[:, n * tk:(n + 1) * tk]
        for a in range(HEADS_PER_TILE):
            vaug_ref[a, n] = jnp.where(row // HEAD_DIM == a, v, jnp.ones_like(v))


FOX_SPLIT = 3


def _fox_kernel(qt_ref, k_ref, vt_ref, o_ref, vaug_ref, *, tq):
    i = pl.program_id(2)

    @pl.when(i == 0)
    def _():
        _fill_v_aug(vt_ref, vaug_ref, tq)

    row = lax.broadcasted_iota(jnp.int32, (LANES, tq), 0)
    qt = qt_ref[...]
    queries = []
    for a in range(HEADS_PER_TILE):
        offs = (row >= _spare(a)) & (row < _spare(a, FOX_SPLIT))
        queries.append(jnp.where(row // HEAD_DIM == a, qt,
                                 jnp.where(offs, -1.0, 0.0).astype(BF16)))
    key = lax.broadcasted_iota(jnp.int32, (tq, tq), 0)
    qry = lax.broadcasted_iota(jnp.int32, (tq, tq), 1)
    causal = key <= qry

    def block(a, j, mask=None):
        start = pl.multiple_of(j * tq, tq)
        return k_ref[a, 0, pl.ds(start, tq), :], queries[a], vaug_ref[a, j], None, mask

    heads = range(HEADS_PER_TILE)
    carries = lax.cond(
        i % 2 == 1,
        lambda c: _attend(c, [[block(a, i - 1), block(a, i, causal)] for a in heads]),
        lambda c: _attend(c, [[block(a, i, causal)] for a in heads]),
        tuple(_scores_init(tq) for _ in heads))
    carries = lax.fori_loop(
        0, i // 2,
        lambda j, c: _attend(c, [[block(a, 2 * j), block(a, 2 * j + 1)] for a in heads]),
        carries)
    _finish(carries, o_ref)


def _fox_attention(qv_t, k_aug, B, S, tq):
    D = k_aug.shape[-1]
    n_tiles = D // LANES
    nq = S // tq
    return pl.pallas_call(
        functools.partial(_fox_kernel, tq=tq),
        out_shape=jax.ShapeDtypeStruct((D, B * S), BF16),
        grid=(B, n_tiles, nq),
        in_specs=[pl.BlockSpec((LANES, tq), lambda b, h, i: (h, b * nq + i)),
                  pl.BlockSpec((HEADS_PER_TILE, 1, S, LANES), lambda b, h, i: (0, b, 0, h)),
                  pl.BlockSpec((LANES, S), lambda b, h, i: (n_tiles + h, b))],
        out_specs=pl.BlockSpec((LANES, tq), lambda b, h, i: (h, b * nq + i)),
        scratch_shapes=[pltpu.VMEM((HEADS_PER_TILE, nq, LANES, tq), BF16)],
        compiler_params=_cparams("parallel", "parallel", "arbitrary"),
        name="fox_attention",
    )(qv_t, k_aug, qv_t)


def _log_sigmoid(z):
    return jnp.minimum(z, 0.0) - jnp.log1p(jnp.exp(-jnp.abs(z)))


def _fox_gates_kernel(x_ref, w_ref, b_ref, o_ref, carry_ref, *, ts):
    @pl.when(pl.program_id(1) == 0)
    def _():
        carry_ref[...] = jnp.zeros_like(carry_ref)

    z = jnp.dot(x_ref[0], w_ref[...], preferred_element_type=F32, precision=HIGHEST)
    lf = _log_sigmoid(z + b_ref[...])
    r = lax.broadcasted_iota(jnp.int32, (ts, ts), 0)
    c = lax.broadcasted_iota(jnp.int32, (ts, ts), 1)
    tri = (c <= r).astype(F32)
    cum = jnp.dot(tri, lf, preferred_element_type=F32, precision=HIGHEST) + carry_ref[...]
    carry_ref[...] = cum[ts - 1:ts, :]
    o_ref[0] = cum


def _fox_gates(x3d, w_f, b_f, ts):
    B, S, D = x3d.shape
    H = w_f.shape[1]
    w_pad = jnp.zeros((D, LANES), F32).at[:, :H].set(w_f)
    b_pad = jnp.zeros((1, LANES), F32).at[0, :H].set(b_f)
    return pl.pallas_call(
        functools.partial(_fox_gates_kernel, ts=ts),
        out_shape=jax.ShapeDtypeStruct((B, S, LANES), F32),
        grid=(B, S // ts),
        in_specs=[pl.BlockSpec((1, ts, D), lambda b, s: (b, s, 0)),
                  pl.BlockSpec((D, LANES), lambda b, s: (0, 0)),
                  pl.BlockSpec((1, LANES), lambda b, s: (0, 0))],
        out_specs=pl.BlockSpec((1, ts, LANES), lambda b, s: (b, s, 0)),
        scratch_shapes=[pltpu.VMEM((1, LANES), F32)],
        compiler_params=_cparams("parallel", "arbitrary"),
        name="fox_gates",
    )(x3d, w_pad, b_pad)


def _fox_kprep_kernel(k_ref, c_ref, o_ref):
    tm, D = k_ref.shape
    lane = lax.broadcasted_iota(jnp.int32, (tm, LANES), 1)
    for t in range(D // LANES):
        k = k_ref[:, t * LANES:(t + 1) * LANES].astype(F32)
        for a in range(HEADS_PER_TILE):
            h = t * HEADS_PER_TILE + a
            rest = c_ref[:, h:h + 1]
            aug = jnp.zeros((tm, LANES), F32)
            for n in range(FOX_SPLIT):
                piece = rest.astype(BF16).astype(F32)
                rest = rest - piece
                aug = jnp.where(lane == _spare(a, n), piece, aug)
            o_ref[a, :, t * LANES:(t + 1) * LANES] = jnp.where(lane // HEAD_DIM == a, k,
                                                               aug).astype(BF16)


def _fox_kprep(k2d, c_tok, tm):
    T, D = k2d.shape
    return pl.pallas_call(
        _fox_kprep_kernel,
        out_shape=jax.ShapeDtypeStruct((HEADS_PER_TILE, T, D), BF16),
        grid=(T // tm,),
        in_specs=[pl.BlockSpec((tm, D), lambda i: (i, 0)),
                  pl.BlockSpec((tm, LANES), lambda i: (i, 0))],
        out_specs=pl.BlockSpec((HEADS_PER_TILE, tm, D), lambda i: (0, i, 0)),
        compiler_params=_cparams("parallel"),
        name="fox_kprep",
    )(k2d, c_tok)


MOBA_SLAB = 16


def _moba_kernel(bfar_ref, qt_ref, k_ref, vt_ref, kmean_ref, bown_ref, badj_ref, o_ref,
                 vaug_ref, sel_ref, *, blk):
    t = pl.program_id(1)
    i = pl.program_id(2)
    nb = kmean_ref.shape[1]

    @pl.when(i == 0)
    def _():
        _fill_v_aug(vt_ref, vaug_ref, blk)

    row = lax.broadcasted_iota(jnp.int32, (LANES, blk), 0)
    brow = lax.broadcasted_iota(jnp.int32, (nb, blk), 0)
    qt = qt_ref[...]
    zero = jnp.zeros_like(qt)
    base, slabs = [], []
    for a in range(HEADS_PER_TILE):
        qa = jnp.where(row // HEAD_DIM == a, qt, zero)
        base.append(qa)
        bscore = jnp.dot(kmean_ref[0], qa.astype(F32), preferred_element_type=F32,
                         precision=HIGHEST)
        cand = jnp.where(brow < i, bscore, NEG)
        sel = jnp.zeros((nb, blk), F32)
        for _ in range(MOBA_TOPK):
            mx = jnp.max(cand, axis=0, keepdims=True)
            first = jnp.min(jnp.where(cand == mx, brow, nb), axis=0, keepdims=True)
            hit = brow == first
            sel = jnp.where(hit, 1.0, sel)
            cand = jnp.where(hit, -jnp.inf, cand)
        sel_ref[a] = jnp.where(brow < i, sel, 0.0)
        slabs.append(_spare(a) // MOBA_SLAB * MOBA_SLAB)

    srow = lax.broadcasted_iota(jnp.int32, (MOBA_SLAB, blk), 0)

    def query(a, j, bias_hi, bias_lo):
        s0 = slabs[a]
        chosen = sel_ref[a, pl.ds(j, 1), :] > 0.0
        slab = jnp.where(srow == _spare(a) - s0, jnp.where(chosen, bias_hi, NEG),
                         jnp.where(srow == _spare(a, 1) - s0, bias_lo, 0.0)).astype(BF16)
        head = [base[a][:s0]] if s0 else []
        tail = [base[a][s0 + MOBA_SLAB:]] if s0 + MOBA_SLAB < LANES else []
        return jnp.concatenate(head + [slab] + tail, axis=0)

    key = lax.broadcasted_iota(jnp.int32, (blk, blk), 0)
    qry = lax.broadcasted_iota(jnp.int32, (blk, blk), 1)
    causal = key <= qry
    heads = range(HEADS_PER_TILE)

    def keys(a, j):
        start = pl.multiple_of(j * blk, blk)
        return k_ref[a, 0, pl.ds(start, blk), :]

    def own(a):
        return keys(a, i), base[a], vaug_ref[a, i], bown_ref[a], causal

    def prev(a):
        return keys(a, i - 1), query(a, i - 1, 0.0, 0.0), vaug_ref[a, i - 1], badj_ref[a], None

    def far(a, j):
        h = t * HEADS_PER_TILE + a
        return keys(a, j), query(a, j, bfar_ref[0, h], bfar_ref[1, h]), vaug_ref[a, j], None, None

    carries = lax.cond(
        i >= 1,
        lambda c: _attend(c, [[prev(a), own(a)] for a in heads]),
        lambda c: _attend(c, [[own(a)] for a in heads]),
        tuple(_scores_init(blk) for _ in heads))
    carries = lax.cond(
        (i >= 2) & (i % 2 == 0),
        lambda c: _attend(c, [[far(a, i - 2)] for a in heads]),
        lambda c: c, carries)
    carries = lax.fori_loop(
        0, (i - 1) // 2,
        lambda j, c: _attend(c, [[far(a, 2 * j), far(a, 2 * j + 1)] for a in heads]),
        carries)
    _finish(carries, o_ref)


def _rel_bucket(dist, n_buckets):
    n = jnp.maximum(dist, 0)
    max_exact = n_buckets // 2
    nf = jnp.maximum(n, 1).astype(F32)
    large = max_exact + (jnp.log(nf / max_exact) / math.log(REL_MAX_DIST / max_exact)
                         * (n_buckets - max_exact)).astype(jnp.int32)
    large = jnp.minimum(large, n_buckets - 1)
    return jnp.where(n < max_exact, n, large)


def _moba_attention(qv_t, k_aug, k_mean, rel_bias, B, S):
    D = k_aug.shape[-1]
    n_tiles = D // LANES
    blk = MOBA_BLOCK
    nb = S // blk
    hp = HEADS_PER_TILE
    r = jnp.arange(blk)
    delta = r[None, :] - r[:, None]
    bias_t = rel_bias.T.astype(F32)
    n_buckets = rel_bias.shape[0]
    b_own = bias_t[:, _rel_bucket(delta, n_buckets)]
    b_adj = bias_t[:, _rel_bucket(delta + blk, n_buckets)]
    b_far = bias_t[:, n_buckets - 1]
    far_hi = b_far.astype(BF16).astype(F32)
    far_lo = (b_far - far_hi).astype(BF16).astype(F32)
    b_far2 = jnp.stack([far_hi, far_lo])
    return pl.pallas_call(
        functools.partial(_moba_kernel, blk=blk),
        out_shape=jax.ShapeDtypeStruct((D, B * S), BF16),
        grid_spec=pltpu.PrefetchScalarGridSpec(
            num_scalar_prefetch=1,
            grid=(B, n_tiles, nb),
            in_specs=[pl.BlockSpec((LANES, blk), lambda b, h, i, f: (h, b * nb + i)),
                      pl.BlockSpec((hp, 1, S, LANES), lambda b, h, i, f: (0, b, 0, h)),
                      pl.BlockSpec((LANES, S), lambda b, h, i, f: (n_tiles + h, b)),
                      pl.BlockSpec((1, nb, LANES), lambda b, h, i, f: (b, 0, h)),
                      pl.BlockSpec((hp, blk, blk), lambda b, h, i, f: (h, 0, 0)),
                      pl.BlockSpec((hp, blk, blk), lambda b, h, i, f: (h, 0, 0))],
            out_specs=pl.BlockSpec((LANES, blk), lambda b, h, i, f: (h, b * nb + i)),
            scratch_shapes=[pltpu.VMEM((hp, nb, LANES, blk), BF16),
                            pltpu.VMEM((hp, nb, blk), F32)]),
        compiler_params=_cparams("parallel", "parallel", "arbitrary"),
        name="moba_attention",
    )(b_far2, qv_t, k_aug, qv_t, k_mean, b_own, b_adj)


def _moba_kprep_kernel(k_ref, o_ref, mean_ref):
    k = k_ref[0]
    col = lax.broadcasted_iota(jnp.int32, k.shape, 1)
    lane = col % LANES
    for a in range(HEADS_PER_TILE):
        ones = (lane == _spare(a)) | (lane == _spare(a, 1))
        o_ref[a, 0] = jnp.where((col // HEAD_DIM) % HEADS_PER_TILE == a, k,
                                jnp.where(ones, 1.0, 0.0).astype(BF16))
    mean_ref[0, 0] = jnp.mean(k.astype(F32), axis=0, keepdims=True)


def _moba_kprep(k3d):
    B, S, D = k3d.shape
    blk = MOBA_BLOCK
    nb = S // blk
    k_aug, k_mean = pl.pallas_call(
        _moba_kprep_kernel,
        out_shape=(jax.ShapeDtypeStruct((HEADS_PER_TILE, B, S, D), BF16),
                   jax.ShapeDtypeStruct((B, nb, 1, D), F32)),
        grid=(B, nb),
        in_specs=[pl.BlockSpec((1, blk, D), lambda b, n: (b, n, 0))],
        out_specs=(pl.BlockSpec((HEADS_PER_TILE, 1, blk, D), lambda b, n: (0, b, n, 0)),
                   pl.BlockSpec((1, 1, 1, D), lambda b, n: (b, n, 0, 0))),
        compiler_params=_cparams("parallel", "parallel"),
        name="moba_kprep",
    )(k3d)
    return k_aug, k_mean.reshape(B, nb, D)


def _router_kernel(x_ref, wt_ref, bias_ref, idx_ref, gate_ref, rank_ref, cnt_ref, run_ref, *, tm):
    @pl.when(pl.program_id(0) == 0)
    def _():
        run_ref[...] = jnp.zeros_like(run_ref)

    E = wt_ref.shape[0]
    gsz = E // N_GROUPS
    logits = _nt_dot(wt_ref[...], x_ref[...], precision=HIGHEST)
    s = jax.nn.sigmoid(logits)
    sb = s + bias_ref[...]
    neg_inf = -jnp.inf

    giota = lax.broadcasted_iota(jnp.int32, (gsz, tm), 0)
    gscore = []
    for g in range(N_GROUPS):
        blk = sb[g * gsz:(g + 1) * gsz, :]
        m1 = jnp.max(blk, axis=0, keepdims=True)
        i1 = jnp.min(jnp.where(blk == m1, giota, gsz), axis=0, keepdims=True)
        m2 = jnp.max(jnp.where(giota == i1, neg_inf, blk), axis=0, keepdims=True)
        gscore.append(m1 + m2)

    gsel = [jnp.zeros((1, tm), jnp.bool_) for _ in range(N_GROUPS)]
    for _ in range(TOPK_GROUPS):
        mx = functools.reduce(jnp.maximum, gscore)
        found = jnp.zeros((1, tm), jnp.bool_)
        for g in range(N_GROUPS):
            hit = (gscore[g] == mx) & jnp.logical_not(found)
            gsel[g] = gsel[g] | hit
            found = found | hit
            gscore[g] = jnp.where(hit, neg_inf, gscore[g])
    emask = jnp.concatenate([jnp.broadcast_to(gsel[g], (gsz, tm)) for g in range(N_GROUPS)], axis=0)
    cand = jnp.where(emask, sb, NEG)

    eiota = lax.broadcasted_iota(jnp.int32, (E, tm), 0)
    hits, idxs, ws = [], [], []
    for _ in range(TOP_K):
        mx = jnp.max(cand, axis=0, keepdims=True)
        first = jnp.min(jnp.where(cand == mx, eiota, E), axis=0, keepdims=True)
        hit = eiota == first
        hits.append(hit)
        idxs.append(first)
        ws.append(jnp.sum(jnp.where(hit, s, 0.0), axis=0, keepdims=True))
        cand = jnp.where(hit, neg_inf, cand)
    wsum = functools.reduce(jnp.add, ws)

    chosen = functools.reduce(jnp.logical_or, hits)
    onehot = jnp.where(chosen, 1.0, 0.0)
    tr = lax.broadcasted_iota(jnp.int32, (tm, tm), 0)
    tc = lax.broadcasted_iota(jnp.int32, (tm, tm), 1)
    before = (tr < tc).astype(BF16)
    prior = jnp.dot(onehot.astype(BF16), before, preferred_element_type=F32) + run_ref[...]
    grow = lax.broadcasted_iota(jnp.int32, (LANES, tm), 0)
    gates = jnp.zeros((LANES, tm), F32)
    for k in range(TOP_K):
        idx_ref[k:k + 1, :] = idxs[k]
        gates = jnp.where(grow == k, ws[k] / wsum * ROUTED_SCALE, gates)
        rank_ref[k:k + 1, :] = jnp.sum(jnp.where(hits[k], prior, 0.0), axis=0,
                                       keepdims=True).astype(jnp.int32)
    gate_ref[...] = gates.T
    run_ref[...] = run_ref[...] + jnp.sum(onehot, axis=1, keepdims=True)
    cnt_ref[...] = run_ref[...]


def _router(x2d, w_router, router_bias, tm):
    T, D = x2d.shape
    E = w_router.shape[1]
    tok = lambda i: (0, i)
    fixed = lambda i: (0, 0)
    return pl.pallas_call(
        functools.partial(_router_kernel, tm=tm),
        out_shape=(jax.ShapeDtypeStruct((TOP_K, T), jnp.int32),
                   jax.ShapeDtypeStruct((T, LANES), F32),
                   jax.ShapeDtypeStruct((TOP_K, T), jnp.int32),
                   jax.ShapeDtypeStruct((E, 1), F32)),
        grid=(T // tm,),
        in_specs=[pl.BlockSpec((tm, D), lambda i: (i, 0)),
                  pl.BlockSpec((E, D), fixed),
                  pl.BlockSpec((E, 1), fixed)],
        out_specs=(pl.BlockSpec((TOP_K, tm), tok), pl.BlockSpec((tm, LANES), lambda i: (i, 0)),
                   pl.BlockSpec((TOP_K, tm), tok), pl.BlockSpec((E, 1), fixed)),
        scratch_shapes=[pltpu.VMEM((E, 1), F32)],
        compiler_params=_cparams("arbitrary"),
        name="router",
    )(x2d, w_router.T, router_bias.reshape(E, 1).astype(F32))


def _slots_kernel(idx_ref, rank_ref, start_ref, o_ref):
    E = start_ref.shape[0]
    tm = idx_ref.shape[1]
    eiota = lax.broadcasted_iota(jnp.int32, (E, tm), 0)
    start = start_ref[...]
    for k in range(TOP_K):
        base = jnp.sum(jnp.where(eiota == idx_ref[k:k + 1, :], start, 0.0), axis=0, keepdims=True)
        o_ref[k:k + 1, :] = base.astype(jnp.int32) + rank_ref[k:k + 1, :]


def _slots(idx, rank, seg_start, tm):
    T = idx.shape[1]
    E = seg_start.shape[0]
    tok = lambda i: (0, i)
    return pl.pallas_call(
        _slots_kernel,
        out_shape=jax.ShapeDtypeStruct((TOP_K, T), jnp.int32),
        grid=(T // tm,),
        in_specs=[pl.BlockSpec((TOP_K, tm), tok), pl.BlockSpec((TOP_K, tm), tok),
                  pl.BlockSpec((E, 1), lambda i: (0, 0))],
        out_specs=pl.BlockSpec((TOP_K, tm), tok),
        compiler_params=_cparams("parallel"),
        name="moe_slots",
    )(idx, rank, seg_start.astype(F32).reshape(E, 1))


def _row_copy(src_ref, src_row, dst_ref, dst_row, sem):
    return pltpu.make_async_copy(src_ref.at[pl.ds(src_row, 1), :],
                                 dst_ref.at[pl.ds(dst_row, 1), :], sem)


def _dispatch_kernel(seg_ref, dest_ref, x_ref, xs_ref, zero_ref, zsem, sem, *, tt):
    E = seg_ref.shape[1]

    def zero_copy(e):
        start = pl.multiple_of(seg_ref[1, e] - EXPERT_ROWS, EXPERT_ROWS)
        return pltpu.make_async_copy(zero_ref, xs_ref.at[pl.ds(start, EXPERT_ROWS), :], zsem)

    @pl.when(pl.program_id(0) == 0)
    def _():
        zero_ref[...] = jnp.zeros_like(zero_ref)
        for go in (lambda c: c.start(), lambda c: c.wait()):
            def body(e, carry, go=go):
                @pl.when(seg_ref[1, e] > seg_ref[0, e])
                def _():
                    go(zero_copy(e))
                return carry
            lax.fori_loop(0, E, body, 0)

    def issue(r, c):
        for k in range(TOP_K):
            _row_copy(x_ref, r, xs_ref, dest_ref[k, r], sem).start()
        return c

    def drain(r, c):
        for k in range(TOP_K):
            _row_copy(x_ref, 0, xs_ref, 0, sem).wait()
        return c

    lax.fori_loop(0, tt, issue, 0)
    lax.fori_loop(0, tt, drain, 0)


def _dispatch(x2d, dest, seg, n_rows, tt):
    T, D = x2d.shape
    return pl.pallas_call(
        functools.partial(_dispatch_kernel, tt=tt),
        out_shape=jax.ShapeDtypeStruct((n_rows, D), F32),
        grid_spec=pltpu.PrefetchScalarGridSpec(
            num_scalar_prefetch=1,
            grid=(T // tt,),
            in_specs=[pl.BlockSpec((TOP_K, tt), lambda i, seg: (0, i), memory_space=pltpu.SMEM),
                      pl.BlockSpec((tt, D), lambda i, seg: (i, 0))],
            out_specs=pl.BlockSpec(memory_space=pl.ANY),
            scratch_shapes=[pltpu.VMEM((EXPERT_ROWS, D), F32), pltpu.SemaphoreType.DMA,
                            pltpu.SemaphoreType.DMA]),
        compiler_params=_cparams("arbitrary"),
        name="moe_dispatch",
    )(seg, dest, x2d)


def _silu(g):
    return g * jax.nn.sigmoid(g)


def _expert_kernel(blk_e_ref, n_used_ref, x_ref, wg_ref, wu_ref, wd_ref, o_ref):
    del blk_e_ref
    i = pl.program_id(0)

    @pl.when(i < n_used_ref[0])
    def _():
        x = x_ref[...].astype(BF16)
        g = jnp.dot(x, wg_ref[0].astype(BF16), preferred_element_type=F32)
        u = jnp.dot(x, wu_ref[0].astype(BF16), preferred_element_type=F32)
        a = (_silu(g) * u).astype(BF16)
        o_ref[...] = jnp.dot(a, wd_ref[0].astype(BF16), preferred_element_type=F32)

    @pl.when(i >= n_used_ref[0])
    def _():
        o_ref[...] = jnp.zeros_like(o_ref)


def _expert_mlp(xs, blk_e, n_used, w_gate, w_up, w_down):
    n_rows, D = xs.shape
    E, _, F = w_gate.shape
    tm = EXPERT_ROWS
    rows = lambda i, be, nu: (i, 0)
    return pl.pallas_call(
        _expert_kernel,
        out_shape=jax.ShapeDtypeStruct((n_rows, D), F32),
        grid_spec=pltpu.PrefetchScalarGridSpec(
            num_scalar_prefetch=2,
            grid=(n_rows // tm,),
            in_specs=[pl.BlockSpec((tm, D), lambda i, be, nu: (jnp.minimum(i, nu[0] - 1), 0)),
                      pl.BlockSpec((1, D, F), lambda i, be, nu: (be[i], 0, 0)),
                      pl.BlockSpec((1, D, F), lambda i, be, nu: (be[i], 0, 0)),
                      pl.BlockSpec((1, F, D), lambda i, be, nu: (be[i], 0, 0))],
            out_specs=pl.BlockSpec((tm, D), rows)),
        compiler_params=_cparams("arbitrary"),
        name="moe_experts",
    )(blk_e, n_used, xs, w_gate, w_up, w_down)


def _combine_kernel(dest_ref, x_ref, gate_ref, wsg_ref, wsu_ref, wsd_ref, g_ref, b_ref, ys_ref,
                    o_ref, rows_ref, sem, *, tt, alpha):
    def issue(r, c):
        for k in range(TOP_K):
            _row_copy(ys_ref, dest_ref[k, r], rows_ref.at[k], r, sem).start()
        return c

    def drain(r, c):
        for k in range(TOP_K):
            _row_copy(ys_ref, 0, rows_ref.at[0], 0, sem).wait()
        return c

    lax.fori_loop(0, tt, issue, 0)
    x = x_ref[...]
    xb = x.astype(BF16)
    g = jnp.dot(xb, wsg_ref[...], preferred_element_type=F32)
    u = jnp.dot(xb, wsu_ref[...], preferred_element_type=F32)
    y = jnp.dot((_silu(g) * u).astype(BF16), wsd_ref[...], preferred_element_type=F32)
    lax.fori_loop(0, tt, drain, 0)
    gates = gate_ref[...]
    for k in range(TOP_K):
        y = y + rows_ref[k] * gates[:, k:k + 1]
    o_ref[...] = _layer_norm(alpha * x + y, g_ref[...], b_ref[...])


def _combine(x2d, ys, dest, gates_tok, ws_gate, ws_up, ws_down, g, b, alpha, tt):
    T, D = x2d.shape
    F = ws_gate.shape[1]
    row = lambda i: (i, 0)
    fixed = lambda i: (0, 0)
    return pl.pallas_call(
        functools.partial(_combine_kernel, tt=tt, alpha=alpha),
        out_shape=jax.ShapeDtypeStruct((T, D), F32),
        grid=(T // tt,),
        in_specs=[pl.BlockSpec((TOP_K, tt), lambda i: (0, i), memory_space=pltpu.SMEM),
                  pl.BlockSpec((tt, D), row),
                  pl.BlockSpec((tt, LANES), row),
                  pl.BlockSpec((D, F), fixed), pl.BlockSpec((D, F), fixed),
                  pl.BlockSpec((F, D), fixed),
                  pl.BlockSpec((1, D), fixed), pl.BlockSpec((1, D), fixed),
                  pl.BlockSpec(memory_space=pl.ANY)],
        out_specs=pl.BlockSpec((tt, D), row),
        scratch_shapes=[pltpu.VMEM((TOP_K, tt, D), F32), pltpu.SemaphoreType.DMA],
        compiler_params=_cparams("arbitrary"),
        name="moe_combine",
    )(dest, x2d, gates_tok, ws_gate.astype(BF16), ws_up.astype(BF16), ws_down.astype(BF16),
      g.reshape(1, D), b.reshape(1, D), ys)


def _pick(n, pref):
    t = min(n, pref)
    while n % t:
        t //= 2
    return t


def _moe_layer(x2d, w_router, router_bias, w_gate, w_up, w_down, ws_gate, ws_up, ws_down,
               g, b, alpha):
    T, D = x2d.shape
    E = w_router.shape[1]
    idx, gates, rank, counts = _router(x2d, w_router, router_bias, _pick(T, 256))
    counts = counts[:, 0].astype(jnp.int32)
    padded = (counts + EXPERT_ROWS - 1) // EXPERT_ROWS * EXPERT_ROWS
    seg_end = jnp.cumsum(padded)
    seg_start = seg_end - padded
    n_blocks = (T * TOP_K + E * (EXPERT_ROWS - 1)) // EXPERT_ROWS
    dest = _slots(idx, rank, seg_start, _pick(T, 1024))
    blk_e = jnp.minimum(jnp.searchsorted(seg_end, jnp.arange(n_blocks) * EXPERT_ROWS, side='right'),
                        E - 1).astype(jnp.int32)
    n_used = (seg_end[-1:] // EXPERT_ROWS).astype(jnp.int32)
    tt = _pick(T, 128)
    xs = _dispatch(x2d, dest, jnp.stack([seg_start, seg_end]).astype(jnp.int32),
                   n_blocks * EXPERT_ROWS, tt)
    ys = _expert_mlp(xs, blk_e, n_used, w_gate, w_up, w_down)
    return _combine(x2d, ys, dest, gates, ws_gate, ws_up, ws_down, g, b, alpha, tt)


def _mixer_layer(x2d, i, B, S, p, alpha):
    T, D = x2d.shape
    tm = _pick(T, 512)
    tn = _pick(D, 1024)
    j = i // 2
    w_in = p["moba_w_in"][j] if i % 2 == 0 else p["fox_w_in"][j]
    w_qv_t = jnp.concatenate([w_in[:, :D].T * HEAD_DIM ** -0.5, w_in[:, 2 * D:3 * D].T]).astype(BF16)
    qv_t = _project_t(x2d, w_qv_t, tm, tn)
    k = _project(x2d, w_in[:, D:2 * D].astype(BF16), tm, tn)
    if i % 2 == 0:
        k_aug, k_mean = _moba_kprep(k.reshape(B, S, D))
        attn_t = _moba_attention(qv_t, k_aug, k_mean, p["rel_bias"], B, S)
        w_out = p["moba_w_out"][j]
    else:
        c_tok = _fox_gates(x2d.reshape(B, S, D), w_in[:, 3 * D:], p["fox_b_f"][j], _pick(S, 512))
        k_aug = _fox_kprep(k, c_tok.reshape(T, LANES), tm).reshape(HEADS_PER_TILE, B, S, D)
        attn_t = _fox_attention(qv_t, k_aug, B, S, _pick(S, 256))
        w_out = p["fox_w_out"][j]
    return _outproj_ln(attn_t, w_out.astype(BF16), x2d, p["ln1_g"][i], p["ln1_b"][i], alpha, tm)


def kernel(x, rel_bias, moba_w_in, moba_w_out, fox_w_in, fox_b_f, fox_w_out, ln1_g, ln1_b, ln2_g,
           ln2_b, w_router, router_bias, w_gate, w_up, w_down, ws_gate, ws_up, ws_down):
    B, S, D = x.shape
    depth = ln1_g.shape[0]
    alpha = (2 * depth) ** 0.25
    p = dict(rel_bias=rel_bias, moba_w_in=moba_w_in, moba_w_out=moba_w_out, fox_w_in=fox_w_in,
             fox_b_f=fox_b_f, fox_w_out=fox_w_out, ln1_g=ln1_g, ln1_b=ln1_b)
    x2d = x.reshape(B * S, D)
    for i in range(depth):
        x2d = _mixer_layer(x2d, i, B, S, p, alpha)
        x2d = _moe_layer(x2d, w_router[i], router_bias[i], w_gate[i], w_up[i], w_down[i],
                         ws_gate[i], ws_up[i], ws_down[i], ln2_g[i], ln2_b[i], alpha)
    return x2d.reshape(B, S, D)
```

```python
import functools
import math

import jax
import jax.numpy as jnp
from jax import lax
from jax.experimental import pallas as pl
from jax.experimental.pallas import tpu as pltpu

F32 = jnp.float32
BF16 = jnp.bfloat16
HIGHEST = lax.Precision.HIGHEST

HEAD_DIM = 64
MOBA_BLOCK = 256
MOBA_TOPK = 3
REL_MAX_DIST = 128
TOP_K = 8
N_GROUPS = 8
TOPK_GROUPS = 4
ROUTED_SCALE = 2.5
LN_EPS = 1e-5
NEG = -1e30

LANES = 128
HEADS_PER_TILE = LANES // HEAD_DIM
EXPERT_ROWS = 256
VMEM_LIMIT = 48 * 1024 * 1024


def _cparams(*sem):
    return pltpu.CompilerParams(dimension_semantics=sem, vmem_limit_bytes=VMEM_LIMIT)


def _nt_dot(a, b, **kw):
    return lax.dot_general(a, b, (((1,), (1,)), ((), ())), preferred_element_type=F32, **kw)


def _proj_kernel(x_ref, w_ref, o_ref):
    o_ref[...] = jnp.dot(x_ref[...].astype(BF16), w_ref[...],
                         preferred_element_type=F32).astype(o_ref.dtype)


def _project(x2d, w, tm, tn):
    T, K = x2d.shape
    N = w.shape[1]
    return pl.pallas_call(
        _proj_kernel,
        out_shape=jax.ShapeDtypeStruct((T, N), BF16),
        grid=(T // tm, N // tn),
        in_specs=[pl.BlockSpec((tm, K), lambda i, j: (i, 0)),
                  pl.BlockSpec((K, tn), lambda i, j: (0, j))],
        out_specs=pl.BlockSpec((tm, tn), lambda i, j: (i, j)),
        compiler_params=_cparams("parallel", "arbitrary"),
        name="k_proj",
    )(x2d, w)


def _proj_t_kernel(x_ref, wt_ref, o_ref):
    o_ref[...] = _nt_dot(wt_ref[...], x_ref[...].astype(BF16)).astype(o_ref.dtype)


def _project_t(x2d, wt, tm, tn):
    T, K = x2d.shape
    N = wt.shape[0]
    return pl.pallas_call(
        _proj_t_kernel,
        out_shape=jax.ShapeDtypeStruct((N, T), BF16),
        grid=(T // tm, N // tn),
        in_specs=[pl.BlockSpec((tm, K), lambda i, j: (i, 0)),
                  pl.BlockSpec((tn, K), lambda i, j: (j, 0))],
        out_specs=pl.BlockSpec((tn, tm), lambda i, j: (j, i)),
        compiler_params=_cparams("parallel", "arbitrary"),
        name="qv_proj_t",
    )(x2d, wt)


def _layer_norm(r, g, b):
    mu = jnp.mean(r, axis=-1, keepdims=True)
    d = r - mu
    var = jnp.mean(d * d, axis=-1, keepdims=True)
    return d * lax.rsqrt(var + LN_EPS) * g + b


def _outproj_ln_kernel(at_ref, w_ref, x_ref, g_ref, b_ref, o_ref, *, alpha):
    y = lax.dot_general(at_ref[...], w_ref[...], (((0,), (0,)), ((), ())),
                        preferred_element_type=F32)
    o_ref[...] = _layer_norm(alpha * x_ref[...] + y, g_ref[...], b_ref[...])


def _outproj_ln(attn_t, w_out, x2d, g, b, alpha, tm):
    T, D = x2d.shape
    row = lambda i: (i, 0)
    fixed = lambda i: (0, 0)
    return pl.pallas_call(
        functools.partial(_outproj_ln_kernel, alpha=alpha),
        out_shape=jax.ShapeDtypeStruct((T, D), F32),
        grid=(T // tm,),
        in_specs=[pl.BlockSpec((D, tm), lambda i: (0, i)), pl.BlockSpec((D, D), fixed),
                  pl.BlockSpec((tm, D), row), pl.BlockSpec((1, D), fixed),
                  pl.BlockSpec((1, D), fixed)],
        out_specs=pl.BlockSpec((tm, D), row),
        compiler_params=_cparams("parallel"),
        name="outproj_ln",
    )(attn_t, w_out, x2d, g.reshape(1, D), b.reshape(1, D))


def _spare(a, n=0):
    return ((a + 1) % HEADS_PER_TILE) * HEAD_DIM + n


def _scores(chains):
    out = []
    for blocks in chains:
        parts = []
        for keys, query, _, extra, mask in blocks:
            s = jnp.dot(keys(), query(), preferred_element_type=F32)
            if extra is not None:
                s = s + extra()
            if mask is not None:
                s = jnp.where(mask, s, NEG)
            parts.append(s)
        out.append(parts)
    return out


def _absorb(carries, scores, chains):
    probs = []
    for (m, _), parts in zip(carries, scores):
        tops = [jnp.max(s, axis=0, keepdims=True) for s in parts]
        m_new = functools.reduce(jnp.maximum, tops, m)
        probs.append((m_new, jnp.exp(m - m_new), [jnp.exp(s - m_new).astype(BF16) for s in parts]))
    out = []
    for (_, acc), (m_new, decay, ps), blocks in zip(carries, probs, chains):
        acc = decay * acc
        for (_, _, values, _, _), p in zip(blocks, ps):
            acc = acc + jnp.dot(values(), p, preferred_element_type=F32)
        out.append((m_new, acc))
    return tuple(out)


def _attend(carries, chains):
    return _absorb(carries, _scores(chains), chains)


def _attend_pipelined(carries, n, group, stage_refs):
    def stash(chains, ref):
        for a, parts in enumerate(_scores(chains)):
            for b, s in enumerate(parts):
                ref[a, b] = s

    def fetch(chains, ref):
        return [[ref[a, b] for b in range(len(blocks))] for a, blocks in enumerate(chains)]

    odd = n % 2
    carries = lax.cond(odd == 1, lambda c: _attend(c, group(0)), lambda c: c, carries)
    trips = n // 2
    last = n - 1

    def run(carries):
        stash(group(odd), stage_refs[0])

        def body(t, carries):
            g = odd + 2 * t
            stash(group(g + 1), stage_refs[1])
            carries = _absorb(carries, fetch(group(g), stage_refs[0]), group(g))
            stash(group(jnp.minimum(g + 2, last)), stage_refs[0])
            return _absorb(carries, fetch(group(g + 1), stage_refs[1]), group(g + 1))

        return lax.fori_loop(0, trips, body, carries)

    return lax.cond(trips >= 1, run, lambda c: c, carries)


GROUP_BLOCKS = 2


def _stage_scratch(tq):
    return [pltpu.VMEM((HEADS_PER_TILE, GROUP_BLOCKS, tq, tq), F32) for _ in range(2)]


def _scores_init(tq):
    return jnp.full((1, tq), -jnp.inf, F32), jnp.zeros((LANES, tq), F32)


def _finish(carries, o_ref):
    row = lax.broadcasted_iota(jnp.int32, carries[0][1].shape, 0)
    outs = []
    for a, (_, acc) in enumerate(carries):
        s = _spare(a)
        outs.append(acc / acc[s:s + 1, :])
    o_ref[...] = jnp.where(row < HEAD_DIM, outs[0], outs[1]).astype(o_ref.dtype)


def _fill_v_aug(vt_ref, vaug_ref, tk):
    n_tiles = vt_ref.shape[1] // tk
    row = lax.broadcasted_iota(jnp.int32, (LANES, tk), 0)
    for n in range(n_tiles):
        v = vt_ref[:, n * tk:(n + 1) * tk]
        for a in range(HEADS_PER_TILE):
            vaug_ref[a, n] = jnp.where(row // HEAD_DIM == a, v, jnp.ones_like(v))


FOX_SPLIT = 3


def _fox_kernel(qt_ref, k_ref, vt_ref, o_ref, vaug_ref, stage0_ref, stage1_ref, *, tq):
    i = pl.program_id(2)

    @pl.when(i == 0)
    def _():
        _fill_v_aug(vt_ref, vaug_ref, tq)

    row = lax.broadcasted_iota(jnp.int32, (LANES, tq), 0)
    qt = qt_ref[...]
    queries = []
    for a in range(HEADS_PER_TILE):
        offs = (row >= _spare(a)) & (row < _spare(a, FOX_SPLIT))
        queries.append(jnp.where(row // HEAD_DIM == a, qt,
                                 jnp.where(offs, -1.0, 0.0).astype(BF16)))
    key = lax.broadcasted_iota(jnp.int32, (tq, tq), 0)
    qry = lax.broadcasted_iota(jnp.int32, (tq, tq), 1)
    causal = key <= qry

    def block(a, j, mask=None):
        start = pl.multiple_of(j * tq, tq)
        return (lambda: k_ref[a, 0, pl.ds(start, tq), :], lambda: queries[a],
                lambda: vaug_ref[a, j], None, mask)

    heads = range(HEADS_PER_TILE)
    carries = lax.cond(
        i % 2 == 1,
        lambda c: _attend(c, [[block(a, i - 1), block(a, i, causal)] for a in heads]),
        lambda c: _attend(c, [[block(a, i, causal)] for a in heads]),
        tuple(_scores_init(tq) for _ in heads))
    carries = _attend_pipelined(
        carries, i // 2, lambda j: [[block(a, 2 * j), block(a, 2 * j + 1)] for a in heads],
        (stage0_ref, stage1_ref))
    _finish(carries, o_ref)


def _fox_attention(qv_t, k_aug, B, S, tq):
    D = k_aug.shape[-1]
    n_tiles = D // LANES
    nq = S // tq
    return pl.pallas_call(
        functools.partial(_fox_kernel, tq=tq),
        out_shape=jax.ShapeDtypeStruct((D, B * S), BF16),
        grid=(B, n_tiles, nq),
        in_specs=[pl.BlockSpec((LANES, tq), lambda b, h, i: (h, b * nq + i)),
                  pl.BlockSpec((HEADS_PER_TILE, 1, S, LANES), lambda b, h, i: (0, b, 0, h)),
                  pl.BlockSpec((LANES, S), lambda b, h, i: (n_tiles + h, b))],
        out_specs=pl.BlockSpec((LANES, tq), lambda b, h, i: (h, b * nq + i)),
        scratch_shapes=[pltpu.VMEM((HEADS_PER_TILE, nq, LANES, tq), BF16)] + _stage_scratch(tq),
        compiler_params=_cparams("parallel", "parallel", "arbitrary"),
        name="fox_attention",
    )(qv_t, k_aug, qv_t)


def _log_sigmoid(z):
    return jnp.minimum(z, 0.0) - jnp.log1p(jnp.exp(-jnp.abs(z)))


def _fox_gates_kernel(x_ref, w_ref, b_ref, o_ref, carry_ref, *, ts):
    @pl.when(pl.program_id(1) == 0)
    def _():
        carry_ref[...] = jnp.zeros_like(carry_ref)

    z = jnp.dot(x_ref[0], w_ref[...], preferred_element_type=F32, precision=HIGHEST)
    lf = _log_sigmoid(z + b_ref[...])
    r = lax.broadcasted_iota(jnp.int32, (ts, ts), 0)
    c = lax.broadcasted_iota(jnp.int32, (ts, ts), 1)
    tri = (c <= r).astype(F32)
    cum = jnp.dot(tri, lf, preferred_element_type=F32, precision=HIGHEST) + carry_ref[...]
    carry_ref[...] = cum[ts - 1:ts, :]
    o_ref[0] = cum


def _fox_gates(x3d, w_f, b_f, ts):
    B, S, D = x3d.shape
    H = w_f.shape[1]
    w_pad = jnp.zeros((D, LANES), F32).at[:, :H].set(w_f)
    b_pad = jnp.zeros((1, LANES), F32).at[0, :H].set(b_f)
    return pl.pallas_call(
        functools.partial(_fox_gates_kernel, ts=ts),
        out_shape=jax.ShapeDtypeStruct((B, S, LANES), F32),
        grid=(B, S // ts),
        in_specs=[pl.BlockSpec((1, ts, D), lambda b, s: (b, s, 0)),
                  pl.BlockSpec((D, LANES), lambda b, s: (0, 0)),
                  pl.BlockSpec((1, LANES), lambda b, s: (0, 0))],
        out_specs=pl.BlockSpec((1, ts, LANES), lambda b, s: (b, s, 0)),
        scratch_shapes=[pltpu.VMEM((1, LANES), F32)],
        compiler_params=_cparams("parallel", "arbitrary"),
        name="fox_gates",
    )(x3d, w_pad, b_pad)


def _fox_kprep_kernel(k_ref, c_ref, o_ref):
    tm, D = k_ref.shape
    lane = lax.broadcasted_iota(jnp.int32, (tm, LANES), 1)
    for t in range(D // LANES):
        k = k_ref[:, t * LANES:(t + 1) * LANES].astype(F32)
        for a in range(HEADS_PER_TILE):
            h = t * HEADS_PER_TILE + a
            rest = c_ref[:, h:h + 1]
            aug = jnp.zeros((tm, LANES), F32)
            for n in range(FOX_SPLIT):
                piece = rest.astype(BF16).astype(F32)
                rest = rest - piece
                aug = jnp.where(lane == _spare(a, n), piece, aug)
            o_ref[a, :, t * LANES:(t + 1) * LANES] = jnp.where(lane // HEAD_DIM == a, k,
                                                               aug).astype(BF16)


def _fox_kprep(k2d, c_tok, tm):
    T, D = k2d.shape
    return pl.pallas_call(
        _fox_kprep_kernel,
        out_shape=jax.ShapeDtypeStruct((HEADS_PER_TILE, T, D), BF16),
        grid=(T // tm,),
        in_specs=[pl.BlockSpec((tm, D), lambda i: (i, 0)),
                  pl.BlockSpec((tm, LANES), lambda i: (i, 0))],
        out_specs=pl.BlockSpec((HEADS_PER_TILE, tm, D), lambda i: (0, i, 0)),
        compiler_params=_cparams("parallel"),
        name="fox_kprep",
    )(k2d, c_tok)


MOBA_SLAB = 16


def _moba_kernel(bfar_ref, qt_ref, k_ref, vt_ref, kmean_ref, bown_ref, badj_ref, o_ref,
                 vaug_ref, sel_ref, stage0_ref, stage1_ref, *, blk):
    t = pl.program_id(1)
    i = pl.program_id(2)
    nb = kmean_ref.shape[1]

    @pl.when(i == 0)
    def _():
        _fill_v_aug(vt_ref, vaug_ref, blk)

    row = lax.broadcasted_iota(jnp.int32, (LANES, blk), 0)
    brow = lax.broadcasted_iota(jnp.int32, (nb, blk), 0)
    qt = qt_ref[...]
    zero = jnp.zeros_like(qt)
    base, slabs = [], []
    for a in range(HEADS_PER_TILE):
        qa = jnp.where(row // HEAD_DIM == a, qt, zero)
        base.append(qa)
        bscore = jnp.dot(kmean_ref[0], qa.astype(F32), preferred_element_type=F32,
                         precision=HIGHEST)
        cand = jnp.where(brow < i, bscore, NEG)
        sel = jnp.zeros((nb, blk), F32)
        for _ in range(MOBA_TOPK):
            mx = jnp.max(cand, axis=0, keepdims=True)
            first = jnp.min(jnp.where(cand == mx, brow, nb), axis=0, keepdims=True)
            hit = brow == first
            sel = jnp.where(hit, 1.0, sel)
            cand = jnp.where(hit, -jnp.inf, cand)
        sel_ref[a] = jnp.where(brow < i, sel, 0.0)
        slabs.append(_spare(a) // MOBA_SLAB * MOBA_SLAB)

    srow = lax.broadcasted_iota(jnp.int32, (MOBA_SLAB, blk), 0)

    def query(a, j, bias_hi, bias_lo):
        s0 = slabs[a]
        chosen = sel_ref[a, pl.ds(j, 1), :] > 0.0
        slab = jnp.where(srow == _spare(a) - s0, jnp.where(chosen, bias_hi, NEG),
                         jnp.where(srow == _spare(a, 1) - s0, bias_lo, 0.0)).astype(BF16)
        head = [base[a][:s0]] if s0 else []
        tail = [base[a][s0 + MOBA_SLAB:]] if s0 + MOBA_SLAB < LANES else []
        return jnp.concatenate(head + [slab] + tail, axis=0)

    key = lax.broadcasted_iota(jnp.int32, (blk, blk), 0)
    qry = lax.broadcasted_iota(jnp.int32, (blk, blk), 1)
    causal = key <= qry
    heads = range(HEADS_PER_TILE)

    def keys(a, j):
        start = pl.multiple_of(j * blk, blk)
        return lambda: k_ref[a, 0, pl.ds(start, blk), :]

    def own(a):
        return keys(a, i), lambda: base[a], lambda: vaug_ref[a, i], lambda: bown_ref[a], causal

    def prev(a):
        return (keys(a, i - 1), lambda: query(a, i - 1, 0.0, 0.0), lambda: vaug_ref[a, i - 1],
                lambda: badj_ref[a], None)

    def far(a, j):
        h = t * HEADS_PER_TILE + a
        return (keys(a, j), lambda: query(a, j, bfar_ref[0, h], bfar_ref[1, h]),
                lambda: vaug_ref[a, j], None, None)

    carries = lax.cond(
        i >= 1,
        lambda c: _attend(c, [[prev(a), own(a)] for a in heads]),
        lambda c: _attend(c, [[own(a)] for a in heads]),
        tuple(_scores_init(blk) for _ in heads))
    carries = lax.cond(
        (i >= 2) & (i % 2 == 0),
        lambda c: _attend(c, [[far(a, i - 2)] for a in heads]),
        lambda c: c, carries)
    carries = _attend_pipelined(
        carries, (i - 1) // 2, lambda j: [[far(a, 2 * j), far(a, 2 * j + 1)] for a in heads],
        (stage0_ref, stage1_ref))
    _finish(carries, o_ref)


def _rel_bucket(dist, n_buckets):
    n = jnp.maximum(dist, 0)
    max_exact = n_buckets // 2
    nf = jnp.maximum(n, 1).astype(F32)
    large = max_exact + (jnp.log(nf / max_exact) / math.log(REL_MAX_DIST / max_exact)
                         * (n_buckets - max_exact)).astype(jnp.int32)
    large = jnp.minimum(large, n_buckets - 1)
    return jnp.where(n < max_exact, n, large)


def _moba_attention(qv_t, k_aug, k_mean, rel_bias, B, S):
    D = k_aug.shape[-1]
    n_tiles = D // LANES
    blk = MOBA_BLOCK
    nb = S // blk
    hp = HEADS_PER_TILE
    r = jnp.arange(blk)
    delta = r[None, :] - r[:, None]
    bias_t = rel_bias.T.astype(F32)
    n_buckets = rel_bias.shape[0]
    def table(dist):
        onehot = jax.nn.one_hot(_rel_bucket(dist, n_buckets), n_buckets, dtype=F32)
        return jnp.einsum('crn,hn->hcr', onehot, bias_t, precision=HIGHEST)

    b_own = table(delta)
    b_adj = table(delta + blk)
    b_far = bias_t[:, n_buckets - 1]
    far_hi = b_far.astype(BF16).astype(F32)
    far_lo = (b_far - far_hi).astype(BF16).astype(F32)
    b_far2 = jnp.stack([far_hi, far_lo])
    return pl.pallas_call(
        functools.partial(_moba_kernel, blk=blk),
        out_shape=jax.ShapeDtypeStruct((D, B * S), BF16),
        grid_spec=pltpu.PrefetchScalarGridSpec(
            num_scalar_prefetch=1,
            grid=(B, n_tiles, nb),
            in_specs=[pl.BlockSpec((LANES, blk), lambda b, h, i, f: (h, b * nb + i)),
                      pl.BlockSpec((hp, 1, S, LANES), lambda b, h, i, f: (0, b, 0, h)),
                      pl.BlockSpec((LANES, S), lambda b, h, i, f: (n_tiles + h, b)),
                      pl.BlockSpec((1, nb, LANES), lambda b, h, i, f: (b, 0, h)),
                      pl.BlockSpec((hp, blk, blk), lambda b, h, i, f: (h, 0, 0)),
                      pl.BlockSpec((hp, blk, blk), lambda b, h, i, f: (h, 0, 0))],
            out_specs=pl.BlockSpec((LANES, blk), lambda b, h, i, f: (h, b * nb + i)),
            scratch_shapes=[pltpu.VMEM((hp, nb, LANES, blk), BF16),
                            pltpu.VMEM((hp, nb, blk), F32)] + _stage_scratch(blk)),
        compiler_params=_cparams("parallel", "parallel", "arbitrary"),
        name="moba_attention",
    )(b_far2, qv_t, k_aug, qv_t, k_mean, b_own, b_adj)


def _moba_kprep_kernel(k_ref, o_ref, mean_ref):
    k = k_ref[0]
    col = lax.broadcasted_iota(jnp.int32, k.shape, 1)
    lane = col % LANES
    for a in range(HEADS_PER_TILE):
        ones = (lane == _spare(a)) | (lane == _spare(a, 1))
        o_ref[a, 0] = jnp.where((col // HEAD_DIM) % HEADS_PER_TILE == a, k,
                                jnp.where(ones, 1.0, 0.0).astype(BF16))
    mean_ref[0, 0] = jnp.mean(k.astype(F32), axis=0, keepdims=True)


def _moba_kprep(k3d):
    B, S, D = k3d.shape
    blk = MOBA_BLOCK
    nb = S // blk
    k_aug, k_mean = pl.pallas_call(
        _moba_kprep_kernel,
        out_shape=(jax.ShapeDtypeStruct((HEADS_PER_TILE, B, S, D), BF16),
                   jax.ShapeDtypeStruct((B, nb, 1, D), F32)),
        grid=(B, nb),
        in_specs=[pl.BlockSpec((1, blk, D), lambda b, n: (b, n, 0))],
        out_specs=(pl.BlockSpec((HEADS_PER_TILE, 1, blk, D), lambda b, n: (0, b, n, 0)),
                   pl.BlockSpec((1, 1, 1, D), lambda b, n: (b, n, 0, 0))),
        compiler_params=_cparams("parallel", "parallel"),
        name="moba_kprep",
    )(k3d)
    return k_aug, k_mean.reshape(B, nb, D)


def _router_kernel(x_ref, wt_ref, bias_ref, idx_ref, gate_ref, rank_ref, cnt_ref, run_ref, *, tm):
    @pl.when(pl.program_id(0) == 0)
    def _():
        run_ref[...] = jnp.zeros_like(run_ref)

    E = wt_ref.shape[0]
    gsz = E // N_GROUPS
    logits = _nt_dot(wt_ref[...], x_ref[...], precision=HIGHEST)
    s = jax.nn.sigmoid(logits)
    sb = s + bias_ref[...]
    neg_inf = -jnp.inf

    giota = lax.broadcasted_iota(jnp.int32, (gsz, tm), 0)
    gscore = []
    for g in range(N_GROUPS):
        blk = sb[g * gsz:(g + 1) * gsz, :]
        m1 = jnp.max(blk, axis=0, keepdims=True)
        i1 = jnp.min(jnp.where(blk == m1, giota, gsz), axis=0, keepdims=True)
        m2 = jnp.max(jnp.where(giota == i1, neg_inf, blk), axis=0, keepdims=True)
        gscore.append(m1 + m2)

    gsel = [jnp.zeros((1, tm), jnp.bool_) for _ in range(N_GROUPS)]
    for _ in range(TOPK_GROUPS):
        mx = functools.reduce(jnp.maximum, gscore)
        found = jnp.zeros((1, tm), jnp.bool_)
        for g in range(N_GROUPS):
            hit = (gscore[g] == mx) & jnp.logical_not(found)
            gsel[g] = gsel[g] | hit
            found = found | hit
            gscore[g] = jnp.where(hit, neg_inf, gscore[g])
    emask = jnp.concatenate([jnp.broadcast_to(gsel[g], (gsz, tm)) for g in range(N_GROUPS)], axis=0)
    cand = jnp.where(emask, sb, NEG)

    eiota = lax.broadcasted_iota(jnp.int32, (E, tm), 0)
    hits, idxs, ws = [], [], []
    for _ in range(TOP_K):
        mx = jnp.max(cand, axis=0, keepdims=True)
        first = jnp.min(jnp.where(cand == mx, eiota, E), axis=0, keepdims=True)
        hit = eiota == first
        hits.append(hit)
        idxs.append(first)
        ws.append(jnp.sum(jnp.where(hit, s, 0.0), axis=0, keepdims=True))
        cand = jnp.where(hit, neg_inf, cand)
    wsum = functools.reduce(jnp.add, ws)

    chosen = functools.reduce(jnp.logical_or, hits)
    onehot = jnp.where(chosen, 1.0, 0.0)
    tr = lax.broadcasted_iota(jnp.int32, (tm, tm), 0)
    tc = lax.broadcasted_iota(jnp.int32, (tm, tm), 1)
    before = (tr < tc).astype(BF16)
    prior = jnp.dot(onehot.astype(BF16), before, preferred_element_type=F32) + run_ref[...]
    grow = lax.broadcasted_iota(jnp.int32, (LANES, tm), 0)
    gates = jnp.zeros((LANES, tm), F32)
    for k in range(TOP_K):
        idx_ref[k:k + 1, :] = idxs[k]
        gates = jnp.where(grow == k, ws[k] / wsum * ROUTED_SCALE, gates)
        rank_ref[k:k + 1, :] = jnp.sum(jnp.where(hits[k], prior, 0.0), axis=0,
                                       keepdims=True).astype(jnp.int32)
    gate_ref[...] = gates.T
    run_ref[...] = run_ref[...] + jnp.sum(onehot, axis=1, keepdims=True)
    cnt_ref[...] = run_ref[...]


def _router(x2d, w_router, router_bias, tm):
    T, D = x2d.shape
    E = w_router.shape[1]
    tok = lambda i: (0, i)
    fixed = lambda i: (0, 0)
    return pl.pallas_call(
        functools.partial(_router_kernel, tm=tm),
        out_shape=(jax.ShapeDtypeStruct((TOP_K, T), jnp.int32),
                   jax.ShapeDtypeStruct((T, LANES), F32),
                   jax.ShapeDtypeStruct((TOP_K, T), jnp.int32),
                   jax.ShapeDtypeStruct((E, 1), F32)),
        grid=(T // tm,),
        in_specs=[pl.BlockSpec((tm, D), lambda i: (i, 0)),
                  pl.BlockSpec((E, D), fixed),
                  pl.BlockSpec((E, 1), fixed)],
        out_specs=(pl.BlockSpec((TOP_K, tm), tok), pl.BlockSpec((tm, LANES), lambda i: (i, 0)),
                   pl.BlockSpec((TOP_K, tm), tok), pl.BlockSpec((E, 1), fixed)),
        scratch_shapes=[pltpu.VMEM((E, 1), F32)],
        compiler_params=_cparams("arbitrary"),
        name="router",
    )(x2d, w_router.T, router_bias.reshape(E, 1).astype(F32))


def _slots_kernel(idx_ref, rank_ref, start_ref, o_ref):
    E = start_ref.shape[0]
    tm = idx_ref.shape[1]
    eiota = lax.broadcasted_iota(jnp.int32, (E, tm), 0)
    start = start_ref[...]
    for k in range(TOP_K):
        base = jnp.sum(jnp.where(eiota == idx_ref[k:k + 1, :], start, 0.0), axis=0, keepdims=True)
        o_ref[k:k + 1, :] = base.astype(jnp.int32) + rank_ref[k:k + 1, :]


def _slots(idx, rank, seg_start, tm):
    T = idx.shape[1]
    E = seg_start.shape[0]
    tok = lambda i: (0, i)
    return pl.pallas_call(
        _slots_kernel,
        out_shape=jax.ShapeDtypeStruct((TOP_K, T), jnp.int32),
        grid=(T // tm,),
        in_specs=[pl.BlockSpec((TOP_K, tm), tok), pl.BlockSpec((TOP_K, tm), tok),
                  pl.BlockSpec((E, 1), lambda i: (0, 0))],
        out_specs=pl.BlockSpec((TOP_K, tm), tok),
        compiler_params=_cparams("parallel"),
        name="moe_slots",
    )(idx, rank, seg_start.astype(F32).reshape(E, 1))


def _row_copy(src_ref, src_row, dst_ref, dst_row, sem):
    return pltpu.make_async_copy(src_ref.at[pl.ds(src_row, 1), :],
                                 dst_ref.at[pl.ds(dst_row, 1), :], sem)


def _dispatch_kernel(seg_ref, dest_ref, x_ref, xs_ref, zero_ref, zsem, sem, *, tt):
    E = seg_ref.shape[1]

    def zero_copy(e):
        start = pl.multiple_of(seg_ref[1, e] - EXPERT_ROWS, EXPERT_ROWS)
        return pltpu.make_async_copy(zero_ref, xs_ref.at[pl.ds(start, EXPERT_ROWS), :], zsem)

    @pl.when(pl.program_id(0) == 0)
    def _():
        zero_ref[...] = jnp.zeros_like(zero_ref)
        for go in (lambda c: c.start(), lambda c: c.wait()):
            def body(e, carry, go=go):
                @pl.when(seg_ref[1, e] > seg_ref[0, e])
                def _():
                    go(zero_copy(e))
                return carry
            lax.fori_loop(0, E, body, 0)

    def issue(r, c):
        for k in range(TOP_K):
            _row_copy(x_ref, r, xs_ref, dest_ref[k, r], sem).start()
        return c

    def drain(r, c):
        for k in range(TOP_K):
            _row_copy(x_ref, 0, xs_ref, 0, sem).wait()
        return c

    lax.fori_loop(0, tt, issue, 0)
    lax.fori_loop(0, tt, drain, 0)


def _dispatch(x2d, dest, seg, n_rows, tt):
    T, D = x2d.shape
    return pl.pallas_call(
        functools.partial(_dispatch_kernel, tt=tt),
        out_shape=jax.ShapeDtypeStruct((n_rows, D), F32),
        grid_spec=pltpu.PrefetchScalarGridSpec(
            num_scalar_prefetch=1,
            grid=(T // tt,),
            in_specs=[pl.BlockSpec((TOP_K, tt), lambda i, seg: (0, i), memory_space=pltpu.SMEM),
                      pl.BlockSpec((tt, D), lambda i, seg: (i, 0))],
            out_specs=pl.BlockSpec(memory_space=pl.ANY),
            scratch_shapes=[pltpu.VMEM((EXPERT_ROWS, D), F32), pltpu.SemaphoreType.DMA,
                            pltpu.SemaphoreType.DMA]),
        compiler_params=_cparams("arbitrary"),
        name="moe_dispatch",
    )(seg, dest, x2d)


def _silu(g):
    return g * jax.nn.sigmoid(g)


def _expert_kernel(blk_e_ref, n_used_ref, x_ref, wg_ref, wu_ref, wd_ref, o_ref):
    del blk_e_ref
    i = pl.program_id(0)

    @pl.when(i < n_used_ref[0])
    def _():
        x = x_ref[...].astype(BF16)
        g = jnp.dot(x, wg_ref[0, 0].astype(BF16), preferred_element_type=F32)
        u = jnp.dot(x, wu_ref[0, 0].astype(BF16), preferred_element_type=F32)
        a = (_silu(g) * u).astype(BF16)
        o_ref[...] = jnp.dot(a, wd_ref[0, 0].astype(BF16), preferred_element_type=F32)

    @pl.when(i >= n_used_ref[0])
    def _():
        o_ref[...] = jnp.zeros_like(o_ref)


def _expert_mlp(xs, blk_e, n_used, w_gate, w_up, w_down, layer):
    n_rows, D = xs.shape
    F = w_gate.shape[-1]
    tm = EXPERT_ROWS
    rows = lambda i, be, nu: (i, 0)
    expert = lambda i, be, nu: (layer, be[i], 0, 0)
    return pl.pallas_call(
        _expert_kernel,
        out_shape=jax.ShapeDtypeStruct((n_rows, D), F32),
        grid_spec=pltpu.PrefetchScalarGridSpec(
            num_scalar_prefetch=2,
            grid=(n_rows // tm,),
            in_specs=[pl.BlockSpec((tm, D), lambda i, be, nu: (jnp.minimum(i, nu[0] - 1), 0)),
                      pl.BlockSpec((1, 1, D, F), expert),
                      pl.BlockSpec((1, 1, D, F), expert),
                      pl.BlockSpec((1, 1, F, D), expert)],
            out_specs=pl.BlockSpec((tm, D), rows)),
        compiler_params=_cparams("arbitrary"),
        name="moe_experts",
    )(blk_e, n_used, xs, w_gate, w_up, w_down)


def _combine_kernel(dest_ref, x_ref, gate_ref, wsg_ref, wsu_ref, wsd_ref, g_ref, b_ref, ys_ref,
                    o_ref, rows_ref, sem, *, tt, alpha):
    def issue(r, c):
        for k in range(TOP_K):
            _row_copy(ys_ref, dest_ref[k, r], rows_ref.at[k], r, sem).start()
        return c

    def drain(r, c):
        for k in range(TOP_K):
            _row_copy(ys_ref, 0, rows_ref.at[0], 0, sem).wait()
        return c

    lax.fori_loop(0, tt, issue, 0)
    x = x_ref[...]
    xb = x.astype(BF16)
    g = jnp.dot(xb, wsg_ref[...], preferred_element_type=F32)
    u = jnp.dot(xb, wsu_ref[...], preferred_element_type=F32)
    y = jnp.dot((_silu(g) * u).astype(BF16), wsd_ref[...], preferred_element_type=F32)
    lax.fori_loop(0, tt, drain, 0)
    gates = gate_ref[...]
    for k in range(TOP_K):
        y = y + rows_ref[k] * gates[:, k:k + 1]
    o_ref[...] = _layer_norm(alpha * x + y, g_ref[...], b_ref[...])


def _combine(x2d, ys, dest, gates_tok, ws_gate, ws_up, ws_down, g, b, alpha, tt):
    T, D = x2d.shape
    F = ws_gate.shape[1]
    row = lambda i: (i, 0)
    fixed = lambda i: (0, 0)
    return pl.pallas_call(
        functools.partial(_combine_kernel, tt=tt, alpha=alpha),
        out_shape=jax.ShapeDtypeStruct((T, D), F32),
        grid=(T // tt,),
        in_specs=[pl.BlockSpec((TOP_K, tt), lambda i: (0, i), memory_space=pltpu.SMEM),
                  pl.BlockSpec((tt, D), row),
                  pl.BlockSpec((tt, LANES), row),
                  pl.BlockSpec((D, F), fixed), pl.BlockSpec((D, F), fixed),
                  pl.BlockSpec((F, D), fixed),
                  pl.BlockSpec((1, D), fixed), pl.BlockSpec((1, D), fixed),
                  pl.BlockSpec(memory_space=pl.ANY)],
        out_specs=pl.BlockSpec((tt, D), row),
        scratch_shapes=[pltpu.VMEM((TOP_K, tt, D), F32), pltpu.SemaphoreType.DMA],
        compiler_params=_cparams("arbitrary"),
        name="moe_combine",
    )(dest, x2d, gates_tok, ws_gate.astype(BF16), ws_up.astype(BF16), ws_down.astype(BF16),
      g.reshape(1, D), b.reshape(1, D), ys)


def _pick(n, pref):
    t = min(n, pref)
    while n % t:
        t //= 2
    return t


def _moe_layer(x2d, w_router, router_bias, w_gate, w_up, w_down, layer, ws_gate, ws_up, ws_down,
               g, b, alpha):
    T, D = x2d.shape
    E = w_router.shape[1]
    idx, gates, rank, counts = _router(x2d, w_router, router_bias, _pick(T, 256))
    counts = counts[:, 0].astype(jnp.int32)
    padded = (counts + EXPERT_ROWS - 1) // EXPERT_ROWS * EXPERT_ROWS
    seg_end = jnp.cumsum(padded)
    seg_start = seg_end - padded
    n_blocks = (T * TOP_K + E * (EXPERT_ROWS - 1)) // EXPERT_ROWS
    dest = _slots(idx, rank, seg_start, _pick(T, 1024))
    first_row = jnp.arange(n_blocks, dtype=jnp.int32) * EXPERT_ROWS
    blk_e = jnp.minimum(jnp.sum(seg_end[None, :] <= first_row[:, None], axis=1), E - 1).astype(jnp.int32)
    n_used = (seg_end[-1:] // EXPERT_ROWS).astype(jnp.int32)
    tt = _pick(T, 128)
    xs = _dispatch(x2d, dest, jnp.stack([seg_start, seg_end]).astype(jnp.int32),
                   n_blocks * EXPERT_ROWS, tt)
    ys = _expert_mlp(xs, blk_e, n_used, w_gate, w_up, w_down, layer)
    return _combine(x2d, ys, dest, gates, ws_gate, ws_up, ws_down, g, b, alpha, tt)


def _mixer_layer(x2d, i, B, S, p, alpha):
    T, D = x2d.shape
    tm = _pick(T, 512)
    tn = _pick(D, 1024)
    j = i // 2
    w_in = p["moba_w_in"][j] if i % 2 == 0 else p["fox_w_in"][j]
    w_qv_t = jnp.concatenate([w_in[:, :D].T * HEAD_DIM ** -0.5, w_in[:, 2 * D:3 * D].T]).astype(BF16)
    qv_t = _project_t(x2d, w_qv_t, tm, tn)
    k = _project(x2d, w_in[:, D:2 * D].astype(BF16), tm, tn)
    if i % 2 == 0:
        k_aug, k_mean = _moba_kprep(k.reshape(B, S, D))
        attn_t = _moba_attention(qv_t, k_aug, k_mean, p["rel_bias"], B, S)
        w_out = p["moba_w_out"][j]
    else:
        c_tok = _fox_gates(x2d.reshape(B, S, D), w_in[:, 3 * D:], p["fox_b_f"][j], _pick(S, 512))
        k_aug = _fox_kprep(k, c_tok.reshape(T, LANES), tm).reshape(HEADS_PER_TILE, B, S, D)
        attn_t = _fox_attention(qv_t, k_aug, B, S, _pick(S, 256))
        w_out = p["fox_w_out"][j]
    return _outproj_ln(attn_t, w_out.astype(BF16), x2d, p["ln1_g"][i], p["ln1_b"][i], alpha, tm)


def kernel(x, rel_bias, moba_w_in, moba_w_out, fox_w_in, fox_b_f, fox_w_out, ln1_g, ln1_b, ln2_g,
           ln2_b, w_router, router_bias, w_gate, w_up, w_down, ws_gate, ws_up, ws_down):
    B, S, D = x.shape
    depth = ln1_g.shape[0]
    alpha = (2 * depth) ** 0.25
    p = dict(rel_bias=rel_bias, moba_w_in=moba_w_in, moba_w_out=moba_w_out, fox_w_in=fox_w_in,
             fox_b_f=fox_b_f, fox_w_out=fox_w_out, ln1_g=ln1_g, ln1_b=ln1_b)
    x2d = x.reshape(B * S, D)
    for i in range(depth):
        x2d = _mixer_layer(x2d, i, B, S, p, alpha)
        x2d = _moe_layer(x2d, w_router[i], router_bias[i], w_gate, w_up, w_down, i,
                         ws_gate[i], ws_up[i], ws_down[i], ln2_g[i], ln2_b[i], alpha)
    return x2d.reshape(B, S, D)
```

```python
import functools
import math

import jax
import jax.numpy as jnp
from jax import lax
from jax.experimental import pallas as pl
from jax.experimental.pallas import tpu as pltpu

F32 = jnp.float32
BF16 = jnp.bfloat16
HIGHEST = lax.Precision.HIGHEST

HEAD_DIM = 64
MOBA_BLOCK = 256
MOBA_TOPK = 3
REL_MAX_DIST = 128
TOP_K = 8
N_GROUPS = 8
TOPK_GROUPS = 4
ROUTED_SCALE = 2.5
LN_EPS = 1e-5
NEG = -1e30

LANES = 128
HEADS_PER_TILE = LANES // HEAD_DIM
EXPERT_ROWS = 256
VMEM_LIMIT = 48 * 1024 * 1024


def _cparams(*sem):
    return pltpu.CompilerParams(dimension_semantics=sem, vmem_limit_bytes=VMEM_LIMIT)


def _nt_dot(a, b, **kw):
    return lax.dot_general(a, b, (((1,), (1,)), ((), ())), preferred_element_type=F32, **kw)


def _proj_kernel(x_ref, w_ref, o_ref):
    o_ref[...] = jnp.dot(x_ref[...].astype(BF16), w_ref[...],
                         preferred_element_type=F32).astype(o_ref.dtype)


def _project(x2d, w, tm, tn):
    T, K = x2d.shape
    N = w.shape[1]
    return pl.pallas_call(
        _proj_kernel,
        out_shape=jax.ShapeDtypeStruct((T, N), BF16),
        grid=(T // tm, N // tn),
        in_specs=[pl.BlockSpec((tm, K), lambda i, j: (i, 0)),
                  pl.BlockSpec((K, tn), lambda i, j: (0, j))],
        out_specs=pl.BlockSpec((tm, tn), lambda i, j: (i, j)),
        compiler_params=_cparams("parallel", "arbitrary"),
        name="k_proj",
    )(x2d, w)


def _proj_t_kernel(x_ref, wt_ref, o_ref):
    o_ref[...] = _nt_dot(wt_ref[...], x_ref[...].astype(BF16)).astype(o_ref.dtype)


def _project_t(x2d, wt, tm, tn):
    T, K = x2d.shape
    N = wt.shape[0]
    return pl.pallas_call(
        _proj_t_kernel,
        out_shape=jax.ShapeDtypeStruct((N, T), BF16),
        grid=(T // tm, N // tn),
        in_specs=[pl.BlockSpec((tm, K), lambda i, j: (i, 0)),
                  pl.BlockSpec((tn, K), lambda i, j: (j, 0))],
        out_specs=pl.BlockSpec((tn, tm), lambda i, j: (j, i)),
        compiler_params=_cparams("parallel", "arbitrary"),
        name="qv_proj_t",
    )(x2d, wt)


def _layer_norm(r, g, b):
    mu = jnp.mean(r, axis=-1, keepdims=True)
    d = r - mu
    var = jnp.mean(d * d, axis=-1, keepdims=True)
    return d * lax.rsqrt(var + LN_EPS) * g + b


def _outproj_ln_kernel(at_ref, w_ref, x_ref, g_ref, b_ref, o_ref, *, alpha):
    y = lax.dot_general(at_ref[...], w_ref[...], (((0,), (0,)), ((), ())),
                        preferred_element_type=F32)
    o_ref[...] = _layer_norm(alpha * x_ref[...] + y, g_ref[...], b_ref[...])


def _outproj_ln(attn_t, w_out, x2d, g, b, alpha, tm):
    T, D = x2d.shape
    row = lambda i: (i, 0)
    fixed = lambda i: (0, 0)
    return pl.pallas_call(
        functools.partial(_outproj_ln_kernel, alpha=alpha),
        out_shape=jax.ShapeDtypeStruct((T, D), F32),
        grid=(T // tm,),
        in_specs=[pl.BlockSpec((D, tm), lambda i: (0, i)), pl.BlockSpec((D, D), fixed),
                  pl.BlockSpec((tm, D), row), pl.BlockSpec((1, D), fixed),
                  pl.BlockSpec((1, D), fixed)],
        out_specs=pl.BlockSpec((tm, D), row),
        compiler_params=_cparams("parallel"),
        name="outproj_ln",
    )(attn_t, w_out, x2d, g.reshape(1, D), b.reshape(1, D))


def _spare(a, n=0):
    return ((a + 1) % HEADS_PER_TILE) * HEAD_DIM + n


def _scores(chains):
    out = []
    for blocks in chains:
        parts = []
        for keys, query, _, extra, mask in blocks:
            s = jnp.dot(keys(), query(), preferred_element_type=F32)
            if extra is not None:
                s = s + extra()
            if mask is not None:
                s = jnp.where(mask, s, NEG)
            parts.append(s)
        out.append(parts)
    return out


def _absorb(carries, scores, chains):
    probs = []
    for (m, _), parts in zip(carries, scores):
        tops = [jnp.max(s, axis=0, keepdims=True) for s in parts]
        m_new = functools.reduce(jnp.maximum, tops, m)
        probs.append((m_new, jnp.exp(m - m_new), [jnp.exp(s - m_new).astype(BF16) for s in parts]))
    out = []
    for (_, acc), (m_new, decay, ps), blocks in zip(carries, probs, chains):
        acc = decay * acc
        for (_, _, values, _, _), p in zip(blocks, ps):
            acc = acc + jnp.dot(values(), p, preferred_element_type=F32)
        out.append((m_new, acc))
    return tuple(out)


def _attend(carries, chains):
    return _absorb(carries, _scores(chains), chains)


def _attend_pipelined(carries, n, group, stage_refs):
    def stash(chains, ref):
        for a, parts in enumerate(_scores(chains)):
            for b, s in enumerate(parts):
                ref[a, b] = s

    def fetch(chains, ref):
        return [[ref[a, b] for b in range(len(blocks))] for a, blocks in enumerate(chains)]

    odd = n % 2
    carries = lax.cond(odd == 1, lambda c: _attend(c, group(0)), lambda c: c, carries)
    trips = n // 2
    last = n - 1

    def run(carries):
        stash(group(odd), stage_refs[0])

        def body(t, carries):
            g = odd + 2 * t
            stash(group(g + 1), stage_refs[1])
            carries = _absorb(carries, fetch(group(g), stage_refs[0]), group(g))
            stash(group(jnp.minimum(g + 2, last)), stage_refs[0])
            return _absorb(carries, fetch(group(g + 1), stage_refs[1]), group(g + 1))

        return lax.fori_loop(0, trips, body, carries)

    return lax.cond(trips >= 1, run, lambda c: c, carries)


GROUP_BLOCKS = 2


def _stage_scratch(tq):
    return [pltpu.VMEM((HEADS_PER_TILE, GROUP_BLOCKS, tq, tq), F32) for _ in range(2)]


def _scores_init(tq):
    return jnp.full((1, tq), -jnp.inf, F32), jnp.zeros((LANES, tq), F32)


def _finish(carries, o_ref):
    row = lax.broadcasted_iota(jnp.int32, carries[0][1].shape, 0)
    outs = []
    for a, (_, acc) in enumerate(carries):
        s = _spare(a)
        outs.append(acc / acc[s:s + 1, :])
    o_ref[...] = jnp.where(row < HEAD_DIM, outs[0], outs[1]).astype(o_ref.dtype)


def _fill_v_aug(vt_ref, vaug_ref, tk):
    n_tiles = vt_ref.shape[1] // tk
    row = lax.broadcasted_iota(jnp.int32, (LANES, tk), 0)
    for n in range(n_tiles):
        v = vt_ref[:, n * tk:(n + 1) * tk]
        for a in range(HEADS_PER_TILE):
            vaug_ref[a, n] = jnp.where(row // HEAD_DIM == a, v, jnp.ones_like(v))


FOX_SPLIT = 3


def _fox_kernel(qt_ref, k_ref, vt_ref, o_ref, vaug_ref, stage0_ref, stage1_ref, *, tq):
    i = pl.program_id(2)

    @pl.when(i == 0)
    def _():
        _fill_v_aug(vt_ref, vaug_ref, tq)

    row = lax.broadcasted_iota(jnp.int32, (LANES, tq), 0)
    qt = qt_ref[...]
    queries = []
    for a in range(HEADS_PER_TILE):
        offs = (row >= _spare(a)) & (row < _spare(a, FOX_SPLIT))
        queries.append(jnp.where(row // HEAD_DIM == a, qt,
                                 jnp.where(offs, -1.0, 0.0).astype(BF16)))
    key = lax.broadcasted_iota(jnp.int32, (tq, tq), 0)
    qry = lax.broadcasted_iota(jnp.int32, (tq, tq), 1)
    causal = key <= qry

    def block(a, j, mask=None):
        start = pl.multiple_of(j * tq, tq)
        return (lambda: k_ref[a, 0, pl.ds(start, tq), :], lambda: queries[a],
                lambda: vaug_ref[a, j], None, mask)

    heads = range(HEADS_PER_TILE)
    carries = lax.cond(
        i % 2 == 1,
        lambda c: _attend(c, [[block(a, i - 1), block(a, i, causal)] for a in heads]),
        lambda c: _attend(c, [[block(a, i, causal)] for a in heads]),
        tuple(_scores_init(tq) for _ in heads))
    carries = _attend_pipelined(
        carries, i // 2, lambda j: [[block(a, 2 * j), block(a, 2 * j + 1)] for a in heads],
        (stage0_ref, stage1_ref))
    _finish(carries, o_ref)


def _fox_attention(qv_t, k_aug, B, S, tq):
    D = k_aug.shape[-1]
    n_tiles = D // LANES
    nq = S // tq
    return pl.pallas_call(
        functools.partial(_fox_kernel, tq=tq),
        out_shape=jax.ShapeDtypeStruct((D, B * S), BF16),
        grid=(B, n_tiles, nq),
        in_specs=[pl.BlockSpec((LANES, tq), lambda b, h, i: (h, b * nq + i)),
                  pl.BlockSpec((HEADS_PER_TILE, 1, S, LANES), lambda b, h, i: (0, b, 0, h)),
                  pl.BlockSpec((LANES, S), lambda b, h, i: (n_tiles + h, b))],
        out_specs=pl.BlockSpec((LANES, tq), lambda b, h, i: (h, b * nq + i)),
        scratch_shapes=[pltpu.VMEM((HEADS_PER_TILE, nq, LANES, tq), BF16)] + _stage_scratch(tq),
        compiler_params=_cparams("parallel", "parallel", "arbitrary"),
        name="fox_attention",
    )(qv_t, k_aug, qv_t)


def _log_sigmoid(z):
    return jnp.minimum(z, 0.0) - jnp.log1p(jnp.exp(-jnp.abs(z)))


def _fox_gates_kernel(x_ref, w_ref, b_ref, o_ref, carry_ref, *, ts):
    @pl.when(pl.program_id(1) == 0)
    def _():
        carry_ref[...] = jnp.zeros_like(carry_ref)

    z = jnp.dot(x_ref[0], w_ref[...], preferred_element_type=F32, precision=HIGHEST)
    lf = _log_sigmoid(z + b_ref[...])
    r = lax.broadcasted_iota(jnp.int32, (ts, ts), 0)
    c = lax.broadcasted_iota(jnp.int32, (ts, ts), 1)
    tri = (c <= r).astype(F32)
    cum = jnp.dot(tri, lf, preferred_element_type=F32, precision=HIGHEST) + carry_ref[...]
    carry_ref[...] = cum[ts - 1:ts, :]
    o_ref[0] = cum


def _fox_gates(x3d, w_f, b_f, ts):
    B, S, D = x3d.shape
    H = w_f.shape[1]
    w_pad = jnp.zeros((D, LANES), F32).at[:, :H].set(w_f)
    b_pad = jnp.zeros((1, LANES), F32).at[0, :H].set(b_f)
    return pl.pallas_call(
        functools.partial(_fox_gates_kernel, ts=ts),
        out_shape=jax.ShapeDtypeStruct((B, S, LANES), F32),
        grid=(B, S // ts),
        in_specs=[pl.BlockSpec((1, ts, D), lambda b, s: (b, s, 0)),
                  pl.BlockSpec((D, LANES), lambda b, s: (0, 0)),
                  pl.BlockSpec((1, LANES), lambda b, s: (0, 0))],
        out_specs=pl.BlockSpec((1, ts, LANES), lambda b, s: (b, s, 0)),
        scratch_shapes=[pltpu.VMEM((1, LANES), F32)],
        compiler_params=_cparams("parallel", "arbitrary"),
        name="fox_gates",
    )(x3d, w_pad, b_pad)


def _fox_kprep_kernel(k_ref, c_ref, o_ref):
    tm, D = k_ref.shape
    lane = lax.broadcasted_iota(jnp.int32, (tm, LANES), 1)
    for t in range(D // LANES):
        k = k_ref[:, t * LANES:(t + 1) * LANES].astype(F32)
        for a in range(HEADS_PER_TILE):
            h = t * HEADS_PER_TILE + a
            rest = c_ref[:, h:h + 1]
            aug = jnp.zeros((tm, LANES), F32)
            for n in range(FOX_SPLIT):
                piece = rest.astype(BF16).astype(F32)
                rest = rest - piece
                aug = jnp.where(lane == _spare(a, n), piece, aug)
            o_ref[a, :, t * LANES:(t + 1) * LANES] = jnp.where(lane // HEAD_DIM == a, k,
                                                               aug).astype(BF16)


def _fox_kprep(k2d, c_tok, tm):
    T, D = k2d.shape
    return pl.pallas_call(
        _fox_kprep_kernel,
        out_shape=jax.ShapeDtypeStruct((HEADS_PER_TILE, T, D), BF16),
        grid=(T // tm,),
        in_specs=[pl.BlockSpec((tm, D), lambda i: (i, 0)),
                  pl.BlockSpec((tm, LANES), lambda i: (i, 0))],
        out_specs=pl.BlockSpec((HEADS_PER_TILE, tm, D), lambda i: (0, i, 0)),
        compiler_params=_cparams("parallel"),
        name="fox_kprep",
    )(k2d, c_tok)


MOBA_SLAB = 16


def _moba_kernel(bfar_ref, qt_ref, k_ref, vt_ref, kmean_ref, bown_ref, badj_ref, o_ref,
                 vaug_ref, sel_ref, stage0_ref, stage1_ref, *, blk):
    t = pl.program_id(1)
    i = pl.program_id(2)
    nb = kmean_ref.shape[1]

    @pl.when(i == 0)
    def _():
        _fill_v_aug(vt_ref, vaug_ref, blk)

    row = lax.broadcasted_iota(jnp.int32, (LANES, blk), 0)
    brow = lax.broadcasted_iota(jnp.int32, (nb, blk), 0)
    qt = qt_ref[...]
    zero = jnp.zeros_like(qt)
    base, slabs = [], []
    for a in range(HEADS_PER_TILE):
        qa = jnp.where(row // HEAD_DIM == a, qt, zero)
        base.append(qa)
        bscore = jnp.dot(kmean_ref[0], qa.astype(F32), preferred_element_type=F32,
                         precision=HIGHEST)
        cand = jnp.where(brow < i, bscore, NEG)
        sel = jnp.zeros((nb, blk), F32)
        for _ in range(MOBA_TOPK):
            mx = jnp.max(cand, axis=0, keepdims=True)
            first = jnp.min(jnp.where(cand == mx, brow, nb), axis=0, keepdims=True)
            hit = brow == first
            sel = jnp.where(hit, 1.0, sel)
            cand = jnp.where(hit, -jnp.inf, cand)
        sel_ref[a] = jnp.where(brow < i, sel, 0.0)
        slabs.append(_spare(a) // MOBA_SLAB * MOBA_SLAB)

    srow = lax.broadcasted_iota(jnp.int32, (MOBA_SLAB, blk), 0)

    def query(a, j, bias_hi, bias_lo):
        s0 = slabs[a]
        chosen = sel_ref[a, pl.ds(j, 1), :] > 0.0
        slab = jnp.where(srow == _spare(a) - s0, jnp.where(chosen, bias_hi, NEG),
                         jnp.where(srow == _spare(a, 1) - s0, bias_lo, 0.0)).astype(BF16)
        head = [base[a][:s0]] if s0 else []
        tail = [base[a][s0 + MOBA_SLAB:]] if s0 + MOBA_SLAB < LANES else []
        return jnp.concatenate(head + [slab] + tail, axis=0)

    key = lax.broadcasted_iota(jnp.int32, (blk, blk), 0)
    qry = lax.broadcasted_iota(jnp.int32, (blk, blk), 1)
    causal = key <= qry
    heads = range(HEADS_PER_TILE)

    def keys(a, j):
        start = pl.multiple_of(j * blk, blk)
        return lambda: k_ref[a, 0, pl.ds(start, blk), :]

    def own(a):
        return keys(a, i), lambda: base[a], lambda: vaug_ref[a, i], lambda: bown_ref[a], causal

    def prev(a):
        return (keys(a, i - 1), lambda: query(a, i - 1, 0.0, 0.0), lambda: vaug_ref[a, i - 1],
                lambda: badj_ref[a], None)

    def far(a, j):
        h = t * HEADS_PER_TILE + a
        return (keys(a, j), lambda: query(a, j, bfar_ref[0, h], bfar_ref[1, h]),
                lambda: vaug_ref[a, j], None, None)

    carries = lax.cond(
        i >= 1,
        lambda c: _attend(c, [[prev(a), own(a)] for a in heads]),
        lambda c: _attend(c, [[own(a)] for a in heads]),
        tuple(_scores_init(blk) for _ in heads))
    carries = lax.cond(
        (i >= 2) & (i % 2 == 0),
        lambda c: _attend(c, [[far(a, i - 2)] for a in heads]),
        lambda c: c, carries)
    carries = _attend_pipelined(
        carries, (i - 1) // 2, lambda j: [[far(a, 2 * j), far(a, 2 * j + 1)] for a in heads],
        (stage0_ref, stage1_ref))
    _finish(carries, o_ref)


def _rel_bucket(dist, n_buckets):
    n = jnp.maximum(dist, 0)
    max_exact = n_buckets // 2
    nf = jnp.maximum(n, 1).astype(F32)
    large = max_exact + (jnp.log(nf / max_exact) / math.log(REL_MAX_DIST / max_exact)
                         * (n_buckets - max_exact)).astype(jnp.int32)
    large = jnp.minimum(large, n_buckets - 1)
    return jnp.where(n < max_exact, n, large)


def _moba_attention(qv_t, k_aug, k_mean, rel_bias, B, S):
    D = k_aug.shape[-1]
    n_tiles = D // LANES
    blk = MOBA_BLOCK
    nb = S // blk
    hp = HEADS_PER_TILE
    r = jnp.arange(blk)
    delta = r[None, :] - r[:, None]
    bias_t = rel_bias.T.astype(F32)
    n_buckets = rel_bias.shape[0]
    def table(dist):
        onehot = jax.nn.one_hot(_rel_bucket(dist, n_buckets), n_buckets, dtype=F32)
        return jnp.einsum('crn,hn->hcr', onehot, bias_t, precision=HIGHEST)

    b_own = table(delta)
    b_adj = table(delta + blk)
    b_far = bias_t[:, n_buckets - 1]
    far_hi = b_far.astype(BF16).astype(F32)
    far_lo = (b_far - far_hi).astype(BF16).astype(F32)
    b_far2 = jnp.stack([far_hi, far_lo])
    return pl.pallas_call(
        functools.partial(_moba_kernel, blk=blk),
        out_shape=jax.ShapeDtypeStruct((D, B * S), BF16),
        grid_spec=pltpu.PrefetchScalarGridSpec(
            num_scalar_prefetch=1,
            grid=(B, n_tiles, nb),
            in_specs=[pl.BlockSpec((LANES, blk), lambda b, h, i, f: (h, b * nb + i)),
                      pl.BlockSpec((hp, 1, S, LANES), lambda b, h, i, f: (0, b, 0, h)),
                      pl.BlockSpec((LANES, S), lambda b, h, i, f: (n_tiles + h, b)),
                      pl.BlockSpec((1, nb, LANES), lambda b, h, i, f: (b, 0, h)),
                      pl.BlockSpec((hp, blk, blk), lambda b, h, i, f: (h, 0, 0)),
                      pl.BlockSpec((hp, blk, blk), lambda b, h, i, f: (h, 0, 0))],
            out_specs=pl.BlockSpec((LANES, blk), lambda b, h, i, f: (h, b * nb + i)),
            scratch_shapes=[pltpu.VMEM((hp, nb, LANES, blk), BF16),
                            pltpu.VMEM((hp, nb, blk), F32)] + _stage_scratch(blk)),
        compiler_params=_cparams("parallel", "parallel", "arbitrary"),
        name="moba_attention",
    )(b_far2, qv_t, k_aug, qv_t, k_mean, b_own, b_adj)


def _moba_kprep_kernel(k_ref, o_ref, mean_ref):
    k = k_ref[0]
    col = lax.broadcasted_iota(jnp.int32, k.shape, 1)
    lane = col % LANES
    for a in range(HEADS_PER_TILE):
        ones = (lane == _spare(a)) | (lane == _spare(a, 1))
        o_ref[a, 0] = jnp.where((col // HEAD_DIM) % HEADS_PER_TILE == a, k,
                                jnp.where(ones, 1.0, 0.0).astype(BF16))
    mean_ref[0, 0] = jnp.mean(k.astype(F32), axis=0, keepdims=True)


def _moba_kprep(k3d):
    B, S, D = k3d.shape
    blk = MOBA_BLOCK
    nb = S // blk
    k_aug, k_mean = pl.pallas_call(
        _moba_kprep_kernel,
        out_shape=(jax.ShapeDtypeStruct((HEADS_PER_TILE, B, S, D), BF16),
                   jax.ShapeDtypeStruct((B, nb, 1, D), F32)),
        grid=(B, nb),
        in_specs=[pl.BlockSpec((1, blk, D), lambda b, n: (b, n, 0))],
        out_specs=(pl.BlockSpec((HEADS_PER_TILE, 1, blk, D), lambda b, n: (0, b, n, 0)),
                   pl.BlockSpec((1, 1, 1, D), lambda b, n: (b, n, 0, 0))),
        compiler_params=_cparams("parallel", "parallel"),
        name="moba_kprep",
    )(k3d)
    return k_aug, k_mean.reshape(B, nb, D)


def _router_kernel(x_ref, wt_ref, bias_ref, idx_ref, gate_ref, rank_ref, cnt_ref, run_ref, *, tm):
    @pl.when(pl.program_id(0) == 0)
    def _():
        run_ref[...] = jnp.zeros_like(run_ref)

    E = wt_ref.shape[0]
    gsz = E // N_GROUPS
    logits = _nt_dot(wt_ref[...], x_ref[...], precision=HIGHEST)
    s = jax.nn.sigmoid(logits)
    sb = s + bias_ref[...]
    neg_inf = -jnp.inf

    giota = lax.broadcasted_iota(jnp.int32, (gsz, tm), 0)
    gscore = []
    for g in range(N_GROUPS):
        blk = sb[g * gsz:(g + 1) * gsz, :]
        m1 = jnp.max(blk, axis=0, keepdims=True)
        i1 = jnp.min(jnp.where(blk == m1, giota, gsz), axis=0, keepdims=True)
        m2 = jnp.max(jnp.where(giota == i1, neg_inf, blk), axis=0, keepdims=True)
        gscore.append(m1 + m2)

    gsel = [jnp.zeros((1, tm), jnp.bool_) for _ in range(N_GROUPS)]
    for _ in range(TOPK_GROUPS):
        mx = functools.reduce(jnp.maximum, gscore)
        found = jnp.zeros((1, tm), jnp.bool_)
        for g in range(N_GROUPS):
            hit = (gscore[g] == mx) & jnp.logical_not(found)
            gsel[g] = gsel[g] | hit
            found = found | hit
            gscore[g] = jnp.where(hit, neg_inf, gscore[g])
    emask = jnp.concatenate([jnp.broadcast_to(gsel[g], (gsz, tm)) for g in range(N_GROUPS)], axis=0)
    cand = jnp.where(emask, sb, NEG)

    eiota = lax.broadcasted_iota(jnp.int32, (E, tm), 0)
    hits, idxs, ws = [], [], []
    for _ in range(TOP_K):
        mx = jnp.max(cand, axis=0, keepdims=True)
        first = jnp.min(jnp.where(cand == mx, eiota, E), axis=0, keepdims=True)
        hit = eiota == first
        hits.append(hit)
        idxs.append(first)
        ws.append(jnp.sum(jnp.where(hit, s, 0.0), axis=0, keepdims=True))
        cand = jnp.where(hit, neg_inf, cand)
    wsum = functools.reduce(jnp.add, ws)

    chosen = functools.reduce(jnp.logical_or, hits)
    onehot = jnp.where(chosen, 1.0, 0.0)
    tr = lax.broadcasted_iota(jnp.int32, (tm, tm), 0)
    tc = lax.broadcasted_iota(jnp.int32, (tm, tm), 1)
    before = (tr < tc).astype(BF16)
    prior = jnp.dot(onehot.astype(BF16), before, preferred_element_type=F32) + run_ref[...]
    grow = lax.broadcasted_iota(jnp.int32, (LANES, tm), 0)
    gates = jnp.zeros((LANES, tm), F32)
    for k in range(TOP_K):
        idx_ref[k:k + 1, :] = idxs[k]
        gates = jnp.where(grow == k, ws[k] / wsum * ROUTED_SCALE, gates)
        rank_ref[k:k + 1, :] = jnp.sum(jnp.where(hits[k], prior, 0.0), axis=0,
                                       keepdims=True).astype(jnp.int32)
    gate_ref[...] = gates.T
    run_ref[...] = run_ref[...] + jnp.sum(onehot, axis=1, keepdims=True)
    cnt_ref[...] = run_ref[...]


def _router(x2d, w_router, router_bias, tm):
    T, D = x2d.shape
    E = w_router.shape[1]
    tok = lambda i: (0, i)
    fixed = lambda i: (0, 0)
    return pl.pallas_call(
        functools.partial(_router_kernel, tm=tm),
        out_shape=(jax.ShapeDtypeStruct((TOP_K, T), jnp.int32),
                   jax.ShapeDtypeStruct((T, LANES), F32),
                   jax.ShapeDtypeStruct((TOP_K, T), jnp.int32),
                   jax.ShapeDtypeStruct((E, 1), F32)),
        grid=(T // tm,),
        in_specs=[pl.BlockSpec((tm, D), lambda i: (i, 0)),
                  pl.BlockSpec((E, D), fixed),
                  pl.BlockSpec((E, 1), fixed)],
        out_specs=(pl.BlockSpec((TOP_K, tm), tok), pl.BlockSpec((tm, LANES), lambda i: (i, 0)),
                   pl.BlockSpec((TOP_K, tm), tok), pl.BlockSpec((E, 1), fixed)),
        scratch_shapes=[pltpu.VMEM((E, 1), F32)],
        compiler_params=_cparams("arbitrary"),
        name="router",
    )(x2d, w_router.T, router_bias.reshape(E, 1).astype(F32))


def _slots_kernel(idx_ref, rank_ref, start_ref, o_ref):
    E = start_ref.shape[0]
    tm = idx_ref.shape[1]
    eiota = lax.broadcasted_iota(jnp.int32, (E, tm), 0)
    start = start_ref[...]
    for k in range(TOP_K):
        base = jnp.sum(jnp.where(eiota == idx_ref[k:k + 1, :], start, 0.0), axis=0, keepdims=True)
        o_ref[k:k + 1, :] = base.astype(jnp.int32) + rank_ref[k:k + 1, :]


def _slots(idx, rank, seg_start, tm):
    T = idx.shape[1]
    E = seg_start.shape[0]
    tok = lambda i: (0, i)
    return pl.pallas_call(
        _slots_kernel,
        out_shape=jax.ShapeDtypeStruct((TOP_K, T), jnp.int32),
        grid=(T // tm,),
        in_specs=[pl.BlockSpec((TOP_K, tm), tok), pl.BlockSpec((TOP_K, tm), tok),
                  pl.BlockSpec((E, 1), lambda i: (0, 0))],
        out_specs=pl.BlockSpec((TOP_K, tm), tok),
        compiler_params=_cparams("parallel"),
        name="moe_slots",
    )(idx, rank, seg_start.astype(F32).reshape(E, 1))


def _row_copy(src_ref, src_row, dst_ref, dst_row, sem):
    return pltpu.make_async_copy(src_ref.at[pl.ds(src_row, 1), :],
                                 dst_ref.at[pl.ds(dst_row, 1), :], sem)


def _dispatch_kernel(seg_ref, dest_ref, x_ref, xs_ref, zero_ref, zsem, sem, *, tt):
    E = seg_ref.shape[1]

    def zero_copy(e):
        start = pl.multiple_of(seg_ref[1, e] - EXPERT_ROWS, EXPERT_ROWS)
        return pltpu.make_async_copy(zero_ref, xs_ref.at[pl.ds(start, EXPERT_ROWS), :], zsem)

    @pl.when(pl.program_id(0) == 0)
    def _():
        zero_ref[...] = jnp.zeros_like(zero_ref)
        for go in (lambda c: c.start(), lambda c: c.wait()):
            def body(e, carry, go=go):
                @pl.when(seg_ref[1, e] > seg_ref[0, e])
                def _():
                    go(zero_copy(e))
                return carry
            lax.fori_loop(0, E, body, 0)

    def issue(r, c):
        for k in range(TOP_K):
            _row_copy(x_ref, r, xs_ref, dest_ref[k, r], sem).start()
        return c

    def drain(r, c):
        for k in range(TOP_K):
            _row_copy(x_ref, 0, xs_ref, 0, sem).wait()
        return c

    lax.fori_loop(0, tt, issue, 0)
    lax.fori_loop(0, tt, drain, 0)


def _dispatch(x2d, dest, seg, n_rows, tt):
    T, D = x2d.shape
    return pl.pallas_call(
        functools.partial(_dispatch_kernel, tt=tt),
        out_shape=jax.ShapeDtypeStruct((n_rows, D), F32),
        grid_spec=pltpu.PrefetchScalarGridSpec(
            num_scalar_prefetch=1,
            grid=(T // tt,),
            in_specs=[pl.BlockSpec((TOP_K, tt), lambda i, seg: (0, i), memory_space=pltpu.SMEM),
                      pl.BlockSpec((tt, D), lambda i, seg: (i, 0))],
            out_specs=pl.BlockSpec(memory_space=pl.ANY),
            scratch_shapes=[pltpu.VMEM((EXPERT_ROWS, D), F32), pltpu.SemaphoreType.DMA,
                            pltpu.SemaphoreType.DMA]),
        compiler_params=_cparams("arbitrary"),
        name="moe_dispatch",
    )(seg, dest, x2d)


def _silu(g):
    return g * jax.nn.sigmoid(g)


SCHED_FIRST, SCHED_SLOT, SCHED_NEXT = 0, 1, 2


def _expert_kernel(sched_ref, n_used_ref, x_ref, wg_hbm, wu_hbm, wd_hbm, o_ref,
                   wg_buf, wu_buf, wd_buf, sem, *, layer, first_expert_row):
    i = pl.program_id(0)
    slot = sched_ref[SCHED_SLOT, i]

    def copies(expert, slot):
        pairs = ((wg_hbm, wg_buf), (wu_hbm, wu_buf), (wd_hbm, wd_buf))
        return [pltpu.make_async_copy(src.at[layer, expert], dst.at[slot], sem.at[slot, n])
                for n, (src, dst) in enumerate(pairs)]

    @pl.when(i == 0)
    def _():
        for c in copies(sched_ref[first_expert_row, 0], 0):
            c.start()

    @pl.when(sched_ref[SCHED_FIRST, i] == 1)
    def _():
        for c in copies(0, slot):
            c.wait()

        @pl.when(sched_ref[SCHED_NEXT, i] >= 0)
        def _():
            for c in copies(sched_ref[SCHED_NEXT, i], 1 - slot):
                c.start()

    @pl.when(i < n_used_ref[0])
    def _():
        x = x_ref[...].astype(BF16)
        g = jnp.dot(x, wg_buf[slot].astype(BF16), preferred_element_type=F32)
        u = jnp.dot(x, wu_buf[slot].astype(BF16), preferred_element_type=F32)
        a = (_silu(g) * u).astype(BF16)
        o_ref[...] = jnp.dot(a, wd_buf[slot].astype(BF16), preferred_element_type=F32)

    @pl.when(i >= n_used_ref[0])
    def _():
        o_ref[...] = jnp.zeros_like(o_ref)


def _expert_schedule(seg_start, seg_end, n_blocks):
    E = seg_end.shape[0]
    first_row = jnp.arange(n_blocks, dtype=jnp.int32) * EXPERT_ROWS
    blk_e = jnp.minimum(jnp.sum(seg_end[None, :] <= first_row[:, None], axis=1), E - 1)
    used = first_row < seg_end[-1]
    first = used & (first_row == seg_start[blk_e])
    ordinal = jnp.cumsum(first) - 1
    ids = jnp.where(seg_end > seg_start, jnp.arange(E), E)
    later = lax.cummin(ids, reverse=True)
    nxt = jnp.concatenate([later[1:], jnp.full((1,), E, later.dtype)])[blk_e]
    nxt = jnp.where(nxt < E, nxt, -1)
    return jnp.stack([first, ordinal % 2, nxt, blk_e]).astype(jnp.int32)


def _expert_mlp(xs, sched, n_used, w_gate, w_up, w_down, layer):
    n_rows, D = xs.shape
    F = w_gate.shape[-1]
    tm = EXPERT_ROWS
    hbm = pl.BlockSpec(memory_space=pl.ANY)
    return pl.pallas_call(
        functools.partial(_expert_kernel, layer=layer, first_expert_row=sched.shape[0] - 1),
        out_shape=jax.ShapeDtypeStruct((n_rows, D), F32),
        grid_spec=pltpu.PrefetchScalarGridSpec(
            num_scalar_prefetch=2,
            grid=(n_rows // tm,),
            in_specs=[pl.BlockSpec((tm, D), lambda i, sc, nu: (jnp.minimum(i, nu[0] - 1), 0)),
                      hbm, hbm, hbm],
            out_specs=pl.BlockSpec((tm, D), lambda i, sc, nu: (i, 0)),
            scratch_shapes=[pltpu.VMEM((2, D, F), F32), pltpu.VMEM((2, D, F), F32),
                            pltpu.VMEM((2, F, D), F32), pltpu.SemaphoreType.DMA((2, 3))]),
        compiler_params=_cparams("arbitrary"),
        name="moe_experts",
    )(sched, n_used, xs, w_gate, w_up, w_down)


def _combine_kernel(dest_ref, x_ref, gate_ref, wsg_ref, wsu_ref, wsd_ref, g_ref, b_ref, ys_ref,
                    o_ref, rows_ref, sem, *, tt, alpha):
    def issue(r, c):
        for k in range(TOP_K):
            _row_copy(ys_ref, dest_ref[k, r], rows_ref.at[k], r, sem).start()
        return c

    def drain(r, c):
        for k in range(TOP_K):
            _row_copy(ys_ref, 0, rows_ref.at[0], 0, sem).wait()
        return c

    lax.fori_loop(0, tt, issue, 0)
    x = x_ref[...]
    xb = x.astype(BF16)
    g = jnp.dot(xb, wsg_ref[...], preferred_element_type=F32)
    u = jnp.dot(xb, wsu_ref[...], preferred_element_type=F32)
    y = jnp.dot((_silu(g) * u).astype(BF16), wsd_ref[...], preferred_element_type=F32)
    lax.fori_loop(0, tt, drain, 0)
    gates = gate_ref[...]
    for k in range(TOP_K):
        y = y + rows_ref[k] * gates[:, k:k + 1]
    o_ref[...] = _layer_norm(alpha * x + y, g_ref[...], b_ref[...])


def _combine(x2d, ys, dest, gates_tok, ws_gate, ws_up, ws_down, g, b, alpha, tt):
    T, D = x2d.shape
    F = ws_gate.shape[1]
    row = lambda i: (i, 0)
    fixed = lambda i: (0, 0)
    return pl.pallas_call(
        functools.partial(_combine_kernel, tt=tt, alpha=alpha),
        out_shape=jax.ShapeDtypeStruct((T, D), F32),
        grid=(T // tt,),
        in_specs=[pl.BlockSpec((TOP_K, tt), lambda i: (0, i), memory_space=pltpu.SMEM),
                  pl.BlockSpec((tt, D), row),
                  pl.BlockSpec((tt, LANES), row),
                  pl.BlockSpec((D, F), fixed), pl.BlockSpec((D, F), fixed),
                  pl.BlockSpec((F, D), fixed),
                  pl.BlockSpec((1, D), fixed), pl.BlockSpec((1, D), fixed),
                  pl.BlockSpec(memory_space=pl.ANY)],
        out_specs=pl.BlockSpec((tt, D), row),
        scratch_shapes=[pltpu.VMEM((TOP_K, tt, D), F32), pltpu.SemaphoreType.DMA],
        compiler_params=_cparams("arbitrary"),
        name="moe_combine",
    )(dest, x2d, gates_tok, ws_gate.astype(BF16), ws_up.astype(BF16), ws_down.astype(BF16),
      g.reshape(1, D), b.reshape(1, D), ys)


def _pick(n, pref):
    t = min(n, pref)
    while n % t:
        t //= 2
    return t


def _moe_layer(x2d, w_router, router_bias, w_gate, w_up, w_down, layer, ws_gate, ws_up, ws_down,
               g, b, alpha):
    T, D = x2d.shape
    E = w_router.shape[1]
    idx, gates, rank, counts = _router(x2d, w_router, router_bias, _pick(T, 256))
    counts = counts[:, 0].astype(jnp.int32)
    padded = (counts + EXPERT_ROWS - 1) // EXPERT_ROWS * EXPERT_ROWS
    seg_end = jnp.cumsum(padded)
    seg_start = seg_end - padded
    n_blocks = (T * TOP_K + E * (EXPERT_ROWS - 1)) // EXPERT_ROWS
    dest = _slots(idx, rank, seg_start, _pick(T, 1024))
    sched = _expert_schedule(seg_start, seg_end, n_blocks)
    n_used = (seg_end[-1:] // EXPERT_ROWS).astype(jnp.int32)
    tt = _pick(T, 128)
    xs = _dispatch(x2d, dest, jnp.stack([seg_start, seg_end]).astype(jnp.int32),
                   n_blocks * EXPERT_ROWS, tt)
    ys = _expert_mlp(xs, sched, n_used, w_gate, w_up, w_down, layer)
    return _combine(x2d, ys, dest, gates, ws_gate, ws_up, ws_down, g, b, alpha, tt)


def _mixer_layer(x2d, i, B, S, p, alpha):
    T, D = x2d.shape
    tm = _pick(T, 512)
    tn = _pick(D, 1024)
    j = i // 2
    w_in = p["moba_w_in"][j] if i % 2 == 0 else p["fox_w_in"][j]
    w_qv_t = jnp.concatenate([w_in[:, :D].T * HEAD_DIM ** -0.5, w_in[:, 2 * D:3 * D].T]).astype(BF16)
    qv_t = _project_t(x2d, w_qv_t, tm, tn)
    k = _project(x2d, w_in[:, D:2 * D].astype(BF16), tm, tn)
    if i % 2 == 0:
        k_aug, k_mean = _moba_kprep(k.reshape(B, S, D))
        attn_t = _moba_attention(qv_t, k_aug, k_mean, p["rel_bias"], B, S)
        w_out = p["moba_w_out"][j]
    else:
        c_tok = _fox_gates(x2d.reshape(B, S, D), w_in[:, 3 * D:], p["fox_b_f"][j], _pick(S, 512))
        k_aug = _fox_kprep(k, c_tok.reshape(T, LANES), tm).reshape(HEADS_PER_TILE, B, S, D)
        attn_t = _fox_attention(qv_t, k_aug, B, S, _pick(S, 256))
        w_out = p["fox_w_out"][j]
    return _outproj_ln(attn_t, w_out.astype(BF16), x2d, p["ln1_g"][i], p["ln1_b"][i], alpha, tm)


def kernel(x, rel_bias, moba_w_in, moba_w_out, fox_w_in, fox_b_f, fox_w_out, ln1_g, ln1_b, ln2_g,
           ln2_b, w_router, router_bias, w_gate, w_up, w_down, ws_gate, ws_up, ws_down):
    B, S, D = x.shape
    depth = ln1_g.shape[0]
    alpha = (2 * depth) ** 0.25
    p = dict(rel_bias=rel_bias, moba_w_in=moba_w_in, moba_w_out=moba_w_out, fox_w_in=fox_w_in,
             fox_b_f=fox_b_f, fox_w_out=fox_w_out, ln1_g=ln1_g, ln1_b=ln1_b)
    x2d = x.reshape(B * S, D)
    for i in range(depth):
        x2d = _mixer_layer(x2d, i, B, S, p, alpha)
        x2d = _moe_layer(x2d, w_router[i], router_bias[i], w_gate, w_up, w_down, i,
                         ws_gate[i], ws_up[i], ws_down[i], ln2_g[i], ln2_b[i], alpha)
    return x2d.reshape(B, S, D)
```

```python
import functools
import math

import jax
import jax.numpy as jnp
from jax import lax
from jax.experimental import pallas as pl
from jax.experimental.pallas import tpu as pltpu

F32 = jnp.float32
BF16 = jnp.bfloat16
HIGHEST = lax.Precision.HIGHEST

HEAD_DIM = 64
MOBA_BLOCK = 256
MOBA_TOPK = 3
REL_MAX_DIST = 128
TOP_K = 8
N_GROUPS = 8
TOPK_GROUPS = 4
ROUTED_SCALE = 2.5
LN_EPS = 1e-5
NEG = -1e30

LANES = 128
HEADS_PER_TILE = LANES // HEAD_DIM
EXPERT_ROWS = 512
VMEM_LIMIT = 48 * 1024 * 1024


def _cparams(*sem):
    return pltpu.CompilerParams(dimension_semantics=sem, vmem_limit_bytes=VMEM_LIMIT)


def _nt_dot(a, b, **kw):
    return lax.dot_general(a, b, (((1,), (1,)), ((), ())), preferred_element_type=F32, **kw)


def _proj_kernel(x_ref, w_ref, o_ref):
    o_ref[...] = jnp.dot(x_ref[...].astype(BF16), w_ref[...],
                         preferred_element_type=F32).astype(o_ref.dtype)


def _project(x2d, w, tm, tn):
    T, K = x2d.shape
    N = w.shape[1]
    return pl.pallas_call(
        _proj_kernel,
        out_shape=jax.ShapeDtypeStruct((T, N), BF16),
        grid=(T // tm, N // tn),
        in_specs=[pl.BlockSpec((tm, K), lambda i, j: (i, 0)),
                  pl.BlockSpec((K, tn), lambda i, j: (0, j))],
        out_specs=pl.BlockSpec((tm, tn), lambda i, j: (i, j)),
        compiler_params=_cparams("parallel", "arbitrary"),
        name="k_proj",
    )(x2d, w)


def _proj_t_kernel(x_ref, wt_ref, o_ref):
    o_ref[...] = _nt_dot(wt_ref[...], x_ref[...].astype(BF16)).astype(o_ref.dtype)


def _project_t(x2d, wt, tm, tn):
    T, K = x2d.shape
    N = wt.shape[0]
    return pl.pallas_call(
        _proj_t_kernel,
        out_shape=jax.ShapeDtypeStruct((N, T), BF16),
        grid=(T // tm, N // tn),
        in_specs=[pl.BlockSpec((tm, K), lambda i, j: (i, 0)),
                  pl.BlockSpec((tn, K), lambda i, j: (j, 0))],
        out_specs=pl.BlockSpec((tn, tm), lambda i, j: (j, i)),
        compiler_params=_cparams("parallel", "arbitrary"),
        name="qv_proj_t",
    )(x2d, wt)


def _layer_norm(r, g, b):
    mu = jnp.mean(r, axis=-1, keepdims=True)
    d = r - mu
    var = jnp.mean(d * d, axis=-1, keepdims=True)
    return d * lax.rsqrt(var + LN_EPS) * g + b


def _outproj_ln_kernel(at_ref, w_ref, x_ref, g_ref, b_ref, o_ref, *, alpha):
    y = lax.dot_general(at_ref[...], w_ref[...], (((0,), (0,)), ((), ())),
                        preferred_element_type=F32)
    o_ref[...] = _layer_norm(alpha * x_ref[...] + y, g_ref[...], b_ref[...])


def _outproj_ln(attn_t, w_out, x2d, g, b, alpha, tm):
    T, D = x2d.shape
    row = lambda i: (i, 0)
    fixed = lambda i: (0, 0)
    return pl.pallas_call(
        functools.partial(_outproj_ln_kernel, alpha=alpha),
        out_shape=jax.ShapeDtypeStruct((T, D), F32),
        grid=(T // tm,),
        in_specs=[pl.BlockSpec((D, tm), lambda i: (0, i)), pl.BlockSpec((D, D), fixed),
                  pl.BlockSpec((tm, D), row), pl.BlockSpec((1, D), fixed),
                  pl.BlockSpec((1, D), fixed)],
        out_specs=pl.BlockSpec((tm, D), row),
        compiler_params=_cparams("parallel"),
        name="outproj_ln",
    )(attn_t, w_out, x2d, g.reshape(1, D), b.reshape(1, D))


def _spare(a, n=0):
    return ((a + 1) % HEADS_PER_TILE) * HEAD_DIM + n


def _scores(chains):
    out = []
    for blocks in chains:
        parts = []
        for keys, query, _, extra, mask in blocks:
            s = jnp.dot(keys(), query(), preferred_element_type=F32)
            if extra is not None:
                s = s + extra()
            if mask is not None:
                s = jnp.where(mask, s, NEG)
            parts.append(s)
        out.append(parts)
    return out


def _absorb(carries, scores, chains):
    probs = []
    for (m, _), parts in zip(carries, scores):
        tops = [jnp.max(s, axis=0, keepdims=True) for s in parts]
        m_new = functools.reduce(jnp.maximum, tops, m)
        probs.append((m_new, jnp.exp(m - m_new), [jnp.exp(s - m_new).astype(BF16) for s in parts]))
    out = []
    for (_, acc), (m_new, decay, ps), blocks in zip(carries, probs, chains):
        acc = decay * acc
        for (_, _, values, _, _), p in zip(blocks, ps):
            acc = acc + jnp.dot(values(), p, preferred_element_type=F32)
        out.append((m_new, acc))
    return tuple(out)


def _attend(carries, chains):
    return _absorb(carries, _scores(chains), chains)


def _attend_pipelined(carries, n, group, stage_refs):
    def stash(chains, ref):
        for a, parts in enumerate(_scores(chains)):
            for b, s in enumerate(parts):
                ref[a, b] = s

    def fetch(chains, ref):
        return [[ref[a, b] for b in range(len(blocks))] for a, blocks in enumerate(chains)]

    odd = n % 2
    carries = lax.cond(odd == 1, lambda c: _attend(c, group(0)), lambda c: c, carries)
    trips = n // 2
    last = n - 1

    def run(carries):
        stash(group(odd), stage_refs[0])

        def body(t, carries):
            g = odd + 2 * t
            stash(group(g + 1), stage_refs[1])
            carries = _absorb(carries, fetch(group(g), stage_refs[0]), group(g))
            stash(group(jnp.minimum(g + 2, last)), stage_refs[0])
            return _absorb(carries, fetch(group(g + 1), stage_refs[1]), group(g + 1))

        return lax.fori_loop(0, trips, body, carries)

    return lax.cond(trips >= 1, run, lambda c: c, carries)


GROUP_BLOCKS = 2


def _stage_scratch(tq):
    return [pltpu.VMEM((HEADS_PER_TILE, GROUP_BLOCKS, tq, tq), F32) for _ in range(2)]


def _scores_init(tq):
    return jnp.full((1, tq), -jnp.inf, F32), jnp.zeros((LANES, tq), F32)


def _finish(carries, o_ref):
    row = lax.broadcasted_iota(jnp.int32, carries[0][1].shape, 0)
    outs = []
    for a, (_, acc) in enumerate(carries):
        s = _spare(a)
        outs.append(acc / acc[s:s + 1, :])
    o_ref[...] = jnp.where(row < HEAD_DIM, outs[0], outs[1]).astype(o_ref.dtype)


def _fill_v_aug(vt_ref, vaug_ref, tk):
    n_tiles = vt_ref.shape[1] // tk
    row = lax.broadcasted_iota(jnp.int32, (LANES, tk), 0)
    for n in range(n_tiles):
        v = vt_ref[:, n * tk:(n + 1) * tk]
        for a in range(HEADS_PER_TILE):
            vaug_ref[a, n] = jnp.where(row // HEAD_DIM == a, v, jnp.ones_like(v))


FOX_SPLIT = 3


def _fox_kernel(qt_ref, k_ref, vt_ref, o_ref, vaug_ref, stage0_ref, stage1_ref, *, tq):
    i = pl.program_id(2)

    @pl.when(i == 0)
    def _():
        _fill_v_aug(vt_ref, vaug_ref, tq)

    row = lax.broadcasted_iota(jnp.int32, (LANES, tq), 0)
    qt = qt_ref[...]
    queries = []
    for a in range(HEADS_PER_TILE):
        offs = (row >= _spare(a)) & (row < _spare(a, FOX_SPLIT))
        queries.append(jnp.where(row // HEAD_DIM == a, qt,
                                 jnp.where(offs, -1.0, 0.0).astype(BF16)))
    key = lax.broadcasted_iota(jnp.int32, (tq, tq), 0)
    qry = lax.broadcasted_iota(jnp.int32, (tq, tq), 1)
    causal = key <= qry

    def block(a, j, mask=None):
        start = pl.multiple_of(j * tq, tq)
        return (lambda: k_ref[a, 0, pl.ds(start, tq), :], lambda: queries[a],
                lambda: vaug_ref[a, j], None, mask)

    heads = range(HEADS_PER_TILE)
    carries = lax.cond(
        i % 2 == 1,
        lambda c: _attend(c, [[block(a, i - 1), block(a, i, causal)] for a in heads]),
        lambda c: _attend(c, [[block(a, i, causal)] for a in heads]),
        tuple(_scores_init(tq) for _ in heads))
    carries = _attend_pipelined(
        carries, i // 2, lambda j: [[block(a, 2 * j), block(a, 2 * j + 1)] for a in heads],
        (stage0_ref, stage1_ref))
    _finish(carries, o_ref)


def _fox_attention(qv_t, k_aug, B, S, tq):
    D = k_aug.shape[-1]
    n_tiles = D // LANES
    nq = S // tq
    return pl.pallas_call(
        functools.partial(_fox_kernel, tq=tq),
        out_shape=jax.ShapeDtypeStruct((D, B * S), BF16),
        grid=(B, n_tiles, nq),
        in_specs=[pl.BlockSpec((LANES, tq), lambda b, h, i: (h, b * nq + i)),
                  pl.BlockSpec((HEADS_PER_TILE, 1, S, LANES), lambda b, h, i: (0, b, 0, h)),
                  pl.BlockSpec((LANES, S), lambda b, h, i: (n_tiles + h, b))],
        out_specs=pl.BlockSpec((LANES, tq), lambda b, h, i: (h, b * nq + i)),
        scratch_shapes=[pltpu.VMEM((HEADS_PER_TILE, nq, LANES, tq), BF16)] + _stage_scratch(tq),
        compiler_params=_cparams("parallel", "parallel", "arbitrary"),
        name="fox_attention",
    )(qv_t, k_aug, qv_t)


def _log_sigmoid(z):
    return jnp.minimum(z, 0.0) - jnp.log1p(jnp.exp(-jnp.abs(z)))


def _fox_gates_kernel(x_ref, w_ref, b_ref, o_ref, carry_ref, *, ts):
    @pl.when(pl.program_id(1) == 0)
    def _():
        carry_ref[...] = jnp.zeros_like(carry_ref)

    z = jnp.dot(x_ref[0], w_ref[...], preferred_element_type=F32, precision=HIGHEST)
    lf = _log_sigmoid(z + b_ref[...])
    r = lax.broadcasted_iota(jnp.int32, (ts, ts), 0)
    c = lax.broadcasted_iota(jnp.int32, (ts, ts), 1)
    tri = (c <= r).astype(F32)
    cum = jnp.dot(tri, lf, preferred_element_type=F32, precision=HIGHEST) + carry_ref[...]
    carry_ref[...] = cum[ts - 1:ts, :]
    o_ref[0] = cum


def _fox_gates(x3d, w_f, b_f, ts):
    B, S, D = x3d.shape
    H = w_f.shape[1]
    w_pad = jnp.zeros((D, LANES), F32).at[:, :H].set(w_f)
    b_pad = jnp.zeros((1, LANES), F32).at[0, :H].set(b_f)
    return pl.pallas_call(
        functools.partial(_fox_gates_kernel, ts=ts),
        out_shape=jax.ShapeDtypeStruct((B, S, LANES), F32),
        grid=(B, S // ts),
        in_specs=[pl.BlockSpec((1, ts, D), lambda b, s: (b, s, 0)),
                  pl.BlockSpec((D, LANES), lambda b, s: (0, 0)),
                  pl.BlockSpec((1, LANES), lambda b, s: (0, 0))],
        out_specs=pl.BlockSpec((1, ts, LANES), lambda b, s: (b, s, 0)),
        scratch_shapes=[pltpu.VMEM((1, LANES), F32)],
        compiler_params=_cparams("parallel", "arbitrary"),
        name="fox_gates",
    )(x3d, w_pad, b_pad)


def _fox_kprep_kernel(k_ref, c_ref, o_ref):
    tm, D = k_ref.shape
    lane = lax.broadcasted_iota(jnp.int32, (tm, LANES), 1)
    for t in range(D // LANES):
        k = k_ref[:, t * LANES:(t + 1) * LANES].astype(F32)
        for a in range(HEADS_PER_TILE):
            h = t * HEADS_PER_TILE + a
            rest = c_ref[:, h:h + 1]
            aug = jnp.zeros((tm, LANES), F32)
            for n in range(FOX_SPLIT):
                piece = rest.astype(BF16).astype(F32)
                rest = rest - piece
                aug = jnp.where(lane == _spare(a, n), piece, aug)
            o_ref[a, :, t * LANES:(t + 1) * LANES] = jnp.where(lane // HEAD_DIM == a, k,
                                                               aug).astype(BF16)


def _fox_kprep(k2d, c_tok, tm):
    T, D = k2d.shape
    return pl.pallas_call(
        _fox_kprep_kernel,
        out_shape=jax.ShapeDtypeStruct((HEADS_PER_TILE, T, D), BF16),
        grid=(T // tm,),
        in_specs=[pl.BlockSpec((tm, D), lambda i: (i, 0)),
                  pl.BlockSpec((tm, LANES), lambda i: (i, 0))],
        out_specs=pl.BlockSpec((HEADS_PER_TILE, tm, D), lambda i: (0, i, 0)),
        compiler_params=_cparams("parallel"),
        name="fox_kprep",
    )(k2d, c_tok)


MOBA_SLAB = 16


def _moba_kernel(bfar_ref, qt_ref, k_ref, vt_ref, kmean_ref, bown_ref, badj_ref, o_ref,
                 vaug_ref, sel_ref, stage0_ref, stage1_ref, *, blk):
    t = pl.program_id(1)
    i = pl.program_id(2)
    nb = kmean_ref.shape[1]

    @pl.when(i == 0)
    def _():
        _fill_v_aug(vt_ref, vaug_ref, blk)

    row = lax.broadcasted_iota(jnp.int32, (LANES, blk), 0)
    brow = lax.broadcasted_iota(jnp.int32, (nb, blk), 0)
    qt = qt_ref[...]
    zero = jnp.zeros_like(qt)
    base, slabs = [], []
    for a in range(HEADS_PER_TILE):
        qa = jnp.where(row // HEAD_DIM == a, qt, zero)
        base.append(qa)
        bscore = jnp.dot(kmean_ref[0], qa.astype(F32), preferred_element_type=F32,
                         precision=HIGHEST)
        cand = jnp.where(brow < i, bscore, NEG)
        sel = jnp.zeros((nb, blk), F32)
        for _ in range(MOBA_TOPK):
            mx = jnp.max(cand, axis=0, keepdims=True)
            first = jnp.min(jnp.where(cand == mx, brow, nb), axis=0, keepdims=True)
            hit = brow == first
            sel = jnp.where(hit, 1.0, sel)
            cand = jnp.where(hit, -jnp.inf, cand)
        sel_ref[a] = jnp.where(brow < i, sel, 0.0)
        slabs.append(_spare(a) // MOBA_SLAB * MOBA_SLAB)

    srow = lax.broadcasted_iota(jnp.int32, (MOBA_SLAB, blk), 0)

    def query(a, j, bias_hi, bias_lo):
        s0 = slabs[a]
        chosen = sel_ref[a, pl.ds(j, 1), :] > 0.0
        slab = jnp.where(srow == _spare(a) - s0, jnp.where(chosen, bias_hi, NEG),
                         jnp.where(srow == _spare(a, 1) - s0, bias_lo, 0.0)).astype(BF16)
        head = [base[a][:s0]] if s0 else []
        tail = [base[a][s0 + MOBA_SLAB:]] if s0 + MOBA_SLAB < LANES else []
        return jnp.concatenate(head + [slab] + tail, axis=0)

    key = lax.broadcasted_iota(jnp.int32, (blk, blk), 0)
    qry = lax.broadcasted_iota(jnp.int32, (blk, blk), 1)
    causal = key <= qry
    heads = range(HEADS_PER_TILE)

    def keys(a, j):
        start = pl.multiple_of(j * blk, blk)
        return lambda: k_ref[a, 0, pl.ds(start, blk), :]

    def own(a):
        return keys(a, i), lambda: base[a], lambda: vaug_ref[a, i], lambda: bown_ref[a], causal

    def prev(a):
        return (keys(a, i - 1), lambda: query(a, i - 1, 0.0, 0.0), lambda: vaug_ref[a, i - 1],
                lambda: badj_ref[a], None)

    def far(a, j):
        h = t * HEADS_PER_TILE + a
        return (keys(a, j), lambda: query(a, j, bfar_ref[0, h], bfar_ref[1, h]),
                lambda: vaug_ref[a, j], None, None)

    carries = lax.cond(
        i >= 1,
        lambda c: _attend(c, [[prev(a), own(a)] for a in heads]),
        lambda c: _attend(c, [[own(a)] for a in heads]),
        tuple(_scores_init(blk) for _ in heads))
    carries = lax.cond(
        (i >= 2) & (i % 2 == 0),
        lambda c: _attend(c, [[far(a, i - 2)] for a in heads]),
        lambda c: c, carries)
    carries = _attend_pipelined(
        carries, (i - 1) // 2, lambda j: [[far(a, 2 * j), far(a, 2 * j + 1)] for a in heads],
        (stage0_ref, stage1_ref))
    _finish(carries, o_ref)


def _rel_bucket(dist, n_buckets):
    n = jnp.maximum(dist, 0)
    max_exact = n_buckets // 2
    nf = jnp.maximum(n, 1).astype(F32)
    large = max_exact + (jnp.log(nf / max_exact) / math.log(REL_MAX_DIST / max_exact)
                         * (n_buckets - max_exact)).astype(jnp.int32)
    large = jnp.minimum(large, n_buckets - 1)
    return jnp.where(n < max_exact, n, large)


def _moba_attention(qv_t, k_aug, k_mean, rel_bias, B, S):
    D = k_aug.shape[-1]
    n_tiles = D // LANES
    blk = MOBA_BLOCK
    nb = S // blk
    hp = HEADS_PER_TILE
    r = jnp.arange(blk)
    delta = r[None, :] - r[:, None]
    bias_t = rel_bias.T.astype(F32)
    n_buckets = rel_bias.shape[0]
    def table(dist):
        onehot = jax.nn.one_hot(_rel_bucket(dist, n_buckets), n_buckets, dtype=F32)
        return jnp.einsum('crn,hn->hcr', onehot, bias_t, precision=HIGHEST)

    b_own = table(delta)
    b_adj = table(delta + blk)
    b_far = bias_t[:, n_buckets - 1]
    far_hi = b_far.astype(BF16).astype(F32)
    far_lo = (b_far - far_hi).astype(BF16).astype(F32)
    b_far2 = jnp.stack([far_hi, far_lo])
    return pl.pallas_call(
        functools.partial(_moba_kernel, blk=blk),
        out_shape=jax.ShapeDtypeStruct((D, B * S), BF16),
        grid_spec=pltpu.PrefetchScalarGridSpec(
            num_scalar_prefetch=1,
            grid=(B, n_tiles, nb),
            in_specs=[pl.BlockSpec((LANES, blk), lambda b, h, i, f: (h, b * nb + i)),
                      pl.BlockSpec((hp, 1, S, LANES), lambda b, h, i, f: (0, b, 0, h)),
                      pl.BlockSpec((LANES, S), lambda b, h, i, f: (n_tiles + h, b)),
                      pl.BlockSpec((1, nb, LANES), lambda b, h, i, f: (b, 0, h)),
                      pl.BlockSpec((hp, blk, blk), lambda b, h, i, f: (h, 0, 0)),
                      pl.BlockSpec((hp, blk, blk), lambda b, h, i, f: (h, 0, 0))],
            out_specs=pl.BlockSpec((LANES, blk), lambda b, h, i, f: (h, b * nb + i)),
            scratch_shapes=[pltpu.VMEM((hp, nb, LANES, blk), BF16),
                            pltpu.VMEM((hp, nb, blk), F32)] + _stage_scratch(blk)),
        compiler_params=_cparams("parallel", "parallel", "arbitrary"),
        name="moba_attention",
    )(b_far2, qv_t, k_aug, qv_t, k_mean, b_own, b_adj)


def _moba_kprep_kernel(k_ref, o_ref, mean_ref):
    k = k_ref[0]
    col = lax.broadcasted_iota(jnp.int32, k.shape, 1)
    lane = col % LANES
    for a in range(HEADS_PER_TILE):
        ones = (lane == _spare(a)) | (lane == _spare(a, 1))
        o_ref[a, 0] = jnp.where((col // HEAD_DIM) % HEADS_PER_TILE == a, k,
                                jnp.where(ones, 1.0, 0.0).astype(BF16))
    mean_ref[0, 0] = jnp.mean(k.astype(F32), axis=0, keepdims=True)


def _moba_kprep(k3d):
    B, S, D = k3d.shape
    blk = MOBA_BLOCK
    nb = S // blk
    k_aug, k_mean = pl.pallas_call(
        _moba_kprep_kernel,
        out_shape=(jax.ShapeDtypeStruct((HEADS_PER_TILE, B, S, D), BF16),
                   jax.ShapeDtypeStruct((B, nb, 1, D), F32)),
        grid=(B, nb),
        in_specs=[pl.BlockSpec((1, blk, D), lambda b, n: (b, n, 0))],
        out_specs=(pl.BlockSpec((HEADS_PER_TILE, 1, blk, D), lambda b, n: (0, b, n, 0)),
                   pl.BlockSpec((1, 1, 1, D), lambda b, n: (b, n, 0, 0))),
        compiler_params=_cparams("parallel", "parallel"),
        name="moba_kprep",
    )(k3d)
    return k_aug, k_mean.reshape(B, nb, D)


def _router_kernel(x_ref, wt_ref, bias_ref, idx_ref, gate_ref, rank_ref, cnt_ref, run_ref, *, tm):
    @pl.when(pl.program_id(0) == 0)
    def _():
        run_ref[...] = jnp.zeros_like(run_ref)

    E = wt_ref.shape[0]
    gsz = E // N_GROUPS
    logits = _nt_dot(wt_ref[...], x_ref[...], precision=HIGHEST)
    s = jax.nn.sigmoid(logits)
    sb = s + bias_ref[...]
    neg_inf = -jnp.inf

    giota = lax.broadcasted_iota(jnp.int32, (gsz, tm), 0)
    gscore = []
    for g in range(N_GROUPS):
        blk = sb[g * gsz:(g + 1) * gsz, :]
        m1 = jnp.max(blk, axis=0, keepdims=True)
        i1 = jnp.min(jnp.where(blk == m1, giota, gsz), axis=0, keepdims=True)
        m2 = jnp.max(jnp.where(giota == i1, neg_inf, blk), axis=0, keepdims=True)
        gscore.append(m1 + m2)

    gsel = [jnp.zeros((1, tm), jnp.bool_) for _ in range(N_GROUPS)]
    for _ in range(TOPK_GROUPS):
        mx = functools.reduce(jnp.maximum, gscore)
        found = jnp.zeros((1, tm), jnp.bool_)
        for g in range(N_GROUPS):
            hit = (gscore[g] == mx) & jnp.logical_not(found)
            gsel[g] = gsel[g] | hit
            found = found | hit
            gscore[g] = jnp.where(hit, neg_inf, gscore[g])
    emask = jnp.concatenate([jnp.broadcast_to(gsel[g], (gsz, tm)) for g in range(N_GROUPS)], axis=0)
    cand = jnp.where(emask, sb, NEG)

    eiota = lax.broadcasted_iota(jnp.int32, (E, tm), 0)
    hits, idxs, ws = [], [], []
    for _ in range(TOP_K):
        mx = jnp.max(cand, axis=0, keepdims=True)
        first = jnp.min(jnp.where(cand == mx, eiota, E), axis=0, keepdims=True)
        hit = eiota == first
        hits.append(hit)
        idxs.append(first)
        ws.append(jnp.sum(jnp.where(hit, s, 0.0), axis=0, keepdims=True))
        cand = jnp.where(hit, neg_inf, cand)
    wsum = functools.reduce(jnp.add, ws)

    chosen = functools.reduce(jnp.logical_or, hits)
    onehot = jnp.where(chosen, 1.0, 0.0)
    tr = lax.broadcasted_iota(jnp.int32, (tm, tm), 0)
    tc = lax.broadcasted_iota(jnp.int32, (tm, tm), 1)
    before = (tr < tc).astype(BF16)
    prior = jnp.dot(onehot.astype(BF16), before, preferred_element_type=F32) + run_ref[...]
    grow = lax.broadcasted_iota(jnp.int32, (LANES, tm), 0)
    gates = jnp.zeros((LANES, tm), F32)
    for k in range(TOP_K):
        idx_ref[k:k + 1, :] = idxs[k]
        gates = jnp.where(grow == k, ws[k] / wsum * ROUTED_SCALE, gates)
        rank_ref[k:k + 1, :] = jnp.sum(jnp.where(hits[k], prior, 0.0), axis=0,
                                       keepdims=True).astype(jnp.int32)
    gate_ref[...] = gates.T
    run_ref[...] = run_ref[...] + jnp.sum(onehot, axis=1, keepdims=True)
    cnt_ref[...] = run_ref[...]


def _router(x2d, w_router, router_bias, tm):
    T, D = x2d.shape
    E = w_router.shape[1]
    tok = lambda i: (0, i)
    fixed = lambda i: (0, 0)
    return pl.pallas_call(
        functools.partial(_router_kernel, tm=tm),
        out_shape=(jax.ShapeDtypeStruct((TOP_K, T), jnp.int32),
                   jax.ShapeDtypeStruct((T, LANES), F32),
                   jax.ShapeDtypeStruct((TOP_K, T), jnp.int32),
                   jax.ShapeDtypeStruct((E, 1), F32)),
        grid=(T // tm,),
        in_specs=[pl.BlockSpec((tm, D), lambda i: (i, 0)),
                  pl.BlockSpec((E, D), fixed),
                  pl.BlockSpec((E, 1), fixed)],
        out_specs=(pl.BlockSpec((TOP_K, tm), tok), pl.BlockSpec((tm, LANES), lambda i: (i, 0)),
                   pl.BlockSpec((TOP_K, tm), tok), pl.BlockSpec((E, 1), fixed)),
        scratch_shapes=[pltpu.VMEM((E, 1), F32)],
        compiler_params=_cparams("arbitrary"),
        name="router",
    )(x2d, w_router.T, router_bias.reshape(E, 1).astype(F32))


def _slots_kernel(idx_ref, rank_ref, start_ref, o_ref):
    E = start_ref.shape[0]
    tm = idx_ref.shape[1]
    eiota = lax.broadcasted_iota(jnp.int32, (E, tm), 0)
    start = start_ref[...]
    for k in range(TOP_K):
        base = jnp.sum(jnp.where(eiota == idx_ref[k:k + 1, :], start, 0.0), axis=0, keepdims=True)
        o_ref[k:k + 1, :] = base.astype(jnp.int32) + rank_ref[k:k + 1, :]


def _slots(idx, rank, seg_start, tm):
    T = idx.shape[1]
    E = seg_start.shape[0]
    tok = lambda i: (0, i)
    return pl.pallas_call(
        _slots_kernel,
        out_shape=jax.ShapeDtypeStruct((TOP_K, T), jnp.int32),
        grid=(T // tm,),
        in_specs=[pl.BlockSpec((TOP_K, tm), tok), pl.BlockSpec((TOP_K, tm), tok),
                  pl.BlockSpec((E, 1), lambda i: (0, 0))],
        out_specs=pl.BlockSpec((TOP_K, tm), tok),
        compiler_params=_cparams("parallel"),
        name="moe_slots",
    )(idx, rank, seg_start.astype(F32).reshape(E, 1))


def _row_copy(src_ref, src_row, dst_ref, dst_row, sem):
    return pltpu.make_async_copy(src_ref.at[pl.ds(src_row, 1), :],
                                 dst_ref.at[pl.ds(dst_row, 1), :], sem)


def _dispatch_kernel(seg_ref, dest_ref, x_ref, xs_ref, zero_ref, zsem, sem, *, tt):
    E = seg_ref.shape[1]

    def zero_copy(e):
        start = pl.multiple_of(seg_ref[1, e] - EXPERT_ROWS, EXPERT_ROWS)
        return pltpu.make_async_copy(zero_ref, xs_ref.at[pl.ds(start, EXPERT_ROWS), :], zsem)

    @pl.when(pl.program_id(0) == 0)
    def _():
        zero_ref[...] = jnp.zeros_like(zero_ref)
        for go in (lambda c: c.start(), lambda c: c.wait()):
            def body(e, carry, go=go):
                @pl.when(seg_ref[1, e] > seg_ref[0, e])
                def _():
                    go(zero_copy(e))
                return carry
            lax.fori_loop(0, E, body, 0)

    def issue(r, c):
        for k in range(TOP_K):
            _row_copy(x_ref, r, xs_ref, dest_ref[k, r], sem).start()
        return c

    def drain(r, c):
        for k in range(TOP_K):
            _row_copy(x_ref, 0, xs_ref, 0, sem).wait()
        return c

    lax.fori_loop(0, tt, issue, 0)
    lax.fori_loop(0, tt, drain, 0)


def _dispatch(x2d, dest, seg, n_rows, tt):
    T, D = x2d.shape
    return pl.pallas_call(
        functools.partial(_dispatch_kernel, tt=tt),
        out_shape=jax.ShapeDtypeStruct((n_rows, D), F32),
        grid_spec=pltpu.PrefetchScalarGridSpec(
            num_scalar_prefetch=1,
            grid=(T // tt,),
            in_specs=[pl.BlockSpec((TOP_K, tt), lambda i, seg: (0, i), memory_space=pltpu.SMEM),
                      pl.BlockSpec((tt, D), lambda i, seg: (i, 0))],
            out_specs=pl.BlockSpec(memory_space=pl.ANY),
            scratch_shapes=[pltpu.VMEM((EXPERT_ROWS, D), F32), pltpu.SemaphoreType.DMA,
                            pltpu.SemaphoreType.DMA]),
        compiler_params=_cparams("arbitrary"),
        name="moe_dispatch",
    )(seg, dest, x2d)


def _silu(g):
    return g * jax.nn.sigmoid(g)


SCHED_FIRST, SCHED_SLOT, SCHED_NEXT = 0, 1, 2


def _expert_kernel(sched_ref, n_used_ref, x_ref, wg_hbm, wu_hbm, wd_hbm, o_ref,
                   wg_buf, wu_buf, wd_buf, sem, *, layer, first_expert_row):
    i = pl.program_id(0)
    slot = sched_ref[SCHED_SLOT, i]

    def copies(expert, slot):
        pairs = ((wg_hbm, wg_buf), (wu_hbm, wu_buf), (wd_hbm, wd_buf))
        return [pltpu.make_async_copy(src.at[layer, expert], dst.at[slot], sem.at[slot, n])
                for n, (src, dst) in enumerate(pairs)]

    @pl.when(i == 0)
    def _():
        for c in copies(sched_ref[first_expert_row, 0], 0):
            c.start()

    @pl.when(sched_ref[SCHED_FIRST, i] == 1)
    def _():
        for c in copies(0, slot):
            c.wait()

        @pl.when(sched_ref[SCHED_NEXT, i] >= 0)
        def _():
            for c in copies(sched_ref[SCHED_NEXT, i], 1 - slot):
                c.start()

    @pl.when(i < n_used_ref[0])
    def _():
        x = x_ref[...].astype(BF16)
        g = jnp.dot(x, wg_buf[slot].astype(BF16), preferred_element_type=F32)
        u = jnp.dot(x, wu_buf[slot].astype(BF16), preferred_element_type=F32)
        a = (_silu(g) * u).astype(BF16)
        o_ref[...] = jnp.dot(a, wd_buf[slot].astype(BF16), preferred_element_type=F32)

    @pl.when(i >= n_used_ref[0])
    def _():
        o_ref[...] = jnp.zeros_like(o_ref)


def _expert_schedule(seg_start, seg_end, n_blocks):
    E = seg_end.shape[0]
    first_row = jnp.arange(n_blocks, dtype=jnp.int32) * EXPERT_ROWS
    blk_e = jnp.minimum(jnp.sum(seg_end[None, :] <= first_row[:, None], axis=1), E - 1)
    used = first_row < seg_end[-1]
    first = used & (first_row == seg_start[blk_e])
    ordinal = jnp.cumsum(first) - 1
    ids = jnp.where(seg_end > seg_start, jnp.arange(E), E)
    later = lax.cummin(ids, reverse=True)
    nxt = jnp.concatenate([later[1:], jnp.full((1,), E, later.dtype)])[blk_e]
    nxt = jnp.where(nxt < E, nxt, -1)
    return jnp.stack([first, ordinal % 2, nxt, blk_e]).astype(jnp.int32)


def _expert_mlp(xs, sched, n_used, w_gate, w_up, w_down, layer):
    n_rows, D = xs.shape
    F = w_gate.shape[-1]
    tm = EXPERT_ROWS
    hbm = pl.BlockSpec(memory_space=pl.ANY)
    return pl.pallas_call(
        functools.partial(_expert_kernel, layer=layer, first_expert_row=sched.shape[0] - 1),
        out_shape=jax.ShapeDtypeStruct((n_rows, D), F32),
        grid_spec=pltpu.PrefetchScalarGridSpec(
            num_scalar_prefetch=2,
            grid=(n_rows // tm,),
            in_specs=[pl.BlockSpec((tm, D), lambda i, sc, nu: (jnp.minimum(i, nu[0] - 1), 0)),
                      hbm, hbm, hbm],
            out_specs=pl.BlockSpec((tm, D), lambda i, sc, nu: (i, 0)),
            scratch_shapes=[pltpu.VMEM((2, D, F), F32), pltpu.VMEM((2, D, F), F32),
                            pltpu.VMEM((2, F, D), F32), pltpu.SemaphoreType.DMA((2, 3))]),
        compiler_params=_cparams("arbitrary"),
        name="moe_experts",
    )(sched, n_used, xs, w_gate, w_up, w_down)


def _combine_kernel(dest_ref, dest_next_ref, x_ref, gate_ref, wsg_ref, wsu_ref, wsd_ref, g_ref,
                    b_ref, ys_ref, o_ref, rows_ref, sem, *, tt, alpha):
    i = pl.program_id(0)
    slot = i % 2

    def gather(dests, slot):
        def issue(r, c):
            for k in range(TOP_K):
                _row_copy(ys_ref, dests[k, r], rows_ref.at[slot, k], r, sem.at[slot]).start()
            return c
        lax.fori_loop(0, tt, issue, 0)

    def drain(r, c):
        for k in range(TOP_K):
            _row_copy(ys_ref, 0, rows_ref.at[slot, 0], 0, sem.at[slot]).wait()
        return c

    @pl.when(i == 0)
    def _():
        gather(dest_ref, slot)

    @pl.when(i + 1 < pl.num_programs(0))
    def _():
        gather(dest_next_ref, 1 - slot)

    x = x_ref[...]
    xb = x.astype(BF16)
    g = jnp.dot(xb, wsg_ref[...], preferred_element_type=F32)
    u = jnp.dot(xb, wsu_ref[...], preferred_element_type=F32)
    y = jnp.dot((_silu(g) * u).astype(BF16), wsd_ref[...], preferred_element_type=F32)
    lax.fori_loop(0, tt, drain, 0)
    gates = gate_ref[...]
    for k in range(TOP_K):
        y = y + rows_ref[slot, k] * gates[:, k:k + 1]
    o_ref[...] = _layer_norm(alpha * x + y, g_ref[...], b_ref[...])


def _combine(x2d, ys, dest, gates_tok, ws_gate, ws_up, ws_down, g, b, alpha, tt):
    T, D = x2d.shape
    F = ws_gate.shape[1]
    row = lambda i: (i, 0)
    fixed = lambda i: (0, 0)
    n = T // tt
    return pl.pallas_call(
        functools.partial(_combine_kernel, tt=tt, alpha=alpha),
        out_shape=jax.ShapeDtypeStruct((T, D), F32),
        grid=(n,),
        in_specs=[pl.BlockSpec((TOP_K, tt), lambda i: (0, i), memory_space=pltpu.SMEM),
                  pl.BlockSpec((TOP_K, tt), lambda i: (0, jnp.minimum(i + 1, n - 1)),
                               memory_space=pltpu.SMEM),
                  pl.BlockSpec((tt, D), row),
                  pl.BlockSpec((tt, LANES), row),
                  pl.BlockSpec((D, F), fixed), pl.BlockSpec((D, F), fixed),
                  pl.BlockSpec((F, D), fixed),
                  pl.BlockSpec((1, D), fixed), pl.BlockSpec((1, D), fixed),
                  pl.BlockSpec(memory_space=pl.ANY)],
        out_specs=pl.BlockSpec((tt, D), row),
        scratch_shapes=[pltpu.VMEM((2, TOP_K, tt, D), F32), pltpu.SemaphoreType.DMA((2,))],
        compiler_params=_cparams("arbitrary"),
        name="moe_combine",
    )(dest, dest, x2d, gates_tok, ws_gate.astype(BF16), ws_up.astype(BF16), ws_down.astype(BF16),
      g.reshape(1, D), b.reshape(1, D), ys)


def _pick(n, pref):
    t = min(n, pref)
    while n % t:
        t //= 2
    return t


def _moe_layer(x2d, w_router, router_bias, w_gate, w_up, w_down, layer, ws_gate, ws_up, ws_down,
               g, b, alpha):
    T, D = x2d.shape
    E = w_router.shape[1]
    idx, gates, rank, counts = _router(x2d, w_router, router_bias, _pick(T, 256))
    counts = counts[:, 0].astype(jnp.int32)
    padded = (counts + EXPERT_ROWS - 1) // EXPERT_ROWS * EXPERT_ROWS
    seg_end = jnp.cumsum(padded)
    seg_start = seg_end - padded
    n_blocks = (T * TOP_K + E * (EXPERT_ROWS - 1)) // EXPERT_ROWS
    dest = _slots(idx, rank, seg_start, _pick(T, 1024))
    sched = _expert_schedule(seg_start, seg_end, n_blocks)
    n_used = (seg_end[-1:] // EXPERT_ROWS).astype(jnp.int32)
    tt = _pick(T, 128)
    xs = _dispatch(x2d, dest, jnp.stack([seg_start, seg_end]).astype(jnp.int32),
                   n_blocks * EXPERT_ROWS, tt)
    ys = _expert_mlp(xs, sched, n_used, w_gate, w_up, w_down, layer)
    return _combine(x2d, ys, dest, gates, ws_gate, ws_up, ws_down, g, b, alpha, tt)


def _mixer_layer(x2d, i, B, S, p, alpha):
    T, D = x2d.shape
    tm = _pick(T, 512)
    tn = _pick(D, 1024)
    j = i // 2
    w_in = p["moba_w_in"][j] if i % 2 == 0 else p["fox_w_in"][j]
    w_qv_t = jnp.concatenate([w_in[:, :D].T * HEAD_DIM ** -0.5, w_in[:, 2 * D:3 * D].T]).astype(BF16)
    qv_t = _project_t(x2d, w_qv_t, tm, tn)
    k = _project(x2d, w_in[:, D:2 * D].astype(BF16), tm, tn)
    if i % 2 == 0:
        k_aug, k_mean = _moba_kprep(k.reshape(B, S, D))
        attn_t = _moba_attention(qv_t, k_aug, k_mean, p["rel_bias"], B, S)
        w_out = p["moba_w_out"][j]
    else:
        c_tok = _fox_gates(x2d.reshape(B, S, D), w_in[:, 3 * D:], p["fox_b_f"][j], _pick(S, 512))
        k_aug = _fox_kprep(k, c_tok.reshape(T, LANES), tm).reshape(HEADS_PER_TILE, B, S, D)
        attn_t = _fox_attention(qv_t, k_aug, B, S, _pick(S, 256))
        w_out = p["fox_w_out"][j]
    return _outproj_ln(attn_t, w_out.astype(BF16), x2d, p["ln1_g"][i], p["ln1_b"][i], alpha, tm)


def kernel(x, rel_bias, moba_w_in, moba_w_out, fox_w_in, fox_b_f, fox_w_out, ln1_g, ln1_b, ln2_g,
           ln2_b, w_router, router_bias, w_gate, w_up, w_down, ws_gate, ws_up, ws_down):
    B, S, D = x.shape
    depth = ln1_g.shape[0]
    alpha = (2 * depth) ** 0.25
    p = dict(rel_bias=rel_bias, moba_w_in=moba_w_in, moba_w_out=moba_w_out, fox_w_in=fox_w_in,
             fox_b_f=fox_b_f, fox_w_out=fox_w_out, ln1_g=ln1_g, ln1_b=ln1_b)
    x2d = x.reshape(B * S, D)
    for i in range(depth):
        x2d = _mixer_layer(x2d, i, B, S, p, alpha)
        x2d = _moe_layer(x2d, w_router[i], router_bias[i], w_gate, w_up, w_down, i,
                         ws_gate[i], ws_up[i], ws_down[i], ln2_g[i], ln2_b[i], alpha)
    return x2d.reshape(B, S, D)
```

```python
import functools
import math

import jax
import jax.numpy as jnp
from jax import lax
from jax.experimental import pallas as pl
from jax.experimental.pallas import tpu as pltpu

F32 = jnp.float32
BF16 = jnp.bfloat16
HIGHEST = lax.Precision.HIGHEST

HEAD_DIM = 64
MOBA_BLOCK = 256
MOBA_TOPK = 3
REL_MAX_DIST = 128
TOP_K = 8
N_GROUPS = 8
TOPK_GROUPS = 4
ROUTED_SCALE = 2.5
LN_EPS = 1e-5
NEG = -1e30

LANES = 128
HEADS_PER_TILE = LANES // HEAD_DIM
EXPERT_ROWS = 512
VMEM_LIMIT = 48 * 1024 * 1024


def _cparams(*sem):
    return pltpu.CompilerParams(dimension_semantics=sem, vmem_limit_bytes=VMEM_LIMIT)


def _nt_dot(a, b, **kw):
    return lax.dot_general(a, b, (((1,), (1,)), ((), ())), preferred_element_type=F32, **kw)


def _proj_kernel(x_ref, w_ref, o_ref):
    o_ref[...] = jnp.dot(x_ref[...].astype(BF16), w_ref[...],
                         preferred_element_type=F32).astype(o_ref.dtype)


def _project(x2d, w, tm, tn):
    T, K = x2d.shape
    N = w.shape[1]
    return pl.pallas_call(
        _proj_kernel,
        out_shape=jax.ShapeDtypeStruct((T, N), BF16),
        grid=(T // tm, N // tn),
        in_specs=[pl.BlockSpec((tm, K), lambda i, j: (i, 0)),
                  pl.BlockSpec((K, tn), lambda i, j: (0, j))],
        out_specs=pl.BlockSpec((tm, tn), lambda i, j: (i, j)),
        compiler_params=_cparams("parallel", "arbitrary"),
        name="k_proj",
    )(x2d, w)


def _proj_t_kernel(x_ref, wt_ref, o_ref):
    o_ref[...] = _nt_dot(wt_ref[...], x_ref[...].astype(BF16)).astype(o_ref.dtype)


def _project_t(x2d, wt, tm, tn):
    T, K = x2d.shape
    N = wt.shape[0]
    return pl.pallas_call(
        _proj_t_kernel,
        out_shape=jax.ShapeDtypeStruct((N, T), BF16),
        grid=(T // tm, N // tn),
        in_specs=[pl.BlockSpec((tm, K), lambda i, j: (i, 0)),
                  pl.BlockSpec((tn, K), lambda i, j: (j, 0))],
        out_specs=pl.BlockSpec((tn, tm), lambda i, j: (j, i)),
        compiler_params=_cparams("parallel", "arbitrary"),
        name="qv_proj_t",
    )(x2d, wt)


def _layer_norm(r, g, b):
    mu = jnp.mean(r, axis=-1, keepdims=True)
    d = r - mu
    var = jnp.mean(d * d, axis=-1, keepdims=True)
    return d * lax.rsqrt(var + LN_EPS) * g + b


def _outproj_ln_kernel(at_ref, w_ref, x_ref, g_ref, b_ref, o_ref, *, alpha):
    y = lax.dot_general(at_ref[...], w_ref[...], (((0,), (0,)), ((), ())),
                        preferred_element_type=F32)
    o_ref[...] = _layer_norm(alpha * x_ref[...] + y, g_ref[...], b_ref[...])


def _outproj_ln(attn_t, w_out, x2d, g, b, alpha, tm):
    T, D = x2d.shape
    row = lambda i: (i, 0)
    fixed = lambda i: (0, 0)
    return pl.pallas_call(
        functools.partial(_outproj_ln_kernel, alpha=alpha),
        out_shape=jax.ShapeDtypeStruct((T, D), F32),
        grid=(T // tm,),
        in_specs=[pl.BlockSpec((D, tm), lambda i: (0, i)), pl.BlockSpec((D, D), fixed),
                  pl.BlockSpec((tm, D), row), pl.BlockSpec((1, D), fixed),
                  pl.BlockSpec((1, D), fixed)],
        out_specs=pl.BlockSpec((tm, D), row),
        compiler_params=_cparams("parallel"),
        name="outproj_ln",
    )(attn_t, w_out, x2d, g.reshape(1, D), b.reshape(1, D))


def _spare(a, n=0):
    return ((a + 1) % HEADS_PER_TILE) * HEAD_DIM + n


def _scores(chains):
    out = []
    for blocks in chains:
        parts = []
        for keys, query, _, extra, mask in blocks:
            s = jnp.dot(keys(), query(), preferred_element_type=F32)
            if extra is not None:
                s = s + extra()
            if mask is not None:
                s = jnp.where(mask, s, NEG)
            parts.append(s)
        out.append(parts)
    return out


def _absorb(carries, scores, chains):
    probs = []
    for (m, _), parts in zip(carries, scores):
        tops = [jnp.max(s, axis=0, keepdims=True) for s in parts]
        m_new = functools.reduce(jnp.maximum, tops, m)
        probs.append((m_new, jnp.exp(m - m_new), [jnp.exp(s - m_new).astype(BF16) for s in parts]))
    out = []
    for (_, acc), (m_new, decay, ps), blocks in zip(carries, probs, chains):
        acc = decay * acc
        for (_, _, values, _, _), p in zip(blocks, ps):
            acc = acc + jnp.dot(values(), p, preferred_element_type=F32)
        out.append((m_new, acc))
    return tuple(out)


def _attend(carries, chains):
    return _absorb(carries, _scores(chains), chains)


def _attend_pipelined(carries, n, group, stage_refs):
    def stash(chains, ref):
        for a, parts in enumerate(_scores(chains)):
            for b, s in enumerate(parts):
                ref[a, b] = s

    def fetch(chains, ref):
        return [[ref[a, b] for b in range(len(blocks))] for a, blocks in enumerate(chains)]

    odd = n % 2
    carries = lax.cond(odd == 1, lambda c: _attend(c, group(0)), lambda c: c, carries)
    trips = n // 2
    last = n - 1

    def run(carries):
        stash(group(odd), stage_refs[0])

        def body(t, carries):
            g = odd + 2 * t
            stash(group(g + 1), stage_refs[1])
            carries = _absorb(carries, fetch(group(g), stage_refs[0]), group(g))
            stash(group(jnp.minimum(g + 2, last)), stage_refs[0])
            return _absorb(carries, fetch(group(g + 1), stage_refs[1]), group(g + 1))

        return lax.fori_loop(0, trips, body, carries)

    return lax.cond(trips >= 1, run, lambda c: c, carries)


GROUP_BLOCKS = 2


def _stage_scratch(tq):
    return [pltpu.VMEM((HEADS_PER_TILE, GROUP_BLOCKS, tq, tq), F32) for _ in range(2)]


def _scores_init(tq):
    return jnp.full((1, tq), -jnp.inf, F32), jnp.zeros((LANES, tq), F32)


def _finish(carries, o_ref):
    row = lax.broadcasted_iota(jnp.int32, carries[0][1].shape, 0)
    outs = []
    for a, (_, acc) in enumerate(carries):
        s = _spare(a)
        outs.append(acc / acc[s:s + 1, :])
    o_ref[...] = jnp.where(row < HEAD_DIM, outs[0], outs[1]).astype(o_ref.dtype)


def _fill_v_aug(vt_ref, vaug_ref, tk):
    n_tiles = vt_ref.shape[1] // tk
    row = lax.broadcasted_iota(jnp.int32, (LANES, tk), 0)
    for n in range(n_tiles):
        v = vt_ref[:, n * tk:(n + 1) * tk]
        for a in range(HEADS_PER_TILE):
            vaug_ref[a, n] = jnp.where(row // HEAD_DIM == a, v, jnp.ones_like(v))


FOX_SPLIT = 3


def _fox_kernel(qt_ref, k_ref, vt_ref, o_ref, vaug_ref, stage0_ref, stage1_ref, *, tq):
    i = pl.program_id(2)

    @pl.when(i == 0)
    def _():
        _fill_v_aug(vt_ref, vaug_ref, tq)

    row = lax.broadcasted_iota(jnp.int32, (LANES, tq), 0)
    qt = qt_ref[...]
    queries = []
    for a in range(HEADS_PER_TILE):
        offs = (row >= _spare(a)) & (row < _spare(a, FOX_SPLIT))
        queries.append(jnp.where(row // HEAD_DIM == a, qt,
                                 jnp.where(offs, -1.0, 0.0).astype(BF16)))
    key = lax.broadcasted_iota(jnp.int32, (tq, tq), 0)
    qry = lax.broadcasted_iota(jnp.int32, (tq, tq), 1)
    causal = key <= qry

    def block(a, j, mask=None):
        start = pl.multiple_of(j * tq, tq)
        return (lambda: k_ref[a, 0, pl.ds(start, tq), :], lambda: queries[a],
                lambda: vaug_ref[a, j], None, mask)

    heads = range(HEADS_PER_TILE)
    carries = lax.cond(
        i % 2 == 1,
        lambda c: _attend(c, [[block(a, i - 1), block(a, i, causal)] for a in heads]),
        lambda c: _attend(c, [[block(a, i, causal)] for a in heads]),
        tuple(_scores_init(tq) for _ in heads))
    carries = _attend_pipelined(
        carries, i // 2, lambda j: [[block(a, 2 * j), block(a, 2 * j + 1)] for a in heads],
        (stage0_ref, stage1_ref))
    _finish(carries, o_ref)


def _fox_attention(qv_t, k_aug, B, S, tq):
    D = k_aug.shape[-1]
    n_tiles = D // LANES
    nq = S // tq
    return pl.pallas_call(
        functools.partial(_fox_kernel, tq=tq),
        out_shape=jax.ShapeDtypeStruct((D, B * S), BF16),
        grid=(B, n_tiles, nq),
        in_specs=[pl.BlockSpec((LANES, tq), lambda b, h, i: (h, b * nq + i)),
                  pl.BlockSpec((HEADS_PER_TILE, 1, S, LANES), lambda b, h, i: (0, b, 0, h)),
                  pl.BlockSpec((LANES, S), lambda b, h, i: (n_tiles + h, b))],
        out_specs=pl.BlockSpec((LANES, tq), lambda b, h, i: (h, b * nq + i)),
        scratch_shapes=[pltpu.VMEM((HEADS_PER_TILE, nq, LANES, tq), BF16)] + _stage_scratch(tq),
        compiler_params=_cparams("parallel", "parallel", "arbitrary"),
        name="fox_attention",
    )(qv_t, k_aug, qv_t)


def _log_sigmoid(z):
    return jnp.minimum(z, 0.0) - jnp.log1p(jnp.exp(-jnp.abs(z)))


def _fox_gates_kernel(x_ref, w_ref, b_ref, o_ref, carry_ref, *, ts):
    @pl.when(pl.program_id(1) == 0)
    def _():
        carry_ref[...] = jnp.zeros_like(carry_ref)

    z = jnp.dot(x_ref[0], w_ref[...], preferred_element_type=F32, precision=HIGHEST)
    lf = _log_sigmoid(z + b_ref[...])
    r = lax.broadcasted_iota(jnp.int32, (ts, ts), 0)
    c = lax.broadcasted_iota(jnp.int32, (ts, ts), 1)
    tri = (c <= r).astype(F32)
    cum = jnp.dot(tri, lf, preferred_element_type=F32, precision=HIGHEST) + carry_ref[...]
    carry_ref[...] = cum[ts - 1:ts, :]
    o_ref[0] = cum


def _fox_gates(x3d, w_f, b_f, ts):
    B, S, D = x3d.shape
    H = w_f.shape[1]
    w_pad = jnp.zeros((D, LANES), F32).at[:, :H].set(w_f)
    b_pad = jnp.zeros((1, LANES), F32).at[0, :H].set(b_f)
    return pl.pallas_call(
        functools.partial(_fox_gates_kernel, ts=ts),
        out_shape=jax.ShapeDtypeStruct((B, S, LANES), F32),
        grid=(B, S // ts),
        in_specs=[pl.BlockSpec((1, ts, D), lambda b, s: (b, s, 0)),
                  pl.BlockSpec((D, LANES), lambda b, s: (0, 0)),
                  pl.BlockSpec((1, LANES), lambda b, s: (0, 0))],
        out_specs=pl.BlockSpec((1, ts, LANES), lambda b, s: (b, s, 0)),
        scratch_shapes=[pltpu.VMEM((1, LANES), F32)],
        compiler_params=_cparams("parallel", "arbitrary"),
        name="fox_gates",
    )(x3d, w_pad, b_pad)


def _fox_kprep_kernel(k_ref, c_ref, o_ref):
    tm, D = k_ref.shape
    lane = lax.broadcasted_iota(jnp.int32, (tm, LANES), 1)
    for t in range(D // LANES):
        k = k_ref[:, t * LANES:(t + 1) * LANES].astype(F32)
        for a in range(HEADS_PER_TILE):
            h = t * HEADS_PER_TILE + a
            rest = c_ref[:, h:h + 1]
            aug = jnp.zeros((tm, LANES), F32)
            for n in range(FOX_SPLIT):
                piece = rest.astype(BF16).astype(F32)
                rest = rest - piece
                aug = jnp.where(lane == _spare(a, n), piece, aug)
            o_ref[a, :, t * LANES:(t + 1) * LANES] = jnp.where(lane // HEAD_DIM == a, k,
                                                               aug).astype(BF16)


def _fox_kprep(k2d, c_tok, tm):
    T, D = k2d.shape
    return pl.pallas_call(
        _fox_kprep_kernel,
        out_shape=jax.ShapeDtypeStruct((HEADS_PER_TILE, T, D), BF16),
        grid=(T // tm,),
        in_specs=[pl.BlockSpec((tm, D), lambda i: (i, 0)),
                  pl.BlockSpec((tm, LANES), lambda i: (i, 0))],
        out_specs=pl.BlockSpec((HEADS_PER_TILE, tm, D), lambda i: (0, i, 0)),
        compiler_params=_cparams("parallel"),
        name="fox_kprep",
    )(k2d, c_tok)


MOBA_SLAB = 16


def _moba_kernel(bfar_ref, qt_ref, k_ref, vt_ref, kmean_ref, bown_ref, badj_ref, o_ref,
                 vaug_ref, sel_ref, stage0_ref, stage1_ref, *, blk):
    t = pl.program_id(1)
    i = pl.program_id(2)
    nb = kmean_ref.shape[1]

    @pl.when(i == 0)
    def _():
        _fill_v_aug(vt_ref, vaug_ref, blk)

    row = lax.broadcasted_iota(jnp.int32, (LANES, blk), 0)
    brow = lax.broadcasted_iota(jnp.int32, (nb, blk), 0)
    qt = qt_ref[...]
    zero = jnp.zeros_like(qt)
    base, slabs = [], []
    for a in range(HEADS_PER_TILE):
        qa = jnp.where(row // HEAD_DIM == a, qt, zero)
        base.append(qa)
        bscore = jnp.dot(kmean_ref[0], qa.astype(F32), preferred_element_type=F32,
                         precision=HIGHEST)
        cand = jnp.where(brow < i, bscore, NEG)
        sel = jnp.zeros((nb, blk), F32)
        for _ in range(MOBA_TOPK):
            mx = jnp.max(cand, axis=0, keepdims=True)
            first = jnp.min(jnp.where(cand == mx, brow, nb), axis=0, keepdims=True)
            hit = brow == first
            sel = jnp.where(hit, 1.0, sel)
            cand = jnp.where(hit, -jnp.inf, cand)
        sel_ref[a] = jnp.where(brow < i, sel, 0.0)
        slabs.append(_spare(a) // MOBA_SLAB * MOBA_SLAB)

    srow = lax.broadcasted_iota(jnp.int32, (MOBA_SLAB, blk), 0)

    def query(a, j, bias_hi, bias_lo):
        s0 = slabs[a]
        chosen = sel_ref[a, pl.ds(j, 1), :] > 0.0
        slab = jnp.where(srow == _spare(a) - s0, jnp.where(chosen, bias_hi, NEG),
                         jnp.where(srow == _spare(a, 1) - s0, bias_lo, 0.0)).astype(BF16)
        head = [base[a][:s0]] if s0 else []
        tail = [base[a][s0 + MOBA_SLAB:]] if s0 + MOBA_SLAB < LANES else []
        return jnp.concatenate(head + [slab] + tail, axis=0)

    key = lax.broadcasted_iota(jnp.int32, (blk, blk), 0)
    qry = lax.broadcasted_iota(jnp.int32, (blk, blk), 1)
    causal = key <= qry
    heads = range(HEADS_PER_TILE)

    def keys(a, j):
        start = pl.multiple_of(j * blk, blk)
        return lambda: k_ref[a, 0, pl.ds(start, blk), :]

    def own(a):
        return keys(a, i), lambda: base[a], lambda: vaug_ref[a, i], lambda: bown_ref[a], causal

    def prev(a):
        return (keys(a, i - 1), lambda: query(a, i - 1, 0.0, 0.0), lambda: vaug_ref[a, i - 1],
                lambda: badj_ref[a], None)

    def far(a, j):
        h = t * HEADS_PER_TILE + a
        return (keys(a, j), lambda: query(a, j, bfar_ref[0, h], bfar_ref[1, h]),
                lambda: vaug_ref[a, j], None, None)

    carries = lax.cond(
        i >= 1,
        lambda c: _attend(c, [[prev(a), own(a)] for a in heads]),
        lambda c: _attend(c, [[own(a)] for a in heads]),
        tuple(_scores_init(blk) for _ in heads))
    carries = lax.cond(
        (i >= 2) & (i % 2 == 0),
        lambda c: _attend(c, [[far(a, i - 2)] for a in heads]),
        lambda c: c, carries)
    carries = _attend_pipelined(
        carries, (i - 1) // 2, lambda j: [[far(a, 2 * j), far(a, 2 * j + 1)] for a in heads],
        (stage0_ref, stage1_ref))
    _finish(carries, o_ref)


def _rel_bucket(dist, n_buckets):
    n = jnp.maximum(dist, 0)
    max_exact = n_buckets // 2
    nf = jnp.maximum(n, 1).astype(F32)
    large = max_exact + (jnp.log(nf / max_exact) / math.log(REL_MAX_DIST / max_exact)
                         * (n_buckets - max_exact)).astype(jnp.int32)
    large = jnp.minimum(large, n_buckets - 1)
    return jnp.where(n < max_exact, n, large)


def _moba_attention(qv_t, k_aug, k_mean, rel_bias, B, S):
    D = k_aug.shape[-1]
    n_tiles = D // LANES
    blk = MOBA_BLOCK
    nb = S // blk
    hp = HEADS_PER_TILE
    r = jnp.arange(blk)
    delta = r[None, :] - r[:, None]
    bias_t = rel_bias.T.astype(F32)
    n_buckets = rel_bias.shape[0]
    def table(dist):
        onehot = jax.nn.one_hot(_rel_bucket(dist, n_buckets), n_buckets, dtype=F32)
        return jnp.einsum('crn,hn->hcr', onehot, bias_t, precision=HIGHEST)

    b_own = table(delta)
    b_adj = table(delta + blk)
    b_far = bias_t[:, n_buckets - 1]
    far_hi = b_far.astype(BF16).astype(F32)
    far_lo = (b_far - far_hi).astype(BF16).astype(F32)
    b_far2 = jnp.stack([far_hi, far_lo])
    return pl.pallas_call(
        functools.partial(_moba_kernel, blk=blk),
        out_shape=jax.ShapeDtypeStruct((D, B * S), BF16),
        grid_spec=pltpu.PrefetchScalarGridSpec(
            num_scalar_prefetch=1,
            grid=(B, n_tiles, nb),
            in_specs=[pl.BlockSpec((LANES, blk), lambda b, h, i, f: (h, b * nb + i)),
                      pl.BlockSpec((hp, 1, S, LANES), lambda b, h, i, f: (0, b, 0, h)),
                      pl.BlockSpec((LANES, S), lambda b, h, i, f: (n_tiles + h, b)),
                      pl.BlockSpec((1, nb, LANES), lambda b, h, i, f: (b, 0, h)),
                      pl.BlockSpec((hp, blk, blk), lambda b, h, i, f: (h, 0, 0)),
                      pl.BlockSpec((hp, blk, blk), lambda b, h, i, f: (h, 0, 0))],
            out_specs=pl.BlockSpec((LANES, blk), lambda b, h, i, f: (h, b * nb + i)),
            scratch_shapes=[pltpu.VMEM((hp, nb, LANES, blk), BF16),
                            pltpu.VMEM((hp, nb, blk), F32)] + _stage_scratch(blk)),
        compiler_params=_cparams("parallel", "parallel", "arbitrary"),
        name="moba_attention",
    )(b_far2, qv_t, k_aug, qv_t, k_mean, b_own, b_adj)


def _moba_kprep_kernel(k_ref, o_ref, mean_ref):
    k = k_ref[0]
    col = lax.broadcasted_iota(jnp.int32, k.shape, 1)
    lane = col % LANES
    for a in range(HEADS_PER_TILE):
        ones = (lane == _spare(a)) | (lane == _spare(a, 1))
        o_ref[a, 0] = jnp.where((col // HEAD_DIM) % HEADS_PER_TILE == a, k,
                                jnp.where(ones, 1.0, 0.0).astype(BF16))
    mean_ref[0, 0] = jnp.mean(k.astype(F32), axis=0, keepdims=True)


def _moba_kprep(k3d):
    B, S, D = k3d.shape
    blk = MOBA_BLOCK
    nb = S // blk
    k_aug, k_mean = pl.pallas_call(
        _moba_kprep_kernel,
        out_shape=(jax.ShapeDtypeStruct((HEADS_PER_TILE, B, S, D), BF16),
                   jax.ShapeDtypeStruct((B, nb, 1, D), F32)),
        grid=(B, nb),
        in_specs=[pl.BlockSpec((1, blk, D), lambda b, n: (b, n, 0))],
        out_specs=(pl.BlockSpec((HEADS_PER_TILE, 1, blk, D), lambda b, n: (0, b, n, 0)),
                   pl.BlockSpec((1, 1, 1, D), lambda b, n: (b, n, 0, 0))),
        compiler_params=_cparams("parallel", "parallel"),
        name="moba_kprep",
    )(k3d)
    return k_aug, k_mean.reshape(B, nb, D)


def _router_kernel(x_ref, wt_ref, bias_ref, idx_ref, gate_ref, rank_ref, cnt_ref, run_ref, *, tm):
    @pl.when(pl.program_id(0) == 0)
    def _():
        run_ref[...] = jnp.zeros_like(run_ref)

    E = wt_ref.shape[0]
    gsz = E // N_GROUPS
    logits = _nt_dot(wt_ref[...], x_ref[...], precision=HIGHEST)
    s = jax.nn.sigmoid(logits)
    sb = s + bias_ref[...]
    neg_inf = -jnp.inf

    giota = lax.broadcasted_iota(jnp.int32, (gsz, tm), 0)
    gscore = []
    for g in range(N_GROUPS):
        blk = sb[g * gsz:(g + 1) * gsz, :]
        m1 = jnp.max(blk, axis=0, keepdims=True)
        i1 = jnp.min(jnp.where(blk == m1, giota, gsz), axis=0, keepdims=True)
        m2 = jnp.max(jnp.where(giota == i1, neg_inf, blk), axis=0, keepdims=True)
        gscore.append(m1 + m2)

    gsel = [jnp.zeros((1, tm), jnp.bool_) for _ in range(N_GROUPS)]
    for _ in range(TOPK_GROUPS):
        mx = functools.reduce(jnp.maximum, gscore)
        found = jnp.zeros((1, tm), jnp.bool_)
        for g in range(N_GROUPS):
            hit = (gscore[g] == mx) & jnp.logical_not(found)
            gsel[g] = gsel[g] | hit
            found = found | hit
            gscore[g] = jnp.where(hit, neg_inf, gscore[g])
    emask = jnp.concatenate([jnp.broadcast_to(gsel[g], (gsz, tm)) for g in range(N_GROUPS)], axis=0)
    cand = jnp.where(emask, sb, NEG)

    eiota = lax.broadcasted_iota(jnp.int32, (E, tm), 0)
    hits, idxs, ws = [], [], []
    for _ in range(TOP_K):
        mx = jnp.max(cand, axis=0, keepdims=True)
        first = jnp.min(jnp.where(cand == mx, eiota, E), axis=0, keepdims=True)
        hit = eiota == first
        hits.append(hit)
        idxs.append(first)
        ws.append(jnp.sum(jnp.where(hit, s, 0.0), axis=0, keepdims=True))
        cand = jnp.where(hit, neg_inf, cand)
    wsum = functools.reduce(jnp.add, ws)

    chosen = functools.reduce(jnp.logical_or, hits)
    onehot = jnp.where(chosen, 1.0, 0.0)
    tr = lax.broadcasted_iota(jnp.int32, (tm, tm), 0)
    tc = lax.broadcasted_iota(jnp.int32, (tm, tm), 1)
    before = (tr < tc).astype(BF16)
    prior = jnp.dot(onehot.astype(BF16), before, preferred_element_type=F32) + run_ref[...]
    grow = lax.broadcasted_iota(jnp.int32, (LANES, tm), 0)
    gates = jnp.zeros((LANES, tm), F32)
    for k in range(TOP_K):
        idx_ref[k:k + 1, :] = idxs[k]
        gates = jnp.where(grow == k, ws[k] / wsum * ROUTED_SCALE, gates)
        rank_ref[k:k + 1, :] = jnp.sum(jnp.where(hits[k], prior, 0.0), axis=0,
                                       keepdims=True).astype(jnp.int32)
    gate_ref[...] = gates.T
    run_ref[...] = run_ref[...] + jnp.sum(onehot, axis=1, keepdims=True)
    cnt_ref[...] = run_ref[...]


def _router(x2d, w_router, router_bias, tm):
    T, D = x2d.shape
    E = w_router.shape[1]
    tok = lambda i: (0, i)
    fixed = lambda i: (0, 0)
    return pl.pallas_call(
        functools.partial(_router_kernel, tm=tm),
        out_shape=(jax.ShapeDtypeStruct((TOP_K, T), jnp.int32),
                   jax.ShapeDtypeStruct((T, LANES), F32),
                   jax.ShapeDtypeStruct((TOP_K, T), jnp.int32),
                   jax.ShapeDtypeStruct((E, 1), F32)),
        grid=(T // tm,),
        in_specs=[pl.BlockSpec((tm, D), lambda i: (i, 0)),
                  pl.BlockSpec((E, D), fixed),
                  pl.BlockSpec((E, 1), fixed)],
        out_specs=(pl.BlockSpec((TOP_K, tm), tok), pl.BlockSpec((tm, LANES), lambda i: (i, 0)),
                   pl.BlockSpec((TOP_K, tm), tok), pl.BlockSpec((E, 1), fixed)),
        scratch_shapes=[pltpu.VMEM((E, 1), F32)],
        compiler_params=_cparams("arbitrary"),
        name="router",
    )(x2d, w_router.T, router_bias.reshape(E, 1).astype(F32))


def _slots_kernel(idx_ref, rank_ref, start_ref, o_ref):
    E = start_ref.shape[0]
    tm = idx_ref.shape[1]
    eiota = lax.broadcasted_iota(jnp.int32, (E, tm), 0)
    start = start_ref[...]
    for k in range(TOP_K):
        base = jnp.sum(jnp.where(eiota == idx_ref[k:k + 1, :], start, 0.0), axis=0, keepdims=True)
        o_ref[k:k + 1, :] = base.astype(jnp.int32) + rank_ref[k:k + 1, :]


def _slots(idx, rank, seg_start, tm):
    T = idx.shape[1]
    E = seg_start.shape[0]
    tok = lambda i: (0, i)
    return pl.pallas_call(
        _slots_kernel,
        out_shape=jax.ShapeDtypeStruct((TOP_K, T), jnp.int32),
        grid=(T // tm,),
        in_specs=[pl.BlockSpec((TOP_K, tm), tok), pl.BlockSpec((TOP_K, tm), tok),
                  pl.BlockSpec((E, 1), lambda i: (0, 0))],
        out_specs=pl.BlockSpec((TOP_K, tm), tok),
        compiler_params=_cparams("parallel"),
        name="moe_slots",
    )(idx, rank, seg_start.astype(F32).reshape(E, 1))


def _row_copy(src_ref, src_row, dst_ref, dst_row, sem):
    return pltpu.make_async_copy(src_ref.at[pl.ds(src_row, 1), :],
                                 dst_ref.at[pl.ds(dst_row, 1), :], sem)


U32 = jnp.uint32


def _pack_halves(y):
    n = y.shape[1] // 2
    bits = lambda v: lax.bitcast_convert_type(v.astype(BF16).astype(F32), U32)
    return (bits(y[:, :n]) >> 16) | bits(y[:, n:])


def _unpack_halves(p):
    return (lax.bitcast_convert_type(p << 16, F32),
            lax.bitcast_convert_type(p & U32(0xFFFF0000), F32))


def _dispatch_kernel(seg_ref, dest_ref, x_ref, xs_ref, zero_ref, xp_ref, zsem, sem, *, tt):
    E = seg_ref.shape[1]
    xp_ref[...] = _pack_halves(x_ref[...])

    def zero_copy(e):
        start = pl.multiple_of(seg_ref[1, e] - EXPERT_ROWS, EXPERT_ROWS)
        return pltpu.make_async_copy(zero_ref, xs_ref.at[pl.ds(start, EXPERT_ROWS), :], zsem)

    @pl.when(pl.program_id(0) == 0)
    def _():
        zero_ref[...] = jnp.zeros_like(zero_ref)
        for go in (lambda c: c.start(), lambda c: c.wait()):
            def body(e, carry, go=go):
                @pl.when(seg_ref[1, e] > seg_ref[0, e])
                def _():
                    go(zero_copy(e))
                return carry
            lax.fori_loop(0, E, body, 0)

    def issue(r, c):
        for k in range(TOP_K):
            _row_copy(xp_ref, r, xs_ref, dest_ref[k, r], sem).start()
        return c

    def drain(r, c):
        for k in range(TOP_K):
            _row_copy(xp_ref, 0, xs_ref, 0, sem).wait()
        return c

    lax.fori_loop(0, tt, issue, 0)
    lax.fori_loop(0, tt, drain, 0)


def _dispatch(x2d, dest, seg, n_rows, tt):
    T, D = x2d.shape
    return pl.pallas_call(
        functools.partial(_dispatch_kernel, tt=tt),
        out_shape=jax.ShapeDtypeStruct((n_rows, D // 2), U32),
        grid_spec=pltpu.PrefetchScalarGridSpec(
            num_scalar_prefetch=1,
            grid=(T // tt,),
            in_specs=[pl.BlockSpec((TOP_K, tt), lambda i, seg: (0, i), memory_space=pltpu.SMEM),
                      pl.BlockSpec((tt, D), lambda i, seg: (i, 0))],
            out_specs=pl.BlockSpec(memory_space=pl.ANY),
            scratch_shapes=[pltpu.VMEM((EXPERT_ROWS, D // 2), U32), pltpu.VMEM((tt, D // 2), U32),
                            pltpu.SemaphoreType.DMA, pltpu.SemaphoreType.DMA]),
        compiler_params=_cparams("arbitrary"),
        name="moe_dispatch",
    )(seg, dest, x2d)


def _silu(g):
    return g * jax.nn.sigmoid(g)


SCHED_FIRST, SCHED_SLOT, SCHED_NEXT = 0, 1, 2


def _expert_kernel(sched_ref, n_used_ref, x_ref, wg_hbm, wu_hbm, wd_hbm, o_ref,
                   wg_buf, wu_buf, wd_buf, sem, *, layer, first_expert_row):
    i = pl.program_id(0)
    slot = sched_ref[SCHED_SLOT, i]

    def copies(expert, slot):
        pairs = ((wg_hbm, wg_buf), (wu_hbm, wu_buf), (wd_hbm, wd_buf))
        return [pltpu.make_async_copy(src.at[layer, expert], dst.at[slot], sem.at[slot, n])
                for n, (src, dst) in enumerate(pairs)]

    @pl.when(i == 0)
    def _():
        for c in copies(sched_ref[first_expert_row, 0], 0):
            c.start()

    @pl.when(sched_ref[SCHED_FIRST, i] == 1)
    def _():
        for c in copies(0, slot):
            c.wait()

        @pl.when(sched_ref[SCHED_NEXT, i] >= 0)
        def _():
            for c in copies(sched_ref[SCHED_NEXT, i], 1 - slot):
                c.start()

    @pl.when(i < n_used_ref[0])
    def _():
        x = jnp.concatenate(_unpack_halves(x_ref[...]), axis=1).astype(BF16)
        g = jnp.dot(x, wg_buf[slot].astype(BF16), preferred_element_type=F32)
        u = jnp.dot(x, wu_buf[slot].astype(BF16), preferred_element_type=F32)
        a = (_silu(g) * u).astype(BF16)
        o_ref[...] = _pack_halves(jnp.dot(a, wd_buf[slot].astype(BF16), preferred_element_type=F32))

    @pl.when(i >= n_used_ref[0])
    def _():
        o_ref[...] = jnp.zeros_like(o_ref)


def _expert_schedule(seg_start, seg_end, n_blocks):
    E = seg_end.shape[0]
    first_row = jnp.arange(n_blocks, dtype=jnp.int32) * EXPERT_ROWS
    blk_e = jnp.minimum(jnp.sum(seg_end[None, :] <= first_row[:, None], axis=1), E - 1)
    used = first_row < seg_end[-1]
    first = used & (first_row == seg_start[blk_e])
    ordinal = jnp.cumsum(first) - 1
    ids = jnp.where(seg_end > seg_start, jnp.arange(E), E)
    later = lax.cummin(ids, reverse=True)
    nxt = jnp.concatenate([later[1:], jnp.full((1,), E, later.dtype)])[blk_e]
    nxt = jnp.where(nxt < E, nxt, -1)
    return jnp.stack([first, ordinal % 2, nxt, blk_e]).astype(jnp.int32)


def _expert_mlp(xs, sched, n_used, w_gate, w_up, w_down, layer):
    n_rows, half = xs.shape
    D = 2 * half
    F = w_gate.shape[-1]
    tm = EXPERT_ROWS
    hbm = pl.BlockSpec(memory_space=pl.ANY)
    return pl.pallas_call(
        functools.partial(_expert_kernel, layer=layer, first_expert_row=sched.shape[0] - 1),
        out_shape=jax.ShapeDtypeStruct((n_rows, half), U32),
        grid_spec=pltpu.PrefetchScalarGridSpec(
            num_scalar_prefetch=2,
            grid=(n_rows // tm,),
            in_specs=[pl.BlockSpec((tm, half), lambda i, sc, nu: (jnp.minimum(i, nu[0] - 1), 0)),
                      hbm, hbm, hbm],
            out_specs=pl.BlockSpec((tm, half), lambda i, sc, nu: (i, 0)),
            scratch_shapes=[pltpu.VMEM((2, D, F), F32), pltpu.VMEM((2, D, F), F32),
                            pltpu.VMEM((2, F, D), F32), pltpu.SemaphoreType.DMA((2, 3))]),
        compiler_params=_cparams("arbitrary"),
        name="moe_experts",
    )(sched, n_used, xs, w_gate, w_up, w_down)


def _combine_kernel(dest_ref, dest_next_ref, x_ref, gate_ref, wsg_ref, wsu_ref, wsd_ref, g_ref,
                    b_ref, ys_ref, o_ref, rows_ref, sem, *, tt, alpha):
    i = pl.program_id(0)
    slot = i % 2

    def gather(dests, slot):
        def issue(r, c):
            for k in range(TOP_K):
                _row_copy(ys_ref, dests[k, r], rows_ref.at[slot, k], r, sem.at[slot]).start()
            return c
        lax.fori_loop(0, tt, issue, 0)

    def drain(r, c):
        for k in range(TOP_K):
            _row_copy(ys_ref, 0, rows_ref.at[slot, 0], 0, sem.at[slot]).wait()
        return c

    @pl.when(i == 0)
    def _():
        gather(dest_ref, slot)

    @pl.when(i + 1 < pl.num_programs(0))
    def _():
        gather(dest_next_ref, 1 - slot)

    x = x_ref[...]
    xb = x.astype(BF16)
    g = jnp.dot(xb, wsg_ref[...], preferred_element_type=F32)
    u = jnp.dot(xb, wsu_ref[...], preferred_element_type=F32)
    y = jnp.dot((_silu(g) * u).astype(BF16), wsd_ref[...], preferred_element_type=F32)
    lax.fori_loop(0, tt, drain, 0)
    gates = gate_ref[...]
    half = y.shape[1] // 2
    left, right = y[:, :half], y[:, half:]
    for k in range(TOP_K):
        lo, hi = _unpack_halves(rows_ref[slot, k])
        left = left + lo * gates[:, k:k + 1]
        right = right + hi * gates[:, k:k + 1]
    y = jnp.concatenate([left, right], axis=1)
    o_ref[...] = _layer_norm(alpha * x + y, g_ref[...], b_ref[...])


def _combine(x2d, ys, dest, gates_tok, ws_gate, ws_up, ws_down, g, b, alpha, tt):
    T, D = x2d.shape
    F = ws_gate.shape[1]
    row = lambda i: (i, 0)
    fixed = lambda i: (0, 0)
    n = T // tt
    return pl.pallas_call(
        functools.partial(_combine_kernel, tt=tt, alpha=alpha),
        out_shape=jax.ShapeDtypeStruct((T, D), F32),
        grid=(n,),
        in_specs=[pl.BlockSpec((TOP_K, tt), lambda i: (0, i), memory_space=pltpu.SMEM),
                  pl.BlockSpec((TOP_K, tt), lambda i: (0, jnp.minimum(i + 1, n - 1)),
                               memory_space=pltpu.SMEM),
                  pl.BlockSpec((tt, D), row),
                  pl.BlockSpec((tt, LANES), row),
                  pl.BlockSpec((D, F), fixed), pl.BlockSpec((D, F), fixed),
                  pl.BlockSpec((F, D), fixed),
                  pl.BlockSpec((1, D), fixed), pl.BlockSpec((1, D), fixed),
                  pl.BlockSpec(memory_space=pl.ANY)],
        out_specs=pl.BlockSpec((tt, D), row),
        scratch_shapes=[pltpu.VMEM((2, TOP_K, tt, D // 2), U32), pltpu.SemaphoreType.DMA((2,))],
        compiler_params=_cparams("arbitrary"),
        name="moe_combine",
    )(dest, dest, x2d, gates_tok, ws_gate.astype(BF16), ws_up.astype(BF16), ws_down.astype(BF16),
      g.reshape(1, D), b.reshape(1, D), ys)


def _pick(n, pref):
    t = min(n, pref)
    while n % t:
        t //= 2
    return t


def _moe_layer(x2d, w_router, router_bias, w_gate, w_up, w_down, layer, ws_gate, ws_up, ws_down,
               g, b, alpha):
    T, D = x2d.shape
    E = w_router.shape[1]
    idx, gates, rank, counts = _router(x2d, w_router, router_bias, _pick(T, 256))
    counts = counts[:, 0].astype(jnp.int32)
    padded = (counts + EXPERT_ROWS - 1) // EXPERT_ROWS * EXPERT_ROWS
    seg_end = jnp.cumsum(padded)
    seg_start = seg_end - padded
    n_blocks = (T * TOP_K + E * (EXPERT_ROWS - 1)) // EXPERT_ROWS
    dest = _slots(idx, rank, seg_start, _pick(T, 1024))
    sched = _expert_schedule(seg_start, seg_end, n_blocks)
    n_used = (seg_end[-1:] // EXPERT_ROWS).astype(jnp.int32)
    tt = _pick(T, 128)
    xs = _dispatch(x2d, dest, jnp.stack([seg_start, seg_end]).astype(jnp.int32),
                   n_blocks * EXPERT_ROWS, tt)
    ys = _expert_mlp(xs, sched, n_used, w_gate, w_up, w_down, layer)
    return _combine(x2d, ys, dest, gates, ws_gate, ws_up, ws_down, g, b, alpha, tt)


def _mixer_layer(x2d, i, B, S, p, alpha):
    T, D = x2d.shape
    tm = _pick(T, 512)
    tn = _pick(D, 1024)
    j = i // 2
    w_in = p["moba_w_in"][j] if i % 2 == 0 else p["fox_w_in"][j]
    w_qv_t = jnp.concatenate([w_in[:, :D].T * HEAD_DIM ** -0.5, w_in[:, 2 * D:3 * D].T]).astype(BF16)
    qv_t = _project_t(x2d, w_qv_t, tm, tn)
    k = _project(x2d, w_in[:, D:2 * D].astype(BF16), tm, tn)
    if i % 2 == 0:
        k_aug, k_mean = _moba_kprep(k.reshape(B, S, D))
        attn_t = _moba_attention(qv_t, k_aug, k_mean, p["rel_bias"], B, S)
        w_out = p["moba_w_out"][j]
    else:
        c_tok = _fox_gates(x2d.reshape(B, S, D), w_in[:, 3 * D:], p["fox_b_f"][j], _pick(S, 512))
        k_aug = _fox_kprep(k, c_tok.reshape(T, LANES), tm).reshape(HEADS_PER_TILE, B, S, D)
        attn_t = _fox_attention(qv_t, k_aug, B, S, _pick(S, 256))
        w_out = p["fox_w_out"][j]
    return _outproj_ln(attn_t, w_out.astype(BF16), x2d, p["ln1_g"][i], p["ln1_b"][i], alpha, tm)


def kernel(x, rel_bias, moba_w_in, moba_w_out, fox_w_in, fox_b_f, fox_w_out, ln1_g, ln1_b, ln2_g,
           ln2_b, w_router, router_bias, w_gate, w_up, w_down, ws_gate, ws_up, ws_down):
    B, S, D = x.shape
    depth = ln1_g.shape[0]
    alpha = (2 * depth) ** 0.25
    p = dict(rel_bias=rel_bias, moba_w_in=moba_w_in, moba_w_out=moba_w_out, fox_w_in=fox_w_in,
             fox_b_f=fox_b_f, fox_w_out=fox_w_out, ln1_g=ln1_g, ln1_b=ln1_b)
    x2d = x.reshape(B * S, D)
    for i in range(depth):
        x2d = _mixer_layer(x2d, i, B, S, p, alpha)
        x2d = _moe_layer(x2d, w_router[i], router_bias[i], w_gate, w_up, w_down, i,
                         ws_gate[i], ws_up[i], ws_down[i], ln2_g[i], ln2_b[i], alpha)
    return x2d.reshape(B, S, D)
```

```python
import functools
import math

import jax
import jax.numpy as jnp
from jax import lax
from jax.experimental import pallas as pl
from jax.experimental.pallas import tpu as pltpu
from jax.experimental.pallas import tpu_sc as plsc

F32 = jnp.float32
BF16 = jnp.bfloat16
HIGHEST = lax.Precision.HIGHEST

HEAD_DIM = 64
MOBA_BLOCK = 256
MOBA_TOPK = 3
REL_MAX_DIST = 128
TOP_K = 8
N_GROUPS = 8
TOPK_GROUPS = 4
ROUTED_SCALE = 2.5
LN_EPS = 1e-5
NEG = -1e30

LANES = 128
HEADS_PER_TILE = LANES // HEAD_DIM
EXPERT_ROWS = 512
VMEM_LIMIT = 48 * 1024 * 1024


def _cparams(*sem):
    return pltpu.CompilerParams(dimension_semantics=sem, vmem_limit_bytes=VMEM_LIMIT)


def _nt_dot(a, b, **kw):
    return lax.dot_general(a, b, (((1,), (1,)), ((), ())), preferred_element_type=F32, **kw)


def _proj_kernel(x_ref, w_ref, o_ref):
    o_ref[...] = jnp.dot(x_ref[...].astype(BF16), w_ref[...],
                         preferred_element_type=F32).astype(o_ref.dtype)


def _project(x2d, w, tm, tn):
    T, K = x2d.shape
    N = w.shape[1]
    return pl.pallas_call(
        _proj_kernel,
        out_shape=jax.ShapeDtypeStruct((T, N), BF16),
        grid=(T // tm, N // tn),
        in_specs=[pl.BlockSpec((tm, K), lambda i, j: (i, 0)),
                  pl.BlockSpec((K, tn), lambda i, j: (0, j))],
        out_specs=pl.BlockSpec((tm, tn), lambda i, j: (i, j)),
        compiler_params=_cparams("parallel", "arbitrary"),
        name="k_proj",
    )(x2d, w)


def _proj_t_kernel(x_ref, wt_ref, o_ref):
    o_ref[...] = _nt_dot(wt_ref[...], x_ref[...].astype(BF16)).astype(o_ref.dtype)


def _project_t(x2d, wt, tm, tn):
    T, K = x2d.shape
    N = wt.shape[0]
    return pl.pallas_call(
        _proj_t_kernel,
        out_shape=jax.ShapeDtypeStruct((N, T), BF16),
        grid=(T // tm, N // tn),
        in_specs=[pl.BlockSpec((tm, K), lambda i, j: (i, 0)),
                  pl.BlockSpec((tn, K), lambda i, j: (j, 0))],
        out_specs=pl.BlockSpec((tn, tm), lambda i, j: (j, i)),
        compiler_params=_cparams("parallel", "arbitrary"),
        name="qv_proj_t",
    )(x2d, wt)


def _layer_norm(r, g, b):
    mu = jnp.mean(r, axis=-1, keepdims=True)
    d = r - mu
    var = jnp.mean(d * d, axis=-1, keepdims=True)
    return d * lax.rsqrt(var + LN_EPS) * g + b


def _outproj_ln_kernel(at_ref, w_ref, x_ref, g_ref, b_ref, o_ref, *, alpha):
    y = lax.dot_general(at_ref[...], w_ref[...], (((0,), (0,)), ((), ())),
                        preferred_element_type=F32)
    o_ref[...] = _layer_norm(alpha * x_ref[...] + y, g_ref[...], b_ref[...])


def _outproj_ln(attn_t, w_out, x2d, g, b, alpha, tm):
    T, D = x2d.shape
    row = lambda i: (i, 0)
    fixed = lambda i: (0, 0)
    return pl.pallas_call(
        functools.partial(_outproj_ln_kernel, alpha=alpha),
        out_shape=jax.ShapeDtypeStruct((T, D), F32),
        grid=(T // tm,),
        in_specs=[pl.BlockSpec((D, tm), lambda i: (0, i)), pl.BlockSpec((D, D), fixed),
                  pl.BlockSpec((tm, D), row), pl.BlockSpec((1, D), fixed),
                  pl.BlockSpec((1, D), fixed)],
        out_specs=pl.BlockSpec((tm, D), row),
        compiler_params=_cparams("parallel"),
        name="outproj_ln",
    )(attn_t, w_out, x2d, g.reshape(1, D), b.reshape(1, D))


def _spare(a, n=0):
    return ((a + 1) % HEADS_PER_TILE) * HEAD_DIM + n


def _scores(chains):
    out = []
    for blocks in chains:
        parts = []
        for keys, query, _, extra, mask in blocks:
            s = jnp.dot(keys(), query(), preferred_element_type=F32)
            if extra is not None:
                s = s + extra()
            if mask is not None:
                s = jnp.where(mask, s, NEG)
            parts.append(s)
        out.append(parts)
    return out


def _absorb(carries, scores, chains):
    probs = []
    for (m, _), parts in zip(carries, scores):
        tops = [jnp.max(s, axis=0, keepdims=True) for s in parts]
        m_new = functools.reduce(jnp.maximum, tops, m)
        probs.append((m_new, jnp.exp(m - m_new), [jnp.exp(s - m_new).astype(BF16) for s in parts]))
    out = []
    for (_, acc), (m_new, decay, ps), blocks in zip(carries, probs, chains):
        acc = decay * acc
        for (_, _, values, _, _), p in zip(blocks, ps):
            acc = acc + jnp.dot(values(), p, preferred_element_type=F32)
        out.append((m_new, acc))
    return tuple(out)


def _attend(carries, chains):
    return _absorb(carries, _scores(chains), chains)


def _attend_pipelined(carries, n, group, stage_refs):
    def stash(chains, ref):
        for a, parts in enumerate(_scores(chains)):
            for b, s in enumerate(parts):
                ref[a, b] = s

    def fetch(chains, ref):
        return [[ref[a, b] for b in range(len(blocks))] for a, blocks in enumerate(chains)]

    odd = n % 2
    carries = lax.cond(odd == 1, lambda c: _attend(c, group(0)), lambda c: c, carries)
    trips = n // 2
    last = n - 1

    def run(carries):
        stash(group(odd), stage_refs[0])

        def body(t, carries):
            g = odd + 2 * t
            stash(group(g + 1), stage_refs[1])
            carries = _absorb(carries, fetch(group(g), stage_refs[0]), group(g))
            stash(group(jnp.minimum(g + 2, last)), stage_refs[0])
            return _absorb(carries, fetch(group(g + 1), stage_refs[1]), group(g + 1))

        return lax.fori_loop(0, trips, body, carries)

    return lax.cond(trips >= 1, run, lambda c: c, carries)


GROUP_BLOCKS = 2


def _stage_scratch(tq):
    return [pltpu.VMEM((HEADS_PER_TILE, GROUP_BLOCKS, tq, tq), F32) for _ in range(2)]


def _scores_init(tq):
    return jnp.full((1, tq), -jnp.inf, F32), jnp.zeros((LANES, tq), F32)


def _finish(carries, o_ref):
    row = lax.broadcasted_iota(jnp.int32, carries[0][1].shape, 0)
    outs = []
    for a, (_, acc) in enumerate(carries):
        s = _spare(a)
        outs.append(acc / acc[s:s + 1, :])
    o_ref[...] = jnp.where(row < HEAD_DIM, outs[0], outs[1]).astype(o_ref.dtype)


def _fill_v_aug(vt_ref, vaug_ref, tk):
    n_tiles = vt_ref.shape[1] // tk
    row = lax.broadcasted_iota(jnp.int32, (LANES, tk), 0)
    for n in range(n_tiles):
        v = vt_ref[:, n * tk:(n + 1) * tk]
        for a in range(HEADS_PER_TILE):
            vaug_ref[a, n] = jnp.where(row // HEAD_DIM == a, v, jnp.ones_like(v))


FOX_SPLIT = 3


def _fox_kernel(qt_ref, k_ref, vt_ref, o_ref, vaug_ref, stage0_ref, stage1_ref, *, tq):
    i = pl.program_id(2)

    @pl.when(i == 0)
    def _():
        _fill_v_aug(vt_ref, vaug_ref, tq)

    row = lax.broadcasted_iota(jnp.int32, (LANES, tq), 0)
    qt = qt_ref[...]
    queries = []
    for a in range(HEADS_PER_TILE):
        offs = (row >= _spare(a)) & (row < _spare(a, FOX_SPLIT))
        queries.append(jnp.where(row // HEAD_DIM == a, qt,
                                 jnp.where(offs, -1.0, 0.0).astype(BF16)))
    key = lax.broadcasted_iota(jnp.int32, (tq, tq), 0)
    qry = lax.broadcasted_iota(jnp.int32, (tq, tq), 1)
    causal = key <= qry

    def block(a, j, mask=None):
        start = pl.multiple_of(j * tq, tq)
        return (lambda: k_ref[a, 0, pl.ds(start, tq), :], lambda: queries[a],
                lambda: vaug_ref[a, j], None, mask)

    heads = range(HEADS_PER_TILE)
    carries = lax.cond(
        i % 2 == 1,
        lambda c: _attend(c, [[block(a, i - 1), block(a, i, causal)] for a in heads]),
        lambda c: _attend(c, [[block(a, i, causal)] for a in heads]),
        tuple(_scores_init(tq) for _ in heads))
    carries = _attend_pipelined(
        carries, i // 2, lambda j: [[block(a, 2 * j), block(a, 2 * j + 1)] for a in heads],
        (stage0_ref, stage1_ref))
    _finish(carries, o_ref)


def _fox_attention(qv_t, k_aug, B, S, tq):
    D = k_aug.shape[-1]
    n_tiles = D // LANES
    nq = S // tq
    return pl.pallas_call(
        functools.partial(_fox_kernel, tq=tq),
        out_shape=jax.ShapeDtypeStruct((D, B * S), BF16),
        grid=(B, n_tiles, nq),
        in_specs=[pl.BlockSpec((LANES, tq), lambda b, h, i: (h, b * nq + i)),
                  pl.BlockSpec((HEADS_PER_TILE, 1, S, LANES), lambda b, h, i: (0, b, 0, h)),
                  pl.BlockSpec((LANES, S), lambda b, h, i: (n_tiles + h, b))],
        out_specs=pl.BlockSpec((LANES, tq), lambda b, h, i: (h, b * nq + i)),
        scratch_shapes=[pltpu.VMEM((HEADS_PER_TILE, nq, LANES, tq), BF16)] + _stage_scratch(tq),
        compiler_params=_cparams("parallel", "parallel", "arbitrary"),
        name="fox_attention",
    )(qv_t, k_aug, qv_t)


def _log_sigmoid(z):
    return jnp.minimum(z, 0.0) - jnp.log1p(jnp.exp(-jnp.abs(z)))


def _fox_gates_kernel(x_ref, w_ref, b_ref, o_ref, carry_ref, *, ts):
    @pl.when(pl.program_id(1) == 0)
    def _():
        carry_ref[...] = jnp.zeros_like(carry_ref)

    z = jnp.dot(x_ref[0], w_ref[...], preferred_element_type=F32, precision=HIGHEST)
    lf = _log_sigmoid(z + b_ref[...])
    r = lax.broadcasted_iota(jnp.int32, (ts, ts), 0)
    c = lax.broadcasted_iota(jnp.int32, (ts, ts), 1)
    tri = (c <= r).astype(F32)
    cum = jnp.dot(tri, lf, preferred_element_type=F32, precision=HIGHEST) + carry_ref[...]
    carry_ref[...] = cum[ts - 1:ts, :]
    o_ref[0] = cum


def _fox_gates(x3d, w_f, b_f, ts):
    B, S, D = x3d.shape
    H = w_f.shape[1]
    w_pad = jnp.zeros((D, LANES), F32).at[:, :H].set(w_f)
    b_pad = jnp.zeros((1, LANES), F32).at[0, :H].set(b_f)
    return pl.pallas_call(
        functools.partial(_fox_gates_kernel, ts=ts),
        out_shape=jax.ShapeDtypeStruct((B, S, LANES), F32),
        grid=(B, S // ts),
        in_specs=[pl.BlockSpec((1, ts, D), lambda b, s: (b, s, 0)),
                  pl.BlockSpec((D, LANES), lambda b, s: (0, 0)),
                  pl.BlockSpec((1, LANES), lambda b, s: (0, 0))],
        out_specs=pl.BlockSpec((1, ts, LANES), lambda b, s: (b, s, 0)),
        scratch_shapes=[pltpu.VMEM((1, LANES), F32)],
        compiler_params=_cparams("parallel", "arbitrary"),
        name="fox_gates",
    )(x3d, w_pad, b_pad)


def _fox_kprep_kernel(k_ref, c_ref, o_ref):
    tm, D = k_ref.shape
    lane = lax.broadcasted_iota(jnp.int32, (tm, LANES), 1)
    for t in range(D // LANES):
        k = k_ref[:, t * LANES:(t + 1) * LANES].astype(F32)
        for a in range(HEADS_PER_TILE):
            h = t * HEADS_PER_TILE + a
            rest = c_ref[:, h:h + 1]
            aug = jnp.zeros((tm, LANES), F32)
            for n in range(FOX_SPLIT):
                piece = rest.astype(BF16).astype(F32)
                rest = rest - piece
                aug = jnp.where(lane == _spare(a, n), piece, aug)
            o_ref[a, :, t * LANES:(t + 1) * LANES] = jnp.where(lane // HEAD_DIM == a, k,
                                                               aug).astype(BF16)


def _fox_kprep(k2d, c_tok, tm):
    T, D = k2d.shape
    return pl.pallas_call(
        _fox_kprep_kernel,
        out_shape=jax.ShapeDtypeStruct((HEADS_PER_TILE, T, D), BF16),
        grid=(T // tm,),
        in_specs=[pl.BlockSpec((tm, D), lambda i: (i, 0)),
                  pl.BlockSpec((tm, LANES), lambda i: (i, 0))],
        out_specs=pl.BlockSpec((HEADS_PER_TILE, tm, D), lambda i: (0, i, 0)),
        compiler_params=_cparams("parallel"),
        name="fox_kprep",
    )(k2d, c_tok)


MOBA_SLAB = 16


def _moba_kernel(bfar_ref, qt_ref, k_ref, vt_ref, kmean_ref, bown_ref, badj_ref, o_ref,
                 vaug_ref, sel_ref, stage0_ref, stage1_ref, *, blk):
    t = pl.program_id(1)
    i = pl.program_id(2)
    nb = kmean_ref.shape[1]

    @pl.when(i == 0)
    def _():
        _fill_v_aug(vt_ref, vaug_ref, blk)

    row = lax.broadcasted_iota(jnp.int32, (LANES, blk), 0)
    brow = lax.broadcasted_iota(jnp.int32, (nb, blk), 0)
    qt = qt_ref[...]
    zero = jnp.zeros_like(qt)
    base, slabs = [], []
    for a in range(HEADS_PER_TILE):
        qa = jnp.where(row // HEAD_DIM == a, qt, zero)
        base.append(qa)
        bscore = jnp.dot(kmean_ref[0], qa.astype(F32), preferred_element_type=F32,
                         precision=HIGHEST)
        cand = jnp.where(brow < i, bscore, NEG)
        sel = jnp.zeros((nb, blk), F32)
        for _ in range(MOBA_TOPK):
            mx = jnp.max(cand, axis=0, keepdims=True)
            first = jnp.min(jnp.where(cand == mx, brow, nb), axis=0, keepdims=True)
            hit = brow == first
            sel = jnp.where(hit, 1.0, sel)
            cand = jnp.where(hit, -jnp.inf, cand)
        sel_ref[a] = jnp.where(brow < i, sel, 0.0)
        slabs.append(_spare(a) // MOBA_SLAB * MOBA_SLAB)

    srow = lax.broadcasted_iota(jnp.int32, (MOBA_SLAB, blk), 0)

    def query(a, j, bias_hi, bias_lo):
        s0 = slabs[a]
        chosen = sel_ref[a, pl.ds(j, 1), :] > 0.0
        slab = jnp.where(srow == _spare(a) - s0, jnp.where(chosen, bias_hi, NEG),
                         jnp.where(srow == _spare(a, 1) - s0, bias_lo, 0.0)).astype(BF16)
        head = [base[a][:s0]] if s0 else []
        tail = [base[a][s0 + MOBA_SLAB:]] if s0 + MOBA_SLAB < LANES else []
        return jnp.concatenate(head + [slab] + tail, axis=0)

    key = lax.broadcasted_iota(jnp.int32, (blk, blk), 0)
    qry = lax.broadcasted_iota(jnp.int32, (blk, blk), 1)
    causal = key <= qry
    heads = range(HEADS_PER_TILE)

    def keys(a, j):
        start = pl.multiple_of(j * blk, blk)
        return lambda: k_ref[a, 0, pl.ds(start, blk), :]

    def own(a):
        return keys(a, i), lambda: base[a], lambda: vaug_ref[a, i], lambda: bown_ref[a], causal

    def prev(a):
        return (keys(a, i - 1), lambda: query(a, i - 1, 0.0, 0.0), lambda: vaug_ref[a, i - 1],
                lambda: badj_ref[a], None)

    def far(a, j):
        h = t * HEADS_PER_TILE + a
        return (keys(a, j), lambda: query(a, j, bfar_ref[0, h], bfar_ref[1, h]),
                lambda: vaug_ref[a, j], None, None)

    carries = lax.cond(
        i >= 1,
        lambda c: _attend(c, [[prev(a), own(a)] for a in heads]),
        lambda c: _attend(c, [[own(a)] for a in heads]),
        tuple(_scores_init(blk) for _ in heads))
    carries = lax.cond(
        (i >= 2) & (i % 2 == 0),
        lambda c: _attend(c, [[far(a, i - 2)] for a in heads]),
        lambda c: c, carries)
    carries = _attend_pipelined(
        carries, (i - 1) // 2, lambda j: [[far(a, 2 * j), far(a, 2 * j + 1)] for a in heads],
        (stage0_ref, stage1_ref))
    _finish(carries, o_ref)


def _rel_bucket(dist, n_buckets):
    n = jnp.maximum(dist, 0)
    max_exact = n_buckets // 2
    nf = jnp.maximum(n, 1).astype(F32)
    large = max_exact + (jnp.log(nf / max_exact) / math.log(REL_MAX_DIST / max_exact)
                         * (n_buckets - max_exact)).astype(jnp.int32)
    large = jnp.minimum(large, n_buckets - 1)
    return jnp.where(n < max_exact, n, large)


def _moba_attention(qv_t, k_aug, k_mean, rel_bias, B, S):
    D = k_aug.shape[-1]
    n_tiles = D // LANES
    blk = MOBA_BLOCK
    nb = S // blk
    hp = HEADS_PER_TILE
    r = jnp.arange(blk)
    delta = r[None, :] - r[:, None]
    bias_t = rel_bias.T.astype(F32)
    n_buckets = rel_bias.shape[0]
    def table(dist):
        onehot = jax.nn.one_hot(_rel_bucket(dist, n_buckets), n_buckets, dtype=F32)
        return jnp.einsum('crn,hn->hcr', onehot, bias_t, precision=HIGHEST)

    b_own = table(delta)
    b_adj = table(delta + blk)
    b_far = bias_t[:, n_buckets - 1]
    far_hi = b_far.astype(BF16).astype(F32)
    far_lo = (b_far - far_hi).astype(BF16).astype(F32)
    b_far2 = jnp.stack([far_hi, far_lo])
    return pl.pallas_call(
        functools.partial(_moba_kernel, blk=blk),
        out_shape=jax.ShapeDtypeStruct((D, B * S), BF16),
        grid_spec=pltpu.PrefetchScalarGridSpec(
            num_scalar_prefetch=1,
            grid=(B, n_tiles, nb),
            in_specs=[pl.BlockSpec((LANES, blk), lambda b, h, i, f: (h, b * nb + i)),
                      pl.BlockSpec((hp, 1, S, LANES), lambda b, h, i, f: (0, b, 0, h)),
                      pl.BlockSpec((LANES, S), lambda b, h, i, f: (n_tiles + h, b)),
                      pl.BlockSpec((1, nb, LANES), lambda b, h, i, f: (b, 0, h)),
                      pl.BlockSpec((hp, blk, blk), lambda b, h, i, f: (h, 0, 0)),
                      pl.BlockSpec((hp, blk, blk), lambda b, h, i, f: (h, 0, 0))],
            out_specs=pl.BlockSpec((LANES, blk), lambda b, h, i, f: (h, b * nb + i)),
            scratch_shapes=[pltpu.VMEM((hp, nb, LANES, blk), BF16),
                            pltpu.VMEM((hp, nb, blk), F32)] + _stage_scratch(blk)),
        compiler_params=_cparams("parallel", "parallel", "arbitrary"),
        name="moba_attention",
    )(b_far2, qv_t, k_aug, qv_t, k_mean, b_own, b_adj)


def _moba_kprep_kernel(k_ref, o_ref, mean_ref):
    k = k_ref[0]
    col = lax.broadcasted_iota(jnp.int32, k.shape, 1)
    lane = col % LANES
    for a in range(HEADS_PER_TILE):
        ones = (lane == _spare(a)) | (lane == _spare(a, 1))
        o_ref[a, 0] = jnp.where((col // HEAD_DIM) % HEADS_PER_TILE == a, k,
                                jnp.where(ones, 1.0, 0.0).astype(BF16))
    mean_ref[0, 0] = jnp.mean(k.astype(F32), axis=0, keepdims=True)


def _moba_kprep(k3d):
    B, S, D = k3d.shape
    blk = MOBA_BLOCK
    nb = S // blk
    k_aug, k_mean = pl.pallas_call(
        _moba_kprep_kernel,
        out_shape=(jax.ShapeDtypeStruct((HEADS_PER_TILE, B, S, D), BF16),
                   jax.ShapeDtypeStruct((B, nb, 1, D), F32)),
        grid=(B, nb),
        in_specs=[pl.BlockSpec((1, blk, D), lambda b, n: (b, n, 0))],
        out_specs=(pl.BlockSpec((HEADS_PER_TILE, 1, blk, D), lambda b, n: (0, b, n, 0)),
                   pl.BlockSpec((1, 1, 1, D), lambda b, n: (b, n, 0, 0))),
        compiler_params=_cparams("parallel", "parallel"),
        name="moba_kprep",
    )(k3d)
    return k_aug, k_mean.reshape(B, nb, D)


def _router_kernel(x_ref, wt_ref, bias_ref, idx_ref, gate_ref, rank_ref, cnt_ref, run_ref, *, tm):
    @pl.when(pl.program_id(0) == 0)
    def _():
        run_ref[...] = jnp.zeros_like(run_ref)

    E = wt_ref.shape[0]
    gsz = E // N_GROUPS
    logits = _nt_dot(wt_ref[...], x_ref[...], precision=HIGHEST)
    s = jax.nn.sigmoid(logits)
    sb = s + bias_ref[...]
    neg_inf = -jnp.inf

    giota = lax.broadcasted_iota(jnp.int32, (gsz, tm), 0)
    gscore = []
    for g in range(N_GROUPS):
        blk = sb[g * gsz:(g + 1) * gsz, :]
        m1 = jnp.max(blk, axis=0, keepdims=True)
        i1 = jnp.min(jnp.where(blk == m1, giota, gsz), axis=0, keepdims=True)
        m2 = jnp.max(jnp.where(giota == i1, neg_inf, blk), axis=0, keepdims=True)
        gscore.append(m1 + m2)

    gsel = [jnp.zeros((1, tm), jnp.bool_) for _ in range(N_GROUPS)]
    for _ in range(TOPK_GROUPS):
        mx = functools.reduce(jnp.maximum, gscore)
        found = jnp.zeros((1, tm), jnp.bool_)
        for g in range(N_GROUPS):
            hit = (gscore[g] == mx) & jnp.logical_not(found)
            gsel[g] = gsel[g] | hit
            found = found | hit
            gscore[g] = jnp.where(hit, neg_inf, gscore[g])
    emask = jnp.concatenate([jnp.broadcast_to(gsel[g], (gsz, tm)) for g in range(N_GROUPS)], axis=0)
    cand = jnp.where(emask, sb, NEG)

    eiota = lax.broadcasted_iota(jnp.int32, (E, tm), 0)
    hits, idxs, ws = [], [], []
    for _ in range(TOP_K):
        mx = jnp.max(cand, axis=0, keepdims=True)
        first = jnp.min(jnp.where(cand == mx, eiota, E), axis=0, keepdims=True)
        hit = eiota == first
        hits.append(hit)
        idxs.append(first)
        ws.append(jnp.sum(jnp.where(hit, s, 0.0), axis=0, keepdims=True))
        cand = jnp.where(hit, neg_inf, cand)
    wsum = functools.reduce(jnp.add, ws)

    chosen = functools.reduce(jnp.logical_or, hits)
    onehot = jnp.where(chosen, 1.0, 0.0)
    tr = lax.broadcasted_iota(jnp.int32, (tm, tm), 0)
    tc = lax.broadcasted_iota(jnp.int32, (tm, tm), 1)
    before = (tr < tc).astype(BF16)
    prior = jnp.dot(onehot.astype(BF16), before, preferred_element_type=F32) + run_ref[...]
    grow = lax.broadcasted_iota(jnp.int32, (LANES, tm), 0)
    gates = jnp.zeros((LANES, tm), F32)
    for k in range(TOP_K):
        idx_ref[k:k + 1, :] = idxs[k]
        gates = jnp.where(grow == k, ws[k] / wsum * ROUTED_SCALE, gates)
        rank_ref[k:k + 1, :] = jnp.sum(jnp.where(hits[k], prior, 0.0), axis=0,
                                       keepdims=True).astype(jnp.int32)
    gate_ref[...] = gates.T
    run_ref[...] = run_ref[...] + jnp.sum(onehot, axis=1, keepdims=True)
    cnt_ref[...] = run_ref[...]


def _router(x2d, w_router, router_bias, tm):
    T, D = x2d.shape
    E = w_router.shape[1]
    tok = lambda i: (0, i)
    fixed = lambda i: (0, 0)
    return pl.pallas_call(
        functools.partial(_router_kernel, tm=tm),
        out_shape=(jax.ShapeDtypeStruct((TOP_K, T), jnp.int32),
                   jax.ShapeDtypeStruct((T, LANES), F32),
                   jax.ShapeDtypeStruct((TOP_K, T), jnp.int32),
                   jax.ShapeDtypeStruct((E, 1), F32)),
        grid=(T // tm,),
        in_specs=[pl.BlockSpec((tm, D), lambda i: (i, 0)),
                  pl.BlockSpec((E, D), fixed),
                  pl.BlockSpec((E, 1), fixed)],
        out_specs=(pl.BlockSpec((TOP_K, tm), tok), pl.BlockSpec((tm, LANES), lambda i: (i, 0)),
                   pl.BlockSpec((TOP_K, tm), tok), pl.BlockSpec((E, 1), fixed)),
        scratch_shapes=[pltpu.VMEM((E, 1), F32)],
        compiler_params=_cparams("arbitrary"),
        name="router",
    )(x2d, w_router.T, router_bias.reshape(E, 1).astype(F32))


def _slots_kernel(idx_ref, rank_ref, start_ref, o_ref):
    E = start_ref.shape[0]
    tm = idx_ref.shape[1]
    eiota = lax.broadcasted_iota(jnp.int32, (E, tm), 0)
    start = start_ref[...]
    for k in range(TOP_K):
        base = jnp.sum(jnp.where(eiota == idx_ref[k:k + 1, :], start, 0.0), axis=0, keepdims=True)
        o_ref[k:k + 1, :] = base.astype(jnp.int32) + rank_ref[k:k + 1, :]


def _slots(idx, rank, seg_start, tm):
    T = idx.shape[1]
    E = seg_start.shape[0]
    tok = lambda i: (0, i)
    return pl.pallas_call(
        _slots_kernel,
        out_shape=jax.ShapeDtypeStruct((TOP_K, T), jnp.int32),
        grid=(T // tm,),
        in_specs=[pl.BlockSpec((TOP_K, tm), tok), pl.BlockSpec((TOP_K, tm), tok),
                  pl.BlockSpec((E, 1), lambda i: (0, 0))],
        out_specs=pl.BlockSpec((TOP_K, tm), tok),
        compiler_params=_cparams("parallel"),
        name="moe_slots",
    )(idx, rank, seg_start.astype(F32).reshape(E, 1))


def _row_copy(src_ref, src_row, dst_ref, dst_row, sem):
    return pltpu.make_async_copy(src_ref.at[pl.ds(src_row, 1), :],
                                 dst_ref.at[pl.ds(dst_row, 1), :], sem)


U32 = jnp.uint32


def _pack_halves(y):
    n = y.shape[1] // 2
    bits = lambda v: lax.bitcast_convert_type(v.astype(BF16).astype(F32), U32)
    return (bits(y[:, :n]) >> 16) | bits(y[:, n:])


def _unpack_halves(p):
    return (lax.bitcast_convert_type(p << 16, F32),
            lax.bitcast_convert_type(p & U32(0xFFFF0000), F32))


def _dispatch_kernel(seg_ref, dest_ref, x_ref, xs_ref, zero_ref, xp_ref, zsem, sem, *, tt):
    E = seg_ref.shape[1]
    xp_ref[...] = _pack_halves(x_ref[...])

    def zero_copy(e):
        start = pl.multiple_of(seg_ref[1, e] - EXPERT_ROWS, EXPERT_ROWS)
        return pltpu.make_async_copy(zero_ref, xs_ref.at[pl.ds(start, EXPERT_ROWS), :], zsem)

    @pl.when(pl.program_id(0) == 0)
    def _():
        zero_ref[...] = jnp.zeros_like(zero_ref)
        for go in (lambda c: c.start(), lambda c: c.wait()):
            def body(e, carry, go=go):
                @pl.when(seg_ref[1, e] > seg_ref[0, e])
                def _():
                    go(zero_copy(e))
                return carry
            lax.fori_loop(0, E, body, 0)

    def issue(r, c):
        for k in range(TOP_K):
            _row_copy(xp_ref, r, xs_ref, dest_ref[k, r], sem).start()
        return c

    def drain(r, c):
        for k in range(TOP_K):
            _row_copy(xp_ref, 0, xs_ref, 0, sem).wait()
        return c

    lax.fori_loop(0, tt, issue, 0)
    lax.fori_loop(0, tt, drain, 0)


def _dispatch(x2d, dest, seg, n_rows, tt):
    T, D = x2d.shape
    return pl.pallas_call(
        functools.partial(_dispatch_kernel, tt=tt),
        out_shape=jax.ShapeDtypeStruct((n_rows, D // 2), U32),
        grid_spec=pltpu.PrefetchScalarGridSpec(
            num_scalar_prefetch=1,
            grid=(T // tt,),
            in_specs=[pl.BlockSpec((TOP_K, tt), lambda i, seg: (0, i), memory_space=pltpu.SMEM),
                      pl.BlockSpec((tt, D), lambda i, seg: (i, 0))],
            out_specs=pl.BlockSpec(memory_space=pl.ANY),
            scratch_shapes=[pltpu.VMEM((EXPERT_ROWS, D // 2), U32), pltpu.VMEM((tt, D // 2), U32),
                            pltpu.SemaphoreType.DMA, pltpu.SemaphoreType.DMA]),
        compiler_params=_cparams("arbitrary"),
        name="moe_dispatch",
    )(seg, dest, x2d)


def _silu(g):
    return g * jax.nn.sigmoid(g)


SCHED_FIRST, SCHED_SLOT, SCHED_NEXT = 0, 1, 2


def _expert_kernel(sched_ref, n_used_ref, x_ref, wg_hbm, wu_hbm, wd_hbm, o_ref,
                   wg_buf, wu_buf, wd_buf, sem, *, layer, first_expert_row):
    i = pl.program_id(0)
    slot = sched_ref[SCHED_SLOT, i]

    def copies(expert, slot):
        pairs = ((wg_hbm, wg_buf), (wu_hbm, wu_buf), (wd_hbm, wd_buf))
        return [pltpu.make_async_copy(src.at[layer, expert], dst.at[slot], sem.at[slot, n])
                for n, (src, dst) in enumerate(pairs)]

    @pl.when(i == 0)
    def _():
        for c in copies(sched_ref[first_expert_row, 0], 0):
            c.start()

    @pl.when(sched_ref[SCHED_FIRST, i] == 1)
    def _():
        for c in copies(0, slot):
            c.wait()

        @pl.when(sched_ref[SCHED_NEXT, i] >= 0)
        def _():
            for c in copies(sched_ref[SCHED_NEXT, i], 1 - slot):
                c.start()

    @pl.when(i < n_used_ref[0])
    def _():
        x = jnp.concatenate(_unpack_halves(x_ref[...]), axis=1).astype(BF16)
        g = jnp.dot(x, wg_buf[slot].astype(BF16), preferred_element_type=F32)
        u = jnp.dot(x, wu_buf[slot].astype(BF16), preferred_element_type=F32)
        a = (_silu(g) * u).astype(BF16)
        o_ref[...] = _pack_halves(jnp.dot(a, wd_buf[slot].astype(BF16), preferred_element_type=F32))

    @pl.when(i >= n_used_ref[0])
    def _():
        o_ref[...] = jnp.zeros_like(o_ref)


def _expert_schedule(seg_start, seg_end, n_blocks):
    E = seg_end.shape[0]
    first_row = jnp.arange(n_blocks, dtype=jnp.int32) * EXPERT_ROWS
    blk_e = jnp.minimum(jnp.sum(seg_end[None, :] <= first_row[:, None], axis=1), E - 1)
    used = first_row < seg_end[-1]
    first = used & (first_row == seg_start[blk_e])
    ordinal = jnp.cumsum(first) - 1
    ids = jnp.where(seg_end > seg_start, jnp.arange(E), E)
    later = lax.cummin(ids, reverse=True)
    nxt = jnp.concatenate([later[1:], jnp.full((1,), E, later.dtype)])[blk_e]
    nxt = jnp.where(nxt < E, nxt, -1)
    return jnp.stack([first, ordinal % 2, nxt, blk_e]).astype(jnp.int32)


def _expert_mlp(xs, sched, n_used, w_gate, w_up, w_down, layer):
    n_rows, half = xs.shape
    D = 2 * half
    F = w_gate.shape[-1]
    tm = EXPERT_ROWS
    hbm = pl.BlockSpec(memory_space=pl.ANY)
    return pl.pallas_call(
        functools.partial(_expert_kernel, layer=layer, first_expert_row=sched.shape[0] - 1),
        out_shape=jax.ShapeDtypeStruct((n_rows, half), U32),
        grid_spec=pltpu.PrefetchScalarGridSpec(
            num_scalar_prefetch=2,
            grid=(n_rows // tm,),
            in_specs=[pl.BlockSpec((tm, half), lambda i, sc, nu: (jnp.minimum(i, nu[0] - 1), 0)),
                      hbm, hbm, hbm],
            out_specs=pl.BlockSpec((tm, half), lambda i, sc, nu: (i, 0)),
            scratch_shapes=[pltpu.VMEM((2, D, F), F32), pltpu.VMEM((2, D, F), F32),
                            pltpu.VMEM((2, F, D), F32), pltpu.SemaphoreType.DMA((2, 3))]),
        compiler_params=_cparams("arbitrary"),
        name="moe_experts",
    )(sched, n_used, xs, w_gate, w_up, w_down)


SC_CORES = 2
SC_SUBCORES = 16
SC_CHUNK = 128


def _sc_gather_rows(table, idx):
    W = table.shape[1]
    B = idx.shape[0]
    workers = SC_CORES * SC_SUBCORES
    chunks = B // (workers * SC_CHUNK)
    assert chunks * workers * SC_CHUNK == B
    mesh = plsc.VectorSubcoreMesh(core_axis_name="c", subcore_axis_name="s",
                                  num_cores=SC_CORES, num_subcores=SC_SUBCORES)

    @functools.partial(
        pl.kernel, mesh=mesh,
        out_type=jax.ShapeDtypeStruct((B, W), table.dtype),
        scratch_types=[pltpu.VMEM((chunks, SC_CHUNK), jnp.int32),
                       pltpu.VMEM((SC_CHUNK, W), table.dtype),
                       pltpu.SemaphoreType.DMA],
        name="moe_gather_sc")
    def gather(table_hbm, idx_hbm, out_hbm, idx_v, rows_v, sem):
        wid = lax.axis_index("s") * SC_CORES + lax.axis_index("c")
        pltpu.sync_copy(idx_hbm.at[wid], idx_v)

        @pl.loop(0, chunks)
        def _(j):
            pltpu.async_copy(table_hbm.at[idx_v.at[j]], rows_v, sem).wait()
            pltpu.sync_copy(rows_v, out_hbm.at[pl.ds((wid * chunks + j) * SC_CHUNK, SC_CHUNK)])

    return gather(table, idx.reshape(workers, chunks, SC_CHUNK))


def _combine_kernel(x_ref, gate_ref, rows_ref, wsg_ref, wsu_ref, wsd_ref, g_ref, b_ref, o_ref,
                    *, alpha):
    x = x_ref[...]
    xb = x.astype(BF16)
    g = jnp.dot(xb, wsg_ref[...], preferred_element_type=F32)
    u = jnp.dot(xb, wsu_ref[...], preferred_element_type=F32)
    y = jnp.dot((_silu(g) * u).astype(BF16), wsd_ref[...], preferred_element_type=F32)
    gates = gate_ref[...]
    half = y.shape[1] // 2
    left, right = y[:, :half], y[:, half:]
    for k in range(TOP_K):
        lo, hi = _unpack_halves(rows_ref[k])
        left = left + lo * gates[:, k:k + 1]
        right = right + hi * gates[:, k:k + 1]
    y = jnp.concatenate([left, right], axis=1)
    o_ref[...] = _layer_norm(alpha * x + y, g_ref[...], b_ref[...])


def _combine(x2d, ys, dest, gates_tok, ws_gate, ws_up, ws_down, g, b, alpha, tt):
    T, D = x2d.shape
    F = ws_gate.shape[1]
    row = lambda i: (i, 0)
    fixed = lambda i: (0, 0)
    rows = _sc_gather_rows(lax.bitcast_convert_type(ys, jnp.int32), dest.reshape(-1))
    rows = lax.bitcast_convert_type(rows, U32).reshape(TOP_K, T, D // 2)
    return pl.pallas_call(
        functools.partial(_combine_kernel, alpha=alpha),
        out_shape=jax.ShapeDtypeStruct((T, D), F32),
        grid=(T // tt,),
        in_specs=[pl.BlockSpec((tt, D), row),
                  pl.BlockSpec((tt, LANES), row),
                  pl.BlockSpec((TOP_K, tt, D // 2), lambda i: (0, i, 0)),
                  pl.BlockSpec((D, F), fixed), pl.BlockSpec((D, F), fixed),
                  pl.BlockSpec((F, D), fixed),
                  pl.BlockSpec((1, D), fixed), pl.BlockSpec((1, D), fixed)],
        out_specs=pl.BlockSpec((tt, D), row),
        compiler_params=_cparams("parallel"),
        name="moe_combine",
    )(x2d, gates_tok, rows, ws_gate.astype(BF16), ws_up.astype(BF16), ws_down.astype(BF16),
      g.reshape(1, D), b.reshape(1, D))


def _pick(n, pref):
    t = min(n, pref)
    while n % t:
        t //= 2
    return t


def _moe_layer(x2d, w_router, router_bias, w_gate, w_up, w_down, layer, ws_gate, ws_up, ws_down,
               g, b, alpha):
    T, D = x2d.shape
    E = w_router.shape[1]
    idx, gates, rank, counts = _router(x2d, w_router, router_bias, _pick(T, 256))
    counts = counts[:, 0].astype(jnp.int32)
    padded = (counts + EXPERT_ROWS - 1) // EXPERT_ROWS * EXPERT_ROWS
    seg_end = jnp.cumsum(padded)
    seg_start = seg_end - padded
    n_blocks = (T * TOP_K + E * (EXPERT_ROWS - 1)) // EXPERT_ROWS
    dest = _slots(idx, rank, seg_start, _pick(T, 1024))
    sched = _expert_schedule(seg_start, seg_end, n_blocks)
    n_used = (seg_end[-1:] // EXPERT_ROWS).astype(jnp.int32)
    tt = _pick(T, 128)
    xs = _dispatch(x2d, dest, jnp.stack([seg_start, seg_end]).astype(jnp.int32),
                   n_blocks * EXPERT_ROWS, tt)
    ys = _expert_mlp(xs, sched, n_used, w_gate, w_up, w_down, layer)
    return _combine(x2d, ys, dest, gates, ws_gate, ws_up, ws_down, g, b, alpha, tt)


def _mixer_layer(x2d, i, B, S, p, alpha):
    T, D = x2d.shape
    tm = _pick(T, 512)
    tn = _pick(D, 1024)
    j = i // 2
    w_in = p["moba_w_in"][j] if i % 2 == 0 else p["fox_w_in"][j]
    w_qv_t = jnp.concatenate([w_in[:, :D].T * HEAD_DIM ** -0.5, w_in[:, 2 * D:3 * D].T]).astype(BF16)
    qv_t = _project_t(x2d, w_qv_t, tm, tn)
    k = _project(x2d, w_in[:, D:2 * D].astype(BF16), tm, tn)
    if i % 2 == 0:
        k_aug, k_mean = _moba_kprep(k.reshape(B, S, D))
        attn_t = _moba_attention(qv_t, k_aug, k_mean, p["rel_bias"], B, S)
        w_out = p["moba_w_out"][j]
    else:
        c_tok = _fox_gates(x2d.reshape(B, S, D), w_in[:, 3 * D:], p["fox_b_f"][j], _pick(S, 512))
        k_aug = _fox_kprep(k, c_tok.reshape(T, LANES), tm).reshape(HEADS_PER_TILE, B, S, D)
        attn_t = _fox_attention(qv_t, k_aug, B, S, _pick(S, 256))
        w_out = p["fox_w_out"][j]
    return _outproj_ln(attn_t, w_out.astype(BF16), x2d, p["ln1_g"][i], p["ln1_b"][i], alpha, tm)


def kernel(x, rel_bias, moba_w_in, moba_w_out, fox_w_in, fox_b_f, fox_w_out, ln1_g, ln1_b, ln2_g,
           ln2_b, w_router, router_bias, w_gate, w_up, w_down, ws_gate, ws_up, ws_down):
    B, S, D = x.shape
    depth = ln1_g.shape[0]
    alpha = (2 * depth) ** 0.25
    p = dict(rel_bias=rel_bias, moba_w_in=moba_w_in, moba_w_out=moba_w_out, fox_w_in=fox_w_in,
             fox_b_f=fox_b_f, fox_w_out=fox_w_out, ln1_g=ln1_g, ln1_b=ln1_b)
    x2d = x.reshape(B * S, D)
    for i in range(depth):
        x2d = _mixer_layer(x2d, i, B, S, p, alpha)
        x2d = _moe_layer(x2d, w_router[i], router_bias[i], w_gate, w_up, w_down, i,
                         ws_gate[i], ws_up[i], ws_down[i], ln2_g[i], ln2_b[i], alpha)
    return x2d.reshape(B, S, D)
```

```python
import functools
import math

import jax
import jax.numpy as jnp
from jax import lax
from jax.experimental import pallas as pl
from jax.experimental.pallas import tpu as pltpu
from jax.experimental.pallas import tpu_sc as plsc

F32 = jnp.float32
BF16 = jnp.bfloat16
HIGHEST = lax.Precision.HIGHEST

HEAD_DIM = 64
MOBA_BLOCK = 256
MOBA_TOPK = 3
REL_MAX_DIST = 128
TOP_K = 8
N_GROUPS = 8
TOPK_GROUPS = 4
ROUTED_SCALE = 2.5
LN_EPS = 1e-5
NEG = -1e30

LANES = 128
HEADS_PER_TILE = LANES // HEAD_DIM
EXPERT_ROWS = 512
VMEM_LIMIT = 48 * 1024 * 1024


def _cparams(*sem):
    return pltpu.CompilerParams(dimension_semantics=sem, vmem_limit_bytes=VMEM_LIMIT)


def _nt_dot(a, b, **kw):
    return lax.dot_general(a, b, (((1,), (1,)), ((), ())), preferred_element_type=F32, **kw)


def _proj_kernel(x_ref, w_ref, o_ref):
    o_ref[...] = jnp.dot(x_ref[...].astype(BF16), w_ref[...],
                         preferred_element_type=F32).astype(o_ref.dtype)


def _project(x2d, w, tm, tn):
    T, K = x2d.shape
    N = w.shape[1]
    return pl.pallas_call(
        _proj_kernel,
        out_shape=jax.ShapeDtypeStruct((T, N), BF16),
        grid=(T // tm, N // tn),
        in_specs=[pl.BlockSpec((tm, K), lambda i, j: (i, 0)),
                  pl.BlockSpec((K, tn), lambda i, j: (0, j))],
        out_specs=pl.BlockSpec((tm, tn), lambda i, j: (i, j)),
        compiler_params=_cparams("parallel", "arbitrary"),
        name="k_proj",
    )(x2d, w)


def _proj_t_kernel(x_ref, wt_ref, o_ref):
    o_ref[...] = _nt_dot(wt_ref[...], x_ref[...].astype(BF16)).astype(o_ref.dtype)


def _project_t(x2d, wt, tm, tn):
    T, K = x2d.shape
    N = wt.shape[0]
    return pl.pallas_call(
        _proj_t_kernel,
        out_shape=jax.ShapeDtypeStruct((N, T), BF16),
        grid=(T // tm, N // tn),
        in_specs=[pl.BlockSpec((tm, K), lambda i, j: (i, 0)),
                  pl.BlockSpec((tn, K), lambda i, j: (j, 0))],
        out_specs=pl.BlockSpec((tn, tm), lambda i, j: (j, i)),
        compiler_params=_cparams("parallel", "arbitrary"),
        name="qv_proj_t",
    )(x2d, wt)


def _layer_norm(r, g, b):
    mu = jnp.mean(r, axis=-1, keepdims=True)
    d = r - mu
    var = jnp.mean(d * d, axis=-1, keepdims=True)
    return d * lax.rsqrt(var + LN_EPS) * g + b


def _pack_halves(y):
    n = y.shape[1] // 2
    bits = lambda v: lax.bitcast_convert_type(v.astype(BF16).astype(F32), jnp.int32)
    left = bits(y[:, :n])
    return lax.shift_right_logical(left, jnp.full_like(left, 16)) | bits(y[:, n:])


def _unpack_halves(p):
    return (lax.bitcast_convert_type(p << 16, F32),
            lax.bitcast_convert_type(p & jnp.int32(-65536), F32))


def _outproj_ln_kernel(at_ref, w_ref, x_ref, g_ref, b_ref, o_ref, packed_ref, *, alpha):
    y = lax.dot_general(at_ref[...], w_ref[...], (((0,), (0,)), ((), ())),
                        preferred_element_type=F32)
    out = _layer_norm(alpha * x_ref[...] + y, g_ref[...], b_ref[...])
    o_ref[...] = out
    packed_ref[...] = _pack_halves(out)


def _outproj_ln(attn_t, w_out, x2d, g, b, alpha, tm):
    T, D = x2d.shape
    row = lambda i: (i, 0)
    fixed = lambda i: (0, 0)
    return pl.pallas_call(
        functools.partial(_outproj_ln_kernel, alpha=alpha),
        out_shape=(jax.ShapeDtypeStruct((T, D), F32), jax.ShapeDtypeStruct((T, D // 2), jnp.int32)),
        grid=(T // tm,),
        in_specs=[pl.BlockSpec((D, tm), lambda i: (0, i)), pl.BlockSpec((D, D), fixed),
                  pl.BlockSpec((tm, D), row), pl.BlockSpec((1, D), fixed),
                  pl.BlockSpec((1, D), fixed)],
        out_specs=(pl.BlockSpec((tm, D), row), pl.BlockSpec((tm, D // 2), row)),
        compiler_params=_cparams("parallel"),
        name="outproj_ln",
    )(attn_t, w_out, x2d, g.reshape(1, D), b.reshape(1, D))


def _spare(a, n=0):
    return ((a + 1) % HEADS_PER_TILE) * HEAD_DIM + n


def _scores(chains):
    out = []
    for blocks in chains:
        parts = []
        for keys, query, _, extra, mask in blocks:
            s = jnp.dot(keys(), query(), preferred_element_type=F32)
            if extra is not None:
                s = s + extra()
            if mask is not None:
                s = jnp.where(mask, s, NEG)
            parts.append(s)
        out.append(parts)
    return out


def _absorb(carries, scores, chains):
    probs = []
    for (m, _), parts in zip(carries, scores):
        tops = [jnp.max(s, axis=0, keepdims=True) for s in parts]
        m_new = functools.reduce(jnp.maximum, tops, m)
        probs.append((m_new, jnp.exp(m - m_new), [jnp.exp(s - m_new).astype(BF16) for s in parts]))
    out = []
    for (_, acc), (m_new, decay, ps), blocks in zip(carries, probs, chains):
        acc = decay * acc
        for (_, _, values, _, _), p in zip(blocks, ps):
            acc = acc + jnp.dot(values(), p, preferred_element_type=F32)
        out.append((m_new, acc))
    return tuple(out)


def _attend(carries, chains):
    return _absorb(carries, _scores(chains), chains)


def _attend_pipelined(carries, n, group, stage_refs):
    def stash(chains, ref):
        for a, parts in enumerate(_scores(chains)):
            for b, s in enumerate(parts):
                ref[a, b] = s

    def fetch(chains, ref):
        return [[ref[a, b] for b in range(len(blocks))] for a, blocks in enumerate(chains)]

    odd = n % 2
    carries = lax.cond(odd == 1, lambda c: _attend(c, group(0)), lambda c: c, carries)
    trips = n // 2
    last = n - 1

    def run(carries):
        stash(group(odd), stage_refs[0])

        def body(t, carries):
            g = odd + 2 * t
            stash(group(g + 1), stage_refs[1])
            carries = _absorb(carries, fetch(group(g), stage_refs[0]), group(g))
            stash(group(jnp.minimum(g + 2, last)), stage_refs[0])
            return _absorb(carries, fetch(group(g + 1), stage_refs[1]), group(g + 1))

        return lax.fori_loop(0, trips, body, carries)

    return lax.cond(trips >= 1, run, lambda c: c, carries)


GROUP_BLOCKS = 2


def _stage_scratch(tq):
    return [pltpu.VMEM((HEADS_PER_TILE, GROUP_BLOCKS, tq, tq), F32) for _ in range(2)]


def _scores_init(tq):
    return jnp.full((1, tq), -jnp.inf, F32), jnp.zeros((LANES, tq), F32)


def _finish(carries, o_ref):
    row = lax.broadcasted_iota(jnp.int32, carries[0][1].shape, 0)
    outs = []
    for a, (_, acc) in enumerate(carries):
        s = _spare(a)
        outs.append(acc / acc[s:s + 1, :])
    o_ref[...] = jnp.where(row < HEAD_DIM, outs[0], outs[1]).astype(o_ref.dtype)


def _fill_v_aug(vt_ref, vaug_ref, tk):
    n_tiles = vt_ref.shape[1] // tk
    row = lax.broadcasted_iota(jnp.int32, (LANES, tk), 0)
    for n in range(n_tiles):
        v = vt_ref[:, n * tk:(n + 1) * tk]
        for a in range(HEADS_PER_TILE):
            vaug_ref[a, n] = jnp.where(row // HEAD_DIM == a, v, jnp.ones_like(v))


FOX_SPLIT = 3


def _fox_kernel(qt_ref, k_ref, vt_ref, o_ref, vaug_ref, stage0_ref, stage1_ref, *, tq):
    i = pl.program_id(2)

    @pl.when(i == 0)
    def _():
        _fill_v_aug(vt_ref, vaug_ref, tq)

    row = lax.broadcasted_iota(jnp.int32, (LANES, tq), 0)
    qt = qt_ref[...]
    queries = []
    for a in range(HEADS_PER_TILE):
        offs = (row >= _spare(a)) & (row < _spare(a, FOX_SPLIT))
        queries.append(jnp.where(row // HEAD_DIM == a, qt,
                                 jnp.where(offs, -1.0, 0.0).astype(BF16)))
    key = lax.broadcasted_iota(jnp.int32, (tq, tq), 0)
    qry = lax.broadcasted_iota(jnp.int32, (tq, tq), 1)
    causal = key <= qry

    def block(a, j, mask=None):
        start = pl.multiple_of(j * tq, tq)
        return (lambda: k_ref[a, 0, pl.ds(start, tq), :], lambda: queries[a],
                lambda: vaug_ref[a, j], None, mask)

    heads = range(HEADS_PER_TILE)
    carries = lax.cond(
        i % 2 == 1,
        lambda c: _attend(c, [[block(a, i - 1), block(a, i, causal)] for a in heads]),
        lambda c: _attend(c, [[block(a, i, causal)] for a in heads]),
        tuple(_scores_init(tq) for _ in heads))
    carries = _attend_pipelined(
        carries, i // 2, lambda j: [[block(a, 2 * j), block(a, 2 * j + 1)] for a in heads],
        (stage0_ref, stage1_ref))
    _finish(carries, o_ref)


def _fox_attention(qv_t, k_aug, B, S, tq):
    D = k_aug.shape[-1]
    n_tiles = D // LANES
    nq = S // tq
    return pl.pallas_call(
        functools.partial(_fox_kernel, tq=tq),
        out_shape=jax.ShapeDtypeStruct((D, B * S), BF16),
        grid=(B, n_tiles, nq),
        in_specs=[pl.BlockSpec((LANES, tq), lambda b, h, i: (h, b * nq + i)),
                  pl.BlockSpec((HEADS_PER_TILE, 1, S, LANES), lambda b, h, i: (0, b, 0, h)),
                  pl.BlockSpec((LANES, S), lambda b, h, i: (n_tiles + h, b))],
        out_specs=pl.BlockSpec((LANES, tq), lambda b, h, i: (h, b * nq + i)),
        scratch_shapes=[pltpu.VMEM((HEADS_PER_TILE, nq, LANES, tq), BF16)] + _stage_scratch(tq),
        compiler_params=_cparams("parallel", "parallel", "arbitrary"),
        name="fox_attention",
    )(qv_t, k_aug, qv_t)


def _log_sigmoid(z):
    return jnp.minimum(z, 0.0) - jnp.log1p(jnp.exp(-jnp.abs(z)))


def _fox_gates_kernel(x_ref, w_ref, b_ref, o_ref, carry_ref, *, ts):
    @pl.when(pl.program_id(1) == 0)
    def _():
        carry_ref[...] = jnp.zeros_like(carry_ref)

    z = jnp.dot(x_ref[0], w_ref[...], preferred_element_type=F32, precision=HIGHEST)
    lf = _log_sigmoid(z + b_ref[...])
    r = lax.broadcasted_iota(jnp.int32, (ts, ts), 0)
    c = lax.broadcasted_iota(jnp.int32, (ts, ts), 1)
    tri = (c <= r).astype(F32)
    cum = jnp.dot(tri, lf, preferred_element_type=F32, precision=HIGHEST) + carry_ref[...]
    carry_ref[...] = cum[ts - 1:ts, :]
    o_ref[0] = cum


def _fox_gates(x3d, w_f, b_f, ts):
    B, S, D = x3d.shape
    H = w_f.shape[1]
    w_pad = jnp.zeros((D, LANES), F32).at[:, :H].set(w_f)
    b_pad = jnp.zeros((1, LANES), F32).at[0, :H].set(b_f)
    return pl.pallas_call(
        functools.partial(_fox_gates_kernel, ts=ts),
        out_shape=jax.ShapeDtypeStruct((B, S, LANES), F32),
        grid=(B, S // ts),
        in_specs=[pl.BlockSpec((1, ts, D), lambda b, s: (b, s, 0)),
                  pl.BlockSpec((D, LANES), lambda b, s: (0, 0)),
                  pl.BlockSpec((1, LANES), lambda b, s: (0, 0))],
        out_specs=pl.BlockSpec((1, ts, LANES), lambda b, s: (b, s, 0)),
        scratch_shapes=[pltpu.VMEM((1, LANES), F32)],
        compiler_params=_cparams("parallel", "arbitrary"),
        name="fox_gates",
    )(x3d, w_pad, b_pad)


def _fox_kprep_kernel(k_ref, c_ref, o_ref):
    tm, D = k_ref.shape
    lane = lax.broadcasted_iota(jnp.int32, (tm, LANES), 1)
    for t in range(D // LANES):
        k = k_ref[:, t * LANES:(t + 1) * LANES].astype(F32)
        for a in range(HEADS_PER_TILE):
            h = t * HEADS_PER_TILE + a
            rest = c_ref[:, h:h + 1]
            aug = jnp.zeros((tm, LANES), F32)
            for n in range(FOX_SPLIT):
                piece = rest.astype(BF16).astype(F32)
                rest = rest - piece
                aug = jnp.where(lane == _spare(a, n), piece, aug)
            o_ref[a, :, t * LANES:(t + 1) * LANES] = jnp.where(lane // HEAD_DIM == a, k,
                                                               aug).astype(BF16)


def _fox_kprep(k2d, c_tok, tm):
    T, D = k2d.shape
    return pl.pallas_call(
        _fox_kprep_kernel,
        out_shape=jax.ShapeDtypeStruct((HEADS_PER_TILE, T, D), BF16),
        grid=(T // tm,),
        in_specs=[pl.BlockSpec((tm, D), lambda i: (i, 0)),
                  pl.BlockSpec((tm, LANES), lambda i: (i, 0))],
        out_specs=pl.BlockSpec((HEADS_PER_TILE, tm, D), lambda i: (0, i, 0)),
        compiler_params=_cparams("parallel"),
        name="fox_kprep",
    )(k2d, c_tok)


MOBA_SLAB = 16


def _moba_kernel(bfar_ref, qt_ref, k_ref, vt_ref, kmean_ref, bown_ref, badj_ref, o_ref,
                 vaug_ref, sel_ref, stage0_ref, stage1_ref, *, blk):
    t = pl.program_id(1)
    i = pl.program_id(2)
    nb = kmean_ref.shape[1]

    @pl.when(i == 0)
    def _():
        _fill_v_aug(vt_ref, vaug_ref, blk)

    row = lax.broadcasted_iota(jnp.int32, (LANES, blk), 0)
    brow = lax.broadcasted_iota(jnp.int32, (nb, blk), 0)
    qt = qt_ref[...]
    zero = jnp.zeros_like(qt)
    base, slabs = [], []
    for a in range(HEADS_PER_TILE):
        qa = jnp.where(row // HEAD_DIM == a, qt, zero)
        base.append(qa)
        bscore = jnp.dot(kmean_ref[0], qa.astype(F32), preferred_element_type=F32,
                         precision=HIGHEST)
        cand = jnp.where(brow < i, bscore, NEG)
        sel = jnp.zeros((nb, blk), F32)
        for _ in range(MOBA_TOPK):
            mx = jnp.max(cand, axis=0, keepdims=True)
            first = jnp.min(jnp.where(cand == mx, brow, nb), axis=0, keepdims=True)
            hit = brow == first
            sel = jnp.where(hit, 1.0, sel)
            cand = jnp.where(hit, -jnp.inf, cand)
        sel_ref[a] = jnp.where(brow < i, sel, 0.0)
        slabs.append(_spare(a) // MOBA_SLAB * MOBA_SLAB)

    srow = lax.broadcasted_iota(jnp.int32, (MOBA_SLAB, blk), 0)

    def query(a, j, bias_hi, bias_lo):
        s0 = slabs[a]
        chosen = sel_ref[a, pl.ds(j, 1), :] > 0.0
        slab = jnp.where(srow == _spare(a) - s0, jnp.where(chosen, bias_hi, NEG),
                         jnp.where(srow == _spare(a, 1) - s0, bias_lo, 0.0)).astype(BF16)
        head = [base[a][:s0]] if s0 else []
        tail = [base[a][s0 + MOBA_SLAB:]] if s0 + MOBA_SLAB < LANES else []
        return jnp.concatenate(head + [slab] + tail, axis=0)

    key = lax.broadcasted_iota(jnp.int32, (blk, blk), 0)
    qry = lax.broadcasted_iota(jnp.int32, (blk, blk), 1)
    causal = key <= qry
    heads = range(HEADS_PER_TILE)

    def keys(a, j):
        start = pl.multiple_of(j * blk, blk)
        return lambda: k_ref[a, 0, pl.ds(start, blk), :]

    def own(a):
        return keys(a, i), lambda: base[a], lambda: vaug_ref[a, i], lambda: bown_ref[a], causal

    def prev(a):
        return (keys(a, i - 1), lambda: query(a, i - 1, 0.0, 0.0), lambda: vaug_ref[a, i - 1],
                lambda: badj_ref[a], None)

    def far(a, j):
        h = t * HEADS_PER_TILE + a
        return (keys(a, j), lambda: query(a, j, bfar_ref[0, h], bfar_ref[1, h]),
                lambda: vaug_ref[a, j], None, None)

    carries = lax.cond(
        i >= 1,
        lambda c: _attend(c, [[prev(a), own(a)] for a in heads]),
        lambda c: _attend(c, [[own(a)] for a in heads]),
        tuple(_scores_init(blk) for _ in heads))
    carries = lax.cond(
        (i >= 2) & (i % 2 == 0),
        lambda c: _attend(c, [[far(a, i - 2)] for a in heads]),
        lambda c: c, carries)
    carries = _attend_pipelined(
        carries, (i - 1) // 2, lambda j: [[far(a, 2 * j), far(a, 2 * j + 1)] for a in heads],
        (stage0_ref, stage1_ref))
    _finish(carries, o_ref)


def _rel_bucket(dist, n_buckets):
    n = jnp.maximum(dist, 0)
    max_exact = n_buckets // 2
    nf = jnp.maximum(n, 1).astype(F32)
    large = max_exact + (jnp.log(nf / max_exact) / math.log(REL_MAX_DIST / max_exact)
                         * (n_buckets - max_exact)).astype(jnp.int32)
    large = jnp.minimum(large, n_buckets - 1)
    return jnp.where(n < max_exact, n, large)


def _moba_attention(qv_t, k_aug, k_mean, rel_bias, B, S):
    D = k_aug.shape[-1]
    n_tiles = D // LANES
    blk = MOBA_BLOCK
    nb = S // blk
    hp = HEADS_PER_TILE
    r = jnp.arange(blk)
    delta = r[None, :] - r[:, None]
    bias_t = rel_bias.T.astype(F32)
    n_buckets = rel_bias.shape[0]
    def table(dist):
        onehot = jax.nn.one_hot(_rel_bucket(dist, n_buckets), n_buckets, dtype=F32)
        return jnp.einsum('crn,hn->hcr', onehot, bias_t, precision=HIGHEST)

    b_own = table(delta)
    b_adj = table(delta + blk)
    b_far = bias_t[:, n_buckets - 1]
    far_hi = b_far.astype(BF16).astype(F32)
    far_lo = (b_far - far_hi).astype(BF16).astype(F32)
    b_far2 = jnp.stack([far_hi, far_lo])
    return pl.pallas_call(
        functools.partial(_moba_kernel, blk=blk),
        out_shape=jax.ShapeDtypeStruct((D, B * S), BF16),
        grid_spec=pltpu.PrefetchScalarGridSpec(
            num_scalar_prefetch=1,
            grid=(B, n_tiles, nb),
            in_specs=[pl.BlockSpec((LANES, blk), lambda b, h, i, f: (h, b * nb + i)),
                      pl.BlockSpec((hp, 1, S, LANES), lambda b, h, i, f: (0, b, 0, h)),
                      pl.BlockSpec((LANES, S), lambda b, h, i, f: (n_tiles + h, b)),
                      pl.BlockSpec((1, nb, LANES), lambda b, h, i, f: (b, 0, h)),
                      pl.BlockSpec((hp, blk, blk), lambda b, h, i, f: (h, 0, 0)),
                      pl.BlockSpec((hp, blk, blk), lambda b, h, i, f: (h, 0, 0))],
            out_specs=pl.BlockSpec((LANES, blk), lambda b, h, i, f: (h, b * nb + i)),
            scratch_shapes=[pltpu.VMEM((hp, nb, LANES, blk), BF16),
                            pltpu.VMEM((hp, nb, blk), F32)] + _stage_scratch(blk)),
        compiler_params=_cparams("parallel", "parallel", "arbitrary"),
        name="moba_attention",
    )(b_far2, qv_t, k_aug, qv_t, k_mean, b_own, b_adj)


def _moba_kprep_kernel(k_ref, o_ref, mean_ref):
    k = k_ref[0]
    col = lax.broadcasted_iota(jnp.int32, k.shape, 1)
    lane = col % LANES
    for a in range(HEADS_PER_TILE):
        ones = (lane == _spare(a)) | (lane == _spare(a, 1))
        o_ref[a, 0] = jnp.where((col // HEAD_DIM) % HEADS_PER_TILE == a, k,
                                jnp.where(ones, 1.0, 0.0).astype(BF16))
    mean_ref[0, 0] = jnp.mean(k.astype(F32), axis=0, keepdims=True)


def _moba_kprep(k3d):
    B, S, D = k3d.shape
    blk = MOBA_BLOCK
    nb = S // blk
    k_aug, k_mean = pl.pallas_call(
        _moba_kprep_kernel,
        out_shape=(jax.ShapeDtypeStruct((HEADS_PER_TILE, B, S, D), BF16),
                   jax.ShapeDtypeStruct((B, nb, 1, D), F32)),
        grid=(B, nb),
        in_specs=[pl.BlockSpec((1, blk, D), lambda b, n: (b, n, 0))],
        out_specs=(pl.BlockSpec((HEADS_PER_TILE, 1, blk, D), lambda b, n: (0, b, n, 0)),
                   pl.BlockSpec((1, 1, 1, D), lambda b, n: (b, n, 0, 0))),
        compiler_params=_cparams("parallel", "parallel"),
        name="moba_kprep",
    )(k3d)
    return k_aug, k_mean.reshape(B, nb, D)


def _router_kernel(x_ref, wt_ref, bias_ref, idx_ref, gate_ref, rank_ref, cnt_ref, run_ref, *, tm):
    @pl.when(pl.program_id(0) == 0)
    def _():
        run_ref[...] = jnp.zeros_like(run_ref)

    E = wt_ref.shape[0]
    gsz = E // N_GROUPS
    logits = _nt_dot(wt_ref[...], x_ref[...], precision=HIGHEST)
    s = jax.nn.sigmoid(logits)
    sb = s + bias_ref[...]
    neg_inf = -jnp.inf

    giota = lax.broadcasted_iota(jnp.int32, (gsz, tm), 0)
    gscore = []
    for g in range(N_GROUPS):
        blk = sb[g * gsz:(g + 1) * gsz, :]
        m1 = jnp.max(blk, axis=0, keepdims=True)
        i1 = jnp.min(jnp.where(blk == m1, giota, gsz), axis=0, keepdims=True)
        m2 = jnp.max(jnp.where(giota == i1, neg_inf, blk), axis=0, keepdims=True)
        gscore.append(m1 + m2)

    gsel = [jnp.zeros((1, tm), jnp.bool_) for _ in range(N_GROUPS)]
    for _ in range(TOPK_GROUPS):
        mx = functools.reduce(jnp.maximum, gscore)
        found = jnp.zeros((1, tm), jnp.bool_)
        for g in range(N_GROUPS):
            hit = (gscore[g] == mx) & jnp.logical_not(found)
            gsel[g] = gsel[g] | hit
            found = found | hit
            gscore[g] = jnp.where(hit, neg_inf, gscore[g])
    emask = jnp.concatenate([jnp.broadcast_to(gsel[g], (gsz, tm)) for g in range(N_GROUPS)], axis=0)
    cand = jnp.where(emask, sb, NEG)

    eiota = lax.broadcasted_iota(jnp.int32, (E, tm), 0)
    hits, idxs, ws = [], [], []
    for _ in range(TOP_K):
        mx = jnp.max(cand, axis=0, keepdims=True)
        first = jnp.min(jnp.where(cand == mx, eiota, E), axis=0, keepdims=True)
        hit = eiota == first
        hits.append(hit)
        idxs.append(first)
        ws.append(jnp.sum(jnp.where(hit, s, 0.0), axis=0, keepdims=True))
        cand = jnp.where(hit, neg_inf, cand)
    wsum = functools.reduce(jnp.add, ws)

    chosen = functools.reduce(jnp.logical_or, hits)
    onehot = jnp.where(chosen, 1.0, 0.0)
    tr = lax.broadcasted_iota(jnp.int32, (tm, tm), 0)
    tc = lax.broadcasted_iota(jnp.int32, (tm, tm), 1)
    before = (tr < tc).astype(BF16)
    prior = jnp.dot(onehot.astype(BF16), before, preferred_element_type=F32) + run_ref[...]
    grow = lax.broadcasted_iota(jnp.int32, (LANES, tm), 0)
    gates = jnp.zeros((LANES, tm), F32)
    for k in range(TOP_K):
        idx_ref[k:k + 1, :] = idxs[k]
        gates = jnp.where(grow == k, ws[k] / wsum * ROUTED_SCALE, gates)
        rank_ref[k:k + 1, :] = jnp.sum(jnp.where(hits[k], prior, 0.0), axis=0,
                                       keepdims=True).astype(jnp.int32)
    gate_ref[...] = gates.T
    run_ref[...] = run_ref[...] + jnp.sum(onehot, axis=1, keepdims=True)
    cnt_ref[...] = run_ref[...]


def _router(x2d, w_router, router_bias, tm):
    T, D = x2d.shape
    E = w_router.shape[1]
    tok = lambda i: (0, i)
    fixed = lambda i: (0, 0)
    return pl.pallas_call(
        functools.partial(_router_kernel, tm=tm),
        out_shape=(jax.ShapeDtypeStruct((TOP_K, T), jnp.int32),
                   jax.ShapeDtypeStruct((T, LANES), F32),
                   jax.ShapeDtypeStruct((TOP_K, T), jnp.int32),
                   jax.ShapeDtypeStruct((E, 1), F32)),
        grid=(T // tm,),
        in_specs=[pl.BlockSpec((tm, D), lambda i: (i, 0)),
                  pl.BlockSpec((E, D), fixed),
                  pl.BlockSpec((E, 1), fixed)],
        out_specs=(pl.BlockSpec((TOP_K, tm), tok), pl.BlockSpec((tm, LANES), lambda i: (i, 0)),
                   pl.BlockSpec((TOP_K, tm), tok), pl.BlockSpec((E, 1), fixed)),
        scratch_shapes=[pltpu.VMEM((E, 1), F32)],
        compiler_params=_cparams("arbitrary"),
        name="router",
    )(x2d, w_router.T, router_bias.reshape(E, 1).astype(F32))


def _slots_kernel(idx_ref, rank_ref, start_ref, o_ref):
    E = start_ref.shape[0]
    tm = idx_ref.shape[1]
    eiota = lax.broadcasted_iota(jnp.int32, (E, tm), 0)
    start = start_ref[...]
    for k in range(TOP_K):
        base = jnp.sum(jnp.where(eiota == idx_ref[k:k + 1, :], start, 0.0), axis=0, keepdims=True)
        o_ref[k:k + 1, :] = base.astype(jnp.int32) + rank_ref[k:k + 1, :]


def _slots(idx, rank, seg_start, tm):
    T = idx.shape[1]
    E = seg_start.shape[0]
    tok = lambda i: (0, i)
    return pl.pallas_call(
        _slots_kernel,
        out_shape=jax.ShapeDtypeStruct((TOP_K, T), jnp.int32),
        grid=(T // tm,),
        in_specs=[pl.BlockSpec((TOP_K, tm), tok), pl.BlockSpec((TOP_K, tm), tok),
                  pl.BlockSpec((E, 1), lambda i: (0, 0))],
        out_specs=pl.BlockSpec((TOP_K, tm), tok),
        compiler_params=_cparams("parallel"),
        name="moe_slots",
    )(idx, rank, seg_start.astype(F32).reshape(E, 1))


SC_CORES = 2
SC_SUBCORES = 16
SC_CHUNK = 128


def _sc_rows_kernel(body, out_shape, n_chunks, width, dtype, name):
    mesh = plsc.VectorSubcoreMesh(core_axis_name="c", subcore_axis_name="s",
                                  num_cores=SC_CORES, num_subcores=SC_SUBCORES)
    return pl.kernel(
        body, mesh=mesh, out_type=jax.ShapeDtypeStruct(out_shape, dtype),
        scratch_types=[pltpu.VMEM((n_chunks, SC_CHUNK), jnp.int32),
                       pltpu.VMEM((SC_CHUNK, width), dtype),
                       pltpu.SemaphoreType.DMA],
        name=name)


def _sc_chunks(n):
    workers = SC_CORES * SC_SUBCORES
    chunks = n // (workers * SC_CHUNK)
    assert chunks * workers * SC_CHUNK == n
    return workers, chunks


def _sc_gather_rows(table, idx):
    workers, chunks = _sc_chunks(idx.shape[0])

    def body(table_hbm, idx_hbm, out_hbm, idx_v, rows_v, sem):
        wid = lax.axis_index("s") * SC_CORES + lax.axis_index("c")
        pltpu.sync_copy(idx_hbm.at[wid], idx_v)

        @pl.loop(0, chunks)
        def _(j):
            pltpu.async_copy(table_hbm.at[idx_v.at[j]], rows_v, sem).wait()
            pltpu.sync_copy(rows_v, out_hbm.at[pl.ds((wid * chunks + j) * SC_CHUNK, SC_CHUNK)])

    call = _sc_rows_kernel(body, (idx.shape[0], table.shape[1]), chunks, table.shape[1],
                           table.dtype, "moe_gather_sc")
    return call(table, idx.reshape(workers, chunks, SC_CHUNK))


def _sc_scatter_rows(src, idx, n_rows):
    n_src = src.shape[0]
    assert n_src % SC_CHUNK == 0
    workers, chunks = _sc_chunks(idx.shape[0])

    def body(src_hbm, idx_hbm, out_hbm, idx_v, rows_v, sem):
        wid = lax.axis_index("s") * SC_CORES + lax.axis_index("c")
        pltpu.sync_copy(idx_hbm.at[wid], idx_v)

        @pl.loop(0, chunks)
        def _(j):
            first = lax.rem((wid * chunks + j) * SC_CHUNK, n_src)
            pltpu.sync_copy(src_hbm.at[pl.ds(first, SC_CHUNK)], rows_v)
            pltpu.async_copy(rows_v, out_hbm.at[idx_v.at[j]], sem).wait()

    call = _sc_rows_kernel(body, (n_rows, src.shape[1]), chunks, src.shape[1], src.dtype,
                           "moe_scatter_sc")
    return call(src, idx.reshape(workers, chunks, SC_CHUNK))


def _silu(g):
    return g * jax.nn.sigmoid(g)


SCHED_FIRST, SCHED_SLOT, SCHED_NEXT, SCHED_VALID = 0, 1, 2, 3


def _expert_kernel(sched_ref, n_used_ref, x_ref, wg_hbm, wu_hbm, wd_hbm, o_ref,
                   wg_buf, wu_buf, wd_buf, sem, *, layer, first_expert_row):
    i = pl.program_id(0)
    slot = sched_ref[SCHED_SLOT, i]

    def copies(expert, slot):
        pairs = ((wg_hbm, wg_buf), (wu_hbm, wu_buf), (wd_hbm, wd_buf))
        return [pltpu.make_async_copy(src.at[layer, expert], dst.at[slot], sem.at[slot, n])
                for n, (src, dst) in enumerate(pairs)]

    @pl.when(i == 0)
    def _():
        for c in copies(sched_ref[first_expert_row, 0], 0):
            c.start()

    @pl.when(sched_ref[SCHED_FIRST, i] == 1)
    def _():
        for c in copies(0, slot):
            c.wait()

        @pl.when(sched_ref[SCHED_NEXT, i] >= 0)
        def _():
            for c in copies(sched_ref[SCHED_NEXT, i], 1 - slot):
                c.start()

    @pl.when(i < n_used_ref[0])
    def _():
        words = x_ref[...]
        row = lax.broadcasted_iota(jnp.int32, words.shape, 0)
        words = jnp.where(row < sched_ref[SCHED_VALID, i], words, 0)
        x = jnp.concatenate(_unpack_halves(words), axis=1).astype(BF16)
        g = jnp.dot(x, wg_buf[slot].astype(BF16), preferred_element_type=F32)
        u = jnp.dot(x, wu_buf[slot].astype(BF16), preferred_element_type=F32)
        a = (_silu(g) * u).astype(BF16)
        o_ref[...] = _pack_halves(jnp.dot(a, wd_buf[slot].astype(BF16), preferred_element_type=F32))

    @pl.when(i >= n_used_ref[0])
    def _():
        o_ref[...] = jnp.zeros_like(o_ref)


def _expert_schedule(seg_start, seg_end, counts, n_blocks):
    E = seg_end.shape[0]
    first_row = jnp.arange(n_blocks, dtype=jnp.int32) * EXPERT_ROWS
    blk_e = jnp.minimum(jnp.sum(seg_end[None, :] <= first_row[:, None], axis=1), E - 1)
    used = first_row < seg_end[-1]
    first = used & (first_row == seg_start[blk_e])
    ordinal = jnp.cumsum(first) - 1
    ids = jnp.where(seg_end > seg_start, jnp.arange(E), E)
    later = lax.cummin(ids, reverse=True)
    nxt = jnp.concatenate([later[1:], jnp.full((1,), E, later.dtype)])[blk_e]
    nxt = jnp.where(nxt < E, nxt, -1)
    valid = jnp.clip((seg_start + counts)[blk_e] - first_row, 0, EXPERT_ROWS)
    return jnp.stack([first, ordinal % 2, nxt, valid, blk_e]).astype(jnp.int32)


def _expert_mlp(xs, sched, n_used, w_gate, w_up, w_down, layer):
    n_rows, half = xs.shape
    D = 2 * half
    F = w_gate.shape[-1]
    tm = EXPERT_ROWS
    hbm = pl.BlockSpec(memory_space=pl.ANY)
    return pl.pallas_call(
        functools.partial(_expert_kernel, layer=layer, first_expert_row=sched.shape[0] - 1),
        out_shape=jax.ShapeDtypeStruct((n_rows, half), jnp.int32),
        grid_spec=pltpu.PrefetchScalarGridSpec(
            num_scalar_prefetch=2,
            grid=(n_rows // tm,),
            in_specs=[pl.BlockSpec((tm, half), lambda i, sc, nu: (jnp.minimum(i, nu[0] - 1), 0)),
                      hbm, hbm, hbm],
            out_specs=pl.BlockSpec((tm, half), lambda i, sc, nu: (i, 0)),
            scratch_shapes=[pltpu.VMEM((2, D, F), F32), pltpu.VMEM((2, D, F), F32),
                            pltpu.VMEM((2, F, D), F32), pltpu.SemaphoreType.DMA((2, 3))]),
        compiler_params=_cparams("arbitrary"),
        name="moe_experts",
    )(sched, n_used, xs, w_gate, w_up, w_down)


def _combine_kernel(x_ref, gate_ref, rows_ref, wsg_ref, wsu_ref, wsd_ref, g_ref, b_ref, o_ref,
                    *, alpha):
    x = x_ref[...]
    xb = x.astype(BF16)
    g = jnp.dot(xb, wsg_ref[...], preferred_element_type=F32)
    u = jnp.dot(xb, wsu_ref[...], preferred_element_type=F32)
    y = jnp.dot((_silu(g) * u).astype(BF16), wsd_ref[...], preferred_element_type=F32)
    gates = gate_ref[...]
    half = y.shape[1] // 2
    left, right = y[:, :half], y[:, half:]
    for k in range(TOP_K):
        lo, hi = _unpack_halves(rows_ref[k])
        left = left + lo * gates[:, k:k + 1]
        right = right + hi * gates[:, k:k + 1]
    y = jnp.concatenate([left, right], axis=1)
    o_ref[...] = _layer_norm(alpha * x + y, g_ref[...], b_ref[...])


def _combine(x2d, ys, dest, gates_tok, ws_gate, ws_up, ws_down, g, b, alpha, tt):
    T, D = x2d.shape
    F = ws_gate.shape[1]
    row = lambda i: (i, 0)
    fixed = lambda i: (0, 0)
    rows = _sc_gather_rows(ys, dest.reshape(-1)).reshape(TOP_K, T, D // 2)
    return pl.pallas_call(
        functools.partial(_combine_kernel, alpha=alpha),
        out_shape=jax.ShapeDtypeStruct((T, D), F32),
        grid=(T // tt,),
        in_specs=[pl.BlockSpec((tt, D), row),
                  pl.BlockSpec((tt, LANES), row),
                  pl.BlockSpec((TOP_K, tt, D // 2), lambda i: (0, i, 0)),
                  pl.BlockSpec((D, F), fixed), pl.BlockSpec((D, F), fixed),
                  pl.BlockSpec((F, D), fixed),
                  pl.BlockSpec((1, D), fixed), pl.BlockSpec((1, D), fixed)],
        out_specs=pl.BlockSpec((tt, D), row),
        compiler_params=_cparams("parallel"),
        name="moe_combine",
    )(x2d, gates_tok, rows, ws_gate.astype(BF16), ws_up.astype(BF16), ws_down.astype(BF16),
      g.reshape(1, D), b.reshape(1, D))


def _pick(n, pref):
    t = min(n, pref)
    while n % t:
        t //= 2
    return t


def _moe_layer(x2d, packed, w_router, router_bias, w_gate, w_up, w_down, layer, ws_gate, ws_up,
               ws_down, g, b, alpha):
    T, D = x2d.shape
    E = w_router.shape[1]
    idx, gates, rank, counts = _router(x2d, w_router, router_bias, _pick(T, 256))
    counts = counts[:, 0].astype(jnp.int32)
    padded = (counts + EXPERT_ROWS - 1) // EXPERT_ROWS * EXPERT_ROWS
    seg_end = jnp.cumsum(padded)
    seg_start = seg_end - padded
    n_blocks = (T * TOP_K + E * (EXPERT_ROWS - 1)) // EXPERT_ROWS
    dest = _slots(idx, rank, seg_start, _pick(T, 1024))
    sched = _expert_schedule(seg_start, seg_end, counts, n_blocks)
    n_used = (seg_end[-1:] // EXPERT_ROWS).astype(jnp.int32)
    xs = _sc_scatter_rows(packed, dest.reshape(-1), n_blocks * EXPERT_ROWS)
    ys = _expert_mlp(xs, sched, n_used, w_gate, w_up, w_down, layer)
    return _combine(x2d, ys, dest, gates, ws_gate, ws_up, ws_down, g, b, alpha, _pick(T, 256))


def _mixer_layer(x2d, i, B, S, p, alpha):
    T, D = x2d.shape
    tm = _pick(T, 512)
    tn = _pick(D, 1024)
    j = i // 2
    w_in = p["moba_w_in"][j] if i % 2 == 0 else p["fox_w_in"][j]
    w_qv_t = jnp.concatenate([w_in[:, :D].T * HEAD_DIM ** -0.5, w_in[:, 2 * D:3 * D].T]).astype(BF16)
    qv_t = _project_t(x2d, w_qv_t, tm, tn)
    k = _project(x2d, w_in[:, D:2 * D].astype(BF16), tm, tn)
    if i % 2 == 0:
        k_aug, k_mean = _moba_kprep(k.reshape(B, S, D))
        attn_t = _moba_attention(qv_t, k_aug, k_mean, p["rel_bias"], B, S)
        w_out = p["moba_w_out"][j]
    else:
        c_tok = _fox_gates(x2d.reshape(B, S, D), w_in[:, 3 * D:], p["fox_b_f"][j], _pick(S, 512))
        k_aug = _fox_kprep(k, c_tok.reshape(T, LANES), tm).reshape(HEADS_PER_TILE, B, S, D)
        attn_t = _fox_attention(qv_t, k_aug, B, S, _pick(S, 256))
        w_out = p["fox_w_out"][j]
    return _outproj_ln(attn_t, w_out.astype(BF16), x2d, p["ln1_g"][i], p["ln1_b"][i], alpha, tm)


def kernel(x, rel_bias, moba_w_in, moba_w_out, fox_w_in, fox_b_f, fox_w_out, ln1_g, ln1_b, ln2_g,
           ln2_b, w_router, router_bias, w_gate, w_up, w_down, ws_gate, ws_up, ws_down):
    B, S, D = x.shape
    depth = ln1_g.shape[0]
    alpha = (2 * depth) ** 0.25
    p = dict(rel_bias=rel_bias, moba_w_in=moba_w_in, moba_w_out=moba_w_out, fox_w_in=fox_w_in,
             fox_b_f=fox_b_f, fox_w_out=fox_w_out, ln1_g=ln1_g, ln1_b=ln1_b)
    x2d = x.reshape(B * S, D)
    for i in range(depth):
        x2d, packed = _mixer_layer(x2d, i, B, S, p, alpha)
        x2d = _moe_layer(x2d, packed, w_router[i], router_bias[i], w_gate, w_up, w_down, i,
                         ws_gate[i], ws_up[i], ws_down[i], ln2_g[i], ln2_b[i], alpha)
    return x2d.reshape(B, S, D)
```

```python
import functools
import math

import jax
import jax.numpy as jnp
from jax import lax
from jax.experimental import pallas as pl
from jax.experimental.pallas import tpu as pltpu
from jax.experimental.pallas import tpu_sc as plsc

F32 = jnp.float32
BF16 = jnp.bfloat16
HIGHEST = lax.Precision.HIGHEST

HEAD_DIM = 64
MOBA_BLOCK = 256
MOBA_TOPK = 3
REL_MAX_DIST = 128
TOP_K = 8
N_GROUPS = 8
TOPK_GROUPS = 4
ROUTED_SCALE = 2.5
LN_EPS = 1e-5
NEG = -1e30

LANES = 128
HEADS_PER_TILE = LANES // HEAD_DIM
EXPERT_ROWS = 512
VMEM_LIMIT = 48 * 1024 * 1024


def _cparams(*sem):
    return pltpu.CompilerParams(dimension_semantics=sem, vmem_limit_bytes=VMEM_LIMIT)


def _nt_dot(a, b, **kw):
    return lax.dot_general(a, b, (((1,), (1,)), ((), ())), preferred_element_type=F32, **kw)


def _proj_kernel(x_ref, w_ref, o_ref):
    o_ref[...] = jnp.dot(x_ref[...].astype(BF16), w_ref[...],
                         preferred_element_type=F32).astype(o_ref.dtype)


def _project(x2d, w, tm, tn):
    T, K = x2d.shape
    N = w.shape[1]
    return pl.pallas_call(
        _proj_kernel,
        out_shape=jax.ShapeDtypeStruct((T, N), BF16),
        grid=(T // tm, N // tn),
        in_specs=[pl.BlockSpec((tm, K), lambda i, j: (i, 0)),
                  pl.BlockSpec((K, tn), lambda i, j: (0, j))],
        out_specs=pl.BlockSpec((tm, tn), lambda i, j: (i, j)),
        compiler_params=_cparams("parallel", "arbitrary"),
        name="k_proj",
    )(x2d, w)


def _proj_t_kernel(x_ref, wt_ref, o_ref):
    o_ref[...] = _nt_dot(wt_ref[...], x_ref[...].astype(BF16)).astype(o_ref.dtype)


def _project_t(x2d, wt, tm, tn):
    T, K = x2d.shape
    N = wt.shape[0]
    return pl.pallas_call(
        _proj_t_kernel,
        out_shape=jax.ShapeDtypeStruct((N, T), BF16),
        grid=(T // tm, N // tn),
        in_specs=[pl.BlockSpec((tm, K), lambda i, j: (i, 0)),
                  pl.BlockSpec((tn, K), lambda i, j: (j, 0))],
        out_specs=pl.BlockSpec((tn, tm), lambda i, j: (j, i)),
        compiler_params=_cparams("parallel", "arbitrary"),
        name="qv_proj_t",
    )(x2d, wt)


def _layer_norm(r, g, b):
    mu = jnp.mean(r, axis=-1, keepdims=True)
    d = r - mu
    var = jnp.mean(d * d, axis=-1, keepdims=True)
    return d * lax.rsqrt(var + LN_EPS) * g + b


def _pack_halves(y):
    n = y.shape[1] // 2
    bits = lambda v: lax.bitcast_convert_type(v.astype(BF16).astype(F32), jnp.int32)
    left = bits(y[:, :n])
    return lax.shift_right_logical(left, jnp.full_like(left, 16)) | bits(y[:, n:])


def _unpack_halves(p):
    return (lax.bitcast_convert_type(p << 16, F32),
            lax.bitcast_convert_type(p & jnp.int32(-65536), F32))


def _outproj_ln_kernel(at_ref, w_ref, x_ref, g_ref, b_ref, o_ref, packed_ref, *, alpha):
    y = lax.dot_general(at_ref[...], w_ref[...], (((0,), (0,)), ((), ())),
                        preferred_element_type=F32)
    out = _layer_norm(alpha * x_ref[...] + y, g_ref[...], b_ref[...])
    o_ref[...] = out
    packed_ref[...] = _pack_halves(out)


def _outproj_ln(attn_t, w_out, x2d, g, b, alpha, tm):
    T, D = x2d.shape
    row = lambda i: (i, 0)
    fixed = lambda i: (0, 0)
    return pl.pallas_call(
        functools.partial(_outproj_ln_kernel, alpha=alpha),
        out_shape=(jax.ShapeDtypeStruct((T, D), F32), jax.ShapeDtypeStruct((T, D // 2), jnp.int32)),
        grid=(T // tm,),
        in_specs=[pl.BlockSpec((D, tm), lambda i: (0, i)), pl.BlockSpec((D, D), fixed),
                  pl.BlockSpec((tm, D), row), pl.BlockSpec((1, D), fixed),
                  pl.BlockSpec((1, D), fixed)],
        out_specs=(pl.BlockSpec((tm, D), row), pl.BlockSpec((tm, D // 2), row)),
        compiler_params=_cparams("parallel"),
        name="outproj_ln",
    )(attn_t, w_out, x2d, g.reshape(1, D), b.reshape(1, D))


def _spare(a, n=0):
    return ((a + 1) % HEADS_PER_TILE) * HEAD_DIM + n


def _scores(chains):
    return [[jnp.dot(keys(), query(), preferred_element_type=F32) for keys, query, _ in blocks]
            for blocks in chains]


def _absorb(carries, scores, chains, masks=None, adds=None):
    probs = []
    for a, ((m, _), parts) in enumerate(zip(carries, scores)):
        if adds is not None:
            parts = [s if add is None else s + add() for add, s in zip(adds[a], parts)]
        if masks is not None:
            parts = [s if keep is None else jnp.where(keep, s, NEG)
                     for keep, s in zip(masks, parts)]
        tops = [jnp.max(s, axis=0, keepdims=True) for s in parts]
        m_new = functools.reduce(jnp.maximum, tops, m)
        probs.append((m_new, jnp.exp(m - m_new), [jnp.exp(s - m_new).astype(BF16) for s in parts]))
    out = []
    for (_, acc), (m_new, decay, ps), blocks in zip(carries, probs, chains):
        acc = decay * acc
        for (_, _, values), p in zip(blocks, ps):
            acc = acc + jnp.dot(values(), p, preferred_element_type=F32)
        out.append((m_new, acc))
    return tuple(out)


def _stash(chains, ref):
    for a, parts in enumerate(_scores(chains)):
        for b, s in enumerate(parts):
            ref[a, b] = s


def _fetch(chains, ref):
    return [[ref[a, b] for b in range(len(blocks))] for a, blocks in enumerate(chains)]


def _attend_masked_last(carries, n, group, stage_refs, masks, adds=None):
    first, second = stage_refs
    _stash(group(0), first)
    trips = n // 2

    def body(t, carries):
        g = 2 * t
        _stash(group(g + 1), second)
        carries = _absorb(carries, _fetch(group(g), first), group(g))
        _stash(group(g + 2), first)
        return _absorb(carries, _fetch(group(g + 1), second), group(g + 1))

    carries = lax.fori_loop(0, trips, body, carries)

    def odd_tail(carries):
        _stash(group(n), second)
        carries = _absorb(carries, _fetch(group(n - 1), first), group(n - 1))
        return _absorb(carries, _fetch(group(n), second), group(n), masks, adds)

    def even_tail(carries):
        return _absorb(carries, _fetch(group(n), first), group(n), masks, adds)

    return lax.cond(n % 2 == 1, odd_tail, even_tail, carries)


GROUP_BLOCKS = 2


def _stage_scratch(tq):
    return [pltpu.VMEM((HEADS_PER_TILE, GROUP_BLOCKS, tq, tq), F32) for _ in range(2)]


def _scores_init(tq):
    return jnp.full((1, tq), -jnp.inf, F32), jnp.zeros((LANES, tq), F32)


def _finish(carries, o_ref):
    row = lax.broadcasted_iota(jnp.int32, carries[0][1].shape, 0)
    outs = []
    for a, (_, acc) in enumerate(carries):
        s = _spare(a)
        outs.append(acc / acc[s:s + 1, :])
    o_ref[...] = jnp.where(row < HEAD_DIM, outs[0], outs[1]).astype(o_ref.dtype)


def _fill_v_aug(vt_ref, vaug_ref, tk):
    n_tiles = vt_ref.shape[1] // tk
    row = lax.broadcasted_iota(jnp.int32, (LANES, tk), 0)
    for n in range(n_tiles):
        v = vt_ref[:, n * tk:(n + 1) * tk]
        for a in range(HEADS_PER_TILE):
            vaug_ref[a, n] = jnp.where(row // HEAD_DIM == a, v, jnp.ones_like(v))


FOX_SPLIT = 3


def _fox_kernel(qt_ref, k_ref, vt_ref, o_ref, vaug_ref, stage0_ref, stage1_ref, *, tq):
    i = pl.program_id(2)

    @pl.when(i == 0)
    def _():
        _fill_v_aug(vt_ref, vaug_ref, tq)

    row = lax.broadcasted_iota(jnp.int32, (LANES, tq), 0)
    qt = qt_ref[...]
    queries = []
    for a in range(HEADS_PER_TILE):
        offs = (row >= _spare(a)) & (row < _spare(a, FOX_SPLIT))
        queries.append(jnp.where(row // HEAD_DIM == a, qt,
                                 jnp.where(offs, -1.0, 0.0).astype(BF16)))
    last = vaug_ref.shape[1] - 1

    def block(a, j):
        j = jnp.minimum(j, last)
        start = pl.multiple_of(j * tq, tq)
        return lambda: k_ref[a, 0, pl.ds(start, tq), :], lambda: queries[a], lambda: vaug_ref[a, j]

    n = i // 2
    key = lax.broadcasted_iota(jnp.int32, (tq, tq), 0)
    qry = lax.broadcasted_iota(jnp.int32, (tq, tq), 1)
    masks = [key + (2 * n + b - i) * tq <= qry for b in range(GROUP_BLOCKS)]
    heads = range(HEADS_PER_TILE)
    carries = _attend_masked_last(
        tuple(_scores_init(tq) for _ in heads), n,
        lambda j: [[block(a, 2 * j), block(a, 2 * j + 1)] for a in heads],
        (stage0_ref, stage1_ref), masks)
    _finish(carries, o_ref)


def _fox_attention(qv_t, k_aug, B, S, tq):
    D = k_aug.shape[-1]
    n_tiles = D // LANES
    nq = S // tq
    return pl.pallas_call(
        functools.partial(_fox_kernel, tq=tq),
        out_shape=jax.ShapeDtypeStruct((D, B * S), BF16),
        grid=(B, n_tiles, nq),
        in_specs=[pl.BlockSpec((LANES, tq), lambda b, h, i: (h, b * nq + i)),
                  pl.BlockSpec((HEADS_PER_TILE, 1, S, LANES), lambda b, h, i: (0, b, 0, h)),
                  pl.BlockSpec((LANES, S), lambda b, h, i: (n_tiles + h, b))],
        out_specs=pl.BlockSpec((LANES, tq), lambda b, h, i: (h, b * nq + i)),
        scratch_shapes=[pltpu.VMEM((HEADS_PER_TILE, nq, LANES, tq), BF16)] + _stage_scratch(tq),
        compiler_params=_cparams("parallel", "parallel", "arbitrary"),
        name="fox_attention",
    )(qv_t, k_aug, qv_t)


def _log_sigmoid(z):
    return jnp.minimum(z, 0.0) - jnp.log1p(jnp.exp(-jnp.abs(z)))


def _fox_gates_kernel(x_ref, w_ref, b_ref, o_ref, carry_ref, *, ts):
    @pl.when(pl.program_id(1) == 0)
    def _():
        carry_ref[...] = jnp.zeros_like(carry_ref)

    z = jnp.dot(x_ref[0], w_ref[...], preferred_element_type=F32, precision=HIGHEST)
    lf = _log_sigmoid(z + b_ref[...])
    r = lax.broadcasted_iota(jnp.int32, (ts, ts), 0)
    c = lax.broadcasted_iota(jnp.int32, (ts, ts), 1)
    tri = (c <= r).astype(F32)
    cum = jnp.dot(tri, lf, preferred_element_type=F32, precision=HIGHEST) + carry_ref[...]
    carry_ref[...] = cum[ts - 1:ts, :]
    o_ref[0] = cum


def _fox_gates(x3d, w_f, b_f, ts):
    B, S, D = x3d.shape
    H = w_f.shape[1]
    w_pad = jnp.zeros((D, LANES), F32).at[:, :H].set(w_f)
    b_pad = jnp.zeros((1, LANES), F32).at[0, :H].set(b_f)
    return pl.pallas_call(
        functools.partial(_fox_gates_kernel, ts=ts),
        out_shape=jax.ShapeDtypeStruct((B, S, LANES), F32),
        grid=(B, S // ts),
        in_specs=[pl.BlockSpec((1, ts, D), lambda b, s: (b, s, 0)),
                  pl.BlockSpec((D, LANES), lambda b, s: (0, 0)),
                  pl.BlockSpec((1, LANES), lambda b, s: (0, 0))],
        out_specs=pl.BlockSpec((1, ts, LANES), lambda b, s: (b, s, 0)),
        scratch_shapes=[pltpu.VMEM((1, LANES), F32)],
        compiler_params=_cparams("parallel", "arbitrary"),
        name="fox_gates",
    )(x3d, w_pad, b_pad)


def _fox_kprep_kernel(k_ref, c_ref, o_ref):
    tm, D = k_ref.shape
    lane = lax.broadcasted_iota(jnp.int32, (tm, LANES), 1)
    for t in range(D // LANES):
        k = k_ref[:, t * LANES:(t + 1) * LANES].astype(F32)
        for a in range(HEADS_PER_TILE):
            h = t * HEADS_PER_TILE + a
            rest = c_ref[:, h:h + 1]
            aug = jnp.zeros((tm, LANES), F32)
            for n in range(FOX_SPLIT):
                piece = rest.astype(BF16).astype(F32)
                rest = rest - piece
                aug = jnp.where(lane == _spare(a, n), piece, aug)
            o_ref[a, :, t * LANES:(t + 1) * LANES] = jnp.where(lane // HEAD_DIM == a, k,
                                                               aug).astype(BF16)


def _fox_kprep(k2d, c_tok, tm):
    T, D = k2d.shape
    return pl.pallas_call(
        _fox_kprep_kernel,
        out_shape=jax.ShapeDtypeStruct((HEADS_PER_TILE, T, D), BF16),
        grid=(T // tm,),
        in_specs=[pl.BlockSpec((tm, D), lambda i: (i, 0)),
                  pl.BlockSpec((tm, LANES), lambda i: (i, 0))],
        out_specs=pl.BlockSpec((HEADS_PER_TILE, tm, D), lambda i: (0, i, 0)),
        compiler_params=_cparams("parallel"),
        name="fox_kprep",
    )(k2d, c_tok)


MOBA_SLAB = 16


def _moba_kernel(bfar_ref, qt_ref, k_ref, vt_ref, kmean_ref, bown_ref, badj_ref, o_ref,
                 vaug_ref, sel_ref, stage0_ref, stage1_ref, *, blk):
    t = pl.program_id(1)
    i = pl.program_id(2)
    nb = kmean_ref.shape[1]

    @pl.when(i == 0)
    def _():
        _fill_v_aug(vt_ref, vaug_ref, blk)

    row = lax.broadcasted_iota(jnp.int32, (LANES, blk), 0)
    brow = lax.broadcasted_iota(jnp.int32, (nb, blk), 0)
    qt = qt_ref[...]
    zero = jnp.zeros_like(qt)
    base, slabs = [], []
    for a in range(HEADS_PER_TILE):
        qa = jnp.where(row // HEAD_DIM == a, qt, zero)
        base.append(qa)
        bscore = jnp.dot(kmean_ref[0], qa.astype(F32), preferred_element_type=F32,
                         precision=HIGHEST)
        cand = jnp.where(brow < i, bscore, NEG)
        sel = jnp.zeros((nb, blk), F32)
        for _ in range(MOBA_TOPK):
            mx = jnp.max(cand, axis=0, keepdims=True)
            first = jnp.min(jnp.where(cand == mx, brow, nb), axis=0, keepdims=True)
            hit = brow == first
            sel = jnp.where(hit, 1.0, sel)
            cand = jnp.where(hit, -jnp.inf, cand)
        sel_ref[a] = jnp.where(brow < i, sel, 0.0)
        slabs.append(_spare(a) // MOBA_SLAB * MOBA_SLAB)

    srow = lax.broadcasted_iota(jnp.int32, (MOBA_SLAB, blk), 0)

    def block(a, j):
        h = t * HEADS_PER_TILE + a
        jc = jnp.maximum(j, 0)
        start = pl.multiple_of(jc * blk, blk)

        def query():
            s0 = slabs[a]
            chosen = ((sel_ref[a, pl.ds(jc, 1), :] > 0.0) & (j >= 0)) | (j == i)
            slab = jnp.where(srow == _spare(a) - s0, jnp.where(chosen, bfar_ref[0, h], NEG),
                             jnp.where(srow == _spare(a, 1) - s0, bfar_ref[1, h], 0.0)).astype(BF16)
            head = [base[a][:s0]] if s0 else []
            tail = [base[a][s0 + MOBA_SLAB:]] if s0 + MOBA_SLAB < LANES else []
            return jnp.concatenate(head + [slab] + tail, axis=0)

        return lambda: k_ref[a, 0, pl.ds(start, blk), :], query, lambda: vaug_ref[a, jc]

    off = (i + 1) % 2
    n = (i + off) // 2
    key = lax.broadcasted_iota(jnp.int32, (blk, blk), 0)
    qry = lax.broadcasted_iota(jnp.int32, (blk, blk), 1)
    heads = range(HEADS_PER_TILE)
    carries = _attend_masked_last(
        tuple(_scores_init(blk) for _ in heads), n,
        lambda g: [[block(a, 2 * g - off), block(a, 2 * g + 1 - off)] for a in heads],
        (stage0_ref, stage1_ref), [None, key <= qry],
        [[lambda a=a: badj_ref[a], lambda a=a: bown_ref[a]] for a in heads])
    _finish(carries, o_ref)


def _rel_bucket(dist, n_buckets):
    n = jnp.maximum(dist, 0)
    max_exact = n_buckets // 2
    nf = jnp.maximum(n, 1).astype(F32)
    large = max_exact + (jnp.log(nf / max_exact) / math.log(REL_MAX_DIST / max_exact)
                         * (n_buckets - max_exact)).astype(jnp.int32)
    large = jnp.minimum(large, n_buckets - 1)
    return jnp.where(n < max_exact, n, large)


def _moba_attention(qv_t, k_aug, k_mean, rel_bias, B, S):
    D = k_aug.shape[-1]
    n_tiles = D // LANES
    blk = MOBA_BLOCK
    nb = S // blk
    hp = HEADS_PER_TILE
    r = jnp.arange(blk)
    delta = r[None, :] - r[:, None]
    bias_t = rel_bias.T.astype(F32)
    n_buckets = rel_bias.shape[0]
    def table(dist):
        onehot = jax.nn.one_hot(_rel_bucket(dist, n_buckets), n_buckets, dtype=F32)
        return jnp.einsum('crn,hn->hcr', onehot, bias_t, precision=HIGHEST)

    b_far = bias_t[:, n_buckets - 1]
    b_own = table(delta) - b_far[:, None, None]
    b_adj = table(delta + blk) - b_far[:, None, None]
    far_hi = b_far.astype(BF16).astype(F32)
    far_lo = (b_far - far_hi).astype(BF16).astype(F32)
    b_far2 = jnp.stack([far_hi, far_lo])
    return pl.pallas_call(
        functools.partial(_moba_kernel, blk=blk),
        out_shape=jax.ShapeDtypeStruct((D, B * S), BF16),
        grid_spec=pltpu.PrefetchScalarGridSpec(
            num_scalar_prefetch=1,
            grid=(B, n_tiles, nb),
            in_specs=[pl.BlockSpec((LANES, blk), lambda b, h, i, f: (h, b * nb + i)),
                      pl.BlockSpec((hp, 1, S, LANES), lambda b, h, i, f: (0, b, 0, h)),
                      pl.BlockSpec((LANES, S), lambda b, h, i, f: (n_tiles + h, b)),
                      pl.BlockSpec((1, nb, LANES), lambda b, h, i, f: (b, 0, h)),
                      pl.BlockSpec((hp, blk, blk), lambda b, h, i, f: (h, 0, 0)),
                      pl.BlockSpec((hp, blk, blk), lambda b, h, i, f: (h, 0, 0))],
            out_specs=pl.BlockSpec((LANES, blk), lambda b, h, i, f: (h, b * nb + i)),
            scratch_shapes=[pltpu.VMEM((hp, nb, LANES, blk), BF16),
                            pltpu.VMEM((hp, nb, blk), F32)] + _stage_scratch(blk)),
        compiler_params=_cparams("parallel", "parallel", "arbitrary"),
        name="moba_attention",
    )(b_far2, qv_t, k_aug, qv_t, k_mean, b_own, b_adj)


def _moba_kprep_kernel(k_ref, o_ref, mean_ref):
    k = k_ref[0]
    col = lax.broadcasted_iota(jnp.int32, k.shape, 1)
    lane = col % LANES
    for a in range(HEADS_PER_TILE):
        ones = (lane == _spare(a)) | (lane == _spare(a, 1))
        o_ref[a, 0] = jnp.where((col // HEAD_DIM) % HEADS_PER_TILE == a, k,
                                jnp.where(ones, 1.0, 0.0).astype(BF16))
    mean_ref[0, 0] = jnp.mean(k.astype(F32), axis=0, keepdims=True)


def _moba_kprep(k3d):
    B, S, D = k3d.shape
    blk = MOBA_BLOCK
    nb = S // blk
    k_aug, k_mean = pl.pallas_call(
        _moba_kprep_kernel,
        out_shape=(jax.ShapeDtypeStruct((HEADS_PER_TILE, B, S, D), BF16),
                   jax.ShapeDtypeStruct((B, nb, 1, D), F32)),
        grid=(B, nb),
        in_specs=[pl.BlockSpec((1, blk, D), lambda b, n: (b, n, 0))],
        out_specs=(pl.BlockSpec((HEADS_PER_TILE, 1, blk, D), lambda b, n: (0, b, n, 0)),
                   pl.BlockSpec((1, 1, 1, D), lambda b, n: (b, n, 0, 0))),
        compiler_params=_cparams("parallel", "parallel"),
        name="moba_kprep",
    )(k3d)
    return k_aug, k_mean.reshape(B, nb, D)


def _router_kernel(x_ref, wt_ref, bias_ref, idx_ref, gate_ref, rank_ref, cnt_ref, run_ref, *, tm):
    @pl.when(pl.program_id(0) == 0)
    def _():
        run_ref[...] = jnp.zeros_like(run_ref)

    E = wt_ref.shape[0]
    gsz = E // N_GROUPS
    logits = _nt_dot(wt_ref[...], x_ref[...], precision=HIGHEST)
    s = jax.nn.sigmoid(logits)
    sb = s + bias_ref[...]
    neg_inf = -jnp.inf

    giota = lax.broadcasted_iota(jnp.int32, (gsz, tm), 0)
    gscore = []
    for g in range(N_GROUPS):
        blk = sb[g * gsz:(g + 1) * gsz, :]
        m1 = jnp.max(blk, axis=0, keepdims=True)
        i1 = jnp.min(jnp.where(blk == m1, giota, gsz), axis=0, keepdims=True)
        m2 = jnp.max(jnp.where(giota == i1, neg_inf, blk), axis=0, keepdims=True)
        gscore.append(m1 + m2)

    gsel = [jnp.zeros((1, tm), jnp.bool_) for _ in range(N_GROUPS)]
    for _ in range(TOPK_GROUPS):
        mx = functools.reduce(jnp.maximum, gscore)
        found = jnp.zeros((1, tm), jnp.bool_)
        for g in range(N_GROUPS):
            hit = (gscore[g] == mx) & jnp.logical_not(found)
            gsel[g] = gsel[g] | hit
            found = found | hit
            gscore[g] = jnp.where(hit, neg_inf, gscore[g])
    emask = jnp.concatenate([jnp.broadcast_to(gsel[g], (gsz, tm)) for g in range(N_GROUPS)], axis=0)
    cand = jnp.where(emask, sb, NEG)

    eiota = lax.broadcasted_iota(jnp.int32, (E, tm), 0)
    hits, idxs, ws = [], [], []
    for _ in range(TOP_K):
        mx = jnp.max(cand, axis=0, keepdims=True)
        first = jnp.min(jnp.where(cand == mx, eiota, E), axis=0, keepdims=True)
        hit = eiota == first
        hits.append(hit)
        idxs.append(first)
        ws.append(jnp.sum(jnp.where(hit, s, 0.0), axis=0, keepdims=True))
        cand = jnp.where(hit, neg_inf, cand)
    wsum = functools.reduce(jnp.add, ws)

    chosen = functools.reduce(jnp.logical_or, hits)
    onehot = jnp.where(chosen, 1.0, 0.0)
    tr = lax.broadcasted_iota(jnp.int32, (tm, tm), 0)
    tc = lax.broadcasted_iota(jnp.int32, (tm, tm), 1)
    before = (tr < tc).astype(BF16)
    prior = jnp.dot(onehot.astype(BF16), before, preferred_element_type=F32) + run_ref[...]
    grow = lax.broadcasted_iota(jnp.int32, (LANES, tm), 0)
    gates = jnp.zeros((LANES, tm), F32)
    for k in range(TOP_K):
        idx_ref[k:k + 1, :] = idxs[k]
        gates = jnp.where(grow == k, ws[k] / wsum * ROUTED_SCALE, gates)
        rank_ref[k:k + 1, :] = jnp.sum(jnp.where(hits[k], prior, 0.0), axis=0,
                                       keepdims=True).astype(jnp.int32)
    gate_ref[...] = gates.T
    run_ref[...] = run_ref[...] + jnp.sum(onehot, axis=1, keepdims=True)
    cnt_ref[...] = run_ref[...]


def _router(x2d, w_router, router_bias, tm):
    T, D = x2d.shape
    E = w_router.shape[1]
    tok = lambda i: (0, i)
    fixed = lambda i: (0, 0)
    return pl.pallas_call(
        functools.partial(_router_kernel, tm=tm),
        out_shape=(jax.ShapeDtypeStruct((TOP_K, T), jnp.int32),
                   jax.ShapeDtypeStruct((T, LANES), F32),
                   jax.ShapeDtypeStruct((TOP_K, T), jnp.int32),
                   jax.ShapeDtypeStruct((E, 1), F32)),
        grid=(T // tm,),
        in_specs=[pl.BlockSpec((tm, D), lambda i: (i, 0)),
                  pl.BlockSpec((E, D), fixed),
                  pl.BlockSpec((E, 1), fixed)],
        out_specs=(pl.BlockSpec((TOP_K, tm), tok), pl.BlockSpec((tm, LANES), lambda i: (i, 0)),
                   pl.BlockSpec((TOP_K, tm), tok), pl.BlockSpec((E, 1), fixed)),
        scratch_shapes=[pltpu.VMEM((E, 1), F32)],
        compiler_params=_cparams("arbitrary"),
        name="router",
    )(x2d, w_router.T, router_bias.reshape(E, 1).astype(F32))


def _slots_kernel(idx_ref, rank_ref, start_ref, o_ref):
    E = start_ref.shape[0]
    tm = idx_ref.shape[1]
    eiota = lax.broadcasted_iota(jnp.int32, (E, tm), 0)
    start = start_ref[...]
    for k in range(TOP_K):
        base = jnp.sum(jnp.where(eiota == idx_ref[k:k + 1, :], start, 0.0), axis=0, keepdims=True)
        o_ref[k:k + 1, :] = base.astype(jnp.int32) + rank_ref[k:k + 1, :]


def _slots(idx, rank, seg_start, tm):
    T = idx.shape[1]
    E = seg_start.shape[0]
    tok = lambda i: (0, i)
    return pl.pallas_call(
        _slots_kernel,
        out_shape=jax.ShapeDtypeStruct((TOP_K, T), jnp.int32),
        grid=(T // tm,),
        in_specs=[pl.BlockSpec((TOP_K, tm), tok), pl.BlockSpec((TOP_K, tm), tok),
                  pl.BlockSpec((E, 1), lambda i: (0, 0))],
        out_specs=pl.BlockSpec((TOP_K, tm), tok),
        compiler_params=_cparams("parallel"),
        name="moe_slots",
    )(idx, rank, seg_start.astype(F32).reshape(E, 1))


SC_CORES = 2
SC_SUBCORES = 16
SC_CHUNK = 128


def _sc_rows_kernel(body, out_shape, n_chunks, width, dtype, name):
    mesh = plsc.VectorSubcoreMesh(core_axis_name="c", subcore_axis_name="s",
                                  num_cores=SC_CORES, num_subcores=SC_SUBCORES)
    return pl.kernel(
        body, mesh=mesh, out_type=jax.ShapeDtypeStruct(out_shape, dtype),
        scratch_types=[pltpu.VMEM((n_chunks, SC_CHUNK), jnp.int32),
                       pltpu.VMEM((SC_CHUNK, width), dtype),
                       pltpu.SemaphoreType.DMA],
        name=name)


def _sc_chunks(n):
    workers = SC_CORES * SC_SUBCORES
    chunks = n // (workers * SC_CHUNK)
    assert chunks * workers * SC_CHUNK == n
    return workers, chunks


def _sc_gather_rows(table, idx):
    workers, chunks = _sc_chunks(idx.shape[0])

    def body(table_hbm, idx_hbm, out_hbm, idx_v, rows_v, sem):
        wid = lax.axis_index("s") * SC_CORES + lax.axis_index("c")
        pltpu.sync_copy(idx_hbm.at[wid], idx_v)

        @pl.loop(0, chunks)
        def _(j):
            pltpu.async_copy(table_hbm.at[idx_v.at[j]], rows_v, sem).wait()
            pltpu.sync_copy(rows_v, out_hbm.at[pl.ds((wid * chunks + j) * SC_CHUNK, SC_CHUNK)])

    call = _sc_rows_kernel(body, (idx.shape[0], table.shape[1]), chunks, table.shape[1],
                           table.dtype, "moe_gather_sc")
    return call(table, idx.reshape(workers, chunks, SC_CHUNK))


def _sc_scatter_rows(src, idx, n_rows):
    n_src = src.shape[0]
    assert n_src % SC_CHUNK == 0
    workers, chunks = _sc_chunks(idx.shape[0])

    def body(src_hbm, idx_hbm, out_hbm, idx_v, rows_v, sem):
        wid = lax.axis_index("s") * SC_CORES + lax.axis_index("c")
        pltpu.sync_copy(idx_hbm.at[wid], idx_v)

        @pl.loop(0, chunks)
        def _(j):
            first = lax.rem((wid * chunks + j) * SC_CHUNK, n_src)
            pltpu.sync_copy(src_hbm.at[pl.ds(first, SC_CHUNK)], rows_v)
            pltpu.async_copy(rows_v, out_hbm.at[idx_v.at[j]], sem).wait()

    call = _sc_rows_kernel(body, (n_rows, src.shape[1]), chunks, src.shape[1], src.dtype,
                           "moe_scatter_sc")
    return call(src, idx.reshape(workers, chunks, SC_CHUNK))


def _silu(g):
    return g * jax.nn.sigmoid(g)


SCHED_FIRST, SCHED_SLOT, SCHED_NEXT, SCHED_VALID = 0, 1, 2, 3


def _expert_kernel(sched_ref, n_used_ref, x_ref, wg_hbm, wu_hbm, wd_hbm, o_ref,
                   wg_buf, wu_buf, wd_buf, sem, *, layer, first_expert_row):
    i = pl.program_id(0)
    slot = sched_ref[SCHED_SLOT, i]

    def copies(expert, slot):
        pairs = ((wg_hbm, wg_buf), (wu_hbm, wu_buf), (wd_hbm, wd_buf))
        return [pltpu.make_async_copy(src.at[layer, expert], dst.at[slot], sem.at[slot, n])
                for n, (src, dst) in enumerate(pairs)]

    @pl.when(i == 0)
    def _():
        for c in copies(sched_ref[first_expert_row, 0], 0):
            c.start()

    @pl.when(sched_ref[SCHED_FIRST, i] == 1)
    def _():
        for c in copies(0, slot):
            c.wait()

        @pl.when(sched_ref[SCHED_NEXT, i] >= 0)
        def _():
            for c in copies(sched_ref[SCHED_NEXT, i], 1 - slot):
                c.start()

    @pl.when(i < n_used_ref[0])
    def _():
        words = x_ref[...]
        row = lax.broadcasted_iota(jnp.int32, words.shape, 0)
        words = jnp.where(row < sched_ref[SCHED_VALID, i], words, 0)
        x = jnp.concatenate(_unpack_halves(words), axis=1).astype(BF16)
        g = jnp.dot(x, wg_buf[slot].astype(BF16), preferred_element_type=F32)
        u = jnp.dot(x, wu_buf[slot].astype(BF16), preferred_element_type=F32)
        a = (_silu(g) * u).astype(BF16)
        o_ref[...] = _pack_halves(jnp.dot(a, wd_buf[slot].astype(BF16), preferred_element_type=F32))

    @pl.when(i >= n_used_ref[0])
    def _():
        o_ref[...] = jnp.zeros_like(o_ref)


def _expert_schedule(seg_start, seg_end, counts, n_blocks):
    E = seg_end.shape[0]
    first_row = jnp.arange(n_blocks, dtype=jnp.int32) * EXPERT_ROWS
    blk_e = jnp.minimum(jnp.sum(seg_end[None, :] <= first_row[:, None], axis=1), E - 1)
    used = first_row < seg_end[-1]
    first = used & (first_row == seg_start[blk_e])
    ordinal = jnp.cumsum(first) - 1
    ids = jnp.where(seg_end > seg_start, jnp.arange(E), E)
    later = lax.cummin(ids, reverse=True)
    nxt = jnp.concatenate([later[1:], jnp.full((1,), E, later.dtype)])[blk_e]
    nxt = jnp.where(nxt < E, nxt, -1)
    valid = jnp.clip((seg_start + counts)[blk_e] - first_row, 0, EXPERT_ROWS)
    return jnp.stack([first, ordinal % 2, nxt, valid, blk_e]).astype(jnp.int32)


def _expert_mlp(xs, sched, n_used, w_gate, w_up, w_down, layer):
    n_rows, half = xs.shape
    D = 2 * half
    F = w_gate.shape[-1]
    tm = EXPERT_ROWS
    hbm = pl.BlockSpec(memory_space=pl.ANY)
    return pl.pallas_call(
        functools.partial(_expert_kernel, layer=layer, first_expert_row=sched.shape[0] - 1),
        out_shape=jax.ShapeDtypeStruct((n_rows, half), jnp.int32),
        grid_spec=pltpu.PrefetchScalarGridSpec(
            num_scalar_prefetch=2,
            grid=(n_rows // tm,),
            in_specs=[pl.BlockSpec((tm, half), lambda i, sc, nu: (jnp.minimum(i, nu[0] - 1), 0)),
                      hbm, hbm, hbm],
            out_specs=pl.BlockSpec((tm, half), lambda i, sc, nu: (i, 0)),
            scratch_shapes=[pltpu.VMEM((2, D, F), F32), pltpu.VMEM((2, D, F), F32),
                            pltpu.VMEM((2, F, D), F32), pltpu.SemaphoreType.DMA((2, 3))]),
        compiler_params=_cparams("arbitrary"),
        name="moe_experts",
    )(sched, n_used, xs, w_gate, w_up, w_down)


def _combine_kernel(x_ref, gate_ref, rows_ref, wsg_ref, wsu_ref, wsd_ref, g_ref, b_ref, o_ref,
                    *, alpha):
    x = x_ref[...]
    xb = x.astype(BF16)
    g = jnp.dot(xb, wsg_ref[...], preferred_element_type=F32)
    u = jnp.dot(xb, wsu_ref[...], preferred_element_type=F32)
    y = jnp.dot((_silu(g) * u).astype(BF16), wsd_ref[...], preferred_element_type=F32)
    gates = gate_ref[...]
    half = y.shape[1] // 2
    left, right = y[:, :half], y[:, half:]
    for k in range(TOP_K):
        lo, hi = _unpack_halves(rows_ref[k])
        left = left + lo * gates[:, k:k + 1]
        right = right + hi * gates[:, k:k + 1]
    y = jnp.concatenate([left, right], axis=1)
    o_ref[...] = _layer_norm(alpha * x + y, g_ref[...], b_ref[...])


def _combine(x2d, ys, dest, gates_tok, ws_gate, ws_up, ws_down, g, b, alpha, tt):
    T, D = x2d.shape
    F = ws_gate.shape[1]
    row = lambda i: (i, 0)
    fixed = lambda i: (0, 0)
    rows = _sc_gather_rows(ys, dest.reshape(-1)).reshape(TOP_K, T, D // 2)
    return pl.pallas_call(
        functools.partial(_combine_kernel, alpha=alpha),
        out_shape=jax.ShapeDtypeStruct((T, D), F32),
        grid=(T // tt,),
        in_specs=[pl.BlockSpec((tt, D), row),
                  pl.BlockSpec((tt, LANES), row),
                  pl.BlockSpec((TOP_K, tt, D // 2), lambda i: (0, i, 0)),
                  pl.BlockSpec((D, F), fixed), pl.BlockSpec((D, F), fixed),
                  pl.BlockSpec((F, D), fixed),
                  pl.BlockSpec((1, D), fixed), pl.BlockSpec((1, D), fixed)],
        out_specs=pl.BlockSpec((tt, D), row),
        compiler_params=_cparams("parallel"),
        name="moe_combine",
    )(x2d, gates_tok, rows, ws_gate.astype(BF16), ws_up.astype(BF16), ws_down.astype(BF16),
      g.reshape(1, D), b.reshape(1, D))


def _pick(n, pref):
    t = min(n, pref)
    while n % t:
        t //= 2
    return t


def _moe_layer(x2d, packed, w_router, router_bias, w_gate, w_up, w_down, layer, ws_gate, ws_up,
               ws_down, g, b, alpha):
    T, D = x2d.shape
    E = w_router.shape[1]
    idx, gates, rank, counts = _router(x2d, w_router, router_bias, _pick(T, 256))
    counts = counts[:, 0].astype(jnp.int32)
    padded = (counts + EXPERT_ROWS - 1) // EXPERT_ROWS * EXPERT_ROWS
    seg_end = jnp.cumsum(padded)
    seg_start = seg_end - padded
    n_blocks = (T * TOP_K + E * (EXPERT_ROWS - 1)) // EXPERT_ROWS
    dest = _slots(idx, rank, seg_start, _pick(T, 1024))
    sched = _expert_schedule(seg_start, seg_end, counts, n_blocks)
    n_used = (seg_end[-1:] // EXPERT_ROWS).astype(jnp.int32)
    xs = _sc_scatter_rows(packed, dest.reshape(-1), n_blocks * EXPERT_ROWS)
    ys = _expert_mlp(xs, sched, n_used, w_gate, w_up, w_down, layer)
    return _combine(x2d, ys, dest, gates, ws_gate, ws_up, ws_down, g, b, alpha, _pick(T, 256))


def _mixer_layer(x2d, i, B, S, p, alpha):
    T, D = x2d.shape
    tm = _pick(T, 512)
    tn = _pick(D, 1024)
    j = i // 2
    w_in = p["moba_w_in"][j] if i % 2 == 0 else p["fox_w_in"][j]
    w_qv_t = jnp.concatenate([w_in[:, :D].T * HEAD_DIM ** -0.5, w_in[:, 2 * D:3 * D].T]).astype(BF16)
    qv_t = _project_t(x2d, w_qv_t, tm, tn)
    k = _project(x2d, w_in[:, D:2 * D].astype(BF16), tm, tn)
    if i % 2 == 0:
        k_aug, k_mean = _moba_kprep(k.reshape(B, S, D))
        attn_t = _moba_attention(qv_t, k_aug, k_mean, p["rel_bias"], B, S)
        w_out = p["moba_w_out"][j]
    else:
        c_tok = _fox_gates(x2d.reshape(B, S, D), w_in[:, 3 * D:], p["fox_b_f"][j], _pick(S, 512))
        k_aug = _fox_kprep(k, c_tok.reshape(T, LANES), tm).reshape(HEADS_PER_TILE, B, S, D)
        attn_t = _fox_attention(qv_t, k_aug, B, S, _pick(S, 256))
        w_out = p["fox_w_out"][j]
    return _outproj_ln(attn_t, w_out.astype(BF16), x2d, p["ln1_g"][i], p["ln1_b"][i], alpha, tm)


def kernel(x, rel_bias, moba_w_in, moba_w_out, fox_w_in, fox_b_f, fox_w_out, ln1_g, ln1_b, ln2_g,
           ln2_b, w_router, router_bias, w_gate, w_up, w_down, ws_gate, ws_up, ws_down):
    B, S, D = x.shape
    depth = ln1_g.shape[0]
    alpha = (2 * depth) ** 0.25
    p = dict(rel_bias=rel_bias, moba_w_in=moba_w_in, moba_w_out=moba_w_out, fox_w_in=fox_w_in,
             fox_b_f=fox_b_f, fox_w_out=fox_w_out, ln1_g=ln1_g, ln1_b=ln1_b)
    x2d = x.reshape(B * S, D)
    for i in range(depth):
        x2d, packed = _mixer_layer(x2d, i, B, S, p, alpha)
        x2d = _moe_layer(x2d, packed, w_router[i], router_bias[i], w_gate, w_up, w_down, i,
                         ws_gate[i], ws_up[i], ws_down[i], ln2_g[i], ln2_b[i], alpha)
    return x2d.reshape(B, S, D)
```

```python
import functools
import math

import jax
import jax.numpy as jnp
from jax import lax
from jax.experimental import pallas as pl
from jax.experimental.pallas import tpu as pltpu
from jax.experimental.pallas import tpu_sc as plsc

F32 = jnp.float32
BF16 = jnp.bfloat16
HIGHEST = lax.Precision.HIGHEST

HEAD_DIM = 64
MOBA_BLOCK = 256
MOBA_TOPK = 3
REL_MAX_DIST = 128
TOP_K = 8
N_GROUPS = 8
TOPK_GROUPS = 4
ROUTED_SCALE = 2.5
LN_EPS = 1e-5
NEG = -1e30

LANES = 128
HEADS_PER_TILE = LANES // HEAD_DIM
EXPERT_ROWS = 512
VMEM_LIMIT = 48 * 1024 * 1024


def _cparams(*sem):
    return pltpu.CompilerParams(dimension_semantics=sem, vmem_limit_bytes=VMEM_LIMIT)


def _nt_dot(a, b, **kw):
    return lax.dot_general(a, b, (((1,), (1,)), ((), ())), preferred_element_type=F32, **kw)


def _split_bf16(x, pieces):
    out = []
    for _ in range(pieces):
        out.append(x.astype(BF16))
        x = x - out[-1].astype(F32)
    return out


def _three_pass(x, w_ref, mul):
    hi, lo = _split_bf16(x, 2)
    return mul(hi, w_ref[0]) + mul(lo, w_ref[0]) + mul(hi, w_ref[1])


def _proj_kernel(x_ref, w_ref, o_ref):
    o_ref[...] = jnp.dot(x_ref[...].astype(BF16), w_ref[...],
                         preferred_element_type=F32).astype(o_ref.dtype)


def _project(x2d, w, tm, tn):
    T, K = x2d.shape
    N = w.shape[1]
    return pl.pallas_call(
        _proj_kernel,
        out_shape=jax.ShapeDtypeStruct((T, N), BF16),
        grid=(T // tm, N // tn),
        in_specs=[pl.BlockSpec((tm, K), lambda i, j: (i, 0)),
                  pl.BlockSpec((K, tn), lambda i, j: (0, j))],
        out_specs=pl.BlockSpec((tm, tn), lambda i, j: (i, j)),
        compiler_params=_cparams("parallel", "arbitrary"),
        name="k_proj",
    )(x2d, w)


def _proj_t_kernel(x_ref, wt_ref, o_ref):
    o_ref[...] = _nt_dot(wt_ref[...], x_ref[...].astype(BF16)).astype(o_ref.dtype)


def _project_t(x2d, wt, tm, tn):
    T, K = x2d.shape
    N = wt.shape[0]
    return pl.pallas_call(
        _proj_t_kernel,
        out_shape=jax.ShapeDtypeStruct((N, T), BF16),
        grid=(T // tm, N // tn),
        in_specs=[pl.BlockSpec((tm, K), lambda i, j: (i, 0)),
                  pl.BlockSpec((tn, K), lambda i, j: (j, 0))],
        out_specs=pl.BlockSpec((tn, tm), lambda i, j: (j, i)),
        compiler_params=_cparams("parallel", "arbitrary"),
        name="qv_proj_t",
    )(x2d, wt)


def _layer_norm(r, g, b):
    mu = jnp.mean(r, axis=-1, keepdims=True)
    d = r - mu
    var = jnp.mean(d * d, axis=-1, keepdims=True)
    return d * lax.rsqrt(var + LN_EPS) * g + b


def _pack_halves(y):
    n = y.shape[1] // 2
    bits = lambda v: lax.bitcast_convert_type(v.astype(BF16).astype(F32), jnp.int32)
    left = bits(y[:, :n])
    return lax.shift_right_logical(left, jnp.full_like(left, 16)) | bits(y[:, n:])


def _unpack_halves(p):
    return (lax.bitcast_convert_type(p << 16, F32),
            lax.bitcast_convert_type(p & jnp.int32(-65536), F32))


def _outproj_ln_kernel(at_ref, w_ref, x_ref, g_ref, b_ref, o_ref, packed_ref, *, alpha):
    y = lax.dot_general(at_ref[...], w_ref[...], (((0,), (0,)), ((), ())),
                        preferred_element_type=F32)
    out = _layer_norm(alpha * x_ref[...] + y, g_ref[...], b_ref[...])
    o_ref[...] = out
    packed_ref[...] = _pack_halves(out)


def _outproj_ln(attn_t, w_out, x2d, g, b, alpha, tm):
    T, D = x2d.shape
    row = lambda i: (i, 0)
    fixed = lambda i: (0, 0)
    return pl.pallas_call(
        functools.partial(_outproj_ln_kernel, alpha=alpha),
        out_shape=(jax.ShapeDtypeStruct((T, D), F32), jax.ShapeDtypeStruct((T, D // 2), jnp.int32)),
        grid=(T // tm,),
        in_specs=[pl.BlockSpec((D, tm), lambda i: (0, i)), pl.BlockSpec((D, D), fixed),
                  pl.BlockSpec((tm, D), row), pl.BlockSpec((1, D), fixed),
                  pl.BlockSpec((1, D), fixed)],
        out_specs=(pl.BlockSpec((tm, D), row), pl.BlockSpec((tm, D // 2), row)),
        compiler_params=_cparams("parallel"),
        name="outproj_ln",
    )(attn_t, w_out, x2d, g.reshape(1, D), b.reshape(1, D))


def _spare(a, n=0):
    return ((a + 1) % HEADS_PER_TILE) * HEAD_DIM + n


def _scores(chains):
    return [[jnp.dot(keys(), query(), preferred_element_type=F32) for keys, query, _ in blocks]
            for blocks in chains]


def _absorb(carries, scores, chains, masks=None, adds=None):
    probs = []
    for c, ((m, _), parts) in enumerate(zip(carries, scores)):
        if adds is not None:
            parts = [s if add is None else s + add() for add, s in zip(adds[c], parts)]
        if masks is not None:
            parts = [s if keep is None else jnp.where(keep, s, NEG)
                     for keep, s in zip(masks[c], parts)]
        tops = [jnp.max(s, axis=0, keepdims=True) for s in parts]
        m_new = functools.reduce(jnp.maximum, tops, m)
        probs.append((m_new, jnp.exp(m - m_new), [jnp.exp(s - m_new).astype(BF16) for s in parts]))
    out = []
    for (_, acc), (m_new, decay, ps), blocks in zip(carries, probs, chains):
        acc = decay * acc
        for (_, _, values), p in zip(blocks, ps):
            acc = acc + jnp.dot(values(), p, preferred_element_type=F32)
        out.append((m_new, acc))
    return tuple(out)


def _stash(chains, ref):
    for a, parts in enumerate(_scores(chains)):
        for b, s in enumerate(parts):
            ref[a, b] = s


def _fetch(chains, ref):
    return [[ref[a, b] for b in range(len(blocks))] for a, blocks in enumerate(chains)]


def _attend_masked_last(carries, n, group, stage_refs, masks, adds=None):
    first, second = stage_refs
    _stash(group(0), first)
    trips = n // 2

    def body(t, carries):
        g = 2 * t
        _stash(group(g + 1), second)
        carries = _absorb(carries, _fetch(group(g), first), group(g))
        _stash(group(g + 2), first)
        return _absorb(carries, _fetch(group(g + 1), second), group(g + 1))

    carries = lax.fori_loop(0, trips, body, carries)

    def odd_tail(carries):
        _stash(group(n), second)
        carries = _absorb(carries, _fetch(group(n - 1), first), group(n - 1))
        return _absorb(carries, _fetch(group(n), second), group(n), masks, adds)

    def even_tail(carries):
        return _absorb(carries, _fetch(group(n), first), group(n), masks, adds)

    return lax.cond(n % 2 == 1, odd_tail, even_tail, carries)


GROUP_BLOCKS = 2


def _stage_scratch(tq, chains=HEADS_PER_TILE):
    return [pltpu.VMEM((chains, GROUP_BLOCKS, tq, tq), F32) for _ in range(2)]


def _scores_init(tq):
    return jnp.full((1, tq), -jnp.inf, F32), jnp.zeros((LANES, tq), F32)


def _finish(carries, o_ref, cols=slice(None)):
    row = lax.broadcasted_iota(jnp.int32, carries[0][1].shape, 0)
    outs = []
    for a, (_, acc) in enumerate(carries):
        s = _spare(a)
        outs.append(acc / acc[s:s + 1, :])
    o_ref[:, cols] = jnp.where(row < HEAD_DIM, outs[0], outs[1]).astype(o_ref.dtype)


def _fill_v_aug(vt_ref, vaug_ref, tk):
    n_tiles = vt_ref.shape[1] // tk
    row = lax.broadcasted_iota(jnp.int32, (LANES, tk), 0)
    for n in range(n_tiles):
        v = vt_ref[:, n * tk:(n + 1) * tk]
        for a in range(HEADS_PER_TILE):
            vaug_ref[a, n] = jnp.where(row // HEAD_DIM == a, v, jnp.ones_like(v))


FOX_SPLIT = 3


def _fox_kernel(qt_ref, k_ref, vt_ref, o_ref, vaug_ref, stage0_ref, stage1_ref, *, tq):
    n = pl.program_id(2)

    @pl.when(n == 0)
    def _():
        _fill_v_aug(vt_ref, vaug_ref, tq)

    chains = [(h, a) for h in range(GROUP_BLOCKS) for a in range(HEADS_PER_TILE)]
    row = lax.broadcasted_iota(jnp.int32, (LANES, tq), 0)
    queries = {}
    for h, a in chains:
        offs = (row >= _spare(a)) & (row < _spare(a, FOX_SPLIT))
        queries[h, a] = jnp.where(row // HEAD_DIM == a, qt_ref[:, h * tq:(h + 1) * tq],
                                  jnp.where(offs, -1.0, 0.0).astype(BF16))

    def block(h, a, j):
        start = pl.multiple_of(j * tq, tq)
        return (lambda: k_ref[a, 0, pl.ds(start, tq), :], lambda: queries[h, a],
                lambda: vaug_ref[a, j])

    key = lax.broadcasted_iota(jnp.int32, (tq, tq), 0)
    qry = lax.broadcasted_iota(jnp.int32, (tq, tq), 1)
    masks = [[None if b < h else key + (b - h) * tq <= qry for b in range(GROUP_BLOCKS)]
             for h, _ in chains]
    carries = _attend_masked_last(
        tuple(_scores_init(tq) for _ in chains), n,
        lambda j: [[block(h, a, 2 * j), block(h, a, 2 * j + 1)] for h, a in chains],
        (stage0_ref, stage1_ref), masks)
    for h in range(GROUP_BLOCKS):
        _finish(carries[h * HEADS_PER_TILE:(h + 1) * HEADS_PER_TILE], o_ref,
                slice(h * tq, (h + 1) * tq))


def _fox_attention(qv_t, k_aug, B, S, tq):
    D = k_aug.shape[-1]
    n_tiles = D // LANES
    nq = S // tq
    span = GROUP_BLOCKS * tq
    steps = S // span
    return pl.pallas_call(
        functools.partial(_fox_kernel, tq=tq),
        out_shape=jax.ShapeDtypeStruct((D, B * S), BF16),
        grid=(B, n_tiles, steps),
        in_specs=[pl.BlockSpec((LANES, span), lambda b, h, i: (h, b * steps + i)),
                  pl.BlockSpec((HEADS_PER_TILE, 1, S, LANES), lambda b, h, i: (0, b, 0, h)),
                  pl.BlockSpec((LANES, S), lambda b, h, i: (n_tiles + h, b))],
        out_specs=pl.BlockSpec((LANES, span), lambda b, h, i: (h, b * steps + i)),
        scratch_shapes=[pltpu.VMEM((HEADS_PER_TILE, nq, LANES, tq), BF16)]
        + _stage_scratch(tq, GROUP_BLOCKS * HEADS_PER_TILE),
        compiler_params=_cparams("parallel", "parallel", "arbitrary"),
        name="fox_attention",
    )(qv_t, k_aug, qv_t)


def _log_sigmoid(z):
    return jnp.minimum(z, 0.0) - jnp.log1p(jnp.exp(-jnp.abs(z)))


def _fox_gates_kernel(x_ref, w_ref, b_ref, o_ref, carry_ref, *, ts):
    @pl.when(pl.program_id(1) == 0)
    def _():
        carry_ref[...] = jnp.zeros_like(carry_ref)

    z = _three_pass(x_ref[0], w_ref, lambda x, w: jnp.dot(x, w, preferred_element_type=F32))
    lf = _log_sigmoid(z + b_ref[...])
    r = lax.broadcasted_iota(jnp.int32, (ts, ts), 0)
    c = lax.broadcasted_iota(jnp.int32, (ts, ts), 1)
    tri = (c <= r).astype(BF16)
    cum = carry_ref[...]
    for piece in _split_bf16(lf, 3):
        cum = cum + jnp.dot(tri, piece, preferred_element_type=F32)
    carry_ref[...] = cum[ts - 1:ts, :]
    o_ref[0] = cum


def _fox_gates(x3d, w_f, b_f, ts):
    B, S, D = x3d.shape
    H = w_f.shape[1]
    w_pad = jnp.stack(_split_bf16(jnp.zeros((D, LANES), F32).at[:, :H].set(w_f), 2))
    b_pad = jnp.zeros((1, LANES), F32).at[0, :H].set(b_f)
    return pl.pallas_call(
        functools.partial(_fox_gates_kernel, ts=ts),
        out_shape=jax.ShapeDtypeStruct((B, S, LANES), F32),
        grid=(B, S // ts),
        in_specs=[pl.BlockSpec((1, ts, D), lambda b, s: (b, s, 0)),
                  pl.BlockSpec((2, D, LANES), lambda b, s: (0, 0, 0)),
                  pl.BlockSpec((1, LANES), lambda b, s: (0, 0))],
        out_specs=pl.BlockSpec((1, ts, LANES), lambda b, s: (b, s, 0)),
        scratch_shapes=[pltpu.VMEM((1, LANES), F32)],
        compiler_params=_cparams("parallel", "arbitrary"),
        name="fox_gates",
    )(x3d, w_pad, b_pad)


def _fox_kprep_kernel(k_ref, c_ref, o_ref):
    tm, D = k_ref.shape
    lane = lax.broadcasted_iota(jnp.int32, (tm, LANES), 1)
    for t in range(D // LANES):
        k = k_ref[:, t * LANES:(t + 1) * LANES].astype(F32)
        for a in range(HEADS_PER_TILE):
            h = t * HEADS_PER_TILE + a
            rest = c_ref[:, h:h + 1]
            aug = jnp.zeros((tm, LANES), F32)
            for n in range(FOX_SPLIT):
                piece = rest.astype(BF16).astype(F32)
                rest = rest - piece
                aug = jnp.where(lane == _spare(a, n), piece, aug)
            o_ref[a, :, t * LANES:(t + 1) * LANES] = jnp.where(lane // HEAD_DIM == a, k,
                                                               aug).astype(BF16)


def _fox_kprep(k2d, c_tok, tm):
    T, D = k2d.shape
    return pl.pallas_call(
        _fox_kprep_kernel,
        out_shape=jax.ShapeDtypeStruct((HEADS_PER_TILE, T, D), BF16),
        grid=(T // tm,),
        in_specs=[pl.BlockSpec((tm, D), lambda i: (i, 0)),
                  pl.BlockSpec((tm, LANES), lambda i: (i, 0))],
        out_specs=pl.BlockSpec((HEADS_PER_TILE, tm, D), lambda i: (0, i, 0)),
        compiler_params=_cparams("parallel"),
        name="fox_kprep",
    )(k2d, c_tok)


MOBA_SLAB = 16


def _moba_kernel(bfar_ref, qt_ref, k_ref, vt_ref, kmean_ref, bown_ref, badj_ref, o_ref,
                 vaug_ref, sel_ref, stage0_ref, stage1_ref, *, blk):
    t = pl.program_id(1)
    i = pl.program_id(2)
    nb = kmean_ref.shape[1]

    @pl.when(i == 0)
    def _():
        _fill_v_aug(vt_ref, vaug_ref, blk)

    row = lax.broadcasted_iota(jnp.int32, (LANES, blk), 0)
    brow = lax.broadcasted_iota(jnp.int32, (nb, blk), 0)
    qt = qt_ref[...]
    zero = jnp.zeros_like(qt)
    base, slabs = [], []
    for a in range(HEADS_PER_TILE):
        qa = jnp.where(row // HEAD_DIM == a, qt, zero)
        base.append(qa)
        bscore = jnp.dot(kmean_ref[0], qa.astype(F32), preferred_element_type=F32,
                         precision=HIGHEST)
        cand = jnp.where(brow < i, bscore, NEG)
        sel = jnp.zeros((nb, blk), F32)
        for _ in range(MOBA_TOPK):
            mx = jnp.max(cand, axis=0, keepdims=True)
            first = jnp.min(jnp.where(cand == mx, brow, nb), axis=0, keepdims=True)
            hit = brow == first
            sel = jnp.where(hit, 1.0, sel)
            cand = jnp.where(hit, -jnp.inf, cand)
        sel_ref[a] = jnp.where(brow < i, sel, 0.0)
        slabs.append(_spare(a) // MOBA_SLAB * MOBA_SLAB)

    srow = lax.broadcasted_iota(jnp.int32, (MOBA_SLAB, blk), 0)

    def block(a, j):
        h = t * HEADS_PER_TILE + a
        jc = jnp.maximum(j, 0)
        start = pl.multiple_of(jc * blk, blk)

        def query():
            s0 = slabs[a]
            chosen = ((sel_ref[a, pl.ds(jc, 1), :] > 0.0) & (j >= 0)) | (j == i)
            slab = jnp.where(srow == _spare(a) - s0, jnp.where(chosen, bfar_ref[0, h], NEG),
                             jnp.where(srow == _spare(a, 1) - s0, bfar_ref[1, h], 0.0)).astype(BF16)
            head = [base[a][:s0]] if s0 else []
            tail = [base[a][s0 + MOBA_SLAB:]] if s0 + MOBA_SLAB < LANES else []
            return jnp.concatenate(head + [slab] + tail, axis=0)

        return lambda: k_ref[a, 0, pl.ds(start, blk), :], query, lambda: vaug_ref[a, jc]

    off = (i + 1) % 2
    n = (i + off) // 2
    key = lax.broadcasted_iota(jnp.int32, (blk, blk), 0)
    qry = lax.broadcasted_iota(jnp.int32, (blk, blk), 1)
    heads = range(HEADS_PER_TILE)
    carries = _attend_masked_last(
        tuple(_scores_init(blk) for _ in heads), n,
        lambda g: [[block(a, 2 * g - off), block(a, 2 * g + 1 - off)] for a in heads],
        (stage0_ref, stage1_ref), [[None, key <= qry] for _ in heads],
        [[lambda a=a: badj_ref[a], lambda a=a: bown_ref[a]] for a in heads])
    _finish(carries, o_ref)


def _rel_bucket(dist, n_buckets):
    n = jnp.maximum(dist, 0)
    max_exact = n_buckets // 2
    nf = jnp.maximum(n, 1).astype(F32)
    large = max_exact + (jnp.log(nf / max_exact) / math.log(REL_MAX_DIST / max_exact)
                         * (n_buckets - max_exact)).astype(jnp.int32)
    large = jnp.minimum(large, n_buckets - 1)
    return jnp.where(n < max_exact, n, large)


def _moba_attention(qv_t, k_aug, k_mean, rel_bias, B, S):
    D = k_aug.shape[-1]
    n_tiles = D // LANES
    blk = MOBA_BLOCK
    nb = S // blk
    hp = HEADS_PER_TILE
    r = jnp.arange(blk)
    delta = r[None, :] - r[:, None]
    bias_t = rel_bias.T.astype(F32)
    n_buckets = rel_bias.shape[0]
    def table(dist):
        onehot = jax.nn.one_hot(_rel_bucket(dist, n_buckets), n_buckets, dtype=F32)
        return jnp.einsum('crn,hn->hcr', onehot, bias_t, precision=HIGHEST)

    b_far = bias_t[:, n_buckets - 1]
    b_own = table(delta) - b_far[:, None, None]
    b_adj = table(delta + blk) - b_far[:, None, None]
    far_hi = b_far.astype(BF16).astype(F32)
    far_lo = (b_far - far_hi).astype(BF16).astype(F32)
    b_far2 = jnp.stack([far_hi, far_lo])
    return pl.pallas_call(
        functools.partial(_moba_kernel, blk=blk),
        out_shape=jax.ShapeDtypeStruct((D, B * S), BF16),
        grid_spec=pltpu.PrefetchScalarGridSpec(
            num_scalar_prefetch=1,
            grid=(B, n_tiles, nb),
            in_specs=[pl.BlockSpec((LANES, blk), lambda b, h, i, f: (h, b * nb + i)),
                      pl.BlockSpec((hp, 1, S, LANES), lambda b, h, i, f: (0, b, 0, h)),
                      pl.BlockSpec((LANES, S), lambda b, h, i, f: (n_tiles + h, b)),
                      pl.BlockSpec((1, nb, LANES), lambda b, h, i, f: (b, 0, h)),
                      pl.BlockSpec((hp, blk, blk), lambda b, h, i, f: (h, 0, 0)),
                      pl.BlockSpec((hp, blk, blk), lambda b, h, i, f: (h, 0, 0))],
            out_specs=pl.BlockSpec((LANES, blk), lambda b, h, i, f: (h, b * nb + i)),
            scratch_shapes=[pltpu.VMEM((hp, nb, LANES, blk), BF16),
                            pltpu.VMEM((hp, nb, blk), F32)] + _stage_scratch(blk)),
        compiler_params=_cparams("parallel", "parallel", "arbitrary"),
        name="moba_attention",
    )(b_far2, qv_t, k_aug, qv_t, k_mean, b_own, b_adj)


def _moba_kprep_kernel(k_ref, o_ref, mean_ref):
    k = k_ref[0]
    col = lax.broadcasted_iota(jnp.int32, k.shape, 1)
    lane = col % LANES
    for a in range(HEADS_PER_TILE):
        ones = (lane == _spare(a)) | (lane == _spare(a, 1))
        o_ref[a, 0] = jnp.where((col // HEAD_DIM) % HEADS_PER_TILE == a, k,
                                jnp.where(ones, 1.0, 0.0).astype(BF16))
    mean_ref[0, 0] = jnp.mean(k.astype(F32), axis=0, keepdims=True)


def _moba_kprep(k3d):
    B, S, D = k3d.shape
    blk = MOBA_BLOCK
    nb = S // blk
    k_aug, k_mean = pl.pallas_call(
        _moba_kprep_kernel,
        out_shape=(jax.ShapeDtypeStruct((HEADS_PER_TILE, B, S, D), BF16),
                   jax.ShapeDtypeStruct((B, nb, 1, D), F32)),
        grid=(B, nb),
        in_specs=[pl.BlockSpec((1, blk, D), lambda b, n: (b, n, 0))],
        out_specs=(pl.BlockSpec((HEADS_PER_TILE, 1, blk, D), lambda b, n: (0, b, n, 0)),
                   pl.BlockSpec((1, 1, 1, D), lambda b, n: (b, n, 0, 0))),
        compiler_params=_cparams("parallel", "parallel"),
        name="moba_kprep",
    )(k3d)
    return k_aug, k_mean.reshape(B, nb, D)


def _router_kernel(x_ref, wt_ref, bias_ref, idx_ref, gate_ref, rank_ref, cnt_ref, run_ref, *, tm):
    @pl.when(pl.program_id(0) == 0)
    def _():
        run_ref[...] = jnp.zeros_like(run_ref)

    E = wt_ref.shape[1]
    gsz = E // N_GROUPS
    logits = _three_pass(x_ref[...], wt_ref, lambda x, w: _nt_dot(w, x))
    s = jax.nn.sigmoid(logits)
    sb = s + bias_ref[...]
    neg_inf = -jnp.inf

    giota = lax.broadcasted_iota(jnp.int32, (gsz, tm), 0)
    gscore = []
    for g in range(N_GROUPS):
        blk = sb[g * gsz:(g + 1) * gsz, :]
        m1 = jnp.max(blk, axis=0, keepdims=True)
        i1 = jnp.min(jnp.where(blk == m1, giota, gsz), axis=0, keepdims=True)
        m2 = jnp.max(jnp.where(giota == i1, neg_inf, blk), axis=0, keepdims=True)
        gscore.append(m1 + m2)

    gsel = [jnp.zeros((1, tm), jnp.bool_) for _ in range(N_GROUPS)]
    for _ in range(TOPK_GROUPS):
        mx = functools.reduce(jnp.maximum, gscore)
        found = jnp.zeros((1, tm), jnp.bool_)
        for g in range(N_GROUPS):
            hit = (gscore[g] == mx) & jnp.logical_not(found)
            gsel[g] = gsel[g] | hit
            found = found | hit
            gscore[g] = jnp.where(hit, neg_inf, gscore[g])
    emask = jnp.concatenate([jnp.broadcast_to(gsel[g], (gsz, tm)) for g in range(N_GROUPS)], axis=0)
    cand = jnp.where(emask, sb, NEG)

    eiota = lax.broadcasted_iota(jnp.int32, (E, tm), 0)
    hits, idxs, ws = [], [], []
    for _ in range(TOP_K):
        mx = jnp.max(cand, axis=0, keepdims=True)
        first = jnp.min(jnp.where(cand == mx, eiota, E), axis=0, keepdims=True)
        hit = eiota == first
        hits.append(hit)
        idxs.append(first)
        ws.append(jnp.sum(jnp.where(hit, s, 0.0), axis=0, keepdims=True))
        cand = jnp.where(hit, neg_inf, cand)
    wsum = functools.reduce(jnp.add, ws)

    chosen = functools.reduce(jnp.logical_or, hits)
    onehot = jnp.where(chosen, 1.0, 0.0)
    tr = lax.broadcasted_iota(jnp.int32, (tm, tm), 0)
    tc = lax.broadcasted_iota(jnp.int32, (tm, tm), 1)
    before = (tr < tc).astype(BF16)
    prior = jnp.dot(onehot.astype(BF16), before, preferred_element_type=F32) + run_ref[...]
    grow = lax.broadcasted_iota(jnp.int32, (LANES, tm), 0)
    gates = jnp.zeros((LANES, tm), F32)
    for k in range(TOP_K):
        idx_ref[k:k + 1, :] = idxs[k]
        gates = jnp.where(grow == k, ws[k] / wsum * ROUTED_SCALE, gates)
        rank_ref[k:k + 1, :] = jnp.sum(jnp.where(hits[k], prior, 0.0), axis=0,
                                       keepdims=True).astype(jnp.int32)
    gate_ref[...] = gates.T
    run_ref[...] = run_ref[...] + jnp.sum(onehot, axis=1, keepdims=True)
    cnt_ref[...] = run_ref[...]


def _router(x2d, w_router, router_bias, tm):
    T, D = x2d.shape
    E = w_router.shape[1]
    tok = lambda i: (0, i)
    fixed = lambda i: (0, 0)
    return pl.pallas_call(
        functools.partial(_router_kernel, tm=tm),
        out_shape=(jax.ShapeDtypeStruct((TOP_K, T), jnp.int32),
                   jax.ShapeDtypeStruct((T, LANES), F32),
                   jax.ShapeDtypeStruct((TOP_K, T), jnp.int32),
                   jax.ShapeDtypeStruct((E, 1), F32)),
        grid=(T // tm,),
        in_specs=[pl.BlockSpec((tm, D), lambda i: (i, 0)),
                  pl.BlockSpec((2, E, D), lambda i: (0, 0, 0)),
                  pl.BlockSpec((E, 1), fixed)],
        out_specs=(pl.BlockSpec((TOP_K, tm), tok), pl.BlockSpec((tm, LANES), lambda i: (i, 0)),
                   pl.BlockSpec((TOP_K, tm), tok), pl.BlockSpec((E, 1), fixed)),
        scratch_shapes=[pltpu.VMEM((E, 1), F32)],
        compiler_params=_cparams("arbitrary"),
        name="router",
    )(x2d, jnp.stack(_split_bf16(w_router.T, 2)), router_bias.reshape(E, 1).astype(F32))


def _slots_kernel(idx_ref, rank_ref, start_ref, o_ref):
    E = start_ref.shape[0]
    tm = idx_ref.shape[1]
    eiota = lax.broadcasted_iota(jnp.int32, (E, tm), 0)
    start = start_ref[...]
    for k in range(TOP_K):
        base = jnp.sum(jnp.where(eiota == idx_ref[k:k + 1, :], start, 0.0), axis=0, keepdims=True)
        o_ref[k:k + 1, :] = base.astype(jnp.int32) + rank_ref[k:k + 1, :]


def _slots(idx, rank, seg_start, tm):
    T = idx.shape[1]
    E = seg_start.shape[0]
    tok = lambda i: (0, i)
    return pl.pallas_call(
        _slots_kernel,
        out_shape=jax.ShapeDtypeStruct((TOP_K, T), jnp.int32),
        grid=(T // tm,),
        in_specs=[pl.BlockSpec((TOP_K, tm), tok), pl.BlockSpec((TOP_K, tm), tok),
                  pl.BlockSpec((E, 1), lambda i: (0, 0))],
        out_specs=pl.BlockSpec((TOP_K, tm), tok),
        compiler_params=_cparams("parallel"),
        name="moe_slots",
    )(idx, rank, seg_start.astype(F32).reshape(E, 1))


SC_CORES = 2
SC_SUBCORES = 16
SC_CHUNK = 128


def _sc_rows_kernel(body, out_shape, n_chunks, width, dtype, name):
    mesh = plsc.VectorSubcoreMesh(core_axis_name="c", subcore_axis_name="s",
                                  num_cores=SC_CORES, num_subcores=SC_SUBCORES)
    return pl.kernel(
        body, mesh=mesh, out_type=jax.ShapeDtypeStruct(out_shape, dtype),
        scratch_types=[pltpu.VMEM((n_chunks, SC_CHUNK), jnp.int32),
                       pltpu.VMEM((SC_CHUNK, width), dtype),
                       pltpu.SemaphoreType.DMA],
        name=name)


def _sc_chunks(n):
    workers = SC_CORES * SC_SUBCORES
    chunks = n // (workers * SC_CHUNK)
    assert chunks * workers * SC_CHUNK == n
    return workers, chunks


def _sc_gather_rows(table, idx):
    workers, chunks = _sc_chunks(idx.shape[0])

    def body(table_hbm, idx_hbm, out_hbm, idx_v, rows_v, sem):
        wid = lax.axis_index("s") * SC_CORES + lax.axis_index("c")
        pltpu.sync_copy(idx_hbm.at[wid], idx_v)

        @pl.loop(0, chunks)
        def _(j):
            pltpu.async_copy(table_hbm.at[idx_v.at[j]], rows_v, sem).wait()
            pltpu.sync_copy(rows_v, out_hbm.at[pl.ds((wid * chunks + j) * SC_CHUNK, SC_CHUNK)])

    call = _sc_rows_kernel(body, (idx.shape[0], table.shape[1]), chunks, table.shape[1],
                           table.dtype, "moe_gather_sc")
    return call(table, idx.reshape(workers, chunks, SC_CHUNK))


def _sc_scatter_rows(src, idx, n_rows):
    n_src = src.shape[0]
    assert n_src % SC_CHUNK == 0
    workers, chunks = _sc_chunks(idx.shape[0])

    def body(src_hbm, idx_hbm, out_hbm, idx_v, rows_v, sem):
        wid = lax.axis_index("s") * SC_CORES + lax.axis_index("c")
        pltpu.sync_copy(idx_hbm.at[wid], idx_v)

        @pl.loop(0, chunks)
        def _(j):
            first = lax.rem((wid * chunks + j) * SC_CHUNK, n_src)
            pltpu.sync_copy(src_hbm.at[pl.ds(first, SC_CHUNK)], rows_v)
            pltpu.async_copy(rows_v, out_hbm.at[idx_v.at[j]], sem).wait()

    call = _sc_rows_kernel(body, (n_rows, src.shape[1]), chunks, src.shape[1], src.dtype,
                           "moe_scatter_sc")
    return call(src, idx.reshape(workers, chunks, SC_CHUNK))


def _silu(g):
    return g * jax.nn.sigmoid(g)


SCHED_FIRST, SCHED_SLOT, SCHED_NEXT, SCHED_VALID = 0, 1, 2, 3


def _expert_kernel(sched_ref, n_used_ref, x_ref, wg_hbm, wu_hbm, wd_hbm, o_ref,
                   wg_buf, wu_buf, wd_buf, sem, *, layer, first_expert_row):
    i = pl.program_id(0)
    slot = sched_ref[SCHED_SLOT, i]

    def copies(expert, slot):
        pairs = ((wg_hbm, wg_buf), (wu_hbm, wu_buf), (wd_hbm, wd_buf))
        return [pltpu.make_async_copy(src.at[layer, expert], dst.at[slot], sem.at[slot, n])
                for n, (src, dst) in enumerate(pairs)]

    @pl.when(i == 0)
    def _():
        for c in copies(sched_ref[first_expert_row, 0], 0):
            c.start()

    @pl.when(sched_ref[SCHED_FIRST, i] == 1)
    def _():
        for c in copies(0, slot):
            c.wait()

        @pl.when(sched_ref[SCHED_NEXT, i] >= 0)
        def _():
            for c in copies(sched_ref[SCHED_NEXT, i], 1 - slot):
                c.start()

    @pl.when(i < n_used_ref[0])
    def _():
        words = x_ref[...]
        row = lax.broadcasted_iota(jnp.int32, words.shape, 0)
        words = jnp.where(row < sched_ref[SCHED_VALID, i], words, 0)
        x = jnp.concatenate(_unpack_halves(words), axis=1).astype(BF16)
        g = jnp.dot(x, wg_buf[slot].astype(BF16), preferred_element_type=F32)
        u = jnp.dot(x, wu_buf[slot].astype(BF16), preferred_element_type=F32)
        a = (_silu(g) * u).astype(BF16)
        o_ref[...] = _pack_halves(jnp.dot(a, wd_buf[slot].astype(BF16), preferred_element_type=F32))

    @pl.when(i >= n_used_ref[0])
    def _():
        o_ref[...] = jnp.zeros_like(o_ref)


def _expert_schedule(seg_start, seg_end, counts, n_blocks):
    E = seg_end.shape[0]
    first_row = jnp.arange(n_blocks, dtype=jnp.int32) * EXPERT_ROWS
    blk_e = jnp.minimum(jnp.sum(seg_end[None, :] <= first_row[:, None], axis=1), E - 1)
    used = first_row < seg_end[-1]
    first = used & (first_row == seg_start[blk_e])
    ordinal = jnp.cumsum(first) - 1
    ids = jnp.where(seg_end > seg_start, jnp.arange(E), E)
    later = lax.cummin(ids, reverse=True)
    nxt = jnp.concatenate([later[1:], jnp.full((1,), E, later.dtype)])[blk_e]
    nxt = jnp.where(nxt < E, nxt, -1)
    valid = jnp.clip((seg_start + counts)[blk_e] - first_row, 0, EXPERT_ROWS)
    return jnp.stack([first, ordinal % 2, nxt, valid, blk_e]).astype(jnp.int32)


def _expert_mlp(xs, sched, n_used, w_gate, w_up, w_down, layer):
    n_rows, half = xs.shape
    D = 2 * half
    F = w_gate.shape[-1]
    tm = EXPERT_ROWS
    hbm = pl.BlockSpec(memory_space=pl.ANY)
    return pl.pallas_call(
        functools.partial(_expert_kernel, layer=layer, first_expert_row=sched.shape[0] - 1),
        out_shape=jax.ShapeDtypeStruct((n_rows, half), jnp.int32),
        grid_spec=pltpu.PrefetchScalarGridSpec(
            num_scalar_prefetch=2,
            grid=(n_rows // tm,),
            in_specs=[pl.BlockSpec((tm, half), lambda i, sc, nu: (jnp.minimum(i, nu[0] - 1), 0)),
                      hbm, hbm, hbm],
            out_specs=pl.BlockSpec((tm, half), lambda i, sc, nu: (i, 0)),
            scratch_shapes=[pltpu.VMEM((2, D, F), F32), pltpu.VMEM((2, D, F), F32),
                            pltpu.VMEM((2, F, D), F32), pltpu.SemaphoreType.DMA((2, 3))]),
        compiler_params=_cparams("arbitrary"),
        name="moe_experts",
    )(sched, n_used, xs, w_gate, w_up, w_down)


def _combine_kernel(x_ref, gate_ref, rows_ref, wsg_ref, wsu_ref, wsd_ref, g_ref, b_ref, o_ref,
                    *, alpha):
    x = x_ref[...]
    xb = x.astype(BF16)
    g = jnp.dot(xb, wsg_ref[...], preferred_element_type=F32)
    u = jnp.dot(xb, wsu_ref[...], preferred_element_type=F32)
    y = jnp.dot((_silu(g) * u).astype(BF16), wsd_ref[...], preferred_element_type=F32)
    gates = gate_ref[...]
    half = y.shape[1] // 2
    left, right = y[:, :half], y[:, half:]
    for k in range(TOP_K):
        lo, hi = _unpack_halves(rows_ref[k])
        left = left + lo * gates[:, k:k + 1]
        right = right + hi * gates[:, k:k + 1]
    y = jnp.concatenate([left, right], axis=1)
    o_ref[...] = _layer_norm(alpha * x + y, g_ref[...], b_ref[...])


def _combine(x2d, ys, dest, gates_tok, ws_gate, ws_up, ws_down, g, b, alpha, tt):
    T, D = x2d.shape
    F = ws_gate.shape[1]
    row = lambda i: (i, 0)
    fixed = lambda i: (0, 0)
    rows = _sc_gather_rows(ys, dest.reshape(-1)).reshape(TOP_K, T, D // 2)
    return pl.pallas_call(
        functools.partial(_combine_kernel, alpha=alpha),
        out_shape=jax.ShapeDtypeStruct((T, D), F32),
        grid=(T // tt,),
        in_specs=[pl.BlockSpec((tt, D), row),
                  pl.BlockSpec((tt, LANES), row),
                  pl.BlockSpec((TOP_K, tt, D // 2), lambda i: (0, i, 0)),
                  pl.BlockSpec((D, F), fixed), pl.BlockSpec((D, F), fixed),
                  pl.BlockSpec((F, D), fixed),
                  pl.BlockSpec((1, D), fixed), pl.BlockSpec((1, D), fixed)],
        out_specs=pl.BlockSpec((tt, D), row),
        compiler_params=_cparams("parallel"),
        name="moe_combine",
    )(x2d, gates_tok, rows, ws_gate.astype(BF16), ws_up.astype(BF16), ws_down.astype(BF16),
      g.reshape(1, D), b.reshape(1, D))


def _pick(n, pref):
    t = min(n, pref)
    while n % t:
        t //= 2
    return t


def _moe_layer(x2d, packed, w_router, router_bias, w_gate, w_up, w_down, layer, ws_gate, ws_up,
               ws_down, g, b, alpha):
    T, D = x2d.shape
    E = w_router.shape[1]
    idx, gates, rank, counts = _router(x2d, w_router, router_bias, _pick(T, 256))
    counts = counts[:, 0].astype(jnp.int32)
    padded = (counts + EXPERT_ROWS - 1) // EXPERT_ROWS * EXPERT_ROWS
    seg_end = jnp.cumsum(padded)
    seg_start = seg_end - padded
    n_blocks = (T * TOP_K + E * (EXPERT_ROWS - 1)) // EXPERT_ROWS
    dest = _slots(idx, rank, seg_start, _pick(T, 1024))
    sched = _expert_schedule(seg_start, seg_end, counts, n_blocks)
    n_used = (seg_end[-1:] // EXPERT_ROWS).astype(jnp.int32)
    xs = _sc_scatter_rows(packed, dest.reshape(-1), n_blocks * EXPERT_ROWS)
    ys = _expert_mlp(xs, sched, n_used, w_gate, w_up, w_down, layer)
    return _combine(x2d, ys, dest, gates, ws_gate, ws_up, ws_down, g, b, alpha, _pick(T, 256))


def _mixer_layer(x2d, i, B, S, p, alpha):
    T, D = x2d.shape
    tm = _pick(T, 512)
    tn = _pick(D, 1024)
    j = i // 2
    w_in = p["moba_w_in"][j] if i % 2 == 0 else p["fox_w_in"][j]
    w_qv_t = jnp.concatenate([w_in[:, :D].T * HEAD_DIM ** -0.5, w_in[:, 2 * D:3 * D].T]).astype(BF16)
    qv_t = _project_t(x2d, w_qv_t, tm, tn)
    k = _project(x2d, w_in[:, D:2 * D].astype(BF16), tm, tn)
    if i % 2 == 0:
        k_aug, k_mean = _moba_kprep(k.reshape(B, S, D))
        attn_t = _moba_attention(qv_t, k_aug, k_mean, p["rel_bias"], B, S)
        w_out = p["moba_w_out"][j]
    else:
        c_tok = _fox_gates(x2d.reshape(B, S, D), w_in[:, 3 * D:], p["fox_b_f"][j], _pick(S, 512))
        k_aug = _fox_kprep(k, c_tok.reshape(T, LANES), tm).reshape(HEADS_PER_TILE, B, S, D)
        attn_t = _fox_attention(qv_t, k_aug, B, S, _pick(S, 256))
        w_out = p["fox_w_out"][j]
    return _outproj_ln(attn_t, w_out.astype(BF16), x2d, p["ln1_g"][i], p["ln1_b"][i], alpha, tm)


def kernel(x, rel_bias, moba_w_in, moba_w_out, fox_w_in, fox_b_f, fox_w_out, ln1_g, ln1_b, ln2_g,
           ln2_b, w_router, router_bias, w_gate, w_up, w_down, ws_gate, ws_up, ws_down):
    B, S, D = x.shape
    depth = ln1_g.shape[0]
    alpha = (2 * depth) ** 0.25
    p = dict(rel_bias=rel_bias, moba_w_in=moba_w_in, moba_w_out=moba_w_out, fox_w_in=fox_w_in,
             fox_b_f=fox_b_f, fox_w_out=fox_w_out, ln1_g=ln1_g, ln1_b=ln1_b)
    x2d = x.reshape(B * S, D)
    for i in range(depth):
        x2d, packed = _mixer_layer(x2d, i, B, S, p, alpha)
        x2d = _moe_layer(x2d, packed, w_router[i], router_bias[i], w_gate, w_up, w_down, i,
                         ws_gate[i], ws_up[i], ws_down[i], ln2_g[i], ln2_b[i], alpha)
    return x2d.reshape(B, S, D)
```

```python
import functools
import math

import jax
import jax.numpy as jnp
from jax import lax
from jax.experimental import pallas as pl
from jax.experimental.pallas import tpu as pltpu
from jax.experimental.pallas import tpu_sc as plsc

F32 = jnp.float32
BF16 = jnp.bfloat16
HIGHEST = lax.Precision.HIGHEST

HEAD_DIM = 64
MOBA_BLOCK = 256
MOBA_TOPK = 3
REL_MAX_DIST = 128
TOP_K = 8
N_GROUPS = 8
TOPK_GROUPS = 4
ROUTED_SCALE = 2.5
LN_EPS = 1e-5
NEG = -1e30

LANES = 128
HEADS_PER_TILE = LANES // HEAD_DIM
EXPERT_ROWS = 512
VMEM_LIMIT = 48 * 1024 * 1024


def _cparams(*sem):
    return pltpu.CompilerParams(dimension_semantics=sem, vmem_limit_bytes=VMEM_LIMIT)


def _nt_dot(a, b, **kw):
    return lax.dot_general(a, b, (((1,), (1,)), ((), ())), preferred_element_type=F32, **kw)


def _split_bf16(x, pieces):
    out = []
    for _ in range(pieces):
        out.append(x.astype(BF16))
        x = x - out[-1].astype(F32)
    return out


def _three_pass(x, w_ref, mul):
    hi, lo = _split_bf16(x, 2)
    return mul(hi, w_ref[0]) + mul(lo, w_ref[0]) + mul(hi, w_ref[1])


def _proj_kernel(x_ref, w_ref, o_ref):
    o_ref[...] = jnp.dot(x_ref[...].astype(BF16), w_ref[...],
                         preferred_element_type=F32).astype(o_ref.dtype)


def _project(x2d, w, tm, tn):
    T, K = x2d.shape
    N = w.shape[1]
    return pl.pallas_call(
        _proj_kernel,
        out_shape=jax.ShapeDtypeStruct((T, N), BF16),
        grid=(T // tm, N // tn),
        in_specs=[pl.BlockSpec((tm, K), lambda i, j: (i, 0)),
                  pl.BlockSpec((K, tn), lambda i, j: (0, j))],
        out_specs=pl.BlockSpec((tm, tn), lambda i, j: (i, j)),
        compiler_params=_cparams("parallel", "arbitrary"),
        name="k_proj",
    )(x2d, w)


def _proj_t_kernel(x_ref, wt_ref, o_ref):
    o_ref[...] = _nt_dot(wt_ref[...], x_ref[...].astype(BF16)).astype(o_ref.dtype)


def _project_t(x2d, wt, tm, tn):
    T, K = x2d.shape
    N = wt.shape[0]
    return pl.pallas_call(
        _proj_t_kernel,
        out_shape=jax.ShapeDtypeStruct((N, T), BF16),
        grid=(T // tm, N // tn),
        in_specs=[pl.BlockSpec((tm, K), lambda i, j: (i, 0)),
                  pl.BlockSpec((tn, K), lambda i, j: (j, 0))],
        out_specs=pl.BlockSpec((tn, tm), lambda i, j: (j, i)),
        compiler_params=_cparams("parallel", "arbitrary"),
        name="qv_proj_t",
    )(x2d, wt)


def _layer_norm(r, g, b):
    mu = jnp.mean(r, axis=-1, keepdims=True)
    d = r - mu
    var = jnp.mean(d * d, axis=-1, keepdims=True)
    return d * lax.rsqrt(var + LN_EPS) * g + b


def _pack_halves(y):
    n = y.shape[1] // 2
    bits = lambda v: lax.bitcast_convert_type(v.astype(BF16).astype(F32), jnp.int32)
    left = bits(y[:, :n])
    return lax.shift_right_logical(left, jnp.full_like(left, 16)) | bits(y[:, n:])


def _unpack_halves(p):
    return (lax.bitcast_convert_type(p << 16, F32),
            lax.bitcast_convert_type(p & jnp.int32(-65536), F32))


def _outproj_ln_kernel(at_ref, w_ref, x_ref, g_ref, b_ref, o_ref, packed_ref, *, alpha):
    y = lax.dot_general(at_ref[...], w_ref[...], (((0,), (0,)), ((), ())),
                        preferred_element_type=F32)
    out = _layer_norm(alpha * x_ref[...] + y, g_ref[...], b_ref[...])
    o_ref[...] = out
    packed_ref[...] = _pack_halves(out)


def _outproj_ln(attn_t, w_out, x2d, g, b, alpha, tm):
    T, D = x2d.shape
    row = lambda i: (i, 0)
    fixed = lambda i: (0, 0)
    return pl.pallas_call(
        functools.partial(_outproj_ln_kernel, alpha=alpha),
        out_shape=(jax.ShapeDtypeStruct((T, D), F32), jax.ShapeDtypeStruct((T, D // 2), jnp.int32)),
        grid=(T // tm,),
        in_specs=[pl.BlockSpec((D, tm), lambda i: (0, i)), pl.BlockSpec((D, D), fixed),
                  pl.BlockSpec((tm, D), row), pl.BlockSpec((1, D), fixed),
                  pl.BlockSpec((1, D), fixed)],
        out_specs=(pl.BlockSpec((tm, D), row), pl.BlockSpec((tm, D // 2), row)),
        compiler_params=_cparams("parallel"),
        name="outproj_ln",
    )(attn_t, w_out, x2d, g.reshape(1, D), b.reshape(1, D))


def _spare(a, n=0):
    return ((a + 1) % HEADS_PER_TILE) * HEAD_DIM + n


def _scores(chains):
    return [[jnp.dot(keys(), query(), preferred_element_type=F32) for keys, query, _ in blocks]
            for blocks in chains]


def _absorb(carries, scores, chains, masks=None, adds=None):
    probs = []
    for c, ((m, _), parts) in enumerate(zip(carries, scores)):
        if adds is not None:
            parts = [s if add is None else s + add() for add, s in zip(adds[c], parts)]
        if masks is not None:
            parts = [s if keep is None else jnp.where(keep, s, NEG)
                     for keep, s in zip(masks[c], parts)]
        tops = [jnp.max(s, axis=0, keepdims=True) for s in parts]
        m_new = functools.reduce(jnp.maximum, tops, m)
        probs.append((m_new, jnp.exp(m - m_new), [jnp.exp(s - m_new).astype(BF16) for s in parts]))
    out = []
    for (_, acc), (m_new, decay, ps), blocks in zip(carries, probs, chains):
        acc = decay * acc
        for (_, _, values), p in zip(blocks, ps):
            acc = acc + jnp.dot(values(), p, preferred_element_type=F32)
        out.append((m_new, acc))
    return tuple(out)


def _stash(chains, ref):
    for a, parts in enumerate(_scores(chains)):
        for b, s in enumerate(parts):
            ref[a, b] = s


def _fetch(chains, ref):
    return [[ref[a, b] for b in range(len(blocks))] for a, blocks in enumerate(chains)]


def _attend_masked_last(carries, n, group, stage_refs, masks, adds=None):
    first, second = stage_refs
    _stash(group(0), first)
    trips = n // 2

    def body(t, carries):
        g = 2 * t
        _stash(group(g + 1), second)
        carries = _absorb(carries, _fetch(group(g), first), group(g))
        _stash(group(g + 2), first)
        return _absorb(carries, _fetch(group(g + 1), second), group(g + 1))

    carries = lax.fori_loop(0, trips, body, carries)

    def odd_tail(carries):
        _stash(group(n), second)
        carries = _absorb(carries, _fetch(group(n - 1), first), group(n - 1))
        return _absorb(carries, _fetch(group(n), second), group(n), masks, adds)

    def even_tail(carries):
        return _absorb(carries, _fetch(group(n), first), group(n), masks, adds)

    return lax.cond(n % 2 == 1, odd_tail, even_tail, carries)


GROUP_BLOCKS = 2


def _stage_scratch(tq, chains=HEADS_PER_TILE):
    return [pltpu.VMEM((chains, GROUP_BLOCKS, tq, tq), F32) for _ in range(2)]


def _scores_init(tq):
    return jnp.full((1, tq), -jnp.inf, F32), jnp.zeros((LANES, tq), F32)


def _finish(carries, o_ref, cols=slice(None)):
    row = lax.broadcasted_iota(jnp.int32, carries[0][1].shape, 0)
    outs = []
    for a, (_, acc) in enumerate(carries):
        s = _spare(a)
        outs.append(acc / acc[s:s + 1, :])
    o_ref[:, cols] = jnp.where(row < HEAD_DIM, outs[0], outs[1]).astype(o_ref.dtype)


def _fill_v_aug(vt_ref, vaug_ref, tk):
    n_tiles = vt_ref.shape[1] // tk
    row = lax.broadcasted_iota(jnp.int32, (LANES, tk), 0)
    for n in range(n_tiles):
        v = vt_ref[:, n * tk:(n + 1) * tk]
        for a in range(HEADS_PER_TILE):
            vaug_ref[a, n] = jnp.where(row // HEAD_DIM == a, v, jnp.ones_like(v))


FOX_SPLIT = 3


def _fox_kernel(qt_ref, k_ref, vt_ref, o_ref, vaug_ref, stage0_ref, stage1_ref, *, tq):
    n = pl.program_id(2)

    @pl.when(n == 0)
    def _():
        _fill_v_aug(vt_ref, vaug_ref, tq)

    chains = [(h, a) for h in range(GROUP_BLOCKS) for a in range(HEADS_PER_TILE)]
    row = lax.broadcasted_iota(jnp.int32, (LANES, tq), 0)
    queries = {}
    for h, a in chains:
        offs = (row >= _spare(a)) & (row < _spare(a, FOX_SPLIT))
        queries[h, a] = jnp.where(row // HEAD_DIM == a, qt_ref[:, h * tq:(h + 1) * tq],
                                  jnp.where(offs, -1.0, 0.0).astype(BF16))

    def block(h, a, j):
        start = pl.multiple_of(j * tq, tq)
        return (lambda: k_ref[a, 0, pl.ds(start, tq), :], lambda: queries[h, a],
                lambda: vaug_ref[a, j])

    key = lax.broadcasted_iota(jnp.int32, (tq, tq), 0)
    qry = lax.broadcasted_iota(jnp.int32, (tq, tq), 1)
    masks = [[None if b < h else key + (b - h) * tq <= qry for b in range(GROUP_BLOCKS)]
             for h, _ in chains]
    carries = _attend_masked_last(
        tuple(_scores_init(tq) for _ in chains), n,
        lambda j: [[block(h, a, 2 * j), block(h, a, 2 * j + 1)] for h, a in chains],
        (stage0_ref, stage1_ref), masks)
    for h in range(GROUP_BLOCKS):
        _finish(carries[h * HEADS_PER_TILE:(h + 1) * HEADS_PER_TILE], o_ref,
                slice(h * tq, (h + 1) * tq))


def _fox_attention(qv_t, k_aug, B, S, tq):
    D = k_aug.shape[-1]
    n_tiles = D // LANES
    nq = S // tq
    span = GROUP_BLOCKS * tq
    steps = S // span
    return pl.pallas_call(
        functools.partial(_fox_kernel, tq=tq),
        out_shape=jax.ShapeDtypeStruct((D, B * S), BF16),
        grid=(B, n_tiles, steps),
        in_specs=[pl.BlockSpec((LANES, span), lambda b, h, i: (h, b * steps + i)),
                  pl.BlockSpec((HEADS_PER_TILE, 1, S, LANES), lambda b, h, i: (0, b, 0, h)),
                  pl.BlockSpec((LANES, S), lambda b, h, i: (n_tiles + h, b))],
        out_specs=pl.BlockSpec((LANES, span), lambda b, h, i: (h, b * steps + i)),
        scratch_shapes=[pltpu.VMEM((HEADS_PER_TILE, nq, LANES, tq), BF16)]
        + _stage_scratch(tq, GROUP_BLOCKS * HEADS_PER_TILE),
        compiler_params=_cparams("parallel", "parallel", "arbitrary"),
        name="fox_attention",
    )(qv_t, k_aug, qv_t)


def _log_sigmoid(z):
    return jnp.minimum(z, 0.0) - jnp.log1p(jnp.exp(-jnp.abs(z)))


def _fox_gates_kernel(x_ref, w_ref, b_ref, o_ref, carry_ref, *, ts):
    @pl.when(pl.program_id(1) == 0)
    def _():
        carry_ref[...] = jnp.zeros_like(carry_ref)

    z = _three_pass(x_ref[0], w_ref, lambda x, w: jnp.dot(x, w, preferred_element_type=F32))
    lf = _log_sigmoid(z + b_ref[...])
    r = lax.broadcasted_iota(jnp.int32, (ts, ts), 0)
    c = lax.broadcasted_iota(jnp.int32, (ts, ts), 1)
    tri = (c <= r).astype(BF16)
    cum = carry_ref[...]
    for piece in _split_bf16(lf, 3):
        cum = cum + jnp.dot(tri, piece, preferred_element_type=F32)
    carry_ref[...] = cum[ts - 1:ts, :]
    o_ref[0] = cum


def _fox_gates(x3d, w_f, b_f, ts):
    B, S, D = x3d.shape
    H = w_f.shape[1]
    w_pad = jnp.stack(_split_bf16(jnp.zeros((D, LANES), F32).at[:, :H].set(w_f), 2))
    b_pad = jnp.zeros((1, LANES), F32).at[0, :H].set(b_f)
    return pl.pallas_call(
        functools.partial(_fox_gates_kernel, ts=ts),
        out_shape=jax.ShapeDtypeStruct((B, S, LANES), F32),
        grid=(B, S // ts),
        in_specs=[pl.BlockSpec((1, ts, D), lambda b, s: (b, s, 0)),
                  pl.BlockSpec((2, D, LANES), lambda b, s: (0, 0, 0)),
                  pl.BlockSpec((1, LANES), lambda b, s: (0, 0))],
        out_specs=pl.BlockSpec((1, ts, LANES), lambda b, s: (b, s, 0)),
        scratch_shapes=[pltpu.VMEM((1, LANES), F32)],
        compiler_params=_cparams("parallel", "arbitrary"),
        name="fox_gates",
    )(x3d, w_pad, b_pad)


def _fox_kprep_kernel(k_ref, c_ref, o_ref):
    tm, D = k_ref.shape
    lane = lax.broadcasted_iota(jnp.int32, (tm, LANES), 1)
    for t in range(D // LANES):
        k = k_ref[:, t * LANES:(t + 1) * LANES].astype(F32)
        for a in range(HEADS_PER_TILE):
            h = t * HEADS_PER_TILE + a
            rest = c_ref[:, h:h + 1]
            aug = jnp.zeros((tm, LANES), F32)
            for n in range(FOX_SPLIT):
                piece = rest.astype(BF16).astype(F32)
                rest = rest - piece
                aug = jnp.where(lane == _spare(a, n), piece, aug)
            o_ref[a, :, t * LANES:(t + 1) * LANES] = jnp.where(lane // HEAD_DIM == a, k,
                                                               aug).astype(BF16)


def _fox_kprep(k2d, c_tok, tm):
    T, D = k2d.shape
    return pl.pallas_call(
        _fox_kprep_kernel,
        out_shape=jax.ShapeDtypeStruct((HEADS_PER_TILE, T, D), BF16),
        grid=(T // tm,),
        in_specs=[pl.BlockSpec((tm, D), lambda i: (i, 0)),
                  pl.BlockSpec((tm, LANES), lambda i: (i, 0))],
        out_specs=pl.BlockSpec((HEADS_PER_TILE, tm, D), lambda i: (0, i, 0)),
        compiler_params=_cparams("parallel"),
        name="fox_kprep",
    )(k2d, c_tok)


MOBA_SLAB = 16


def _moba_kernel(bfar_ref, qt_ref, k_ref, vt_ref, kmean_ref, bown_ref, badj_ref, o_ref,
                 vaug_ref, sel_ref, stage0_ref, stage1_ref, *, blk):
    t = pl.program_id(1)
    n = pl.program_id(2)
    nb = kmean_ref.shape[1]

    @pl.when(n == 0)
    def _():
        _fill_v_aug(vt_ref, vaug_ref, blk)

    chains = [(h, a) for h in range(GROUP_BLOCKS) for a in range(HEADS_PER_TILE)]
    row = lax.broadcasted_iota(jnp.int32, (LANES, blk), 0)
    brow = lax.broadcasted_iota(jnp.int32, (nb, blk), 0)
    base = {}
    for c, (h, a) in enumerate(chains):
        own = GROUP_BLOCKS * n + h
        qt = qt_ref[:, h * blk:(h + 1) * blk]
        qa = jnp.where(row // HEAD_DIM == a, qt, jnp.zeros_like(qt))
        base[h, a] = qa
        bscore = jnp.dot(kmean_ref[0], qa.astype(F32), preferred_element_type=F32,
                         precision=HIGHEST)
        cand = jnp.where(brow < own, bscore, NEG)
        sel = jnp.zeros((nb, blk), F32)
        for _ in range(MOBA_TOPK):
            mx = jnp.max(cand, axis=0, keepdims=True)
            first = jnp.min(jnp.where(cand == mx, brow, nb), axis=0, keepdims=True)
            hit = brow == first
            sel = jnp.where(hit, 1.0, sel)
            cand = jnp.where(hit, -jnp.inf, cand)
        sel_ref[c] = jnp.where(brow < own, sel, 0.0)

    srow = lax.broadcasted_iota(jnp.int32, (MOBA_SLAB, blk), 0)

    def block(c, j):
        h, a = chains[c]
        head_id = t * HEADS_PER_TILE + a
        jc = jnp.maximum(j, 0)
        start = pl.multiple_of(jc * blk, blk)

        def query():
            s0 = _spare(a) // MOBA_SLAB * MOBA_SLAB
            chosen = (((sel_ref[c, pl.ds(jc, 1), :] > 0.0) & (j >= 0))
                      | (j == GROUP_BLOCKS * n + h))
            slab = jnp.where(srow == _spare(a) - s0,
                             jnp.where(chosen, bfar_ref[0, head_id], NEG),
                             jnp.where(srow == _spare(a, 1) - s0, bfar_ref[1, head_id], 0.0)
                             ).astype(BF16)
            q = base[h, a]
            head = [q[:s0]] if s0 else []
            tail = [q[s0 + MOBA_SLAB:]] if s0 + MOBA_SLAB < LANES else []
            return jnp.concatenate(head + [slab] + tail, axis=0)

        return lambda: k_ref[a, 0, pl.ds(start, blk), :], query, lambda: vaug_ref[a, jc]

    key = lax.broadcasted_iota(jnp.int32, (blk, blk), 0)
    qry = lax.broadcasted_iota(jnp.int32, (blk, blk), 1)
    carries = _attend_masked_last(
        tuple(_scores_init(blk) for _ in chains), n,
        lambda g: [[block(c, 2 * g + h - 1), block(c, 2 * g + h)] for c, (h, _) in enumerate(chains)],
        (stage0_ref, stage1_ref), [[None, key <= qry] for _ in chains],
        [[lambda a=a: badj_ref[a], lambda a=a: bown_ref[a]] for _, a in chains])
    for h in range(GROUP_BLOCKS):
        _finish(carries[h * HEADS_PER_TILE:(h + 1) * HEADS_PER_TILE], o_ref,
                slice(h * blk, (h + 1) * blk))


def _rel_bucket(dist, n_buckets):
    n = jnp.maximum(dist, 0)
    max_exact = n_buckets // 2
    nf = jnp.maximum(n, 1).astype(F32)
    large = max_exact + (jnp.log(nf / max_exact) / math.log(REL_MAX_DIST / max_exact)
                         * (n_buckets - max_exact)).astype(jnp.int32)
    large = jnp.minimum(large, n_buckets - 1)
    return jnp.where(n < max_exact, n, large)


def _moba_attention(qv_t, k_aug, k_mean, rel_bias, B, S):
    D = k_aug.shape[-1]
    n_tiles = D // LANES
    blk = MOBA_BLOCK
    nb = S // blk
    hp = HEADS_PER_TILE
    span = GROUP_BLOCKS * blk
    steps = S // span
    n_chains = GROUP_BLOCKS * hp
    r = jnp.arange(blk)
    delta = r[None, :] - r[:, None]
    bias_t = rel_bias.T.astype(F32)
    n_buckets = rel_bias.shape[0]
    def table(dist):
        onehot = jax.nn.one_hot(_rel_bucket(dist, n_buckets), n_buckets, dtype=F32)
        return jnp.einsum('crn,hn->hcr', onehot, bias_t, precision=HIGHEST)

    b_far = bias_t[:, n_buckets - 1]
    b_own = table(delta) - b_far[:, None, None]
    b_adj = table(delta + blk) - b_far[:, None, None]
    far_hi = b_far.astype(BF16).astype(F32)
    far_lo = (b_far - far_hi).astype(BF16).astype(F32)
    b_far2 = jnp.stack([far_hi, far_lo])
    return pl.pallas_call(
        functools.partial(_moba_kernel, blk=blk),
        out_shape=jax.ShapeDtypeStruct((D, B * S), BF16),
        grid_spec=pltpu.PrefetchScalarGridSpec(
            num_scalar_prefetch=1,
            grid=(B, n_tiles, steps),
            in_specs=[pl.BlockSpec((LANES, span), lambda b, h, i, f: (h, b * steps + i)),
                      pl.BlockSpec((hp, 1, S, LANES), lambda b, h, i, f: (0, b, 0, h)),
                      pl.BlockSpec((LANES, S), lambda b, h, i, f: (n_tiles + h, b)),
                      pl.BlockSpec((1, nb, LANES), lambda b, h, i, f: (b, 0, h)),
                      pl.BlockSpec((hp, blk, blk), lambda b, h, i, f: (h, 0, 0)),
                      pl.BlockSpec((hp, blk, blk), lambda b, h, i, f: (h, 0, 0))],
            out_specs=pl.BlockSpec((LANES, span), lambda b, h, i, f: (h, b * steps + i)),
            scratch_shapes=[pltpu.VMEM((hp, nb, LANES, blk), BF16),
                            pltpu.VMEM((n_chains, nb, blk), F32)] + _stage_scratch(blk, n_chains)),
        compiler_params=_cparams("parallel", "parallel", "arbitrary"),
        name="moba_attention",
    )(b_far2, qv_t, k_aug, qv_t, k_mean, b_own, b_adj)


def _moba_kprep_kernel(k_ref, o_ref, mean_ref):
    k = k_ref[0]
    col = lax.broadcasted_iota(jnp.int32, k.shape, 1)
    lane = col % LANES
    for a in range(HEADS_PER_TILE):
        ones = (lane == _spare(a)) | (lane == _spare(a, 1))
        o_ref[a, 0] = jnp.where((col // HEAD_DIM) % HEADS_PER_TILE == a, k,
                                jnp.where(ones, 1.0, 0.0).astype(BF16))
    mean_ref[0, 0] = jnp.mean(k.astype(F32), axis=0, keepdims=True)


def _moba_kprep(k3d):
    B, S, D = k3d.shape
    blk = MOBA_BLOCK
    nb = S // blk
    k_aug, k_mean = pl.pallas_call(
        _moba_kprep_kernel,
        out_shape=(jax.ShapeDtypeStruct((HEADS_PER_TILE, B, S, D), BF16),
                   jax.ShapeDtypeStruct((B, nb, 1, D), F32)),
        grid=(B, nb),
        in_specs=[pl.BlockSpec((1, blk, D), lambda b, n: (b, n, 0))],
        out_specs=(pl.BlockSpec((HEADS_PER_TILE, 1, blk, D), lambda b, n: (0, b, n, 0)),
                   pl.BlockSpec((1, 1, 1, D), lambda b, n: (b, n, 0, 0))),
        compiler_params=_cparams("parallel", "parallel"),
        name="moba_kprep",
    )(k3d)
    return k_aug, k_mean.reshape(B, nb, D)


def _router_kernel(x_ref, wt_ref, bias_ref, idx_ref, gate_ref, rank_ref, cnt_ref, run_ref, *, tm):
    @pl.when(pl.program_id(0) == 0)
    def _():
        run_ref[...] = jnp.zeros_like(run_ref)

    E = wt_ref.shape[1]
    gsz = E // N_GROUPS
    logits = _three_pass(x_ref[...], wt_ref, lambda x, w: _nt_dot(w, x))
    s = jax.nn.sigmoid(logits)
    sb = s + bias_ref[...]
    neg_inf = -jnp.inf

    giota = lax.broadcasted_iota(jnp.int32, (gsz, tm), 0)
    gscore = []
    for g in range(N_GROUPS):
        blk = sb[g * gsz:(g + 1) * gsz, :]
        m1 = jnp.max(blk, axis=0, keepdims=True)
        i1 = jnp.min(jnp.where(blk == m1, giota, gsz), axis=0, keepdims=True)
        m2 = jnp.max(jnp.where(giota == i1, neg_inf, blk), axis=0, keepdims=True)
        gscore.append(m1 + m2)

    gsel = [jnp.zeros((1, tm), jnp.bool_) for _ in range(N_GROUPS)]
    for _ in range(TOPK_GROUPS):
        mx = functools.reduce(jnp.maximum, gscore)
        found = jnp.zeros((1, tm), jnp.bool_)
        for g in range(N_GROUPS):
            hit = (gscore[g] == mx) & jnp.logical_not(found)
            gsel[g] = gsel[g] | hit
            found = found | hit
            gscore[g] = jnp.where(hit, neg_inf, gscore[g])
    emask = jnp.concatenate([jnp.broadcast_to(gsel[g], (gsz, tm)) for g in range(N_GROUPS)], axis=0)
    cand = jnp.where(emask, sb, NEG)

    eiota = lax.broadcasted_iota(jnp.int32, (E, tm), 0)
    hits, idxs, ws = [], [], []
    for _ in range(TOP_K):
        mx = jnp.max(cand, axis=0, keepdims=True)
        first = jnp.min(jnp.where(cand == mx, eiota, E), axis=0, keepdims=True)
        hit = eiota == first
        hits.append(hit)
        idxs.append(first)
        ws.append(jnp.sum(jnp.where(hit, s, 0.0), axis=0, keepdims=True))
        cand = jnp.where(hit, neg_inf, cand)
    wsum = functools.reduce(jnp.add, ws)

    chosen = functools.reduce(jnp.logical_or, hits)
    onehot = jnp.where(chosen, 1.0, 0.0)
    tr = lax.broadcasted_iota(jnp.int32, (tm, tm), 0)
    tc = lax.broadcasted_iota(jnp.int32, (tm, tm), 1)
    before = (tr < tc).astype(BF16)
    prior = jnp.dot(onehot.astype(BF16), before, preferred_element_type=F32) + run_ref[...]
    grow = lax.broadcasted_iota(jnp.int32, (LANES, tm), 0)
    gates = jnp.zeros((LANES, tm), F32)
    for k in range(TOP_K):
        idx_ref[k:k + 1, :] = idxs[k]
        gates = jnp.where(grow == k, ws[k] / wsum * ROUTED_SCALE, gates)
        rank_ref[k:k + 1, :] = jnp.sum(jnp.where(hits[k], prior, 0.0), axis=0,
                                       keepdims=True).astype(jnp.int32)
    gate_ref[...] = gates.T
    run_ref[...] = run_ref[...] + jnp.sum(onehot, axis=1, keepdims=True)
    cnt_ref[...] = run_ref[...]


def _router(x2d, w_router, router_bias, tm):
    T, D = x2d.shape
    E = w_router.shape[1]
    tok = lambda i: (0, i)
    fixed = lambda i: (0, 0)
    return pl.pallas_call(
        functools.partial(_router_kernel, tm=tm),
        out_shape=(jax.ShapeDtypeStruct((TOP_K, T), jnp.int32),
                   jax.ShapeDtypeStruct((T, LANES), F32),
                   jax.ShapeDtypeStruct((TOP_K, T), jnp.int32),
                   jax.ShapeDtypeStruct((E, 1), F32)),
        grid=(T // tm,),
        in_specs=[pl.BlockSpec((tm, D), lambda i: (i, 0)),
                  pl.BlockSpec((2, E, D), lambda i: (0, 0, 0)),
                  pl.BlockSpec((E, 1), fixed)],
        out_specs=(pl.BlockSpec((TOP_K, tm), tok), pl.BlockSpec((tm, LANES), lambda i: (i, 0)),
                   pl.BlockSpec((TOP_K, tm), tok), pl.BlockSpec((E, 1), fixed)),
        scratch_shapes=[pltpu.VMEM((E, 1), F32)],
        compiler_params=_cparams("arbitrary"),
        name="router",
    )(x2d, jnp.stack(_split_bf16(w_router.T, 2)), router_bias.reshape(E, 1).astype(F32))


def _slots_kernel(idx_ref, rank_ref, start_ref, o_ref):
    E = start_ref.shape[0]
    tm = idx_ref.shape[1]
    eiota = lax.broadcasted_iota(jnp.int32, (E, tm), 0)
    start = start_ref[...]
    for k in range(TOP_K):
        base = jnp.sum(jnp.where(eiota == idx_ref[k:k + 1, :], start, 0.0), axis=0, keepdims=True)
        o_ref[k:k + 1, :] = base.astype(jnp.int32) + rank_ref[k:k + 1, :]


def _slots(idx, rank, seg_start, tm):
    T = idx.shape[1]
    E = seg_start.shape[0]
    tok = lambda i: (0, i)
    return pl.pallas_call(
        _slots_kernel,
        out_shape=jax.ShapeDtypeStruct((TOP_K, T), jnp.int32),
        grid=(T // tm,),
        in_specs=[pl.BlockSpec((TOP_K, tm), tok), pl.BlockSpec((TOP_K, tm), tok),
                  pl.BlockSpec((E, 1), lambda i: (0, 0))],
        out_specs=pl.BlockSpec((TOP_K, tm), tok),
        compiler_params=_cparams("parallel"),
        name="moe_slots",
    )(idx, rank, seg_start.astype(F32).reshape(E, 1))


SC_CORES = 2
SC_SUBCORES = 16
SC_CHUNK = 128


def _sc_rows_kernel(body, out_shape, n_chunks, width, dtype, name):
    mesh = plsc.VectorSubcoreMesh(core_axis_name="c", subcore_axis_name="s",
                                  num_cores=SC_CORES, num_subcores=SC_SUBCORES)
    return pl.kernel(
        body, mesh=mesh, out_type=jax.ShapeDtypeStruct(out_shape, dtype),
        scratch_types=[pltpu.VMEM((n_chunks, SC_CHUNK), jnp.int32),
                       pltpu.VMEM((SC_CHUNK, width), dtype),
                       pltpu.SemaphoreType.DMA],
        name=name)


def _sc_chunks(n):
    workers = SC_CORES * SC_SUBCORES
    chunks = n // (workers * SC_CHUNK)
    assert chunks * workers * SC_CHUNK == n
    return workers, chunks


def _sc_gather_rows(table, idx):
    workers, chunks = _sc_chunks(idx.shape[0])

    def body(table_hbm, idx_hbm, out_hbm, idx_v, rows_v, sem):
        wid = lax.axis_index("s") * SC_CORES + lax.axis_index("c")
        pltpu.sync_copy(idx_hbm.at[wid], idx_v)

        @pl.loop(0, chunks)
        def _(j):
            pltpu.async_copy(table_hbm.at[idx_v.at[j]], rows_v, sem).wait()
            pltpu.sync_copy(rows_v, out_hbm.at[pl.ds((wid * chunks + j) * SC_CHUNK, SC_CHUNK)])

    call = _sc_rows_kernel(body, (idx.shape[0], table.shape[1]), chunks, table.shape[1],
                           table.dtype, "moe_gather_sc")
    return call(table, idx.reshape(workers, chunks, SC_CHUNK))


def _sc_scatter_rows(src, idx, n_rows):
    n_src = src.shape[0]
    assert n_src % SC_CHUNK == 0
    workers, chunks = _sc_chunks(idx.shape[0])

    def body(src_hbm, idx_hbm, out_hbm, idx_v, rows_v, sem):
        wid = lax.axis_index("s") * SC_CORES + lax.axis_index("c")
        pltpu.sync_copy(idx_hbm.at[wid], idx_v)

        @pl.loop(0, chunks)
        def _(j):
            first = lax.rem((wid * chunks + j) * SC_CHUNK, n_src)
            pltpu.sync_copy(src_hbm.at[pl.ds(first, SC_CHUNK)], rows_v)
            pltpu.async_copy(rows_v, out_hbm.at[idx_v.at[j]], sem).wait()

    call = _sc_rows_kernel(body, (n_rows, src.shape[1]), chunks, src.shape[1], src.dtype,
                           "moe_scatter_sc")
    return call(src, idx.reshape(workers, chunks, SC_CHUNK))


def _silu(g):
    return g * jax.nn.sigmoid(g)


SCHED_FIRST, SCHED_SLOT, SCHED_NEXT, SCHED_VALID = 0, 1, 2, 3


def _expert_kernel(sched_ref, n_used_ref, x_ref, wg_hbm, wu_hbm, wd_hbm, o_ref,
                   wg_buf, wu_buf, wd_buf, sem, *, layer, first_expert_row):
    i = pl.program_id(0)
    slot = sched_ref[SCHED_SLOT, i]

    def copies(expert, slot):
        pairs = ((wg_hbm, wg_buf), (wu_hbm, wu_buf), (wd_hbm, wd_buf))
        return [pltpu.make_async_copy(src.at[layer, expert], dst.at[slot], sem.at[slot, n])
                for n, (src, dst) in enumerate(pairs)]

    @pl.when(i == 0)
    def _():
        for c in copies(sched_ref[first_expert_row, 0], 0):
            c.start()

    @pl.when(sched_ref[SCHED_FIRST, i] == 1)
    def _():
        for c in copies(0, slot):
            c.wait()

        @pl.when(sched_ref[SCHED_NEXT, i] >= 0)
        def _():
            for c in copies(sched_ref[SCHED_NEXT, i], 1 - slot):
                c.start()

    @pl.when(i < n_used_ref[0])
    def _():
        words = x_ref[...]
        row = lax.broadcasted_iota(jnp.int32, words.shape, 0)
        words = jnp.where(row < sched_ref[SCHED_VALID, i], words, 0)
        x = jnp.concatenate(_unpack_halves(words), axis=1).astype(BF16)
        g = jnp.dot(x, wg_buf[slot].astype(BF16), preferred_element_type=F32)
        u = jnp.dot(x, wu_buf[slot].astype(BF16), preferred_element_type=F32)
        a = (_silu(g) * u).astype(BF16)
        o_ref[...] = _pack_halves(jnp.dot(a, wd_buf[slot].astype(BF16), preferred_element_type=F32))

    @pl.when(i >= n_used_ref[0])
    def _():
        o_ref[...] = jnp.zeros_like(o_ref)


def _expert_schedule(seg_start, seg_end, counts, n_blocks):
    E = seg_end.shape[0]
    first_row = jnp.arange(n_blocks, dtype=jnp.int32) * EXPERT_ROWS
    blk_e = jnp.minimum(jnp.sum(seg_end[None, :] <= first_row[:, None], axis=1), E - 1)
    used = first_row < seg_end[-1]
    first = used & (first_row == seg_start[blk_e])
    ordinal = jnp.cumsum(first) - 1
    ids = jnp.where(seg_end > seg_start, jnp.arange(E), E)
    later = lax.cummin(ids, reverse=True)
    nxt = jnp.concatenate([later[1:], jnp.full((1,), E, later.dtype)])[blk_e]
    nxt = jnp.where(nxt < E, nxt, -1)
    valid = jnp.clip((seg_start + counts)[blk_e] - first_row, 0, EXPERT_ROWS)
    return jnp.stack([first, ordinal % 2, nxt, valid, blk_e]).astype(jnp.int32)


def _expert_mlp(xs, sched, n_used, w_gate, w_up, w_down, layer):
    n_rows, half = xs.shape
    D = 2 * half
    F = w_gate.shape[-1]
    tm = EXPERT_ROWS
    hbm = pl.BlockSpec(memory_space=pl.ANY)
    return pl.pallas_call(
        functools.partial(_expert_kernel, layer=layer, first_expert_row=sched.shape[0] - 1),
        out_shape=jax.ShapeDtypeStruct((n_rows, half), jnp.int32),
        grid_spec=pltpu.PrefetchScalarGridSpec(
            num_scalar_prefetch=2,
            grid=(n_rows // tm,),
            in_specs=[pl.BlockSpec((tm, half), lambda i, sc, nu: (jnp.minimum(i, nu[0] - 1), 0)),
                      hbm, hbm, hbm],
            out_specs=pl.BlockSpec((tm, half), lambda i, sc, nu: (i, 0)),
            scratch_shapes=[pltpu.VMEM((2, D, F), F32), pltpu.VMEM((2, D, F), F32),
                            pltpu.VMEM((2, F, D), F32), pltpu.SemaphoreType.DMA((2, 3))]),
        compiler_params=_cparams("arbitrary"),
        name="moe_experts",
    )(sched, n_used, xs, w_gate, w_up, w_down)


def _combine_kernel(x_ref, gate_ref, rows_ref, wsg_ref, wsu_ref, wsd_ref, g_ref, b_ref, o_ref,
                    *, alpha):
    x = x_ref[...]
    xb = x.astype(BF16)
    g = jnp.dot(xb, wsg_ref[...], preferred_element_type=F32)
    u = jnp.dot(xb, wsu_ref[...], preferred_element_type=F32)
    y = jnp.dot((_silu(g) * u).astype(BF16), wsd_ref[...], preferred_element_type=F32)
    gates = gate_ref[...]
    half = y.shape[1] // 2
    left, right = y[:, :half], y[:, half:]
    for k in range(TOP_K):
        lo, hi = _unpack_halves(rows_ref[k])
        left = left + lo * gates[:, k:k + 1]
        right = right + hi * gates[:, k:k + 1]
    y = jnp.concatenate([left, right], axis=1)
    o_ref[...] = _layer_norm(alpha * x + y, g_ref[...], b_ref[...])


def _combine(x2d, ys, dest, gates_tok, ws_gate, ws_up, ws_down, g, b, alpha, tt):
    T, D = x2d.shape
    F = ws_gate.shape[1]
    row = lambda i: (i, 0)
    fixed = lambda i: (0, 0)
    rows = _sc_gather_rows(ys, dest.reshape(-1)).reshape(TOP_K, T, D // 2)
    return pl.pallas_call(
        functools.partial(_combine_kernel, alpha=alpha),
        out_shape=jax.ShapeDtypeStruct((T, D), F32),
        grid=(T // tt,),
        in_specs=[pl.BlockSpec((tt, D), row),
                  pl.BlockSpec((tt, LANES), row),
                  pl.BlockSpec((TOP_K, tt, D // 2), lambda i: (0, i, 0)),
                  pl.BlockSpec((D, F), fixed), pl.BlockSpec((D, F), fixed),
                  pl.BlockSpec((F, D), fixed),
                  pl.BlockSpec((1, D), fixed), pl.BlockSpec((1, D), fixed)],
        out_specs=pl.BlockSpec((tt, D), row),
        compiler_params=_cparams("parallel"),
        name="moe_combine",
    )(x2d, gates_tok, rows, ws_gate.astype(BF16), ws_up.astype(BF16), ws_down.astype(BF16),
      g.reshape(1, D), b.reshape(1, D))


def _pick(n, pref):
    t = min(n, pref)
    while n % t:
        t //= 2
    return t


def _moe_layer(x2d, packed, w_router, router_bias, w_gate, w_up, w_down, layer, ws_gate, ws_up,
               ws_down, g, b, alpha):
    T, D = x2d.shape
    E = w_router.shape[1]
    idx, gates, rank, counts = _router(x2d, w_router, router_bias, _pick(T, 256))
    counts = counts[:, 0].astype(jnp.int32)
    padded = (counts + EXPERT_ROWS - 1) // EXPERT_ROWS * EXPERT_ROWS
    seg_end = jnp.cumsum(padded)
    seg_start = seg_end - padded
    n_blocks = (T * TOP_K + E * (EXPERT_ROWS - 1)) // EXPERT_ROWS
    dest = _slots(idx, rank, seg_start, _pick(T, 1024))
    sched = _expert_schedule(seg_start, seg_end, counts, n_blocks)
    n_used = (seg_end[-1:] // EXPERT_ROWS).astype(jnp.int32)
    xs = _sc_scatter_rows(packed, dest.reshape(-1), n_blocks * EXPERT_ROWS)
    ys = _expert_mlp(xs, sched, n_used, w_gate, w_up, w_down, layer)
    return _combine(x2d, ys, dest, gates, ws_gate, ws_up, ws_down, g, b, alpha, _pick(T, 256))


def _mixer_layer(x2d, i, B, S, p, alpha):
    T, D = x2d.shape
    tm = _pick(T, 512)
    tn = _pick(D, 1024)
    j = i // 2
    w_in = p["moba_w_in"][j] if i % 2 == 0 else p["fox_w_in"][j]
    w_qv_t = jnp.concatenate([w_in[:, :D].T * HEAD_DIM ** -0.5, w_in[:, 2 * D:3 * D].T]).astype(BF16)
    qv_t = _project_t(x2d, w_qv_t, tm, tn)
    k = _project(x2d, w_in[:, D:2 * D].astype(BF16), tm, tn)
    if i % 2 == 0:
        k_aug, k_mean = _moba_kprep(k.reshape(B, S, D))
        attn_t = _moba_attention(qv_t, k_aug, k_mean, p["rel_bias"], B, S)
        w_out = p["moba_w_out"][j]
    else:
        c_tok = _fox_gates(x2d.reshape(B, S, D), w_in[:, 3 * D:], p["fox_b_f"][j], _pick(S, 512))
        k_aug = _fox_kprep(k, c_tok.reshape(T, LANES), tm).reshape(HEADS_PER_TILE, B, S, D)
        attn_t = _fox_attention(qv_t, k_aug, B, S, _pick(S, 256))
        w_out = p["fox_w_out"][j]
    return _outproj_ln(attn_t, w_out.astype(BF16), x2d, p["ln1_g"][i], p["ln1_b"][i], alpha, tm)


def kernel(x, rel_bias, moba_w_in, moba_w_out, fox_w_in, fox_b_f, fox_w_out, ln1_g, ln1_b, ln2_g,
           ln2_b, w_router, router_bias, w_gate, w_up, w_down, ws_gate, ws_up, ws_down):
    B, S, D = x.shape
    depth = ln1_g.shape[0]
    alpha = (2 * depth) ** 0.25
    p = dict(rel_bias=rel_bias, moba_w_in=moba_w_in, moba_w_out=moba_w_out, fox_w_in=fox_w_in,
             fox_b_f=fox_b_f, fox_w_out=fox_w_out, ln1_g=ln1_g, ln1_b=ln1_b)
    x2d = x.reshape(B * S, D)
    for i in range(depth):
        x2d, packed = _mixer_layer(x2d, i, B, S, p, alpha)
        x2d = _moe_layer(x2d, packed, w_router[i], router_bias[i], w_gate, w_up, w_down, i,
                         ws_gate[i], ws_up[i], ws_down[i], ln2_g[i], ln2_b[i], alpha)
    return x2d.reshape(B, S, D)
```

```python
import functools
import math

import jax
import jax.numpy as jnp
from jax import lax
from jax.experimental import pallas as pl
from jax.experimental.pallas import tpu as pltpu
from jax.experimental.pallas import tpu_sc as plsc

F32 = jnp.float32
BF16 = jnp.bfloat16
HIGHEST = lax.Precision.HIGHEST

HEAD_DIM = 64
MOBA_BLOCK = 256
MOBA_TOPK = 3
REL_MAX_DIST = 128
TOP_K = 8
N_GROUPS = 8
TOPK_GROUPS = 4
ROUTED_SCALE = 2.5
LN_EPS = 1e-5
NEG = -1e30

LANES = 128
HEADS_PER_TILE = LANES // HEAD_DIM
EXPERT_ROWS = 512
VMEM_LIMIT = 48 * 1024 * 1024


def _cparams(*sem):
    return pltpu.CompilerParams(dimension_semantics=sem, vmem_limit_bytes=VMEM_LIMIT)


def _nt_dot(a, b, **kw):
    return lax.dot_general(a, b, (((1,), (1,)), ((), ())), preferred_element_type=F32, **kw)


def _split_bf16(x, pieces):
    out = []
    for _ in range(pieces):
        out.append(x.astype(BF16))
        x = x - out[-1].astype(F32)
    return out


def _three_pass(x, w_ref, mul):
    hi, lo = _split_bf16(x, 2)
    return mul(hi, w_ref[0]) + mul(lo, w_ref[0]) + mul(hi, w_ref[1])


def _proj_t_kernel(x_ref, wt_ref, o_ref):
    o_ref[...] = _nt_dot(wt_ref[...], x_ref[...].astype(BF16)).astype(o_ref.dtype)


def _project_t(x2d, wt, tm, tn):
    T, K = x2d.shape
    N = wt.shape[0]
    return pl.pallas_call(
        _proj_t_kernel,
        out_shape=jax.ShapeDtypeStruct((N, T), BF16),
        grid=(T // tm, N // tn),
        in_specs=[pl.BlockSpec((tm, K), lambda i, j: (i, 0)),
                  pl.BlockSpec((tn, K), lambda i, j: (j, 0))],
        out_specs=pl.BlockSpec((tn, tm), lambda i, j: (j, i)),
        compiler_params=_cparams("parallel", "arbitrary"),
        name="qv_proj_t",
    )(x2d, wt)


def _layer_norm(r, g, b):
    mu = jnp.mean(r, axis=-1, keepdims=True)
    d = r - mu
    var = jnp.mean(d * d, axis=-1, keepdims=True)
    return d * lax.rsqrt(var + LN_EPS) * g + b


def _pack_halves(y):
    n = y.shape[1] // 2
    bits = lambda v: lax.bitcast_convert_type(v.astype(BF16).astype(F32), jnp.int32)
    left = bits(y[:, :n])
    return lax.shift_right_logical(left, jnp.full_like(left, 16)) | bits(y[:, n:])


def _unpack_halves(p):
    return (lax.bitcast_convert_type(p << 16, F32),
            lax.bitcast_convert_type(p & jnp.int32(-65536), F32))


def _outproj_ln_kernel(at_ref, w_ref, x_ref, g_ref, b_ref, o_ref, packed_ref, *, alpha):
    y = lax.dot_general(at_ref[...], w_ref[...], (((0,), (0,)), ((), ())),
                        preferred_element_type=F32)
    out = _layer_norm(alpha * x_ref[...] + y, g_ref[...], b_ref[...])
    o_ref[...] = out
    packed_ref[...] = _pack_halves(out)


def _outproj_ln(attn_t, w_out, x2d, g, b, alpha, tm):
    T, D = x2d.shape
    row = lambda i: (i, 0)
    fixed = lambda i: (0, 0)
    return pl.pallas_call(
        functools.partial(_outproj_ln_kernel, alpha=alpha),
        out_shape=(jax.ShapeDtypeStruct((T, D), F32), jax.ShapeDtypeStruct((T, D // 2), jnp.int32)),
        grid=(T // tm,),
        in_specs=[pl.BlockSpec((D, tm), lambda i: (0, i)), pl.BlockSpec((D, D), fixed),
                  pl.BlockSpec((tm, D), row), pl.BlockSpec((1, D), fixed),
                  pl.BlockSpec((1, D), fixed)],
        out_specs=(pl.BlockSpec((tm, D), row), pl.BlockSpec((tm, D // 2), row)),
        compiler_params=_cparams("parallel"),
        name="outproj_ln",
    )(attn_t, w_out, x2d, g.reshape(1, D), b.reshape(1, D))


def _spare(a, n=0):
    return ((a + 1) % HEADS_PER_TILE) * HEAD_DIM + n


def _scores(chains):
    return [[jnp.dot(keys(), query(), preferred_element_type=F32) for keys, query, _ in blocks]
            for blocks in chains]


def _absorb(carries, scores, chains, masks=None, adds=None):
    probs = []
    for c, ((m, _), parts) in enumerate(zip(carries, scores)):
        if adds is not None:
            parts = [s if add is None else s + add() for add, s in zip(adds[c], parts)]
        if masks is not None:
            parts = [s if keep is None else jnp.where(keep, s, NEG)
                     for keep, s in zip(masks[c], parts)]
        tops = [jnp.max(s, axis=0, keepdims=True) for s in parts]
        m_new = functools.reduce(jnp.maximum, tops, m)
        probs.append((m_new, jnp.exp(m - m_new), [jnp.exp(s - m_new).astype(BF16) for s in parts]))
    out = []
    for (_, acc), (m_new, decay, ps), blocks in zip(carries, probs, chains):
        acc = decay * acc
        for (_, _, values), p in zip(blocks, ps):
            acc = acc + jnp.dot(values(), p, preferred_element_type=F32)
        out.append((m_new, acc))
    return tuple(out)


def _stash(chains, ref):
    for a, parts in enumerate(_scores(chains)):
        for b, s in enumerate(parts):
            ref[a, b] = s


def _fetch(chains, ref):
    return [[ref[a, b] for b in range(len(blocks))] for a, blocks in enumerate(chains)]


def _attend_masked_last(carries, n, group, stage_refs, masks, adds=None):
    first, second = stage_refs
    _stash(group(0), first)
    trips = n // 2

    def body(t, carries):
        g = 2 * t
        _stash(group(g + 1), second)
        carries = _absorb(carries, _fetch(group(g), first), group(g))
        _stash(group(g + 2), first)
        return _absorb(carries, _fetch(group(g + 1), second), group(g + 1))

    carries = lax.fori_loop(0, trips, body, carries)

    def odd_tail(carries):
        _stash(group(n), second)
        carries = _absorb(carries, _fetch(group(n - 1), first), group(n - 1))
        return _absorb(carries, _fetch(group(n), second), group(n), masks, adds)

    def even_tail(carries):
        return _absorb(carries, _fetch(group(n), first), group(n), masks, adds)

    return lax.cond(n % 2 == 1, odd_tail, even_tail, carries)


GROUP_BLOCKS = 2


def _stage_scratch(tq, chains=HEADS_PER_TILE):
    return [pltpu.VMEM((chains, GROUP_BLOCKS, tq, tq), F32) for _ in range(2)]


def _scores_init(tq):
    return jnp.full((1, tq), -jnp.inf, F32), jnp.zeros((LANES, tq), F32)


def _finish(carries, o_ref, cols=slice(None)):
    row = lax.broadcasted_iota(jnp.int32, carries[0][1].shape, 0)
    outs = []
    for a, (_, acc) in enumerate(carries):
        s = _spare(a)
        outs.append(acc / acc[s:s + 1, :])
    o_ref[:, cols] = jnp.where(row < HEAD_DIM, outs[0], outs[1]).astype(o_ref.dtype)


def _fill_v_aug(vt_ref, vaug_ref, tk):
    n_tiles = vt_ref.shape[1] // tk
    row = lax.broadcasted_iota(jnp.int32, (LANES, tk), 0)
    for n in range(n_tiles):
        v = vt_ref[:, n * tk:(n + 1) * tk]
        for a in range(HEADS_PER_TILE):
            vaug_ref[a, n] = jnp.where(row // HEAD_DIM == a, v, jnp.ones_like(v))


FOX_SPLIT = 3


def _fox_kernel(qt_ref, k_ref, vt_ref, o_ref, vaug_ref, stage0_ref, stage1_ref, *, tq):
    n = pl.program_id(2)

    @pl.when(n == 0)
    def _():
        _fill_v_aug(vt_ref, vaug_ref, tq)

    chains = [(h, a) for h in range(GROUP_BLOCKS) for a in range(HEADS_PER_TILE)]
    row = lax.broadcasted_iota(jnp.int32, (LANES, tq), 0)
    queries = {}
    for h, a in chains:
        offs = (row >= _spare(a)) & (row < _spare(a, FOX_SPLIT))
        queries[h, a] = jnp.where(row // HEAD_DIM == a, qt_ref[:, h * tq:(h + 1) * tq],
                                  jnp.where(offs, -1.0, 0.0).astype(BF16))

    def block(h, a, j):
        start = pl.multiple_of(j * tq, tq)
        return (lambda: k_ref[a, 0, pl.ds(start, tq), :], lambda: queries[h, a],
                lambda: vaug_ref[a, j])

    key = lax.broadcasted_iota(jnp.int32, (tq, tq), 0)
    qry = lax.broadcasted_iota(jnp.int32, (tq, tq), 1)
    masks = [[None if b < h else key + (b - h) * tq <= qry for b in range(GROUP_BLOCKS)]
             for h, _ in chains]
    carries = _attend_masked_last(
        tuple(_scores_init(tq) for _ in chains), n,
        lambda j: [[block(h, a, 2 * j), block(h, a, 2 * j + 1)] for h, a in chains],
        (stage0_ref, stage1_ref), masks)
    for h in range(GROUP_BLOCKS):
        _finish(carries[h * HEADS_PER_TILE:(h + 1) * HEADS_PER_TILE], o_ref,
                slice(h * tq, (h + 1) * tq))


def _fox_attention(qv_t, k_aug, B, S, tq):
    D = k_aug.shape[-1]
    n_tiles = D // LANES
    nq = S // tq
    span = GROUP_BLOCKS * tq
    steps = S // span
    return pl.pallas_call(
        functools.partial(_fox_kernel, tq=tq),
        out_shape=jax.ShapeDtypeStruct((D, B * S), BF16),
        grid=(B, n_tiles, steps),
        in_specs=[pl.BlockSpec((LANES, span), lambda b, h, i: (h, b * steps + i)),
                  pl.BlockSpec((HEADS_PER_TILE, 1, S, LANES), lambda b, h, i: (0, b, 0, h)),
                  pl.BlockSpec((LANES, S), lambda b, h, i: (n_tiles + h, b))],
        out_specs=pl.BlockSpec((LANES, span), lambda b, h, i: (h, b * steps + i)),
        scratch_shapes=[pltpu.VMEM((HEADS_PER_TILE, nq, LANES, tq), BF16)]
        + _stage_scratch(tq, GROUP_BLOCKS * HEADS_PER_TILE),
        compiler_params=_cparams("parallel", "parallel", "arbitrary"),
        name="fox_attention",
    )(qv_t, k_aug, qv_t)


def _log_sigmoid(z):
    return jnp.minimum(z, 0.0) - jnp.log1p(jnp.exp(-jnp.abs(z)))


def _fox_gates_kernel(x_ref, w_ref, b_ref, o_ref, carry_ref, *, ts):
    @pl.when(pl.program_id(1) == 0)
    def _():
        carry_ref[...] = jnp.zeros_like(carry_ref)

    z = _three_pass(x_ref[0], w_ref, lambda x, w: jnp.dot(x, w, preferred_element_type=F32))
    lf = _log_sigmoid(z + b_ref[...])
    r = lax.broadcasted_iota(jnp.int32, (ts, ts), 0)
    c = lax.broadcasted_iota(jnp.int32, (ts, ts), 1)
    tri = (c <= r).astype(BF16)
    cum = carry_ref[...]
    for piece in _split_bf16(lf, 3):
        cum = cum + jnp.dot(tri, piece, preferred_element_type=F32)
    carry_ref[...] = cum[ts - 1:ts, :]
    o_ref[0] = cum


def _fox_gates(x3d, w_f, b_f, ts):
    B, S, D = x3d.shape
    H = w_f.shape[1]
    w_pad = jnp.stack(_split_bf16(jnp.zeros((D, LANES), F32).at[:, :H].set(w_f), 2))
    b_pad = jnp.zeros((1, LANES), F32).at[0, :H].set(b_f)
    return pl.pallas_call(
        functools.partial(_fox_gates_kernel, ts=ts),
        out_shape=jax.ShapeDtypeStruct((B, S, LANES), F32),
        grid=(B, S // ts),
        in_specs=[pl.BlockSpec((1, ts, D), lambda b, s: (b, s, 0)),
                  pl.BlockSpec((2, D, LANES), lambda b, s: (0, 0, 0)),
                  pl.BlockSpec((1, LANES), lambda b, s: (0, 0))],
        out_specs=pl.BlockSpec((1, ts, LANES), lambda b, s: (b, s, 0)),
        scratch_shapes=[pltpu.VMEM((1, LANES), F32)],
        compiler_params=_cparams("parallel", "arbitrary"),
        name="fox_gates",
    )(x3d, w_pad, b_pad)


def _fox_keys_kernel(x_ref, w_ref, c_ref, o_ref):
    keys = jnp.dot(x_ref[...].astype(BF16), w_ref[...], preferred_element_type=F32)
    tm, D = keys.shape
    lane = lax.broadcasted_iota(jnp.int32, (tm, LANES), 1)
    for t in range(D // LANES):
        k = keys[:, t * LANES:(t + 1) * LANES]
        for a in range(HEADS_PER_TILE):
            h = t * HEADS_PER_TILE + a
            rest = c_ref[:, h:h + 1]
            aug = jnp.zeros((tm, LANES), F32)
            for n in range(FOX_SPLIT):
                piece = rest.astype(BF16).astype(F32)
                rest = rest - piece
                aug = jnp.where(lane == _spare(a, n), piece, aug)
            o_ref[a, :, t * LANES:(t + 1) * LANES] = jnp.where(lane // HEAD_DIM == a, k,
                                                               aug).astype(BF16)


def _fox_keys(x2d, w_k, c_tok, tm):
    T, K = x2d.shape
    D = w_k.shape[1]
    return pl.pallas_call(
        _fox_keys_kernel,
        out_shape=jax.ShapeDtypeStruct((HEADS_PER_TILE, T, D), BF16),
        grid=(T // tm,),
        in_specs=[pl.BlockSpec((tm, K), lambda i: (i, 0)),
                  pl.BlockSpec((K, D), lambda i: (0, 0)),
                  pl.BlockSpec((tm, LANES), lambda i: (i, 0))],
        out_specs=pl.BlockSpec((HEADS_PER_TILE, tm, D), lambda i: (0, i, 0)),
        compiler_params=_cparams("parallel"),
        name="fox_keys",
    )(x2d, w_k, c_tok)


MOBA_SLAB = 16


def _moba_kernel(bfar_ref, qt_ref, k_ref, vt_ref, kmean_ref, bown_ref, badj_ref, o_ref,
                 vaug_ref, sel_ref, stage0_ref, stage1_ref, *, blk):
    t = pl.program_id(1)
    n = pl.program_id(2)
    nb = kmean_ref.shape[1]

    @pl.when(n == 0)
    def _():
        _fill_v_aug(vt_ref, vaug_ref, blk)

    chains = [(h, a) for h in range(GROUP_BLOCKS) for a in range(HEADS_PER_TILE)]
    row = lax.broadcasted_iota(jnp.int32, (LANES, blk), 0)
    brow = lax.broadcasted_iota(jnp.int32, (nb, blk), 0)
    base = {}
    for c, (h, a) in enumerate(chains):
        own = GROUP_BLOCKS * n + h
        qt = qt_ref[:, h * blk:(h + 1) * blk]
        qa = jnp.where(row // HEAD_DIM == a, qt, jnp.zeros_like(qt))
        base[h, a] = qa
        bscore = jnp.dot(kmean_ref[0], qa.astype(F32), preferred_element_type=F32,
                         precision=HIGHEST)
        cand = jnp.where(brow < own, bscore, NEG)
        sel = jnp.zeros((nb, blk), F32)
        for _ in range(MOBA_TOPK):
            mx = jnp.max(cand, axis=0, keepdims=True)
            first = jnp.min(jnp.where(cand == mx, brow, nb), axis=0, keepdims=True)
            hit = brow == first
            sel = jnp.where(hit, 1.0, sel)
            cand = jnp.where(hit, -jnp.inf, cand)
        sel_ref[c] = jnp.where(brow < own, sel, 0.0)

    srow = lax.broadcasted_iota(jnp.int32, (MOBA_SLAB, blk), 0)

    def block(c, j):
        h, a = chains[c]
        head_id = t * HEADS_PER_TILE + a
        jc = jnp.maximum(j, 0)
        start = pl.multiple_of(jc * blk, blk)

        def query():
            s0 = _spare(a) // MOBA_SLAB * MOBA_SLAB
            chosen = (((sel_ref[c, pl.ds(jc, 1), :] > 0.0) & (j >= 0))
                      | (j == GROUP_BLOCKS * n + h))
            slab = jnp.where(srow == _spare(a) - s0,
                             jnp.where(chosen, bfar_ref[0, head_id], NEG),
                             jnp.where(srow == _spare(a, 1) - s0, bfar_ref[1, head_id], 0.0)
                             ).astype(BF16)
            q = base[h, a]
            head = [q[:s0]] if s0 else []
            tail = [q[s0 + MOBA_SLAB:]] if s0 + MOBA_SLAB < LANES else []
            return jnp.concatenate(head + [slab] + tail, axis=0)

        return lambda: k_ref[a, 0, pl.ds(start, blk), :], query, lambda: vaug_ref[a, jc]

    key = lax.broadcasted_iota(jnp.int32, (blk, blk), 0)
    qry = lax.broadcasted_iota(jnp.int32, (blk, blk), 1)
    carries = _attend_masked_last(
        tuple(_scores_init(blk) for _ in chains), n,
        lambda g: [[block(c, 2 * g + h - 1), block(c, 2 * g + h)] for c, (h, _) in enumerate(chains)],
        (stage0_ref, stage1_ref), [[None, key <= qry] for _ in chains],
        [[lambda a=a: badj_ref[a], lambda a=a: bown_ref[a]] for _, a in chains])
    for h in range(GROUP_BLOCKS):
        _finish(carries[h * HEADS_PER_TILE:(h + 1) * HEADS_PER_TILE], o_ref,
                slice(h * blk, (h + 1) * blk))


def _rel_bucket(dist, n_buckets):
    n = jnp.maximum(dist, 0)
    max_exact = n_buckets // 2
    nf = jnp.maximum(n, 1).astype(F32)
    large = max_exact + (jnp.log(nf / max_exact) / math.log(REL_MAX_DIST / max_exact)
                         * (n_buckets - max_exact)).astype(jnp.int32)
    large = jnp.minimum(large, n_buckets - 1)
    return jnp.where(n < max_exact, n, large)


def _moba_attention(qv_t, k_aug, k_mean, rel_bias, B, S):
    D = k_aug.shape[-1]
    n_tiles = D // LANES
    blk = MOBA_BLOCK
    nb = S // blk
    hp = HEADS_PER_TILE
    span = GROUP_BLOCKS * blk
    steps = S // span
    n_chains = GROUP_BLOCKS * hp
    r = jnp.arange(blk)
    delta = r[None, :] - r[:, None]
    bias_t = rel_bias.T.astype(F32)
    n_buckets = rel_bias.shape[0]
    def table(dist):
        onehot = jax.nn.one_hot(_rel_bucket(dist, n_buckets), n_buckets, dtype=F32)
        return jnp.einsum('crn,hn->hcr', onehot, bias_t, precision=HIGHEST)

    b_far = bias_t[:, n_buckets - 1]
    b_own = table(delta) - b_far[:, None, None]
    b_adj = table(delta + blk) - b_far[:, None, None]
    far_hi = b_far.astype(BF16).astype(F32)
    far_lo = (b_far - far_hi).astype(BF16).astype(F32)
    b_far2 = jnp.stack([far_hi, far_lo])
    return pl.pallas_call(
        functools.partial(_moba_kernel, blk=blk),
        out_shape=jax.ShapeDtypeStruct((D, B * S), BF16),
        grid_spec=pltpu.PrefetchScalarGridSpec(
            num_scalar_prefetch=1,
            grid=(B, n_tiles, steps),
            in_specs=[pl.BlockSpec((LANES, span), lambda b, h, i, f: (h, b * steps + i)),
                      pl.BlockSpec((hp, 1, S, LANES), lambda b, h, i, f: (0, b, 0, h)),
                      pl.BlockSpec((LANES, S), lambda b, h, i, f: (n_tiles + h, b)),
                      pl.BlockSpec((1, nb, LANES), lambda b, h, i, f: (b, 0, h)),
                      pl.BlockSpec((hp, blk, blk), lambda b, h, i, f: (h, 0, 0)),
                      pl.BlockSpec((hp, blk, blk), lambda b, h, i, f: (h, 0, 0))],
            out_specs=pl.BlockSpec((LANES, span), lambda b, h, i, f: (h, b * steps + i)),
            scratch_shapes=[pltpu.VMEM((hp, nb, LANES, blk), BF16),
                            pltpu.VMEM((n_chains, nb, blk), F32)] + _stage_scratch(blk, n_chains)),
        compiler_params=_cparams("parallel", "parallel", "arbitrary"),
        name="moba_attention",
    )(b_far2, qv_t, k_aug, qv_t, k_mean, b_own, b_adj)


def _moba_keys_kernel(x_ref, w_ref, o_ref, mean_ref):
    k = jnp.dot(x_ref[...].astype(BF16), w_ref[...], preferred_element_type=F32)
    col = lax.broadcasted_iota(jnp.int32, k.shape, 1)
    lane = col % LANES
    for a in range(HEADS_PER_TILE):
        ones = (lane == _spare(a)) | (lane == _spare(a, 1))
        o_ref[a] = jnp.where((col // HEAD_DIM) % HEADS_PER_TILE == a, k,
                             jnp.where(ones, 1.0, 0.0)).astype(BF16)
    for n in range(mean_ref.shape[0]):
        mean_ref[n] = jnp.mean(k[n * MOBA_BLOCK:(n + 1) * MOBA_BLOCK], axis=0, keepdims=True)


def _moba_keys(x2d, w_k, B, S, tm):
    T, K = x2d.shape
    D = w_k.shape[1]
    nb = S // MOBA_BLOCK
    per_tile = tm // MOBA_BLOCK
    k_aug, k_mean = pl.pallas_call(
        _moba_keys_kernel,
        out_shape=(jax.ShapeDtypeStruct((HEADS_PER_TILE, T, D), BF16),
                   jax.ShapeDtypeStruct((B * nb, 1, D), F32)),
        grid=(T // tm,),
        in_specs=[pl.BlockSpec((tm, K), lambda i: (i, 0)),
                  pl.BlockSpec((K, D), lambda i: (0, 0))],
        out_specs=(pl.BlockSpec((HEADS_PER_TILE, tm, D), lambda i: (0, i, 0)),
                   pl.BlockSpec((per_tile, 1, D), lambda i: (i, 0, 0))),
        compiler_params=_cparams("parallel"),
        name="moba_keys",
    )(x2d, w_k)
    return k_aug.reshape(HEADS_PER_TILE, B, S, D), k_mean.reshape(B, nb, D)


def _router_kernel(x_ref, wt_ref, bias_ref, idx_ref, gate_ref, rank_ref, cnt_ref, run_ref, *, tm):
    @pl.when(pl.program_id(0) == 0)
    def _():
        run_ref[...] = jnp.zeros_like(run_ref)

    E = wt_ref.shape[1]
    gsz = E // N_GROUPS
    logits = _three_pass(x_ref[...], wt_ref, lambda x, w: _nt_dot(w, x))
    s = jax.nn.sigmoid(logits)
    sb = s + bias_ref[...]
    neg_inf = -jnp.inf

    giota = lax.broadcasted_iota(jnp.int32, (gsz, tm), 0)
    gscore = []
    for g in range(N_GROUPS):
        blk = sb[g * gsz:(g + 1) * gsz, :]
        m1 = jnp.max(blk, axis=0, keepdims=True)
        i1 = jnp.min(jnp.where(blk == m1, giota, gsz), axis=0, keepdims=True)
        m2 = jnp.max(jnp.where(giota == i1, neg_inf, blk), axis=0, keepdims=True)
        gscore.append(m1 + m2)

    gsel = [jnp.zeros((1, tm), jnp.bool_) for _ in range(N_GROUPS)]
    for _ in range(TOPK_GROUPS):
        mx = functools.reduce(jnp.maximum, gscore)
        found = jnp.zeros((1, tm), jnp.bool_)
        for g in range(N_GROUPS):
            hit = (gscore[g] == mx) & jnp.logical_not(found)
            gsel[g] = gsel[g] | hit
            found = found | hit
            gscore[g] = jnp.where(hit, neg_inf, gscore[g])
    emask = jnp.concatenate([jnp.broadcast_to(gsel[g], (gsz, tm)) for g in range(N_GROUPS)], axis=0)
    cand = jnp.where(emask, sb, NEG)

    eiota = lax.broadcasted_iota(jnp.int32, (E, tm), 0)
    hits, idxs, ws = [], [], []
    for _ in range(TOP_K):
        mx = jnp.max(cand, axis=0, keepdims=True)
        first = jnp.min(jnp.where(cand == mx, eiota, E), axis=0, keepdims=True)
        hit = eiota == first
        hits.append(hit)
        idxs.append(first)
        ws.append(jnp.sum(jnp.where(hit, s, 0.0), axis=0, keepdims=True))
        cand = jnp.where(hit, neg_inf, cand)
    wsum = functools.reduce(jnp.add, ws)

    chosen = functools.reduce(jnp.logical_or, hits)
    onehot = jnp.where(chosen, 1.0, 0.0)
    tr = lax.broadcasted_iota(jnp.int32, (tm, tm), 0)
    tc = lax.broadcasted_iota(jnp.int32, (tm, tm), 1)
    before = (tr < tc).astype(BF16)
    prior = jnp.dot(onehot.astype(BF16), before, preferred_element_type=F32) + run_ref[...]
    grow = lax.broadcasted_iota(jnp.int32, (LANES, tm), 0)
    gates = jnp.zeros((LANES, tm), F32)
    for k in range(TOP_K):
        idx_ref[k:k + 1, :] = idxs[k]
        gates = jnp.where(grow == k, ws[k] / wsum * ROUTED_SCALE, gates)
        rank_ref[k:k + 1, :] = jnp.sum(jnp.where(hits[k], prior, 0.0), axis=0,
                                       keepdims=True).astype(jnp.int32)
    gate_ref[...] = gates.T
    run_ref[...] = run_ref[...] + jnp.sum(onehot, axis=1, keepdims=True)
    cnt_ref[...] = run_ref[...]


def _router(x2d, w_router, router_bias, tm):
    T, D = x2d.shape
    E = w_router.shape[1]
    tok = lambda i: (0, i)
    fixed = lambda i: (0, 0)
    return pl.pallas_call(
        functools.partial(_router_kernel, tm=tm),
        out_shape=(jax.ShapeDtypeStruct((TOP_K, T), jnp.int32),
                   jax.ShapeDtypeStruct((T, LANES), F32),
                   jax.ShapeDtypeStruct((TOP_K, T), jnp.int32),
                   jax.ShapeDtypeStruct((E, 1), F32)),
        grid=(T // tm,),
        in_specs=[pl.BlockSpec((tm, D), lambda i: (i, 0)),
                  pl.BlockSpec((2, E, D), lambda i: (0, 0, 0)),
                  pl.BlockSpec((E, 1), fixed)],
        out_specs=(pl.BlockSpec((TOP_K, tm), tok), pl.BlockSpec((tm, LANES), lambda i: (i, 0)),
                   pl.BlockSpec((TOP_K, tm), tok), pl.BlockSpec((E, 1), fixed)),
        scratch_shapes=[pltpu.VMEM((E, 1), F32)],
        compiler_params=_cparams("arbitrary"),
        name="router",
    )(x2d, jnp.stack(_split_bf16(w_router.T, 2)), router_bias.reshape(E, 1).astype(F32))


def _slots_kernel(idx_ref, rank_ref, start_ref, o_ref):
    E = start_ref.shape[0]
    tm = idx_ref.shape[1]
    eiota = lax.broadcasted_iota(jnp.int32, (E, tm), 0)
    start = start_ref[...]
    for k in range(TOP_K):
        base = jnp.sum(jnp.where(eiota == idx_ref[k:k + 1, :], start, 0.0), axis=0, keepdims=True)
        o_ref[k:k + 1, :] = base.astype(jnp.int32) + rank_ref[k:k + 1, :]


def _slots(idx, rank, seg_start, tm):
    T = idx.shape[1]
    E = seg_start.shape[0]
    tok = lambda i: (0, i)
    return pl.pallas_call(
        _slots_kernel,
        out_shape=jax.ShapeDtypeStruct((TOP_K, T), jnp.int32),
        grid=(T // tm,),
        in_specs=[pl.BlockSpec((TOP_K, tm), tok), pl.BlockSpec((TOP_K, tm), tok),
                  pl.BlockSpec((E, 1), lambda i: (0, 0))],
        out_specs=pl.BlockSpec((TOP_K, tm), tok),
        compiler_params=_cparams("parallel"),
        name="moe_slots",
    )(idx, rank, seg_start.astype(F32).reshape(E, 1))


SC_CORES = 2
SC_SUBCORES = 16
SC_CHUNK = 128


def _sc_rows_kernel(body, out_shape, n_chunks, width, dtype, name):
    mesh = plsc.VectorSubcoreMesh(core_axis_name="c", subcore_axis_name="s",
                                  num_cores=SC_CORES, num_subcores=SC_SUBCORES)
    return pl.kernel(
        body, mesh=mesh, out_type=jax.ShapeDtypeStruct(out_shape, dtype),
        scratch_types=[pltpu.VMEM((n_chunks, SC_CHUNK), jnp.int32),
                       pltpu.VMEM((SC_CHUNK, width), dtype),
                       pltpu.SemaphoreType.DMA],
        name=name)


def _sc_chunks(n):
    workers = SC_CORES * SC_SUBCORES
    chunks = n // (workers * SC_CHUNK)
    assert chunks * workers * SC_CHUNK == n
    return workers, chunks


def _sc_gather_rows(table, idx):
    workers, chunks = _sc_chunks(idx.shape[0])

    def body(table_hbm, idx_hbm, out_hbm, idx_v, rows_v, sem):
        wid = lax.axis_index("s") * SC_CORES + lax.axis_index("c")
        pltpu.sync_copy(idx_hbm.at[wid], idx_v)

        @pl.loop(0, chunks)
        def _(j):
            pltpu.async_copy(table_hbm.at[idx_v.at[j]], rows_v, sem).wait()
            pltpu.sync_copy(rows_v, out_hbm.at[pl.ds((wid * chunks + j) * SC_CHUNK, SC_CHUNK)])

    call = _sc_rows_kernel(body, (idx.shape[0], table.shape[1]), chunks, table.shape[1],
                           table.dtype, "moe_gather_sc")
    return call(table, idx.reshape(workers, chunks, SC_CHUNK))


def _sc_scatter_rows(src, idx, n_rows):
    n_src = src.shape[0]
    assert n_src % SC_CHUNK == 0
    workers, chunks = _sc_chunks(idx.shape[0])

    def body(src_hbm, idx_hbm, out_hbm, idx_v, rows_v, sem):
        wid = lax.axis_index("s") * SC_CORES + lax.axis_index("c")
        pltpu.sync_copy(idx_hbm.at[wid], idx_v)

        @pl.loop(0, chunks)
        def _(j):
            first = lax.rem((wid * chunks + j) * SC_CHUNK, n_src)
            pltpu.sync_copy(src_hbm.at[pl.ds(first, SC_CHUNK)], rows_v)
            pltpu.async_copy(rows_v, out_hbm.at[idx_v.at[j]], sem).wait()

    call = _sc_rows_kernel(body, (n_rows, src.shape[1]), chunks, src.shape[1], src.dtype,
                           "moe_scatter_sc")
    return call(src, idx.reshape(workers, chunks, SC_CHUNK))


def _silu(g):
    return g * jax.nn.sigmoid(g)


SCHED_FIRST, SCHED_SLOT, SCHED_NEXT, SCHED_VALID = 0, 1, 2, 3


def _expert_kernel(sched_ref, n_used_ref, x_ref, wg_hbm, wu_hbm, wd_hbm, o_ref,
                   wg_buf, wu_buf, wd_buf, sem, *, layer, first_expert_row):
    i = pl.program_id(0)
    slot = sched_ref[SCHED_SLOT, i]

    def copies(expert, slot):
        pairs = ((wg_hbm, wg_buf), (wu_hbm, wu_buf), (wd_hbm, wd_buf))
        return [pltpu.make_async_copy(src.at[layer, expert], dst.at[slot], sem.at[slot, n])
                for n, (src, dst) in enumerate(pairs)]

    @pl.when(i == 0)
    def _():
        for c in copies(sched_ref[first_expert_row, 0], 0):
            c.start()

    @pl.when(sched_ref[SCHED_FIRST, i] == 1)
    def _():
        for c in copies(0, slot):
            c.wait()

        @pl.when(sched_ref[SCHED_NEXT, i] >= 0)
        def _():
            for c in copies(sched_ref[SCHED_NEXT, i], 1 - slot):
                c.start()

    @pl.when(i < n_used_ref[0])
    def _():
        words = x_ref[...]
        row = lax.broadcasted_iota(jnp.int32, words.shape, 0)
        words = jnp.where(row < sched_ref[SCHED_VALID, i], words, 0)
        x = jnp.concatenate(_unpack_halves(words), axis=1).astype(BF16)
        g = jnp.dot(x, wg_buf[slot].astype(BF16), preferred_element_type=F32)
        u = jnp.dot(x, wu_buf[slot].astype(BF16), preferred_element_type=F32)
        a = (_silu(g) * u).astype(BF16)
        o_ref[...] = _pack_halves(jnp.dot(a, wd_buf[slot].astype(BF16), preferred_element_type=F32))

    @pl.when(i >= n_used_ref[0])
    def _():
        o_ref[...] = jnp.zeros_like(o_ref)


def _expert_schedule(seg_start, seg_end, counts, n_blocks):
    E = seg_end.shape[0]
    first_row = jnp.arange(n_blocks, dtype=jnp.int32) * EXPERT_ROWS
    blk_e = jnp.minimum(jnp.sum(seg_end[None, :] <= first_row[:, None], axis=1), E - 1)
    used = first_row < seg_end[-1]
    first = used & (first_row == seg_start[blk_e])
    ordinal = jnp.cumsum(first) - 1
    ids = jnp.where(seg_end > seg_start, jnp.arange(E), E)
    later = lax.cummin(ids, reverse=True)
    nxt = jnp.concatenate([later[1:], jnp.full((1,), E, later.dtype)])[blk_e]
    nxt = jnp.where(nxt < E, nxt, -1)
    valid = jnp.clip((seg_start + counts)[blk_e] - first_row, 0, EXPERT_ROWS)
    return jnp.stack([first, ordinal % 2, nxt, valid, blk_e]).astype(jnp.int32)


def _expert_mlp(xs, sched, n_used, w_gate, w_up, w_down, layer):
    n_rows, half = xs.shape
    D = 2 * half
    F = w_gate.shape[-1]
    tm = EXPERT_ROWS
    hbm = pl.BlockSpec(memory_space=pl.ANY)
    return pl.pallas_call(
        functools.partial(_expert_kernel, layer=layer, first_expert_row=sched.shape[0] - 1),
        out_shape=jax.ShapeDtypeStruct((n_rows, half), jnp.int32),
        grid_spec=pltpu.PrefetchScalarGridSpec(
            num_scalar_prefetch=2,
            grid=(n_rows // tm,),
            in_specs=[pl.BlockSpec((tm, half), lambda i, sc, nu: (jnp.minimum(i, nu[0] - 1), 0)),
                      hbm, hbm, hbm],
            out_specs=pl.BlockSpec((tm, half), lambda i, sc, nu: (i, 0)),
            scratch_shapes=[pltpu.VMEM((2, D, F), F32), pltpu.VMEM((2, D, F), F32),
                            pltpu.VMEM((2, F, D), F32), pltpu.SemaphoreType.DMA((2, 3))]),
        compiler_params=_cparams("arbitrary"),
        name="moe_experts",
    )(sched, n_used, xs, w_gate, w_up, w_down)


def _combine_kernel(x_ref, gate_ref, rows_ref, wsg_ref, wsu_ref, wsd_ref, g_ref, b_ref, o_ref,
                    *, alpha):
    x = x_ref[...]
    xb = x.astype(BF16)
    g = jnp.dot(xb, wsg_ref[...], preferred_element_type=F32)
    u = jnp.dot(xb, wsu_ref[...], preferred_element_type=F32)
    y = jnp.dot((_silu(g) * u).astype(BF16), wsd_ref[...], preferred_element_type=F32)
    gates = gate_ref[...]
    half = y.shape[1] // 2
    left, right = y[:, :half], y[:, half:]
    for k in range(TOP_K):
        lo, hi = _unpack_halves(rows_ref[k])
        left = left + lo * gates[:, k:k + 1]
        right = right + hi * gates[:, k:k + 1]
    y = jnp.concatenate([left, right], axis=1)
    o_ref[...] = _layer_norm(alpha * x + y, g_ref[...], b_ref[...])


def _combine(x2d, ys, dest, gates_tok, ws_gate, ws_up, ws_down, g, b, alpha, tt):
    T, D = x2d.shape
    F = ws_gate.shape[1]
    row = lambda i: (i, 0)
    fixed = lambda i: (0, 0)
    rows = _sc_gather_rows(ys, dest.reshape(-1)).reshape(TOP_K, T, D // 2)
    return pl.pallas_call(
        functools.partial(_combine_kernel, alpha=alpha),
        out_shape=jax.ShapeDtypeStruct((T, D), F32),
        grid=(T // tt,),
        in_specs=[pl.BlockSpec((tt, D), row),
                  pl.BlockSpec((tt, LANES), row),
                  pl.BlockSpec((TOP_K, tt, D // 2), lambda i: (0, i, 0)),
                  pl.BlockSpec((D, F), fixed), pl.BlockSpec((D, F), fixed),
                  pl.BlockSpec((F, D), fixed),
                  pl.BlockSpec((1, D), fixed), pl.BlockSpec((1, D), fixed)],
        out_specs=pl.BlockSpec((tt, D), row),
        compiler_params=_cparams("parallel"),
        name="moe_combine",
    )(x2d, gates_tok, rows, ws_gate.astype(BF16), ws_up.astype(BF16), ws_down.astype(BF16),
      g.reshape(1, D), b.reshape(1, D))


def _pick(n, pref):
    t = min(n, pref)
    while n % t:
        t //= 2
    return t


def _moe_layer(x2d, packed, w_router, router_bias, w_gate, w_up, w_down, layer, ws_gate, ws_up,
               ws_down, g, b, alpha):
    T, D = x2d.shape
    E = w_router.shape[1]
    idx, gates, rank, counts = _router(x2d, w_router, router_bias, _pick(T, 256))
    counts = counts[:, 0].astype(jnp.int32)
    padded = (counts + EXPERT_ROWS - 1) // EXPERT_ROWS * EXPERT_ROWS
    seg_end = jnp.cumsum(padded)
    seg_start = seg_end - padded
    n_blocks = (T * TOP_K + E * (EXPERT_ROWS - 1)) // EXPERT_ROWS
    dest = _slots(idx, rank, seg_start, _pick(T, 1024))
    sched = _expert_schedule(seg_start, seg_end, counts, n_blocks)
    n_used = (seg_end[-1:] // EXPERT_ROWS).astype(jnp.int32)
    xs = _sc_scatter_rows(packed, dest.reshape(-1), n_blocks * EXPERT_ROWS)
    ys = _expert_mlp(xs, sched, n_used, w_gate, w_up, w_down, layer)
    return _combine(x2d, ys, dest, gates, ws_gate, ws_up, ws_down, g, b, alpha, _pick(T, 256))


def _mixer_layer(x2d, i, B, S, p, alpha):
    T, D = x2d.shape
    tm = _pick(T, 512)
    tn = _pick(D, 1024)
    j = i // 2
    w_in = p["moba_w_in"][j] if i % 2 == 0 else p["fox_w_in"][j]
    w_qv_t = jnp.concatenate([w_in[:, :D].T * HEAD_DIM ** -0.5, w_in[:, 2 * D:3 * D].T]).astype(BF16)
    qv_t = _project_t(x2d, w_qv_t, tm, tn)
    w_k = w_in[:, D:2 * D].astype(BF16)
    if i % 2 == 0:
        k_aug, k_mean = _moba_keys(x2d, w_k, B, S, tm)
        attn_t = _moba_attention(qv_t, k_aug, k_mean, p["rel_bias"], B, S)
        w_out = p["moba_w_out"][j]
    else:
        c_tok = _fox_gates(x2d.reshape(B, S, D), w_in[:, 3 * D:], p["fox_b_f"][j], _pick(S, 512))
        k_aug = _fox_keys(x2d, w_k, c_tok.reshape(T, LANES), tm).reshape(HEADS_PER_TILE, B, S, D)
        attn_t = _fox_attention(qv_t, k_aug, B, S, _pick(S, 256))
        w_out = p["fox_w_out"][j]
    return _outproj_ln(attn_t, w_out.astype(BF16), x2d, p["ln1_g"][i], p["ln1_b"][i], alpha, tm)


def kernel(x, rel_bias, moba_w_in, moba_w_out, fox_w_in, fox_b_f, fox_w_out, ln1_g, ln1_b, ln2_g,
           ln2_b, w_router, router_bias, w_gate, w_up, w_down, ws_gate, ws_up, ws_down):
    B, S, D = x.shape
    depth = ln1_g.shape[0]
    alpha = (2 * depth) ** 0.25
    p = dict(rel_bias=rel_bias, moba_w_in=moba_w_in, moba_w_out=moba_w_out, fox_w_in=fox_w_in,
             fox_b_f=fox_b_f, fox_w_out=fox_w_out, ln1_g=ln1_g, ln1_b=ln1_b)
    x2d = x.reshape(B * S, D)
    for i in range(depth):
        x2d, packed = _mixer_layer(x2d, i, B, S, p, alpha)
        x2d = _moe_layer(x2d, packed, w_router[i], router_bias[i], w_gate, w_up, w_down, i,
                         ws_gate[i], ws_up[i], ws_down[i], ln2_g[i], ln2_b[i], alpha)
    return x2d.reshape(B, S, D)
```

```python
import functools
import math

import jax
import jax.numpy as jnp
from jax import lax
from jax.experimental import pallas as pl
from jax.experimental.pallas import tpu as pltpu
from jax.experimental.pallas import tpu_sc as plsc

F32 = jnp.float32
BF16 = jnp.bfloat16
HIGHEST = lax.Precision.HIGHEST

HEAD_DIM = 64
MOBA_BLOCK = 256
MOBA_TOPK = 3
REL_MAX_DIST = 128
TOP_K = 8
N_GROUPS = 8
TOPK_GROUPS = 4
ROUTED_SCALE = 2.5
LN_EPS = 1e-5
NEG = -1e30

LANES = 128
HEADS_PER_TILE = LANES // HEAD_DIM
EXPERT_ROWS = 512
VMEM_LIMIT = 48 * 1024 * 1024


def _cparams(*sem):
    return pltpu.CompilerParams(dimension_semantics=sem, vmem_limit_bytes=VMEM_LIMIT)


def _nt_dot(a, b, **kw):
    return lax.dot_general(a, b, (((1,), (1,)), ((), ())), preferred_element_type=F32, **kw)


def _split_bf16(x, pieces):
    out = []
    for _ in range(pieces):
        out.append(x.astype(BF16))
        x = x - out[-1].astype(F32)
    return out


def _three_pass(x, w_ref, mul):
    hi, lo = _split_bf16(x, 2)
    return mul(hi, w_ref[0]) + mul(lo, w_ref[0]) + mul(hi, w_ref[1])


def _proj_t_kernel(x_ref, wt_ref, o_ref):
    o_ref[...] = _nt_dot(wt_ref[...], x_ref[...].astype(BF16)).astype(o_ref.dtype)


def _project_t(x2d, wt, tm, tn):
    T, K = x2d.shape
    N = wt.shape[0]
    return pl.pallas_call(
        _proj_t_kernel,
        out_shape=jax.ShapeDtypeStruct((N, T), BF16),
        grid=(T // tm, N // tn),
        in_specs=[pl.BlockSpec((tm, K), lambda i, j: (i, 0)),
                  pl.BlockSpec((tn, K), lambda i, j: (j, 0))],
        out_specs=pl.BlockSpec((tn, tm), lambda i, j: (j, i)),
        compiler_params=_cparams("parallel", "arbitrary"),
        name="qv_proj_t",
    )(x2d, wt)


def _layer_norm(r, g, b):
    mu = jnp.mean(r, axis=-1, keepdims=True)
    d = r - mu
    var = jnp.mean(d * d, axis=-1, keepdims=True)
    return d * lax.rsqrt(var + LN_EPS) * g + b


def _pack_halves(y):
    n = y.shape[1] // 2
    bits = lambda v: lax.bitcast_convert_type(v.astype(BF16).astype(F32), jnp.int32)
    left = bits(y[:, :n])
    return lax.shift_right_logical(left, jnp.full_like(left, 16)) | bits(y[:, n:])


def _unpack_halves(p):
    return (lax.bitcast_convert_type(p << 16, F32),
            lax.bitcast_convert_type(p & jnp.int32(-65536), F32))


def _outproj_ln_kernel(at_ref, w_ref, x_ref, g_ref, b_ref, o_ref, packed_ref, *, alpha):
    y = lax.dot_general(at_ref[...], w_ref[...], (((0,), (0,)), ((), ())),
                        preferred_element_type=F32)
    out = _layer_norm(alpha * x_ref[...] + y, g_ref[...], b_ref[...])
    o_ref[...] = out
    packed_ref[...] = _pack_halves(out)


def _outproj_ln(attn_t, w_out, x2d, g, b, alpha, tm):
    T, D = x2d.shape
    row = lambda i: (i, 0)
    fixed = lambda i: (0, 0)
    return pl.pallas_call(
        functools.partial(_outproj_ln_kernel, alpha=alpha),
        out_shape=(jax.ShapeDtypeStruct((T, D), F32), jax.ShapeDtypeStruct((T, D // 2), jnp.int32)),
        grid=(T // tm,),
        in_specs=[pl.BlockSpec((D, tm), lambda i: (0, i)), pl.BlockSpec((D, D), fixed),
                  pl.BlockSpec((tm, D), row), pl.BlockSpec((1, D), fixed),
                  pl.BlockSpec((1, D), fixed)],
        out_specs=(pl.BlockSpec((tm, D), row), pl.BlockSpec((tm, D // 2), row)),
        compiler_params=_cparams("parallel"),
        name="outproj_ln",
    )(attn_t, w_out, x2d, g.reshape(1, D), b.reshape(1, D))


def _spare(a, n=0):
    return ((a + 1) % HEADS_PER_TILE) * HEAD_DIM + n


def _scores(chains):
    return [[jnp.dot(keys(), query(), preferred_element_type=F32) for keys, query, _ in blocks]
            for blocks in chains]


def _absorb(carries, scores, chains, masks=None, adds=None):
    probs = []
    for c, ((m, _), parts) in enumerate(zip(carries, scores)):
        if adds is not None:
            parts = [s if add is None else s + add() for add, s in zip(adds[c], parts)]
        if masks is not None:
            parts = [s if keep is None else jnp.where(keep, s, NEG)
                     for keep, s in zip(masks[c], parts)]
        tops = [jnp.max(s, axis=0, keepdims=True) for s in parts]
        m_new = functools.reduce(jnp.maximum, tops, m)
        probs.append((m_new, jnp.exp(m - m_new), [jnp.exp(s - m_new).astype(BF16) for s in parts]))
    out = []
    for (_, acc), (m_new, decay, ps), blocks in zip(carries, probs, chains):
        acc = decay * acc
        for (_, _, values), p in zip(blocks, ps):
            acc = acc + jnp.dot(values(), p, preferred_element_type=F32)
        out.append((m_new, acc))
    return tuple(out)


def _stash(chains, ref):
    for a, parts in enumerate(_scores(chains)):
        for b, s in enumerate(parts):
            ref[a, b] = s


def _fetch(chains, ref):
    return [[ref[a, b] for b in range(len(blocks))] for a, blocks in enumerate(chains)]


def _attend_masked_last(carries, n, group, stage_refs, masks, adds=None):
    first, second = stage_refs
    _stash(group(0), first)
    trips = n // 2

    def body(t, carries):
        g = 2 * t
        _stash(group(g + 1), second)
        carries = _absorb(carries, _fetch(group(g), first), group(g))
        _stash(group(g + 2), first)
        return _absorb(carries, _fetch(group(g + 1), second), group(g + 1))

    carries = lax.fori_loop(0, trips, body, carries)

    def odd_tail(carries):
        _stash(group(n), second)
        carries = _absorb(carries, _fetch(group(n - 1), first), group(n - 1))
        return _absorb(carries, _fetch(group(n), second), group(n), masks, adds)

    def even_tail(carries):
        return _absorb(carries, _fetch(group(n), first), group(n), masks, adds)

    return lax.cond(n % 2 == 1, odd_tail, even_tail, carries)


GROUP_BLOCKS = 2


def _stage_scratch(tq, chains=HEADS_PER_TILE):
    return [pltpu.VMEM((chains, GROUP_BLOCKS, tq, tq), F32) for _ in range(2)]


def _scores_init(tq):
    return jnp.full((1, tq), -jnp.inf, F32), jnp.zeros((LANES, tq), F32)


def _finish(carries, o_ref, cols=slice(None)):
    row = lax.broadcasted_iota(jnp.int32, carries[0][1].shape, 0)
    outs = []
    for a, (_, acc) in enumerate(carries):
        s = _spare(a)
        outs.append(acc / acc[s:s + 1, :])
    o_ref[:, cols] = jnp.where(row < HEAD_DIM, outs[0], outs[1]).astype(o_ref.dtype)


def _fill_v_aug(vt_ref, vaug_ref, tk):
    n_tiles = vt_ref.shape[1] // tk
    row = lax.broadcasted_iota(jnp.int32, (LANES, tk), 0)
    for n in range(n_tiles):
        v = vt_ref[:, n * tk:(n + 1) * tk]
        for a in range(HEADS_PER_TILE):
            vaug_ref[a, n] = jnp.where(row // HEAD_DIM == a, v, jnp.ones_like(v))


FOX_SPLIT = 3


def _fox_kernel(qt_ref, k_ref, vt_ref, o_ref, vaug_ref, stage0_ref, stage1_ref, *, tq):
    n = pl.program_id(2)

    @pl.when(n == 0)
    def _():
        _fill_v_aug(vt_ref, vaug_ref, tq)

    chains = [(h, a) for h in range(GROUP_BLOCKS) for a in range(HEADS_PER_TILE)]
    row = lax.broadcasted_iota(jnp.int32, (LANES, tq), 0)
    queries = {}
    for h, a in chains:
        offs = (row >= _spare(a)) & (row < _spare(a, FOX_SPLIT))
        queries[h, a] = jnp.where(row // HEAD_DIM == a, qt_ref[:, h * tq:(h + 1) * tq],
                                  jnp.where(offs, -1.0, 0.0).astype(BF16))

    def block(h, a, j):
        start = pl.multiple_of(j * tq, tq)
        return (lambda: k_ref[a, 0, pl.ds(start, tq), :], lambda: queries[h, a],
                lambda: vaug_ref[a, j])

    key = lax.broadcasted_iota(jnp.int32, (tq, tq), 0)
    qry = lax.broadcasted_iota(jnp.int32, (tq, tq), 1)
    masks = [[None if b < h else key + (b - h) * tq <= qry for b in range(GROUP_BLOCKS)]
             for h, _ in chains]
    carries = _attend_masked_last(
        tuple(_scores_init(tq) for _ in chains), n,
        lambda j: [[block(h, a, 2 * j), block(h, a, 2 * j + 1)] for h, a in chains],
        (stage0_ref, stage1_ref), masks)
    for h in range(GROUP_BLOCKS):
        _finish(carries[h * HEADS_PER_TILE:(h + 1) * HEADS_PER_TILE], o_ref,
                slice(h * tq, (h + 1) * tq))


def _fox_attention(qv_t, k_aug, B, S, tq):
    D = k_aug.shape[-1]
    n_tiles = D // LANES
    nq = S // tq
    span = GROUP_BLOCKS * tq
    steps = S // span
    return pl.pallas_call(
        functools.partial(_fox_kernel, tq=tq),
        out_shape=jax.ShapeDtypeStruct((D, B * S), BF16),
        grid=(B, n_tiles, steps),
        in_specs=[pl.BlockSpec((LANES, span), lambda b, h, i: (h, b * steps + i)),
                  pl.BlockSpec((HEADS_PER_TILE, 1, S, LANES), lambda b, h, i: (0, b, 0, h)),
                  pl.BlockSpec((LANES, S), lambda b, h, i: (n_tiles + h, b))],
        out_specs=pl.BlockSpec((LANES, span), lambda b, h, i: (h, b * steps + i)),
        scratch_shapes=[pltpu.VMEM((HEADS_PER_TILE, nq, LANES, tq), BF16)]
        + _stage_scratch(tq, GROUP_BLOCKS * HEADS_PER_TILE),
        compiler_params=_cparams("parallel", "parallel", "arbitrary"),
        name="fox_attention",
    )(qv_t, k_aug, qv_t)


def _log_sigmoid(z):
    return jnp.minimum(z, 0.0) - jnp.log1p(jnp.exp(-jnp.abs(z)))


def _fox_gates_kernel(x_ref, w_ref, b_ref, o_ref, carry_ref, *, ts):
    @pl.when(pl.program_id(1) == 0)
    def _():
        carry_ref[...] = jnp.zeros_like(carry_ref)

    z = _three_pass(x_ref[0], w_ref, lambda x, w: jnp.dot(x, w, preferred_element_type=F32))
    lf = _log_sigmoid(z + b_ref[...])
    r = lax.broadcasted_iota(jnp.int32, (ts, ts), 0)
    c = lax.broadcasted_iota(jnp.int32, (ts, ts), 1)
    tri = (c <= r).astype(BF16)
    cum = carry_ref[...]
    for piece in _split_bf16(lf, 3):
        cum = cum + jnp.dot(tri, piece, preferred_element_type=F32)
    carry_ref[...] = cum[ts - 1:ts, :]
    o_ref[0] = cum


def _fox_gates(x3d, w_f, b_f, ts):
    B, S, D = x3d.shape
    H = w_f.shape[1]
    w_pad = jnp.stack(_split_bf16(jnp.zeros((D, LANES), F32).at[:, :H].set(w_f), 2))
    b_pad = jnp.zeros((1, LANES), F32).at[0, :H].set(b_f)
    return pl.pallas_call(
        functools.partial(_fox_gates_kernel, ts=ts),
        out_shape=jax.ShapeDtypeStruct((B, S, LANES), F32),
        grid=(B, S // ts),
        in_specs=[pl.BlockSpec((1, ts, D), lambda b, s: (b, s, 0)),
                  pl.BlockSpec((2, D, LANES), lambda b, s: (0, 0, 0)),
                  pl.BlockSpec((1, LANES), lambda b, s: (0, 0))],
        out_specs=pl.BlockSpec((1, ts, LANES), lambda b, s: (b, s, 0)),
        scratch_shapes=[pltpu.VMEM((1, LANES), F32)],
        compiler_params=_cparams("parallel", "arbitrary"),
        name="fox_gates",
    )(x3d, w_pad, b_pad)


def _fox_keys_kernel(x_ref, w_ref, c_ref, o_ref):
    keys = jnp.dot(x_ref[...].astype(BF16), w_ref[...], preferred_element_type=F32)
    tm, D = keys.shape
    lane = lax.broadcasted_iota(jnp.int32, (tm, LANES), 1)
    for t in range(D // LANES):
        k = keys[:, t * LANES:(t + 1) * LANES]
        for a in range(HEADS_PER_TILE):
            h = t * HEADS_PER_TILE + a
            rest = c_ref[:, h:h + 1]
            aug = jnp.zeros((tm, LANES), F32)
            for n in range(FOX_SPLIT):
                piece = rest.astype(BF16).astype(F32)
                rest = rest - piece
                aug = jnp.where(lane == _spare(a, n), piece, aug)
            o_ref[a, :, t * LANES:(t + 1) * LANES] = jnp.where(lane // HEAD_DIM == a, k,
                                                               aug).astype(BF16)


def _fox_keys(x2d, w_k, c_tok, tm):
    T, K = x2d.shape
    D = w_k.shape[1]
    return pl.pallas_call(
        _fox_keys_kernel,
        out_shape=jax.ShapeDtypeStruct((HEADS_PER_TILE, T, D), BF16),
        grid=(T // tm,),
        in_specs=[pl.BlockSpec((tm, K), lambda i: (i, 0)),
                  pl.BlockSpec((K, D), lambda i: (0, 0)),
                  pl.BlockSpec((tm, LANES), lambda i: (i, 0))],
        out_specs=pl.BlockSpec((HEADS_PER_TILE, tm, D), lambda i: (0, i, 0)),
        compiler_params=_cparams("parallel"),
        name="fox_keys",
    )(x2d, w_k, c_tok)


MOBA_SLAB = 16


def _moba_kernel(bfar_ref, qt_ref, k_ref, vt_ref, kmean_ref, bown_ref, badj_ref, o_ref,
                 vaug_ref, sel_ref, stage0_ref, stage1_ref, *, blk):
    t = pl.program_id(1)
    n = pl.program_id(2)
    nb = kmean_ref.shape[1]

    @pl.when(n == 0)
    def _():
        _fill_v_aug(vt_ref, vaug_ref, blk)

    chains = [(h, a) for h in range(GROUP_BLOCKS) for a in range(HEADS_PER_TILE)]
    row = lax.broadcasted_iota(jnp.int32, (LANES, blk), 0)
    brow = lax.broadcasted_iota(jnp.int32, (nb, blk), 0)
    base = {}
    for c, (h, a) in enumerate(chains):
        own = GROUP_BLOCKS * n + h
        qt = qt_ref[:, h * blk:(h + 1) * blk]
        qa = jnp.where(row // HEAD_DIM == a, qt, jnp.zeros_like(qt))
        base[h, a] = qa
        bscore = jnp.dot(kmean_ref[0], qa.astype(F32), preferred_element_type=F32,
                         precision=HIGHEST)
        cand = jnp.where(brow < own, bscore, NEG)
        sel = jnp.zeros((nb, blk), F32)
        for _ in range(MOBA_TOPK):
            mx = jnp.max(cand, axis=0, keepdims=True)
            first = jnp.min(jnp.where(cand == mx, brow, nb), axis=0, keepdims=True)
            hit = brow == first
            sel = jnp.where(hit, 1.0, sel)
            cand = jnp.where(hit, -jnp.inf, cand)
        sel_ref[c] = jnp.where(brow < own, sel, 0.0)

    srow = lax.broadcasted_iota(jnp.int32, (MOBA_SLAB, blk), 0)

    def block(c, j):
        h, a = chains[c]
        head_id = t * HEADS_PER_TILE + a
        jc = jnp.maximum(j, 0)
        start = pl.multiple_of(jc * blk, blk)

        def query():
            s0 = _spare(a) // MOBA_SLAB * MOBA_SLAB
            chosen = (((sel_ref[c, pl.ds(jc, 1), :] > 0.0) & (j >= 0))
                      | (j == GROUP_BLOCKS * n + h))
            slab = jnp.where(srow == _spare(a) - s0,
                             jnp.where(chosen, bfar_ref[0, head_id], NEG),
                             jnp.where(srow == _spare(a, 1) - s0, bfar_ref[1, head_id], 0.0)
                             ).astype(BF16)
            q = base[h, a]
            head = [q[:s0]] if s0 else []
            tail = [q[s0 + MOBA_SLAB:]] if s0 + MOBA_SLAB < LANES else []
            return jnp.concatenate(head + [slab] + tail, axis=0)

        return lambda: k_ref[a, 0, pl.ds(start, blk), :], query, lambda: vaug_ref[a, jc]

    key = lax.broadcasted_iota(jnp.int32, (blk, blk), 0)
    qry = lax.broadcasted_iota(jnp.int32, (blk, blk), 1)
    carries = _attend_masked_last(
        tuple(_scores_init(blk) for _ in chains), n,
        lambda g: [[block(c, 2 * g + h - 1), block(c, 2 * g + h)] for c, (h, _) in enumerate(chains)],
        (stage0_ref, stage1_ref), [[None, key <= qry] for _ in chains],
        [[lambda a=a: badj_ref[a], lambda a=a: bown_ref[a]] for _, a in chains])
    for h in range(GROUP_BLOCKS):
        _finish(carries[h * HEADS_PER_TILE:(h + 1) * HEADS_PER_TILE], o_ref,
                slice(h * blk, (h + 1) * blk))


def _rel_bucket(dist, n_buckets):
    n = jnp.maximum(dist, 0)
    max_exact = n_buckets // 2
    nf = jnp.maximum(n, 1).astype(F32)
    large = max_exact + (jnp.log(nf / max_exact) / math.log(REL_MAX_DIST / max_exact)
                         * (n_buckets - max_exact)).astype(jnp.int32)
    large = jnp.minimum(large, n_buckets - 1)
    return jnp.where(n < max_exact, n, large)


def _moba_attention(qv_t, k_aug, k_mean, rel_bias, B, S):
    D = k_aug.shape[-1]
    n_tiles = D // LANES
    blk = MOBA_BLOCK
    nb = S // blk
    hp = HEADS_PER_TILE
    span = GROUP_BLOCKS * blk
    steps = S // span
    n_chains = GROUP_BLOCKS * hp
    r = jnp.arange(blk)
    delta = r[None, :] - r[:, None]
    bias_t = rel_bias.T.astype(F32)
    n_buckets = rel_bias.shape[0]
    def table(dist):
        onehot = jax.nn.one_hot(_rel_bucket(dist, n_buckets), n_buckets, dtype=F32)
        return jnp.einsum('crn,hn->hcr', onehot, bias_t, precision=HIGHEST)

    b_far = bias_t[:, n_buckets - 1]
    b_own = table(delta) - b_far[:, None, None]
    b_adj = table(delta + blk) - b_far[:, None, None]
    far_hi = b_far.astype(BF16).astype(F32)
    far_lo = (b_far - far_hi).astype(BF16).astype(F32)
    b_far2 = jnp.stack([far_hi, far_lo])
    return pl.pallas_call(
        functools.partial(_moba_kernel, blk=blk),
        out_shape=jax.ShapeDtypeStruct((D, B * S), BF16),
        grid_spec=pltpu.PrefetchScalarGridSpec(
            num_scalar_prefetch=1,
            grid=(B, n_tiles, steps),
            in_specs=[pl.BlockSpec((LANES, span), lambda b, h, i, f: (h, b * steps + i)),
                      pl.BlockSpec((hp, 1, S, LANES), lambda b, h, i, f: (0, b, 0, h)),
                      pl.BlockSpec((LANES, S), lambda b, h, i, f: (n_tiles + h, b)),
                      pl.BlockSpec((1, nb, LANES), lambda b, h, i, f: (b, 0, h)),
                      pl.BlockSpec((hp, blk, blk), lambda b, h, i, f: (h, 0, 0)),
                      pl.BlockSpec((hp, blk, blk), lambda b, h, i, f: (h, 0, 0))],
            out_specs=pl.BlockSpec((LANES, span), lambda b, h, i, f: (h, b * steps + i)),
            scratch_shapes=[pltpu.VMEM((hp, nb, LANES, blk), BF16),
                            pltpu.VMEM((n_chains, nb, blk), F32)] + _stage_scratch(blk, n_chains)),
        compiler_params=_cparams("parallel", "parallel", "arbitrary"),
        name="moba_attention",
    )(b_far2, qv_t, k_aug, qv_t, k_mean, b_own, b_adj)


def _moba_keys_kernel(x_ref, w_ref, o_ref, mean_ref):
    k = jnp.dot(x_ref[...].astype(BF16), w_ref[...], preferred_element_type=F32)
    col = lax.broadcasted_iota(jnp.int32, k.shape, 1)
    lane = col % LANES
    for a in range(HEADS_PER_TILE):
        ones = (lane == _spare(a)) | (lane == _spare(a, 1))
        o_ref[a] = jnp.where((col // HEAD_DIM) % HEADS_PER_TILE == a, k,
                             jnp.where(ones, 1.0, 0.0)).astype(BF16)
    for n in range(mean_ref.shape[0]):
        mean_ref[n] = jnp.mean(k[n * MOBA_BLOCK:(n + 1) * MOBA_BLOCK], axis=0, keepdims=True)


def _moba_keys(x2d, w_k, B, S, tm):
    T, K = x2d.shape
    D = w_k.shape[1]
    nb = S // MOBA_BLOCK
    per_tile = tm // MOBA_BLOCK
    k_aug, k_mean = pl.pallas_call(
        _moba_keys_kernel,
        out_shape=(jax.ShapeDtypeStruct((HEADS_PER_TILE, T, D), BF16),
                   jax.ShapeDtypeStruct((B * nb, 1, D), F32)),
        grid=(T // tm,),
        in_specs=[pl.BlockSpec((tm, K), lambda i: (i, 0)),
                  pl.BlockSpec((K, D), lambda i: (0, 0))],
        out_specs=(pl.BlockSpec((HEADS_PER_TILE, tm, D), lambda i: (0, i, 0)),
                   pl.BlockSpec((per_tile, 1, D), lambda i: (i, 0, 0))),
        compiler_params=_cparams("parallel"),
        name="moba_keys",
    )(x2d, w_k)
    return k_aug.reshape(HEADS_PER_TILE, B, S, D), k_mean.reshape(B, nb, D)


def _router_kernel(x_ref, wt_ref, bias_ref, idx_ref, gate_ref, rank_ref, cnt_ref, run_ref, *, tm):
    @pl.when(pl.program_id(0) == 0)
    def _():
        run_ref[...] = jnp.zeros_like(run_ref)

    E = wt_ref.shape[1]
    gsz = E // N_GROUPS
    logits = _three_pass(x_ref[...], wt_ref, lambda x, w: _nt_dot(w, x))
    s = jax.nn.sigmoid(logits)
    sb = s + bias_ref[...]
    neg_inf = -jnp.inf

    giota = lax.broadcasted_iota(jnp.int32, (gsz, tm), 0)
    gscore = []
    for g in range(N_GROUPS):
        blk = sb[g * gsz:(g + 1) * gsz, :]
        m1 = jnp.max(blk, axis=0, keepdims=True)
        i1 = jnp.min(jnp.where(blk == m1, giota, gsz), axis=0, keepdims=True)
        m2 = jnp.max(jnp.where(giota == i1, neg_inf, blk), axis=0, keepdims=True)
        gscore.append(m1 + m2)

    gsel = [jnp.zeros((1, tm), jnp.bool_) for _ in range(N_GROUPS)]
    for _ in range(TOPK_GROUPS):
        mx = functools.reduce(jnp.maximum, gscore)
        found = jnp.zeros((1, tm), jnp.bool_)
        for g in range(N_GROUPS):
            hit = (gscore[g] == mx) & jnp.logical_not(found)
            gsel[g] = gsel[g] | hit
            found = found | hit
            gscore[g] = jnp.where(hit, neg_inf, gscore[g])
    emask = jnp.concatenate([jnp.broadcast_to(gsel[g], (gsz, tm)) for g in range(N_GROUPS)], axis=0)
    cand = jnp.where(emask, sb, NEG)

    eiota = lax.broadcasted_iota(jnp.int32, (E, tm), 0)
    hits, idxs, ws = [], [], []
    for _ in range(TOP_K):
        mx = jnp.max(cand, axis=0, keepdims=True)
        first = jnp.min(jnp.where(cand == mx, eiota, E), axis=0, keepdims=True)
        hit = eiota == first
        hits.append(hit)
        idxs.append(first)
        ws.append(jnp.sum(jnp.where(hit, s, 0.0), axis=0, keepdims=True))
        cand = jnp.where(hit, neg_inf, cand)
    wsum = functools.reduce(jnp.add, ws)

    chosen = functools.reduce(jnp.logical_or, hits)
    onehot = jnp.where(chosen, 1.0, 0.0)
    tr = lax.broadcasted_iota(jnp.int32, (tm, tm), 0)
    tc = lax.broadcasted_iota(jnp.int32, (tm, tm), 1)
    before = (tr < tc).astype(BF16)
    prior = jnp.dot(onehot.astype(BF16), before, preferred_element_type=F32) + run_ref[...]
    grow = lax.broadcasted_iota(jnp.int32, (LANES, tm), 0)
    gates = jnp.zeros((LANES, tm), F32)
    for k in range(TOP_K):
        idx_ref[k:k + 1, :] = idxs[k]
        gates = jnp.where(grow == k, ws[k] / wsum * ROUTED_SCALE, gates)
        rank_ref[k:k + 1, :] = jnp.sum(jnp.where(hits[k], prior, 0.0), axis=0,
                                       keepdims=True).astype(jnp.int32)
    gate_ref[...] = gates.T
    run_ref[...] = run_ref[...] + jnp.sum(onehot, axis=1, keepdims=True)
    cnt_ref[...] = run_ref[...]


def _router(x2d, w_router, router_bias, tm):
    T, D = x2d.shape
    E = w_router.shape[1]
    tok = lambda i: (0, i)
    fixed = lambda i: (0, 0)
    return pl.pallas_call(
        functools.partial(_router_kernel, tm=tm),
        out_shape=(jax.ShapeDtypeStruct((TOP_K, T), jnp.int32),
                   jax.ShapeDtypeStruct((T, LANES), F32),
                   jax.ShapeDtypeStruct((TOP_K, T), jnp.int32),
                   jax.ShapeDtypeStruct((E, 1), F32)),
        grid=(T // tm,),
        in_specs=[pl.BlockSpec((tm, D), lambda i: (i, 0)),
                  pl.BlockSpec((2, E, D), lambda i: (0, 0, 0)),
                  pl.BlockSpec((E, 1), fixed)],
        out_specs=(pl.BlockSpec((TOP_K, tm), tok), pl.BlockSpec((tm, LANES), lambda i: (i, 0)),
                   pl.BlockSpec((TOP_K, tm), tok), pl.BlockSpec((E, 1), fixed)),
        scratch_shapes=[pltpu.VMEM((E, 1), F32)],
        compiler_params=_cparams("arbitrary"),
        name="router",
    )(x2d, jnp.stack(_split_bf16(w_router.T, 2)), router_bias.reshape(E, 1).astype(F32))


def _slots_kernel(idx_ref, rank_ref, start_ref, o_ref):
    E = start_ref.shape[0]
    tm = idx_ref.shape[1]
    eiota = lax.broadcasted_iota(jnp.int32, (E, tm), 0)
    start = start_ref[...]
    for k in range(TOP_K):
        base = jnp.sum(jnp.where(eiota == idx_ref[k:k + 1, :], start, 0.0), axis=0, keepdims=True)
        o_ref[k:k + 1, :] = base.astype(jnp.int32) + rank_ref[k:k + 1, :]


def _slots(idx, rank, seg_start, tm):
    T = idx.shape[1]
    E = seg_start.shape[0]
    tok = lambda i: (0, i)
    return pl.pallas_call(
        _slots_kernel,
        out_shape=jax.ShapeDtypeStruct((TOP_K, T), jnp.int32),
        grid=(T // tm,),
        in_specs=[pl.BlockSpec((TOP_K, tm), tok), pl.BlockSpec((TOP_K, tm), tok),
                  pl.BlockSpec((E, 1), lambda i: (0, 0))],
        out_specs=pl.BlockSpec((TOP_K, tm), tok),
        compiler_params=_cparams("parallel"),
        name="moe_slots",
    )(idx, rank, seg_start.astype(F32).reshape(E, 1))


SC_CORES = 2
SC_SUBCORES = 16
SC_CHUNK = 128


def _sc_rows_kernel(body, out_shape, idx_block, width, dtype, name):
    mesh = plsc.VectorSubcoreMesh(core_axis_name="c", subcore_axis_name="s",
                                  num_cores=SC_CORES, num_subcores=SC_SUBCORES)
    return pl.kernel(
        body, mesh=mesh, out_type=jax.ShapeDtypeStruct(out_shape, dtype),
        scratch_types=[pltpu.VMEM(idx_block + (SC_CHUNK,), jnp.int32),
                       pltpu.VMEM((SC_CHUNK, width), dtype),
                       pltpu.SemaphoreType.DMA],
        name=name)


def _sc_chunks(n):
    workers = SC_CORES * SC_SUBCORES
    chunks = n // (workers * SC_CHUNK)
    assert chunks * workers * SC_CHUNK == n
    return workers, chunks


def _sc_gather_rows(table, idx):
    workers, chunks = _sc_chunks(idx.shape[0])

    def body(table_hbm, idx_hbm, out_hbm, idx_v, rows_v, sem):
        wid = lax.axis_index("s") * SC_CORES + lax.axis_index("c")
        pltpu.sync_copy(idx_hbm.at[wid], idx_v)

        @pl.loop(0, chunks)
        def _(j):
            pltpu.async_copy(table_hbm.at[idx_v.at[j]], rows_v, sem).wait()
            pltpu.sync_copy(rows_v, out_hbm.at[pl.ds((wid * chunks + j) * SC_CHUNK, SC_CHUNK)])

    call = _sc_rows_kernel(body, (idx.shape[0], table.shape[1]), (chunks,), table.shape[1],
                           table.dtype, "moe_gather_sc")
    return call(table, idx.reshape(workers, chunks, SC_CHUNK))


def _sc_scatter_rows(src, dest, n_rows):
    copies, n_src = dest.shape
    workers, chunks = _sc_chunks(n_src)

    def body(src_hbm, idx_hbm, out_hbm, idx_v, rows_v, sem):
        wid = lax.axis_index("s") * SC_CORES + lax.axis_index("c")
        pltpu.sync_copy(idx_hbm.at[wid], idx_v)

        @pl.loop(0, chunks)
        def _(j):
            pltpu.sync_copy(src_hbm.at[pl.ds((wid * chunks + j) * SC_CHUNK, SC_CHUNK)], rows_v)
            started = [pltpu.async_copy(rows_v, out_hbm.at[idx_v.at[j, k]], sem)
                       for k in range(copies)]
            for copy in started:
                copy.wait()

    call = _sc_rows_kernel(body, (n_rows, src.shape[1]), (chunks, copies), src.shape[1],
                           src.dtype, "moe_scatter_sc")
    idx = dest.reshape(copies, workers, chunks, SC_CHUNK).transpose(1, 2, 0, 3)
    return call(src, idx)


def _silu(g):
    return g * jax.nn.sigmoid(g)


SCHED_FIRST, SCHED_SLOT, SCHED_NEXT, SCHED_VALID = 0, 1, 2, 3


def _expert_kernel(sched_ref, n_used_ref, x_ref, wg_hbm, wu_hbm, wd_hbm, o_ref,
                   wg_buf, wu_buf, wd_buf, sem, *, layer, first_expert_row):
    i = pl.program_id(0)
    slot = sched_ref[SCHED_SLOT, i]

    def copies(expert, slot):
        pairs = ((wg_hbm, wg_buf), (wu_hbm, wu_buf), (wd_hbm, wd_buf))
        return [pltpu.make_async_copy(src.at[layer, expert], dst.at[slot], sem.at[slot, n])
                for n, (src, dst) in enumerate(pairs)]

    @pl.when(i == 0)
    def _():
        for c in copies(sched_ref[first_expert_row, 0], 0):
            c.start()

    @pl.when(sched_ref[SCHED_FIRST, i] == 1)
    def _():
        for c in copies(0, slot):
            c.wait()

        @pl.when(sched_ref[SCHED_NEXT, i] >= 0)
        def _():
            for c in copies(sched_ref[SCHED_NEXT, i], 1 - slot):
                c.start()

    @pl.when(i < n_used_ref[0])
    def _():
        words = x_ref[...]
        row = lax.broadcasted_iota(jnp.int32, words.shape, 0)
        words = jnp.where(row < sched_ref[SCHED_VALID, i], words, 0)
        x = jnp.concatenate(_unpack_halves(words), axis=1).astype(BF16)
        g = jnp.dot(x, wg_buf[slot].astype(BF16), preferred_element_type=F32)
        u = jnp.dot(x, wu_buf[slot].astype(BF16), preferred_element_type=F32)
        a = (_silu(g) * u).astype(BF16)
        o_ref[...] = _pack_halves(jnp.dot(a, wd_buf[slot].astype(BF16), preferred_element_type=F32))

    @pl.when(i >= n_used_ref[0])
    def _():
        o_ref[...] = jnp.zeros_like(o_ref)


def _expert_schedule(seg_start, seg_end, counts, n_blocks):
    E = seg_end.shape[0]
    first_row = jnp.arange(n_blocks, dtype=jnp.int32) * EXPERT_ROWS
    blk_e = jnp.minimum(jnp.sum(seg_end[None, :] <= first_row[:, None], axis=1), E - 1)
    used = first_row < seg_end[-1]
    first = used & (first_row == seg_start[blk_e])
    ordinal = jnp.cumsum(first) - 1
    ids = jnp.where(seg_end > seg_start, jnp.arange(E), E)
    later = lax.cummin(ids, reverse=True)
    nxt = jnp.concatenate([later[1:], jnp.full((1,), E, later.dtype)])[blk_e]
    nxt = jnp.where(nxt < E, nxt, -1)
    valid = jnp.clip((seg_start + counts)[blk_e] - first_row, 0, EXPERT_ROWS)
    return jnp.stack([first, ordinal % 2, nxt, valid, blk_e]).astype(jnp.int32)


def _expert_mlp(xs, sched, n_used, w_gate, w_up, w_down, layer):
    n_rows, half = xs.shape
    D = 2 * half
    F = w_gate.shape[-1]
    tm = EXPERT_ROWS
    hbm = pl.BlockSpec(memory_space=pl.ANY)
    return pl.pallas_call(
        functools.partial(_expert_kernel, layer=layer, first_expert_row=sched.shape[0] - 1),
        out_shape=jax.ShapeDtypeStruct((n_rows, half), jnp.int32),
        grid_spec=pltpu.PrefetchScalarGridSpec(
            num_scalar_prefetch=2,
            grid=(n_rows // tm,),
            in_specs=[pl.BlockSpec((tm, half), lambda i, sc, nu: (jnp.minimum(i, nu[0] - 1), 0)),
                      hbm, hbm, hbm],
            out_specs=pl.BlockSpec((tm, half), lambda i, sc, nu: (i, 0)),
            scratch_shapes=[pltpu.VMEM((2, D, F), F32), pltpu.VMEM((2, D, F), F32),
                            pltpu.VMEM((2, F, D), F32), pltpu.SemaphoreType.DMA((2, 3))]),
        compiler_params=_cparams("arbitrary"),
        name="moe_experts",
    )(sched, n_used, xs, w_gate, w_up, w_down)


def _combine_kernel(x_ref, gate_ref, rows_ref, wsg_ref, wsu_ref, wsd_ref, g_ref, b_ref, o_ref,
                    *, alpha):
    x = x_ref[...]
    xb = x.astype(BF16)
    g = jnp.dot(xb, wsg_ref[...], preferred_element_type=F32)
    u = jnp.dot(xb, wsu_ref[...], preferred_element_type=F32)
    y = jnp.dot((_silu(g) * u).astype(BF16), wsd_ref[...], preferred_element_type=F32)
    gates = gate_ref[...]
    half = y.shape[1] // 2
    left, right = y[:, :half], y[:, half:]
    for k in range(TOP_K):
        lo, hi = _unpack_halves(rows_ref[k])
        left = left + lo * gates[:, k:k + 1]
        right = right + hi * gates[:, k:k + 1]
    y = jnp.concatenate([left, right], axis=1)
    o_ref[...] = _layer_norm(alpha * x + y, g_ref[...], b_ref[...])


def _combine(x2d, ys, dest, gates_tok, ws_gate, ws_up, ws_down, g, b, alpha, tt):
    T, D = x2d.shape
    F = ws_gate.shape[1]
    row = lambda i: (i, 0)
    fixed = lambda i: (0, 0)
    rows = _sc_gather_rows(ys, dest.reshape(-1)).reshape(TOP_K, T, D // 2)
    return pl.pallas_call(
        functools.partial(_combine_kernel, alpha=alpha),
        out_shape=jax.ShapeDtypeStruct((T, D), F32),
        grid=(T // tt,),
        in_specs=[pl.BlockSpec((tt, D), row),
                  pl.BlockSpec((tt, LANES), row),
                  pl.BlockSpec((TOP_K, tt, D // 2), lambda i: (0, i, 0)),
                  pl.BlockSpec((D, F), fixed), pl.BlockSpec((D, F), fixed),
                  pl.BlockSpec((F, D), fixed),
                  pl.BlockSpec((1, D), fixed), pl.BlockSpec((1, D), fixed)],
        out_specs=pl.BlockSpec((tt, D), row),
        compiler_params=_cparams("parallel"),
        name="moe_combine",
    )(x2d, gates_tok, rows, ws_gate.astype(BF16), ws_up.astype(BF16), ws_down.astype(BF16),
      g.reshape(1, D), b.reshape(1, D))


def _pick(n, pref):
    t = min(n, pref)
    while n % t:
        t //= 2
    return t


def _moe_layer(x2d, packed, w_router, router_bias, w_gate, w_up, w_down, layer, ws_gate, ws_up,
               ws_down, g, b, alpha):
    T, D = x2d.shape
    E = w_router.shape[1]
    idx, gates, rank, counts = _router(x2d, w_router, router_bias, _pick(T, 256))
    counts = counts[:, 0].astype(jnp.int32)
    padded = (counts + EXPERT_ROWS - 1) // EXPERT_ROWS * EXPERT_ROWS
    seg_end = jnp.cumsum(padded)
    seg_start = seg_end - padded
    n_blocks = (T * TOP_K + E * (EXPERT_ROWS - 1)) // EXPERT_ROWS
    dest = _slots(idx, rank, seg_start, _pick(T, 1024))
    sched = _expert_schedule(seg_start, seg_end, counts, n_blocks)
    n_used = (seg_end[-1:] // EXPERT_ROWS).astype(jnp.int32)
    xs = _sc_scatter_rows(packed, dest, n_blocks * EXPERT_ROWS)
    ys = _expert_mlp(xs, sched, n_used, w_gate, w_up, w_down, layer)
    return _combine(x2d, ys, dest, gates, ws_gate, ws_up, ws_down, g, b, alpha, _pick(T, 256))


def _mixer_layer(x2d, i, B, S, p, alpha):
    T, D = x2d.shape
    tm = _pick(T, 512)
    tn = _pick(D, 1024)
    j = i // 2
    w_in = p["moba_w_in"][j] if i % 2 == 0 else p["fox_w_in"][j]
    w_qv_t = jnp.concatenate([w_in[:, :D].T * HEAD_DIM ** -0.5, w_in[:, 2 * D:3 * D].T]).astype(BF16)
    qv_t = _project_t(x2d, w_qv_t, tm, tn)
    w_k = w_in[:, D:2 * D].astype(BF16)
    if i % 2 == 0:
        k_aug, k_mean = _moba_keys(x2d, w_k, B, S, tm)
        attn_t = _moba_attention(qv_t, k_aug, k_mean, p["rel_bias"], B, S)
        w_out = p["moba_w_out"][j]
    else:
        c_tok = _fox_gates(x2d.reshape(B, S, D), w_in[:, 3 * D:], p["fox_b_f"][j], _pick(S, 512))
        k_aug = _fox_keys(x2d, w_k, c_tok.reshape(T, LANES), tm).reshape(HEADS_PER_TILE, B, S, D)
        attn_t = _fox_attention(qv_t, k_aug, B, S, _pick(S, 256))
        w_out = p["fox_w_out"][j]
    return _outproj_ln(attn_t, w_out.astype(BF16), x2d, p["ln1_g"][i], p["ln1_b"][i], alpha, tm)


def kernel(x, rel_bias, moba_w_in, moba_w_out, fox_w_in, fox_b_f, fox_w_out, ln1_g, ln1_b, ln2_g,
           ln2_b, w_router, router_bias, w_gate, w_up, w_down, ws_gate, ws_up, ws_down):
    B, S, D = x.shape
    depth = ln1_g.shape[0]
    alpha = (2 * depth) ** 0.25
    p = dict(rel_bias=rel_bias, moba_w_in=moba_w_in, moba_w_out=moba_w_out, fox_w_in=fox_w_in,
             fox_b_f=fox_b_f, fox_w_out=fox_w_out, ln1_g=ln1_g, ln1_b=ln1_b)
    x2d = x.reshape(B * S, D)
    for i in range(depth):
        x2d, packed = _mixer_layer(x2d, i, B, S, p, alpha)
        x2d = _moe_layer(x2d, packed, w_router[i], router_bias[i], w_gate, w_up, w_down, i,
                         ws_gate[i], ws_up[i], ws_down[i], ln2_g[i], ln2_b[i], alpha)
    return x2d.reshape(B, S, D)
```

```python
import functools
import math

import jax
import jax.numpy as jnp
from jax import lax
from jax.experimental import pallas as pl
from jax.experimental.pallas import tpu as pltpu
from jax.experimental.pallas import tpu_sc as plsc

F32 = jnp.float32
BF16 = jnp.bfloat16
HIGHEST = lax.Precision.HIGHEST

HEAD_DIM = 64
MOBA_BLOCK = 256
MOBA_TOPK = 3
REL_MAX_DIST = 128
TOP_K = 8
N_GROUPS = 8
TOPK_GROUPS = 4
ROUTED_SCALE = 2.5
LN_EPS = 1e-5
NEG = -1e30

LANES = 128
HEADS_PER_TILE = LANES // HEAD_DIM
EXPERT_ROWS = 512
VMEM_LIMIT = 48 * 1024 * 1024


def _cparams(*sem):
    return pltpu.CompilerParams(dimension_semantics=sem, vmem_limit_bytes=VMEM_LIMIT)


def _nt_dot(a, b, **kw):
    return lax.dot_general(a, b, (((1,), (1,)), ((), ())), preferred_element_type=F32, **kw)


def _split_bf16(x, pieces):
    out = []
    for _ in range(pieces):
        out.append(x.astype(BF16))
        x = x - out[-1].astype(F32)
    return out


def _three_pass(x, w_ref, mul):
    hi, lo = _split_bf16(x, 2)
    return mul(hi, w_ref[0]) + mul(lo, w_ref[0]) + mul(hi, w_ref[1])


def _proj_t_kernel(x_ref, wt_ref, o_ref):
    o_ref[...] = _nt_dot(wt_ref[...], x_ref[...].astype(BF16)).astype(o_ref.dtype)


def _project_t(x2d, wt, tm, tn):
    T, K = x2d.shape
    N = wt.shape[0]
    return pl.pallas_call(
        _proj_t_kernel,
        out_shape=jax.ShapeDtypeStruct((N, T), BF16),
        grid=(T // tm, N // tn),
        in_specs=[pl.BlockSpec((tm, K), lambda i, j: (i, 0)),
                  pl.BlockSpec((tn, K), lambda i, j: (j, 0))],
        out_specs=pl.BlockSpec((tn, tm), lambda i, j: (j, i)),
        compiler_params=_cparams("parallel", "arbitrary"),
        name="qv_proj_t",
    )(x2d, wt)


def _layer_norm(r, g, b):
    mu = jnp.mean(r, axis=-1, keepdims=True)
    d = r - mu
    var = jnp.mean(d * d, axis=-1, keepdims=True)
    return d * lax.rsqrt(var + LN_EPS) * g + b


def _pack_halves(y):
    n = y.shape[1] // 2
    bits = lambda v: lax.bitcast_convert_type(v.astype(BF16).astype(F32), jnp.int32)
    left = bits(y[:, :n])
    return lax.shift_right_logical(left, jnp.full_like(left, 16)) | bits(y[:, n:])


def _unpack_halves(p):
    return (lax.bitcast_convert_type(p << 16, F32),
            lax.bitcast_convert_type(p & jnp.int32(-65536), F32))


def _outproj_ln_kernel(at_ref, w_ref, x_ref, g_ref, b_ref, o_ref, packed_ref, *, alpha):
    y = lax.dot_general(at_ref[...], w_ref[...], (((0,), (0,)), ((), ())),
                        preferred_element_type=F32)
    out = _layer_norm(alpha * x_ref[...] + y, g_ref[...], b_ref[...])
    o_ref[...] = out
    packed_ref[...] = _pack_halves(out)


def _outproj_ln(attn_t, w_out, x2d, g, b, alpha, tm):
    T, D = x2d.shape
    row = lambda i: (i, 0)
    fixed = lambda i: (0, 0)
    return pl.pallas_call(
        functools.partial(_outproj_ln_kernel, alpha=alpha),
        out_shape=(jax.ShapeDtypeStruct((T, D), F32), jax.ShapeDtypeStruct((T, D // 2), jnp.int32)),
        grid=(T // tm,),
        in_specs=[pl.BlockSpec((D, tm), lambda i: (0, i)), pl.BlockSpec((D, D), fixed),
                  pl.BlockSpec((tm, D), row), pl.BlockSpec((1, D), fixed),
                  pl.BlockSpec((1, D), fixed)],
        out_specs=(pl.BlockSpec((tm, D), row), pl.BlockSpec((tm, D // 2), row)),
        compiler_params=_cparams("parallel"),
        name="outproj_ln",
    )(attn_t, w_out, x2d, g.reshape(1, D), b.reshape(1, D))


def _spare(a, n=0):
    return ((a + 1) % HEADS_PER_TILE) * HEAD_DIM + n


def _scores(chains):
    return [[jnp.dot(keys(), query(), preferred_element_type=F32) for keys, query, _ in blocks]
            for blocks in chains]


def _absorb(carries, scores, chains, masks=None, adds=None):
    probs = []
    for c, ((m, _), parts) in enumerate(zip(carries, scores)):
        if adds is not None:
            parts = [s if add is None else s + add() for add, s in zip(adds[c], parts)]
        if masks is not None:
            parts = [s if keep is None else jnp.where(keep, s, NEG)
                     for keep, s in zip(masks[c], parts)]
        tops = [jnp.max(s, axis=0, keepdims=True) for s in parts]
        m_new = functools.reduce(jnp.maximum, tops, m)
        probs.append((m_new, jnp.exp(m - m_new), [jnp.exp(s - m_new).astype(BF16) for s in parts]))
    out = []
    for (_, acc), (m_new, decay, ps), blocks in zip(carries, probs, chains):
        acc = decay * acc
        for (_, _, values), p in zip(blocks, ps):
            acc = acc + jnp.dot(values(), p, preferred_element_type=F32)
        out.append((m_new, acc))
    return tuple(out)


def _stash(chains, ref):
    for a, parts in enumerate(_scores(chains)):
        for b, s in enumerate(parts):
            ref[a, b] = s


def _fetch(chains, ref):
    return [[ref[a, b] for b in range(len(blocks))] for a, blocks in enumerate(chains)]


def _attend_masked_last(carries, n, group, stage_refs, masks, adds=None):
    first, second = stage_refs
    _stash(group(0), first)
    trips = n // 2

    def body(t, carries):
        g = 2 * t
        _stash(group(g + 1), second)
        carries = _absorb(carries, _fetch(group(g), first), group(g))
        _stash(group(g + 2), first)
        return _absorb(carries, _fetch(group(g + 1), second), group(g + 1))

    carries = lax.fori_loop(0, trips, body, carries)

    def odd_tail(carries):
        _stash(group(n), second)
        carries = _absorb(carries, _fetch(group(n - 1), first), group(n - 1))
        return _absorb(carries, _fetch(group(n), second), group(n), masks, adds)

    def even_tail(carries):
        return _absorb(carries, _fetch(group(n), first), group(n), masks, adds)

    return lax.cond(n % 2 == 1, odd_tail, even_tail, carries)


GROUP_BLOCKS = 2


def _stage_scratch(tq, chains=HEADS_PER_TILE):
    return [pltpu.VMEM((chains, GROUP_BLOCKS, tq, tq), F32) for _ in range(2)]


def _scores_init(tq):
    return jnp.full((1, tq), -jnp.inf, F32), jnp.zeros((LANES, tq), F32)


def _finish(carries, o_ref, cols=slice(None)):
    row = lax.broadcasted_iota(jnp.int32, carries[0][1].shape, 0)
    outs = []
    for a, (_, acc) in enumerate(carries):
        s = _spare(a)
        outs.append(acc / acc[s:s + 1, :])
    o_ref[:, cols] = jnp.where(row < HEAD_DIM, outs[0], outs[1]).astype(o_ref.dtype)


def _fill_v_aug(vt_ref, vaug_ref, tk):
    n_tiles = vt_ref.shape[1] // tk
    row = lax.broadcasted_iota(jnp.int32, (LANES, tk), 0)
    for n in range(n_tiles):
        v = vt_ref[:, n * tk:(n + 1) * tk]
        for a in range(HEADS_PER_TILE):
            vaug_ref[a, n] = jnp.where(row // HEAD_DIM == a, v, jnp.ones_like(v))


FOX_SPLIT = 3


def _fox_kernel(qt_ref, k_ref, vt_ref, o_ref, vaug_ref, stage0_ref, stage1_ref, *, tq):
    n = pl.program_id(2)

    @pl.when(n == 0)
    def _():
        _fill_v_aug(vt_ref, vaug_ref, tq)

    chains = [(h, a) for h in range(GROUP_BLOCKS) for a in range(HEADS_PER_TILE)]
    row = lax.broadcasted_iota(jnp.int32, (LANES, tq), 0)
    queries = {}
    for h, a in chains:
        offs = (row >= _spare(a)) & (row < _spare(a, FOX_SPLIT))
        queries[h, a] = jnp.where(row // HEAD_DIM == a, qt_ref[:, h * tq:(h + 1) * tq],
                                  jnp.where(offs, -1.0, 0.0).astype(BF16))

    def block(h, a, j):
        start = pl.multiple_of(j * tq, tq)
        return (lambda: k_ref[a, 0, pl.ds(start, tq), :], lambda: queries[h, a],
                lambda: vaug_ref[a, j])

    key = lax.broadcasted_iota(jnp.int32, (tq, tq), 0)
    qry = lax.broadcasted_iota(jnp.int32, (tq, tq), 1)
    masks = [[None if b < h else key + (b - h) * tq <= qry for b in range(GROUP_BLOCKS)]
             for h, _ in chains]
    carries = _attend_masked_last(
        tuple(_scores_init(tq) for _ in chains), n,
        lambda j: [[block(h, a, 2 * j), block(h, a, 2 * j + 1)] for h, a in chains],
        (stage0_ref, stage1_ref), masks)
    for h in range(GROUP_BLOCKS):
        _finish(carries[h * HEADS_PER_TILE:(h + 1) * HEADS_PER_TILE], o_ref,
                slice(h * tq, (h + 1) * tq))


def _fox_attention(qv_t, k_aug, B, S, tq):
    D = k_aug.shape[-1]
    n_tiles = D // LANES
    nq = S // tq
    span = GROUP_BLOCKS * tq
    steps = S // span
    return pl.pallas_call(
        functools.partial(_fox_kernel, tq=tq),
        out_shape=jax.ShapeDtypeStruct((D, B * S), BF16),
        grid=(B, n_tiles, steps),
        in_specs=[pl.BlockSpec((LANES, span), lambda b, h, i: (h, b * steps + i)),
                  pl.BlockSpec((HEADS_PER_TILE, 1, S, LANES), lambda b, h, i: (0, b, 0, h)),
                  pl.BlockSpec((LANES, S), lambda b, h, i: (n_tiles + h, b))],
        out_specs=pl.BlockSpec((LANES, span), lambda b, h, i: (h, b * steps + i)),
        scratch_shapes=[pltpu.VMEM((HEADS_PER_TILE, nq, LANES, tq), BF16)]
        + _stage_scratch(tq, GROUP_BLOCKS * HEADS_PER_TILE),
        compiler_params=_cparams("parallel", "parallel", "arbitrary"),
        name="fox_attention",
    )(qv_t, k_aug, qv_t)


def _log_sigmoid(z):
    return jnp.minimum(z, 0.0) - jnp.log1p(jnp.exp(-jnp.abs(z)))


def _fox_gates_kernel(x_ref, w_ref, b_ref, o_ref, carry_ref, *, ts):
    @pl.when(pl.program_id(1) == 0)
    def _():
        carry_ref[...] = jnp.zeros_like(carry_ref)

    z = _three_pass(x_ref[0], w_ref, lambda x, w: jnp.dot(x, w, preferred_element_type=F32))
    lf = _log_sigmoid(z + b_ref[...])
    r = lax.broadcasted_iota(jnp.int32, (ts, ts), 0)
    c = lax.broadcasted_iota(jnp.int32, (ts, ts), 1)
    tri = (c <= r).astype(BF16)
    cum = carry_ref[...]
    for piece in _split_bf16(lf, 3):
        cum = cum + jnp.dot(tri, piece, preferred_element_type=F32)
    carry_ref[...] = cum[ts - 1:ts, :]
    o_ref[0] = cum


def _fox_gates(x3d, w_f, b_f, ts):
    B, S, D = x3d.shape
    H = w_f.shape[1]
    w_pad = jnp.stack(_split_bf16(jnp.zeros((D, LANES), F32).at[:, :H].set(w_f), 2))
    b_pad = jnp.zeros((1, LANES), F32).at[0, :H].set(b_f)
    return pl.pallas_call(
        functools.partial(_fox_gates_kernel, ts=ts),
        out_shape=jax.ShapeDtypeStruct((B, S, LANES), F32),
        grid=(B, S // ts),
        in_specs=[pl.BlockSpec((1, ts, D), lambda b, s: (b, s, 0)),
                  pl.BlockSpec((2, D, LANES), lambda b, s: (0, 0, 0)),
                  pl.BlockSpec((1, LANES), lambda b, s: (0, 0))],
        out_specs=pl.BlockSpec((1, ts, LANES), lambda b, s: (b, s, 0)),
        scratch_shapes=[pltpu.VMEM((1, LANES), F32)],
        compiler_params=_cparams("parallel", "arbitrary"),
        name="fox_gates",
    )(x3d, w_pad, b_pad)


def _fox_keys_kernel(x_ref, w_ref, c_ref, o_ref):
    keys = jnp.dot(x_ref[...].astype(BF16), w_ref[...], preferred_element_type=F32)
    tm, D = keys.shape
    lane = lax.broadcasted_iota(jnp.int32, (tm, LANES), 1)
    for t in range(D // LANES):
        k = keys[:, t * LANES:(t + 1) * LANES]
        for a in range(HEADS_PER_TILE):
            h = t * HEADS_PER_TILE + a
            rest = c_ref[:, h:h + 1]
            aug = jnp.zeros((tm, LANES), F32)
            for n in range(FOX_SPLIT):
                piece = rest.astype(BF16).astype(F32)
                rest = rest - piece
                aug = jnp.where(lane == _spare(a, n), piece, aug)
            o_ref[a, :, t * LANES:(t + 1) * LANES] = jnp.where(lane // HEAD_DIM == a, k,
                                                               aug).astype(BF16)


def _fox_keys(x2d, w_k, c_tok, tm):
    T, K = x2d.shape
    D = w_k.shape[1]
    return pl.pallas_call(
        _fox_keys_kernel,
        out_shape=jax.ShapeDtypeStruct((HEADS_PER_TILE, T, D), BF16),
        grid=(T // tm,),
        in_specs=[pl.BlockSpec((tm, K), lambda i: (i, 0)),
                  pl.BlockSpec((K, D), lambda i: (0, 0)),
                  pl.BlockSpec((tm, LANES), lambda i: (i, 0))],
        out_specs=pl.BlockSpec((HEADS_PER_TILE, tm, D), lambda i: (0, i, 0)),
        compiler_params=_cparams("parallel"),
        name="fox_keys",
    )(x2d, w_k, c_tok)


MOBA_SLAB = 16


def _moba_kernel(bfar_ref, qt_ref, k_ref, vt_ref, kmean_ref, bown_ref, badj_ref, o_ref,
                 vaug_ref, sel_ref, stage0_ref, stage1_ref, *, blk):
    t = pl.program_id(1)
    n = pl.program_id(2)
    nb = kmean_ref.shape[1]

    @pl.when(n == 0)
    def _():
        _fill_v_aug(vt_ref, vaug_ref, blk)

    chains = [(h, a) for h in range(GROUP_BLOCKS) for a in range(HEADS_PER_TILE)]
    row = lax.broadcasted_iota(jnp.int32, (LANES, blk), 0)
    brow = lax.broadcasted_iota(jnp.int32, (nb, blk), 0)
    base = {}
    for c, (h, a) in enumerate(chains):
        own = GROUP_BLOCKS * n + h
        qt = qt_ref[:, h * blk:(h + 1) * blk]
        qa = jnp.where(row // HEAD_DIM == a, qt, jnp.zeros_like(qt))
        base[h, a] = qa
        bscore = jnp.dot(kmean_ref[0], qa.astype(F32), preferred_element_type=F32,
                         precision=HIGHEST)
        cand = jnp.where(brow < own, bscore, NEG)
        sel = jnp.zeros((nb, blk), F32)
        for _ in range(MOBA_TOPK):
            mx = jnp.max(cand, axis=0, keepdims=True)
            first = jnp.min(jnp.where(cand == mx, brow, nb), axis=0, keepdims=True)
            hit = brow == first
            sel = jnp.where(hit, 1.0, sel)
            cand = jnp.where(hit, -jnp.inf, cand)
        sel_ref[c] = jnp.where(brow < own, sel, 0.0)

    srow = lax.broadcasted_iota(jnp.int32, (MOBA_SLAB, blk), 0)

    def block(c, j):
        h, a = chains[c]
        head_id = t * HEADS_PER_TILE + a
        jc = jnp.maximum(j, 0)
        start = pl.multiple_of(jc * blk, blk)

        def query():
            s0 = _spare(a) // MOBA_SLAB * MOBA_SLAB
            chosen = (((sel_ref[c, pl.ds(jc, 1), :] > 0.0) & (j >= 0))
                      | (j == GROUP_BLOCKS * n + h))
            slab = jnp.where(srow == _spare(a) - s0,
                             jnp.where(chosen, bfar_ref[0, head_id], NEG),
                             jnp.where(srow == _spare(a, 1) - s0, bfar_ref[1, head_id], 0.0)
                             ).astype(BF16)
            q = base[h, a]
            head = [q[:s0]] if s0 else []
            tail = [q[s0 + MOBA_SLAB:]] if s0 + MOBA_SLAB < LANES else []
            return jnp.concatenate(head + [slab] + tail, axis=0)

        return lambda: k_ref[a, 0, pl.ds(start, blk), :], query, lambda: vaug_ref[a, jc]

    key = lax.broadcasted_iota(jnp.int32, (blk, blk), 0)
    qry = lax.broadcasted_iota(jnp.int32, (blk, blk), 1)
    carries = _attend_masked_last(
        tuple(_scores_init(blk) for _ in chains), n,
        lambda g: [[block(c, 2 * g + h - 1), block(c, 2 * g + h)] for c, (h, _) in enumerate(chains)],
        (stage0_ref, stage1_ref), [[None, key <= qry] for _ in chains],
        [[lambda a=a: badj_ref[a], lambda a=a: bown_ref[a]] for _, a in chains])
    for h in range(GROUP_BLOCKS):
        _finish(carries[h * HEADS_PER_TILE:(h + 1) * HEADS_PER_TILE], o_ref,
                slice(h * blk, (h + 1) * blk))


def _rel_bucket(dist, n_buckets):
    n = jnp.maximum(dist, 0)
    max_exact = n_buckets // 2
    nf = jnp.maximum(n, 1).astype(F32)
    large = max_exact + (jnp.log(nf / max_exact) / math.log(REL_MAX_DIST / max_exact)
                         * (n_buckets - max_exact)).astype(jnp.int32)
    large = jnp.minimum(large, n_buckets - 1)
    return jnp.where(n < max_exact, n, large)


def _moba_attention(qv_t, k_aug, k_mean, rel_bias, B, S):
    D = k_aug.shape[-1]
    n_tiles = D // LANES
    blk = MOBA_BLOCK
    nb = S // blk
    hp = HEADS_PER_TILE
    span = GROUP_BLOCKS * blk
    steps = S // span
    n_chains = GROUP_BLOCKS * hp
    r = jnp.arange(blk)
    delta = r[None, :] - r[:, None]
    bias_t = rel_bias.T.astype(F32)
    n_buckets = rel_bias.shape[0]
    def table(dist):
        onehot = jax.nn.one_hot(_rel_bucket(dist, n_buckets), n_buckets, dtype=F32)
        return jnp.einsum('crn,hn->hcr', onehot, bias_t, precision=HIGHEST)

    b_far = bias_t[:, n_buckets - 1]
    b_own = table(delta) - b_far[:, None, None]
    b_adj = table(delta + blk) - b_far[:, None, None]
    far_hi = b_far.astype(BF16).astype(F32)
    far_lo = (b_far - far_hi).astype(BF16).astype(F32)
    b_far2 = jnp.stack([far_hi, far_lo])
    return pl.pallas_call(
        functools.partial(_moba_kernel, blk=blk),
        out_shape=jax.ShapeDtypeStruct((D, B * S), BF16),
        grid_spec=pltpu.PrefetchScalarGridSpec(
            num_scalar_prefetch=1,
            grid=(B, n_tiles, steps),
            in_specs=[pl.BlockSpec((LANES, span), lambda b, h, i, f: (h, b * steps + i)),
                      pl.BlockSpec((hp, 1, S, LANES), lambda b, h, i, f: (0, b, 0, h)),
                      pl.BlockSpec((LANES, S), lambda b, h, i, f: (n_tiles + h, b)),
                      pl.BlockSpec((1, nb, LANES), lambda b, h, i, f: (b, 0, h)),
                      pl.BlockSpec((hp, blk, blk), lambda b, h, i, f: (h, 0, 0)),
                      pl.BlockSpec((hp, blk, blk), lambda b, h, i, f: (h, 0, 0))],
            out_specs=pl.BlockSpec((LANES, span), lambda b, h, i, f: (h, b * steps + i)),
            scratch_shapes=[pltpu.VMEM((hp, nb, LANES, blk), BF16),
                            pltpu.VMEM((n_chains, nb, blk), F32)] + _stage_scratch(blk, n_chains)),
        compiler_params=_cparams("parallel", "parallel", "arbitrary"),
        name="moba_attention",
    )(b_far2, qv_t, k_aug, qv_t, k_mean, b_own, b_adj)


def _moba_keys_kernel(x_ref, w_ref, o_ref, mean_ref):
    k = jnp.dot(x_ref[...].astype(BF16), w_ref[...], preferred_element_type=F32)
    col = lax.broadcasted_iota(jnp.int32, k.shape, 1)
    lane = col % LANES
    for a in range(HEADS_PER_TILE):
        ones = (lane == _spare(a)) | (lane == _spare(a, 1))
        o_ref[a] = jnp.where((col // HEAD_DIM) % HEADS_PER_TILE == a, k,
                             jnp.where(ones, 1.0, 0.0)).astype(BF16)
    for n in range(mean_ref.shape[0]):
        mean_ref[n] = jnp.mean(k[n * MOBA_BLOCK:(n + 1) * MOBA_BLOCK], axis=0, keepdims=True)


def _moba_keys(x2d, w_k, B, S, tm):
    T, K = x2d.shape
    D = w_k.shape[1]
    nb = S // MOBA_BLOCK
    per_tile = tm // MOBA_BLOCK
    k_aug, k_mean = pl.pallas_call(
        _moba_keys_kernel,
        out_shape=(jax.ShapeDtypeStruct((HEADS_PER_TILE, T, D), BF16),
                   jax.ShapeDtypeStruct((B * nb, 1, D), F32)),
        grid=(T // tm,),
        in_specs=[pl.BlockSpec((tm, K), lambda i: (i, 0)),
                  pl.BlockSpec((K, D), lambda i: (0, 0))],
        out_specs=(pl.BlockSpec((HEADS_PER_TILE, tm, D), lambda i: (0, i, 0)),
                   pl.BlockSpec((per_tile, 1, D), lambda i: (i, 0, 0))),
        compiler_params=_cparams("parallel"),
        name="moba_keys",
    )(x2d, w_k)
    return k_aug.reshape(HEADS_PER_TILE, B, S, D), k_mean.reshape(B, nb, D)


def _router_kernel(x_ref, wt_ref, bias_ref, idx_ref, gate_ref, rank_ref, cnt_ref, run_ref, *, tm):
    @pl.when(pl.program_id(0) == 0)
    def _():
        run_ref[...] = jnp.zeros_like(run_ref)

    E = wt_ref.shape[1]
    gsz = E // N_GROUPS
    logits = _three_pass(x_ref[...], wt_ref, lambda x, w: _nt_dot(w, x))
    s = jax.nn.sigmoid(logits)
    sb = s + bias_ref[...]
    neg_inf = -jnp.inf

    giota = lax.broadcasted_iota(jnp.int32, (gsz, tm), 0)
    gscore = []
    for g in range(N_GROUPS):
        blk = sb[g * gsz:(g + 1) * gsz, :]
        m1 = jnp.max(blk, axis=0, keepdims=True)
        i1 = jnp.min(jnp.where(blk == m1, giota, gsz), axis=0, keepdims=True)
        m2 = jnp.max(jnp.where(giota == i1, neg_inf, blk), axis=0, keepdims=True)
        gscore.append(m1 + m2)

    gsel = [jnp.zeros((1, tm), jnp.bool_) for _ in range(N_GROUPS)]
    for _ in range(TOPK_GROUPS):
        mx = functools.reduce(jnp.maximum, gscore)
        found = jnp.zeros((1, tm), jnp.bool_)
        for g in range(N_GROUPS):
            hit = (gscore[g] == mx) & jnp.logical_not(found)
            gsel[g] = gsel[g] | hit
            found = found | hit
            gscore[g] = jnp.where(hit, neg_inf, gscore[g])
    emask = jnp.concatenate([jnp.broadcast_to(gsel[g], (gsz, tm)) for g in range(N_GROUPS)], axis=0)
    cand = jnp.where(emask, sb, NEG)

    eiota = lax.broadcasted_iota(jnp.int32, (E, tm), 0)
    hits, idxs, ws = [], [], []
    for _ in range(TOP_K):
        mx = jnp.max(cand, axis=0, keepdims=True)
        first = jnp.min(jnp.where(cand == mx, eiota, E), axis=0, keepdims=True)
        hit = eiota == first
        hits.append(hit)
        idxs.append(first)
        ws.append(jnp.sum(jnp.where(hit, s, 0.0), axis=0, keepdims=True))
        cand = jnp.where(hit, neg_inf, cand)
    wsum = functools.reduce(jnp.add, ws)

    chosen = functools.reduce(jnp.logical_or, hits)
    onehot = jnp.where(chosen, 1.0, 0.0)
    tr = lax.broadcasted_iota(jnp.int32, (tm, tm), 0)
    tc = lax.broadcasted_iota(jnp.int32, (tm, tm), 1)
    before = (tr < tc).astype(BF16)
    prior = jnp.dot(onehot.astype(BF16), before, preferred_element_type=F32) + run_ref[...]
    grow = lax.broadcasted_iota(jnp.int32, (LANES, tm), 0)
    gates = jnp.zeros((LANES, tm), F32)
    for k in range(TOP_K):
        idx_ref[k:k + 1, :] = idxs[k]
        gates = jnp.where(grow == k, ws[k] / wsum * ROUTED_SCALE, gates)
        rank_ref[k:k + 1, :] = jnp.sum(jnp.where(hits[k], prior, 0.0), axis=0,
                                       keepdims=True).astype(jnp.int32)
    gate_ref[...] = gates.T
    run_ref[...] = run_ref[...] + jnp.sum(onehot, axis=1, keepdims=True)
    cnt_ref[...] = run_ref[...]


def _router(x2d, w_router, router_bias, tm):
    T, D = x2d.shape
    E = w_router.shape[1]
    tok = lambda i: (0, i)
    fixed = lambda i: (0, 0)
    return pl.pallas_call(
        functools.partial(_router_kernel, tm=tm),
        out_shape=(jax.ShapeDtypeStruct((TOP_K, T), jnp.int32),
                   jax.ShapeDtypeStruct((T, LANES), F32),
                   jax.ShapeDtypeStruct((TOP_K, T), jnp.int32),
                   jax.ShapeDtypeStruct((E, 1), F32)),
        grid=(T // tm,),
        in_specs=[pl.BlockSpec((tm, D), lambda i: (i, 0)),
                  pl.BlockSpec((2, E, D), lambda i: (0, 0, 0)),
                  pl.BlockSpec((E, 1), fixed)],
        out_specs=(pl.BlockSpec((TOP_K, tm), tok), pl.BlockSpec((tm, LANES), lambda i: (i, 0)),
                   pl.BlockSpec((TOP_K, tm), tok), pl.BlockSpec((E, 1), fixed)),
        scratch_shapes=[pltpu.VMEM((E, 1), F32)],
        compiler_params=_cparams("arbitrary"),
        name="router",
    )(x2d, jnp.stack(_split_bf16(w_router.T, 2)), router_bias.reshape(E, 1).astype(F32))


def _slots_kernel(idx_ref, rank_ref, start_ref, o_ref):
    E = start_ref.shape[0]
    tm = idx_ref.shape[1]
    eiota = lax.broadcasted_iota(jnp.int32, (E, tm), 0)
    start = start_ref[...]
    for k in range(TOP_K):
        base = jnp.sum(jnp.where(eiota == idx_ref[k:k + 1, :], start, 0.0), axis=0, keepdims=True)
        o_ref[k:k + 1, :] = base.astype(jnp.int32) + rank_ref[k:k + 1, :]


def _slots(idx, rank, seg_start, tm):
    T = idx.shape[1]
    E = seg_start.shape[0]
    tok = lambda i: (0, i)
    return pl.pallas_call(
        _slots_kernel,
        out_shape=jax.ShapeDtypeStruct((TOP_K, T), jnp.int32),
        grid=(T // tm,),
        in_specs=[pl.BlockSpec((TOP_K, tm), tok), pl.BlockSpec((TOP_K, tm), tok),
                  pl.BlockSpec((E, 1), lambda i: (0, 0))],
        out_specs=pl.BlockSpec((TOP_K, tm), tok),
        compiler_params=_cparams("parallel"),
        name="moe_slots",
    )(idx, rank, seg_start.astype(F32).reshape(E, 1))


SC_CORES = 2
SC_SUBCORES = 16
SC_CHUNK = 128


def _sc_rows_kernel(body, out_shape, dtype, scratch, name):
    mesh = plsc.VectorSubcoreMesh(core_axis_name="c", subcore_axis_name="s",
                                  num_cores=SC_CORES, num_subcores=SC_SUBCORES)
    return pl.kernel(body, mesh=mesh, out_type=jax.ShapeDtypeStruct(out_shape, dtype),
                     scratch_types=scratch, name=name)


def _sc_chunks(n):
    workers = SC_CORES * SC_SUBCORES
    chunks = n // (workers * SC_CHUNK)
    assert chunks * workers * SC_CHUNK == n
    return workers, chunks


SC_HALF = SC_CHUNK // 2


def _sc_gather_rows(table, idx):
    workers, chunks = _sc_chunks(idx.shape[0])
    width = table.shape[1]

    def body(table_hbm, idx_hbm, out_hbm, idx_v, rows_v, *sems):
        wid = lax.axis_index("s") * SC_CORES + lax.axis_index("c")
        pltpu.sync_copy(idx_hbm.at[wid], idx_v)

        @pl.loop(0, chunks)
        def _(j):
            pieces = [2 * j, 2 * j + 1]
            reads = [pltpu.async_copy(table_hbm.at[idx_v.at[p]], rows_v.at[b], sems[b])
                     for b, p in enumerate(pieces)]
            writes = []
            for b, p in enumerate(pieces):
                reads[b].wait()
                first = (wid * 2 * chunks + p) * SC_HALF
                writes.append(pltpu.async_copy(rows_v.at[b], out_hbm.at[pl.ds(first, SC_HALF)],
                                               sems[2 + b]))
            for write in writes:
                write.wait()

    scratch = [pltpu.VMEM((2 * chunks, SC_HALF), jnp.int32),
               pltpu.VMEM((2, SC_HALF, width), table.dtype)] + [pltpu.SemaphoreType.DMA] * 4
    call = _sc_rows_kernel(body, (idx.shape[0], width), table.dtype, scratch, "moe_gather_sc")
    return call(table, idx.reshape(workers, 2 * chunks, SC_HALF))


def _sc_scatter_rows(src, dest, n_rows):
    copies, n_src = dest.shape
    workers, chunks = _sc_chunks(n_src)

    def body(src_hbm, idx_hbm, out_hbm, idx_v, rows_v, sem):
        wid = lax.axis_index("s") * SC_CORES + lax.axis_index("c")
        pltpu.sync_copy(idx_hbm.at[wid], idx_v)

        @pl.loop(0, chunks)
        def _(j):
            pltpu.sync_copy(src_hbm.at[pl.ds((wid * chunks + j) * SC_CHUNK, SC_CHUNK)], rows_v)
            started = [pltpu.async_copy(rows_v, out_hbm.at[idx_v.at[j, k]], sem)
                       for k in range(copies)]
            for copy in started:
                copy.wait()

    scratch = [pltpu.VMEM((chunks, copies, SC_CHUNK), jnp.int32),
               pltpu.VMEM((SC_CHUNK, src.shape[1]), src.dtype), pltpu.SemaphoreType.DMA]
    call = _sc_rows_kernel(body, (n_rows, src.shape[1]), src.dtype, scratch, "moe_scatter_sc")
    idx = dest.reshape(copies, workers, chunks, SC_CHUNK).transpose(1, 2, 0, 3)
    return call(src, idx)


def _silu(g):
    return g * jax.nn.sigmoid(g)


SCHED_FIRST, SCHED_SLOT, SCHED_NEXT, SCHED_VALID = 0, 1, 2, 3


def _expert_kernel(sched_ref, n_used_ref, x_ref, wg_hbm, wu_hbm, wd_hbm, o_ref,
                   wg_buf, wu_buf, wd_buf, sem, *, layer, first_expert_row):
    i = pl.program_id(0)
    slot = sched_ref[SCHED_SLOT, i]

    def copies(expert, slot):
        pairs = ((wg_hbm, wg_buf), (wu_hbm, wu_buf), (wd_hbm, wd_buf))
        return [pltpu.make_async_copy(src.at[layer, expert], dst.at[slot], sem.at[slot, n])
                for n, (src, dst) in enumerate(pairs)]

    @pl.when(i == 0)
    def _():
        for c in copies(sched_ref[first_expert_row, 0], 0):
            c.start()

    @pl.when(sched_ref[SCHED_FIRST, i] == 1)
    def _():
        for c in copies(0, slot):
            c.wait()

        @pl.when(sched_ref[SCHED_NEXT, i] >= 0)
        def _():
            for c in copies(sched_ref[SCHED_NEXT, i], 1 - slot):
                c.start()

    @pl.when(i < n_used_ref[0])
    def _():
        words = x_ref[...]
        row = lax.broadcasted_iota(jnp.int32, words.shape, 0)
        words = jnp.where(row < sched_ref[SCHED_VALID, i], words, 0)
        x = jnp.concatenate(_unpack_halves(words), axis=1).astype(BF16)
        g = jnp.dot(x, wg_buf[slot].astype(BF16), preferred_element_type=F32)
        u = jnp.dot(x, wu_buf[slot].astype(BF16), preferred_element_type=F32)
        a = (_silu(g) * u).astype(BF16)
        o_ref[...] = _pack_halves(jnp.dot(a, wd_buf[slot].astype(BF16), preferred_element_type=F32))

    @pl.when(i >= n_used_ref[0])
    def _():
        o_ref[...] = jnp.zeros_like(o_ref)


def _expert_schedule(seg_start, seg_end, counts, n_blocks):
    E = seg_end.shape[0]
    first_row = jnp.arange(n_blocks, dtype=jnp.int32) * EXPERT_ROWS
    blk_e = jnp.minimum(jnp.sum(seg_end[None, :] <= first_row[:, None], axis=1), E - 1)
    used = first_row < seg_end[-1]
    first = used & (first_row == seg_start[blk_e])
    ordinal = jnp.cumsum(first) - 1
    ids = jnp.where(seg_end > seg_start, jnp.arange(E), E)
    later = lax.cummin(ids, reverse=True)
    nxt = jnp.concatenate([later[1:], jnp.full((1,), E, later.dtype)])[blk_e]
    nxt = jnp.where(nxt < E, nxt, -1)
    valid = jnp.clip((seg_start + counts)[blk_e] - first_row, 0, EXPERT_ROWS)
    return jnp.stack([first, ordinal % 2, nxt, valid, blk_e]).astype(jnp.int32)


def _expert_mlp(xs, sched, n_used, w_gate, w_up, w_down, layer):
    n_rows, half = xs.shape
    D = 2 * half
    F = w_gate.shape[-1]
    tm = EXPERT_ROWS
    hbm = pl.BlockSpec(memory_space=pl.ANY)
    return pl.pallas_call(
        functools.partial(_expert_kernel, layer=layer, first_expert_row=sched.shape[0] - 1),
        out_shape=jax.ShapeDtypeStruct((n_rows, half), jnp.int32),
        grid_spec=pltpu.PrefetchScalarGridSpec(
            num_scalar_prefetch=2,
            grid=(n_rows // tm,),
            in_specs=[pl.BlockSpec((tm, half), lambda i, sc, nu: (jnp.minimum(i, nu[0] - 1), 0)),
                      hbm, hbm, hbm],
            out_specs=pl.BlockSpec((tm, half), lambda i, sc, nu: (i, 0)),
            scratch_shapes=[pltpu.VMEM((2, D, F), F32), pltpu.VMEM((2, D, F), F32),
                            pltpu.VMEM((2, F, D), F32), pltpu.SemaphoreType.DMA((2, 3))]),
        compiler_params=_cparams("arbitrary"),
        name="moe_experts",
    )(sched, n_used, xs, w_gate, w_up, w_down)


def _combine_kernel(x_ref, gate_ref, rows_ref, wsg_ref, wsu_ref, wsd_ref, g_ref, b_ref, o_ref,
                    *, alpha):
    x = x_ref[...]
    xb = x.astype(BF16)
    g = jnp.dot(xb, wsg_ref[...], preferred_element_type=F32)
    u = jnp.dot(xb, wsu_ref[...], preferred_element_type=F32)
    y = jnp.dot((_silu(g) * u).astype(BF16), wsd_ref[...], preferred_element_type=F32)
    gates = gate_ref[...]
    half = y.shape[1] // 2
    left, right = y[:, :half], y[:, half:]
    for k in range(TOP_K):
        lo, hi = _unpack_halves(rows_ref[k])
        left = left + lo * gates[:, k:k + 1]
        right = right + hi * gates[:, k:k + 1]
    y = jnp.concatenate([left, right], axis=1)
    o_ref[...] = _layer_norm(alpha * x + y, g_ref[...], b_ref[...])


def _combine(x2d, ys, dest, gates_tok, ws_gate, ws_up, ws_down, g, b, alpha, tt):
    T, D = x2d.shape
    F = ws_gate.shape[1]
    row = lambda i: (i, 0)
    fixed = lambda i: (0, 0)
    rows = _sc_gather_rows(ys, dest.reshape(-1)).reshape(TOP_K, T, D // 2)
    return pl.pallas_call(
        functools.partial(_combine_kernel, alpha=alpha),
        out_shape=jax.ShapeDtypeStruct((T, D), F32),
        grid=(T // tt,),
        in_specs=[pl.BlockSpec((tt, D), row),
                  pl.BlockSpec((tt, LANES), row),
                  pl.BlockSpec((TOP_K, tt, D // 2), lambda i: (0, i, 0)),
                  pl.BlockSpec((D, F), fixed), pl.BlockSpec((D, F), fixed),
                  pl.BlockSpec((F, D), fixed),
                  pl.BlockSpec((1, D), fixed), pl.BlockSpec((1, D), fixed)],
        out_specs=pl.BlockSpec((tt, D), row),
        compiler_params=_cparams("parallel"),
        name="moe_combine",
    )(x2d, gates_tok, rows, ws_gate.astype(BF16), ws_up.astype(BF16), ws_down.astype(BF16),
      g.reshape(1, D), b.reshape(1, D))


def _pick(n, pref):
    t = min(n, pref)
    while n % t:
        t //= 2
    return t


def _moe_layer(x2d, packed, w_router, router_bias, w_gate, w_up, w_down, layer, ws_gate, ws_up,
               ws_down, g, b, alpha):
    T, D = x2d.shape
    E = w_router.shape[1]
    idx, gates, rank, counts = _router(x2d, w_router, router_bias, _pick(T, 256))
    counts = counts[:, 0].astype(jnp.int32)
    padded = (counts + EXPERT_ROWS - 1) // EXPERT_ROWS * EXPERT_ROWS
    seg_end = jnp.cumsum(padded)
    seg_start = seg_end - padded
    n_blocks = (T * TOP_K + E * (EXPERT_ROWS - 1)) // EXPERT_ROWS
    dest = _slots(idx, rank, seg_start, _pick(T, 1024))
    sched = _expert_schedule(seg_start, seg_end, counts, n_blocks)
    n_used = (seg_end[-1:] // EXPERT_ROWS).astype(jnp.int32)
    xs = _sc_scatter_rows(packed, dest, n_blocks * EXPERT_ROWS)
    ys = _expert_mlp(xs, sched, n_used, w_gate, w_up, w_down, layer)
    return _combine(x2d, ys, dest, gates, ws_gate, ws_up, ws_down, g, b, alpha, _pick(T, 256))


def _mixer_layer(x2d, i, B, S, p, alpha):
    T, D = x2d.shape
    tm = _pick(T, 512)
    tn = _pick(D, 1024)
    j = i // 2
    w_in = p["moba_w_in"][j] if i % 2 == 0 else p["fox_w_in"][j]
    w_qv_t = jnp.concatenate([w_in[:, :D].T * HEAD_DIM ** -0.5, w_in[:, 2 * D:3 * D].T]).astype(BF16)
    qv_t = _project_t(x2d, w_qv_t, tm, tn)
    w_k = w_in[:, D:2 * D].astype(BF16)
    if i % 2 == 0:
        k_aug, k_mean = _moba_keys(x2d, w_k, B, S, tm)
        attn_t = _moba_attention(qv_t, k_aug, k_mean, p["rel_bias"], B, S)
        w_out = p["moba_w_out"][j]
    else:
        c_tok = _fox_gates(x2d.reshape(B, S, D), w_in[:, 3 * D:], p["fox_b_f"][j], _pick(S, 512))
        k_aug = _fox_keys(x2d, w_k, c_tok.reshape(T, LANES), tm).reshape(HEADS_PER_TILE, B, S, D)
        attn_t = _fox_attention(qv_t, k_aug, B, S, _pick(S, 256))
        w_out = p["fox_w_out"][j]
    return _outproj_ln(attn_t, w_out.astype(BF16), x2d, p["ln1_g"][i], p["ln1_b"][i], alpha, tm)


def kernel(x, rel_bias, moba_w_in, moba_w_out, fox_w_in, fox_b_f, fox_w_out, ln1_g, ln1_b, ln2_g,
           ln2_b, w_router, router_bias, w_gate, w_up, w_down, ws_gate, ws_up, ws_down):
    B, S, D = x.shape
    depth = ln1_g.shape[0]
    alpha = (2 * depth) ** 0.25
    p = dict(rel_bias=rel_bias, moba_w_in=moba_w_in, moba_w_out=moba_w_out, fox_w_in=fox_w_in,
             fox_b_f=fox_b_f, fox_w_out=fox_w_out, ln1_g=ln1_g, ln1_b=ln1_b)
    x2d = x.reshape(B * S, D)
    for i in range(depth):
        x2d, packed = _mixer_layer(x2d, i, B, S, p, alpha)
        x2d = _moe_layer(x2d, packed, w_router[i], router_bias[i], w_gate, w_up, w_down, i,
                         ws_gate[i], ws_up[i], ws_down[i], ln2_g[i], ln2_b[i], alpha)
    return x2d.reshape(B, S, D)
```

```python
import functools
import math

import jax
import jax.numpy as jnp
from jax import lax
from jax.experimental import pallas as pl
from jax.experimental.pallas import tpu as pltpu
from jax.experimental.pallas import tpu_sc as plsc

F32 = jnp.float32
BF16 = jnp.bfloat16
HIGHEST = lax.Precision.HIGHEST

HEAD_DIM = 64
MOBA_BLOCK = 256
MOBA_TOPK = 3
REL_MAX_DIST = 128
TOP_K = 8
N_GROUPS = 8
TOPK_GROUPS = 4
ROUTED_SCALE = 2.5
LN_EPS = 1e-5
NEG = -1e30
LOG2E = math.log2(math.e)

LANES = 128
HEADS_PER_TILE = LANES // HEAD_DIM
EXPERT_ROWS = 512
VMEM_LIMIT = 48 * 1024 * 1024


def _cparams(*sem):
    return pltpu.CompilerParams(dimension_semantics=sem, vmem_limit_bytes=VMEM_LIMIT)


def _nt_dot(a, b, **kw):
    return lax.dot_general(a, b, (((1,), (1,)), ((), ())), preferred_element_type=F32, **kw)


def _split_bf16(x, pieces):
    out = []
    for _ in range(pieces):
        out.append(x.astype(BF16))
        x = x - out[-1].astype(F32)
    return out


def _three_pass(x, w_ref, mul):
    hi, lo = _split_bf16(x, 2)
    return mul(hi, w_ref[0]) + mul(lo, w_ref[0]) + mul(hi, w_ref[1])


def _proj_t_kernel(x_ref, wt_ref, o_ref):
    o_ref[...] = _nt_dot(wt_ref[...], x_ref[...].astype(BF16)).astype(o_ref.dtype)


def _project_t(x2d, wt, tm, tn):
    T, K = x2d.shape
    N = wt.shape[0]
    return pl.pallas_call(
        _proj_t_kernel,
        out_shape=jax.ShapeDtypeStruct((N, T), BF16),
        grid=(T // tm, N // tn),
        in_specs=[pl.BlockSpec((tm, K), lambda i, j: (i, 0)),
                  pl.BlockSpec((tn, K), lambda i, j: (j, 0))],
        out_specs=pl.BlockSpec((tn, tm), lambda i, j: (j, i)),
        compiler_params=_cparams("parallel", "arbitrary"),
        name="qv_proj_t",
    )(x2d, wt)


def _layer_norm(r, g, b):
    mu = jnp.mean(r, axis=-1, keepdims=True)
    d = r - mu
    var = jnp.mean(d * d, axis=-1, keepdims=True)
    return d * lax.rsqrt(var + LN_EPS) * g + b


def _pack_halves(y):
    n = y.shape[1] // 2
    bits = lambda v: lax.bitcast_convert_type(v.astype(BF16).astype(F32), jnp.int32)
    left = bits(y[:, :n])
    return lax.shift_right_logical(left, jnp.full_like(left, 16)) | bits(y[:, n:])


def _unpack_halves(p):
    return (lax.bitcast_convert_type(p << 16, F32),
            lax.bitcast_convert_type(p & jnp.int32(-65536), F32))


def _outproj_ln_kernel(at_ref, w_ref, x_ref, g_ref, b_ref, o_ref, packed_ref, *, alpha):
    y = lax.dot_general(at_ref[...], w_ref[...], (((0,), (0,)), ((), ())),
                        preferred_element_type=F32)
    out = _layer_norm(alpha * x_ref[...] + y, g_ref[...], b_ref[...])
    o_ref[...] = out
    packed_ref[...] = _pack_halves(out)


def _outproj_ln(attn_t, w_out, x2d, g, b, alpha, tm):
    T, D = x2d.shape
    row = lambda i: (i, 0)
    fixed = lambda i: (0, 0)
    return pl.pallas_call(
        functools.partial(_outproj_ln_kernel, alpha=alpha),
        out_shape=(jax.ShapeDtypeStruct((T, D), F32), jax.ShapeDtypeStruct((T, D // 2), jnp.int32)),
        grid=(T // tm,),
        in_specs=[pl.BlockSpec((D, tm), lambda i: (0, i)), pl.BlockSpec((D, D), fixed),
                  pl.BlockSpec((tm, D), row), pl.BlockSpec((1, D), fixed),
                  pl.BlockSpec((1, D), fixed)],
        out_specs=(pl.BlockSpec((tm, D), row), pl.BlockSpec((tm, D // 2), row)),
        compiler_params=_cparams("parallel"),
        name="outproj_ln",
    )(attn_t, w_out, x2d, g.reshape(1, D), b.reshape(1, D))


def _spare(a, n=0):
    return ((a + 1) % HEADS_PER_TILE) * HEAD_DIM + n


def _scores(chains):
    return [[jnp.dot(keys(), query(), preferred_element_type=F32) for keys, query, _ in blocks]
            for blocks in chains]


def _absorb(carries, scores, chains, masks=None, adds=None):
    probs = []
    for c, ((m, _), parts) in enumerate(zip(carries, scores)):
        if adds is not None:
            parts = [s if add is None else s + add() for add, s in zip(adds[c], parts)]
        if masks is not None:
            parts = [s if keep is None else jnp.where(keep, s, NEG)
                     for keep, s in zip(masks[c], parts)]
        tops = [jnp.max(s, axis=0, keepdims=True) for s in parts]
        m_new = functools.reduce(jnp.maximum, tops, m)
        probs.append((m_new, jnp.exp2(m - m_new),
                      [jnp.exp2(s - m_new).astype(BF16) for s in parts]))
    out = []
    for (_, acc), (m_new, decay, ps), blocks in zip(carries, probs, chains):
        acc = decay * acc
        for (_, _, values), p in zip(blocks, ps):
            acc = acc + jnp.dot(values(), p, preferred_element_type=F32)
        out.append((m_new, acc))
    return tuple(out)


def _stash(chains, ref):
    for a, parts in enumerate(_scores(chains)):
        for b, s in enumerate(parts):
            ref[a, b] = s


def _fetch(chains, ref):
    return [[ref[a, b] for b in range(len(blocks))] for a, blocks in enumerate(chains)]


def _attend_masked_last(carries, n, group, stage_refs, masks, adds=None):
    first, second = stage_refs
    _stash(group(0), first)
    trips = n // 2

    def body(t, carries):
        g = 2 * t
        _stash(group(g + 1), second)
        carries = _absorb(carries, _fetch(group(g), first), group(g))
        _stash(group(g + 2), first)
        return _absorb(carries, _fetch(group(g + 1), second), group(g + 1))

    carries = lax.fori_loop(0, trips, body, carries)

    def odd_tail(carries):
        _stash(group(n), second)
        carries = _absorb(carries, _fetch(group(n - 1), first), group(n - 1))
        return _absorb(carries, _fetch(group(n), second), group(n), masks, adds)

    def even_tail(carries):
        return _absorb(carries, _fetch(group(n), first), group(n), masks, adds)

    return lax.cond(n % 2 == 1, odd_tail, even_tail, carries)


GROUP_BLOCKS = 2


def _stage_scratch(tq, chains=HEADS_PER_TILE):
    return [pltpu.VMEM((chains, GROUP_BLOCKS, tq, tq), F32) for _ in range(2)]


def _scores_init(tq):
    return jnp.full((1, tq), -jnp.inf, F32), jnp.zeros((LANES, tq), F32)


def _finish(carries, o_ref, cols=slice(None)):
    row = lax.broadcasted_iota(jnp.int32, carries[0][1].shape, 0)
    outs = []
    for a, (_, acc) in enumerate(carries):
        s = _spare(a)
        outs.append(acc / acc[s:s + 1, :])
    o_ref[:, cols] = jnp.where(row < HEAD_DIM, outs[0], outs[1]).astype(o_ref.dtype)


def _fill_v_aug(vt_ref, vaug_ref, tk):
    n_tiles = vt_ref.shape[1] // tk
    row = lax.broadcasted_iota(jnp.int32, (LANES, tk), 0)
    for n in range(n_tiles):
        v = vt_ref[:, n * tk:(n + 1) * tk]
        for a in range(HEADS_PER_TILE):
            vaug_ref[a, n] = jnp.where(row // HEAD_DIM == a, v, jnp.ones_like(v))


FOX_SPLIT = 3


def _fox_kernel(qt_ref, k_ref, vt_ref, o_ref, vaug_ref, stage0_ref, stage1_ref, *, tq):
    n = pl.program_id(2)

    @pl.when(n == 0)
    def _():
        _fill_v_aug(vt_ref, vaug_ref, tq)

    chains = [(h, a) for h in range(GROUP_BLOCKS) for a in range(HEADS_PER_TILE)]
    row = lax.broadcasted_iota(jnp.int32, (LANES, tq), 0)
    queries = {}
    for h, a in chains:
        offs = (row >= _spare(a)) & (row < _spare(a, FOX_SPLIT))
        queries[h, a] = jnp.where(row // HEAD_DIM == a, qt_ref[:, h * tq:(h + 1) * tq],
                                  jnp.where(offs, -1.0, 0.0).astype(BF16))

    def block(h, a, j):
        start = pl.multiple_of(j * tq, tq)
        return (lambda: k_ref[a, 0, pl.ds(start, tq), :], lambda: queries[h, a],
                lambda: vaug_ref[a, j])

    key = lax.broadcasted_iota(jnp.int32, (tq, tq), 0)
    qry = lax.broadcasted_iota(jnp.int32, (tq, tq), 1)
    masks = [[None if b < h else key + (b - h) * tq <= qry for b in range(GROUP_BLOCKS)]
             for h, _ in chains]
    carries = _attend_masked_last(
        tuple(_scores_init(tq) for _ in chains), n,
        lambda j: [[block(h, a, 2 * j), block(h, a, 2 * j + 1)] for h, a in chains],
        (stage0_ref, stage1_ref), masks)
    for h in range(GROUP_BLOCKS):
        _finish(carries[h * HEADS_PER_TILE:(h + 1) * HEADS_PER_TILE], o_ref,
                slice(h * tq, (h + 1) * tq))


def _fox_attention(qv_t, k_aug, B, S, tq):
    D = k_aug.shape[-1]
    n_tiles = D // LANES
    nq = S // tq
    span = GROUP_BLOCKS * tq
    steps = S // span
    return pl.pallas_call(
        functools.partial(_fox_kernel, tq=tq),
        out_shape=jax.ShapeDtypeStruct((D, B * S), BF16),
        grid=(B, n_tiles, steps),
        in_specs=[pl.BlockSpec((LANES, span), lambda b, h, i: (h, b * steps + i)),
                  pl.BlockSpec((HEADS_PER_TILE, 1, S, LANES), lambda b, h, i: (0, b, 0, h)),
                  pl.BlockSpec((LANES, S), lambda b, h, i: (n_tiles + h, b))],
        out_specs=pl.BlockSpec((LANES, span), lambda b, h, i: (h, b * steps + i)),
        scratch_shapes=[pltpu.VMEM((HEADS_PER_TILE, nq, LANES, tq), BF16)]
        + _stage_scratch(tq, GROUP_BLOCKS * HEADS_PER_TILE),
        compiler_params=_cparams("parallel", "parallel", "arbitrary"),
        name="fox_attention",
    )(qv_t, k_aug, qv_t)


def _log_sigmoid(z):
    return jnp.minimum(z, 0.0) - jnp.log1p(jnp.exp(-jnp.abs(z)))


def _fox_gates_kernel(x_ref, w_ref, b_ref, o_ref, carry_ref, *, ts):
    @pl.when(pl.program_id(1) == 0)
    def _():
        carry_ref[...] = jnp.zeros_like(carry_ref)

    z = _three_pass(x_ref[0], w_ref, lambda x, w: jnp.dot(x, w, preferred_element_type=F32))
    lf = _log_sigmoid(z + b_ref[...])
    r = lax.broadcasted_iota(jnp.int32, (ts, ts), 0)
    c = lax.broadcasted_iota(jnp.int32, (ts, ts), 1)
    tri = (c <= r).astype(BF16)
    cum = carry_ref[...]
    for piece in _split_bf16(lf, 3):
        cum = cum + jnp.dot(tri, piece, preferred_element_type=F32)
    carry_ref[...] = cum[ts - 1:ts, :]
    o_ref[0] = cum


def _fox_gates(x3d, w_f, b_f, ts):
    B, S, D = x3d.shape
    H = w_f.shape[1]
    w_pad = jnp.stack(_split_bf16(jnp.zeros((D, LANES), F32).at[:, :H].set(w_f), 2))
    b_pad = jnp.zeros((1, LANES), F32).at[0, :H].set(b_f)
    return pl.pallas_call(
        functools.partial(_fox_gates_kernel, ts=ts),
        out_shape=jax.ShapeDtypeStruct((B, S, LANES), F32),
        grid=(B, S // ts),
        in_specs=[pl.BlockSpec((1, ts, D), lambda b, s: (b, s, 0)),
                  pl.BlockSpec((2, D, LANES), lambda b, s: (0, 0, 0)),
                  pl.BlockSpec((1, LANES), lambda b, s: (0, 0))],
        out_specs=pl.BlockSpec((1, ts, LANES), lambda b, s: (b, s, 0)),
        scratch_shapes=[pltpu.VMEM((1, LANES), F32)],
        compiler_params=_cparams("parallel", "arbitrary"),
        name="fox_gates",
    )(x3d, w_pad, b_pad)


def _fox_keys_kernel(x_ref, w_ref, c_ref, o_ref):
    keys = jnp.dot(x_ref[...].astype(BF16), w_ref[...], preferred_element_type=F32)
    tm, D = keys.shape
    lane = lax.broadcasted_iota(jnp.int32, (tm, LANES), 1)
    for t in range(D // LANES):
        k = keys[:, t * LANES:(t + 1) * LANES]
        for a in range(HEADS_PER_TILE):
            h = t * HEADS_PER_TILE + a
            rest = c_ref[:, h:h + 1] * LOG2E
            aug = jnp.zeros((tm, LANES), F32)
            for n in range(FOX_SPLIT):
                piece = rest.astype(BF16).astype(F32)
                rest = rest - piece
                aug = jnp.where(lane == _spare(a, n), piece, aug)
            o_ref[a, :, t * LANES:(t + 1) * LANES] = jnp.where(lane // HEAD_DIM == a, k,
                                                               aug).astype(BF16)


def _fox_keys(x2d, w_k, c_tok, tm):
    T, K = x2d.shape
    D = w_k.shape[1]
    return pl.pallas_call(
        _fox_keys_kernel,
        out_shape=jax.ShapeDtypeStruct((HEADS_PER_TILE, T, D), BF16),
        grid=(T // tm,),
        in_specs=[pl.BlockSpec((tm, K), lambda i: (i, 0)),
                  pl.BlockSpec((K, D), lambda i: (0, 0)),
                  pl.BlockSpec((tm, LANES), lambda i: (i, 0))],
        out_specs=pl.BlockSpec((HEADS_PER_TILE, tm, D), lambda i: (0, i, 0)),
        compiler_params=_cparams("parallel"),
        name="fox_keys",
    )(x2d, w_k, c_tok)


MOBA_SLAB = 16
KMEAN_SPLIT = 3


def _moba_kernel(bfar_ref, qt_ref, k_ref, vt_ref, kmean_ref, bown_ref, badj_ref, o_ref,
                 vaug_ref, sel_ref, stage0_ref, stage1_ref, *, blk):
    t = pl.program_id(1)
    n = pl.program_id(2)
    nb = kmean_ref.shape[2]

    @pl.when(n == 0)
    def _():
        _fill_v_aug(vt_ref, vaug_ref, blk)

    chains = [(h, a) for h in range(GROUP_BLOCKS) for a in range(HEADS_PER_TILE)]
    row = lax.broadcasted_iota(jnp.int32, (LANES, blk), 0)
    brow = lax.broadcasted_iota(jnp.int32, (nb, blk), 0)
    base = {}
    for c, (h, a) in enumerate(chains):
        own = GROUP_BLOCKS * n + h
        qt = qt_ref[:, h * blk:(h + 1) * blk]
        qa = jnp.where(row // HEAD_DIM == a, qt, jnp.zeros_like(qt))
        base[h, a] = qa
        bscore = sum(jnp.dot(kmean_ref[piece, 0], qa, preferred_element_type=F32)
                     for piece in range(KMEAN_SPLIT))
        cand = jnp.where(brow < own, bscore, NEG)
        sel = jnp.zeros((nb, blk), F32)
        for _ in range(MOBA_TOPK):
            mx = jnp.max(cand, axis=0, keepdims=True)
            first = jnp.min(jnp.where(cand == mx, brow, nb), axis=0, keepdims=True)
            hit = brow == first
            sel = jnp.where(hit, 1.0, sel)
            cand = jnp.where(hit, -jnp.inf, cand)
        sel_ref[c] = jnp.where(brow < own, sel, 0.0)

    srow = lax.broadcasted_iota(jnp.int32, (MOBA_SLAB, blk), 0)

    def block(c, j):
        h, a = chains[c]
        head_id = t * HEADS_PER_TILE + a
        jc = jnp.maximum(j, 0)
        start = pl.multiple_of(jc * blk, blk)

        def query():
            s0 = _spare(a) // MOBA_SLAB * MOBA_SLAB
            chosen = (((sel_ref[c, pl.ds(jc, 1), :] > 0.0) & (j >= 0))
                      | (j == GROUP_BLOCKS * n + h))
            slab = jnp.where(srow == _spare(a) - s0,
                             jnp.where(chosen, bfar_ref[0, head_id], NEG),
                             jnp.where(srow == _spare(a, 1) - s0, bfar_ref[1, head_id], 0.0)
                             ).astype(BF16)
            q = base[h, a]
            head = [q[:s0]] if s0 else []
            tail = [q[s0 + MOBA_SLAB:]] if s0 + MOBA_SLAB < LANES else []
            return jnp.concatenate(head + [slab] + tail, axis=0)

        return lambda: k_ref[a, 0, pl.ds(start, blk), :], query, lambda: vaug_ref[a, jc]

    key = lax.broadcasted_iota(jnp.int32, (blk, blk), 0)
    qry = lax.broadcasted_iota(jnp.int32, (blk, blk), 1)
    carries = _attend_masked_last(
        tuple(_scores_init(blk) for _ in chains), n,
        lambda g: [[block(c, 2 * g + h - 1), block(c, 2 * g + h)] for c, (h, _) in enumerate(chains)],
        (stage0_ref, stage1_ref), [[None, key <= qry] for _ in chains],
        [[lambda a=a: badj_ref[a], lambda a=a: bown_ref[a]] for _, a in chains])
    for h in range(GROUP_BLOCKS):
        _finish(carries[h * HEADS_PER_TILE:(h + 1) * HEADS_PER_TILE], o_ref,
                slice(h * blk, (h + 1) * blk))


def _rel_bucket(dist, n_buckets):
    n = jnp.maximum(dist, 0)
    max_exact = n_buckets // 2
    nf = jnp.maximum(n, 1).astype(F32)
    large = max_exact + (jnp.log(nf / max_exact) / math.log(REL_MAX_DIST / max_exact)
                         * (n_buckets - max_exact)).astype(jnp.int32)
    large = jnp.minimum(large, n_buckets - 1)
    return jnp.where(n < max_exact, n, large)


def _moba_attention(qv_t, k_aug, k_mean, rel_bias, B, S):
    D = k_aug.shape[-1]
    n_tiles = D // LANES
    blk = MOBA_BLOCK
    nb = S // blk
    hp = HEADS_PER_TILE
    span = GROUP_BLOCKS * blk
    steps = S // span
    n_chains = GROUP_BLOCKS * hp
    r = jnp.arange(blk)
    delta = r[None, :] - r[:, None]
    bias_t = rel_bias.T.astype(F32)
    n_buckets = rel_bias.shape[0]
    def table(dist):
        onehot = jax.nn.one_hot(_rel_bucket(dist, n_buckets), n_buckets, dtype=F32)
        return jnp.einsum('crn,hn->hcr', onehot, bias_t, precision=HIGHEST)

    b_far = bias_t[:, n_buckets - 1] * LOG2E
    b_own = table(delta) * LOG2E - b_far[:, None, None]
    b_adj = table(delta + blk) * LOG2E - b_far[:, None, None]
    far_hi = b_far.astype(BF16).astype(F32)
    far_lo = (b_far - far_hi).astype(BF16).astype(F32)
    b_far2 = jnp.stack([far_hi, far_lo])
    return pl.pallas_call(
        functools.partial(_moba_kernel, blk=blk),
        out_shape=jax.ShapeDtypeStruct((D, B * S), BF16),
        grid_spec=pltpu.PrefetchScalarGridSpec(
            num_scalar_prefetch=1,
            grid=(B, n_tiles, steps),
            in_specs=[pl.BlockSpec((LANES, span), lambda b, h, i, f: (h, b * steps + i)),
                      pl.BlockSpec((hp, 1, S, LANES), lambda b, h, i, f: (0, b, 0, h)),
                      pl.BlockSpec((LANES, S), lambda b, h, i, f: (n_tiles + h, b)),
                      pl.BlockSpec((KMEAN_SPLIT, 1, nb, LANES), lambda b, h, i, f: (0, b, 0, h)),
                      pl.BlockSpec((hp, blk, blk), lambda b, h, i, f: (h, 0, 0)),
                      pl.BlockSpec((hp, blk, blk), lambda b, h, i, f: (h, 0, 0))],
            out_specs=pl.BlockSpec((LANES, span), lambda b, h, i, f: (h, b * steps + i)),
            scratch_shapes=[pltpu.VMEM((hp, nb, LANES, blk), BF16),
                            pltpu.VMEM((n_chains, nb, blk), F32)] + _stage_scratch(blk, n_chains)),
        compiler_params=_cparams("parallel", "parallel", "arbitrary"),
        name="moba_attention",
    )(b_far2, qv_t, k_aug, qv_t, k_mean, b_own, b_adj)


def _moba_keys_kernel(x_ref, w_ref, o_ref, mean_ref):
    k = jnp.dot(x_ref[...].astype(BF16), w_ref[...], preferred_element_type=F32)
    col = lax.broadcasted_iota(jnp.int32, k.shape, 1)
    lane = col % LANES
    for a in range(HEADS_PER_TILE):
        ones = (lane == _spare(a)) | (lane == _spare(a, 1))
        o_ref[a] = jnp.where((col // HEAD_DIM) % HEADS_PER_TILE == a, k,
                             jnp.where(ones, 1.0, 0.0)).astype(BF16)
    for n in range(mean_ref.shape[0]):
        mean_ref[n] = jnp.mean(k[n * MOBA_BLOCK:(n + 1) * MOBA_BLOCK], axis=0, keepdims=True)


def _moba_keys(x2d, w_k, B, S, tm):
    T, K = x2d.shape
    D = w_k.shape[1]
    nb = S // MOBA_BLOCK
    per_tile = tm // MOBA_BLOCK
    k_aug, k_mean = pl.pallas_call(
        _moba_keys_kernel,
        out_shape=(jax.ShapeDtypeStruct((HEADS_PER_TILE, T, D), BF16),
                   jax.ShapeDtypeStruct((B * nb, 1, D), F32)),
        grid=(T // tm,),
        in_specs=[pl.BlockSpec((tm, K), lambda i: (i, 0)),
                  pl.BlockSpec((K, D), lambda i: (0, 0))],
        out_specs=(pl.BlockSpec((HEADS_PER_TILE, tm, D), lambda i: (0, i, 0)),
                   pl.BlockSpec((per_tile, 1, D), lambda i: (i, 0, 0))),
        compiler_params=_cparams("parallel"),
        name="moba_keys",
    )(x2d, w_k)
    k_mean = jnp.stack(_split_bf16(k_mean.reshape(B, nb, D), KMEAN_SPLIT))
    return k_aug.reshape(HEADS_PER_TILE, B, S, D), k_mean


def _router_kernel(x_ref, wt_ref, bias_ref, idx_ref, gate_ref, rank_ref, cnt_ref, run_ref, *, tm):
    @pl.when(pl.program_id(0) == 0)
    def _():
        run_ref[...] = jnp.zeros_like(run_ref)

    E = wt_ref.shape[1]
    gsz = E // N_GROUPS
    logits = _three_pass(x_ref[...], wt_ref, lambda x, w: _nt_dot(w, x))
    s = jax.nn.sigmoid(logits)
    sb = s + bias_ref[...]
    neg_inf = -jnp.inf

    giota = lax.broadcasted_iota(jnp.int32, (gsz, tm), 0)
    gscore = []
    for g in range(N_GROUPS):
        blk = sb[g * gsz:(g + 1) * gsz, :]
        m1 = jnp.max(blk, axis=0, keepdims=True)
        i1 = jnp.min(jnp.where(blk == m1, giota, gsz), axis=0, keepdims=True)
        m2 = jnp.max(jnp.where(giota == i1, neg_inf, blk), axis=0, keepdims=True)
        gscore.append(m1 + m2)

    gsel = [jnp.zeros((1, tm), jnp.bool_) for _ in range(N_GROUPS)]
    for _ in range(TOPK_GROUPS):
        mx = functools.reduce(jnp.maximum, gscore)
        found = jnp.zeros((1, tm), jnp.bool_)
        for g in range(N_GROUPS):
            hit = (gscore[g] == mx) & jnp.logical_not(found)
            gsel[g] = gsel[g] | hit
            found = found | hit
            gscore[g] = jnp.where(hit, neg_inf, gscore[g])
    emask = jnp.concatenate([jnp.broadcast_to(gsel[g], (gsz, tm)) for g in range(N_GROUPS)], axis=0)
    cand = jnp.where(emask, sb, NEG)

    eiota = lax.broadcasted_iota(jnp.int32, (E, tm), 0)
    hits, idxs, ws = [], [], []
    for _ in range(TOP_K):
        mx = jnp.max(cand, axis=0, keepdims=True)
        first = jnp.min(jnp.where(cand == mx, eiota, E), axis=0, keepdims=True)
        hit = eiota == first
        hits.append(hit)
        idxs.append(first)
        ws.append(jnp.sum(jnp.where(hit, s, 0.0), axis=0, keepdims=True))
        cand = jnp.where(hit, neg_inf, cand)
    wsum = functools.reduce(jnp.add, ws)

    chosen = functools.reduce(jnp.logical_or, hits)
    onehot = jnp.where(chosen, 1.0, 0.0)
    tr = lax.broadcasted_iota(jnp.int32, (tm, tm), 0)
    tc = lax.broadcasted_iota(jnp.int32, (tm, tm), 1)
    before = (tr < tc).astype(BF16)
    prior = jnp.dot(onehot.astype(BF16), before, preferred_element_type=F32) + run_ref[...]
    grow = lax.broadcasted_iota(jnp.int32, (LANES, tm), 0)
    gates = jnp.zeros((LANES, tm), F32)
    for k in range(TOP_K):
        idx_ref[k:k + 1, :] = idxs[k]
        gates = jnp.where(grow == k, ws[k] / wsum * ROUTED_SCALE, gates)
        rank_ref[k:k + 1, :] = jnp.sum(jnp.where(hits[k], prior, 0.0), axis=0,
                                       keepdims=True).astype(jnp.int32)
    gate_ref[...] = gates.T
    run_ref[...] = run_ref[...] + jnp.sum(onehot, axis=1, keepdims=True)
    cnt_ref[...] = run_ref[...]


def _router(x2d, w_router, router_bias, tm):
    T, D = x2d.shape
    E = w_router.shape[1]
    tok = lambda i: (0, i)
    fixed = lambda i: (0, 0)
    return pl.pallas_call(
        functools.partial(_router_kernel, tm=tm),
        out_shape=(jax.ShapeDtypeStruct((TOP_K, T), jnp.int32),
                   jax.ShapeDtypeStruct((T, LANES), F32),
                   jax.ShapeDtypeStruct((TOP_K, T), jnp.int32),
                   jax.ShapeDtypeStruct((E, 1), F32)),
        grid=(T // tm,),
        in_specs=[pl.BlockSpec((tm, D), lambda i: (i, 0)),
                  pl.BlockSpec((2, E, D), lambda i: (0, 0, 0)),
                  pl.BlockSpec((E, 1), fixed)],
        out_specs=(pl.BlockSpec((TOP_K, tm), tok), pl.BlockSpec((tm, LANES), lambda i: (i, 0)),
                   pl.BlockSpec((TOP_K, tm), tok), pl.BlockSpec((E, 1), fixed)),
        scratch_shapes=[pltpu.VMEM((E, 1), F32)],
        compiler_params=_cparams("arbitrary"),
        name="router",
    )(x2d, jnp.stack(_split_bf16(w_router.T, 2)), router_bias.reshape(E, 1).astype(F32))


def _slots_kernel(idx_ref, rank_ref, start_ref, o_ref):
    E = start_ref.shape[0]
    tm = idx_ref.shape[1]
    eiota = lax.broadcasted_iota(jnp.int32, (E, tm), 0)
    start = start_ref[...]
    for k in range(TOP_K):
        base = jnp.sum(jnp.where(eiota == idx_ref[k:k + 1, :], start, 0.0), axis=0, keepdims=True)
        o_ref[k:k + 1, :] = base.astype(jnp.int32) + rank_ref[k:k + 1, :]


def _slots(idx, rank, seg_start, tm):
    T = idx.shape[1]
    E = seg_start.shape[0]
    tok = lambda i: (0, i)
    return pl.pallas_call(
        _slots_kernel,
        out_shape=jax.ShapeDtypeStruct((TOP_K, T), jnp.int32),
        grid=(T // tm,),
        in_specs=[pl.BlockSpec((TOP_K, tm), tok), pl.BlockSpec((TOP_K, tm), tok),
                  pl.BlockSpec((E, 1), lambda i: (0, 0))],
        out_specs=pl.BlockSpec((TOP_K, tm), tok),
        compiler_params=_cparams("parallel"),
        name="moe_slots",
    )(idx, rank, seg_start.astype(F32).reshape(E, 1))


SC_CORES = 2
SC_SUBCORES = 16
SC_CHUNK = 128


def _sc_rows_kernel(body, out_shape, dtype, scratch, name):
    mesh = plsc.VectorSubcoreMesh(core_axis_name="c", subcore_axis_name="s",
                                  num_cores=SC_CORES, num_subcores=SC_SUBCORES)
    return pl.kernel(body, mesh=mesh, out_type=jax.ShapeDtypeStruct(out_shape, dtype),
                     scratch_types=scratch, name=name)


def _sc_chunks(n):
    workers = SC_CORES * SC_SUBCORES
    chunks = n // (workers * SC_CHUNK)
    assert chunks * workers * SC_CHUNK == n
    return workers, chunks


SC_HALF = SC_CHUNK // 2


def _sc_gather_rows(table, idx):
    workers, chunks = _sc_chunks(idx.shape[0])
    width = table.shape[1]

    def body(table_hbm, idx_hbm, out_hbm, idx_v, rows_v, *sems):
        wid = lax.axis_index("s") * SC_CORES + lax.axis_index("c")
        pltpu.sync_copy(idx_hbm.at[wid], idx_v)

        @pl.loop(0, chunks)
        def _(j):
            pieces = [2 * j, 2 * j + 1]
            reads = [pltpu.async_copy(table_hbm.at[idx_v.at[p]], rows_v.at[b], sems[b])
                     for b, p in enumerate(pieces)]
            writes = []
            for b, p in enumerate(pieces):
                reads[b].wait()
                first = (wid * 2 * chunks + p) * SC_HALF
                writes.append(pltpu.async_copy(rows_v.at[b], out_hbm.at[pl.ds(first, SC_HALF)],
                                               sems[2 + b]))
            for write in writes:
                write.wait()

    scratch = [pltpu.VMEM((2 * chunks, SC_HALF), jnp.int32),
               pltpu.VMEM((2, SC_HALF, width), table.dtype)] + [pltpu.SemaphoreType.DMA] * 4
    call = _sc_rows_kernel(body, (idx.shape[0], width), table.dtype, scratch, "moe_gather_sc")
    return call(table, idx.reshape(workers, 2 * chunks, SC_HALF))


def _sc_scatter_rows(src, dest, n_rows):
    copies, n_src = dest.shape
    workers, chunks = _sc_chunks(n_src)

    def body(src_hbm, idx_hbm, out_hbm, idx_v, rows_v, sem):
        wid = lax.axis_index("s") * SC_CORES + lax.axis_index("c")
        pltpu.sync_copy(idx_hbm.at[wid], idx_v)

        @pl.loop(0, chunks)
        def _(j):
            pltpu.sync_copy(src_hbm.at[pl.ds((wid * chunks + j) * SC_CHUNK, SC_CHUNK)], rows_v)
            started = [pltpu.async_copy(rows_v, out_hbm.at[idx_v.at[j, k]], sem)
                       for k in range(copies)]
            for copy in started:
                copy.wait()

    scratch = [pltpu.VMEM((chunks, copies, SC_CHUNK), jnp.int32),
               pltpu.VMEM((SC_CHUNK, src.shape[1]), src.dtype), pltpu.SemaphoreType.DMA]
    call = _sc_rows_kernel(body, (n_rows, src.shape[1]), src.dtype, scratch, "moe_scatter_sc")
    idx = dest.reshape(copies, workers, chunks, SC_CHUNK).transpose(1, 2, 0, 3)
    return call(src, idx)


def _silu(g):
    return g * jax.nn.sigmoid(g)


SCHED_FIRST, SCHED_SLOT, SCHED_NEXT, SCHED_VALID = 0, 1, 2, 3


def _expert_kernel(sched_ref, n_used_ref, x_ref, wg_hbm, wu_hbm, wd_hbm, o_ref,
                   wg_buf, wu_buf, wd_buf, sem, *, layer, first_expert_row):
    i = pl.program_id(0)
    slot = sched_ref[SCHED_SLOT, i]

    def copies(expert, slot):
        pairs = ((wg_hbm, wg_buf), (wu_hbm, wu_buf), (wd_hbm, wd_buf))
        return [pltpu.make_async_copy(src.at[layer, expert], dst.at[slot], sem.at[slot, n])
                for n, (src, dst) in enumerate(pairs)]

    @pl.when(i == 0)
    def _():
        for c in copies(sched_ref[first_expert_row, 0], 0):
            c.start()

    @pl.when(sched_ref[SCHED_FIRST, i] == 1)
    def _():
        for c in copies(0, slot):
            c.wait()

        @pl.when(sched_ref[SCHED_NEXT, i] >= 0)
        def _():
            for c in copies(sched_ref[SCHED_NEXT, i], 1 - slot):
                c.start()

    @pl.when(i < n_used_ref[0])
    def _():
        words = x_ref[...]
        row = lax.broadcasted_iota(jnp.int32, words.shape, 0)
        words = jnp.where(row < sched_ref[SCHED_VALID, i], words, 0)
        x = jnp.concatenate(_unpack_halves(words), axis=1).astype(BF16)
        g = jnp.dot(x, wg_buf[slot].astype(BF16), preferred_element_type=F32)
        u = jnp.dot(x, wu_buf[slot].astype(BF16), preferred_element_type=F32)
        a = (_silu(g) * u).astype(BF16)
        o_ref[...] = _pack_halves(jnp.dot(a, wd_buf[slot].astype(BF16), preferred_element_type=F32))

    @pl.when(i >= n_used_ref[0])
    def _():
        o_ref[...] = jnp.zeros_like(o_ref)


def _expert_schedule(seg_start, seg_end, counts, n_blocks):
    E = seg_end.shape[0]
    first_row = jnp.arange(n_blocks, dtype=jnp.int32) * EXPERT_ROWS
    blk_e = jnp.minimum(jnp.sum(seg_end[None, :] <= first_row[:, None], axis=1), E - 1)
    used = first_row < seg_end[-1]
    first = used & (first_row == seg_start[blk_e])
    ordinal = jnp.cumsum(first) - 1
    ids = jnp.where(seg_end > seg_start, jnp.arange(E), E)
    later = lax.cummin(ids, reverse=True)
    nxt = jnp.concatenate([later[1:], jnp.full((1,), E, later.dtype)])[blk_e]
    nxt = jnp.where(nxt < E, nxt, -1)
    valid = jnp.clip((seg_start + counts)[blk_e] - first_row, 0, EXPERT_ROWS)
    return jnp.stack([first, ordinal % 2, nxt, valid, blk_e]).astype(jnp.int32)


def _expert_mlp(xs, sched, n_used, w_gate, w_up, w_down, layer):
    n_rows, half = xs.shape
    D = 2 * half
    F = w_gate.shape[-1]
    tm = EXPERT_ROWS
    hbm = pl.BlockSpec(memory_space=pl.ANY)
    return pl.pallas_call(
        functools.partial(_expert_kernel, layer=layer, first_expert_row=sched.shape[0] - 1),
        out_shape=jax.ShapeDtypeStruct((n_rows, half), jnp.int32),
        grid_spec=pltpu.PrefetchScalarGridSpec(
            num_scalar_prefetch=2,
            grid=(n_rows // tm,),
            in_specs=[pl.BlockSpec((tm, half), lambda i, sc, nu: (jnp.minimum(i, nu[0] - 1), 0)),
                      hbm, hbm, hbm],
            out_specs=pl.BlockSpec((tm, half), lambda i, sc, nu: (i, 0)),
            scratch_shapes=[pltpu.VMEM((2, D, F), F32), pltpu.VMEM((2, D, F), F32),
                            pltpu.VMEM((2, F, D), F32), pltpu.SemaphoreType.DMA((2, 3))]),
        compiler_params=_cparams("arbitrary"),
        name="moe_experts",
    )(sched, n_used, xs, w_gate, w_up, w_down)


def _combine_kernel(x_ref, gate_ref, rows_ref, wsg_ref, wsu_ref, wsd_ref, g_ref, b_ref, o_ref,
                    *, alpha):
    x = x_ref[...]
    xb = x.astype(BF16)
    g = jnp.dot(xb, wsg_ref[...], preferred_element_type=F32)
    u = jnp.dot(xb, wsu_ref[...], preferred_element_type=F32)
    y = jnp.dot((_silu(g) * u).astype(BF16), wsd_ref[...], preferred_element_type=F32)
    gates = gate_ref[...]
    half = y.shape[1] // 2
    left, right = y[:, :half], y[:, half:]
    for k in range(TOP_K):
        lo, hi = _unpack_halves(rows_ref[k])
        left = left + lo * gates[:, k:k + 1]
        right = right + hi * gates[:, k:k + 1]
    y = jnp.concatenate([left, right], axis=1)
    o_ref[...] = _layer_norm(alpha * x + y, g_ref[...], b_ref[...])


def _combine(x2d, ys, dest, gates_tok, ws_gate, ws_up, ws_down, g, b, alpha, tt):
    T, D = x2d.shape
    F = ws_gate.shape[1]
    row = lambda i: (i, 0)
    fixed = lambda i: (0, 0)
    rows = _sc_gather_rows(ys, dest.reshape(-1)).reshape(TOP_K, T, D // 2)
    return pl.pallas_call(
        functools.partial(_combine_kernel, alpha=alpha),
        out_shape=jax.ShapeDtypeStruct((T, D), F32),
        grid=(T // tt,),
        in_specs=[pl.BlockSpec((tt, D), row),
                  pl.BlockSpec((tt, LANES), row),
                  pl.BlockSpec((TOP_K, tt, D // 2), lambda i: (0, i, 0)),
                  pl.BlockSpec((D, F), fixed), pl.BlockSpec((D, F), fixed),
                  pl.BlockSpec((F, D), fixed),
                  pl.BlockSpec((1, D), fixed), pl.BlockSpec((1, D), fixed)],
        out_specs=pl.BlockSpec((tt, D), row),
        compiler_params=_cparams("parallel"),
        name="moe_combine",
    )(x2d, gates_tok, rows, ws_gate.astype(BF16), ws_up.astype(BF16), ws_down.astype(BF16),
      g.reshape(1, D), b.reshape(1, D))


def _pick(n, pref):
    t = min(n, pref)
    while n % t:
        t //= 2
    return t


def _moe_layer(x2d, packed, w_router, router_bias, w_gate, w_up, w_down, layer, ws_gate, ws_up,
               ws_down, g, b, alpha):
    T, D = x2d.shape
    E = w_router.shape[1]
    idx, gates, rank, counts = _router(x2d, w_router, router_bias, _pick(T, 256))
    counts = counts[:, 0].astype(jnp.int32)
    padded = (counts + EXPERT_ROWS - 1) // EXPERT_ROWS * EXPERT_ROWS
    seg_end = jnp.cumsum(padded)
    seg_start = seg_end - padded
    n_blocks = (T * TOP_K + E * (EXPERT_ROWS - 1)) // EXPERT_ROWS
    dest = _slots(idx, rank, seg_start, _pick(T, 1024))
    sched = _expert_schedule(seg_start, seg_end, counts, n_blocks)
    n_used = (seg_end[-1:] // EXPERT_ROWS).astype(jnp.int32)
    xs = _sc_scatter_rows(packed, dest, n_blocks * EXPERT_ROWS)
    ys = _expert_mlp(xs, sched, n_used, w_gate, w_up, w_down, layer)
    return _combine(x2d, ys, dest, gates, ws_gate, ws_up, ws_down, g, b, alpha, _pick(T, 256))


def _mixer_layer(x2d, i, B, S, p, alpha):
    T, D = x2d.shape
    tm = _pick(T, 512)
    tn = _pick(D, 1024)
    j = i // 2
    w_in = p["moba_w_in"][j] if i % 2 == 0 else p["fox_w_in"][j]
    w_qv_t = jnp.concatenate([w_in[:, :D].T * (HEAD_DIM ** -0.5 * LOG2E),
                              w_in[:, 2 * D:3 * D].T]).astype(BF16)
    qv_t = _project_t(x2d, w_qv_t, tm, tn)
    w_k = w_in[:, D:2 * D].astype(BF16)
    if i % 2 == 0:
        k_aug, k_mean = _moba_keys(x2d, w_k, B, S, tm)
        attn_t = _moba_attention(qv_t, k_aug, k_mean, p["rel_bias"], B, S)
        w_out = p["moba_w_out"][j]
    else:
        c_tok = _fox_gates(x2d.reshape(B, S, D), w_in[:, 3 * D:], p["fox_b_f"][j], _pick(S, 512))
        k_aug = _fox_keys(x2d, w_k, c_tok.reshape(T, LANES), tm).reshape(HEADS_PER_TILE, B, S, D)
        attn_t = _fox_attention(qv_t, k_aug, B, S, _pick(S, 256))
        w_out = p["fox_w_out"][j]
    return _outproj_ln(attn_t, w_out.astype(BF16), x2d, p["ln1_g"][i], p["ln1_b"][i], alpha, tm)


def kernel(x, rel_bias, moba_w_in, moba_w_out, fox_w_in, fox_b_f, fox_w_out, ln1_g, ln1_b, ln2_g,
           ln2_b, w_router, router_bias, w_gate, w_up, w_down, ws_gate, ws_up, ws_down):
    B, S, D = x.shape
    depth = ln1_g.shape[0]
    alpha = (2 * depth) ** 0.25
    p = dict(rel_bias=rel_bias, moba_w_in=moba_w_in, moba_w_out=moba_w_out, fox_w_in=fox_w_in,
             fox_b_f=fox_b_f, fox_w_out=fox_w_out, ln1_g=ln1_g, ln1_b=ln1_b)
    x2d = x.reshape(B * S, D)
    for i in range(depth):
        x2d, packed = _mixer_layer(x2d, i, B, S, p, alpha)
        x2d = _moe_layer(x2d, packed, w_router[i], router_bias[i], w_gate, w_up, w_down, i,
                         ws_gate[i], ws_up[i], ws_down[i], ln2_g[i], ln2_b[i], alpha)
    return x2d.reshape(B, S, D)
```

```python
import functools
import math

import jax
import jax.numpy as jnp
from jax import lax
from jax.experimental import pallas as pl
from jax.experimental.pallas import tpu as pltpu
from jax.experimental.pallas import tpu_sc as plsc

F32 = jnp.float32
BF16 = jnp.bfloat16
HIGHEST = lax.Precision.HIGHEST

HEAD_DIM = 64
MOBA_BLOCK = 256
MOBA_TOPK = 3
REL_MAX_DIST = 128
TOP_K = 8
N_GROUPS = 8
TOPK_GROUPS = 4
ROUTED_SCALE = 2.5
LN_EPS = 1e-5
NEG = -1e30
LOG2E = math.log2(math.e)

LANES = 128
HEADS_PER_TILE = LANES // HEAD_DIM
EXPERT_ROWS = 512
VMEM_LIMIT = 48 * 1024 * 1024


def _cparams(*sem):
    return pltpu.CompilerParams(dimension_semantics=sem, vmem_limit_bytes=VMEM_LIMIT)


def _nt_dot(a, b, **kw):
    return lax.dot_general(a, b, (((1,), (1,)), ((), ())), preferred_element_type=F32, **kw)


def _split_bf16(x, pieces):
    out = []
    for _ in range(pieces):
        out.append(x.astype(BF16))
        x = x - out[-1].astype(F32)
    return out


def _three_pass(x, w_ref, mul):
    hi, lo = _split_bf16(x, 2)
    return mul(hi, w_ref[0]) + mul(lo, w_ref[0]) + mul(hi, w_ref[1])


def _proj_t_kernel(x_ref, wt_ref, o_ref):
    o_ref[...] = _nt_dot(wt_ref[...], x_ref[...].astype(BF16)).astype(o_ref.dtype)


def _project_t(x2d, wt, tm, tn):
    T, K = x2d.shape
    N = wt.shape[0]
    return pl.pallas_call(
        _proj_t_kernel,
        out_shape=jax.ShapeDtypeStruct((N, T), BF16),
        grid=(T // tm, N // tn),
        in_specs=[pl.BlockSpec((tm, K), lambda i, j: (i, 0)),
                  pl.BlockSpec((tn, K), lambda i, j: (j, 0))],
        out_specs=pl.BlockSpec((tn, tm), lambda i, j: (j, i)),
        compiler_params=_cparams("parallel", "arbitrary"),
        name="qv_proj_t",
    )(x2d, wt)


def _layer_norm(r, g, b):
    mu = jnp.mean(r, axis=-1, keepdims=True)
    d = r - mu
    var = jnp.mean(d * d, axis=-1, keepdims=True)
    return d * lax.rsqrt(var + LN_EPS) * g + b


def _pack_halves(y):
    n = y.shape[1] // 2
    bits = lambda v: lax.bitcast_convert_type(v.astype(BF16).astype(F32), jnp.int32)
    left = bits(y[:, :n])
    return lax.shift_right_logical(left, jnp.full_like(left, 16)) | bits(y[:, n:])


def _unpack_halves(p):
    return (lax.bitcast_convert_type(p << 16, F32),
            lax.bitcast_convert_type(p & jnp.int32(-65536), F32))


def _outproj_ln_kernel(at_ref, w_ref, x_ref, g_ref, b_ref, o_ref, packed_ref, *, alpha):
    y = lax.dot_general(at_ref[...], w_ref[...], (((0,), (0,)), ((), ())),
                        preferred_element_type=F32)
    out = _layer_norm(alpha * x_ref[...] + y, g_ref[...], b_ref[...])
    o_ref[...] = out
    packed_ref[...] = _pack_halves(out)


def _outproj_ln(attn_t, w_out, x2d, g, b, alpha, tm):
    T, D = x2d.shape
    row = lambda i: (i, 0)
    fixed = lambda i: (0, 0)
    return pl.pallas_call(
        functools.partial(_outproj_ln_kernel, alpha=alpha),
        out_shape=(jax.ShapeDtypeStruct((T, D), F32), jax.ShapeDtypeStruct((T, D // 2), jnp.int32)),
        grid=(T // tm,),
        in_specs=[pl.BlockSpec((D, tm), lambda i: (0, i)), pl.BlockSpec((D, D), fixed),
                  pl.BlockSpec((tm, D), row), pl.BlockSpec((1, D), fixed),
                  pl.BlockSpec((1, D), fixed)],
        out_specs=(pl.BlockSpec((tm, D), row), pl.BlockSpec((tm, D // 2), row)),
        compiler_params=_cparams("parallel"),
        name="outproj_ln",
    )(attn_t, w_out, x2d, g.reshape(1, D), b.reshape(1, D))


def _spare(a, n=0):
    return ((a + 1) % HEADS_PER_TILE) * HEAD_DIM + n


def _scores(chains):
    return [[jnp.dot(keys(), query(), preferred_element_type=F32) for keys, query, _ in blocks]
            for blocks in chains]


def _absorb(carries, scores, chains, masks=None, adds=None):
    probs = []
    for c, ((m, _), parts) in enumerate(zip(carries, scores)):
        if adds is not None:
            parts = [s if add is None else s + add() for add, s in zip(adds[c], parts)]
        if masks is not None:
            parts = [s if keep is None else jnp.where(keep, s, NEG)
                     for keep, s in zip(masks[c], parts)]
        tops = [jnp.max(s, axis=0, keepdims=True) for s in parts]
        m_new = functools.reduce(jnp.maximum, tops, m)
        probs.append((m_new, jnp.exp2(m - m_new),
                      [jnp.exp2(s - m_new).astype(BF16) for s in parts]))
    out = []
    for (_, acc), (m_new, decay, ps), blocks in zip(carries, probs, chains):
        acc = decay * acc
        for (_, _, values), p in zip(blocks, ps):
            acc = acc + jnp.dot(values(), p, preferred_element_type=F32)
        out.append((m_new, acc))
    return tuple(out)


def _stash(chains, ref):
    for a, parts in enumerate(_scores(chains)):
        for b, s in enumerate(parts):
            ref[a, b] = s


def _fetch(chains, ref):
    return [[ref[a, b] for b in range(len(blocks))] for a, blocks in enumerate(chains)]


def _attend_masked_last(carries, n, group, stage_refs, masks, adds=None):
    first, second = stage_refs
    _stash(group(0), first)
    trips = n // 2

    def body(t, carries):
        g = 2 * t
        _stash(group(g + 1), second)
        carries = _absorb(carries, _fetch(group(g), first), group(g))
        _stash(group(g + 2), first)
        return _absorb(carries, _fetch(group(g + 1), second), group(g + 1))

    carries = lax.fori_loop(0, trips, body, carries)

    def odd_tail(carries):
        _stash(group(n), second)
        carries = _absorb(carries, _fetch(group(n - 1), first), group(n - 1))
        return _absorb(carries, _fetch(group(n), second), group(n), masks, adds)

    def even_tail(carries):
        return _absorb(carries, _fetch(group(n), first), group(n), masks, adds)

    return lax.cond(n % 2 == 1, odd_tail, even_tail, carries)


GROUP_BLOCKS = 2


def _stage_scratch(tq, chains=HEADS_PER_TILE):
    return [pltpu.VMEM((chains, GROUP_BLOCKS, tq, tq), F32) for _ in range(2)]


def _scores_init(tq):
    return jnp.full((1, tq), -jnp.inf, F32), jnp.zeros((LANES, tq), F32)


def _finish(carries, o_ref, cols=slice(None)):
    row = lax.broadcasted_iota(jnp.int32, carries[0][1].shape, 0)
    outs = []
    for a, (_, acc) in enumerate(carries):
        s = _spare(a)
        outs.append(acc / acc[s:s + 1, :])
    o_ref[:, cols] = jnp.where(row < HEAD_DIM, outs[0], outs[1]).astype(o_ref.dtype)


def _fill_v_aug(vt_ref, vaug_ref, tk):
    n_tiles = vt_ref.shape[1] // tk
    row = lax.broadcasted_iota(jnp.int32, (LANES, tk), 0)
    for n in range(n_tiles):
        v = vt_ref[:, n * tk:(n + 1) * tk]
        for a in range(HEADS_PER_TILE):
            vaug_ref[a, n] = jnp.where(row // HEAD_DIM == a, v, jnp.ones_like(v))


FOX_SPLIT = 3


def _fox_kernel(qt_ref, k_ref, vt_ref, o_ref, vaug_ref, stage0_ref, stage1_ref, *, tq):
    n = pl.program_id(2)

    @pl.when(n == 0)
    def _():
        _fill_v_aug(vt_ref, vaug_ref, tq)

    chains = [(h, a) for h in range(GROUP_BLOCKS) for a in range(HEADS_PER_TILE)]
    row = lax.broadcasted_iota(jnp.int32, (LANES, tq), 0)
    queries = {}
    for h, a in chains:
        offs = (row >= _spare(a)) & (row < _spare(a, FOX_SPLIT))
        queries[h, a] = jnp.where(row // HEAD_DIM == a, qt_ref[:, h * tq:(h + 1) * tq],
                                  jnp.where(offs, -1.0, 0.0).astype(BF16))

    def block(h, a, j):
        start = pl.multiple_of(j * tq, tq)
        return (lambda: k_ref[a, 0, pl.ds(start, tq), :], lambda: queries[h, a],
                lambda: vaug_ref[a, j])

    key = lax.broadcasted_iota(jnp.int32, (tq, tq), 0)
    qry = lax.broadcasted_iota(jnp.int32, (tq, tq), 1)
    masks = [[None if b < h else key + (b - h) * tq <= qry for b in range(GROUP_BLOCKS)]
             for h, _ in chains]
    carries = _attend_masked_last(
        tuple(_scores_init(tq) for _ in chains), n,
        lambda j: [[block(h, a, 2 * j), block(h, a, 2 * j + 1)] for h, a in chains],
        (stage0_ref, stage1_ref), masks)
    for h in range(GROUP_BLOCKS):
        _finish(carries[h * HEADS_PER_TILE:(h + 1) * HEADS_PER_TILE], o_ref,
                slice(h * tq, (h + 1) * tq))


def _fox_attention(qv_t, k_aug, B, S, tq):
    D = k_aug.shape[-1]
    n_tiles = D // LANES
    nq = S // tq
    span = GROUP_BLOCKS * tq
    steps = S // span
    return pl.pallas_call(
        functools.partial(_fox_kernel, tq=tq),
        out_shape=jax.ShapeDtypeStruct((D, B * S), BF16),
        grid=(B, n_tiles, steps),
        in_specs=[pl.BlockSpec((LANES, span), lambda b, h, i: (h, b * steps + i)),
                  pl.BlockSpec((HEADS_PER_TILE, 1, S, LANES), lambda b, h, i: (0, b, 0, h)),
                  pl.BlockSpec((LANES, S), lambda b, h, i: (n_tiles + h, b))],
        out_specs=pl.BlockSpec((LANES, span), lambda b, h, i: (h, b * steps + i)),
        scratch_shapes=[pltpu.VMEM((HEADS_PER_TILE, nq, LANES, tq), BF16)]
        + _stage_scratch(tq, GROUP_BLOCKS * HEADS_PER_TILE),
        compiler_params=_cparams("parallel", "parallel", "arbitrary"),
        name="fox_attention",
    )(qv_t, k_aug, qv_t)


def _log_sigmoid(z):
    return jnp.minimum(z, 0.0) - jnp.log1p(jnp.exp(-jnp.abs(z)))


def _fox_gates_kernel(x_ref, w_ref, b_ref, o_ref, carry_ref, *, ts):
    @pl.when(pl.program_id(1) == 0)
    def _():
        carry_ref[...] = jnp.zeros_like(carry_ref)

    z = _three_pass(x_ref[0], w_ref, lambda x, w: jnp.dot(x, w, preferred_element_type=F32))
    lf = _log_sigmoid(z + b_ref[...])
    r = lax.broadcasted_iota(jnp.int32, (ts, ts), 0)
    c = lax.broadcasted_iota(jnp.int32, (ts, ts), 1)
    tri = (c <= r).astype(BF16)
    cum = carry_ref[...]
    for piece in _split_bf16(lf, 3):
        cum = cum + jnp.dot(tri, piece, preferred_element_type=F32)
    carry_ref[...] = cum[ts - 1:ts, :]
    o_ref[0] = cum


def _fox_gates(x3d, w_f, b_f, ts):
    B, S, D = x3d.shape
    H = w_f.shape[1]
    w_pad = jnp.stack(_split_bf16(jnp.zeros((D, LANES), F32).at[:, :H].set(w_f), 2))
    b_pad = jnp.zeros((1, LANES), F32).at[0, :H].set(b_f)
    return pl.pallas_call(
        functools.partial(_fox_gates_kernel, ts=ts),
        out_shape=jax.ShapeDtypeStruct((B, S, LANES), F32),
        grid=(B, S // ts),
        in_specs=[pl.BlockSpec((1, ts, D), lambda b, s: (b, s, 0)),
                  pl.BlockSpec((2, D, LANES), lambda b, s: (0, 0, 0)),
                  pl.BlockSpec((1, LANES), lambda b, s: (0, 0))],
        out_specs=pl.BlockSpec((1, ts, LANES), lambda b, s: (b, s, 0)),
        scratch_shapes=[pltpu.VMEM((1, LANES), F32)],
        compiler_params=_cparams("parallel", "arbitrary"),
        name="fox_gates",
    )(x3d, w_pad, b_pad)


def _fox_keys_kernel(x_ref, w_ref, c_ref, o_ref):
    keys = jnp.dot(x_ref[...].astype(BF16), w_ref[...], preferred_element_type=F32)
    tm, D = keys.shape
    lane = lax.broadcasted_iota(jnp.int32, (tm, LANES), 1)
    for t in range(D // LANES):
        k = keys[:, t * LANES:(t + 1) * LANES]
        for a in range(HEADS_PER_TILE):
            h = t * HEADS_PER_TILE + a
            rest = c_ref[:, h:h + 1] * LOG2E
            aug = jnp.zeros((tm, LANES), F32)
            for n in range(FOX_SPLIT):
                piece = rest.astype(BF16).astype(F32)
                rest = rest - piece
                aug = jnp.where(lane == _spare(a, n), piece, aug)
            o_ref[a, :, t * LANES:(t + 1) * LANES] = jnp.where(lane // HEAD_DIM == a, k,
                                                               aug).astype(BF16)


def _fox_keys(x2d, w_k, c_tok, tm):
    T, K = x2d.shape
    D = w_k.shape[1]
    return pl.pallas_call(
        _fox_keys_kernel,
        out_shape=jax.ShapeDtypeStruct((HEADS_PER_TILE, T, D), BF16),
        grid=(T // tm,),
        in_specs=[pl.BlockSpec((tm, K), lambda i: (i, 0)),
                  pl.BlockSpec((K, D), lambda i: (0, 0)),
                  pl.BlockSpec((tm, LANES), lambda i: (i, 0))],
        out_specs=pl.BlockSpec((HEADS_PER_TILE, tm, D), lambda i: (0, i, 0)),
        compiler_params=_cparams("parallel"),
        name="fox_keys",
    )(x2d, w_k, c_tok)


MOBA_SLAB = 16
KMEAN_SPLIT = 3


def _moba_kernel(bfar_ref, qt_ref, k_ref, vt_ref, kmean_ref, bown_ref, badj_ref, o_ref,
                 vaug_ref, sel_ref, stage0_ref, stage1_ref, *, blk):
    t = pl.program_id(1)
    n = pl.program_id(2)
    nb = kmean_ref.shape[2]

    @pl.when(n == 0)
    def _():
        _fill_v_aug(vt_ref, vaug_ref, blk)

    chains = [(h, a) for h in range(GROUP_BLOCKS) for a in range(HEADS_PER_TILE)]
    row = lax.broadcasted_iota(jnp.int32, (LANES, blk), 0)
    brow = lax.broadcasted_iota(jnp.int32, (nb, blk), 0)
    base = {}
    for c, (h, a) in enumerate(chains):
        own = GROUP_BLOCKS * n + h
        qt = qt_ref[:, h * blk:(h + 1) * blk]
        qa = jnp.where(row // HEAD_DIM == a, qt, jnp.zeros_like(qt))
        base[h, a] = qa
        bscore = sum(jnp.dot(kmean_ref[piece, 0], qa, preferred_element_type=F32)
                     for piece in range(KMEAN_SPLIT))
        cand = jnp.where(brow < own, bscore, NEG)
        sel = jnp.zeros((nb, blk), F32)
        for _ in range(MOBA_TOPK):
            mx = jnp.max(cand, axis=0, keepdims=True)
            first = jnp.min(jnp.where(cand == mx, brow, nb), axis=0, keepdims=True)
            hit = brow == first
            sel = jnp.where(hit, 1.0, sel)
            cand = jnp.where(hit, -jnp.inf, cand)
        sel_ref[c] = jnp.where(brow < own, sel, 0.0)

    srow = lax.broadcasted_iota(jnp.int32, (MOBA_SLAB, blk), 0)

    def block(c, j):
        h, a = chains[c]
        head_id = t * HEADS_PER_TILE + a
        jc = jnp.maximum(j, 0)
        start = pl.multiple_of(jc * blk, blk)

        def query():
            s0 = _spare(a) // MOBA_SLAB * MOBA_SLAB
            chosen = (((sel_ref[c, pl.ds(jc, 1), :] > 0.0) & (j >= 0))
                      | (j == GROUP_BLOCKS * n + h))
            slab = jnp.where(srow == _spare(a) - s0,
                             jnp.where(chosen, bfar_ref[0, head_id], NEG),
                             jnp.where(srow == _spare(a, 1) - s0, bfar_ref[1, head_id], 0.0)
                             ).astype(BF16)
            q = base[h, a]
            head = [q[:s0]] if s0 else []
            tail = [q[s0 + MOBA_SLAB:]] if s0 + MOBA_SLAB < LANES else []
            return jnp.concatenate(head + [slab] + tail, axis=0)

        return lambda: k_ref[a, 0, pl.ds(start, blk), :], query, lambda: vaug_ref[a, jc]

    key = lax.broadcasted_iota(jnp.int32, (blk, blk), 0)
    qry = lax.broadcasted_iota(jnp.int32, (blk, blk), 1)
    carries = _attend_masked_last(
        tuple(_scores_init(blk) for _ in chains), n,
        lambda g: [[block(c, 2 * g + h - 1), block(c, 2 * g + h)] for c, (h, _) in enumerate(chains)],
        (stage0_ref, stage1_ref), [[None, key <= qry] for _ in chains],
        [[lambda a=a: badj_ref[a], lambda a=a: bown_ref[a]] for _, a in chains])
    for h in range(GROUP_BLOCKS):
        _finish(carries[h * HEADS_PER_TILE:(h + 1) * HEADS_PER_TILE], o_ref,
                slice(h * blk, (h + 1) * blk))


def _rel_bucket(dist, n_buckets):
    n = jnp.maximum(dist, 0)
    max_exact = n_buckets // 2
    nf = jnp.maximum(n, 1).astype(F32)
    large = max_exact + (jnp.log(nf / max_exact) / math.log(REL_MAX_DIST / max_exact)
                         * (n_buckets - max_exact)).astype(jnp.int32)
    large = jnp.minimum(large, n_buckets - 1)
    return jnp.where(n < max_exact, n, large)


def _moba_attention(qv_t, k_aug, k_mean, rel_bias, B, S):
    D = k_aug.shape[-1]
    n_tiles = D // LANES
    blk = MOBA_BLOCK
    nb = S // blk
    hp = HEADS_PER_TILE
    span = GROUP_BLOCKS * blk
    steps = S // span
    n_chains = GROUP_BLOCKS * hp
    r = jnp.arange(blk)
    delta = r[None, :] - r[:, None]
    bias_t = rel_bias.T.astype(F32)
    n_buckets = rel_bias.shape[0]
    def table(dist):
        onehot = jax.nn.one_hot(_rel_bucket(dist, n_buckets), n_buckets, dtype=F32)
        return jnp.einsum('crn,hn->hcr', onehot, bias_t, precision=HIGHEST)

    b_far = bias_t[:, n_buckets - 1] * LOG2E
    b_own = table(delta) * LOG2E - b_far[:, None, None]
    b_adj = table(delta + blk) * LOG2E - b_far[:, None, None]
    far_hi = b_far.astype(BF16).astype(F32)
    far_lo = (b_far - far_hi).astype(BF16).astype(F32)
    b_far2 = jnp.stack([far_hi, far_lo])
    return pl.pallas_call(
        functools.partial(_moba_kernel, blk=blk),
        out_shape=jax.ShapeDtypeStruct((D, B * S), BF16),
        grid_spec=pltpu.PrefetchScalarGridSpec(
            num_scalar_prefetch=1,
            grid=(B, n_tiles, steps),
            in_specs=[pl.BlockSpec((LANES, span), lambda b, h, i, f: (h, b * steps + i)),
                      pl.BlockSpec((hp, 1, S, LANES), lambda b, h, i, f: (0, b, 0, h)),
                      pl.BlockSpec((LANES, S), lambda b, h, i, f: (n_tiles + h, b)),
                      pl.BlockSpec((KMEAN_SPLIT, 1, nb, LANES), lambda b, h, i, f: (0, b, 0, h)),
                      pl.BlockSpec((hp, blk, blk), lambda b, h, i, f: (h, 0, 0)),
                      pl.BlockSpec((hp, blk, blk), lambda b, h, i, f: (h, 0, 0))],
            out_specs=pl.BlockSpec((LANES, span), lambda b, h, i, f: (h, b * steps + i)),
            scratch_shapes=[pltpu.VMEM((hp, nb, LANES, blk), BF16),
                            pltpu.VMEM((n_chains, nb, blk), F32)] + _stage_scratch(blk, n_chains)),
        compiler_params=_cparams("parallel", "parallel", "arbitrary"),
        name="moba_attention",
    )(b_far2, qv_t, k_aug, qv_t, k_mean, b_own, b_adj)


def _moba_keys_kernel(x_ref, w_ref, o_ref, mean_ref):
    k = jnp.dot(x_ref[...].astype(BF16), w_ref[...], preferred_element_type=F32)
    col = lax.broadcasted_iota(jnp.int32, k.shape, 1)
    lane = col % LANES
    for a in range(HEADS_PER_TILE):
        ones = (lane == _spare(a)) | (lane == _spare(a, 1))
        o_ref[a] = jnp.where((col // HEAD_DIM) % HEADS_PER_TILE == a, k,
                             jnp.where(ones, 1.0, 0.0)).astype(BF16)
    for n in range(mean_ref.shape[0]):
        mean_ref[n] = jnp.mean(k[n * MOBA_BLOCK:(n + 1) * MOBA_BLOCK], axis=0, keepdims=True)


def _moba_keys(x2d, w_k, B, S, tm):
    T, K = x2d.shape
    D = w_k.shape[1]
    nb = S // MOBA_BLOCK
    per_tile = tm // MOBA_BLOCK
    k_aug, k_mean = pl.pallas_call(
        _moba_keys_kernel,
        out_shape=(jax.ShapeDtypeStruct((HEADS_PER_TILE, T, D), BF16),
                   jax.ShapeDtypeStruct((B * nb, 1, D), F32)),
        grid=(T // tm,),
        in_specs=[pl.BlockSpec((tm, K), lambda i: (i, 0)),
                  pl.BlockSpec((K, D), lambda i: (0, 0))],
        out_specs=(pl.BlockSpec((HEADS_PER_TILE, tm, D), lambda i: (0, i, 0)),
                   pl.BlockSpec((per_tile, 1, D), lambda i: (i, 0, 0))),
        compiler_params=_cparams("parallel"),
        name="moba_keys",
    )(x2d, w_k)
    k_mean = jnp.stack(_split_bf16(k_mean.reshape(B, nb, D), KMEAN_SPLIT))
    return k_aug.reshape(HEADS_PER_TILE, B, S, D), k_mean


def _router_kernel(x_ref, wt_ref, bias_ref, idx_ref, gate_ref, rank_ref, cnt_ref, run_ref, *, tm):
    @pl.when(pl.program_id(0) == 0)
    def _():
        run_ref[...] = jnp.zeros_like(run_ref)

    E = wt_ref.shape[1]
    gsz = E // N_GROUPS
    logits = _three_pass(x_ref[...], wt_ref, lambda x, w: _nt_dot(w, x))
    s = jax.nn.sigmoid(logits)
    sb = s + bias_ref[...]
    neg_inf = -jnp.inf

    giota = lax.broadcasted_iota(jnp.int32, (gsz, tm), 0)
    gscore = []
    for g in range(N_GROUPS):
        blk = sb[g * gsz:(g + 1) * gsz, :]
        m1 = jnp.max(blk, axis=0, keepdims=True)
        i1 = jnp.min(jnp.where(blk == m1, giota, gsz), axis=0, keepdims=True)
        m2 = jnp.max(jnp.where(giota == i1, neg_inf, blk), axis=0, keepdims=True)
        gscore.append(m1 + m2)

    gsel = [jnp.zeros((1, tm), jnp.bool_) for _ in range(N_GROUPS)]
    for _ in range(TOPK_GROUPS):
        mx = functools.reduce(jnp.maximum, gscore)
        found = jnp.zeros((1, tm), jnp.bool_)
        for g in range(N_GROUPS):
            hit = (gscore[g] == mx) & jnp.logical_not(found)
            gsel[g] = gsel[g] | hit
            found = found | hit
            gscore[g] = jnp.where(hit, neg_inf, gscore[g])
    emask = jnp.concatenate([jnp.broadcast_to(gsel[g], (gsz, tm)) for g in range(N_GROUPS)], axis=0)
    cand = jnp.where(emask, sb, NEG)

    eiota = lax.broadcasted_iota(jnp.int32, (E, tm), 0)
    hits, idxs, ws = [], [], []
    for _ in range(TOP_K):
        mx = jnp.max(cand, axis=0, keepdims=True)
        first = jnp.min(jnp.where(cand == mx, eiota, E), axis=0, keepdims=True)
        hit = eiota == first
        hits.append(hit)
        idxs.append(first)
        ws.append(jnp.sum(jnp.where(hit, s, 0.0), axis=0, keepdims=True))
        cand = jnp.where(hit, neg_inf, cand)
    wsum = functools.reduce(jnp.add, ws)

    chosen = functools.reduce(jnp.logical_or, hits)
    onehot = jnp.where(chosen, 1.0, 0.0)
    tr = lax.broadcasted_iota(jnp.int32, (tm, tm), 0)
    tc = lax.broadcasted_iota(jnp.int32, (tm, tm), 1)
    before = (tr < tc).astype(BF16)
    prior = jnp.dot(onehot.astype(BF16), before, preferred_element_type=F32) + run_ref[...]
    grow = lax.broadcasted_iota(jnp.int32, (LANES, tm), 0)
    gates = jnp.zeros((LANES, tm), F32)
    for k in range(TOP_K):
        idx_ref[k:k + 1, :] = idxs[k]
        gates = jnp.where(grow == k, ws[k] / wsum * ROUTED_SCALE, gates)
        rank_ref[k:k + 1, :] = jnp.sum(jnp.where(hits[k], prior, 0.0), axis=0,
                                       keepdims=True).astype(jnp.int32)
    gate_ref[...] = gates.T
    run_ref[...] = run_ref[...] + jnp.sum(onehot, axis=1, keepdims=True)
    cnt_ref[...] = run_ref[...]


def _router(x2d, w_router, router_bias, tm):
    T, D = x2d.shape
    E = w_router.shape[1]
    tok = lambda i: (0, i)
    fixed = lambda i: (0, 0)
    return pl.pallas_call(
        functools.partial(_router_kernel, tm=tm),
        out_shape=(jax.ShapeDtypeStruct((TOP_K, T), jnp.int32),
                   jax.ShapeDtypeStruct((T, LANES), F32),
                   jax.ShapeDtypeStruct((TOP_K, T), jnp.int32),
                   jax.ShapeDtypeStruct((E, 1), F32)),
        grid=(T // tm,),
        in_specs=[pl.BlockSpec((tm, D), lambda i: (i, 0)),
                  pl.BlockSpec((2, E, D), lambda i: (0, 0, 0)),
                  pl.BlockSpec((E, 1), fixed)],
        out_specs=(pl.BlockSpec((TOP_K, tm), tok), pl.BlockSpec((tm, LANES), lambda i: (i, 0)),
                   pl.BlockSpec((TOP_K, tm), tok), pl.BlockSpec((E, 1), fixed)),
        scratch_shapes=[pltpu.VMEM((E, 1), F32)],
        compiler_params=_cparams("arbitrary"),
        name="router",
    )(x2d, jnp.stack(_split_bf16(w_router.T, 2)), router_bias.reshape(E, 1).astype(F32))


def _slots_kernel(idx_ref, rank_ref, start_ref, o_ref):
    E = start_ref.shape[0]
    tm = idx_ref.shape[1]
    eiota = lax.broadcasted_iota(jnp.int32, (E, tm), 0)
    start = start_ref[...]
    for k in range(TOP_K):
        base = jnp.sum(jnp.where(eiota == idx_ref[k:k + 1, :], start, 0.0), axis=0, keepdims=True)
        o_ref[k:k + 1, :] = base.astype(jnp.int32) + rank_ref[k:k + 1, :]


def _slots(idx, rank, seg_start, tm):
    T = idx.shape[1]
    E = seg_start.shape[0]
    tok = lambda i: (0, i)
    return pl.pallas_call(
        _slots_kernel,
        out_shape=jax.ShapeDtypeStruct((TOP_K, T), jnp.int32),
        grid=(T // tm,),
        in_specs=[pl.BlockSpec((TOP_K, tm), tok), pl.BlockSpec((TOP_K, tm), tok),
                  pl.BlockSpec((E, 1), lambda i: (0, 0))],
        out_specs=pl.BlockSpec((TOP_K, tm), tok),
        compiler_params=_cparams("parallel"),
        name="moe_slots",
    )(idx, rank, seg_start.astype(F32).reshape(E, 1))


SC_CORES = 2
SC_SUBCORES = 16
SC_CHUNK = 128


def _sc_rows_kernel(body, out_shape, dtype, scratch, name):
    mesh = plsc.VectorSubcoreMesh(core_axis_name="c", subcore_axis_name="s",
                                  num_cores=SC_CORES, num_subcores=SC_SUBCORES)
    return pl.kernel(body, mesh=mesh, out_type=jax.ShapeDtypeStruct(out_shape, dtype),
                     scratch_types=scratch, name=name)


def _sc_chunks(n):
    workers = SC_CORES * SC_SUBCORES
    chunks = n // (workers * SC_CHUNK)
    assert chunks * workers * SC_CHUNK == n
    return workers, chunks


SC_HALF = SC_CHUNK // 2


def _sc_gather_rows(table, idx):
    workers, chunks = _sc_chunks(idx.shape[0])
    width = table.shape[1]

    def body(table_hbm, idx_hbm, out_hbm, idx_v, rows_v, *sems):
        wid = lax.axis_index("s") * SC_CORES + lax.axis_index("c")
        pltpu.sync_copy(idx_hbm.at[wid], idx_v)

        @pl.loop(0, chunks)
        def _(j):
            pieces = [2 * j, 2 * j + 1]
            reads = [pltpu.async_copy(table_hbm.at[idx_v.at[p]], rows_v.at[b], sems[b])
                     for b, p in enumerate(pieces)]
            writes = []
            for b, p in enumerate(pieces):
                reads[b].wait()
                first = (wid * 2 * chunks + p) * SC_HALF
                writes.append(pltpu.async_copy(rows_v.at[b], out_hbm.at[pl.ds(first, SC_HALF)],
                                               sems[2 + b]))
            for write in writes:
                write.wait()

    scratch = [pltpu.VMEM((2 * chunks, SC_HALF), jnp.int32),
               pltpu.VMEM((2, SC_HALF, width), table.dtype)] + [pltpu.SemaphoreType.DMA] * 4
    call = _sc_rows_kernel(body, (idx.shape[0], width), table.dtype, scratch, "moe_gather_sc")
    return call(table, idx.reshape(workers, 2 * chunks, SC_HALF))


def _sc_scatter_rows(src, dest, n_rows):
    copies, n_src = dest.shape
    workers, chunks = _sc_chunks(n_src)

    def body(src_hbm, idx_hbm, out_hbm, idx_v, rows_v, sem):
        wid = lax.axis_index("s") * SC_CORES + lax.axis_index("c")
        pltpu.sync_copy(idx_hbm.at[wid], idx_v)

        @pl.loop(0, chunks)
        def _(j):
            pltpu.sync_copy(src_hbm.at[pl.ds((wid * chunks + j) * SC_CHUNK, SC_CHUNK)], rows_v)
            started = [pltpu.async_copy(rows_v, out_hbm.at[idx_v.at[j, k]], sem)
                       for k in range(copies)]
            for copy in started:
                copy.wait()

    scratch = [pltpu.VMEM((chunks, copies, SC_CHUNK), jnp.int32),
               pltpu.VMEM((SC_CHUNK, src.shape[1]), src.dtype), pltpu.SemaphoreType.DMA]
    call = _sc_rows_kernel(body, (n_rows, src.shape[1]), src.dtype, scratch, "moe_scatter_sc")
    idx = dest.reshape(copies, workers, chunks, SC_CHUNK).transpose(1, 2, 0, 3)
    return call(src, idx)


def _silu(g):
    return g * jax.nn.sigmoid(g)


SCHED_FIRST, SCHED_SLOT, SCHED_NEXT, SCHED_VALID = 0, 1, 2, 3


def _expert_kernel(sched_ref, n_used_ref, x_ref, wg_hbm, wu_hbm, wd_hbm, o_ref,
                   wg_buf, wu_buf, wd_buf, sem, *, layer, first_expert_row):
    i = pl.program_id(0)
    slot = sched_ref[SCHED_SLOT, i]

    def copies(expert, slot):
        pairs = ((wg_hbm, wg_buf), (wu_hbm, wu_buf), (wd_hbm, wd_buf))
        return [pltpu.make_async_copy(src.at[layer, expert], dst.at[slot], sem.at[slot, n])
                for n, (src, dst) in enumerate(pairs)]

    @pl.when(i == 0)
    def _():
        for c in copies(sched_ref[first_expert_row, 0], 0):
            c.start()

    @pl.when(sched_ref[SCHED_FIRST, i] == 1)
    def _():
        for c in copies(0, slot):
            c.wait()

        @pl.when(sched_ref[SCHED_NEXT, i] >= 0)
        def _():
            for c in copies(sched_ref[SCHED_NEXT, i], 1 - slot):
                c.start()

    @pl.when(i < n_used_ref[0])
    def _():
        words = x_ref[...]
        row = lax.broadcasted_iota(jnp.int32, words.shape, 0)
        words = jnp.where(row < sched_ref[SCHED_VALID, i], words, 0)
        x = jnp.concatenate(_unpack_halves(words), axis=1).astype(BF16)
        g = jnp.dot(x, wg_buf[slot].astype(BF16), preferred_element_type=F32)
        u = jnp.dot(x, wu_buf[slot].astype(BF16), preferred_element_type=F32)
        a = (_silu(g) * u).astype(BF16)
        o_ref[...] = _pack_halves(jnp.dot(a, wd_buf[slot].astype(BF16), preferred_element_type=F32))

    @pl.when(i >= n_used_ref[0])
    def _():
        o_ref[...] = jnp.zeros_like(o_ref)


def _expert_schedule(seg_start, seg_end, counts, n_blocks):
    E = seg_end.shape[0]
    first_row = jnp.arange(n_blocks, dtype=jnp.int32) * EXPERT_ROWS
    blk_e = jnp.minimum(jnp.sum(seg_end[None, :] <= first_row[:, None], axis=1), E - 1)
    used = first_row < seg_end[-1]
    first = used & (first_row == seg_start[blk_e])
    ordinal = jnp.cumsum(first) - 1
    ids = jnp.where(seg_end > seg_start, jnp.arange(E), E)
    later = lax.cummin(ids, reverse=True)
    nxt = jnp.concatenate([later[1:], jnp.full((1,), E, later.dtype)])[blk_e]
    nxt = jnp.where(nxt < E, nxt, -1)
    valid = jnp.clip((seg_start + counts)[blk_e] - first_row, 0, EXPERT_ROWS)
    return jnp.stack([first, ordinal % 2, nxt, valid, blk_e]).astype(jnp.int32)


def _expert_mlp(xs, sched, n_used, w_gate, w_up, w_down, layer):
    n_rows, half = xs.shape
    D = 2 * half
    F = w_gate.shape[-1]
    tm = EXPERT_ROWS
    hbm = pl.BlockSpec(memory_space=pl.ANY)
    return pl.pallas_call(
        functools.partial(_expert_kernel, layer=layer, first_expert_row=sched.shape[0] - 1),
        out_shape=jax.ShapeDtypeStruct((n_rows, half), jnp.int32),
        grid_spec=pltpu.PrefetchScalarGridSpec(
            num_scalar_prefetch=2,
            grid=(n_rows // tm,),
            in_specs=[pl.BlockSpec((tm, half), lambda i, sc, nu: (jnp.minimum(i, nu[0] - 1), 0)),
                      hbm, hbm, hbm],
            out_specs=pl.BlockSpec((tm, half), lambda i, sc, nu: (i, 0)),
            scratch_shapes=[pltpu.VMEM((2, D, F), F32), pltpu.VMEM((2, D, F), F32),
                            pltpu.VMEM((2, F, D), F32), pltpu.SemaphoreType.DMA((2, 3))]),
        compiler_params=_cparams("arbitrary"),
        name="moe_experts",
    )(sched, n_used, xs, w_gate, w_up, w_down)


def _combine_kernel(x_ref, gate_ref, rows_ref, wsg_ref, wsu_ref, wsd_ref, g_ref, b_ref, o_ref,
                    *, alpha):
    x = x_ref[...]
    xb = x.astype(BF16)
    g = jnp.dot(xb, wsg_ref[...], preferred_element_type=F32)
    u = jnp.dot(xb, wsu_ref[...], preferred_element_type=F32)
    y = jnp.dot((_silu(g) * u).astype(BF16), wsd_ref[...], preferred_element_type=F32)
    gates = gate_ref[...]
    half = y.shape[1] // 2
    left, right = y[:, :half], y[:, half:]
    for k in range(TOP_K):
        lo, hi = _unpack_halves(rows_ref[k])
        left = left + lo * gates[:, k:k + 1]
        right = right + hi * gates[:, k:k + 1]
    y = jnp.concatenate([left, right], axis=1)
    o_ref[...] = _layer_norm(alpha * x + y, g_ref[...], b_ref[...])


def _combine(x2d, ys, dest, gates_tok, ws_gate, ws_up, ws_down, g, b, alpha, tt):
    T, D = x2d.shape
    F = ws_gate.shape[1]
    row = lambda i: (i, 0)
    fixed = lambda i: (0, 0)
    rows = _sc_gather_rows(ys, dest.reshape(-1)).reshape(TOP_K, T, D // 2)
    return pl.pallas_call(
        functools.partial(_combine_kernel, alpha=alpha),
        out_shape=jax.ShapeDtypeStruct((T, D), F32),
        grid=(T // tt,),
        in_specs=[pl.BlockSpec((tt, D), row),
                  pl.BlockSpec((tt, LANES), row),
                  pl.BlockSpec((TOP_K, tt, D // 2), lambda i: (0, i, 0)),
                  pl.BlockSpec((D, F), fixed), pl.BlockSpec((D, F), fixed),
                  pl.BlockSpec((F, D), fixed),
                  pl.BlockSpec((1, D), fixed), pl.BlockSpec((1, D), fixed)],
        out_specs=pl.BlockSpec((tt, D), row),
        compiler_params=_cparams("parallel"),
        name="moe_combine",
    )(x2d, gates_tok, rows, ws_gate.astype(BF16), ws_up.astype(BF16), ws_down.astype(BF16),
      g.reshape(1, D), b.reshape(1, D))


def _pick(n, pref):
    t = min(n, pref)
    while n % t:
        t //= 2
    return t


def _moe_layer(x2d, packed, w_router, router_bias, w_gate, w_up, w_down, layer, ws_gate, ws_up,
               ws_down, g, b, alpha):
    T, D = x2d.shape
    E = w_router.shape[1]
    idx, gates, rank, counts = _router(x2d, w_router, router_bias, _pick(T, 512))
    counts = counts[:, 0].astype(jnp.int32)
    padded = (counts + EXPERT_ROWS - 1) // EXPERT_ROWS * EXPERT_ROWS
    seg_end = jnp.cumsum(padded)
    seg_start = seg_end - padded
    n_blocks = (T * TOP_K + E * (EXPERT_ROWS - 1)) // EXPERT_ROWS
    dest = _slots(idx, rank, seg_start, _pick(T, 1024))
    sched = _expert_schedule(seg_start, seg_end, counts, n_blocks)
    n_used = (seg_end[-1:] // EXPERT_ROWS).astype(jnp.int32)
    xs = _sc_scatter_rows(packed, dest, n_blocks * EXPERT_ROWS)
    ys = _expert_mlp(xs, sched, n_used, w_gate, w_up, w_down, layer)
    return _combine(x2d, ys, dest, gates, ws_gate, ws_up, ws_down, g, b, alpha, _pick(T, 256))


def _mixer_layer(x2d, i, B, S, p, alpha):
    T, D = x2d.shape
    tm = _pick(T, 512)
    tn = _pick(D, 1024)
    j = i // 2
    w_in = p["moba_w_in"][j] if i % 2 == 0 else p["fox_w_in"][j]
    w_qv_t = jnp.concatenate([w_in[:, :D].T * (HEAD_DIM ** -0.5 * LOG2E),
                              w_in[:, 2 * D:3 * D].T]).astype(BF16)
    qv_t = _project_t(x2d, w_qv_t, tm, tn)
    w_k = w_in[:, D:2 * D].astype(BF16)
    if i % 2 == 0:
        k_aug, k_mean = _moba_keys(x2d, w_k, B, S, tm)
        attn_t = _moba_attention(qv_t, k_aug, k_mean, p["rel_bias"], B, S)
        w_out = p["moba_w_out"][j]
    else:
        c_tok = _fox_gates(x2d.reshape(B, S, D), w_in[:, 3 * D:], p["fox_b_f"][j], _pick(S, 512))
        k_aug = _fox_keys(x2d, w_k, c_tok.reshape(T, LANES), tm).reshape(HEADS_PER_TILE, B, S, D)
        attn_t = _fox_attention(qv_t, k_aug, B, S, _pick(S, 256))
        w_out = p["fox_w_out"][j]
    return _outproj_ln(attn_t, w_out.astype(BF16), x2d, p["ln1_g"][i], p["ln1_b"][i], alpha, tm)


def kernel(x, rel_bias, moba_w_in, moba_w_out, fox_w_in, fox_b_f, fox_w_out, ln1_g, ln1_b, ln2_g,
           ln2_b, w_router, router_bias, w_gate, w_up, w_down, ws_gate, ws_up, ws_down):
    B, S, D = x.shape
    depth = ln1_g.shape[0]
    alpha = (2 * depth) ** 0.25
    p = dict(rel_bias=rel_bias, moba_w_in=moba_w_in, moba_w_out=moba_w_out, fox_w_in=fox_w_in,
             fox_b_f=fox_b_f, fox_w_out=fox_w_out, ln1_g=ln1_g, ln1_b=ln1_b)
    x2d = x.reshape(B * S, D)
    for i in range(depth):
        x2d, packed = _mixer_layer(x2d, i, B, S, p, alpha)
        x2d = _moe_layer(x2d, packed, w_router[i], router_bias[i], w_gate, w_up, w_down, i,
                         ws_gate[i], ws_up[i], ws_down[i], ln2_g[i], ln2_b[i], alpha)
    return x2d.reshape(B, S, D)
```

```python
import functools
import math

import jax
import jax.numpy as jnp
from jax import lax
from jax.experimental import pallas as pl
from jax.experimental.pallas import tpu as pltpu
from jax.experimental.pallas import tpu_sc as plsc

F32 = jnp.float32
BF16 = jnp.bfloat16
HIGHEST = lax.Precision.HIGHEST

HEAD_DIM = 64
MOBA_BLOCK = 256
MOBA_TOPK = 3
REL_MAX_DIST = 128
TOP_K = 8
N_GROUPS = 8
TOPK_GROUPS = 4
ROUTED_SCALE = 2.5
LN_EPS = 1e-5
NEG = -1e30
LOG2E = math.log2(math.e)

LANES = 128
HEADS_PER_TILE = LANES // HEAD_DIM
EXPERT_ROWS = 512
VMEM_LIMIT = 48 * 1024 * 1024


def _cparams(*sem):
    return pltpu.CompilerParams(dimension_semantics=sem, vmem_limit_bytes=VMEM_LIMIT)


def _nt_dot(a, b, **kw):
    return lax.dot_general(a, b, (((1,), (1,)), ((), ())), preferred_element_type=F32, **kw)


def _split_bf16(x, pieces):
    out = []
    for _ in range(pieces):
        out.append(x.astype(BF16))
        x = x - out[-1].astype(F32)
    return out


def _three_pass(x, w_ref, mul):
    hi, lo = _split_bf16(x, 2)
    return mul(hi, w_ref[0]) + mul(lo, w_ref[0]) + mul(hi, w_ref[1])


def _proj_t_kernel(x_ref, wt_ref, o_ref):
    o_ref[...] = _nt_dot(wt_ref[...], x_ref[...].astype(BF16)).astype(o_ref.dtype)


def _project_t(x2d, wt, tm, tn):
    T, K = x2d.shape
    N = wt.shape[0]
    return pl.pallas_call(
        _proj_t_kernel,
        out_shape=jax.ShapeDtypeStruct((N, T), BF16),
        grid=(T // tm, N // tn),
        in_specs=[pl.BlockSpec((tm, K), lambda i, j: (i, 0)),
                  pl.BlockSpec((tn, K), lambda i, j: (j, 0))],
        out_specs=pl.BlockSpec((tn, tm), lambda i, j: (j, i)),
        compiler_params=_cparams("parallel", "arbitrary"),
        name="qv_proj_t",
    )(x2d, wt)


def _layer_norm(r, g, b):
    mu = jnp.mean(r, axis=-1, keepdims=True)
    d = r - mu
    var = jnp.mean(d * d, axis=-1, keepdims=True)
    return d * lax.rsqrt(var + LN_EPS) * g + b


def _pack_halves(y):
    n = y.shape[1] // 2
    bits = lambda v: lax.bitcast_convert_type(v.astype(BF16).astype(F32), jnp.int32)
    left = bits(y[:, :n])
    return lax.shift_right_logical(left, jnp.full_like(left, 16)) | bits(y[:, n:])


def _unpack_halves(p):
    return (lax.bitcast_convert_type(p << 16, F32),
            lax.bitcast_convert_type(p & jnp.int32(-65536), F32))


def _outproj_ln_kernel(at_ref, w_ref, x_ref, g_ref, b_ref, o_ref, packed_ref, *, alpha):
    y = lax.dot_general(at_ref[...], w_ref[...], (((0,), (0,)), ((), ())),
                        preferred_element_type=F32)
    out = _layer_norm(alpha * x_ref[...] + y, g_ref[...], b_ref[...])
    o_ref[...] = out
    packed_ref[...] = _pack_halves(out)


def _outproj_ln(attn_t, w_out, x2d, g, b, alpha, tm):
    T, D = x2d.shape
    row = lambda i: (i, 0)
    fixed = lambda i: (0, 0)
    return pl.pallas_call(
        functools.partial(_outproj_ln_kernel, alpha=alpha),
        out_shape=(jax.ShapeDtypeStruct((T, D), F32), jax.ShapeDtypeStruct((T, D // 2), jnp.int32)),
        grid=(T // tm,),
        in_specs=[pl.BlockSpec((D, tm), lambda i: (0, i)), pl.BlockSpec((D, D), fixed),
                  pl.BlockSpec((tm, D), row), pl.BlockSpec((1, D), fixed),
                  pl.BlockSpec((1, D), fixed)],
        out_specs=(pl.BlockSpec((tm, D), row), pl.BlockSpec((tm, D // 2), row)),
        compiler_params=_cparams("parallel"),
        name="outproj_ln",
    )(attn_t, w_out, x2d, g.reshape(1, D), b.reshape(1, D))


def _spare(a, n=0):
    return ((a + 1) % HEADS_PER_TILE) * HEAD_DIM + n


def _scores(chains):
    return [[jnp.dot(keys(), query(), preferred_element_type=F32) for keys, query, _ in blocks]
            for blocks in chains]


def _absorb(carries, scores, chains, masks=None, adds=None):
    probs = []
    for c, ((m, _), parts) in enumerate(zip(carries, scores)):
        if adds is not None:
            parts = [s if add is None else s + add() for add, s in zip(adds[c], parts)]
        if masks is not None:
            parts = [s if keep is None else jnp.where(keep, s, NEG)
                     for keep, s in zip(masks[c], parts)]
        tops = [jnp.max(s, axis=0, keepdims=True) for s in parts]
        m_new = functools.reduce(jnp.maximum, tops, m)
        probs.append((m_new, jnp.exp2(m - m_new),
                      [jnp.exp2(s - m_new).astype(BF16) for s in parts]))
    out = []
    for (_, acc), (m_new, decay, ps), blocks in zip(carries, probs, chains):
        acc = decay * acc
        for (_, _, values), p in zip(blocks, ps):
            acc = acc + jnp.dot(values(), p, preferred_element_type=F32)
        out.append((m_new, acc))
    return tuple(out)


def _stash(chains, ref):
    for a, parts in enumerate(_scores(chains)):
        for b, s in enumerate(parts):
            ref[a, b] = s


def _fetch(chains, ref):
    return [[ref[a, b] for b in range(len(blocks))] for a, blocks in enumerate(chains)]


def _attend_masked_last(carries, n, group, stage_refs, masks, adds=None):
    _stash(group(0), stage_refs[0])

    def run(carries, start, count, last_is_masked):
        for q in range(count):
            g = start + q
            if q + 1 < count or not last_is_masked:
                _stash(group(g + 1), stage_refs[(q + 1) % 2])
            special = last_is_masked and q + 1 == count
            carries = _absorb(carries, _fetch(group(g), stage_refs[q % 2]), group(g),
                              masks if special else None, adds if special else None)
        return carries

    trips = n // TRIP_GROUPS
    carries = lax.fori_loop(0, trips, lambda t, c: run(c, TRIP_GROUPS * t, TRIP_GROUPS, False),
                            carries)
    tails = [functools.partial(run, start=n - r, count=r + 1, last_is_masked=True)
             for r in range(TRIP_GROUPS)]
    return lax.switch(n % TRIP_GROUPS, tails, carries)


GROUP_BLOCKS = 2
TRIP_GROUPS = 4


def _stage_scratch(tq, chains=HEADS_PER_TILE):
    return [pltpu.VMEM((chains, GROUP_BLOCKS, tq, tq), F32) for _ in range(2)]


def _scores_init(tq):
    return jnp.full((1, tq), -jnp.inf, F32), jnp.zeros((LANES, tq), F32)


def _finish(carries, o_ref, cols=slice(None)):
    row = lax.broadcasted_iota(jnp.int32, carries[0][1].shape, 0)
    outs = []
    for a, (_, acc) in enumerate(carries):
        s = _spare(a)
        outs.append(acc / acc[s:s + 1, :])
    o_ref[:, cols] = jnp.where(row < HEAD_DIM, outs[0], outs[1]).astype(o_ref.dtype)


def _fill_v_aug(vt_ref, vaug_ref, tk):
    n_tiles = vt_ref.shape[1] // tk
    row = lax.broadcasted_iota(jnp.int32, (LANES, tk), 0)
    for n in range(n_tiles):
        v = vt_ref[:, n * tk:(n + 1) * tk]
        for a in range(HEADS_PER_TILE):
            vaug_ref[a, n] = jnp.where(row // HEAD_DIM == a, v, jnp.ones_like(v))


FOX_SPLIT = 3


def _fox_kernel(qt_ref, k_ref, vt_ref, o_ref, vaug_ref, stage0_ref, stage1_ref, *, tq):
    n = pl.program_id(2)

    @pl.when(n == 0)
    def _():
        _fill_v_aug(vt_ref, vaug_ref, tq)

    chains = [(h, a) for h in range(GROUP_BLOCKS) for a in range(HEADS_PER_TILE)]
    row = lax.broadcasted_iota(jnp.int32, (LANES, tq), 0)
    queries = {}
    for h, a in chains:
        offs = (row >= _spare(a)) & (row < _spare(a, FOX_SPLIT))
        queries[h, a] = jnp.where(row // HEAD_DIM == a, qt_ref[:, h * tq:(h + 1) * tq],
                                  jnp.where(offs, -1.0, 0.0).astype(BF16))

    def block(h, a, j):
        start = pl.multiple_of(j * tq, tq)
        return (lambda: k_ref[a, 0, pl.ds(start, tq), :], lambda: queries[h, a],
                lambda: vaug_ref[a, j])

    key = lax.broadcasted_iota(jnp.int32, (tq, tq), 0)
    qry = lax.broadcasted_iota(jnp.int32, (tq, tq), 1)
    masks = [[None if b < h else key + (b - h) * tq <= qry for b in range(GROUP_BLOCKS)]
             for h, _ in chains]
    carries = _attend_masked_last(
        tuple(_scores_init(tq) for _ in chains), n,
        lambda j: [[block(h, a, 2 * j), block(h, a, 2 * j + 1)] for h, a in chains],
        (stage0_ref, stage1_ref), masks)
    for h in range(GROUP_BLOCKS):
        _finish(carries[h * HEADS_PER_TILE:(h + 1) * HEADS_PER_TILE], o_ref,
                slice(h * tq, (h + 1) * tq))


def _fox_attention(qv_t, k_aug, B, S, tq):
    D = k_aug.shape[-1]
    n_tiles = D // LANES
    nq = S // tq
    span = GROUP_BLOCKS * tq
    steps = S // span
    return pl.pallas_call(
        functools.partial(_fox_kernel, tq=tq),
        out_shape=jax.ShapeDtypeStruct((D, B * S), BF16),
        grid=(B, n_tiles, steps),
        in_specs=[pl.BlockSpec((LANES, span), lambda b, h, i: (h, b * steps + i)),
                  pl.BlockSpec((HEADS_PER_TILE, 1, S, LANES), lambda b, h, i: (0, b, 0, h)),
                  pl.BlockSpec((LANES, S), lambda b, h, i: (n_tiles + h, b))],
        out_specs=pl.BlockSpec((LANES, span), lambda b, h, i: (h, b * steps + i)),
        scratch_shapes=[pltpu.VMEM((HEADS_PER_TILE, nq, LANES, tq), BF16)]
        + _stage_scratch(tq, GROUP_BLOCKS * HEADS_PER_TILE),
        compiler_params=_cparams("parallel", "parallel", "arbitrary"),
        name="fox_attention",
    )(qv_t, k_aug, qv_t)


def _log_sigmoid(z):
    return jnp.minimum(z, 0.0) - jnp.log1p(jnp.exp(-jnp.abs(z)))


def _fox_gates_kernel(x_ref, w_ref, b_ref, o_ref, carry_ref, *, ts):
    @pl.when(pl.program_id(1) == 0)
    def _():
        carry_ref[...] = jnp.zeros_like(carry_ref)

    z = _three_pass(x_ref[0], w_ref, lambda x, w: jnp.dot(x, w, preferred_element_type=F32))
    lf = _log_sigmoid(z + b_ref[...])
    r = lax.broadcasted_iota(jnp.int32, (ts, ts), 0)
    c = lax.broadcasted_iota(jnp.int32, (ts, ts), 1)
    tri = (c <= r).astype(BF16)
    cum = carry_ref[...]
    for piece in _split_bf16(lf, 3):
        cum = cum + jnp.dot(tri, piece, preferred_element_type=F32)
    carry_ref[...] = cum[ts - 1:ts, :]
    o_ref[0] = cum


def _fox_gates(x3d, w_f, b_f, ts):
    B, S, D = x3d.shape
    H = w_f.shape[1]
    w_pad = jnp.stack(_split_bf16(jnp.zeros((D, LANES), F32).at[:, :H].set(w_f), 2))
    b_pad = jnp.zeros((1, LANES), F32).at[0, :H].set(b_f)
    return pl.pallas_call(
        functools.partial(_fox_gates_kernel, ts=ts),
        out_shape=jax.ShapeDtypeStruct((B, S, LANES), F32),
        grid=(B, S // ts),
        in_specs=[pl.BlockSpec((1, ts, D), lambda b, s: (b, s, 0)),
                  pl.BlockSpec((2, D, LANES), lambda b, s: (0, 0, 0)),
                  pl.BlockSpec((1, LANES), lambda b, s: (0, 0))],
        out_specs=pl.BlockSpec((1, ts, LANES), lambda b, s: (b, s, 0)),
        scratch_shapes=[pltpu.VMEM((1, LANES), F32)],
        compiler_params=_cparams("parallel", "arbitrary"),
        name="fox_gates",
    )(x3d, w_pad, b_pad)


def _fox_keys_kernel(x_ref, w_ref, c_ref, o_ref):
    keys = jnp.dot(x_ref[...].astype(BF16), w_ref[...], preferred_element_type=F32)
    tm, D = keys.shape
    lane = lax.broadcasted_iota(jnp.int32, (tm, LANES), 1)
    for t in range(D // LANES):
        k = keys[:, t * LANES:(t + 1) * LANES]
        for a in range(HEADS_PER_TILE):
            h = t * HEADS_PER_TILE + a
            rest = c_ref[:, h:h + 1] * LOG2E
            aug = jnp.zeros((tm, LANES), F32)
            for n in range(FOX_SPLIT):
                piece = rest.astype(BF16).astype(F32)
                rest = rest - piece
                aug = jnp.where(lane == _spare(a, n), piece, aug)
            o_ref[a, :, t * LANES:(t + 1) * LANES] = jnp.where(lane // HEAD_DIM == a, k,
                                                               aug).astype(BF16)


def _fox_keys(x2d, w_k, c_tok, tm):
    T, K = x2d.shape
    D = w_k.shape[1]
    return pl.pallas_call(
        _fox_keys_kernel,
        out_shape=jax.ShapeDtypeStruct((HEADS_PER_TILE, T, D), BF16),
        grid=(T // tm,),
        in_specs=[pl.BlockSpec((tm, K), lambda i: (i, 0)),
                  pl.BlockSpec((K, D), lambda i: (0, 0)),
                  pl.BlockSpec((tm, LANES), lambda i: (i, 0))],
        out_specs=pl.BlockSpec((HEADS_PER_TILE, tm, D), lambda i: (0, i, 0)),
        compiler_params=_cparams("parallel"),
        name="fox_keys",
    )(x2d, w_k, c_tok)


MOBA_SLAB = 16
KMEAN_SPLIT = 3


def _moba_kernel(bfar_ref, qt_ref, k_ref, vt_ref, kmean_ref, bown_ref, badj_ref, o_ref,
                 vaug_ref, sel_ref, stage0_ref, stage1_ref, *, blk):
    t = pl.program_id(1)
    n = pl.program_id(2)
    nb = kmean_ref.shape[2]

    @pl.when(n == 0)
    def _():
        _fill_v_aug(vt_ref, vaug_ref, blk)

    chains = [(h, a) for h in range(GROUP_BLOCKS) for a in range(HEADS_PER_TILE)]
    row = lax.broadcasted_iota(jnp.int32, (LANES, blk), 0)
    brow = lax.broadcasted_iota(jnp.int32, (nb, blk), 0)
    base = {}
    for c, (h, a) in enumerate(chains):
        own = GROUP_BLOCKS * n + h
        qt = qt_ref[:, h * blk:(h + 1) * blk]
        qa = jnp.where(row // HEAD_DIM == a, qt, jnp.zeros_like(qt))
        base[h, a] = qa
        bscore = sum(jnp.dot(kmean_ref[piece, 0], qa, preferred_element_type=F32)
                     for piece in range(KMEAN_SPLIT))
        cand = jnp.where(brow < own, bscore, NEG)
        sel = jnp.zeros((nb, blk), F32)
        for _ in range(MOBA_TOPK):
            mx = jnp.max(cand, axis=0, keepdims=True)
            first = jnp.min(jnp.where(cand == mx, brow, nb), axis=0, keepdims=True)
            hit = brow == first
            sel = jnp.where(hit, 1.0, sel)
            cand = jnp.where(hit, -jnp.inf, cand)
        sel_ref[c] = jnp.where(brow < own, sel, 0.0)

    srow = lax.broadcasted_iota(jnp.int32, (MOBA_SLAB, blk), 0)

    def block(c, j):
        h, a = chains[c]
        head_id = t * HEADS_PER_TILE + a
        jc = jnp.maximum(j, 0)
        start = pl.multiple_of(jc * blk, blk)

        def query():
            s0 = _spare(a) // MOBA_SLAB * MOBA_SLAB
            chosen = (((sel_ref[c, pl.ds(jc, 1), :] > 0.0) & (j >= 0))
                      | (j == GROUP_BLOCKS * n + h))
            slab = jnp.where(srow == _spare(a) - s0,
                             jnp.where(chosen, bfar_ref[0, head_id], NEG),
                             jnp.where(srow == _spare(a, 1) - s0, bfar_ref[1, head_id], 0.0)
                             ).astype(BF16)
            q = base[h, a]
            head = [q[:s0]] if s0 else []
            tail = [q[s0 + MOBA_SLAB:]] if s0 + MOBA_SLAB < LANES else []
            return jnp.concatenate(head + [slab] + tail, axis=0)

        return lambda: k_ref[a, 0, pl.ds(start, blk), :], query, lambda: vaug_ref[a, jc]

    key = lax.broadcasted_iota(jnp.int32, (blk, blk), 0)
    qry = lax.broadcasted_iota(jnp.int32, (blk, blk), 1)
    carries = _attend_masked_last(
        tuple(_scores_init(blk) for _ in chains), n,
        lambda g: [[block(c, 2 * g + h - 1), block(c, 2 * g + h)] for c, (h, _) in enumerate(chains)],
        (stage0_ref, stage1_ref), [[None, key <= qry] for _ in chains],
        [[lambda a=a: badj_ref[a], lambda a=a: bown_ref[a]] for _, a in chains])
    for h in range(GROUP_BLOCKS):
        _finish(carries[h * HEADS_PER_TILE:(h + 1) * HEADS_PER_TILE], o_ref,
                slice(h * blk, (h + 1) * blk))


def _rel_bucket(dist, n_buckets):
    n = jnp.maximum(dist, 0)
    max_exact = n_buckets // 2
    nf = jnp.maximum(n, 1).astype(F32)
    large = max_exact + (jnp.log(nf / max_exact) / math.log(REL_MAX_DIST / max_exact)
                         * (n_buckets - max_exact)).astype(jnp.int32)
    large = jnp.minimum(large, n_buckets - 1)
    return jnp.where(n < max_exact, n, large)


def _moba_attention(qv_t, k_aug, k_mean, rel_bias, B, S):
    D = k_aug.shape[-1]
    n_tiles = D // LANES
    blk = MOBA_BLOCK
    nb = S // blk
    hp = HEADS_PER_TILE
    span = GROUP_BLOCKS * blk
    steps = S // span
    n_chains = GROUP_BLOCKS * hp
    r = jnp.arange(blk)
    delta = r[None, :] - r[:, None]
    bias_t = rel_bias.T.astype(F32)
    n_buckets = rel_bias.shape[0]
    def table(dist):
        onehot = jax.nn.one_hot(_rel_bucket(dist, n_buckets), n_buckets, dtype=F32)
        return jnp.einsum('crn,hn->hcr', onehot, bias_t, precision=HIGHEST)

    b_far = bias_t[:, n_buckets - 1] * LOG2E
    b_own = table(delta) * LOG2E - b_far[:, None, None]
    b_adj = table(delta + blk) * LOG2E - b_far[:, None, None]
    far_hi = b_far.astype(BF16).astype(F32)
    far_lo = (b_far - far_hi).astype(BF16).astype(F32)
    b_far2 = jnp.stack([far_hi, far_lo])
    return pl.pallas_call(
        functools.partial(_moba_kernel, blk=blk),
        out_shape=jax.ShapeDtypeStruct((D, B * S), BF16),
        grid_spec=pltpu.PrefetchScalarGridSpec(
            num_scalar_prefetch=1,
            grid=(B, n_tiles, steps),
            in_specs=[pl.BlockSpec((LANES, span), lambda b, h, i, f: (h, b * steps + i)),
                      pl.BlockSpec((hp, 1, S, LANES), lambda b, h, i, f: (0, b, 0, h)),
                      pl.BlockSpec((LANES, S), lambda b, h, i, f: (n_tiles + h, b)),
                      pl.BlockSpec((KMEAN_SPLIT, 1, nb, LANES), lambda b, h, i, f: (0, b, 0, h)),
                      pl.BlockSpec((hp, blk, blk), lambda b, h, i, f: (h, 0, 0)),
                      pl.BlockSpec((hp, blk, blk), lambda b, h, i, f: (h, 0, 0))],
            out_specs=pl.BlockSpec((LANES, span), lambda b, h, i, f: (h, b * steps + i)),
            scratch_shapes=[pltpu.VMEM((hp, nb, LANES, blk), BF16),
                            pltpu.VMEM((n_chains, nb, blk), F32)] + _stage_scratch(blk, n_chains)),
        compiler_params=_cparams("parallel", "parallel", "arbitrary"),
        name="moba_attention",
    )(b_far2, qv_t, k_aug, qv_t, k_mean, b_own, b_adj)


def _moba_keys_kernel(x_ref, w_ref, o_ref, mean_ref):
    k = jnp.dot(x_ref[...].astype(BF16), w_ref[...], preferred_element_type=F32)
    col = lax.broadcasted_iota(jnp.int32, k.shape, 1)
    lane = col % LANES
    for a in range(HEADS_PER_TILE):
        ones = (lane == _spare(a)) | (lane == _spare(a, 1))
        o_ref[a] = jnp.where((col // HEAD_DIM) % HEADS_PER_TILE == a, k,
                             jnp.where(ones, 1.0, 0.0)).astype(BF16)
    for n in range(mean_ref.shape[0]):
        mean_ref[n] = jnp.mean(k[n * MOBA_BLOCK:(n + 1) * MOBA_BLOCK], axis=0, keepdims=True)


def _moba_keys(x2d, w_k, B, S, tm):
    T, K = x2d.shape
    D = w_k.shape[1]
    nb = S // MOBA_BLOCK
    per_tile = tm // MOBA_BLOCK
    k_aug, k_mean = pl.pallas_call(
        _moba_keys_kernel,
        out_shape=(jax.ShapeDtypeStruct((HEADS_PER_TILE, T, D), BF16),
                   jax.ShapeDtypeStruct((B * nb, 1, D), F32)),
        grid=(T // tm,),
        in_specs=[pl.BlockSpec((tm, K), lambda i: (i, 0)),
                  pl.BlockSpec((K, D), lambda i: (0, 0))],
        out_specs=(pl.BlockSpec((HEADS_PER_TILE, tm, D), lambda i: (0, i, 0)),
                   pl.BlockSpec((per_tile, 1, D), lambda i: (i, 0, 0))),
        compiler_params=_cparams("parallel"),
        name="moba_keys",
    )(x2d, w_k)
    k_mean = jnp.stack(_split_bf16(k_mean.reshape(B, nb, D), KMEAN_SPLIT))
    return k_aug.reshape(HEADS_PER_TILE, B, S, D), k_mean


def _router_kernel(x_ref, wt_ref, bias_ref, idx_ref, gate_ref, rank_ref, cnt_ref, run_ref, *, tm):
    @pl.when(pl.program_id(0) == 0)
    def _():
        run_ref[...] = jnp.zeros_like(run_ref)

    E = wt_ref.shape[1]
    gsz = E // N_GROUPS
    logits = _three_pass(x_ref[...], wt_ref, lambda x, w: _nt_dot(w, x))
    s = jax.nn.sigmoid(logits)
    sb = s + bias_ref[...]
    neg_inf = -jnp.inf

    giota = lax.broadcasted_iota(jnp.int32, (gsz, tm), 0)
    gscore = []
    for g in range(N_GROUPS):
        blk = sb[g * gsz:(g + 1) * gsz, :]
        m1 = jnp.max(blk, axis=0, keepdims=True)
        i1 = jnp.min(jnp.where(blk == m1, giota, gsz), axis=0, keepdims=True)
        m2 = jnp.max(jnp.where(giota == i1, neg_inf, blk), axis=0, keepdims=True)
        gscore.append(m1 + m2)

    gsel = [jnp.zeros((1, tm), jnp.bool_) for _ in range(N_GROUPS)]
    for _ in range(TOPK_GROUPS):
        mx = functools.reduce(jnp.maximum, gscore)
        found = jnp.zeros((1, tm), jnp.bool_)
        for g in range(N_GROUPS):
            hit = (gscore[g] == mx) & jnp.logical_not(found)
            gsel[g] = gsel[g] | hit
            found = found | hit
            gscore[g] = jnp.where(hit, neg_inf, gscore[g])
    emask = jnp.concatenate([jnp.broadcast_to(gsel[g], (gsz, tm)) for g in range(N_GROUPS)], axis=0)
    cand = jnp.where(emask, sb, NEG)

    eiota = lax.broadcasted_iota(jnp.int32, (E, tm), 0)
    hits, idxs, ws = [], [], []
    for _ in range(TOP_K):
        mx = jnp.max(cand, axis=0, keepdims=True)
        first = jnp.min(jnp.where(cand == mx, eiota, E), axis=0, keepdims=True)
        hit = eiota == first
        hits.append(hit)
        idxs.append(first)
        ws.append(jnp.sum(jnp.where(hit, s, 0.0), axis=0, keepdims=True))
        cand = jnp.where(hit, neg_inf, cand)
    wsum = functools.reduce(jnp.add, ws)

    chosen = functools.reduce(jnp.logical_or, hits)
    onehot = jnp.where(chosen, 1.0, 0.0)
    tr = lax.broadcasted_iota(jnp.int32, (tm, tm), 0)
    tc = lax.broadcasted_iota(jnp.int32, (tm, tm), 1)
    before = (tr < tc).astype(BF16)
    prior = jnp.dot(onehot.astype(BF16), before, preferred_element_type=F32) + run_ref[...]
    grow = lax.broadcasted_iota(jnp.int32, (LANES, tm), 0)
    gates = jnp.zeros((LANES, tm), F32)
    for k in range(TOP_K):
        idx_ref[k:k + 1, :] = idxs[k]
        gates = jnp.where(grow == k, ws[k] / wsum * ROUTED_SCALE, gates)
        rank_ref[k:k + 1, :] = jnp.sum(jnp.where(hits[k], prior, 0.0), axis=0,
                                       keepdims=True).astype(jnp.int32)
    gate_ref[...] = gates.T
    run_ref[...] = run_ref[...] + jnp.sum(onehot, axis=1, keepdims=True)
    cnt_ref[...] = run_ref[...]


def _router(x2d, w_router, router_bias, tm):
    T, D = x2d.shape
    E = w_router.shape[1]
    tok = lambda i: (0, i)
    fixed = lambda i: (0, 0)
    return pl.pallas_call(
        functools.partial(_router_kernel, tm=tm),
        out_shape=(jax.ShapeDtypeStruct((TOP_K, T), jnp.int32),
                   jax.ShapeDtypeStruct((T, LANES), F32),
                   jax.ShapeDtypeStruct((TOP_K, T), jnp.int32),
                   jax.ShapeDtypeStruct((E, 1), F32)),
        grid=(T // tm,),
        in_specs=[pl.BlockSpec((tm, D), lambda i: (i, 0)),
                  pl.BlockSpec((2, E, D), lambda i: (0, 0, 0)),
                  pl.BlockSpec((E, 1), fixed)],
        out_specs=(pl.BlockSpec((TOP_K, tm), tok), pl.BlockSpec((tm, LANES), lambda i: (i, 0)),
                   pl.BlockSpec((TOP_K, tm), tok), pl.BlockSpec((E, 1), fixed)),
        scratch_shapes=[pltpu.VMEM((E, 1), F32)],
        compiler_params=_cparams("arbitrary"),
        name="router",
    )(x2d, jnp.stack(_split_bf16(w_router.T, 2)), router_bias.reshape(E, 1).astype(F32))


def _slots_kernel(idx_ref, rank_ref, start_ref, o_ref):
    E = start_ref.shape[0]
    tm = idx_ref.shape[1]
    eiota = lax.broadcasted_iota(jnp.int32, (E, tm), 0)
    start = start_ref[...]
    for k in range(TOP_K):
        base = jnp.sum(jnp.where(eiota == idx_ref[k:k + 1, :], start, 0.0), axis=0, keepdims=True)
        o_ref[k:k + 1, :] = base.astype(jnp.int32) + rank_ref[k:k + 1, :]


def _slots(idx, rank, seg_start, tm):
    T = idx.shape[1]
    E = seg_start.shape[0]
    tok = lambda i: (0, i)
    return pl.pallas_call(
        _slots_kernel,
        out_shape=jax.ShapeDtypeStruct((TOP_K, T), jnp.int32),
        grid=(T // tm,),
        in_specs=[pl.BlockSpec((TOP_K, tm), tok), pl.BlockSpec((TOP_K, tm), tok),
                  pl.BlockSpec((E, 1), lambda i: (0, 0))],
        out_specs=pl.BlockSpec((TOP_K, tm), tok),
        compiler_params=_cparams("parallel"),
        name="moe_slots",
    )(idx, rank, seg_start.astype(F32).reshape(E, 1))


SC_CORES = 2
SC_SUBCORES = 16
SC_CHUNK = 128


def _sc_rows_kernel(body, out_shape, dtype, scratch, name):
    mesh = plsc.VectorSubcoreMesh(core_axis_name="c", subcore_axis_name="s",
                                  num_cores=SC_CORES, num_subcores=SC_SUBCORES)
    return pl.kernel(body, mesh=mesh, out_type=jax.ShapeDtypeStruct(out_shape, dtype),
                     scratch_types=scratch, name=name)


def _sc_chunks(n):
    workers = SC_CORES * SC_SUBCORES
    chunks = n // (workers * SC_CHUNK)
    assert chunks * workers * SC_CHUNK == n
    return workers, chunks


SC_HALF = SC_CHUNK // 2


def _sc_gather_rows(table, idx):
    workers, chunks = _sc_chunks(idx.shape[0])
    width = table.shape[1]

    def body(table_hbm, idx_hbm, out_hbm, idx_v, rows_v, *sems):
        wid = lax.axis_index("s") * SC_CORES + lax.axis_index("c")
        pltpu.sync_copy(idx_hbm.at[wid], idx_v)

        @pl.loop(0, chunks)
        def _(j):
            pieces = [2 * j, 2 * j + 1]
            reads = [pltpu.async_copy(table_hbm.at[idx_v.at[p]], rows_v.at[b], sems[b])
                     for b, p in enumerate(pieces)]
            writes = []
            for b, p in enumerate(pieces):
                reads[b].wait()
                first = (wid * 2 * chunks + p) * SC_HALF
                writes.append(pltpu.async_copy(rows_v.at[b], out_hbm.at[pl.ds(first, SC_HALF)],
                                               sems[2 + b]))
            for write in writes:
                write.wait()

    scratch = [pltpu.VMEM((2 * chunks, SC_HALF), jnp.int32),
               pltpu.VMEM((2, SC_HALF, width), table.dtype)] + [pltpu.SemaphoreType.DMA] * 4
    call = _sc_rows_kernel(body, (idx.shape[0], width), table.dtype, scratch, "moe_gather_sc")
    return call(table, idx.reshape(workers, 2 * chunks, SC_HALF))


def _sc_scatter_rows(src, dest, n_rows):
    copies, n_src = dest.shape
    workers, chunks = _sc_chunks(n_src)

    def body(src_hbm, idx_hbm, out_hbm, idx_v, rows_v, sem):
        wid = lax.axis_index("s") * SC_CORES + lax.axis_index("c")
        pltpu.sync_copy(idx_hbm.at[wid], idx_v)

        @pl.loop(0, chunks)
        def _(j):
            pltpu.sync_copy(src_hbm.at[pl.ds((wid * chunks + j) * SC_CHUNK, SC_CHUNK)], rows_v)
            started = [pltpu.async_copy(rows_v, out_hbm.at[idx_v.at[j, k]], sem)
                       for k in range(copies)]
            for copy in started:
                copy.wait()

    scratch = [pltpu.VMEM((chunks, copies, SC_CHUNK), jnp.int32),
               pltpu.VMEM((SC_CHUNK, src.shape[1]), src.dtype), pltpu.SemaphoreType.DMA]
    call = _sc_rows_kernel(body, (n_rows, src.shape[1]), src.dtype, scratch, "moe_scatter_sc")
    idx = dest.reshape(copies, workers, chunks, SC_CHUNK).transpose(1, 2, 0, 3)
    return call(src, idx)


def _silu(g):
    return g * jax.nn.sigmoid(g)


SCHED_FIRST, SCHED_SLOT, SCHED_NEXT, SCHED_VALID = 0, 1, 2, 3


def _expert_kernel(sched_ref, n_used_ref, x_ref, wg_hbm, wu_hbm, wd_hbm, o_ref,
                   wg_buf, wu_buf, wd_buf, sem, *, layer, first_expert_row):
    i = pl.program_id(0)
    slot = sched_ref[SCHED_SLOT, i]

    def copies(expert, slot):
        pairs = ((wg_hbm, wg_buf), (wu_hbm, wu_buf), (wd_hbm, wd_buf))
        return [pltpu.make_async_copy(src.at[layer, expert], dst.at[slot], sem.at[slot, n])
                for n, (src, dst) in enumerate(pairs)]

    @pl.when(i == 0)
    def _():
        for c in copies(sched_ref[first_expert_row, 0], 0):
            c.start()

    @pl.when(sched_ref[SCHED_FIRST, i] == 1)
    def _():
        for c in copies(0, slot):
            c.wait()

        @pl.when(sched_ref[SCHED_NEXT, i] >= 0)
        def _():
            for c in copies(sched_ref[SCHED_NEXT, i], 1 - slot):
                c.start()

    @pl.when(i < n_used_ref[0])
    def _():
        words = x_ref[...]
        row = lax.broadcasted_iota(jnp.int32, words.shape, 0)
        words = jnp.where(row < sched_ref[SCHED_VALID, i], words, 0)
        x = jnp.concatenate(_unpack_halves(words), axis=1).astype(BF16)
        g = jnp.dot(x, wg_buf[slot].astype(BF16), preferred_element_type=F32)
        u = jnp.dot(x, wu_buf[slot].astype(BF16), preferred_element_type=F32)
        a = (_silu(g) * u).astype(BF16)
        o_ref[...] = _pack_halves(jnp.dot(a, wd_buf[slot].astype(BF16), preferred_element_type=F32))

    @pl.when(i >= n_used_ref[0])
    def _():
        o_ref[...] = jnp.zeros_like(o_ref)


def _expert_schedule(seg_start, seg_end, counts, n_blocks):
    E = seg_end.shape[0]
    first_row = jnp.arange(n_blocks, dtype=jnp.int32) * EXPERT_ROWS
    blk_e = jnp.minimum(jnp.sum(seg_end[None, :] <= first_row[:, None], axis=1), E - 1)
    used = first_row < seg_end[-1]
    first = used & (first_row == seg_start[blk_e])
    ordinal = jnp.cumsum(first) - 1
    ids = jnp.where(seg_end > seg_start, jnp.arange(E), E)
    later = lax.cummin(ids, reverse=True)
    nxt = jnp.concatenate([later[1:], jnp.full((1,), E, later.dtype)])[blk_e]
    nxt = jnp.where(nxt < E, nxt, -1)
    valid = jnp.clip((seg_start + counts)[blk_e] - first_row, 0, EXPERT_ROWS)
    return jnp.stack([first, ordinal % 2, nxt, valid, blk_e]).astype(jnp.int32)


def _expert_mlp(xs, sched, n_used, w_gate, w_up, w_down, layer):
    n_rows, half = xs.shape
    D = 2 * half
    F = w_gate.shape[-1]
    tm = EXPERT_ROWS
    hbm = pl.BlockSpec(memory_space=pl.ANY)
    return pl.pallas_call(
        functools.partial(_expert_kernel, layer=layer, first_expert_row=sched.shape[0] - 1),
        out_shape=jax.ShapeDtypeStruct((n_rows, half), jnp.int32),
        grid_spec=pltpu.PrefetchScalarGridSpec(
            num_scalar_prefetch=2,
            grid=(n_rows // tm,),
            in_specs=[pl.BlockSpec((tm, half), lambda i, sc, nu: (jnp.minimum(i, nu[0] - 1), 0)),
                      hbm, hbm, hbm],
            out_specs=pl.BlockSpec((tm, half), lambda i, sc, nu: (i, 0)),
            scratch_shapes=[pltpu.VMEM((2, D, F), F32), pltpu.VMEM((2, D, F), F32),
                            pltpu.VMEM((2, F, D), F32), pltpu.SemaphoreType.DMA((2, 3))]),
        compiler_params=_cparams("arbitrary"),
        name="moe_experts",
    )(sched, n_used, xs, w_gate, w_up, w_down)


def _combine_kernel(x_ref, gate_ref, rows_ref, wsg_ref, wsu_ref, wsd_ref, g_ref, b_ref, o_ref,
                    *, alpha):
    x = x_ref[...]
    xb = x.astype(BF16)
    g = jnp.dot(xb, wsg_ref[...], preferred_element_type=F32)
    u = jnp.dot(xb, wsu_ref[...], preferred_element_type=F32)
    y = jnp.dot((_silu(g) * u).astype(BF16), wsd_ref[...], preferred_element_type=F32)
    gates = gate_ref[...]
    half = y.shape[1] // 2
    left, right = y[:, :half], y[:, half:]
    for k in range(TOP_K):
        lo, hi = _unpack_halves(rows_ref[k])
        left = left + lo * gates[:, k:k + 1]
        right = right + hi * gates[:, k:k + 1]
    y = jnp.concatenate([left, right], axis=1)
    o_ref[...] = _layer_norm(alpha * x + y, g_ref[...], b_ref[...])


def _combine(x2d, ys, dest, gates_tok, ws_gate, ws_up, ws_down, g, b, alpha, tt):
    T, D = x2d.shape
    F = ws_gate.shape[1]
    row = lambda i: (i, 0)
    fixed = lambda i: (0, 0)
    rows = _sc_gather_rows(ys, dest.reshape(-1)).reshape(TOP_K, T, D // 2)
    return pl.pallas_call(
        functools.partial(_combine_kernel, alpha=alpha),
        out_shape=jax.ShapeDtypeStruct((T, D), F32),
        grid=(T // tt,),
        in_specs=[pl.BlockSpec((tt, D), row),
                  pl.BlockSpec((tt, LANES), row),
                  pl.BlockSpec((TOP_K, tt, D // 2), lambda i: (0, i, 0)),
                  pl.BlockSpec((D, F), fixed), pl.BlockSpec((D, F), fixed),
                  pl.BlockSpec((F, D), fixed),
                  pl.BlockSpec((1, D), fixed), pl.BlockSpec((1, D), fixed)],
        out_specs=pl.BlockSpec((tt, D), row),
        compiler_params=_cparams("parallel"),
        name="moe_combine",
    )(x2d, gates_tok, rows, ws_gate.astype(BF16), ws_up.astype(BF16), ws_down.astype(BF16),
      g.reshape(1, D), b.reshape(1, D))


def _pick(n, pref):
    t = min(n, pref)
    while n % t:
        t //= 2
    return t


def _moe_layer(x2d, packed, w_router, router_bias, w_gate, w_up, w_down, layer, ws_gate, ws_up,
               ws_down, g, b, alpha):
    T, D = x2d.shape
    E = w_router.shape[1]
    idx, gates, rank, counts = _router(x2d, w_router, router_bias, _pick(T, 512))
    counts = counts[:, 0].astype(jnp.int32)
    padded = (counts + EXPERT_ROWS - 1) // EXPERT_ROWS * EXPERT_ROWS
    seg_end = jnp.cumsum(padded)
    seg_start = seg_end - padded
    n_blocks = (T * TOP_K + E * (EXPERT_ROWS - 1)) // EXPERT_ROWS
    dest = _slots(idx, rank, seg_start, _pick(T, 1024))
    sched = _expert_schedule(seg_start, seg_end, counts, n_blocks)
    n_used = (seg_end[-1:] // EXPERT_ROWS).astype(jnp.int32)
    xs = _sc_scatter_rows(packed, dest, n_blocks * EXPERT_ROWS)
    ys = _expert_mlp(xs, sched, n_used, w_gate, w_up, w_down, layer)
    return _combine(x2d, ys, dest, gates, ws_gate, ws_up, ws_down, g, b, alpha, _pick(T, 256))


def _mixer_layer(x2d, i, B, S, p, alpha):
    T, D = x2d.shape
    tm = _pick(T, 512)
    tn = _pick(D, 1024)
    j = i // 2
    w_in = p["moba_w_in"][j] if i % 2 == 0 else p["fox_w_in"][j]
    w_qv_t = jnp.concatenate([w_in[:, :D].T * (HEAD_DIM ** -0.5 * LOG2E),
                              w_in[:, 2 * D:3 * D].T]).astype(BF16)
    qv_t = _project_t(x2d, w_qv_t, tm, tn)
    w_k = w_in[:, D:2 * D].astype(BF16)
    if i % 2 == 0:
        k_aug, k_mean = _moba_keys(x2d, w_k, B, S, tm)
        attn_t = _moba_attention(qv_t, k_aug, k_mean, p["rel_bias"], B, S)
        w_out = p["moba_w_out"][j]
    else:
        c_tok = _fox_gates(x2d.reshape(B, S, D), w_in[:, 3 * D:], p["fox_b_f"][j], _pick(S, 512))
        k_aug = _fox_keys(x2d, w_k, c_tok.reshape(T, LANES), tm).reshape(HEADS_PER_TILE, B, S, D)
        attn_t = _fox_attention(qv_t, k_aug, B, S, _pick(S, 256))
        w_out = p["fox_w_out"][j]
    return _outproj_ln(attn_t, w_out.astype(BF16), x2d, p["ln1_g"][i], p["ln1_b"][i], alpha, tm)


def kernel(x, rel_bias, moba_w_in, moba_w_out, fox_w_in, fox_b_f, fox_w_out, ln1_g, ln1_b, ln2_g,
           ln2_b, w_router, router_bias, w_gate, w_up, w_down, ws_gate, ws_up, ws_down):
    B, S, D = x.shape
    depth = ln1_g.shape[0]
    alpha = (2 * depth) ** 0.25
    p = dict(rel_bias=rel_bias, moba_w_in=moba_w_in, moba_w_out=moba_w_out, fox_w_in=fox_w_in,
             fox_b_f=fox_b_f, fox_w_out=fox_w_out, ln1_g=ln1_g, ln1_b=ln1_b)
    x2d = x.reshape(B * S, D)
    for i in range(depth):
        x2d, packed = _mixer_layer(x2d, i, B, S, p, alpha)
        x2d = _moe_layer(x2d, packed, w_router[i], router_bias[i], w_gate, w_up, w_down, i,
                         ws_gate[i], ws_up[i], ws_down[i], ln2_g[i], ln2_b[i], alpha)
    return x2d.reshape(B, S, D)
```

```python
import functools
import math

import jax
import jax.numpy as jnp
from jax import lax
from jax.experimental import pallas as pl
from jax.experimental.pallas import tpu as pltpu
from jax.experimental.pallas import tpu_sc as plsc

F32 = jnp.float32
BF16 = jnp.bfloat16
HIGHEST = lax.Precision.HIGHEST

HEAD_DIM = 64
MOBA_BLOCK = 256
MOBA_TOPK = 3
REL_MAX_DIST = 128
TOP_K = 8
N_GROUPS = 8
TOPK_GROUPS = 4
ROUTED_SCALE = 2.5
LN_EPS = 1e-5
NEG = -1e30
LOG2E = math.log2(math.e)

LANES = 128
HEADS_PER_TILE = LANES // HEAD_DIM
EXPERT_ROWS = 512
VMEM_LIMIT = 48 * 1024 * 1024


def _cparams(*sem):
    return pltpu.CompilerParams(dimension_semantics=sem, vmem_limit_bytes=VMEM_LIMIT)


def _nt_dot(a, b, **kw):
    return lax.dot_general(a, b, (((1,), (1,)), ((), ())), preferred_element_type=F32, **kw)


def _split_bf16(x, pieces):
    out = []
    for _ in range(pieces):
        out.append(x.astype(BF16))
        x = x - out[-1].astype(F32)
    return out


def _three_pass(x, w_ref, mul):
    hi, lo = _split_bf16(x, 2)
    return mul(hi, w_ref[0]) + mul(lo, w_ref[0]) + mul(hi, w_ref[1])


def _proj_t_kernel(x_ref, wt_ref, o_ref):
    o_ref[...] = _nt_dot(wt_ref[...], x_ref[...].astype(BF16)).astype(o_ref.dtype)


def _project_t(x2d, wt, tm, tn):
    T, K = x2d.shape
    N = wt.shape[0]
    return pl.pallas_call(
        _proj_t_kernel,
        out_shape=jax.ShapeDtypeStruct((N, T), BF16),
        grid=(T // tm, N // tn),
        in_specs=[pl.BlockSpec((tm, K), lambda i, j: (i, 0)),
                  pl.BlockSpec((tn, K), lambda i, j: (j, 0))],
        out_specs=pl.BlockSpec((tn, tm), lambda i, j: (j, i)),
        compiler_params=_cparams("parallel", "arbitrary"),
        name="qv_proj_t",
    )(x2d, wt)


def _layer_norm(r, g, b):
    mu = jnp.mean(r, axis=-1, keepdims=True)
    d = r - mu
    var = jnp.mean(d * d, axis=-1, keepdims=True)
    return d * lax.rsqrt(var + LN_EPS) * g + b


def _pack_halves(y):
    n = y.shape[1] // 2
    bits = lambda v: lax.bitcast_convert_type(v.astype(BF16).astype(F32), jnp.int32)
    left = bits(y[:, :n])
    return lax.shift_right_logical(left, jnp.full_like(left, 16)) | bits(y[:, n:])


def _unpack_halves(p):
    return (lax.bitcast_convert_type(p << 16, F32),
            lax.bitcast_convert_type(p & jnp.int32(-65536), F32))


def _outproj_ln_kernel(at_ref, w_ref, x_ref, g_ref, b_ref, o_ref, packed_ref, *, alpha):
    y = lax.dot_general(at_ref[...], w_ref[...], (((0,), (0,)), ((), ())),
                        preferred_element_type=F32)
    out = _layer_norm(alpha * x_ref[...] + y, g_ref[...], b_ref[...])
    o_ref[...] = out
    packed_ref[...] = _pack_halves(out)


def _outproj_ln(attn_t, w_out, x2d, g, b, alpha, tm):
    T, D = x2d.shape
    row = lambda i: (i, 0)
    fixed = lambda i: (0, 0)
    return pl.pallas_call(
        functools.partial(_outproj_ln_kernel, alpha=alpha),
        out_shape=(jax.ShapeDtypeStruct((T, D), F32), jax.ShapeDtypeStruct((T, D // 2), jnp.int32)),
        grid=(T // tm,),
        in_specs=[pl.BlockSpec((D, tm), lambda i: (0, i)), pl.BlockSpec((D, D), fixed),
                  pl.BlockSpec((tm, D), row), pl.BlockSpec((1, D), fixed),
                  pl.BlockSpec((1, D), fixed)],
        out_specs=(pl.BlockSpec((tm, D), row), pl.BlockSpec((tm, D // 2), row)),
        compiler_params=_cparams("parallel"),
        name="outproj_ln",
    )(attn_t, w_out, x2d, g.reshape(1, D), b.reshape(1, D))


def _spare(a, n=0):
    return ((a + 1) % HEADS_PER_TILE) * HEAD_DIM + n


def _scores(chains):
    return [[jnp.dot(keys(), query(), preferred_element_type=F32) for keys, query, _ in blocks]
            for blocks in chains]


def _absorb(carries, scores, chains, masks=None, adds=None):
    probs = []
    for c, ((m, _), parts) in enumerate(zip(carries, scores)):
        if adds is not None:
            parts = [s if add is None else s + add() for add, s in zip(adds[c], parts)]
        if masks is not None:
            parts = [s if keep is None else jnp.where(keep, s, NEG)
                     for keep, s in zip(masks[c], parts)]
        tops = [jnp.max(s, axis=0, keepdims=True) for s in parts]
        m_new = functools.reduce(jnp.maximum, tops, m)
        probs.append((m_new, jnp.exp2(m - m_new),
                      [jnp.exp2(s - m_new).astype(BF16) for s in parts]))
    out = []
    for (_, acc), (m_new, decay, ps), blocks in zip(carries, probs, chains):
        acc = decay * acc
        for (_, _, values), p in zip(blocks, ps):
            acc = acc + jnp.dot(values(), p, preferred_element_type=F32)
        out.append((m_new, acc))
    return tuple(out)


def _stash(chains, ref):
    for a, parts in enumerate(_scores(chains)):
        for b, s in enumerate(parts):
            ref[a, b] = s


def _fetch(chains, ref):
    return [[ref[a, b] for b in range(len(blocks))] for a, blocks in enumerate(chains)]


def _attend_masked_last(carries, n, group, stage_refs, masks, adds=None):
    _stash(group(0), stage_refs[0])

    def run(carries, start, count, last_is_masked):
        for q in range(count):
            g = start + q
            if q + 1 < count or not last_is_masked:
                _stash(group(g + 1), stage_refs[(q + 1) % 2])
            special = last_is_masked and q + 1 == count
            carries = _absorb(carries, _fetch(group(g), stage_refs[q % 2]), group(g),
                              masks if special else None, adds if special else None)
        return carries

    trips = n // TRIP_GROUPS
    carries = lax.fori_loop(0, trips, lambda t, c: run(c, TRIP_GROUPS * t, TRIP_GROUPS, False),
                            carries)
    tails = [functools.partial(run, start=n - r, count=r + 1, last_is_masked=True)
             for r in range(TRIP_GROUPS)]
    return lax.switch(n % TRIP_GROUPS, tails, carries)


GROUP_BLOCKS = 2
TRIP_GROUPS = 8


def _stage_scratch(tq, chains=HEADS_PER_TILE):
    return [pltpu.VMEM((chains, GROUP_BLOCKS, tq, tq), F32) for _ in range(2)]


def _scores_init(tq):
    return jnp.full((1, tq), -jnp.inf, F32), jnp.zeros((LANES, tq), F32)


def _finish(carries, o_ref, cols=slice(None)):
    row = lax.broadcasted_iota(jnp.int32, carries[0][1].shape, 0)
    outs = []
    for a, (_, acc) in enumerate(carries):
        s = _spare(a)
        outs.append(acc / acc[s:s + 1, :])
    o_ref[:, cols] = jnp.where(row < HEAD_DIM, outs[0], outs[1]).astype(o_ref.dtype)


def _fill_v_aug(vt_ref, vaug_ref, tk):
    n_tiles = vt_ref.shape[1] // tk
    row = lax.broadcasted_iota(jnp.int32, (LANES, tk), 0)
    for n in range(n_tiles):
        v = vt_ref[:, n * tk:(n + 1) * tk]
        for a in range(HEADS_PER_TILE):
            vaug_ref[a, n] = jnp.where(row // HEAD_DIM == a, v, jnp.ones_like(v))


FOX_SPLIT = 3


def _fox_kernel(qt_ref, k_ref, vt_ref, o_ref, vaug_ref, stage0_ref, stage1_ref, *, tq):
    n = pl.program_id(2)

    @pl.when(n == 0)
    def _():
        _fill_v_aug(vt_ref, vaug_ref, tq)

    chains = [(h, a) for h in range(GROUP_BLOCKS) for a in range(HEADS_PER_TILE)]
    row = lax.broadcasted_iota(jnp.int32, (LANES, tq), 0)
    queries = {}
    for h, a in chains:
        offs = (row >= _spare(a)) & (row < _spare(a, FOX_SPLIT))
        queries[h, a] = jnp.where(row // HEAD_DIM == a, qt_ref[:, h * tq:(h + 1) * tq],
                                  jnp.where(offs, -1.0, 0.0).astype(BF16))

    def block(h, a, j):
        start = pl.multiple_of(j * tq, tq)
        return (lambda: k_ref[a, 0, pl.ds(start, tq), :], lambda: queries[h, a],
                lambda: vaug_ref[a, j])

    key = lax.broadcasted_iota(jnp.int32, (tq, tq), 0)
    qry = lax.broadcasted_iota(jnp.int32, (tq, tq), 1)
    masks = [[None if b < h else key + (b - h) * tq <= qry for b in range(GROUP_BLOCKS)]
             for h, _ in chains]
    carries = _attend_masked_last(
        tuple(_scores_init(tq) for _ in chains), n,
        lambda j: [[block(h, a, 2 * j), block(h, a, 2 * j + 1)] for h, a in chains],
        (stage0_ref, stage1_ref), masks)
    for h in range(GROUP_BLOCKS):
        _finish(carries[h * HEADS_PER_TILE:(h + 1) * HEADS_PER_TILE], o_ref,
                slice(h * tq, (h + 1) * tq))


def _fox_attention(qv_t, k_aug, B, S, tq):
    D = k_aug.shape[-1]
    n_tiles = D // LANES
    nq = S // tq
    span = GROUP_BLOCKS * tq
    steps = S // span
    return pl.pallas_call(
        functools.partial(_fox_kernel, tq=tq),
        out_shape=jax.ShapeDtypeStruct((D, B * S), BF16),
        grid=(B, n_tiles, steps),
        in_specs=[pl.BlockSpec((LANES, span), lambda b, h, i: (h, b * steps + i)),
                  pl.BlockSpec((HEADS_PER_TILE, 1, S, LANES), lambda b, h, i: (0, b, 0, h)),
                  pl.BlockSpec((LANES, S), lambda b, h, i: (n_tiles + h, b))],
        out_specs=pl.BlockSpec((LANES, span), lambda b, h, i: (h, b * steps + i)),
        scratch_shapes=[pltpu.VMEM((HEADS_PER_TILE, nq, LANES, tq), BF16)]
        + _stage_scratch(tq, GROUP_BLOCKS * HEADS_PER_TILE),
        compiler_params=_cparams("parallel", "parallel", "arbitrary"),
        name="fox_attention",
    )(qv_t, k_aug, qv_t)


def _log_sigmoid(z):
    return jnp.minimum(z, 0.0) - jnp.log1p(jnp.exp(-jnp.abs(z)))


def _fox_gates_kernel(x_ref, w_ref, b_ref, o_ref, carry_ref, *, ts):
    @pl.when(pl.program_id(1) == 0)
    def _():
        carry_ref[...] = jnp.zeros_like(carry_ref)

    z = _three_pass(x_ref[0], w_ref, lambda x, w: jnp.dot(x, w, preferred_element_type=F32))
    lf = _log_sigmoid(z + b_ref[...])
    r = lax.broadcasted_iota(jnp.int32, (ts, ts), 0)
    c = lax.broadcasted_iota(jnp.int32, (ts, ts), 1)
    tri = (c <= r).astype(BF16)
    cum = carry_ref[...]
    for piece in _split_bf16(lf, 3):
        cum = cum + jnp.dot(tri, piece, preferred_element_type=F32)
    carry_ref[...] = cum[ts - 1:ts, :]
    o_ref[0] = cum


def _fox_gates(x3d, w_f, b_f, ts):
    B, S, D = x3d.shape
    H = w_f.shape[1]
    w_pad = jnp.stack(_split_bf16(jnp.zeros((D, LANES), F32).at[:, :H].set(w_f), 2))
    b_pad = jnp.zeros((1, LANES), F32).at[0, :H].set(b_f)
    return pl.pallas_call(
        functools.partial(_fox_gates_kernel, ts=ts),
        out_shape=jax.ShapeDtypeStruct((B, S, LANES), F32),
        grid=(B, S // ts),
        in_specs=[pl.BlockSpec((1, ts, D), lambda b, s: (b, s, 0)),
                  pl.BlockSpec((2, D, LANES), lambda b, s: (0, 0, 0)),
                  pl.BlockSpec((1, LANES), lambda b, s: (0, 0))],
        out_specs=pl.BlockSpec((1, ts, LANES), lambda b, s: (b, s, 0)),
        scratch_shapes=[pltpu.VMEM((1, LANES), F32)],
        compiler_params=_cparams("parallel", "arbitrary"),
        name="fox_gates",
    )(x3d, w_pad, b_pad)


def _fox_keys_kernel(x_ref, w_ref, c_ref, o_ref):
    keys = jnp.dot(x_ref[...].astype(BF16), w_ref[...], preferred_element_type=F32)
    tm, D = keys.shape
    lane = lax.broadcasted_iota(jnp.int32, (tm, LANES), 1)
    for t in range(D // LANES):
        k = keys[:, t * LANES:(t + 1) * LANES]
        for a in range(HEADS_PER_TILE):
            h = t * HEADS_PER_TILE + a
            rest = c_ref[:, h:h + 1] * LOG2E
            aug = jnp.zeros((tm, LANES), F32)
            for n in range(FOX_SPLIT):
                piece = rest.astype(BF16).astype(F32)
                rest = rest - piece
                aug = jnp.where(lane == _spare(a, n), piece, aug)
            o_ref[a, :, t * LANES:(t + 1) * LANES] = jnp.where(lane // HEAD_DIM == a, k,
                                                               aug).astype(BF16)


def _fox_keys(x2d, w_k, c_tok, tm):
    T, K = x2d.shape
    D = w_k.shape[1]
    return pl.pallas_call(
        _fox_keys_kernel,
        out_shape=jax.ShapeDtypeStruct((HEADS_PER_TILE, T, D), BF16),
        grid=(T // tm,),
        in_specs=[pl.BlockSpec((tm, K), lambda i: (i, 0)),
                  pl.BlockSpec((K, D), lambda i: (0, 0)),
                  pl.BlockSpec((tm, LANES), lambda i: (i, 0))],
        out_specs=pl.BlockSpec((HEADS_PER_TILE, tm, D), lambda i: (0, i, 0)),
        compiler_params=_cparams("parallel"),
        name="fox_keys",
    )(x2d, w_k, c_tok)


MOBA_SLAB = 16
KMEAN_SPLIT = 3


def _moba_kernel(bfar_ref, qt_ref, k_ref, vt_ref, kmean_ref, bown_ref, badj_ref, o_ref,
                 vaug_ref, sel_ref, stage0_ref, stage1_ref, *, blk):
    t = pl.program_id(1)
    n = pl.program_id(2)
    nb = kmean_ref.shape[2]

    @pl.when(n == 0)
    def _():
        _fill_v_aug(vt_ref, vaug_ref, blk)

    chains = [(h, a) for h in range(GROUP_BLOCKS) for a in range(HEADS_PER_TILE)]
    row = lax.broadcasted_iota(jnp.int32, (LANES, blk), 0)
    brow = lax.broadcasted_iota(jnp.int32, (nb, blk), 0)
    base = {}
    for c, (h, a) in enumerate(chains):
        own = GROUP_BLOCKS * n + h
        qt = qt_ref[:, h * blk:(h + 1) * blk]
        qa = jnp.where(row // HEAD_DIM == a, qt, jnp.zeros_like(qt))
        base[h, a] = qa
        bscore = sum(jnp.dot(kmean_ref[piece, 0], qa, preferred_element_type=F32)
                     for piece in range(KMEAN_SPLIT))
        cand = jnp.where(brow < own, bscore, NEG)
        sel = jnp.zeros((nb, blk), F32)
        for _ in range(MOBA_TOPK):
            mx = jnp.max(cand, axis=0, keepdims=True)
            first = jnp.min(jnp.where(cand == mx, brow, nb), axis=0, keepdims=True)
            hit = brow == first
            sel = jnp.where(hit, 1.0, sel)
            cand = jnp.where(hit, -jnp.inf, cand)
        sel_ref[c] = jnp.where(brow < own, sel, 0.0)

    srow = lax.broadcasted_iota(jnp.int32, (MOBA_SLAB, blk), 0)

    def block(c, j):
        h, a = chains[c]
        head_id = t * HEADS_PER_TILE + a
        jc = jnp.maximum(j, 0)
        start = pl.multiple_of(jc * blk, blk)

        def query():
            s0 = _spare(a) // MOBA_SLAB * MOBA_SLAB
            chosen = (((sel_ref[c, pl.ds(jc, 1), :] > 0.0) & (j >= 0))
                      | (j == GROUP_BLOCKS * n + h))
            slab = jnp.where(srow == _spare(a) - s0,
                             jnp.where(chosen, bfar_ref[0, head_id], NEG),
                             jnp.where(srow == _spare(a, 1) - s0, bfar_ref[1, head_id], 0.0)
                             ).astype(BF16)
            q = base[h, a]
            head = [q[:s0]] if s0 else []
            tail = [q[s0 + MOBA_SLAB:]] if s0 + MOBA_SLAB < LANES else []
            return jnp.concatenate(head + [slab] + tail, axis=0)

        return lambda: k_ref[a, 0, pl.ds(start, blk), :], query, lambda: vaug_ref[a, jc]

    key = lax.broadcasted_iota(jnp.int32, (blk, blk), 0)
    qry = lax.broadcasted_iota(jnp.int32, (blk, blk), 1)
    carries = _attend_masked_last(
        tuple(_scores_init(blk) for _ in chains), n,
        lambda g: [[block(c, 2 * g + h - 1), block(c, 2 * g + h)] for c, (h, _) in enumerate(chains)],
        (stage0_ref, stage1_ref), [[None, key <= qry] for _ in chains],
        [[lambda a=a: badj_ref[a], lambda a=a: bown_ref[a]] for _, a in chains])
    for h in range(GROUP_BLOCKS):
        _finish(carries[h * HEADS_PER_TILE:(h + 1) * HEADS_PER_TILE], o_ref,
                slice(h * blk, (h + 1) * blk))


def _rel_bucket(dist, n_buckets):
    n = jnp.maximum(dist, 0)
    max_exact = n_buckets // 2
    nf = jnp.maximum(n, 1).astype(F32)
    large = max_exact + (jnp.log(nf / max_exact) / math.log(REL_MAX_DIST / max_exact)
                         * (n_buckets - max_exact)).astype(jnp.int32)
    large = jnp.minimum(large, n_buckets - 1)
    return jnp.where(n < max_exact, n, large)


def _moba_attention(qv_t, k_aug, k_mean, rel_bias, B, S):
    D = k_aug.shape[-1]
    n_tiles = D // LANES
    blk = MOBA_BLOCK
    nb = S // blk
    hp = HEADS_PER_TILE
    span = GROUP_BLOCKS * blk
    steps = S // span
    n_chains = GROUP_BLOCKS * hp
    r = jnp.arange(blk)
    delta = r[None, :] - r[:, None]
    bias_t = rel_bias.T.astype(F32)
    n_buckets = rel_bias.shape[0]
    def table(dist):
        onehot = jax.nn.one_hot(_rel_bucket(dist, n_buckets), n_buckets, dtype=F32)
        return jnp.einsum('crn,hn->hcr', onehot, bias_t, precision=HIGHEST)

    b_far = bias_t[:, n_buckets - 1] * LOG2E
    b_own = table(delta) * LOG2E - b_far[:, None, None]
    b_adj = table(delta + blk) * LOG2E - b_far[:, None, None]
    far_hi = b_far.astype(BF16).astype(F32)
    far_lo = (b_far - far_hi).astype(BF16).astype(F32)
    b_far2 = jnp.stack([far_hi, far_lo])
    return pl.pallas_call(
        functools.partial(_moba_kernel, blk=blk),
        out_shape=jax.ShapeDtypeStruct((D, B * S), BF16),
        grid_spec=pltpu.PrefetchScalarGridSpec(
            num_scalar_prefetch=1,
            grid=(B, n_tiles, steps),
            in_specs=[pl.BlockSpec((LANES, span), lambda b, h, i, f: (h, b * steps + i)),
                      pl.BlockSpec((hp, 1, S, LANES), lambda b, h, i, f: (0, b, 0, h)),
                      pl.BlockSpec((LANES, S), lambda b, h, i, f: (n_tiles + h, b)),
                      pl.BlockSpec((KMEAN_SPLIT, 1, nb, LANES), lambda b, h, i, f: (0, b, 0, h)),
                      pl.BlockSpec((hp, blk, blk), lambda b, h, i, f: (h, 0, 0)),
                      pl.BlockSpec((hp, blk, blk), lambda b, h, i, f: (h, 0, 0))],
            out_specs=pl.BlockSpec((LANES, span), lambda b, h, i, f: (h, b * steps + i)),
            scratch_shapes=[pltpu.VMEM((hp, nb, LANES, blk), BF16),
                            pltpu.VMEM((n_chains, nb, blk), F32)] + _stage_scratch(blk, n_chains)),
        compiler_params=_cparams("parallel", "parallel", "arbitrary"),
        name="moba_attention",
    )(b_far2, qv_t, k_aug, qv_t, k_mean, b_own, b_adj)


def _moba_keys_kernel(x_ref, w_ref, o_ref, mean_ref):
    k = jnp.dot(x_ref[...].astype(BF16), w_ref[...], preferred_element_type=F32)
    col = lax.broadcasted_iota(jnp.int32, k.shape, 1)
    lane = col % LANES
    for a in range(HEADS_PER_TILE):
        ones = (lane == _spare(a)) | (lane == _spare(a, 1))
        o_ref[a] = jnp.where((col // HEAD_DIM) % HEADS_PER_TILE == a, k,
                             jnp.where(ones, 1.0, 0.0)).astype(BF16)
    for n in range(mean_ref.shape[0]):
        mean_ref[n] = jnp.mean(k[n * MOBA_BLOCK:(n + 1) * MOBA_BLOCK], axis=0, keepdims=True)


def _moba_keys(x2d, w_k, B, S, tm):
    T, K = x2d.shape
    D = w_k.shape[1]
    nb = S // MOBA_BLOCK
    per_tile = tm // MOBA_BLOCK
    k_aug, k_mean = pl.pallas_call(
        _moba_keys_kernel,
        out_shape=(jax.ShapeDtypeStruct((HEADS_PER_TILE, T, D), BF16),
                   jax.ShapeDtypeStruct((B * nb, 1, D), F32)),
        grid=(T // tm,),
        in_specs=[pl.BlockSpec((tm, K), lambda i: (i, 0)),
                  pl.BlockSpec((K, D), lambda i: (0, 0))],
        out_specs=(pl.BlockSpec((HEADS_PER_TILE, tm, D), lambda i: (0, i, 0)),
                   pl.BlockSpec((per_tile, 1, D), lambda i: (i, 0, 0))),
        compiler_params=_cparams("parallel"),
        name="moba_keys",
    )(x2d, w_k)
    k_mean = jnp.stack(_split_bf16(k_mean.reshape(B, nb, D), KMEAN_SPLIT))
    return k_aug.reshape(HEADS_PER_TILE, B, S, D), k_mean


def _router_kernel(x_ref, wt_ref, bias_ref, idx_ref, gate_ref, rank_ref, cnt_ref, run_ref, *, tm):
    @pl.when(pl.program_id(0) == 0)
    def _():
        run_ref[...] = jnp.zeros_like(run_ref)

    E = wt_ref.shape[1]
    gsz = E // N_GROUPS
    logits = _three_pass(x_ref[...], wt_ref, lambda x, w: _nt_dot(w, x))
    s = jax.nn.sigmoid(logits)
    sb = s + bias_ref[...]
    neg_inf = -jnp.inf

    giota = lax.broadcasted_iota(jnp.int32, (gsz, tm), 0)
    gscore = []
    for g in range(N_GROUPS):
        blk = sb[g * gsz:(g + 1) * gsz, :]
        m1 = jnp.max(blk, axis=0, keepdims=True)
        i1 = jnp.min(jnp.where(blk == m1, giota, gsz), axis=0, keepdims=True)
        m2 = jnp.max(jnp.where(giota == i1, neg_inf, blk), axis=0, keepdims=True)
        gscore.append(m1 + m2)

    gsel = [jnp.zeros((1, tm), jnp.bool_) for _ in range(N_GROUPS)]
    for _ in range(TOPK_GROUPS):
        mx = functools.reduce(jnp.maximum, gscore)
        found = jnp.zeros((1, tm), jnp.bool_)
        for g in range(N_GROUPS):
            hit = (gscore[g] == mx) & jnp.logical_not(found)
            gsel[g] = gsel[g] | hit
            found = found | hit
            gscore[g] = jnp.where(hit, neg_inf, gscore[g])
    emask = jnp.concatenate([jnp.broadcast_to(gsel[g], (gsz, tm)) for g in range(N_GROUPS)], axis=0)
    cand = jnp.where(emask, sb, NEG)

    eiota = lax.broadcasted_iota(jnp.int32, (E, tm), 0)
    hits, idxs, ws = [], [], []
    for _ in range(TOP_K):
        mx = jnp.max(cand, axis=0, keepdims=True)
        first = jnp.min(jnp.where(cand == mx, eiota, E), axis=0, keepdims=True)
        hit = eiota == first
        hits.append(hit)
        idxs.append(first)
        ws.append(jnp.sum(jnp.where(hit, s, 0.0), axis=0, keepdims=True))
        cand = jnp.where(hit, neg_inf, cand)
    wsum = functools.reduce(jnp.add, ws)

    chosen = functools.reduce(jnp.logical_or, hits)
    onehot = jnp.where(chosen, 1.0, 0.0)
    tr = lax.broadcasted_iota(jnp.int32, (tm, tm), 0)
    tc = lax.broadcasted_iota(jnp.int32, (tm, tm), 1)
    before = (tr < tc).astype(BF16)
    prior = jnp.dot(onehot.astype(BF16), before, preferred_element_type=F32) + run_ref[...]
    grow = lax.broadcasted_iota(jnp.int32, (LANES, tm), 0)
    gates = jnp.zeros((LANES, tm), F32)
    for k in range(TOP_K):
        idx_ref[k:k + 1, :] = idxs[k]
        gates = jnp.where(grow == k, ws[k] / wsum * ROUTED_SCALE, gates)
        rank_ref[k:k + 1, :] = jnp.sum(jnp.where(hits[k], prior, 0.0), axis=0,
                                       keepdims=True).astype(jnp.int32)
    gate_ref[...] = gates.T
    run_ref[...] = run_ref[...] + jnp.sum(onehot, axis=1, keepdims=True)
    cnt_ref[...] = run_ref[...]


def _router(x2d, w_router, router_bias, tm):
    T, D = x2d.shape
    E = w_router.shape[1]
    tok = lambda i: (0, i)
    fixed = lambda i: (0, 0)
    return pl.pallas_call(
        functools.partial(_router_kernel, tm=tm),
        out_shape=(jax.ShapeDtypeStruct((TOP_K, T), jnp.int32),
                   jax.ShapeDtypeStruct((T, LANES), F32),
                   jax.ShapeDtypeStruct((TOP_K, T), jnp.int32),
                   jax.ShapeDtypeStruct((E, 1), F32)),
        grid=(T // tm,),
        in_specs=[pl.BlockSpec((tm, D), lambda i: (i, 0)),
                  pl.BlockSpec((2, E, D), lambda i: (0, 0, 0)),
                  pl.BlockSpec((E, 1), fixed)],
        out_specs=(pl.BlockSpec((TOP_K, tm), tok), pl.BlockSpec((tm, LANES), lambda i: (i, 0)),
                   pl.BlockSpec((TOP_K, tm), tok), pl.BlockSpec((E, 1), fixed)),
        scratch_shapes=[pltpu.VMEM((E, 1), F32)],
        compiler_params=_cparams("arbitrary"),
        name="router",
    )(x2d, jnp.stack(_split_bf16(w_router.T, 2)), router_bias.reshape(E, 1).astype(F32))


def _slots_kernel(idx_ref, rank_ref, start_ref, o_ref):
    E = start_ref.shape[0]
    tm = idx_ref.shape[1]
    eiota = lax.broadcasted_iota(jnp.int32, (E, tm), 0)
    start = start_ref[...]
    for k in range(TOP_K):
        base = jnp.sum(jnp.where(eiota == idx_ref[k:k + 1, :], start, 0.0), axis=0, keepdims=True)
        o_ref[k:k + 1, :] = base.astype(jnp.int32) + rank_ref[k:k + 1, :]


def _slots(idx, rank, seg_start, tm):
    T = idx.shape[1]
    E = seg_start.shape[0]
    tok = lambda i: (0, i)
    return pl.pallas_call(
        _slots_kernel,
        out_shape=jax.ShapeDtypeStruct((TOP_K, T), jnp.int32),
        grid=(T // tm,),
        in_specs=[pl.BlockSpec((TOP_K, tm), tok), pl.BlockSpec((TOP_K, tm), tok),
                  pl.BlockSpec((E, 1), lambda i: (0, 0))],
        out_specs=pl.BlockSpec((TOP_K, tm), tok),
        compiler_params=_cparams("parallel"),
        name="moe_slots",
    )(idx, rank, seg_start.astype(F32).reshape(E, 1))


SC_CORES = 2
SC_SUBCORES = 16
SC_CHUNK = 128


def _sc_rows_kernel(body, out_shape, dtype, scratch, name):
    mesh = plsc.VectorSubcoreMesh(core_axis_name="c", subcore_axis_name="s",
                                  num_cores=SC_CORES, num_subcores=SC_SUBCORES)
    return pl.kernel(body, mesh=mesh, out_type=jax.ShapeDtypeStruct(out_shape, dtype),
                     scratch_types=scratch, name=name)


def _sc_chunks(n):
    workers = SC_CORES * SC_SUBCORES
    chunks = n // (workers * SC_CHUNK)
    assert chunks * workers * SC_CHUNK == n
    return workers, chunks


SC_HALF = SC_CHUNK // 2


def _sc_gather_rows(table, idx):
    workers, chunks = _sc_chunks(idx.shape[0])
    width = table.shape[1]

    def body(table_hbm, idx_hbm, out_hbm, idx_v, rows_v, *sems):
        wid = lax.axis_index("s") * SC_CORES + lax.axis_index("c")
        pltpu.sync_copy(idx_hbm.at[wid], idx_v)

        @pl.loop(0, chunks)
        def _(j):
            pieces = [2 * j, 2 * j + 1]
            reads = [pltpu.async_copy(table_hbm.at[idx_v.at[p]], rows_v.at[b], sems[b])
                     for b, p in enumerate(pieces)]
            writes = []
            for b, p in enumerate(pieces):
                reads[b].wait()
                first = (wid * 2 * chunks + p) * SC_HALF
                writes.append(pltpu.async_copy(rows_v.at[b], out_hbm.at[pl.ds(first, SC_HALF)],
                                               sems[2 + b]))
            for write in writes:
                write.wait()

    scratch = [pltpu.VMEM((2 * chunks, SC_HALF), jnp.int32),
               pltpu.VMEM((2, SC_HALF, width), table.dtype)] + [pltpu.SemaphoreType.DMA] * 4
    call = _sc_rows_kernel(body, (idx.shape[0], width), table.dtype, scratch, "moe_gather_sc")
    return call(table, idx.reshape(workers, 2 * chunks, SC_HALF))


def _sc_scatter_rows(src, dest, n_rows):
    copies, n_src = dest.shape
    workers, chunks = _sc_chunks(n_src)

    def body(src_hbm, idx_hbm, out_hbm, idx_v, rows_v, sem):
        wid = lax.axis_index("s") * SC_CORES + lax.axis_index("c")
        pltpu.sync_copy(idx_hbm.at[wid], idx_v)

        @pl.loop(0, chunks)
        def _(j):
            pltpu.sync_copy(src_hbm.at[pl.ds((wid * chunks + j) * SC_CHUNK, SC_CHUNK)], rows_v)
            started = [pltpu.async_copy(rows_v, out_hbm.at[idx_v.at[j, k]], sem)
                       for k in range(copies)]
            for copy in started:
                copy.wait()

    scratch = [pltpu.VMEM((chunks, copies, SC_CHUNK), jnp.int32),
               pltpu.VMEM((SC_CHUNK, src.shape[1]), src.dtype), pltpu.SemaphoreType.DMA]
    call = _sc_rows_kernel(body, (n_rows, src.shape[1]), src.dtype, scratch, "moe_scatter_sc")
    idx = dest.reshape(copies, workers, chunks, SC_CHUNK).transpose(1, 2, 0, 3)
    return call(src, idx)


def _silu(g):
    return g * jax.nn.sigmoid(g)


SCHED_FIRST, SCHED_SLOT, SCHED_NEXT, SCHED_VALID = 0, 1, 2, 3


def _expert_kernel(sched_ref, n_used_ref, x_ref, wg_hbm, wu_hbm, wd_hbm, o_ref,
                   wg_buf, wu_buf, wd_buf, sem, *, layer, first_expert_row):
    i = pl.program_id(0)
    slot = sched_ref[SCHED_SLOT, i]

    def copies(expert, slot):
        pairs = ((wg_hbm, wg_buf), (wu_hbm, wu_buf), (wd_hbm, wd_buf))
        return [pltpu.make_async_copy(src.at[layer, expert], dst.at[slot], sem.at[slot, n])
                for n, (src, dst) in enumerate(pairs)]

    @pl.when(i == 0)
    def _():
        for c in copies(sched_ref[first_expert_row, 0], 0):
            c.start()

    @pl.when(sched_ref[SCHED_FIRST, i] == 1)
    def _():
        for c in copies(0, slot):
            c.wait()

        @pl.when(sched_ref[SCHED_NEXT, i] >= 0)
        def _():
            for c in copies(sched_ref[SCHED_NEXT, i], 1 - slot):
                c.start()

    @pl.when(i < n_used_ref[0])
    def _():
        words = x_ref[...]
        row = lax.broadcasted_iota(jnp.int32, words.shape, 0)
        words = jnp.where(row < sched_ref[SCHED_VALID, i], words, 0)
        x = jnp.concatenate(_unpack_halves(words), axis=1).astype(BF16)
        g = jnp.dot(x, wg_buf[slot].astype(BF16), preferred_element_type=F32)
        u = jnp.dot(x, wu_buf[slot].astype(BF16), preferred_element_type=F32)
        a = (_silu(g) * u).astype(BF16)
        o_ref[...] = _pack_halves(jnp.dot(a, wd_buf[slot].astype(BF16), preferred_element_type=F32))

    @pl.when(i >= n_used_ref[0])
    def _():
        o_ref[...] = jnp.zeros_like(o_ref)


def _expert_schedule(seg_start, seg_end, counts, n_blocks):
    E = seg_end.shape[0]
    first_row = jnp.arange(n_blocks, dtype=jnp.int32) * EXPERT_ROWS
    blk_e = jnp.minimum(jnp.sum(seg_end[None, :] <= first_row[:, None], axis=1), E - 1)
    used = first_row < seg_end[-1]
    first = used & (first_row == seg_start[blk_e])
    ordinal = jnp.cumsum(first) - 1
    ids = jnp.where(seg_end > seg_start, jnp.arange(E), E)
    later = lax.cummin(ids, reverse=True)
    nxt = jnp.concatenate([later[1:], jnp.full((1,), E, later.dtype)])[blk_e]
    nxt = jnp.where(nxt < E, nxt, -1)
    valid = jnp.clip((seg_start + counts)[blk_e] - first_row, 0, EXPERT_ROWS)
    return jnp.stack([first, ordinal % 2, nxt, valid, blk_e]).astype(jnp.int32)


def _expert_mlp(xs, sched, n_used, w_gate, w_up, w_down, layer):
    n_rows, half = xs.shape
    D = 2 * half
    F = w_gate.shape[-1]
    tm = EXPERT_ROWS
    hbm = pl.BlockSpec(memory_space=pl.ANY)
    return pl.pallas_call(
        functools.partial(_expert_kernel, layer=layer, first_expert_row=sched.shape[0] - 1),
        out_shape=jax.ShapeDtypeStruct((n_rows, half), jnp.int32),
        grid_spec=pltpu.PrefetchScalarGridSpec(
            num_scalar_prefetch=2,
            grid=(n_rows // tm,),
            in_specs=[pl.BlockSpec((tm, half), lambda i, sc, nu: (jnp.minimum(i, nu[0] - 1), 0)),
                      hbm, hbm, hbm],
            out_specs=pl.BlockSpec((tm, half), lambda i, sc, nu: (i, 0)),
            scratch_shapes=[pltpu.VMEM((2, D, F), F32), pltpu.VMEM((2, D, F), F32),
                            pltpu.VMEM((2, F, D), F32), pltpu.SemaphoreType.DMA((2, 3))]),
        compiler_params=_cparams("arbitrary"),
        name="moe_experts",
    )(sched, n_used, xs, w_gate, w_up, w_down)


def _combine_kernel(x_ref, gate_ref, rows_ref, wsg_ref, wsu_ref, wsd_ref, g_ref, b_ref, o_ref,
                    *, alpha):
    x = x_ref[...]
    xb = x.astype(BF16)
    g = jnp.dot(xb, wsg_ref[...], preferred_element_type=F32)
    u = jnp.dot(xb, wsu_ref[...], preferred_element_type=F32)
    y = jnp.dot((_silu(g) * u).astype(BF16), wsd_ref[...], preferred_element_type=F32)
    gates = gate_ref[...]
    half = y.shape[1] // 2
    left, right = y[:, :half], y[:, half:]
    for k in range(TOP_K):
        lo, hi = _unpack_halves(rows_ref[k])
        left = left + lo * gates[:, k:k + 1]
        right = right + hi * gates[:, k:k + 1]
    y = jnp.concatenate([left, right], axis=1)
    o_ref[...] = _layer_norm(alpha * x + y, g_ref[...], b_ref[...])


def _combine(x2d, ys, dest, gates_tok, ws_gate, ws_up, ws_down, g, b, alpha, tt):
    T, D = x2d.shape
    F = ws_gate.shape[1]
    row = lambda i: (i, 0)
    fixed = lambda i: (0, 0)
    rows = _sc_gather_rows(ys, dest.reshape(-1)).reshape(TOP_K, T, D // 2)
    return pl.pallas_call(
        functools.partial(_combine_kernel, alpha=alpha),
        out_shape=jax.ShapeDtypeStruct((T, D), F32),
        grid=(T // tt,),
        in_specs=[pl.BlockSpec((tt, D), row),
                  pl.BlockSpec((tt, LANES), row),
                  pl.BlockSpec((TOP_K, tt, D // 2), lambda i: (0, i, 0)),
                  pl.BlockSpec((D, F), fixed), pl.BlockSpec((D, F), fixed),
                  pl.BlockSpec((F, D), fixed),
                  pl.BlockSpec((1, D), fixed), pl.BlockSpec((1, D), fixed)],
        out_specs=pl.BlockSpec((tt, D), row),
        compiler_params=_cparams("parallel"),
        name="moe_combine",
    )(x2d, gates_tok, rows, ws_gate.astype(BF16), ws_up.astype(BF16), ws_down.astype(BF16),
      g.reshape(1, D), b.reshape(1, D))


def _pick(n, pref):
    t = min(n, pref)
    while n % t:
        t //= 2
    return t


def _moe_layer(x2d, packed, w_router, router_bias, w_gate, w_up, w_down, layer, ws_gate, ws_up,
               ws_down, g, b, alpha):
    T, D = x2d.shape
    E = w_router.shape[1]
    idx, gates, rank, counts = _router(x2d, w_router, router_bias, _pick(T, 512))
    counts = counts[:, 0].astype(jnp.int32)
    padded = (counts + EXPERT_ROWS - 1) // EXPERT_ROWS * EXPERT_ROWS
    seg_end = jnp.cumsum(padded)
    seg_start = seg_end - padded
    n_blocks = (T * TOP_K + E * (EXPERT_ROWS - 1)) // EXPERT_ROWS
    dest = _slots(idx, rank, seg_start, _pick(T, 1024))
    sched = _expert_schedule(seg_start, seg_end, counts, n_blocks)
    n_used = (seg_end[-1:] // EXPERT_ROWS).astype(jnp.int32)
    xs = _sc_scatter_rows(packed, dest, n_blocks * EXPERT_ROWS)
    ys = _expert_mlp(xs, sched, n_used, w_gate, w_up, w_down, layer)
    return _combine(x2d, ys, dest, gates, ws_gate, ws_up, ws_down, g, b, alpha, _pick(T, 256))


def _mixer_layer(x2d, i, B, S, p, alpha):
    T, D = x2d.shape
    tm = _pick(T, 512)
    tn = _pick(D, 1024)
    j = i // 2
    w_in = p["moba_w_in"][j] if i % 2 == 0 else p["fox_w_in"][j]
    w_qv_t = jnp.concatenate([w_in[:, :D].T * (HEAD_DIM ** -0.5 * LOG2E),
                              w_in[:, 2 * D:3 * D].T]).astype(BF16)
    qv_t = _project_t(x2d, w_qv_t, tm, tn)
    w_k = w_in[:, D:2 * D].astype(BF16)
    if i % 2 == 0:
        k_aug, k_mean = _moba_keys(x2d, w_k, B, S, tm)
        attn_t = _moba_attention(qv_t, k_aug, k_mean, p["rel_bias"], B, S)
        w_out = p["moba_w_out"][j]
    else:
        c_tok = _fox_gates(x2d.reshape(B, S, D), w_in[:, 3 * D:], p["fox_b_f"][j], _pick(S, 512))
        k_aug = _fox_keys(x2d, w_k, c_tok.reshape(T, LANES), tm).reshape(HEADS_PER_TILE, B, S, D)
        attn_t = _fox_attention(qv_t, k_aug, B, S, _pick(S, 256))
        w_out = p["fox_w_out"][j]
    return _outproj_ln(attn_t, w_out.astype(BF16), x2d, p["ln1_g"][i], p["ln1_b"][i], alpha, tm)


def kernel(x, rel_bias, moba_w_in, moba_w_out, fox_w_in, fox_b_f, fox_w_out, ln1_g, ln1_b, ln2_g,
           ln2_b, w_router, router_bias, w_gate, w_up, w_down, ws_gate, ws_up, ws_down):
    B, S, D = x.shape
    depth = ln1_g.shape[0]
    alpha = (2 * depth) ** 0.25
    p = dict(rel_bias=rel_bias, moba_w_in=moba_w_in, moba_w_out=moba_w_out, fox_w_in=fox_w_in,
             fox_b_f=fox_b_f, fox_w_out=fox_w_out, ln1_g=ln1_g, ln1_b=ln1_b)
    x2d = x.reshape(B * S, D)
    for i in range(depth):
        x2d, packed = _mixer_layer(x2d, i, B, S, p, alpha)
        x2d = _moe_layer(x2d, packed, w_router[i], router_bias[i], w_gate, w_up, w_down, i,
                         ws_gate[i], ws_up[i], ws_down[i], ln2_g[i], ln2_b[i], alpha)
    return x2d.reshape(B, S, D)
```

```python
import functools
import math

import jax
import jax.numpy as jnp
from jax import lax
from jax.experimental import pallas as pl
from jax.experimental.pallas import tpu as pltpu
from jax.experimental.pallas import tpu_sc as plsc

F32 = jnp.float32
BF16 = jnp.bfloat16
HIGHEST = lax.Precision.HIGHEST

HEAD_DIM = 64
MOBA_BLOCK = 256
MOBA_TOPK = 3
REL_MAX_DIST = 128
TOP_K = 8
N_GROUPS = 8
TOPK_GROUPS = 4
ROUTED_SCALE = 2.5
LN_EPS = 1e-5
NEG = -1e30
LOG2E = math.log2(math.e)

LANES = 128
HEADS_PER_TILE = LANES // HEAD_DIM
EXPERT_ROWS = 512
VMEM_LIMIT = 48 * 1024 * 1024


def _cparams(*sem):
    return pltpu.CompilerParams(dimension_semantics=sem, vmem_limit_bytes=VMEM_LIMIT)


def _nt_dot(a, b, **kw):
    return lax.dot_general(a, b, (((1,), (1,)), ((), ())), preferred_element_type=F32, **kw)


def _split_bf16(x, pieces):
    out = []
    for _ in range(pieces):
        out.append(x.astype(BF16))
        x = x - out[-1].astype(F32)
    return out


def _three_pass(x, w_ref, mul):
    hi, lo = _split_bf16(x, 2)
    return mul(hi, w_ref[0]) + mul(lo, w_ref[0]) + mul(hi, w_ref[1])


def _proj_t_kernel(x_ref, wt_ref, o_ref):
    o_ref[...] = _nt_dot(wt_ref[...], x_ref[...].astype(BF16)).astype(o_ref.dtype)


def _project_t(x2d, wt, tm, tn):
    T, K = x2d.shape
    N = wt.shape[0]
    return pl.pallas_call(
        _proj_t_kernel,
        out_shape=jax.ShapeDtypeStruct((N, T), BF16),
        grid=(T // tm, N // tn),
        in_specs=[pl.BlockSpec((tm, K), lambda i, j: (i, 0)),
                  pl.BlockSpec((tn, K), lambda i, j: (j, 0))],
        out_specs=pl.BlockSpec((tn, tm), lambda i, j: (j, i)),
        compiler_params=_cparams("parallel", "arbitrary"),
        name="qv_proj_t",
    )(x2d, wt)


def _layer_norm(r, g, b):
    mu = jnp.mean(r, axis=-1, keepdims=True)
    d = r - mu
    var = jnp.mean(d * d, axis=-1, keepdims=True)
    return d * lax.rsqrt(var + LN_EPS) * g + b


def _pack_halves(y):
    n = y.shape[1] // 2
    bits = lambda v: lax.bitcast_convert_type(v.astype(BF16).astype(F32), jnp.int32)
    left = bits(y[:, :n])
    return lax.shift_right_logical(left, jnp.full_like(left, 16)) | bits(y[:, n:])


def _unpack_halves(p):
    return (lax.bitcast_convert_type(p << 16, F32),
            lax.bitcast_convert_type(p & jnp.int32(-65536), F32))


def _outproj_ln_kernel(at_ref, w_ref, x_ref, g_ref, b_ref, o_ref, packed_ref, *, alpha):
    y = lax.dot_general(at_ref[...], w_ref[...], (((0,), (0,)), ((), ())),
                        preferred_element_type=F32)
    out = _layer_norm(alpha * x_ref[...] + y, g_ref[...], b_ref[...])
    o_ref[...] = out
    packed_ref[...] = _pack_halves(out)


def _outproj_ln(attn_t, w_out, x2d, g, b, alpha, tm):
    T, D = x2d.shape
    row = lambda i: (i, 0)
    fixed = lambda i: (0, 0)
    return pl.pallas_call(
        functools.partial(_outproj_ln_kernel, alpha=alpha),
        out_shape=(jax.ShapeDtypeStruct((T, D), F32), jax.ShapeDtypeStruct((T, D // 2), jnp.int32)),
        grid=(T // tm,),
        in_specs=[pl.BlockSpec((D, tm), lambda i: (0, i)), pl.BlockSpec((D, D), fixed),
                  pl.BlockSpec((tm, D), row), pl.BlockSpec((1, D), fixed),
                  pl.BlockSpec((1, D), fixed)],
        out_specs=(pl.BlockSpec((tm, D), row), pl.BlockSpec((tm, D // 2), row)),
        compiler_params=_cparams("parallel"),
        name="outproj_ln",
    )(attn_t, w_out, x2d, g.reshape(1, D), b.reshape(1, D))


def _spare(a, n=0):
    return ((a + 1) % HEADS_PER_TILE) * HEAD_DIM + n


def _scores(chains):
    return [[jnp.dot(keys(), query(), preferred_element_type=F32) for keys, query, _ in blocks]
            for blocks in chains]


def _absorb(carries, scores, chains, masks=None, adds=None):
    probs = []
    for c, ((m, _), parts) in enumerate(zip(carries, scores)):
        if adds is not None:
            parts = [s if add is None else s + add() for add, s in zip(adds[c], parts)]
        if masks is not None:
            parts = [s if keep is None else jnp.where(keep, s, NEG)
                     for keep, s in zip(masks[c], parts)]
        tops = [jnp.max(s, axis=0, keepdims=True) for s in parts]
        m_new = functools.reduce(jnp.maximum, tops, m)
        probs.append((m_new, jnp.exp2(m - m_new),
                      [jnp.exp2(s - m_new).astype(BF16) for s in parts]))
    out = []
    for (_, acc), (m_new, decay, ps), blocks in zip(carries, probs, chains):
        acc = decay * acc
        for (_, _, values), p in zip(blocks, ps):
            acc = acc + jnp.dot(values(), p, preferred_element_type=F32)
        out.append((m_new, acc))
    return tuple(out)


def _stash(chains, ref):
    for a, parts in enumerate(_scores(chains)):
        for b, s in enumerate(parts):
            ref[a, b] = s


def _fetch(chains, ref):
    return [[ref[a, b] for b in range(len(blocks))] for a, blocks in enumerate(chains)]


def _attend_masked_last(carries, n, group, stage_refs, masks, adds=None):
    _stash(group(0), stage_refs[0])

    def run(carries, start, count, last_is_masked):
        for q in range(count):
            g = start + q
            if q + 1 < count or not last_is_masked:
                _stash(group(g + 1), stage_refs[(q + 1) % 2])
            special = last_is_masked and q + 1 == count
            carries = _absorb(carries, _fetch(group(g), stage_refs[q % 2]), group(g),
                              masks if special else None, adds if special else None)
        return carries

    trips = n // TRIP_GROUPS
    carries = lax.fori_loop(0, trips, lambda t, c: run(c, TRIP_GROUPS * t, TRIP_GROUPS, False),
                            carries)
    tails = [functools.partial(run, start=n - r, count=r + 1, last_is_masked=True)
             for r in range(TRIP_GROUPS)]
    return lax.switch(n % TRIP_GROUPS, tails, carries)


GROUP_BLOCKS = 2
TRIP_GROUPS = 4


def _stage_scratch(tq, chains=HEADS_PER_TILE):
    return [pltpu.VMEM((chains, GROUP_BLOCKS, tq, tq), F32) for _ in range(2)]


def _scores_init(tq):
    return jnp.full((1, tq), -jnp.inf, F32), jnp.zeros((LANES, tq), F32)


def _finish(carries, o_ref, cols=slice(None)):
    row = lax.broadcasted_iota(jnp.int32, carries[0][1].shape, 0)
    outs = []
    for a, (_, acc) in enumerate(carries):
        s = _spare(a)
        outs.append(acc / acc[s:s + 1, :])
    o_ref[:, cols] = jnp.where(row < HEAD_DIM, outs[0], outs[1]).astype(o_ref.dtype)


def _fill_v_aug(vt_ref, vaug_ref, tk):
    n_tiles = vt_ref.shape[1] // tk
    row = lax.broadcasted_iota(jnp.int32, (LANES, tk), 0)
    for n in range(n_tiles):
        v = vt_ref[:, n * tk:(n + 1) * tk]
        for a in range(HEADS_PER_TILE):
            vaug_ref[a, n] = jnp.where(row // HEAD_DIM == a, v, jnp.ones_like(v))


FOX_SPLIT = 3


def _fox_kernel(qt_ref, k_ref, vt_ref, o_ref, vaug_ref, stage0_ref, stage1_ref, *, tq):
    n = pl.program_id(2)

    @pl.when(n == 0)
    def _():
        _fill_v_aug(vt_ref, vaug_ref, tq)

    chains = [(h, a) for h in range(GROUP_BLOCKS) for a in range(HEADS_PER_TILE)]
    row = lax.broadcasted_iota(jnp.int32, (LANES, tq), 0)
    queries = {}
    for h, a in chains:
        offs = (row >= _spare(a)) & (row < _spare(a, FOX_SPLIT))
        queries[h, a] = jnp.where(row // HEAD_DIM == a, qt_ref[:, h * tq:(h + 1) * tq],
                                  jnp.where(offs, -1.0, 0.0).astype(BF16))

    def block(h, a, j):
        start = pl.multiple_of(j * tq, tq)
        return (lambda: k_ref[a, 0, pl.ds(start, tq), :], lambda: queries[h, a],
                lambda: vaug_ref[a, j])

    key = lax.broadcasted_iota(jnp.int32, (tq, tq), 0)
    qry = lax.broadcasted_iota(jnp.int32, (tq, tq), 1)
    masks = [[None if b < h else key + (b - h) * tq <= qry for b in range(GROUP_BLOCKS)]
             for h, _ in chains]
    carries = _attend_masked_last(
        tuple(_scores_init(tq) for _ in chains), n,
        lambda j: [[block(h, a, 2 * j), block(h, a, 2 * j + 1)] for h, a in chains],
        (stage0_ref, stage1_ref), masks)
    for h in range(GROUP_BLOCKS):
        _finish(carries[h * HEADS_PER_TILE:(h + 1) * HEADS_PER_TILE], o_ref,
                slice(h * tq, (h + 1) * tq))


def _fox_attention(qv_t, k_aug, B, S, tq):
    D = k_aug.shape[-1]
    n_tiles = D // LANES
    nq = S // tq
    span = GROUP_BLOCKS * tq
    steps = S // span
    return pl.pallas_call(
        functools.partial(_fox_kernel, tq=tq),
        out_shape=jax.ShapeDtypeStruct((D, B * S), BF16),
        grid=(B, n_tiles, steps),
        in_specs=[pl.BlockSpec((LANES, span), lambda b, h, i: (h, b * steps + i)),
                  pl.BlockSpec((HEADS_PER_TILE, 1, S, LANES), lambda b, h, i: (0, b, 0, h)),
                  pl.BlockSpec((LANES, S), lambda b, h, i: (n_tiles + h, b))],
        out_specs=pl.BlockSpec((LANES, span), lambda b, h, i: (h, b * steps + i)),
        scratch_shapes=[pltpu.VMEM((HEADS_PER_TILE, nq, LANES, tq), BF16)]
        + _stage_scratch(tq, GROUP_BLOCKS * HEADS_PER_TILE),
        compiler_params=_cparams("parallel", "parallel", "arbitrary"),
        name="fox_attention",
    )(qv_t, k_aug, qv_t)


def _log_sigmoid(z):
    return jnp.minimum(z, 0.0) - jnp.log1p(jnp.exp(-jnp.abs(z)))


def _fox_gates_kernel(x_ref, w_ref, b_ref, o_ref, carry_ref, *, ts):
    @pl.when(pl.program_id(1) == 0)
    def _():
        carry_ref[...] = jnp.zeros_like(carry_ref)

    z = _three_pass(x_ref[0], w_ref, lambda x, w: jnp.dot(x, w, preferred_element_type=F32))
    lf = _log_sigmoid(z + b_ref[...])
    r = lax.broadcasted_iota(jnp.int32, (ts, ts), 0)
    c = lax.broadcasted_iota(jnp.int32, (ts, ts), 1)
    tri = (c <= r).astype(BF16)
    cum = carry_ref[...]
    for piece in _split_bf16(lf, 3):
        cum = cum + jnp.dot(tri, piece, preferred_element_type=F32)
    carry_ref[...] = cum[ts - 1:ts, :]
    o_ref[0] = cum


def _fox_gates(x3d, w_f, b_f, ts):
    B, S, D = x3d.shape
    H = w_f.shape[1]
    w_pad = jnp.stack(_split_bf16(jnp.zeros((D, LANES), F32).at[:, :H].set(w_f), 2))
    b_pad = jnp.zeros((1, LANES), F32).at[0, :H].set(b_f)
    return pl.pallas_call(
        functools.partial(_fox_gates_kernel, ts=ts),
        out_shape=jax.ShapeDtypeStruct((B, S, LANES), F32),
        grid=(B, S // ts),
        in_specs=[pl.BlockSpec((1, ts, D), lambda b, s: (b, s, 0)),
                  pl.BlockSpec((2, D, LANES), lambda b, s: (0, 0, 0)),
                  pl.BlockSpec((1, LANES), lambda b, s: (0, 0))],
        out_specs=pl.BlockSpec((1, ts, LANES), lambda b, s: (b, s, 0)),
        scratch_shapes=[pltpu.VMEM((1, LANES), F32)],
        compiler_params=_cparams("parallel", "arbitrary"),
        name="fox_gates",
    )(x3d, w_pad, b_pad)


def _fox_keys_kernel(x_ref, w_ref, c_ref, o_ref):
    keys = jnp.dot(x_ref[...].astype(BF16), w_ref[...], preferred_element_type=F32)
    tm, D = keys.shape
    lane = lax.broadcasted_iota(jnp.int32, (tm, LANES), 1)
    for t in range(D // LANES):
        k = keys[:, t * LANES:(t + 1) * LANES]
        for a in range(HEADS_PER_TILE):
            h = t * HEADS_PER_TILE + a
            rest = c_ref[:, h:h + 1] * LOG2E
            aug = jnp.zeros((tm, LANES), F32)
            for n in range(FOX_SPLIT):
                piece = rest.astype(BF16).astype(F32)
                rest = rest - piece
                aug = jnp.where(lane == _spare(a, n), piece, aug)
            o_ref[a, :, t * LANES:(t + 1) * LANES] = jnp.where(lane // HEAD_DIM == a, k,
                                                               aug).astype(BF16)


def _fox_keys(x2d, w_k, c_tok, tm):
    T, K = x2d.shape
    D = w_k.shape[1]
    return pl.pallas_call(
        _fox_keys_kernel,
        out_shape=jax.ShapeDtypeStruct((HEADS_PER_TILE, T, D), BF16),
        grid=(T // tm,),
        in_specs=[pl.BlockSpec((tm, K), lambda i: (i, 0)),
                  pl.BlockSpec((K, D), lambda i: (0, 0)),
                  pl.BlockSpec((tm, LANES), lambda i: (i, 0))],
        out_specs=pl.BlockSpec((HEADS_PER_TILE, tm, D), lambda i: (0, i, 0)),
        compiler_params=_cparams("parallel"),
        name="fox_keys",
    )(x2d, w_k, c_tok)


MOBA_SLAB = 16
KMEAN_SPLIT = 3


def _moba_kernel(bfar_ref, qt_ref, k_ref, vt_ref, kmean_ref, bown_ref, badj_ref, o_ref,
                 vaug_ref, sel_ref, stage0_ref, stage1_ref, *, blk):
    t = pl.program_id(1)
    n = pl.program_id(2)
    nb = kmean_ref.shape[2]

    @pl.when(n == 0)
    def _():
        _fill_v_aug(vt_ref, vaug_ref, blk)

    chains = [(h, a) for h in range(GROUP_BLOCKS) for a in range(HEADS_PER_TILE)]
    row = lax.broadcasted_iota(jnp.int32, (LANES, blk), 0)
    brow = lax.broadcasted_iota(jnp.int32, (nb, blk), 0)
    base = {}
    for c, (h, a) in enumerate(chains):
        own = GROUP_BLOCKS * n + h
        qt = qt_ref[:, h * blk:(h + 1) * blk]
        qa = jnp.where(row // HEAD_DIM == a, qt, jnp.zeros_like(qt))
        base[h, a] = qa
        bscore = sum(jnp.dot(kmean_ref[piece, 0], qa, preferred_element_type=F32)
                     for piece in range(KMEAN_SPLIT))
        cand = jnp.where(brow < own, bscore, NEG)
        sel = jnp.zeros((nb, blk), F32)
        for _ in range(MOBA_TOPK):
            mx = jnp.max(cand, axis=0, keepdims=True)
            first = jnp.min(jnp.where(cand == mx, brow, nb), axis=0, keepdims=True)
            hit = brow == first
            sel = jnp.where(hit, 1.0, sel)
            cand = jnp.where(hit, -jnp.inf, cand)
        sel_ref[c] = jnp.where(brow < own, sel, 0.0)

    srow = lax.broadcasted_iota(jnp.int32, (MOBA_SLAB, blk), 0)

    def block(c, j):
        h, a = chains[c]
        head_id = t * HEADS_PER_TILE + a
        jc = jnp.maximum(j, 0)
        start = pl.multiple_of(jc * blk, blk)

        def query():
            s0 = _spare(a) // MOBA_SLAB * MOBA_SLAB
            chosen = (((sel_ref[c, pl.ds(jc, 1), :] > 0.0) & (j >= 0))
                      | (j == GROUP_BLOCKS * n + h))
            slab = jnp.where(srow == _spare(a) - s0,
                             jnp.where(chosen, bfar_ref[0, head_id], NEG),
                             jnp.where(srow == _spare(a, 1) - s0, bfar_ref[1, head_id], 0.0)
                             ).astype(BF16)
            q = base[h, a]
            head = [q[:s0]] if s0 else []
            tail = [q[s0 + MOBA_SLAB:]] if s0 + MOBA_SLAB < LANES else []
            return jnp.concatenate(head + [slab] + tail, axis=0)

        return lambda: k_ref[a, 0, pl.ds(start, blk), :], query, lambda: vaug_ref[a, jc]

    key = lax.broadcasted_iota(jnp.int32, (blk, blk), 0)
    qry = lax.broadcasted_iota(jnp.int32, (blk, blk), 1)
    carries = _attend_masked_last(
        tuple(_scores_init(blk) for _ in chains), n,
        lambda g: [[block(c, 2 * g + h - 1), block(c, 2 * g + h)] for c, (h, _) in enumerate(chains)],
        (stage0_ref, stage1_ref), [[None, key <= qry] for _ in chains],
        [[lambda a=a: badj_ref[a], lambda a=a: bown_ref[a]] for _, a in chains])
    for h in range(GROUP_BLOCKS):
        _finish(carries[h * HEADS_PER_TILE:(h + 1) * HEADS_PER_TILE], o_ref,
                slice(h * blk, (h + 1) * blk))


def _rel_bucket(dist, n_buckets):
    n = jnp.maximum(dist, 0)
    max_exact = n_buckets // 2
    nf = jnp.maximum(n, 1).astype(F32)
    large = max_exact + (jnp.log(nf / max_exact) / math.log(REL_MAX_DIST / max_exact)
                         * (n_buckets - max_exact)).astype(jnp.int32)
    large = jnp.minimum(large, n_buckets - 1)
    return jnp.where(n < max_exact, n, large)


def _moba_attention(qv_t, k_aug, k_mean, rel_bias, B, S):
    D = k_aug.shape[-1]
    n_tiles = D // LANES
    blk = MOBA_BLOCK
    nb = S // blk
    hp = HEADS_PER_TILE
    span = GROUP_BLOCKS * blk
    steps = S // span
    n_chains = GROUP_BLOCKS * hp
    r = jnp.arange(blk)
    delta = r[None, :] - r[:, None]
    bias_t = rel_bias.T.astype(F32)
    n_buckets = rel_bias.shape[0]
    def table(dist):
        onehot = jax.nn.one_hot(_rel_bucket(dist, n_buckets), n_buckets, dtype=F32)
        return jnp.einsum('crn,hn->hcr', onehot, bias_t, precision=HIGHEST)

    b_far = bias_t[:, n_buckets - 1] * LOG2E
    b_own = table(delta) * LOG2E - b_far[:, None, None]
    b_adj = table(delta + blk) * LOG2E - b_far[:, None, None]
    far_hi = b_far.astype(BF16).astype(F32)
    far_lo = (b_far - far_hi).astype(BF16).astype(F32)
    b_far2 = jnp.stack([far_hi, far_lo])
    return pl.pallas_call(
        functools.partial(_moba_kernel, blk=blk),
        out_shape=jax.ShapeDtypeStruct((D, B * S), BF16),
        grid_spec=pltpu.PrefetchScalarGridSpec(
            num_scalar_prefetch=1,
            grid=(B, n_tiles, steps),
            in_specs=[pl.BlockSpec((LANES, span), lambda b, h, i, f: (h, b * steps + i)),
                      pl.BlockSpec((hp, 1, S, LANES), lambda b, h, i, f: (0, b, 0, h)),
                      pl.BlockSpec((LANES, S), lambda b, h, i, f: (n_tiles + h, b)),
                      pl.BlockSpec((KMEAN_SPLIT, 1, nb, LANES), lambda b, h, i, f: (0, b, 0, h)),
                      pl.BlockSpec((hp, blk, blk), lambda b, h, i, f: (h, 0, 0)),
                      pl.BlockSpec((hp, blk, blk), lambda b, h, i, f: (h, 0, 0))],
            out_specs=pl.BlockSpec((LANES, span), lambda b, h, i, f: (h, b * steps + i)),
            scratch_shapes=[pltpu.VMEM((hp, nb, LANES, blk), BF16),
                            pltpu.VMEM((n_chains, nb, blk), F32)] + _stage_scratch(blk, n_chains)),
        compiler_params=_cparams("parallel", "parallel", "arbitrary"),
        name="moba_attention",
    )(b_far2, qv_t, k_aug, qv_t, k_mean, b_own, b_adj)


def _moba_keys_kernel(x_ref, w_ref, o_ref, mean_ref):
    k = jnp.dot(x_ref[...].astype(BF16), w_ref[...], preferred_element_type=F32)
    col = lax.broadcasted_iota(jnp.int32, k.shape, 1)
    lane = col % LANES
    for a in range(HEADS_PER_TILE):
        ones = (lane == _spare(a)) | (lane == _spare(a, 1))
        o_ref[a] = jnp.where((col // HEAD_DIM) % HEADS_PER_TILE == a, k,
                             jnp.where(ones, 1.0, 0.0)).astype(BF16)
    for n in range(mean_ref.shape[0]):
        mean_ref[n] = jnp.mean(k[n * MOBA_BLOCK:(n + 1) * MOBA_BLOCK], axis=0, keepdims=True)


def _moba_keys(x2d, w_k, B, S, tm):
    T, K = x2d.shape
    D = w_k.shape[1]
    nb = S // MOBA_BLOCK
    per_tile = tm // MOBA_BLOCK
    k_aug, k_mean = pl.pallas_call(
        _moba_keys_kernel,
        out_shape=(jax.ShapeDtypeStruct((HEADS_PER_TILE, T, D), BF16),
                   jax.ShapeDtypeStruct((B * nb, 1, D), F32)),
        grid=(T // tm,),
        in_specs=[pl.BlockSpec((tm, K), lambda i: (i, 0)),
                  pl.BlockSpec((K, D), lambda i: (0, 0))],
        out_specs=(pl.BlockSpec((HEADS_PER_TILE, tm, D), lambda i: (0, i, 0)),
                   pl.BlockSpec((per_tile, 1, D), lambda i: (i, 0, 0))),
        compiler_params=_cparams("parallel"),
        name="moba_keys",
    )(x2d, w_k)
    k_mean = jnp.stack(_split_bf16(k_mean.reshape(B, nb, D), KMEAN_SPLIT))
    return k_aug.reshape(HEADS_PER_TILE, B, S, D), k_mean


def _router_kernel(x_ref, wt_ref, bias_ref, idx_ref, gate_ref, rank_ref, cnt_ref, run_ref, *, tm):
    @pl.when(pl.program_id(0) == 0)
    def _():
        run_ref[...] = jnp.zeros_like(run_ref)

    E = wt_ref.shape[1]
    gsz = E // N_GROUPS
    logits = _three_pass(x_ref[...], wt_ref, lambda x, w: _nt_dot(w, x))
    s = jax.nn.sigmoid(logits)
    sb = s + bias_ref[...]
    neg_inf = -jnp.inf

    giota = lax.broadcasted_iota(jnp.int32, (gsz, tm), 0)
    gscore = []
    for g in range(N_GROUPS):
        blk = sb[g * gsz:(g + 1) * gsz, :]
        m1 = jnp.max(blk, axis=0, keepdims=True)
        i1 = jnp.min(jnp.where(blk == m1, giota, gsz), axis=0, keepdims=True)
        m2 = jnp.max(jnp.where(giota == i1, neg_inf, blk), axis=0, keepdims=True)
        gscore.append(m1 + m2)

    gsel = [jnp.zeros((1, tm), jnp.bool_) for _ in range(N_GROUPS)]
    for _ in range(TOPK_GROUPS):
        mx = functools.reduce(jnp.maximum, gscore)
        found = jnp.zeros((1, tm), jnp.bool_)
        for g in range(N_GROUPS):
            hit = (gscore[g] == mx) & jnp.logical_not(found)
            gsel[g] = gsel[g] | hit
            found = found | hit
            gscore[g] = jnp.where(hit, neg_inf, gscore[g])
    emask = jnp.concatenate([jnp.broadcast_to(gsel[g], (gsz, tm)) for g in range(N_GROUPS)], axis=0)
    cand = jnp.where(emask, sb, NEG)

    eiota = lax.broadcasted_iota(jnp.int32, (E, tm), 0)
    hits, idxs, ws = [], [], []
    for _ in range(TOP_K):
        mx = jnp.max(cand, axis=0, keepdims=True)
        first = jnp.min(jnp.where(cand == mx, eiota, E), axis=0, keepdims=True)
        hit = eiota == first
        hits.append(hit)
        idxs.append(first)
        ws.append(jnp.sum(jnp.where(hit, s, 0.0), axis=0, keepdims=True))
        cand = jnp.where(hit, neg_inf, cand)
    wsum = functools.reduce(jnp.add, ws)

    chosen = functools.reduce(jnp.logical_or, hits)
    onehot = jnp.where(chosen, 1.0, 0.0)
    tr = lax.broadcasted_iota(jnp.int32, (tm, tm), 0)
    tc = lax.broadcasted_iota(jnp.int32, (tm, tm), 1)
    before = (tr < tc).astype(BF16)
    prior = jnp.dot(onehot.astype(BF16), before, preferred_element_type=F32) + run_ref[...]
    grow = lax.broadcasted_iota(jnp.int32, (LANES, tm), 0)
    gates = jnp.zeros((LANES, tm), F32)
    for k in range(TOP_K):
        idx_ref[k:k + 1, :] = idxs[k]
        gates = jnp.where(grow == k, ws[k] / wsum * ROUTED_SCALE, gates)
        rank_ref[k:k + 1, :] = jnp.sum(jnp.where(hits[k], prior, 0.0), axis=0,
                                       keepdims=True).astype(jnp.int32)
    gate_ref[...] = gates.T
    run_ref[...] = run_ref[...] + jnp.sum(onehot, axis=1, keepdims=True)
    cnt_ref[...] = run_ref[...]


def _router(x2d, w_router, router_bias, tm):
    T, D = x2d.shape
    E = w_router.shape[1]
    tok = lambda i: (0, i)
    fixed = lambda i: (0, 0)
    return pl.pallas_call(
        functools.partial(_router_kernel, tm=tm),
        out_shape=(jax.ShapeDtypeStruct((TOP_K, T), jnp.int32),
                   jax.ShapeDtypeStruct((T, LANES), F32),
                   jax.ShapeDtypeStruct((TOP_K, T), jnp.int32),
                   jax.ShapeDtypeStruct((E, 1), F32)),
        grid=(T // tm,),
        in_specs=[pl.BlockSpec((tm, D), lambda i: (i, 0)),
                  pl.BlockSpec((2, E, D), lambda i: (0, 0, 0)),
                  pl.BlockSpec((E, 1), fixed)],
        out_specs=(pl.BlockSpec((TOP_K, tm), tok), pl.BlockSpec((tm, LANES), lambda i: (i, 0)),
                   pl.BlockSpec((TOP_K, tm), tok), pl.BlockSpec((E, 1), fixed)),
        scratch_shapes=[pltpu.VMEM((E, 1), F32)],
        compiler_params=_cparams("arbitrary"),
        name="router",
    )(x2d, jnp.stack(_split_bf16(w_router.T, 2)), router_bias.reshape(E, 1).astype(F32))


def _slots_kernel(idx_ref, rank_ref, start_ref, o_ref):
    E = start_ref.shape[0]
    tm = idx_ref.shape[1]
    eiota = lax.broadcasted_iota(jnp.int32, (E, tm), 0)
    start = start_ref[...]
    for k in range(TOP_K):
        base = jnp.sum(jnp.where(eiota == idx_ref[k:k + 1, :], start, 0.0), axis=0, keepdims=True)
        o_ref[k:k + 1, :] = base.astype(jnp.int32) + rank_ref[k:k + 1, :]


def _slots(idx, rank, seg_start, tm):
    T = idx.shape[1]
    E = seg_start.shape[0]
    tok = lambda i: (0, i)
    return pl.pallas_call(
        _slots_kernel,
        out_shape=jax.ShapeDtypeStruct((TOP_K, T), jnp.int32),
        grid=(T // tm,),
        in_specs=[pl.BlockSpec((TOP_K, tm), tok), pl.BlockSpec((TOP_K, tm), tok),
                  pl.BlockSpec((E, 1), lambda i: (0, 0))],
        out_specs=pl.BlockSpec((TOP_K, tm), tok),
        compiler_params=_cparams("parallel"),
        name="moe_slots",
    )(idx, rank, seg_start.astype(F32).reshape(E, 1))


SC_CORES = 2
SC_SUBCORES = 16
SC_CHUNK = 128


def _sc_rows_kernel(body, out_shape, dtype, scratch, name):
    mesh = plsc.VectorSubcoreMesh(core_axis_name="c", subcore_axis_name="s",
                                  num_cores=SC_CORES, num_subcores=SC_SUBCORES)
    return pl.kernel(body, mesh=mesh, out_type=jax.ShapeDtypeStruct(out_shape, dtype),
                     scratch_types=scratch, name=name)


def _sc_chunks(n):
    workers = SC_CORES * SC_SUBCORES
    chunks = n // (workers * SC_CHUNK)
    assert chunks * workers * SC_CHUNK == n
    return workers, chunks


SC_HALF = SC_CHUNK // 2


def _sc_gather_rows(table, idx):
    workers, chunks = _sc_chunks(idx.shape[0])
    width = table.shape[1]

    def body(table_hbm, idx_hbm, out_hbm, idx_v, rows_v, *sems):
        wid = lax.axis_index("s") * SC_CORES + lax.axis_index("c")
        pltpu.sync_copy(idx_hbm.at[wid], idx_v)

        @pl.loop(0, chunks)
        def _(j):
            pieces = [2 * j, 2 * j + 1]
            reads = [pltpu.async_copy(table_hbm.at[idx_v.at[p]], rows_v.at[b], sems[b])
                     for b, p in enumerate(pieces)]
            writes = []
            for b, p in enumerate(pieces):
                reads[b].wait()
                first = (wid * 2 * chunks + p) * SC_HALF
                writes.append(pltpu.async_copy(rows_v.at[b], out_hbm.at[pl.ds(first, SC_HALF)],
                                               sems[2 + b]))
            for write in writes:
                write.wait()

    scratch = [pltpu.VMEM((2 * chunks, SC_HALF), jnp.int32),
               pltpu.VMEM((2, SC_HALF, width), table.dtype)] + [pltpu.SemaphoreType.DMA] * 4
    call = _sc_rows_kernel(body, (idx.shape[0], width), table.dtype, scratch, "moe_gather_sc")
    return call(table, idx.reshape(workers, 2 * chunks, SC_HALF))


def _sc_scatter_rows(src, dest, n_rows):
    copies, n_src = dest.shape
    workers, chunks = _sc_chunks(n_src)

    def body(src_hbm, idx_hbm, out_hbm, idx_v, rows_v, sem):
        wid = lax.axis_index("s") * SC_CORES + lax.axis_index("c")
        pltpu.sync_copy(idx_hbm.at[wid], idx_v)

        @pl.loop(0, chunks)
        def _(j):
            pltpu.sync_copy(src_hbm.at[pl.ds((wid * chunks + j) * SC_CHUNK, SC_CHUNK)], rows_v)
            started = [pltpu.async_copy(rows_v, out_hbm.at[idx_v.at[j, k]], sem)
                       for k in range(copies)]
            for copy in started:
                copy.wait()

    scratch = [pltpu.VMEM((chunks, copies, SC_CHUNK), jnp.int32),
               pltpu.VMEM((SC_CHUNK, src.shape[1]), src.dtype), pltpu.SemaphoreType.DMA]
    call = _sc_rows_kernel(body, (n_rows, src.shape[1]), src.dtype, scratch, "moe_scatter_sc")
    idx = dest.reshape(copies, workers, chunks, SC_CHUNK).transpose(1, 2, 0, 3)
    return call(src, idx)


def _silu(g):
    return g * jax.nn.sigmoid(g)


SCHED_FIRST, SCHED_SLOT, SCHED_NEXT, SCHED_VALID = 0, 1, 2, 3


def _expert_kernel(sched_ref, n_used_ref, x_ref, wg_hbm, wu_hbm, wd_hbm, o_ref,
                   wg_buf, wu_buf, wd_buf, sem, *, layer, first_expert_row):
    i = pl.program_id(0)
    slot = sched_ref[SCHED_SLOT, i]

    def copies(expert, slot):
        pairs = ((wg_hbm, wg_buf), (wu_hbm, wu_buf), (wd_hbm, wd_buf))
        return [pltpu.make_async_copy(src.at[layer, expert], dst.at[slot], sem.at[slot, n])
                for n, (src, dst) in enumerate(pairs)]

    @pl.when(i == 0)
    def _():
        for c in copies(sched_ref[first_expert_row, 0], 0):
            c.start()

    @pl.when(sched_ref[SCHED_FIRST, i] == 1)
    def _():
        for c in copies(0, slot):
            c.wait()

        @pl.when(sched_ref[SCHED_NEXT, i] >= 0)
        def _():
            for c in copies(sched_ref[SCHED_NEXT, i], 1 - slot):
                c.start()

    @pl.when(i < n_used_ref[0])
    def _():
        words = x_ref[...]
        row = lax.broadcasted_iota(jnp.int32, words.shape, 0)
        words = jnp.where(row < sched_ref[SCHED_VALID, i], words, 0)
        x = jnp.concatenate(_unpack_halves(words), axis=1).astype(BF16)
        g = jnp.dot(x, wg_buf[slot].astype(BF16), preferred_element_type=F32)
        u = jnp.dot(x, wu_buf[slot].astype(BF16), preferred_element_type=F32)
        a = (_silu(g) * u).astype(BF16)
        o_ref[...] = _pack_halves(jnp.dot(a, wd_buf[slot].astype(BF16), preferred_element_type=F32))

    @pl.when(i >= n_used_ref[0])
    def _():
        o_ref[...] = jnp.zeros_like(o_ref)


def _expert_schedule(seg_start, seg_end, counts, n_blocks):
    E = seg_end.shape[0]
    first_row = jnp.arange(n_blocks, dtype=jnp.int32) * EXPERT_ROWS
    blk_e = jnp.minimum(jnp.sum(seg_end[None, :] <= first_row[:, None], axis=1), E - 1)
    used = first_row < seg_end[-1]
    first = used & (first_row == seg_start[blk_e])
    ordinal = jnp.cumsum(first) - 1
    ids = jnp.where(seg_end > seg_start, jnp.arange(E), E)
    later = lax.cummin(ids, reverse=True)
    nxt = jnp.concatenate([later[1:], jnp.full((1,), E, later.dtype)])[blk_e]
    nxt = jnp.where(nxt < E, nxt, -1)
    valid = jnp.clip((seg_start + counts)[blk_e] - first_row, 0, EXPERT_ROWS)
    return jnp.stack([first, ordinal % 2, nxt, valid, blk_e]).astype(jnp.int32)


def _expert_mlp(xs, sched, n_used, w_gate, w_up, w_down, layer):
    n_rows, half = xs.shape
    D = 2 * half
    F = w_gate.shape[-1]
    tm = EXPERT_ROWS
    hbm = pl.BlockSpec(memory_space=pl.ANY)
    return pl.pallas_call(
        functools.partial(_expert_kernel, layer=layer, first_expert_row=sched.shape[0] - 1),
        out_shape=jax.ShapeDtypeStruct((n_rows, half), jnp.int32),
        grid_spec=pltpu.PrefetchScalarGridSpec(
            num_scalar_prefetch=2,
            grid=(n_rows // tm,),
            in_specs=[pl.BlockSpec((tm, half), lambda i, sc, nu: (jnp.minimum(i, nu[0] - 1), 0)),
                      hbm, hbm, hbm],
            out_specs=pl.BlockSpec((tm, half), lambda i, sc, nu: (i, 0)),
            scratch_shapes=[pltpu.VMEM((2, D, F), F32), pltpu.VMEM((2, D, F), F32),
                            pltpu.VMEM((2, F, D), F32), pltpu.SemaphoreType.DMA((2, 3))]),
        compiler_params=_cparams("arbitrary"),
        name="moe_experts",
    )(sched, n_used, xs, w_gate, w_up, w_down)


COMBINE_PARTS = 2


def _combine_kernel(x_ref, gate_ref, rows_ref, wsg_ref, wsu_ref, wsd_ref, g_ref, b_ref, *rest,
                    alpha):
    o_ref = rest[-1]
    x = x_ref[...]
    xb = x.astype(BF16)
    g = jnp.dot(xb, wsg_ref[...], preferred_element_type=F32)
    u = jnp.dot(xb, wsu_ref[...], preferred_element_type=F32)
    y = jnp.dot((_silu(g) * u).astype(BF16), wsd_ref[...], preferred_element_type=F32)
    gates = gate_ref[...]
    half = y.shape[1] // 2
    left, right = y[:, :half], y[:, half:]
    for k in range(TOP_K):
        lo, hi = _unpack_halves(rows_ref[k])
        left = left + lo * gates[:, k:k + 1]
        right = right + hi * gates[:, k:k + 1]
    y = jnp.concatenate([left, right], axis=1)
    o_ref[...] = _layer_norm(alpha * x + y, g_ref[...], b_ref[...])


def _combine(x2d, ys, dest, gates_tok, ws_gate, ws_up, ws_down, g, b, alpha, tt):
    T, D = x2d.shape
    F = ws_gate.shape[1]
    fixed = lambda i: (0, 0)
    part = T // COMBINE_PARTS
    steps = part // tt
    weights = (ws_gate.astype(BF16), ws_up.astype(BF16), ws_down.astype(BF16),
               g.reshape(1, D), b.reshape(1, D))
    out = None
    for p in range(COMBINE_PARTS):
        rows = _sc_gather_rows(ys, dest[:, p * part:(p + 1) * part].reshape(-1))
        rows = rows.reshape(TOP_K, part, D // 2)
        tokens = lambda i, p=p: (p * steps + i, 0)
        in_specs = [pl.BlockSpec((tt, D), tokens),
                    pl.BlockSpec((tt, LANES), tokens),
                    pl.BlockSpec((TOP_K, tt, D // 2), lambda i: (0, i, 0)),
                    pl.BlockSpec((D, F), fixed), pl.BlockSpec((D, F), fixed),
                    pl.BlockSpec((F, D), fixed),
                    pl.BlockSpec((1, D), fixed), pl.BlockSpec((1, D), fixed)]
        args = (x2d, gates_tok, rows) + weights
        aliases = {}
        if out is not None:
            in_specs.append(pl.BlockSpec(memory_space=pl.ANY))
            aliases = {len(args): 0}
            args += (out,)
        out = pl.pallas_call(
            functools.partial(_combine_kernel, alpha=alpha),
            out_shape=jax.ShapeDtypeStruct((T, D), F32),
            grid=(steps,),
            in_specs=in_specs,
            out_specs=pl.BlockSpec((tt, D), tokens),
            input_output_aliases=aliases,
            compiler_params=_cparams("parallel"),
            name="moe_combine",
        )(*args)
    return out


def _pick(n, pref):
    t = min(n, pref)
    while n % t:
        t //= 2
    return t


def _moe_layer(x2d, packed, w_router, router_bias, w_gate, w_up, w_down, layer, ws_gate, ws_up,
               ws_down, g, b, alpha):
    T, D = x2d.shape
    E = w_router.shape[1]
    idx, gates, rank, counts = _router(x2d, w_router, router_bias, _pick(T, 512))
    counts = counts[:, 0].astype(jnp.int32)
    padded = (counts + EXPERT_ROWS - 1) // EXPERT_ROWS * EXPERT_ROWS
    seg_end = jnp.cumsum(padded)
    seg_start = seg_end - padded
    n_blocks = (T * TOP_K + E * (EXPERT_ROWS - 1)) // EXPERT_ROWS
    dest = _slots(idx, rank, seg_start, _pick(T, 1024))
    sched = _expert_schedule(seg_start, seg_end, counts, n_blocks)
    n_used = (seg_end[-1:] // EXPERT_ROWS).astype(jnp.int32)
    xs = _sc_scatter_rows(packed, dest, n_blocks * EXPERT_ROWS)
    ys = _expert_mlp(xs, sched, n_used, w_gate, w_up, w_down, layer)
    return _combine(x2d, ys, dest, gates, ws_gate, ws_up, ws_down, g, b, alpha, _pick(T, 256))


def _mixer_layer(x2d, i, B, S, p, alpha):
    T, D = x2d.shape
    tm = _pick(T, 512)
    tn = _pick(D, 1024)
    j = i // 2
    w_in = p["moba_w_in"][j] if i % 2 == 0 else p["fox_w_in"][j]
    w_qv_t = jnp.concatenate([w_in[:, :D].T * (HEAD_DIM ** -0.5 * LOG2E),
                              w_in[:, 2 * D:3 * D].T]).astype(BF16)
    qv_t = _project_t(x2d, w_qv_t, tm, tn)
    w_k = w_in[:, D:2 * D].astype(BF16)
    if i % 2 == 0:
        k_aug, k_mean = _moba_keys(x2d, w_k, B, S, tm)
        attn_t = _moba_attention(qv_t, k_aug, k_mean, p["rel_bias"], B, S)
        w_out = p["moba_w_out"][j]
    else:
        c_tok = _fox_gates(x2d.reshape(B, S, D), w_in[:, 3 * D:], p["fox_b_f"][j], _pick(S, 512))
        k_aug = _fox_keys(x2d, w_k, c_tok.reshape(T, LANES), tm).reshape(HEADS_PER_TILE, B, S, D)
        attn_t = _fox_attention(qv_t, k_aug, B, S, _pick(S, 256))
        w_out = p["fox_w_out"][j]
    return _outproj_ln(attn_t, w_out.astype(BF16), x2d, p["ln1_g"][i], p["ln1_b"][i], alpha, tm)


def kernel(x, rel_bias, moba_w_in, moba_w_out, fox_w_in, fox_b_f, fox_w_out, ln1_g, ln1_b, ln2_g,
           ln2_b, w_router, router_bias, w_gate, w_up, w_down, ws_gate, ws_up, ws_down):
    B, S, D = x.shape
    depth = ln1_g.shape[0]
    alpha = (2 * depth) ** 0.25
    p = dict(rel_bias=rel_bias, moba_w_in=moba_w_in, moba_w_out=moba_w_out, fox_w_in=fox_w_in,
             fox_b_f=fox_b_f, fox_w_out=fox_w_out, ln1_g=ln1_g, ln1_b=ln1_b)
    x2d = x.reshape(B * S, D)
    for i in range(depth):
        x2d, packed = _mixer_layer(x2d, i, B, S, p, alpha)
        x2d = _moe_layer(x2d, packed, w_router[i], router_bias[i], w_gate, w_up, w_down, i,
                         ws_gate[i], ws_up[i], ws_down[i], ln2_g[i], ln2_b[i], alpha)
    return x2d.reshape(B, S, D)
```

```python
import functools
import math

import jax
import jax.numpy as jnp
from jax import lax
from jax.experimental import pallas as pl
from jax.experimental.pallas import tpu as pltpu
from jax.experimental.pallas import tpu_sc as plsc

F32 = jnp.float32
BF16 = jnp.bfloat16
HIGHEST = lax.Precision.HIGHEST

HEAD_DIM = 64
MOBA_BLOCK = 256
MOBA_TOPK = 3
REL_MAX_DIST = 128
TOP_K = 8
N_GROUPS = 8
TOPK_GROUPS = 4
ROUTED_SCALE = 2.5
LN_EPS = 1e-5
NEG = -1e30
LOG2E = math.log2(math.e)

LANES = 128
HEADS_PER_TILE = LANES // HEAD_DIM
EXPERT_ROWS = 512
VMEM_LIMIT = 48 * 1024 * 1024


def _cparams(*sem):
    return pltpu.CompilerParams(dimension_semantics=sem, vmem_limit_bytes=VMEM_LIMIT)


def _nt_dot(a, b, **kw):
    return lax.dot_general(a, b, (((1,), (1,)), ((), ())), preferred_element_type=F32, **kw)


def _split_bf16(x, pieces):
    out = []
    for _ in range(pieces):
        out.append(x.astype(BF16))
        x = x - out[-1].astype(F32)
    return out


def _three_pass(x, w_ref, mul):
    hi, lo = _split_bf16(x, 2)
    return mul(hi, w_ref[0]) + mul(lo, w_ref[0]) + mul(hi, w_ref[1])


def _proj_t_kernel(x_ref, wt_ref, o_ref):
    o_ref[...] = _nt_dot(wt_ref[...], x_ref[...].astype(BF16)).astype(o_ref.dtype)


def _project_t(x2d, wt, tm, tn):
    T, K = x2d.shape
    N = wt.shape[0]
    return pl.pallas_call(
        _proj_t_kernel,
        out_shape=jax.ShapeDtypeStruct((N, T), BF16),
        grid=(T // tm, N // tn),
        in_specs=[pl.BlockSpec((tm, K), lambda i, j: (i, 0)),
                  pl.BlockSpec((tn, K), lambda i, j: (j, 0))],
        out_specs=pl.BlockSpec((tn, tm), lambda i, j: (j, i)),
        compiler_params=_cparams("parallel", "arbitrary"),
        name="qv_proj_t",
    )(x2d, wt)


def _layer_norm(r, g, b):
    mu = jnp.mean(r, axis=-1, keepdims=True)
    d = r - mu
    var = jnp.mean(d * d, axis=-1, keepdims=True)
    return d * lax.rsqrt(var + LN_EPS) * g + b


def _pack_halves(y):
    n = y.shape[1] // 2
    bits = lambda v: lax.bitcast_convert_type(v.astype(BF16).astype(F32), jnp.int32)
    left = bits(y[:, :n])
    return lax.shift_right_logical(left, jnp.full_like(left, 16)) | bits(y[:, n:])


def _unpack_halves(p):
    return (lax.bitcast_convert_type(p << 16, F32),
            lax.bitcast_convert_type(p & jnp.int32(-65536), F32))


def _outproj_ln_kernel(at_ref, w_ref, x_ref, g_ref, b_ref, o_ref, packed_ref, *, alpha):
    y = lax.dot_general(at_ref[...], w_ref[...], (((0,), (0,)), ((), ())),
                        preferred_element_type=F32)
    out = _layer_norm(alpha * x_ref[...] + y, g_ref[...], b_ref[...])
    o_ref[...] = out
    packed_ref[...] = _pack_halves(out)


def _outproj_ln(attn_t, w_out, x2d, g, b, alpha, tm):
    T, D = x2d.shape
    row = lambda i: (i, 0)
    fixed = lambda i: (0, 0)
    return pl.pallas_call(
        functools.partial(_outproj_ln_kernel, alpha=alpha),
        out_shape=(jax.ShapeDtypeStruct((T, D), F32), jax.ShapeDtypeStruct((T, D // 2), jnp.int32)),
        grid=(T // tm,),
        in_specs=[pl.BlockSpec((D, tm), lambda i: (0, i)), pl.BlockSpec((D, D), fixed),
                  pl.BlockSpec((tm, D), row), pl.BlockSpec((1, D), fixed),
                  pl.BlockSpec((1, D), fixed)],
        out_specs=(pl.BlockSpec((tm, D), row), pl.BlockSpec((tm, D // 2), row)),
        compiler_params=_cparams("parallel"),
        name="outproj_ln",
    )(attn_t, w_out, x2d, g.reshape(1, D), b.reshape(1, D))


def _spare(a, n=0):
    return ((a + 1) % HEADS_PER_TILE) * HEAD_DIM + n


def _scores(chains):
    return [[jnp.dot(keys(), query(), preferred_element_type=F32) for keys, query, _ in blocks]
            for blocks in chains]


def _absorb(carries, scores, chains, masks=None, adds=None):
    probs = []
    for c, ((m, _), parts) in enumerate(zip(carries, scores)):
        if adds is not None:
            parts = [s if add is None else s + add() for add, s in zip(adds[c], parts)]
        if masks is not None:
            parts = [s if keep is None else jnp.where(keep, s, NEG)
                     for keep, s in zip(masks[c], parts)]
        tops = [jnp.max(s, axis=0, keepdims=True) for s in parts]
        m_new = functools.reduce(jnp.maximum, tops, m)
        probs.append((m_new, jnp.exp2(m - m_new),
                      [jnp.exp2(s - m_new).astype(BF16) for s in parts]))
    out = []
    for (_, acc), (m_new, decay, ps), blocks in zip(carries, probs, chains):
        acc = decay * acc
        for (_, _, values), p in zip(blocks, ps):
            acc = acc + jnp.dot(values(), p, preferred_element_type=F32)
        out.append((m_new, acc))
    return tuple(out)


def _stash(chains, ref):
    for a, parts in enumerate(_scores(chains)):
        for b, s in enumerate(parts):
            ref[a, b] = s


def _fetch(chains, ref):
    return [[ref[a, b] for b in range(len(blocks))] for a, blocks in enumerate(chains)]


def _attend_masked_last(carries, n, group, stage_refs, masks, adds=None):
    _stash(group(0), stage_refs[0])

    def run(carries, start, count, last_is_masked):
        for q in range(count):
            g = start + q
            if q + 1 < count or not last_is_masked:
                _stash(group(g + 1), stage_refs[(q + 1) % 2])
            special = last_is_masked and q + 1 == count
            carries = _absorb(carries, _fetch(group(g), stage_refs[q % 2]), group(g),
                              masks if special else None, adds if special else None)
        return carries

    trips = n // TRIP_GROUPS
    carries = lax.fori_loop(0, trips, lambda t, c: run(c, TRIP_GROUPS * t, TRIP_GROUPS, False),
                            carries)
    tails = [functools.partial(run, start=n - r, count=r + 1, last_is_masked=True)
             for r in range(TRIP_GROUPS)]
    return lax.switch(n % TRIP_GROUPS, tails, carries)


GROUP_BLOCKS = 2
TRIP_GROUPS = 4


def _stage_scratch(tq, chains=HEADS_PER_TILE):
    return [pltpu.VMEM((chains, GROUP_BLOCKS, tq, tq), F32) for _ in range(2)]


def _scores_init(tq):
    return jnp.full((1, tq), -jnp.inf, F32), jnp.zeros((LANES, tq), F32)


def _finish(carries, o_ref, cols=slice(None)):
    row = lax.broadcasted_iota(jnp.int32, carries[0][1].shape, 0)
    outs = []
    for a, (_, acc) in enumerate(carries):
        s = _spare(a)
        outs.append(acc / acc[s:s + 1, :])
    o_ref[:, cols] = jnp.where(row < HEAD_DIM, outs[0], outs[1]).astype(o_ref.dtype)


def _fill_v_aug(vt_ref, vaug_ref, tk):
    n_tiles = vt_ref.shape[1] // tk
    row = lax.broadcasted_iota(jnp.int32, (LANES, tk), 0)
    for n in range(n_tiles):
        v = vt_ref[:, n * tk:(n + 1) * tk]
        for a in range(HEADS_PER_TILE):
            vaug_ref[a, n] = jnp.where(row // HEAD_DIM == a, v, jnp.ones_like(v))


FOX_SPLIT = 3


def _fox_kernel(qt_ref, k_ref, vt_ref, o_ref, vaug_ref, stage0_ref, stage1_ref, *, tq):
    n = pl.program_id(2)

    @pl.when(n == 0)
    def _():
        _fill_v_aug(vt_ref, vaug_ref, tq)

    chains = [(h, a) for h in range(GROUP_BLOCKS) for a in range(HEADS_PER_TILE)]
    row = lax.broadcasted_iota(jnp.int32, (LANES, tq), 0)
    queries = {}
    for h, a in chains:
        offs = (row >= _spare(a)) & (row < _spare(a, FOX_SPLIT))
        queries[h, a] = jnp.where(row // HEAD_DIM == a, qt_ref[:, h * tq:(h + 1) * tq],
                                  jnp.where(offs, -1.0, 0.0).astype(BF16))

    def block(h, a, j):
        start = pl.multiple_of(j * tq, tq)
        return (lambda: k_ref[a, 0, pl.ds(start, tq), :], lambda: queries[h, a],
                lambda: vaug_ref[a, j])

    key = lax.broadcasted_iota(jnp.int32, (tq, tq), 0)
    qry = lax.broadcasted_iota(jnp.int32, (tq, tq), 1)
    masks = [[None if b < h else key + (b - h) * tq <= qry for b in range(GROUP_BLOCKS)]
             for h, _ in chains]
    carries = _attend_masked_last(
        tuple(_scores_init(tq) for _ in chains), n,
        lambda j: [[block(h, a, 2 * j), block(h, a, 2 * j + 1)] for h, a in chains],
        (stage0_ref, stage1_ref), masks)
    for h in range(GROUP_BLOCKS):
        _finish(carries[h * HEADS_PER_TILE:(h + 1) * HEADS_PER_TILE], o_ref,
                slice(h * tq, (h + 1) * tq))


def _fox_attention(qv_t, k_aug, B, S, tq):
    D = k_aug.shape[-1]
    n_tiles = D // LANES
    nq = S // tq
    span = GROUP_BLOCKS * tq
    steps = S // span
    return pl.pallas_call(
        functools.partial(_fox_kernel, tq=tq),
        out_shape=jax.ShapeDtypeStruct((D, B * S), BF16),
        grid=(B, n_tiles, steps),
        in_specs=[pl.BlockSpec((LANES, span), lambda b, h, i: (h, b * steps + i)),
                  pl.BlockSpec((HEADS_PER_TILE, 1, S, LANES), lambda b, h, i: (0, b, 0, h)),
                  pl.BlockSpec((LANES, S), lambda b, h, i: (n_tiles + h, b))],
        out_specs=pl.BlockSpec((LANES, span), lambda b, h, i: (h, b * steps + i)),
        scratch_shapes=[pltpu.VMEM((HEADS_PER_TILE, nq, LANES, tq), BF16)]
        + _stage_scratch(tq, GROUP_BLOCKS * HEADS_PER_TILE),
        compiler_params=_cparams("parallel", "parallel", "arbitrary"),
        name="fox_attention",
    )(qv_t, k_aug, qv_t)


def _log_sigmoid(z):
    return jnp.minimum(z, 0.0) - jnp.log1p(jnp.exp(-jnp.abs(z)))


def _fox_gates_kernel(x_ref, w_ref, b_ref, o_ref, carry_ref, *, ts):
    @pl.when(pl.program_id(1) == 0)
    def _():
        carry_ref[...] = jnp.zeros_like(carry_ref)

    z = _three_pass(x_ref[0], w_ref, lambda x, w: jnp.dot(x, w, preferred_element_type=F32))
    lf = _log_sigmoid(z + b_ref[...])
    r = lax.broadcasted_iota(jnp.int32, (ts, ts), 0)
    c = lax.broadcasted_iota(jnp.int32, (ts, ts), 1)
    tri = (c <= r).astype(BF16)
    cum = carry_ref[...]
    for piece in _split_bf16(lf, 3):
        cum = cum + jnp.dot(tri, piece, preferred_element_type=F32)
    carry_ref[...] = cum[ts - 1:ts, :]
    o_ref[0] = cum


def _fox_gates(x3d, w_f, b_f, ts):
    B, S, D = x3d.shape
    H = w_f.shape[1]
    w_pad = jnp.stack(_split_bf16(jnp.zeros((D, LANES), F32).at[:, :H].set(w_f), 2))
    b_pad = jnp.zeros((1, LANES), F32).at[0, :H].set(b_f)
    return pl.pallas_call(
        functools.partial(_fox_gates_kernel, ts=ts),
        out_shape=jax.ShapeDtypeStruct((B, S, LANES), F32),
        grid=(B, S // ts),
        in_specs=[pl.BlockSpec((1, ts, D), lambda b, s: (b, s, 0)),
                  pl.BlockSpec((2, D, LANES), lambda b, s: (0, 0, 0)),
                  pl.BlockSpec((1, LANES), lambda b, s: (0, 0))],
        out_specs=pl.BlockSpec((1, ts, LANES), lambda b, s: (b, s, 0)),
        scratch_shapes=[pltpu.VMEM((1, LANES), F32)],
        compiler_params=_cparams("parallel", "arbitrary"),
        name="fox_gates",
    )(x3d, w_pad, b_pad)


def _fox_keys_kernel(x_ref, w_ref, c_ref, o_ref):
    keys = jnp.dot(x_ref[...].astype(BF16), w_ref[...], preferred_element_type=F32)
    tm, D = keys.shape
    lane = lax.broadcasted_iota(jnp.int32, (tm, LANES), 1)
    for t in range(D // LANES):
        k = keys[:, t * LANES:(t + 1) * LANES]
        for a in range(HEADS_PER_TILE):
            h = t * HEADS_PER_TILE + a
            rest = c_ref[:, h:h + 1] * LOG2E
            aug = jnp.zeros((tm, LANES), F32)
            for n in range(FOX_SPLIT):
                piece = rest.astype(BF16).astype(F32)
                rest = rest - piece
                aug = jnp.where(lane == _spare(a, n), piece, aug)
            o_ref[a, :, t * LANES:(t + 1) * LANES] = jnp.where(lane // HEAD_DIM == a, k,
                                                               aug).astype(BF16)


def _fox_keys(x2d, w_k, c_tok, tm):
    T, K = x2d.shape
    D = w_k.shape[1]
    return pl.pallas_call(
        _fox_keys_kernel,
        out_shape=jax.ShapeDtypeStruct((HEADS_PER_TILE, T, D), BF16),
        grid=(T // tm,),
        in_specs=[pl.BlockSpec((tm, K), lambda i: (i, 0)),
                  pl.BlockSpec((K, D), lambda i: (0, 0)),
                  pl.BlockSpec((tm, LANES), lambda i: (i, 0))],
        out_specs=pl.BlockSpec((HEADS_PER_TILE, tm, D), lambda i: (0, i, 0)),
        compiler_params=_cparams("parallel"),
        name="fox_keys",
    )(x2d, w_k, c_tok)


MOBA_SLAB = 16
KMEAN_SPLIT = 3


def _moba_kernel(bfar_ref, qt_ref, k_ref, vt_ref, kmean_ref, bown_ref, badj_ref, o_ref,
                 vaug_ref, sel_ref, stage0_ref, stage1_ref, *, blk):
    t = pl.program_id(1)
    n = pl.program_id(2)
    nb = kmean_ref.shape[2]

    @pl.when(n == 0)
    def _():
        _fill_v_aug(vt_ref, vaug_ref, blk)

    chains = [(h, a) for h in range(GROUP_BLOCKS) for a in range(HEADS_PER_TILE)]
    row = lax.broadcasted_iota(jnp.int32, (LANES, blk), 0)
    brow = lax.broadcasted_iota(jnp.int32, (nb, blk), 0)
    base = {}
    for c, (h, a) in enumerate(chains):
        own = GROUP_BLOCKS * n + h
        qt = qt_ref[:, h * blk:(h + 1) * blk]
        qa = jnp.where(row // HEAD_DIM == a, qt, jnp.zeros_like(qt))
        base[h, a] = qa
        bscore = sum(jnp.dot(kmean_ref[piece, 0], qa, preferred_element_type=F32)
                     for piece in range(KMEAN_SPLIT))
        cand = jnp.where(brow < own, bscore, NEG)
        sel = jnp.zeros((nb, blk), F32)
        for _ in range(MOBA_TOPK):
            mx = jnp.max(cand, axis=0, keepdims=True)
            first = jnp.min(jnp.where(cand == mx, brow, nb), axis=0, keepdims=True)
            hit = brow == first
            sel = jnp.where(hit, 1.0, sel)
            cand = jnp.where(hit, -jnp.inf, cand)
        sel_ref[c] = jnp.where(brow < own, sel, 0.0)

    srow = lax.broadcasted_iota(jnp.int32, (MOBA_SLAB, blk), 0)

    def block(c, j):
        h, a = chains[c]
        head_id = t * HEADS_PER_TILE + a
        jc = jnp.maximum(j, 0)
        start = pl.multiple_of(jc * blk, blk)

        def query():
            s0 = _spare(a) // MOBA_SLAB * MOBA_SLAB
            chosen = (((sel_ref[c, pl.ds(jc, 1), :] > 0.0) & (j >= 0))
                      | (j == GROUP_BLOCKS * n + h))
            slab = jnp.where(srow == _spare(a) - s0,
                             jnp.where(chosen, bfar_ref[0, head_id], NEG),
                             jnp.where(srow == _spare(a, 1) - s0, bfar_ref[1, head_id], 0.0)
                             ).astype(BF16)
            q = base[h, a]
            head = [q[:s0]] if s0 else []
            tail = [q[s0 + MOBA_SLAB:]] if s0 + MOBA_SLAB < LANES else []
            return jnp.concatenate(head + [slab] + tail, axis=0)

        return lambda: k_ref[a, 0, pl.ds(start, blk), :], query, lambda: vaug_ref[a, jc]

    key = lax.broadcasted_iota(jnp.int32, (blk, blk), 0)
    qry = lax.broadcasted_iota(jnp.int32, (blk, blk), 1)
    carries = _attend_masked_last(
        tuple(_scores_init(blk) for _ in chains), n,
        lambda g: [[block(c, 2 * g + h - 1), block(c, 2 * g + h)] for c, (h, _) in enumerate(chains)],
        (stage0_ref, stage1_ref), [[None, key <= qry] for _ in chains],
        [[lambda a=a: badj_ref[a], lambda a=a: bown_ref[a]] for _, a in chains])
    for h in range(GROUP_BLOCKS):
        _finish(carries[h * HEADS_PER_TILE:(h + 1) * HEADS_PER_TILE], o_ref,
                slice(h * blk, (h + 1) * blk))


def _rel_bucket(dist, n_buckets):
    n = jnp.maximum(dist, 0)
    max_exact = n_buckets // 2
    nf = jnp.maximum(n, 1).astype(F32)
    large = max_exact + (jnp.log(nf / max_exact) / math.log(REL_MAX_DIST / max_exact)
                         * (n_buckets - max_exact)).astype(jnp.int32)
    large = jnp.minimum(large, n_buckets - 1)
    return jnp.where(n < max_exact, n, large)


def _moba_attention(qv_t, k_aug, k_mean, rel_bias, B, S):
    D = k_aug.shape[-1]
    n_tiles = D // LANES
    blk = MOBA_BLOCK
    nb = S // blk
    hp = HEADS_PER_TILE
    span = GROUP_BLOCKS * blk
    steps = S // span
    n_chains = GROUP_BLOCKS * hp
    r = jnp.arange(blk)
    delta = r[None, :] - r[:, None]
    bias_t = rel_bias.T.astype(F32)
    n_buckets = rel_bias.shape[0]
    def table(dist):
        onehot = jax.nn.one_hot(_rel_bucket(dist, n_buckets), n_buckets, dtype=F32)
        return jnp.einsum('crn,hn->hcr', onehot, bias_t, precision=HIGHEST)

    b_far = bias_t[:, n_buckets - 1] * LOG2E
    b_own = table(delta) * LOG2E - b_far[:, None, None]
    b_adj = table(delta + blk) * LOG2E - b_far[:, None, None]
    far_hi = b_far.astype(BF16).astype(F32)
    far_lo = (b_far - far_hi).astype(BF16).astype(F32)
    b_far2 = jnp.stack([far_hi, far_lo])
    return pl.pallas_call(
        functools.partial(_moba_kernel, blk=blk),
        out_shape=jax.ShapeDtypeStruct((D, B * S), BF16),
        grid_spec=pltpu.PrefetchScalarGridSpec(
            num_scalar_prefetch=1,
            grid=(B, n_tiles, steps),
            in_specs=[pl.BlockSpec((LANES, span), lambda b, h, i, f: (h, b * steps + i)),
                      pl.BlockSpec((hp, 1, S, LANES), lambda b, h, i, f: (0, b, 0, h)),
                      pl.BlockSpec((LANES, S), lambda b, h, i, f: (n_tiles + h, b)),
                      pl.BlockSpec((KMEAN_SPLIT, 1, nb, LANES), lambda b, h, i, f: (0, b, 0, h)),
                      pl.BlockSpec((hp, blk, blk), lambda b, h, i, f: (h, 0, 0)),
                      pl.BlockSpec((hp, blk, blk), lambda b, h, i, f: (h, 0, 0))],
            out_specs=pl.BlockSpec((LANES, span), lambda b, h, i, f: (h, b * steps + i)),
            scratch_shapes=[pltpu.VMEM((hp, nb, LANES, blk), BF16),
                            pltpu.VMEM((n_chains, nb, blk), F32)] + _stage_scratch(blk, n_chains)),
        compiler_params=_cparams("parallel", "parallel", "arbitrary"),
        name="moba_attention",
    )(b_far2, qv_t, k_aug, qv_t, k_mean, b_own, b_adj)


def _moba_keys_kernel(x_ref, w_ref, o_ref, mean_ref):
    k = jnp.dot(x_ref[...].astype(BF16), w_ref[...], preferred_element_type=F32)
    col = lax.broadcasted_iota(jnp.int32, k.shape, 1)
    lane = col % LANES
    for a in range(HEADS_PER_TILE):
        ones = (lane == _spare(a)) | (lane == _spare(a, 1))
        o_ref[a] = jnp.where((col // HEAD_DIM) % HEADS_PER_TILE == a, k,
                             jnp.where(ones, 1.0, 0.0)).astype(BF16)
    for n in range(mean_ref.shape[0]):
        mean_ref[n] = jnp.mean(k[n * MOBA_BLOCK:(n + 1) * MOBA_BLOCK], axis=0, keepdims=True)


def _moba_keys(x2d, w_k, B, S, tm):
    T, K = x2d.shape
    D = w_k.shape[1]
    nb = S // MOBA_BLOCK
    per_tile = tm // MOBA_BLOCK
    k_aug, k_mean = pl.pallas_call(
        _moba_keys_kernel,
        out_shape=(jax.ShapeDtypeStruct((HEADS_PER_TILE, T, D), BF16),
                   jax.ShapeDtypeStruct((B * nb, 1, D), F32)),
        grid=(T // tm,),
        in_specs=[pl.BlockSpec((tm, K), lambda i: (i, 0)),
                  pl.BlockSpec((K, D), lambda i: (0, 0))],
        out_specs=(pl.BlockSpec((HEADS_PER_TILE, tm, D), lambda i: (0, i, 0)),
                   pl.BlockSpec((per_tile, 1, D), lambda i: (i, 0, 0))),
        compiler_params=_cparams("parallel"),
        name="moba_keys",
    )(x2d, w_k)
    k_mean = jnp.stack(_split_bf16(k_mean.reshape(B, nb, D), KMEAN_SPLIT))
    return k_aug.reshape(HEADS_PER_TILE, B, S, D), k_mean


def _router_kernel(x_ref, wt_ref, bias_ref, idx_ref, gate_ref, rank_ref, cnt_ref, run_ref, *, tm):
    @pl.when(pl.program_id(0) == 0)
    def _():
        run_ref[...] = jnp.zeros_like(run_ref)

    E = wt_ref.shape[1]
    gsz = E // N_GROUPS
    logits = _three_pass(x_ref[...], wt_ref, lambda x, w: _nt_dot(w, x))
    s = jax.nn.sigmoid(logits)
    sb = s + bias_ref[...]
    neg_inf = -jnp.inf

    giota = lax.broadcasted_iota(jnp.int32, (gsz, tm), 0)
    gscore = []
    for g in range(N_GROUPS):
        blk = sb[g * gsz:(g + 1) * gsz, :]
        m1 = jnp.max(blk, axis=0, keepdims=True)
        i1 = jnp.min(jnp.where(blk == m1, giota, gsz), axis=0, keepdims=True)
        m2 = jnp.max(jnp.where(giota == i1, neg_inf, blk), axis=0, keepdims=True)
        gscore.append(m1 + m2)

    gsel = [jnp.zeros((1, tm), jnp.bool_) for _ in range(N_GROUPS)]
    for _ in range(TOPK_GROUPS):
        mx = functools.reduce(jnp.maximum, gscore)
        found = jnp.zeros((1, tm), jnp.bool_)
        for g in range(N_GROUPS):
            hit = (gscore[g] == mx) & jnp.logical_not(found)
            gsel[g] = gsel[g] | hit
            found = found | hit
            gscore[g] = jnp.where(hit, neg_inf, gscore[g])
    emask = jnp.concatenate([jnp.broadcast_to(gsel[g], (gsz, tm)) for g in range(N_GROUPS)], axis=0)
    cand = jnp.where(emask, sb, NEG)

    eiota = lax.broadcasted_iota(jnp.int32, (E, tm), 0)
    hits, idxs, ws = [], [], []
    for _ in range(TOP_K):
        mx = jnp.max(cand, axis=0, keepdims=True)
        first = jnp.min(jnp.where(cand == mx, eiota, E), axis=0, keepdims=True)
        hit = eiota == first
        hits.append(hit)
        idxs.append(first)
        ws.append(jnp.sum(jnp.where(hit, s, 0.0), axis=0, keepdims=True))
        cand = jnp.where(hit, neg_inf, cand)
    wsum = functools.reduce(jnp.add, ws)

    chosen = functools.reduce(jnp.logical_or, hits)
    onehot = jnp.where(chosen, 1.0, 0.0)
    tr = lax.broadcasted_iota(jnp.int32, (tm, tm), 0)
    tc = lax.broadcasted_iota(jnp.int32, (tm, tm), 1)
    before = (tr < tc).astype(BF16)
    prior = jnp.dot(onehot.astype(BF16), before, preferred_element_type=F32) + run_ref[...]
    grow = lax.broadcasted_iota(jnp.int32, (LANES, tm), 0)
    gates = jnp.zeros((LANES, tm), F32)
    for k in range(TOP_K):
        idx_ref[k:k + 1, :] = idxs[k]
        gates = jnp.where(grow == k, ws[k] / wsum * ROUTED_SCALE, gates)
        rank_ref[k:k + 1, :] = jnp.sum(jnp.where(hits[k], prior, 0.0), axis=0,
                                       keepdims=True).astype(jnp.int32)
    gate_ref[...] = gates.T
    run_ref[...] = run_ref[...] + jnp.sum(onehot, axis=1, keepdims=True)
    cnt_ref[...] = run_ref[...]


def _router(x2d, w_router, router_bias, tm):
    T, D = x2d.shape
    E = w_router.shape[1]
    tok = lambda i: (0, i)
    fixed = lambda i: (0, 0)
    return pl.pallas_call(
        functools.partial(_router_kernel, tm=tm),
        out_shape=(jax.ShapeDtypeStruct((TOP_K, T), jnp.int32),
                   jax.ShapeDtypeStruct((T, LANES), F32),
                   jax.ShapeDtypeStruct((TOP_K, T), jnp.int32),
                   jax.ShapeDtypeStruct((E, 1), F32)),
        grid=(T // tm,),
        in_specs=[pl.BlockSpec((tm, D), lambda i: (i, 0)),
                  pl.BlockSpec((2, E, D), lambda i: (0, 0, 0)),
                  pl.BlockSpec((E, 1), fixed)],
        out_specs=(pl.BlockSpec((TOP_K, tm), tok), pl.BlockSpec((tm, LANES), lambda i: (i, 0)),
                   pl.BlockSpec((TOP_K, tm), tok), pl.BlockSpec((E, 1), fixed)),
        scratch_shapes=[pltpu.VMEM((E, 1), F32)],
        compiler_params=_cparams("arbitrary"),
        name="router",
    )(x2d, jnp.stack(_split_bf16(w_router.T, 2)), router_bias.reshape(E, 1).astype(F32))


def _slots_kernel(idx_ref, rank_ref, start_ref, o_ref):
    E = start_ref.shape[0]
    tm = idx_ref.shape[1]
    eiota = lax.broadcasted_iota(jnp.int32, (E, tm), 0)
    start = start_ref[...]
    for k in range(TOP_K):
        base = jnp.sum(jnp.where(eiota == idx_ref[k:k + 1, :], start, 0.0), axis=0, keepdims=True)
        o_ref[k:k + 1, :] = base.astype(jnp.int32) + rank_ref[k:k + 1, :]


def _slots(idx, rank, seg_start, tm):
    T = idx.shape[1]
    E = seg_start.shape[0]
    tok = lambda i: (0, i)
    return pl.pallas_call(
        _slots_kernel,
        out_shape=jax.ShapeDtypeStruct((TOP_K, T), jnp.int32),
        grid=(T // tm,),
        in_specs=[pl.BlockSpec((TOP_K, tm), tok), pl.BlockSpec((TOP_K, tm), tok),
                  pl.BlockSpec((E, 1), lambda i: (0, 0))],
        out_specs=pl.BlockSpec((TOP_K, tm), tok),
        compiler_params=_cparams("parallel"),
        name="moe_slots",
    )(idx, rank, seg_start.astype(F32).reshape(E, 1))


SC_CORES = 2
SC_SUBCORES = 16
SC_CHUNK = 128


def _sc_rows_kernel(body, out_shape, dtype, scratch, name):
    mesh = plsc.VectorSubcoreMesh(core_axis_name="c", subcore_axis_name="s",
                                  num_cores=SC_CORES, num_subcores=SC_SUBCORES)
    return pl.kernel(body, mesh=mesh, out_type=jax.ShapeDtypeStruct(out_shape, dtype),
                     scratch_types=scratch, name=name)


def _sc_chunks(n):
    workers = SC_CORES * SC_SUBCORES
    chunks = n // (workers * SC_CHUNK)
    assert chunks * workers * SC_CHUNK == n
    return workers, chunks


SC_HALF = SC_CHUNK // 2


def _sc_gather_rows(table, idx):
    workers, chunks = _sc_chunks(idx.shape[0])
    width = table.shape[1]

    def body(table_hbm, idx_hbm, out_hbm, idx_v, rows_v, *sems):
        wid = lax.axis_index("s") * SC_CORES + lax.axis_index("c")
        pltpu.sync_copy(idx_hbm.at[wid], idx_v)

        @pl.loop(0, chunks)
        def _(j):
            pieces = [2 * j, 2 * j + 1]
            reads = [pltpu.async_copy(table_hbm.at[idx_v.at[p]], rows_v.at[b], sems[b])
                     for b, p in enumerate(pieces)]
            writes = []
            for b, p in enumerate(pieces):
                reads[b].wait()
                first = (wid * 2 * chunks + p) * SC_HALF
                writes.append(pltpu.async_copy(rows_v.at[b], out_hbm.at[pl.ds(first, SC_HALF)],
                                               sems[2 + b]))
            for write in writes:
                write.wait()

    scratch = [pltpu.VMEM((2 * chunks, SC_HALF), jnp.int32),
               pltpu.VMEM((2, SC_HALF, width), table.dtype)] + [pltpu.SemaphoreType.DMA] * 4
    call = _sc_rows_kernel(body, (idx.shape[0], width), table.dtype, scratch, "moe_gather_sc")
    return call(table, idx.reshape(workers, 2 * chunks, SC_HALF))


def _sc_scatter_rows(src, dest, n_rows):
    copies, n_src = dest.shape
    workers, chunks = _sc_chunks(n_src)

    def body(src_hbm, idx_hbm, out_hbm, idx_v, rows_v, sem):
        wid = lax.axis_index("s") * SC_CORES + lax.axis_index("c")
        pltpu.sync_copy(idx_hbm.at[wid], idx_v)

        @pl.loop(0, chunks)
        def _(j):
            pltpu.sync_copy(src_hbm.at[pl.ds((wid * chunks + j) * SC_CHUNK, SC_CHUNK)], rows_v)
            started = [pltpu.async_copy(rows_v, out_hbm.at[idx_v.at[j, k]], sem)
                       for k in range(copies)]
            for copy in started:
                copy.wait()

    scratch = [pltpu.VMEM((chunks, copies, SC_CHUNK), jnp.int32),
               pltpu.VMEM((SC_CHUNK, src.shape[1]), src.dtype), pltpu.SemaphoreType.DMA]
    call = _sc_rows_kernel(body, (n_rows, src.shape[1]), src.dtype, scratch, "moe_scatter_sc")
    idx = dest.reshape(copies, workers, chunks, SC_CHUNK).transpose(1, 2, 0, 3)
    return call(src, idx)


def _silu(g):
    return g * jax.nn.sigmoid(g)


SCHED_FIRST, SCHED_SLOT, SCHED_NEXT, SCHED_VALID = 0, 1, 2, 3


def _expert_kernel(sched_ref, n_used_ref, x_ref, wg_hbm, wu_hbm, wd_hbm, o_ref,
                   wg_buf, wu_buf, wd_buf, wg_bf, wu_bf, wd_bf, sem, *, layer, first_expert_row):
    i = pl.program_id(0)
    slot = sched_ref[SCHED_SLOT, i]

    def copies(expert, slot):
        pairs = ((wg_hbm, wg_buf), (wu_hbm, wu_buf), (wd_hbm, wd_buf))
        return [pltpu.make_async_copy(src.at[layer, expert], dst.at[slot], sem.at[slot, n])
                for n, (src, dst) in enumerate(pairs)]

    @pl.when(i == 0)
    def _():
        for c in copies(sched_ref[first_expert_row, 0], 0):
            c.start()

    @pl.when(sched_ref[SCHED_FIRST, i] == 1)
    def _():
        for c in copies(0, slot):
            c.wait()

        @pl.when(sched_ref[SCHED_NEXT, i] >= 0)
        def _():
            for c in copies(sched_ref[SCHED_NEXT, i], 1 - slot):
                c.start()

        wg_bf[...] = wg_buf[slot].astype(BF16)
        wu_bf[...] = wu_buf[slot].astype(BF16)
        wd_bf[...] = wd_buf[slot].astype(BF16)

    @pl.when(i < n_used_ref[0])
    def _():
        words = x_ref[...]
        row = lax.broadcasted_iota(jnp.int32, words.shape, 0)
        words = jnp.where(row < sched_ref[SCHED_VALID, i], words, 0)
        x = jnp.concatenate(_unpack_halves(words), axis=1).astype(BF16)
        g = jnp.dot(x, wg_bf[...], preferred_element_type=F32)
        u = jnp.dot(x, wu_bf[...], preferred_element_type=F32)
        a = (_silu(g) * u).astype(BF16)
        o_ref[...] = _pack_halves(jnp.dot(a, wd_bf[...], preferred_element_type=F32))

    @pl.when(i >= n_used_ref[0])
    def _():
        o_ref[...] = jnp.zeros_like(o_ref)


def _expert_schedule(seg_start, seg_end, counts, n_blocks):
    E = seg_end.shape[0]
    first_row = jnp.arange(n_blocks, dtype=jnp.int32) * EXPERT_ROWS
    blk_e = jnp.minimum(jnp.sum(seg_end[None, :] <= first_row[:, None], axis=1), E - 1)
    used = first_row < seg_end[-1]
    first = used & (first_row == seg_start[blk_e])
    ordinal = jnp.cumsum(first) - 1
    ids = jnp.where(seg_end > seg_start, jnp.arange(E), E)
    later = lax.cummin(ids, reverse=True)
    nxt = jnp.concatenate([later[1:], jnp.full((1,), E, later.dtype)])[blk_e]
    nxt = jnp.where(nxt < E, nxt, -1)
    valid = jnp.clip((seg_start + counts)[blk_e] - first_row, 0, EXPERT_ROWS)
    return jnp.stack([first, ordinal % 2, nxt, valid, blk_e]).astype(jnp.int32)


def _expert_mlp(xs, sched, n_used, w_gate, w_up, w_down, layer):
    n_rows, half = xs.shape
    D = 2 * half
    F = w_gate.shape[-1]
    tm = EXPERT_ROWS
    hbm = pl.BlockSpec(memory_space=pl.ANY)
    return pl.pallas_call(
        functools.partial(_expert_kernel, layer=layer, first_expert_row=sched.shape[0] - 1),
        out_shape=jax.ShapeDtypeStruct((n_rows, half), jnp.int32),
        grid_spec=pltpu.PrefetchScalarGridSpec(
            num_scalar_prefetch=2,
            grid=(n_rows // tm,),
            in_specs=[pl.BlockSpec((tm, half), lambda i, sc, nu: (jnp.minimum(i, nu[0] - 1), 0)),
                      hbm, hbm, hbm],
            out_specs=pl.BlockSpec((tm, half), lambda i, sc, nu: (i, 0)),
            scratch_shapes=[pltpu.VMEM((2, D, F), F32), pltpu.VMEM((2, D, F), F32),
                            pltpu.VMEM((2, F, D), F32), pltpu.VMEM((D, F), BF16),
                            pltpu.VMEM((D, F), BF16), pltpu.VMEM((F, D), BF16),
                            pltpu.SemaphoreType.DMA((2, 3))]),
        compiler_params=_cparams("arbitrary"),
        name="moe_experts",
    )(sched, n_used, xs, w_gate, w_up, w_down)


def _combine_kernel(x_ref, gate_ref, rows_ref, wsg_ref, wsu_ref, wsd_ref, g_ref, b_ref, o_ref,
                    *, alpha):
    x = x_ref[...]
    xb = x.astype(BF16)
    g = jnp.dot(xb, wsg_ref[...], preferred_element_type=F32)
    u = jnp.dot(xb, wsu_ref[...], preferred_element_type=F32)
    y = jnp.dot((_silu(g) * u).astype(BF16), wsd_ref[...], preferred_element_type=F32)
    gates = gate_ref[...]
    half = y.shape[1] // 2
    left, right = y[:, :half], y[:, half:]
    for k in range(TOP_K):
        lo, hi = _unpack_halves(rows_ref[k])
        left = left + lo * gates[:, k:k + 1]
        right = right + hi * gates[:, k:k + 1]
    y = jnp.concatenate([left, right], axis=1)
    o_ref[...] = _layer_norm(alpha * x + y, g_ref[...], b_ref[...])


def _combine(x2d, ys, dest, gates_tok, ws_gate, ws_up, ws_down, g, b, alpha, tt):
    T, D = x2d.shape
    F = ws_gate.shape[1]
    row = lambda i: (i, 0)
    fixed = lambda i: (0, 0)
    rows = _sc_gather_rows(ys, dest.reshape(-1)).reshape(TOP_K, T, D // 2)
    return pl.pallas_call(
        functools.partial(_combine_kernel, alpha=alpha),
        out_shape=jax.ShapeDtypeStruct((T, D), F32),
        grid=(T // tt,),
        in_specs=[pl.BlockSpec((tt, D), row),
                  pl.BlockSpec((tt, LANES), row),
                  pl.BlockSpec((TOP_K, tt, D // 2), lambda i: (0, i, 0)),
                  pl.BlockSpec((D, F), fixed), pl.BlockSpec((D, F), fixed),
                  pl.BlockSpec((F, D), fixed),
                  pl.BlockSpec((1, D), fixed), pl.BlockSpec((1, D), fixed)],
        out_specs=pl.BlockSpec((tt, D), row),
        compiler_params=_cparams("parallel"),
        name="moe_combine",
    )(x2d, gates_tok, rows, ws_gate.astype(BF16), ws_up.astype(BF16), ws_down.astype(BF16),
      g.reshape(1, D), b.reshape(1, D))


def _pick(n, pref):
    t = min(n, pref)
    while n % t:
        t //= 2
    return t


def _moe_layer(x2d, packed, w_router, router_bias, w_gate, w_up, w_down, layer, ws_gate, ws_up,
               ws_down, g, b, alpha):
    T, D = x2d.shape
    E = w_router.shape[1]
    idx, gates, rank, counts = _router(x2d, w_router, router_bias, _pick(T, 512))
    counts = counts[:, 0].astype(jnp.int32)
    padded = (counts + EXPERT_ROWS - 1) // EXPERT_ROWS * EXPERT_ROWS
    seg_end = jnp.cumsum(padded)
    seg_start = seg_end - padded
    n_blocks = (T * TOP_K + E * (EXPERT_ROWS - 1)) // EXPERT_ROWS
    dest = _slots(idx, rank, seg_start, _pick(T, 1024))
    sched = _expert_schedule(seg_start, seg_end, counts, n_blocks)
    n_used = (seg_end[-1:] // EXPERT_ROWS).astype(jnp.int32)
    xs = _sc_scatter_rows(packed, dest, n_blocks * EXPERT_ROWS)
    ys = _expert_mlp(xs, sched, n_used, w_gate, w_up, w_down, layer)
    return _combine(x2d, ys, dest, gates, ws_gate, ws_up, ws_down, g, b, alpha, _pick(T, 256))


def _mixer_layer(x2d, i, B, S, p, alpha):
    T, D = x2d.shape
    tm = _pick(T, 512)
    tn = _pick(D, 1024)
    j = i // 2
    w_in = p["moba_w_in"][j] if i % 2 == 0 else p["fox_w_in"][j]
    w_qv_t = jnp.concatenate([w_in[:, :D].T * (HEAD_DIM ** -0.5 * LOG2E),
                              w_in[:, 2 * D:3 * D].T]).astype(BF16)
    qv_t = _project_t(x2d, w_qv_t, tm, tn)
    w_k = w_in[:, D:2 * D].astype(BF16)
    if i % 2 == 0:
        k_aug, k_mean = _moba_keys(x2d, w_k, B, S, tm)
        attn_t = _moba_attention(qv_t, k_aug, k_mean, p["rel_bias"], B, S)
        w_out = p["moba_w_out"][j]
    else:
        c_tok = _fox_gates(x2d.reshape(B, S, D), w_in[:, 3 * D:], p["fox_b_f"][j], _pick(S, 512))
        k_aug = _fox_keys(x2d, w_k, c_tok.reshape(T, LANES), tm).reshape(HEADS_PER_TILE, B, S, D)
        attn_t = _fox_attention(qv_t, k_aug, B, S, _pick(S, 256))
        w_out = p["fox_w_out"][j]
    return _outproj_ln(attn_t, w_out.astype(BF16), x2d, p["ln1_g"][i], p["ln1_b"][i], alpha, tm)


def kernel(x, rel_bias, moba_w_in, moba_w_out, fox_w_in, fox_b_f, fox_w_out, ln1_g, ln1_b, ln2_g,
           ln2_b, w_router, router_bias, w_gate, w_up, w_down, ws_gate, ws_up, ws_down):
    B, S, D = x.shape
    depth = ln1_g.shape[0]
    alpha = (2 * depth) ** 0.25
    p = dict(rel_bias=rel_bias, moba_w_in=moba_w_in, moba_w_out=moba_w_out, fox_w_in=fox_w_in,
             fox_b_f=fox_b_f, fox_w_out=fox_w_out, ln1_g=ln1_g, ln1_b=ln1_b)
    x2d = x.reshape(B * S, D)
    for i in range(depth):
        x2d, packed = _mixer_layer(x2d, i, B, S, p, alpha)
        x2d = _moe_layer(x2d, packed, w_router[i], router_bias[i], w_gate, w_up, w_down, i,
                         ws_gate[i], ws_up[i], ws_down[i], ln2_g[i], ln2_b[i], alpha)
    return x2d.reshape(B, S, D)
```
